```python
import jax, jax.numpy as jnp
from jax import lax
import numpy as np


D_MODEL = 1024
BATCH = 8
SEQ = 8192
DEPTH = 4

MLA_HEADS = 4
QK_NOPE_DIM = 64
QK_ROPE_DIM = 32
V_HEAD_DIM = 64
Q_RANK = D_MODEL // 4
KV_RANK = D_MODEL // 8
ROPE_BASE = 10000.0
Q_BLOCK = 128
SG_GROUPS = 4
SG_WIDTH = D_MODEL // 4
SG_CHUNK = 128
CONV_WIDTH = D_MODEL // 4
CONV_K = 3
POOL_WINDOWS = (2, 4, 8, 16)
POOL_WIDTH = D_MODEL // 4
POOL_GROUP = POOL_WIDTH // 4
N_BRANCH = 4
D_FF = 4 * D_MODEL
EPS = 1e-6
N_IN = Q_RANK + KV_RANK + QK_ROPE_DIM + 2 * SG_WIDTH + 3 * CONV_WIDTH + POOL_WIDTH + N_BRANCH * D_MODEL

kernel_name = 'hybrid_gated_mla_sgmlp_conv_pool_block'


def rmsnorm(x, g):
    xf = x.astype(jnp.float32)
    y = xf * lax.rsqrt(jnp.mean(xf * xf, axis=-1, keepdims=True) + EPS)
    return (y * g.astype(jnp.float32)).astype(x.dtype)


def layernorm(x, g, b):
    xf = x.astype(jnp.float32)
    mu = jnp.mean(xf, axis=-1, keepdims=True)
    xc = xf - mu
    y = xc * lax.rsqrt(jnp.mean(xc * xc, axis=-1, keepdims=True) + EPS)
    return (y * g.astype(jnp.float32) + b.astype(jnp.float32)).astype(x.dtype)


def split_cols(proj):
    sizes = (Q_RANK, KV_RANK, QK_ROPE_DIM, SG_WIDTH, SG_WIDTH, CONV_WIDTH, CONV_WIDTH, CONV_WIDTH,
             POOL_WIDTH, N_BRANCH * D_MODEL)
    offs = []
    acc = 0
    for s in sizes[:-1]:
        acc += s
        offs.append(acc)
    return jnp.split(proj, offs, axis=-1)


def rope(x, cos, sin):
    x1, x2 = jnp.split(x, 2, axis=-1)
    return jnp.concatenate([x1 * cos - x2 * sin, x2 * cos + x1 * sin], axis=-1)


def mla(c_q, c_kv, k_rope, positions, q_norm, w_uq, kv_norm, w_ukv):
    B_, S_, _ = c_q.shape
    q = (rmsnorm(c_q, q_norm) @ w_uq).reshape(B_, S_, MLA_HEADS, QK_NOPE_DIM + QK_ROPE_DIM)
    q_nope, q_rope = q[..., :QK_NOPE_DIM], q[..., QK_NOPE_DIM:]
    kv = (rmsnorm(c_kv, kv_norm) @ w_ukv).reshape(B_, S_, MLA_HEADS, QK_NOPE_DIM + V_HEAD_DIM)
    k_nope, v = kv[..., :QK_NOPE_DIM], kv[..., QK_NOPE_DIM:]
    inv_freq = ROPE_BASE ** (-jnp.arange(0, QK_ROPE_DIM, 2, dtype=jnp.float32) / QK_ROPE_DIM)
    ang = positions.astype(jnp.float32)[..., None] * inv_freq
    cos = jnp.cos(ang).astype(q.dtype)
    sin = jnp.sin(ang).astype(q.dtype)
    q_rope = rope(q_rope, cos[:, :, None, :], sin[:, :, None, :])
    k_rope = rope(k_rope, cos, sin)
    scale = (QK_NOPE_DIM + QK_ROPE_DIM) ** -0.5
    nb = S_ // Q_BLOCK

    def to_blocks(a):
        return jnp.moveaxis(a.reshape((B_, nb, Q_BLOCK) + a.shape[2:]), 1, 0)

    k_idx = jnp.arange(S_)

    def block(args):
        qn, qr, start = args
        s = jnp.einsum('bqhd,bkhd->bhqk', qn, k_nope) + jnp.einsum('bqhr,bkr->bhqk', qr, k_rope)
        s = s.astype(jnp.float32) * scale
        q_idx = start + jnp.arange(Q_BLOCK)
        s = jnp.where(k_idx[None, :] <= q_idx[:, None], s, -jnp.inf)
        p = jax.nn.softmax(s, axis=-1).astype(v.dtype)
        return jnp.einsum('bhqk,bkhd->bqhd', p, v)

    starts = jnp.arange(nb, dtype=jnp.int32) * Q_BLOCK
    o = lax.map(block, (to_blocks(q_nope), to_blocks(q_rope), starts))
    return jnp.moveaxis(o, 0, 1).reshape(B_, S_, MLA_HEADS * V_HEAD_DIM)


def spatial_gating(u, v, ln_g, ln_b, w_s, b_s):
    B_, S_, _ = u.shape
    u = jax.nn.gelu(u)
    v = layernorm(jax.nn.gelu(v), ln_g, ln_b)
    n = S_ // SG_CHUNK
    vc = v.reshape(B_, n, SG_CHUNK, SG_GROUPS, SG_WIDTH // SG_GROUPS)
    mask = jnp.tril(jnp.ones((SG_CHUNK, SG_CHUNK), dtype=bool))
    w = jnp.where(mask, w_s, 0)
    s = jnp.einsum('gts,bnsgc->bntgc', w, vc) + b_s.T[:, :, None]
    return u * s.reshape(B_, S_, SG_WIDTH)


def short_conv(xin, bg, cg, conv_w):
    z = cg * xin
    S_ = z.shape[1]
    zp = jnp.pad(z, ((0, 0), (CONV_K - 1, 0), (0, 0)))
    y = sum(conv_w[k] * zp[:, k:k + S_] for k in range(CONV_K))
    return bg * y


def multiscale_pool(p, w_pool, scale):
    B_, S_, _ = p.shape
    pg = p.reshape(B_, S_, len(POOL_WINDOWS), POOL_GROUP).astype(jnp.float32)
    cs = jnp.cumsum(pg, axis=1)
    t = jnp.arange(S_)
    outs = []
    for g, win in enumerate(POOL_WINDOWS):
        c = cs[:, :, g]
        lagged = jnp.pad(c, ((0, 0), (win, 0), (0, 0)))[:, :S_]
        cnt = jnp.minimum(t + 1, win).astype(jnp.float32)[None, :, None]
        outs.append((c - lagged) / cnt - pg[:, :, g])
    pooled = jnp.stack(outs, axis=2).astype(p.dtype)
    mixed = jnp.einsum('bsgc,gcd->bsgd', pooled, w_pool)
    return mixed.reshape(B_, S_, POOL_WIDTH) * scale


def _fwd_setup_inputs(seed: int = 0) -> dict:
    key = jax.random.key(seed)
    ks = jax.random.split(key, 26)
    L = DEPTH

    def nrm(k, shape, fan_in):
        return jax.random.normal(k, shape, jnp.float32) * fan_in ** -0.5

    def gain(k, shape):
        return 1.0 + 0.05 * jax.random.normal(k, shape, jnp.float32)

    def small(k, shape, s):
        return s * jax.random.normal(k, shape, jnp.float32)

    x = jax.random.normal(ks[0], (BATCH, SEQ, D_MODEL), jnp.float32)
    positions = (jax.random.randint(ks[1], (BATCH, 1), 0, 4096, dtype=jnp.int32)
                 + jnp.arange(SEQ, dtype=jnp.int32)[None, :])
    return {
        'x': x,
        'positions': positions,
        'norm_mix_pre': gain(ks[2], (L, D_MODEL)),
        'w_in': nrm(ks[3], (L, D_MODEL, N_IN), D_MODEL),
        'gate_b': small(ks[4], (L, N_BRANCH * D_MODEL), 0.01),
        'q_norm': gain(ks[5], (L, Q_RANK)),
        'w_uq': nrm(ks[6], (L, Q_RANK, MLA_HEADS * (QK_NOPE_DIM + QK_ROPE_DIM)), Q_RANK),
        'kv_norm': gain(ks[7], (L, KV_RANK)),
        'w_ukv': nrm(ks[8], (L, KV_RANK, MLA_HEADS * (QK_NOPE_DIM + V_HEAD_DIM)), KV_RANK),
        'w_br_mla': nrm(ks[9], (L, MLA_HEADS * V_HEAD_DIM, D_MODEL), MLA_HEADS * V_HEAD_DIM),
        'sg_ln_g': gain(ks[10], (L, SG_WIDTH)),
        'sg_ln_b': small(ks[11], (L, SG_WIDTH), 0.02),
        'sg_w': nrm(ks[12], (L, SG_GROUPS, SG_CHUNK, SG_CHUNK), SG_CHUNK),
        'sg_b': gain(ks[13], (L, SG_GROUPS, SG_CHUNK)),
        'w_br_sg': nrm(ks[14], (L, SG_WIDTH, D_MODEL), SG_WIDTH),
        'conv_w': nrm(ks[15], (L, CONV_K, CONV_WIDTH), CONV_K),
        'w_br_conv': nrm(ks[16], (L, CONV_WIDTH, D_MODEL), CONV_WIDTH),
        'pool_w': nrm(ks[17], (L, len(POOL_WINDOWS), POOL_GROUP, POOL_GROUP), POOL_GROUP),
        'pool_scale': gain(ks[18], (L, POOL_WIDTH)),
        'w_br_pool': nrm(ks[19], (L, POOL_WIDTH, D_MODEL), POOL_WIDTH),
        'w_out': nrm(ks[20], (L, D_MODEL, D_MODEL), D_MODEL),
        'norm_mix_post': gain(ks[21], (L, D_MODEL)),
        'norm_ffn_pre': gain(ks[22], (L, D_MODEL)),
        'w_ff1': nrm(ks[23], (L, D_MODEL, D_FF), D_MODEL),
        'w_ff2': nrm(ks[24], (L, D_FF, D_MODEL), D_FF),
        'norm_ffn_post': gain(ks[25], (L, D_MODEL)),
    }


def _fwd_reference(x, positions, norm_mix_pre, w_in, gate_b, q_norm, w_uq, kv_norm, w_ukv, w_br_mla,
              sg_ln_g, sg_ln_b, sg_w, sg_b, w_br_sg, conv_w, w_br_conv, pool_w, pool_scale, w_br_pool,
              w_out, norm_mix_post, norm_ffn_pre, w_ff1, w_ff2, norm_ffn_post):
    B_, S_, D_ = x.shape
    for l in range(DEPTH):
        h = rmsnorm(x, norm_mix_pre[l])
        (c_q, c_kv, k_r, sg_u, sg_v, cv_x, cv_b, cv_c, pool_in, gate_pre) = split_cols(h @ w_in[l])
        gates = jax.nn.sigmoid(gate_pre + gate_b[l]).reshape(B_, S_, N_BRANCH, D_)
        y_a = mla(c_q, c_kv, k_r, positions, q_norm[l], w_uq[l], kv_norm[l], w_ukv[l]) @ w_br_mla[l]
        y_b = spatial_gating(sg_u, sg_v, sg_ln_g[l], sg_ln_b[l], sg_w[l], sg_b[l]) @ w_br_sg[l]
        y_c = short_conv(cv_x, cv_b, cv_c, conv_w[l]) @ w_br_conv[l]
        y_d = multiscale_pool(pool_in, pool_w[l], pool_scale[l]) @ w_br_pool[l]
        merged = (gates[:, :, 0] * y_a + gates[:, :, 1] * y_b
                  + gates[:, :, 2] * y_c + gates[:, :, 3] * y_d)
        x = x + rmsnorm(merged @ w_out[l], norm_mix_post[l])
        h = rmsnorm(x, norm_ffn_pre[l])
        f = jnp.square(jax.nn.relu(h @ w_ff1[l])) @ w_ff2[l]
        x = x + rmsnorm(f, norm_ffn_post[l])
    return x


import jax as _jax
import jax.numpy as _jnp

TWIN_FORMAT = 'train_step'
FWD_PARAMS = ['x', 'positions', 'norm_mix_pre', 'w_in', 'gate_b', 'q_norm', 'w_uq', 'kv_norm', 'w_ukv', 'w_br_mla', 'sg_ln_g', 'sg_ln_b', 'sg_w', 'sg_b', 'w_br_sg', 'conv_w', 'w_br_conv', 'pool_w', 'pool_scale', 'w_br_pool', 'w_out', 'norm_mix_post', 'norm_ffn_pre', 'w_ff1', 'w_ff2', 'norm_ffn_post']
TWIN_WEIGHTS = ['norm_mix_pre', 'w_in', 'gate_b', 'q_norm', 'w_uq', 'kv_norm', 'w_ukv', 'w_br_mla', 'sg_ln_g', 'sg_ln_b', 'sg_w', 'sg_b', 'w_br_sg', 'conv_w', 'w_br_conv', 'pool_w', 'pool_scale', 'w_br_pool', 'w_out', 'norm_mix_post', 'norm_ffn_pre', 'w_ff1', 'w_ff2', 'norm_ffn_post']
TWIN_DIFF_INPUT = 'x'
TWIN_INPUTS = ['x', 'positions', 'norm_mix_pre', 'w_in', 'gate_b', 'q_norm', 'w_uq', 'kv_norm', 'w_ukv', 'w_br_mla', 'sg_ln_g', 'sg_ln_b', 'sg_w', 'sg_b', 'w_br_sg', 'conv_w', 'w_br_conv', 'pool_w', 'pool_scale', 'w_br_pool', 'w_out', 'norm_mix_post', 'norm_ffn_pre', 'w_ff1', 'w_ff2', 'norm_ffn_post', 'loss_target', 'm_norm_mix_pre', 'm_w_in', 'm_gate_b', 'm_q_norm', 'm_w_uq', 'm_kv_norm', 'm_w_ukv', 'm_w_br_mla', 'm_sg_ln_g', 'm_sg_ln_b', 'm_sg_w', 'm_sg_b', 'm_w_br_sg', 'm_conv_w', 'm_w_br_conv', 'm_pool_w', 'm_pool_scale', 'm_w_br_pool', 'm_w_out', 'm_norm_mix_post', 'm_norm_ffn_pre', 'm_w_ff1', 'm_w_ff2', 'm_norm_ffn_post', 'v_norm_mix_pre', 'v_w_in', 'v_gate_b', 'v_q_norm', 'v_w_uq', 'v_kv_norm', 'v_w_ukv', 'v_w_br_mla', 'v_sg_ln_g', 'v_sg_ln_b', 'v_sg_w', 'v_sg_b', 'v_w_br_sg', 'v_conv_w', 'v_w_br_conv', 'v_pool_w', 'v_pool_scale', 'v_w_br_pool', 'v_w_out', 'v_norm_mix_post', 'v_norm_ffn_pre', 'v_w_ff1', 'v_w_ff2', 'v_norm_ffn_post']
TWIN_OUTPUTS = ['loss', 'grad_x', 'grad_norm_mix_pre', 'grad_w_in', 'grad_gate_b', 'grad_q_norm', 'grad_w_uq', 'grad_kv_norm', 'grad_w_ukv', 'grad_w_br_mla', 'grad_sg_ln_g', 'grad_sg_ln_b', 'grad_sg_w', 'grad_sg_b', 'grad_w_br_sg', 'grad_conv_w', 'grad_w_br_conv', 'grad_pool_w', 'grad_pool_scale', 'grad_w_br_pool', 'grad_w_out', 'grad_norm_mix_post', 'grad_norm_ffn_pre', 'grad_w_ff1', 'grad_w_ff2', 'grad_norm_ffn_post', 'delta_norm_mix_pre', 'delta_w_in', 'delta_gate_b', 'delta_q_norm', 'delta_w_uq', 'delta_kv_norm', 'delta_w_ukv', 'delta_w_br_mla', 'delta_sg_ln_g', 'delta_sg_ln_b', 'delta_sg_w', 'delta_sg_b', 'delta_w_br_sg', 'delta_conv_w', 'delta_w_br_conv', 'delta_pool_w', 'delta_pool_scale', 'delta_w_br_pool', 'delta_w_out', 'delta_norm_mix_post', 'delta_norm_ffn_pre', 'delta_w_ff1', 'delta_w_ff2', 'delta_norm_ffn_post', 'new_m_norm_mix_pre', 'new_m_w_in', 'new_m_gate_b', 'new_m_q_norm', 'new_m_w_uq', 'new_m_kv_norm', 'new_m_w_ukv', 'new_m_w_br_mla', 'new_m_sg_ln_g', 'new_m_sg_ln_b', 'new_m_sg_w', 'new_m_sg_b', 'new_m_w_br_sg', 'new_m_conv_w', 'new_m_w_br_conv', 'new_m_pool_w', 'new_m_pool_scale', 'new_m_w_br_pool', 'new_m_w_out', 'new_m_norm_mix_post', 'new_m_norm_ffn_pre', 'new_m_w_ff1', 'new_m_w_ff2', 'new_m_norm_ffn_post', 'new_v_norm_mix_pre', 'new_v_w_in', 'new_v_gate_b', 'new_v_q_norm', 'new_v_w_uq', 'new_v_kv_norm', 'new_v_w_ukv', 'new_v_w_br_mla', 'new_v_sg_ln_g', 'new_v_sg_ln_b', 'new_v_sg_w', 'new_v_sg_b', 'new_v_w_br_sg', 'new_v_conv_w', 'new_v_w_br_conv', 'new_v_pool_w', 'new_v_pool_scale', 'new_v_w_br_pool', 'new_v_w_out', 'new_v_norm_mix_post', 'new_v_norm_ffn_pre', 'new_v_w_ff1', 'new_v_w_ff2', 'new_v_norm_ffn_post']
TWIN_LEAF_KINDS = {'loss': 'loss', 'grad_x': 'grad_x', 'grad_norm_mix_pre': 'grad_w', 'grad_w_in': 'grad_w', 'grad_gate_b': 'grad_w', 'grad_q_norm': 'grad_w', 'grad_w_uq': 'grad_w', 'grad_kv_norm': 'grad_w', 'grad_w_ukv': 'grad_w', 'grad_w_br_mla': 'grad_w', 'grad_sg_ln_g': 'grad_w', 'grad_sg_ln_b': 'grad_w', 'grad_sg_w': 'grad_w', 'grad_sg_b': 'grad_w', 'grad_w_br_sg': 'grad_w', 'grad_conv_w': 'grad_w', 'grad_w_br_conv': 'grad_w', 'grad_pool_w': 'grad_w', 'grad_pool_scale': 'grad_w', 'grad_w_br_pool': 'grad_w', 'grad_w_out': 'grad_w', 'grad_norm_mix_post': 'grad_w', 'grad_norm_ffn_pre': 'grad_w', 'grad_w_ff1': 'grad_w', 'grad_w_ff2': 'grad_w', 'grad_norm_ffn_post': 'grad_w', 'delta_norm_mix_pre': 'delta_w', 'delta_w_in': 'delta_w', 'delta_gate_b': 'delta_w', 'delta_q_norm': 'delta_w', 'delta_w_uq': 'delta_w', 'delta_kv_norm': 'delta_w', 'delta_w_ukv': 'delta_w', 'delta_w_br_mla': 'delta_w', 'delta_sg_ln_g': 'delta_w', 'delta_sg_ln_b': 'delta_w', 'delta_sg_w': 'delta_w', 'delta_sg_b': 'delta_w', 'delta_w_br_sg': 'delta_w', 'delta_conv_w': 'delta_w', 'delta_w_br_conv': 'delta_w', 'delta_pool_w': 'delta_w', 'delta_pool_scale': 'delta_w', 'delta_w_br_pool': 'delta_w', 'delta_w_out': 'delta_w', 'delta_norm_mix_post': 'delta_w', 'delta_norm_ffn_pre': 'delta_w', 'delta_w_ff1': 'delta_w', 'delta_w_ff2': 'delta_w', 'delta_norm_ffn_post': 'delta_w', 'new_m_norm_mix_pre': 'new_m', 'new_m_w_in': 'new_m', 'new_m_gate_b': 'new_m', 'new_m_q_norm': 'new_m', 'new_m_w_uq': 'new_m', 'new_m_kv_norm': 'new_m', 'new_m_w_ukv': 'new_m', 'new_m_w_br_mla': 'new_m', 'new_m_sg_ln_g': 'new_m', 'new_m_sg_ln_b': 'new_m', 'new_m_sg_w': 'new_m', 'new_m_sg_b': 'new_m', 'new_m_w_br_sg': 'new_m', 'new_m_conv_w': 'new_m', 'new_m_w_br_conv': 'new_m', 'new_m_pool_w': 'new_m', 'new_m_pool_scale': 'new_m', 'new_m_w_br_pool': 'new_m', 'new_m_w_out': 'new_m', 'new_m_norm_mix_post': 'new_m', 'new_m_norm_ffn_pre': 'new_m', 'new_m_w_ff1': 'new_m', 'new_m_w_ff2': 'new_m', 'new_m_norm_ffn_post': 'new_m', 'new_v_norm_mix_pre': 'new_v', 'new_v_w_in': 'new_v', 'new_v_gate_b': 'new_v', 'new_v_q_norm': 'new_v', 'new_v_w_uq': 'new_v', 'new_v_kv_norm': 'new_v', 'new_v_w_ukv': 'new_v', 'new_v_w_br_mla': 'new_v', 'new_v_sg_ln_g': 'new_v', 'new_v_sg_ln_b': 'new_v', 'new_v_sg_w': 'new_v', 'new_v_sg_b': 'new_v', 'new_v_w_br_sg': 'new_v', 'new_v_conv_w': 'new_v', 'new_v_w_br_conv': 'new_v', 'new_v_pool_w': 'new_v', 'new_v_pool_scale': 'new_v', 'new_v_w_br_pool': 'new_v', 'new_v_w_out': 'new_v', 'new_v_norm_mix_post': 'new_v', 'new_v_norm_ffn_pre': 'new_v', 'new_v_w_ff1': 'new_v', 'new_v_w_ff2': 'new_v', 'new_v_norm_ffn_post': 'new_v'}


def _forward(args):
    return _fwd_reference(*[args[k] for k in FWD_PARAMS])


def _output_shape():
    def fwd():
        inp = _fwd_setup_inputs(0)
        return _fwd_reference(*[inp[k] for k in FWD_PARAMS])
    out = _jax.eval_shape(fwd)
    return out.shape, out.dtype

N_MICROBATCH = 1
ADAM_LR = 0.001
ADAM_B1 = 0.9
ADAM_B2 = 0.999
ADAM_EPS = 1e-08
ADAM_WD = 0.01
ADAM_STEP = 10
PER_EXAMPLE_BATCH_AXIS = {'x': 0, 'positions': 0, 'loss_target': 0}
SHARED_INPUTS = []
_WEIGHT_DTYPES = {'norm_mix_pre': _jnp.float32, 'w_in': _jnp.float32, 'gate_b': _jnp.float32, 'q_norm': _jnp.float32, 'w_uq': _jnp.float32, 'kv_norm': _jnp.float32, 'w_ukv': _jnp.float32, 'w_br_mla': _jnp.float32, 'sg_ln_g': _jnp.float32, 'sg_ln_b': _jnp.float32, 'sg_w': _jnp.float32, 'sg_b': _jnp.float32, 'w_br_sg': _jnp.float32, 'conv_w': _jnp.float32, 'w_br_conv': _jnp.float32, 'pool_w': _jnp.float32, 'pool_scale': _jnp.float32, 'w_br_pool': _jnp.float32, 'w_out': _jnp.float32, 'norm_mix_post': _jnp.float32, 'norm_ffn_pre': _jnp.float32, 'w_ff1': _jnp.float32, 'w_ff2': _jnp.float32, 'norm_ffn_post': _jnp.float32}
MOMENT_SCALE = {'norm_mix_pre': 1.017750e+01, 'w_in': 4.106718e+00, 'gate_b': 3.679189e+00, 'q_norm': 1.089079e+00, 'w_uq': 8.263535e-01, 'kv_norm': 3.041621e+01, 'w_ukv': 1.476644e+01, 'w_br_mla': 1.089021e+01, 'sg_ln_g': 1.177871e+00, 'sg_ln_b': 1.096255e+00, 'sg_w': 6.537578e-01, 'sg_b': 1.180728e+00, 'w_br_sg': 1.362063e+01, 'conv_w': 2.703133e+00, 'w_br_conv': 1.802544e+00, 'pool_w': 4.941431e+00, 'pool_scale': 6.072853e+00, 'w_br_pool': 2.645902e+00, 'w_out': 1.753530e+01, 'norm_mix_post': 6.750469e+01, 'norm_ffn_pre': 1.149187e+01, 'w_ff1': 5.692975e+00, 'w_ff2': 2.976638e+01, 'norm_ffn_post': 7.336781e+01}


def _to_microbatches(a, axis):
    t = _jnp.moveaxis(a, axis, 0)
    t = t.reshape((N_MICROBATCH, t.shape[0] // N_MICROBATCH) + t.shape[1:])
    return _jnp.moveaxis(t, 1, axis + 1)


def setup_inputs(seed: int = 0) -> dict:
    inp = _fwd_setup_inputs(seed)
    key = _jax.random.fold_in(_jax.random.key(seed), 7919)
    shape, _ = _output_shape()
    out = dict(inp)
    out["loss_target"] = _jax.random.normal(_jax.random.fold_in(key, 0), shape, _jnp.float32)
    for i, name in enumerate(TWIN_WEIGHTS):
        w = inp[name].astype(_jnp.float32)
        if MOMENT_SCALE is None:
            s = _jnp.sqrt(_jnp.mean(_jnp.square(w)) + 1e-30)
        else:
            s = MOMENT_SCALE[name]
        km, kv = _jax.random.split(_jax.random.fold_in(key, i + 1))
        out[name] = w
        out["m_" + name] = s * _jax.random.normal(km, w.shape, _jnp.float32)
        out["v_" + name] = (s * s) * _jax.random.uniform(kv, w.shape, _jnp.float32, 0.5, 1.5)
    if N_MICROBATCH > 1:
        for name, axis in PER_EXAMPLE_BATCH_AXIS.items():
            out[name] = _to_microbatches(out[name], axis)
    return {'x': out['x'], 'positions': out['positions'], 'norm_mix_pre': out['norm_mix_pre'], 'w_in': out['w_in'], 'gate_b': out['gate_b'], 'q_norm': out['q_norm'], 'w_uq': out['w_uq'], 'kv_norm': out['kv_norm'], 'w_ukv': out['w_ukv'], 'w_br_mla': out['w_br_mla'], 'sg_ln_g': out['sg_ln_g'], 'sg_ln_b': out['sg_ln_b'], 'sg_w': out['sg_w'], 'sg_b': out['sg_b'], 'w_br_sg': out['w_br_sg'], 'conv_w': out['conv_w'], 'w_br_conv': out['w_br_conv'], 'pool_w': out['pool_w'], 'pool_scale': out['pool_scale'], 'w_br_pool': out['w_br_pool'], 'w_out': out['w_out'], 'norm_mix_post': out['norm_mix_post'], 'norm_ffn_pre': out['norm_ffn_pre'], 'w_ff1': out['w_ff1'], 'w_ff2': out['w_ff2'], 'norm_ffn_post': out['norm_ffn_post'], 'loss_target': out['loss_target'], 'm_norm_mix_pre': out['m_norm_mix_pre'], 'm_w_in': out['m_w_in'], 'm_gate_b': out['m_gate_b'], 'm_q_norm': out['m_q_norm'], 'm_w_uq': out['m_w_uq'], 'm_kv_norm': out['m_kv_norm'], 'm_w_ukv': out['m_w_ukv'], 'm_w_br_mla': out['m_w_br_mla'], 'm_sg_ln_g': out['m_sg_ln_g'], 'm_sg_ln_b': out['m_sg_ln_b'], 'm_sg_w': out['m_sg_w'], 'm_sg_b': out['m_sg_b'], 'm_w_br_sg': out['m_w_br_sg'], 'm_conv_w': out['m_conv_w'], 'm_w_br_conv': out['m_w_br_conv'], 'm_pool_w': out['m_pool_w'], 'm_pool_scale': out['m_pool_scale'], 'm_w_br_pool': out['m_w_br_pool'], 'm_w_out': out['m_w_out'], 'm_norm_mix_post': out['m_norm_mix_post'], 'm_norm_ffn_pre': out['m_norm_ffn_pre'], 'm_w_ff1': out['m_w_ff1'], 'm_w_ff2': out['m_w_ff2'], 'm_norm_ffn_post': out['m_norm_ffn_post'], 'v_norm_mix_pre': out['v_norm_mix_pre'], 'v_w_in': out['v_w_in'], 'v_gate_b': out['v_gate_b'], 'v_q_norm': out['v_q_norm'], 'v_w_uq': out['v_w_uq'], 'v_kv_norm': out['v_kv_norm'], 'v_w_ukv': out['v_w_ukv'], 'v_w_br_mla': out['v_w_br_mla'], 'v_sg_ln_g': out['v_sg_ln_g'], 'v_sg_ln_b': out['v_sg_ln_b'], 'v_sg_w': out['v_sg_w'], 'v_sg_b': out['v_sg_b'], 'v_w_br_sg': out['v_w_br_sg'], 'v_conv_w': out['v_conv_w'], 'v_w_br_conv': out['v_w_br_conv'], 'v_pool_w': out['v_pool_w'], 'v_pool_scale': out['v_pool_scale'], 'v_w_br_pool': out['v_w_br_pool'], 'v_w_out': out['v_w_out'], 'v_norm_mix_post': out['v_norm_mix_post'], 'v_norm_ffn_pre': out['v_norm_ffn_pre'], 'v_w_ff1': out['v_w_ff1'], 'v_w_ff2': out['v_w_ff2'], 'v_norm_ffn_post': out['v_norm_ffn_post']}


def _loss(weights, diff, rest, loss_target):
    with _jax.named_scope("forward"):
        args = {**rest, TWIN_DIFF_INPUT: diff, **{k: w.astype(_WEIGHT_DTYPES[k]) for k, w in weights.items()}}
        y = _forward(args)
    with _jax.named_scope("loss_head"):
        err = _jnp.square(y.astype(_jnp.float32) - loss_target)
        return 0.5 * _jnp.sum(_jnp.mean(err, axis=-1)) if err.ndim else 0.5 * err


def _adamw(w, g, m, v):
    m = ADAM_B1 * m + (1.0 - ADAM_B1) * g
    v = ADAM_B2 * v + (1.0 - ADAM_B2) * _jnp.square(g)
    m_hat = m / (1.0 - ADAM_B1 ** ADAM_STEP)
    v_hat = v / (1.0 - ADAM_B2 ** ADAM_STEP)
    delta = -ADAM_LR * (m_hat / (_jnp.sqrt(v_hat) + ADAM_EPS) + ADAM_WD * w)
    return delta, m, v


def reference(x, positions, norm_mix_pre, w_in, gate_b, q_norm, w_uq, kv_norm, w_ukv, w_br_mla, sg_ln_g, sg_ln_b, sg_w, sg_b, w_br_sg, conv_w, w_br_conv, pool_w, pool_scale, w_br_pool, w_out, norm_mix_post, norm_ffn_pre, w_ff1, w_ff2, norm_ffn_post, loss_target, m_norm_mix_pre, m_w_in, m_gate_b, m_q_norm, m_w_uq, m_kv_norm, m_w_ukv, m_w_br_mla, m_sg_ln_g, m_sg_ln_b, m_sg_w, m_sg_b, m_w_br_sg, m_conv_w, m_w_br_conv, m_pool_w, m_pool_scale, m_w_br_pool, m_w_out, m_norm_mix_post, m_norm_ffn_pre, m_w_ff1, m_w_ff2, m_norm_ffn_post, v_norm_mix_pre, v_w_in, v_gate_b, v_q_norm, v_w_uq, v_kv_norm, v_w_ukv, v_w_br_mla, v_sg_ln_g, v_sg_ln_b, v_sg_w, v_sg_b, v_w_br_sg, v_conv_w, v_w_br_conv, v_pool_w, v_pool_scale, v_w_br_pool, v_w_out, v_norm_mix_post, v_norm_ffn_pre, v_w_ff1, v_w_ff2, v_norm_ffn_post):
    given = dict(x=x, positions=positions, norm_mix_pre=norm_mix_pre, w_in=w_in, gate_b=gate_b, q_norm=q_norm, w_uq=w_uq, kv_norm=kv_norm, w_ukv=w_ukv, w_br_mla=w_br_mla, sg_ln_g=sg_ln_g, sg_ln_b=sg_ln_b, sg_w=sg_w, sg_b=sg_b, w_br_sg=w_br_sg, conv_w=conv_w, w_br_conv=w_br_conv, pool_w=pool_w, pool_scale=pool_scale, w_br_pool=w_br_pool, w_out=w_out, norm_mix_post=norm_mix_post, norm_ffn_pre=norm_ffn_pre, w_ff1=w_ff1, w_ff2=w_ff2, norm_ffn_post=norm_ffn_post, loss_target=loss_target, m_norm_mix_pre=m_norm_mix_pre, m_w_in=m_w_in, m_gate_b=m_gate_b, m_q_norm=m_q_norm, m_w_uq=m_w_uq, m_kv_norm=m_kv_norm, m_w_ukv=m_w_ukv, m_w_br_mla=m_w_br_mla, m_sg_ln_g=m_sg_ln_g, m_sg_ln_b=m_sg_ln_b, m_sg_w=m_sg_w, m_sg_b=m_sg_b, m_w_br_sg=m_w_br_sg, m_conv_w=m_conv_w, m_w_br_conv=m_w_br_conv, m_pool_w=m_pool_w, m_pool_scale=m_pool_scale, m_w_br_pool=m_w_br_pool, m_w_out=m_w_out, m_norm_mix_post=m_norm_mix_post, m_norm_ffn_pre=m_norm_ffn_pre, m_w_ff1=m_w_ff1, m_w_ff2=m_w_ff2, m_norm_ffn_post=m_norm_ffn_post, v_norm_mix_pre=v_norm_mix_pre, v_w_in=v_w_in, v_gate_b=v_gate_b, v_q_norm=v_q_norm, v_w_uq=v_w_uq, v_kv_norm=v_kv_norm, v_w_ukv=v_w_ukv, v_w_br_mla=v_w_br_mla, v_sg_ln_g=v_sg_ln_g, v_sg_ln_b=v_sg_ln_b, v_sg_w=v_sg_w, v_sg_b=v_sg_b, v_w_br_sg=v_w_br_sg, v_conv_w=v_conv_w, v_w_br_conv=v_w_br_conv, v_pool_w=v_pool_w, v_pool_scale=v_pool_scale, v_w_br_pool=v_w_br_pool, v_w_out=v_w_out, v_norm_mix_post=v_norm_mix_post, v_norm_ffn_pre=v_norm_ffn_pre, v_w_ff1=v_w_ff1, v_w_ff2=v_w_ff2, v_norm_ffn_post=v_norm_ffn_post)
    weights = {n: given[n] for n in TWIN_WEIGHTS}
    shared = {n: given[n] for n in SHARED_INPUTS}
    per_example = {n: given[n] for n in ['x', 'positions']}
    grad_fn = _jax.value_and_grad(_loss, argnums=(0, 1))

    def one_microbatch(ex, loss_target):
        ex = dict(ex)
        diff = ex.pop(TWIN_DIFF_INPUT)
        return grad_fn(weights, diff, {**shared, **ex}, loss_target)

    if N_MICROBATCH == 1:
        loss, (grad_w, grad_x) = one_microbatch(per_example, given["loss_target"])
    else:
        def body(carry, xs):
            loss_sum, grad_sum = carry
            l_k, (gw_k, gx_k) = one_microbatch(xs[0], xs[1])
            with _jax.named_scope("update"):
                return (loss_sum + l_k, _jax.tree.map(_jnp.add, grad_sum, gw_k)), gx_k

        init = (_jnp.zeros((), _jnp.float32), _jax.tree.map(_jnp.zeros_like, weights))
        (loss, grad_w), grad_x = _jax.lax.scan(body, init, (per_example, given["loss_target"]))
    with _jax.named_scope("update"):
        delta_w, new_m, new_v = {}, {}, {}
        for n in TWIN_WEIGHTS:
            delta_w[n], new_m[n], new_v[n] = _adamw(weights[n], grad_w[n], given["m_" + n], given["v_" + n])
    return (loss, grad_x, *[grad_w[n] for n in TWIN_WEIGHTS], *[delta_w[n] for n in TWIN_WEIGHTS],
            *[new_m[n] for n in TWIN_WEIGHTS], *[new_v[n] for n in TWIN_WEIGHTS])
```

```python
import functools
import math

import jax
import jax.numpy as jnp
from jax import lax
from jax.experimental import pallas as pl
from jax.experimental.pallas import tpu as pltpu

F32 = jnp.float32
BF16 = jnp.bfloat16

D_MODEL = 1024
N_HEADS = 4
D_NOPE = 64
D_ROPE = 32
D_V = 64
Q_RANK = 256
KV_RANK = 128
BR_W = 256
SG_CHUNK = 128
SG_GROUPS = 4
POOL_WINDOWS = (2, 4, 8, 16)
POOL_GROUP = 64
CONV_K = 3
D_FF = 4096
N_BRANCH = 4
N_IN = 6048
EPS = 1e-6
ROPE_BASE = 10000.0
ADAM_LR = 0.001
ADAM_B1 = 0.9
ADAM_B2 = 0.999
ADAM_EPS = 1e-08
ADAM_WD = 0.01
ADAM_STEP = 10

N_DEV = 8
LANES = 128
HEAD_PAD = 128
HALO = 16
PA_W = 640
PM_W = 6 * BR_W
VMEM_LIMIT = 56 * 1024 * 1024
SM_SCALE = (D_NOPE + D_ROPE) ** -0.5
TOK_TILE = 512
ATTN_TILE = 512
INPROJ_BWD_TILE = 256

NN = (((1,), (0,)), ((), ()))
NT = (((1,), (1,)), ((), ()))
TN = (((0,), (0,)), ((), ()))
MESH_ID = pl.DeviceIdType.MESH


def _dot(a, b, dims=NN):
    return lax.dot_general(a, b, dims, preferred_element_type=F32)


def _pcall(body, *, name, grid, in_specs, out_specs, out_shape, scratch=(), vmem=None, aliases=None):
    return pl.pallas_call(
        body, name=name, grid=grid, in_specs=in_specs, out_specs=out_specs, out_shape=out_shape,
        scratch_shapes=scratch, input_output_aliases=aliases or {},
        compiler_params=pltpu.CompilerParams(vmem_limit_bytes=vmem))


def _rows(t, width):
    return pl.BlockSpec((t, width), lambda i: (i, 0))


def _whole(shape):
    nd = len(shape)
    return pl.BlockSpec(tuple(shape), lambda *_: (0,) * nd, pipeline_mode=pl.Buffered(1))


def _acc(shape):
    nd = len(shape)
    return pl.BlockSpec(tuple(shape), lambda *_: (0,) * nd)


def _sds(shape, dtype):
    return jax.ShapeDtypeStruct(tuple(shape), dtype)


def _rms(x, g):
    return x * lax.rsqrt(jnp.mean(x * x, axis=-1, keepdims=True) + EPS) * g


def _rms_bwd(x, g, dy):
    r = lax.rsqrt(jnp.mean(x * x, axis=-1, keepdims=True) + EPS)
    xh = x * r
    dg = jnp.sum(dy * xh, axis=0, keepdims=True)
    dxh = dy * g
    dx = r * (dxh - xh * jnp.mean(dxh * xh, axis=-1, keepdims=True))
    return dx, dg


def _sigmoid(x):
    return 1.0 / (1.0 + jnp.exp(-x))


def _gelu(x):
    return jax.nn.gelu(x, approximate=True)


def _accumulate(ref, val, first):
    @pl.when(first)
    def _():
        ref[...] = val

    @pl.when(jnp.logical_not(first))
    def _():
        ref[...] += val


def _inproj_fwd(x, g1, w_in, t):
    s = x.shape[0]

    def body(x_ref, g_ref, w_ref, hb_ref, pa_ref, pm_ref):
        hb = _rms(x_ref[...], g_ref[...]).astype(BF16)
        hb_ref[...] = hb
        pa_ref[...] = _dot(hb, w_ref[:, :PA_W])
        pm_ref[...] = _dot(hb, w_ref[:, PA_W:])

    return _pcall(
        body, name="inproj_fwd", grid=(s // t,),
        in_specs=[_rows(t, D_MODEL), _whole((1, D_MODEL)), _whole(w_in.shape)],
        out_specs=[_rows(t, D_MODEL), _rows(t, PA_W), _rows(t, PM_W)],
        out_shape=[_sds((s, D_MODEL), BF16), _sds((s, PA_W), F32), _sds((s, PM_W), F32)],
        vmem=VMEM_LIMIT)(x, g1, w_in)


def _attn_prep_fwd(pa, cosf, sins, qn, kvn, wq, wqs, wk, wv, t):
    s = pa.shape[0]

    def body(pa_ref, cos_ref, sin_ref, qn_ref, kvn_ref, wq_ref, wqs_ref, wk_ref, wv_ref, q_ref, k_ref, v_ref):
        cosv, sinv = cos_ref[...], sin_ref[...]
        cqn = _rms(pa_ref[:, 0:Q_RANK], qn_ref[...]).astype(BF16)
        ckvn = _rms(pa_ref[:, Q_RANK:Q_RANK + KV_RANK], kvn_ref[...]).astype(BF16)
        k_rope = pa_ref[:, 384:512] * cosv + pa_ref[:, 512:640] * sinv
        for h in range(N_HEADS):
            q = _dot(cqn, wq_ref[h]) * cosv + _dot(cqn, wqs_ref[h]) * sinv
            q_ref[h] = q.astype(BF16)
            k_ref[h] = (_dot(ckvn, wk_ref[h]) + k_rope).astype(BF16)
        v_ref[...] = _dot(ckvn, wv_ref[...]).astype(BF16)

    head_blk = pl.BlockSpec((N_HEADS, t, HEAD_PAD), lambda i: (0, i, 0))
    return _pcall(
        body, name="attn_prep_fwd", grid=(s // t,),
        in_specs=[_rows(t, PA_W), _rows(t, LANES), _rows(t, LANES), _whole(qn.shape), _whole(kvn.shape),
                  _whole(wq.shape), _whole(wqs.shape), _whole(wk.shape), _whole(wv.shape)],
        out_specs=[head_blk, head_blk, _rows(t, BR_W)],
        out_shape=[_sds((N_HEADS, s, HEAD_PAD), BF16), _sds((N_HEADS, s, HEAD_PAD), BF16), _sds((s, BR_W), BF16)],
        vmem=VMEM_LIMIT)(pa, cosf, sins, qn, kvn, wq, wqs, wk, wv)


def _causal_scores(q, k, masked):
    sc = _dot(q, k, NT) * SM_SCALE
    if masked:
        row = lax.broadcasted_iota(jnp.int32, sc.shape, 0)
        col = lax.broadcasted_iota(jnp.int32, sc.shape, 1)
        sc = jnp.where(col <= row, sc, -jnp.inf)
    return sc


def _flash_fwd(q, k, v, t):
    s = v.shape[0]
    n = s // t
    reps = t // LANES

    def body(q_ref, k_ref, v_ref, o_ref, lse_ref, m_scr, l_scr, acc_scr):
        qi, kj = pl.program_id(0), pl.program_id(1)

        @pl.when(kj == 0)
        def _():
            m_scr[...] = jnp.full(m_scr.shape, -jnp.inf, F32)
            l_scr[...] = jnp.zeros(l_scr.shape, F32)
            acc_scr[...] = jnp.zeros(acc_scr.shape, F32)

        def step(masked):
            for h in range(N_HEADS):
                sc = _causal_scores(q_ref[h], k_ref[h], masked)
                m_prev = m_scr[h]
                m_next = jnp.maximum(m_prev, jnp.max(sc, axis=1, keepdims=True))
                alpha = jnp.exp(m_prev - m_next)
                p = jnp.exp(sc - jnp.tile(m_next, (1, reps)))
                l_scr[h] = alpha * l_scr[h] + jnp.sum(p, axis=1, keepdims=True)
                m_scr[h] = m_next
                pair = (h // 2) * LANES
                acc_scr[h] = acc_scr[h] * alpha + _dot(p.astype(BF16), v_ref[:, pair:pair + LANES])

        @pl.when(kj < qi)
        def _():
            step(False)

        @pl.when(kj == qi)
        def _():
            step(True)
            lane = lax.broadcasted_iota(jnp.int32, (t, LANES), 1)
            for pr in range(N_HEADS // 2):
                o0 = acc_scr[2 * pr] / l_scr[2 * pr]
                o1 = acc_scr[2 * pr + 1] / l_scr[2 * pr + 1]
                o_ref[:, pr * LANES:(pr + 1) * LANES] = jnp.where(lane < D_V, o0, o1).astype(BF16)
            for h in range(N_HEADS):
                lse_ref[h] = m_scr[h] + jnp.log(l_scr[h])

    q_blk = pl.BlockSpec((N_HEADS, t, HEAD_PAD), lambda qi, kj: (0, qi, 0))
    k_blk = pl.BlockSpec((N_HEADS, t, HEAD_PAD), lambda qi, kj: (0, jnp.minimum(kj, qi), 0))
    v_blk = pl.BlockSpec((t, BR_W), lambda qi, kj: (jnp.minimum(kj, qi), 0))
    return _pcall(
        body, name="flash_fwd", grid=(n, n),
        in_specs=[q_blk, k_blk, v_blk],
        out_specs=[pl.BlockSpec((t, BR_W), lambda qi, kj: (qi, 0)), q_blk],
        out_shape=[_sds((s, BR_W), BF16), _sds((N_HEADS, s, LANES), F32)],
        scratch=[pltpu.VMEM((N_HEADS, t, LANES), F32)] * 3,
        vmem=VMEM_LIMIT)(q, k, v)


def _shift_down(ext, k, t):
    return pltpu.roll(ext, k, 0)[HALO:HALO + t]


def _shift_up(ext, k, t):
    return pltpu.roll(ext, t + HALO - k, 0)[0:t]


def _lane_group(shape):
    return lax.shift_right_logical(lax.broadcasted_iota(jnp.int32, shape, 1), 6)


def _group_select(vals):
    grp = _lane_group(vals[0].shape)
    out = vals[0]
    for g in range(1, len(vals)):
        out = jnp.where(grp == g, vals[g], out)
    return out


def _layernorm(x, g, b):
    mu = jnp.mean(x, axis=-1, keepdims=True)
    xc = x - mu
    return xc * lax.rsqrt(jnp.mean(xc * xc, axis=-1, keepdims=True) + EPS) * g + b


def _sg_mix(wm_ref, vnb, bias):
    return _group_select([_dot(wm_ref[g], vnb) for g in range(SG_GROUPS)]) + bias


def _pool_windows(ext):
    s2 = ext + pltpu.roll(ext, 1, 0)
    s4 = s2 + pltpu.roll(s2, 2, 0)
    s8 = s4 + pltpu.roll(s4, 4, 0)
    s16 = s8 + pltpu.roll(s8, 8, 0)
    return [s2, s4, s8, s16]


def _pool_counts(tok):
    return [jnp.minimum(tok + 1, w).astype(F32) for w in POOL_WINDOWS]


def _halo_before(t):
    return pl.BlockSpec((HALO, PM_W), lambda i: (jnp.maximum(i * (t // HALO) - 1, 0), 0))


def _mixers_fwd(pm, ln_g, ln_b, wm, sgb, conv_w, wp, pscale, t):
    s = pm.shape[0]

    def body(pm_ref, halo_ref, lng_ref, lnb_ref, wm_ref, sgb_ref, cw_ref, wp_ref, ps_ref, bsg_ref, bcv_ref, bpl_ref):
        i = pl.program_id(0)
        halo = jnp.where(i > 0, halo_ref[...], 0.0)
        u = _gelu(pm_ref[:, 0:256])
        vnb = _layernorm(_gelu(pm_ref[:, 256:512]), lng_ref[...], lnb_ref[...]).astype(BF16)
        for c in range(t // SG_CHUNK):
            rows = slice(c * SG_CHUNK, (c + 1) * SG_CHUNK)
            bsg_ref[rows, :] = (u[rows] * _sg_mix(wm_ref, vnb[rows], sgb_ref[...])).astype(BF16)
        z = pm_ref[:, 1024:1280] * pm_ref[:, 512:768]
        zext = jnp.concatenate([halo[:, 1024:1280] * halo[:, 512:768], z], axis=0)
        y = cw_ref[0:1, :] * _shift_down(zext, 2, t) + cw_ref[1:2, :] * _shift_down(zext, 1, t) + cw_ref[2:3, :] * z
        bcv_ref[...] = (pm_ref[:, 768:1024] * y).astype(BF16)
        p = pm_ref[:, 1280:1536]
        sums = _pool_windows(jnp.concatenate([halo[:, 1280:1536], p], axis=0))
        tok = i * t + lax.broadcasted_iota(jnp.int32, (t, 1), 0)
        pooled = _group_select([sw[HALO:HALO + t] / cnt - p for sw, cnt in zip(sums, _pool_counts(tok))])
        bpl_ref[...] = (_dot(pooled.astype(BF16), wp_ref[...]) * ps_ref[...]).astype(BF16)

    return _pcall(
        body, name="mixers_fwd", grid=(s // t,),
        in_specs=[_rows(t, PM_W), _halo_before(t), _whole(ln_g.shape), _whole(ln_b.shape), _whole(wm.shape),
                  _whole(sgb.shape), _whole(conv_w.shape), _whole(wp.shape), _whole(pscale.shape)],
        out_specs=[_rows(t, BR_W)] * 3,
        out_shape=[_sds((s, BR_W), BF16)] * 3,
        vmem=VMEM_LIMIT)(pm, pm, ln_g, ln_b, wm, sgb, conv_w, wp, pscale)


def _merge_fwd(x, hb, branches, wg, bg, wbr, wout, g2, t):
    s = x.shape[0]

    def body(x_ref, hb_ref, b0_ref, b1_ref, b2_ref, b3_ref, wg_ref, bg_ref, wbr_ref, wout_ref, g2_ref,
             xmid_ref, mrg_ref, t_ref):
        hb = hb_ref[...]
        merged = jnp.zeros((t, D_MODEL), F32)
        for k, b_ref in enumerate((b0_ref, b1_ref, b2_ref, b3_ref)):
            cols = slice(k * D_MODEL, (k + 1) * D_MODEL)
            gate = _sigmoid(_dot(hb, wg_ref[:, cols]) + bg_ref[:, cols])
            merged = merged + gate * _dot(b_ref[...], wbr_ref[k])
        mb = merged.astype(BF16)
        mrg_ref[...] = mb
        tt = _dot(mb, wout_ref[...])
        t_ref[...] = tt
        xmid_ref[...] = x_ref[...] + _rms(tt, g2_ref[...])

    return _pcall(
        body, name="merge_fwd", grid=(s // t,),
        in_specs=[_rows(t, D_MODEL), _rows(t, D_MODEL)] + [_rows(t, BR_W)] * 4 +
                 [_whole(wg.shape), _whole(bg.shape), _whole(wbr.shape), _whole(wout.shape), _whole(g2.shape)],
        out_specs=[_rows(t, D_MODEL)] * 3,
        out_shape=[_sds((s, D_MODEL), F32), _sds((s, D_MODEL), BF16), _sds((s, D_MODEL), F32)],
        vmem=VMEM_LIMIT)(x, hb, *branches, wg, bg, wbr, wout, g2)


def _ffn_fwd(x, g3, w1, w2, g4, t):
    s = x.shape[0]

    def body(x_ref, g3_ref, w1_ref, w2_ref, g4_ref, xo_ref, f_ref):
        h = _rms(x_ref[...], g3_ref[...]).astype(BF16)
        f = jnp.zeros((t, D_MODEL), F32)
        for j in range(D_FF // D_MODEL):
            cols = slice(j * D_MODEL, (j + 1) * D_MODEL)
            r = jnp.square(jnp.maximum(_dot(h, w1_ref[:, cols]), 0.0)).astype(BF16)
            f = f + _dot(r, w2_ref[cols, :])
        f_ref[...] = f
        xo_ref[...] = x_ref[...] + _rms(f, g4_ref[...])

    return _pcall(
        body, name="ffn_fwd", grid=(s // t,),
        in_specs=[_rows(t, D_MODEL), _whole(g3.shape), _whole(w1.shape), _whole(w2.shape), _whole(g4.shape)],
        out_specs=[_rows(t, D_MODEL)] * 2,
        out_shape=[_sds((s, D_MODEL), F32)] * 2,
        vmem=VMEM_LIMIT)(x, g3, w1, w2, g4)


def _loss_head(y, target, t):
    s = y.shape[0]
    n = s // t

    def body(y_ref, tg_ref, dy_ref, loss_ref, acc_scr):
        i = pl.program_id(0)
        d = y_ref[...] - tg_ref[...]
        dy_ref[...] = d * (1.0 / D_MODEL)
        _accumulate(acc_scr, jnp.sum(d * d, axis=0, keepdims=True), i == 0)

        @pl.when(i == n - 1)
        def _():
            loss_ref[...] = jnp.full(loss_ref.shape, 0.5 / D_MODEL, F32) * jnp.sum(acc_scr[...])

    return _pcall(
        body, name="loss_head", grid=(n,),
        in_specs=[_rows(t, D_MODEL)] * 2,
        out_specs=[_rows(t, D_MODEL), _acc((8, LANES))],
        out_shape=[_sds((s, D_MODEL), F32), _sds((8, LANES), F32)],
        scratch=[pltpu.VMEM((1, D_MODEL), F32)])(y, target)


def _ffn_bwd_norms(x_mid, f, dxo, g3, g4, t):
    s = x_mid.shape[0]

    def body(x_ref, f_ref, dxo_ref, g3_ref, g4_ref, h2_ref, df_ref, dg4_ref):
        h2_ref[...] = _rms(x_ref[...], g3_ref[...]).astype(BF16)
        df, dg4 = _rms_bwd(f_ref[...], g4_ref[...], dxo_ref[...])
        df_ref[...] = df.astype(BF16)
        _accumulate(dg4_ref, dg4, pl.program_id(0) == 0)

    return _pcall(
        body, name="ffn_bwd_norms", grid=(s // t,),
        in_specs=[_rows(t, D_MODEL)] * 3 + [_whole(g3.shape), _whole(g4.shape)],
        out_specs=[_rows(t, D_MODEL), _rows(t, D_MODEL), _acc((1, D_MODEL))],
        out_shape=[_sds((s, D_MODEL), BF16), _sds((s, D_MODEL), BF16), _sds((1, D_MODEL), F32)])(x_mid, f, dxo, g3, g4)


def _ffn_bwd_weights(h2, df, w1, w2, t):
    s = h2.shape[0]
    blk = D_MODEL

    def body(h2_ref, df_ref, w1_ref, w2_ref, da_ref, dw1_ref, dw2_ref):
        first = pl.program_id(1) == 0
        h2v, dfv = h2_ref[...], df_ref[...]
        rl = jnp.maximum(_dot(h2v, w1_ref[...]), 0.0)
        _accumulate(dw2_ref, _dot((rl * rl).astype(BF16), dfv, TN), first)
        da = (_dot(dfv, w2_ref[...], NT) * (2.0 * rl)).astype(BF16)
        da_ref[...] = da
        _accumulate(dw1_ref, _dot(h2v, da, TN), first)

    tok = pl.BlockSpec((t, D_MODEL), lambda j, i: (i, 0))
    return _pcall(
        body, name="ffn_bwd_weights", grid=(D_FF // blk, s // t),
        in_specs=[tok, tok, pl.BlockSpec((D_MODEL, blk), lambda j, i: (0, j)), pl.BlockSpec((blk, D_MODEL), lambda j, i: (j, 0))],
        out_specs=[pl.BlockSpec((t, blk), lambda j, i: (i, j)), pl.BlockSpec((D_MODEL, blk), lambda j, i: (0, j)),
                   pl.BlockSpec((blk, D_MODEL), lambda j, i: (j, 0))],
        out_shape=[_sds((s, D_FF), BF16), _sds((D_MODEL, D_FF), F32), _sds((D_FF, D_MODEL), F32)],
        vmem=VMEM_LIMIT)(h2, df, w1, w2)


def _ffn_bwd_input(da, w1, x_mid, dxo, g3, t):
    s = x_mid.shape[0]

    def body(da_ref, w1_ref, x_ref, dxo_ref, g3_ref, dx_ref, dg3_ref):
        dx, dg3 = _rms_bwd(x_ref[...], g3_ref[...], _dot(da_ref[...], w1_ref[...], NT))
        dx_ref[...] = dxo_ref[...] + dx
        _accumulate(dg3_ref, dg3, pl.program_id(0) == 0)

    return _pcall(
        body, name="ffn_bwd_input", grid=(s // t,),
        in_specs=[_rows(t, D_FF), _whole(w1.shape), _rows(t, D_MODEL), _rows(t, D_MODEL), _whole(g3.shape)],
        out_specs=[_rows(t, D_MODEL), _acc((1, D_MODEL))],
        out_shape=[_sds((s, D_MODEL), F32), _sds((1, D_MODEL), F32)],
        vmem=VMEM_LIMIT)(da, w1, x_mid, dxo, g3)


def _merge_bwd_out(tt, dxmid, g2, wout, merged, t):
    s = tt.shape[0]

    def body(t_ref, dx_ref, g2_ref, wout_ref, mrg_ref, dm_ref, dwout_ref, dg2_ref):
        first = pl.program_id(0) == 0
        dt, dg2 = _rms_bwd(t_ref[...], g2_ref[...], dx_ref[...])
        dtb = dt.astype(BF16)
        dm_ref[...] = _dot(dtb, wout_ref[...], NT)
        _accumulate(dwout_ref, _dot(mrg_ref[...], dtb, TN), first)
        _accumulate(dg2_ref, dg2, first)

    return _pcall(
        body, name="merge_bwd_out", grid=(s // t,),
        in_specs=[_rows(t, D_MODEL), _rows(t, D_MODEL), _whole(g2.shape), _whole(wout.shape), _rows(t, D_MODEL)],
        out_specs=[_rows(t, D_MODEL), _acc((D_MODEL, D_MODEL)), _acc((1, D_MODEL))],
        out_shape=[_sds((s, D_MODEL), F32), _sds((D_MODEL, D_MODEL), F32), _sds((1, D_MODEL), F32)],
        vmem=VMEM_LIMIT)(tt, dxmid, g2, wout, merged)


def _merge_bwd_branch(k, hb, br, dm, wg, bg, wbr, t):
    s = hb.shape[0]

    def body(hb_ref, br_ref, dm_ref, wg_ref, bg_ref, wbr_ref, dbr_ref, dpre_ref, dwg_ref, dbg_ref, dwbr_ref):
        first = pl.program_id(0) == 0
        hbv, brv, dmv, wbrv = hb_ref[...], br_ref[...], dm_ref[...], wbr_ref[0]
        gate = _sigmoid(_dot(hbv, wg_ref[...]) + bg_ref[...])
        dy = (dmv * gate).astype(BF16)
        dpre = dmv * _dot(brv, wbrv) * gate * (1.0 - gate)
        dpb = dpre.astype(BF16)
        dpre_ref[...] = dpb
        dbr_ref[...] = _dot(dy, wbrv, NT)
        _accumulate(dwbr_ref, _dot(brv, dy, TN), first)
        _accumulate(dwg_ref, _dot(hbv, dpb, TN), first)
        _accumulate(dbg_ref, jnp.sum(dpre, axis=0, keepdims=True), first)

    return _pcall(
        body, name=f"merge_bwd_branch{k}", grid=(s // t,),
        in_specs=[_rows(t, D_MODEL), _rows(t, BR_W), _rows(t, D_MODEL),
                  pl.BlockSpec((D_MODEL, D_MODEL), lambda i: (0, k)), pl.BlockSpec((1, D_MODEL), lambda i: (0, k)),
                  pl.BlockSpec((1, BR_W, D_MODEL), lambda i: (k, 0, 0))],
        out_specs=[_rows(t, BR_W), _rows(t, D_MODEL), _acc((D_MODEL, D_MODEL)), _acc((1, D_MODEL)), _acc((BR_W, D_MODEL))],
        out_shape=[_sds((s, BR_W), F32), _sds((s, D_MODEL), BF16), _sds((D_MODEL, D_MODEL), F32),
                   _sds((1, D_MODEL), F32), _sds((BR_W, D_MODEL), F32)],
        vmem=VMEM_LIMIT)(hb, br, dm, wg, bg, wbr)


def _mixers_bwd(pm, dbsg, dbcv, dbpl, ln_g, ln_b, wm, wmt, sgb, conv_w, wp, pscale, t):
    s = pm.shape[0]
    n = s // t
    nb = t // HALO

    def body(pm_ref, before_ref, after_ref, dsg_ref, dcv_ref, dcva_ref, dpl_ref, dpla_ref, lng_ref, lnb_ref,
             wm_ref, wmt_ref, sgb_ref, cw_ref, wp_ref, ps_ref,
             dpm_ref, dlng_ref, dlnb_ref, dwm_ref, dsgb_ref, dcw_ref, dwp_ref, dps_ref):
        i = pl.program_id(0)
        first = i == 0
        before = jnp.where(i > 0, before_ref[...], 0.0)
        after = jnp.where(i < n - 1, after_ref[...], 0.0)

        u_raw, v_raw = pm_ref[:, 0:256], pm_ref[:, 256:512]
        lng, lnb = lng_ref[...], lnb_ref[...]
        u, gelu_u_vjp = jax.vjp(_gelu, u_raw)
        vn, norm_vjp = jax.vjp(lambda v_, g_, b_: _layernorm(_gelu(v_), g_, b_), v_raw, lng, lnb)
        vnb = vn.astype(BF16)
        dsg = dsg_ref[...]
        grp = _lane_group((SG_CHUNK, BR_W))
        tri = (lax.broadcasted_iota(jnp.int32, (SG_CHUNK, SG_CHUNK), 1)
               <= lax.broadcasted_iota(jnp.int32, (SG_CHUNK, SG_CHUNK), 0))
        du_parts, dvn_parts = [], []
        dsgb = jnp.zeros((SG_CHUNK, BR_W), F32)
        dwm = [jnp.zeros((SG_CHUNK, SG_CHUNK), F32) for _ in range(SG_GROUPS)]
        for c in range(t // SG_CHUNK):
            rows = slice(c * SG_CHUNK, (c + 1) * SG_CHUNK)
            mix = _sg_mix(wm_ref, vnb[rows], sgb_ref[...])
            du_parts.append(dsg[rows] * mix)
            ds = dsg[rows] * u[rows]
            dsgb = dsgb + ds
            dsb = [jnp.where(grp == g, ds, 0.0).astype(BF16) for g in range(SG_GROUPS)]
            for g in range(SG_GROUPS):
                dwm[g] = dwm[g] + _dot(dsb[g], vnb[rows], NT)
            dvn_parts.append(_group_select([_dot(wmt_ref[g], dsb[g]) for g in range(SG_GROUPS)]))
        (du_raw,) = gelu_u_vjp(jnp.concatenate(du_parts, axis=0))
        dv_raw, dlng, dlnb = norm_vjp(jnp.concatenate(dvn_parts, axis=0))
        dpm_ref[:, 0:256] = du_raw.astype(BF16)
        dpm_ref[:, 256:512] = dv_raw.astype(BF16)
        _accumulate(dlng_ref, dlng, first)
        _accumulate(dlnb_ref, dlnb, first)
        _accumulate(dsgb_ref, dsgb, first)
        for g in range(SG_GROUPS):
            _accumulate(dwm_ref.at[g], jnp.where(tri, dwm[g], 0.0), first)

        xin, bg, cg = pm_ref[:, 512:768], pm_ref[:, 768:1024], pm_ref[:, 1024:1280]
        z = cg * xin
        zext = jnp.concatenate([before[:, 1024:1280] * before[:, 512:768], z], axis=0)
        z1, z2 = _shift_down(zext, 1, t), _shift_down(zext, 2, t)
        w0, w1, w2 = cw_ref[0:1, :], cw_ref[1:2, :], cw_ref[2:3, :]
        dcv = dcv_ref[...]
        y = w0 * z2 + w1 * z1 + w2 * z
        dy = dcv * bg
        dyext = jnp.concatenate([dy, jnp.where(i < n - 1, dcva_ref[...], 0.0) * after[:, 768:1024]], axis=0)
        dz = w2 * dy + w1 * _shift_up(dyext, 1, t) + w0 * _shift_up(dyext, 2, t)
        dpm_ref[:, 512:768] = (dz * cg).astype(BF16)
        dpm_ref[:, 768:1024] = (dcv * y).astype(BF16)
        dpm_ref[:, 1024:1280] = (dz * xin).astype(BF16)
        dcw = jnp.concatenate([jnp.sum(dy * z2, axis=0, keepdims=True), jnp.sum(dy * z1, axis=0, keepdims=True),
                               jnp.sum(dy * z, axis=0, keepdims=True)], axis=0)
        _accumulate(dcw_ref, jnp.concatenate([dcw, jnp.zeros((8 - CONV_K, BR_W), F32)], axis=0), first)

        p = pm_ref[:, 1280:1536]
        tok = i * t + lax.broadcasted_iota(jnp.int32, (t, 1), 0)
        sums = _pool_windows(jnp.concatenate([before[:, 1280:1536], p], axis=0))
        pooled = _group_select([sw[HALO:HALO + t] / cnt - p for sw, cnt in zip(sums, _pool_counts(tok))]).astype(BF16)
        mixed = _dot(pooled, wp_ref[...])
        dpl = dpl_ref[...]
        ps = ps_ref[...]
        dmix = (dpl * ps).astype(BF16)
        _accumulate(dps_ref, jnp.sum(dpl * mixed, axis=0, keepdims=True), first)
        _accumulate(dwp_ref, _dot(pooled, dmix, TN), first)
        dmix_after = (jnp.where(i < n - 1, dpla_ref[...], 0.0) * ps).astype(BF16)
        dpo = _dot(dmix, wp_ref[...], NT)
        dpo_ext = jnp.concatenate([dpo, _dot(dmix_after, wp_ref[...], NT)], axis=0)
        tok_ext = i * t + lax.broadcasted_iota(jnp.int32, (t + HALO, 1), 0)
        dp_groups = []
        for g, (win, cnt) in enumerate(zip(POOL_WINDOWS, _pool_counts(tok_ext))):
            e = dpo_ext / cnt
            acc = e
            span = 1
            while span < win:
                acc = acc + pltpu.roll(acc, t + HALO - span, 0)
                span *= 2
            dp_groups.append(acc[0:t] - dpo)
        dpm_ref[:, 1280:1536] = _group_select(dp_groups).astype(BF16)

    row_blk = lambda w: pl.BlockSpec((t, w), lambda i: (i, 0))
    after_blk = lambda w: pl.BlockSpec((HALO, w), lambda i: (jnp.minimum((i + 1) * nb, n * nb - 1), 0))
    return _pcall(
        body, name="mixers_bwd", grid=(n,),
        in_specs=[row_blk(PM_W), _halo_before(t), after_blk(PM_W), row_blk(BR_W), row_blk(BR_W), after_blk(BR_W),
                  row_blk(BR_W), after_blk(BR_W), _whole(ln_g.shape), _whole(ln_b.shape), _whole(wm.shape),
                  _whole(wmt.shape), _whole(sgb.shape), _whole(conv_w.shape), _whole(wp.shape), _whole(pscale.shape)],
        out_specs=[row_blk(PM_W), _acc((1, BR_W)), _acc((1, BR_W)), _acc((SG_GROUPS, SG_CHUNK, SG_CHUNK)),
                   _acc((SG_CHUNK, BR_W)), _acc((8, BR_W)), _acc((BR_W, BR_W)), _acc((1, BR_W))],
        out_shape=[_sds((s, PM_W), BF16), _sds((1, BR_W), F32), _sds((1, BR_W), F32),
                   _sds((SG_GROUPS, SG_CHUNK, SG_CHUNK), F32), _sds((SG_CHUNK, BR_W), F32), _sds((8, BR_W), F32),
                   _sds((BR_W, BR_W), F32), _sds((1, BR_W), F32)],
        vmem=VMEM_LIMIT)(pm, pm, pm, dbsg, dbcv, dbcv, dbpl, dbpl, ln_g, ln_b, wm, wmt, sgb, conv_w, wp, pscale)


def _head_delta(o_ref, do_ref, t):
    prod = o_ref[...].astype(F32) * do_ref[...]
    lane = lax.broadcasted_iota(jnp.int32, (t, LANES), 1)
    out = []
    for h in range(N_HEADS):
        pair = prod[:, (h // 2) * LANES:(h // 2 + 1) * LANES]
        mine = (lane < D_V) if h % 2 == 0 else (lane >= D_V)
        out.append(jnp.broadcast_to(jnp.sum(jnp.where(mine, pair, 0.0), axis=1, keepdims=True), (t, LANES)))
    return out


def _head_do(do_ref, h, t):
    lane = lax.broadcasted_iota(jnp.int32, (t, LANES), 1)
    mine = (lane < D_V) if h % 2 == 0 else (lane >= D_V)
    pair = do_ref[:, (h // 2) * LANES:(h // 2 + 1) * LANES]
    return jnp.where(mine, pair, 0.0).astype(BF16)


def _flash_bwd_dq(q, k, v, o, do, lse, t):
    s = v.shape[0]
    n = s // t
    reps = t // LANES

    def body(q_ref, k_ref, v_ref, o_ref, do_ref, lse_ref, dq_ref, delta_scr):
        qi, kj = pl.program_id(0), pl.program_id(1)

        @pl.when(kj == 0)
        def _():
            dq_ref[...] = jnp.zeros(dq_ref.shape, F32)
            for h, dlt in enumerate(_head_delta(o_ref, do_ref, t)):
                delta_scr[h] = dlt

        def step(masked):
            for h in range(N_HEADS):
                kh = k_ref[h]
                p = jnp.exp(_causal_scores(q_ref[h], kh, masked) - jnp.tile(lse_ref[h], (1, reps)))
                pair = (h // 2) * LANES
                dp = _dot(_head_do(do_ref, h, t), v_ref[:, pair:pair + LANES], NT)
                ds = p * (dp - jnp.tile(delta_scr[h], (1, reps))) * SM_SCALE
                dq_ref[h] += _dot(ds.astype(BF16), kh)

        @pl.when(kj < qi)
        def _():
            step(False)

        @pl.when(kj == qi)
        def _():
            step(True)

    q_blk = pl.BlockSpec((N_HEADS, t, HEAD_PAD), lambda qi, kj: (0, qi, 0))
    k_blk = pl.BlockSpec((N_HEADS, t, HEAD_PAD), lambda qi, kj: (0, jnp.minimum(kj, qi), 0))
    v_blk = pl.BlockSpec((t, BR_W), lambda qi, kj: (jnp.minimum(kj, qi), 0))
    o_blk = pl.BlockSpec((t, BR_W), lambda qi, kj: (qi, 0))
    return _pcall(
        body, name="flash_bwd_dq", grid=(n, n),
        in_specs=[q_blk, k_blk, v_blk, o_blk, o_blk, q_blk],
        out_specs=q_blk,
        out_shape=_sds((N_HEADS, s, HEAD_PAD), F32),
        scratch=[pltpu.VMEM((N_HEADS, t, LANES), F32)],
        vmem=VMEM_LIMIT)(q, k, v, o, do, lse)


def _flash_bwd_dkv(q, k, v, o, do, lse, t):
    s = v.shape[0]
    n = s // t
    reps = t // LANES

    def body(q_ref, k_ref, v_ref, o_ref, do_ref, lse_ref, dk_ref, dv_ref, dv_scr):
        kj, qi = pl.program_id(0), pl.program_id(1)

        @pl.when(qi == 0)
        def _():
            dk_ref[...] = jnp.zeros(dk_ref.shape, F32)
            dv_scr[...] = jnp.zeros(dv_scr.shape, F32)

        def step(masked):
            delta = _head_delta(o_ref, do_ref, t)
            for h in range(N_HEADS):
                qh = q_ref[h]
                p = jnp.exp(_causal_scores(qh, k_ref[h], masked) - jnp.tile(lse_ref[h], (1, reps)))
                pair = slice((h // 2) * LANES, (h // 2 + 1) * LANES)
                dv_scr[h] += _dot(p.astype(BF16), do_ref[:, pair].astype(BF16), TN)
                dp = _dot(_head_do(do_ref, h, t), v_ref[:, pair], NT)
                ds = p * (dp - jnp.tile(delta[h], (1, reps))) * SM_SCALE
                dk_ref[h] += _dot(ds.astype(BF16), qh, TN)

        @pl.when(qi > kj)
        def _():
            step(False)

        @pl.when(qi == kj)
        def _():
            step(True)

        @pl.when(qi == n - 1)
        def _():
            lane = lax.broadcasted_iota(jnp.int32, (t, LANES), 1)
            for pr in range(N_HEADS // 2):
                dv_ref[:, pr * LANES:(pr + 1) * LANES] = jnp.where(lane < D_V, dv_scr[2 * pr], dv_scr[2 * pr + 1]).astype(BF16)

    q_blk = pl.BlockSpec((N_HEADS, t, HEAD_PAD), lambda kj, qi: (0, jnp.maximum(qi, kj), 0))
    k_blk = pl.BlockSpec((N_HEADS, t, HEAD_PAD), lambda kj, qi: (0, kj, 0))
    v_blk = pl.BlockSpec((t, BR_W), lambda kj, qi: (kj, 0))
    o_blk = pl.BlockSpec((t, BR_W), lambda kj, qi: (jnp.maximum(qi, kj), 0))
    return _pcall(
        body, name="flash_bwd_dkv", grid=(n, n),
        in_specs=[q_blk, k_blk, v_blk, o_blk, o_blk, q_blk],
        out_specs=[k_blk, v_blk],
        out_shape=[_sds((N_HEADS, s, HEAD_PAD), F32), _sds((s, BR_W), BF16)],
        scratch=[pltpu.VMEM((N_HEADS, t, LANES), F32)],
        vmem=VMEM_LIMIT)(q, k, v, o, do, lse)


def _attn_prep_bwd(pa, dq, dk, dv, cosf, sins, qn, kvn, wq, wqs, wk, wv, t):
    s = pa.shape[0]

    def body(pa_ref, dq_ref, dk_ref, dv_ref, cos_ref, sin_ref, qn_ref, kvn_ref, wq_ref, wqs_ref, wk_ref, wv_ref,
             dpa_ref, dwq_ref, dwqs_ref, dwk_ref, dwv_ref, dqn_ref, dkvn_ref):
        first = pl.program_id(0) == 0
        cosv, sinv = cos_ref[...], sin_ref[...]
        cq, ckv = pa_ref[:, 0:Q_RANK], pa_ref[:, Q_RANK:Q_RANK + KV_RANK]
        cqn = _rms(cq, qn_ref[...]).astype(BF16)
        ckvn = _rms(ckv, kvn_ref[...]).astype(BF16)
        dcqn = jnp.zeros((t, Q_RANK), F32)
        dvv = dv_ref[...]
        dckvn = _dot(dvv, wv_ref[...], NT)
        _accumulate(dwv_ref, _dot(ckvn, dvv, TN), first)
        dk_rope = jnp.zeros((t, HEAD_PAD), F32)
        for h in range(N_HEADS):
            dqh = dq_ref[h]
            dqa, dqs = (dqh * cosv).astype(BF16), (dqh * sinv).astype(BF16)
            _accumulate(dwq_ref.at[h], _dot(cqn, dqa, TN), first)
            _accumulate(dwqs_ref.at[h], _dot(cqn, dqs, TN), first)
            dcqn = dcqn + _dot(dqa, wq_ref[h], NT) + _dot(dqs, wqs_ref[h], NT)
            dkh = dk_ref[h]
            dkb = dkh.astype(BF16)
            _accumulate(dwk_ref.at[h], _dot(ckvn, dkb, TN), first)
            dckvn = dckvn + _dot(dkb, wk_ref[h], NT)
            dk_rope = dk_rope + dkh
        dcq, dqn = _rms_bwd(cq, qn_ref[...], dcqn)
        dckv, dkvn = _rms_bwd(ckv, kvn_ref[...], dckvn)
        _accumulate(dqn_ref, dqn, first)
        _accumulate(dkvn_ref, dkvn, first)
        dpa_ref[:, 0:Q_RANK] = dcq.astype(BF16)
        dpa_ref[:, Q_RANK:Q_RANK + KV_RANK] = dckv.astype(BF16)
        dpa_ref[:, 384:512] = (dk_rope * cosv).astype(BF16)
        dpa_ref[:, 512:640] = (dk_rope * sinv).astype(BF16)

    head_blk = pl.BlockSpec((N_HEADS, t, HEAD_PAD), lambda i: (0, i, 0))
    return _pcall(
        body, name="attn_prep_bwd", grid=(s // t,),
        in_specs=[_rows(t, PA_W), head_blk, head_blk, _rows(t, BR_W), _rows(t, LANES), _rows(t, LANES),
                  _whole(qn.shape), _whole(kvn.shape), _whole(wq.shape), _whole(wqs.shape), _whole(wk.shape), _whole(wv.shape)],
        out_specs=[_rows(t, PA_W), _acc(wq.shape), _acc(wqs.shape), _acc(wk.shape), _acc(wv.shape),
                   _acc(qn.shape), _acc(kvn.shape)],
        out_shape=[_sds((s, PA_W), BF16), _sds(wq.shape, F32), _sds(wqs.shape, F32), _sds(wk.shape, F32),
                   _sds(wv.shape, F32), _sds(qn.shape, F32), _sds(kvn.shape, F32)],
        vmem=VMEM_LIMIT)(pa, dq, dk, dv, cosf, sins, qn, kvn, wq, wqs, wk, wv)


def _inproj_bwd(x, g1, dxmid, dpa, dpm, dpres, w_in, wg, t):
    s = x.shape[0]

    def body(x_ref, g1_ref, dxm_ref, dpa_ref, dpm_ref, d0_ref, d1_ref, d2_ref, d3_ref, w_ref, wg_ref,
             dx_ref, dg1_ref, dwa_ref, dwm_ref):
        first = pl.program_id(0) == 0
        xv, g1v = x_ref[...], g1_ref[...]
        hb = _rms(xv, g1v).astype(BF16)
        dpav, dpmv = dpa_ref[...], dpm_ref[...]
        dh = _dot(dpav, w_ref[:, :PA_W], NT) + _dot(dpmv, w_ref[:, PA_W:], NT)
        for k, d_ref in enumerate((d0_ref, d1_ref, d2_ref, d3_ref)):
            dh = dh + _dot(d_ref[...], wg_ref[:, k * D_MODEL:(k + 1) * D_MODEL], NT)
        dx, dg1 = _rms_bwd(xv, g1v, dh)
        dx_ref[...] = dxm_ref[...] + dx
        _accumulate(dg1_ref, dg1, first)
        _accumulate(dwa_ref, _dot(hb, dpav, TN), first)
        _accumulate(dwm_ref, _dot(hb, dpmv, TN), first)

    return _pcall(
        body, name="inproj_bwd", grid=(s // t,),
        in_specs=[_rows(t, D_MODEL), _whole(g1.shape), _rows(t, D_MODEL), _rows(t, PA_W), _rows(t, PM_W)] +
                 [_rows(t, D_MODEL)] * 4 + [_whole(w_in.shape), _whole(wg.shape)],
        out_specs=[_rows(t, D_MODEL), _acc((1, D_MODEL)), _acc((D_MODEL, PA_W)), _acc((D_MODEL, PM_W))],
        out_shape=[_sds((s, D_MODEL), F32), _sds((1, D_MODEL), F32), _sds((D_MODEL, PA_W), F32), _sds((D_MODEL, PM_W), F32)],
        vmem=VMEM_LIMIT)(x, g1, dxmid, dpa, dpm, *dpres, w_in, wg)


def _my_place():
    return lax.axis_index("x"), lax.axis_index("y"), lax.axis_index("c")


def _flip(place, k):
    x, y, c = place
    return (1 - x if k & 4 else x, 1 - y if k & 2 else y, 1 - c if k & 1 else c)


def _rank(place):
    x, y, c = place
    return 4 * x + 2 * y + c


def _all_gather(shard):
    r, c = shard.shape
    chips = (4, 2, 6)

    def body(x_ref, out_ref, send_sems, recv_sems, local_sem):
        me = _my_place()

        def copy(k, src_place, to, src=None):
            slot = out_ref.at[_rank(src_place)]
            return pltpu.make_async_remote_copy(
                src_ref=slot if src is None else src, dst_ref=slot, send_sem=send_sems.at[k], recv_sem=recv_sems.at[k],
                device_id=to, device_id_type=MESH_ID)

        mine = pltpu.make_async_copy(x_ref, out_ref.at[_rank(me)], local_sem)
        mine.start()
        sibling = _flip(me, 1)
        first = [copy(0, me, sibling, src=x_ref)] + [copy(1 + j, me, _flip(me, kc), src=x_ref) for j, kc in enumerate(chips)]
        for cp in first:
            cp.start()
        passed = [copy(4 + j, _flip(me, kc), sibling) for j, kc in enumerate(chips)]
        for j, kc in enumerate(chips):
            copy(1 + j, _flip(me, kc), me).wait_recv()
            passed[j].start()
        copy(0, sibling, me).wait_recv()
        for j, kc in enumerate(chips):
            copy(4 + j, _flip(sibling, kc), me).wait_recv()
        for cp in first + passed:
            cp.wait_send()
        mine.wait()

    return pl.pallas_call(
        body, name=f"all_gather_{r}x{c}_{jnp.dtype(shard.dtype).name}",
        out_shape=_sds((N_DEV, r, c), shard.dtype),
        in_specs=[pl.BlockSpec(memory_space=pl.ANY)], out_specs=pl.BlockSpec(memory_space=pl.ANY),
        scratch_shapes=[pltpu.SemaphoreType.DMA((7,)), pltpu.SemaphoreType.DMA((7,)), pltpu.SemaphoreType.DMA],
    )(shard)


def _all_to_all(blocks):
    _, r, c = blocks.shape

    def body(x_ref, out_ref, send_sems, recv_sems, local_sem):
        me = _my_place()
        my_rank = _rank(me)
        mine = pltpu.make_async_copy(x_ref.at[my_rank], out_ref.at[my_rank], local_sem)
        mine.start()
        sends = []
        for k in range(1, N_DEV):
            peer = _flip(me, k)
            sends.append(pltpu.make_async_remote_copy(
                src_ref=x_ref.at[_rank(peer)], dst_ref=out_ref.at[my_rank], send_sem=send_sems.at[k - 1],
                recv_sem=recv_sems.at[k - 1], device_id=peer, device_id_type=MESH_ID))
        for cp in sends:
            cp.start()
        for k in range(1, N_DEV):
            slot = out_ref.at[_rank(_flip(me, k))]
            pltpu.make_async_remote_copy(
                src_ref=slot, dst_ref=slot, send_sem=send_sems.at[k - 1], recv_sem=recv_sems.at[k - 1],
                device_id=_flip(me, k), device_id_type=MESH_ID).wait_recv()
        for cp in sends:
            cp.wait_send()
        mine.wait()

    return pl.pallas_call(
        body, name="grad_all_to_all",
        out_shape=_sds(blocks.shape, blocks.dtype),
        in_specs=[pl.BlockSpec(memory_space=pl.ANY)], out_specs=pl.BlockSpec(memory_space=pl.ANY),
        scratch_shapes=[pltpu.SemaphoreType.DMA((7,)), pltpu.SemaphoreType.DMA((7,)), pltpu.SemaphoreType.DMA],
    )(blocks)


def _sum_slots(parts, name):
    _, r, c = parts.shape
    t = math.gcd(r, ROW_PAD)

    def body(p_ref, o_ref):
        acc = p_ref[0].astype(F32)
        for d in range(1, N_DEV):
            acc = acc + p_ref[d].astype(F32)
        o_ref[...] = acc

    return _pcall(
        body, name=name, grid=(r // t,),
        in_specs=[pl.BlockSpec((N_DEV, t, c), lambda i: (0, i, 0))], out_specs=_rows(t, c),
        out_shape=_sds((r, c), F32))(parts)


def _adamw(w, g, m, v, name):
    r, c = w.shape
    t = r if r <= ROW_PAD else math.gcd(r, ROW_PAD)

    def body(w_ref, g_ref, m_ref, v_ref, d_ref, nm_ref, nv_ref):
        gv = g_ref[...]
        nm = ADAM_B1 * m_ref[...] + (1.0 - ADAM_B1) * gv
        nv = ADAM_B2 * v_ref[...] + (1.0 - ADAM_B2) * jnp.square(gv)
        m_hat = nm / (1.0 - ADAM_B1 ** ADAM_STEP)
        v_hat = nv / (1.0 - ADAM_B2 ** ADAM_STEP)
        d_ref[...] = -ADAM_LR * (m_hat / (jnp.sqrt(v_hat) + ADAM_EPS) + ADAM_WD * w_ref[...])
        nm_ref[...] = nm
        nv_ref[...] = nv

    return _pcall(
        body, name=name, grid=(r // t,),
        in_specs=[_rows(t, c)] * 4, out_specs=[_rows(t, c)] * 3, out_shape=[_sds((r, c), F32)] * 3)(w, g, m, v)


SHARDED = ("w_in", "w_uq", "w_ukv", "w_br_mla", "w_br_sg", "w_br_conv", "w_br_pool", "w_out", "w_ff1", "w_ff2", "conv_w")
ROW_SHARDED = ("w_out", "w_ff2")
REPLICATED = ("norm_mix_pre", "gate_b", "q_norm", "kv_norm", "sg_ln_g", "sg_ln_b", "sg_w", "sg_b", "pool_w",
              "pool_scale", "norm_mix_post", "norm_ffn_pre", "norm_ffn_post")
ROW_PAD = 256


def _pack_rows(arrays, dtype):
    rows, offsets, at = [], [], 0
    for a in arrays:
        flat = a.reshape(-1).astype(dtype)
        n = -(-flat.shape[0] // D_MODEL)
        flat = jnp.pad(flat, (0, n * D_MODEL - flat.shape[0]))
        rows.append(flat.reshape(n, D_MODEL))
        offsets.append((at, n))
        at += n
    pad = -at % ROW_PAD
    if pad:
        rows.append(jnp.zeros((pad, D_MODEL), dtype))
    return jnp.concatenate(rows, axis=0), offsets


def _unpack_rows(buf, offsets, shapes):
    lead = buf.shape[:-2]
    out = []
    for (at, n), shape in zip(offsets, shapes):
        size = math.prod(shape)
        flat = buf[..., at:at + n, :].reshape(lead + (n * D_MODEL,))[..., :size]
        out.append(flat.reshape(lead + tuple(shape)))
    return out


def _join_shards(name, gathered):
    if name in ROW_SHARDED:
        g = jnp.moveaxis(gathered, 0, 1)
        return g.reshape(g.shape[0], g.shape[1] * g.shape[2], g.shape[3])
    g = jnp.moveaxis(gathered, 0, -2)
    return g.reshape(g.shape[:-2] + (g.shape[-2] * g.shape[-1],))


def _split_shards(name, full):
    if name in ROW_SHARDED:
        l, rr, c = full.shape
        return jnp.moveaxis(full.reshape(l, N_DEV, rr // N_DEV, c), 1, 0)
    g = full.reshape(full.shape[:-1] + (N_DEV, full.shape[-1] // N_DEV))
    return jnp.moveaxis(g, -2, 0)


def _swap_halves(a):
    half = a.shape[-1] // 2
    return jnp.concatenate([a[..., half:], a[..., :half]], axis=-1)


def _pad_cols(a, left, total):
    return jnp.pad(a, ((0, 0),) * (a.ndim - 1) + ((left, total - left - a.shape[-1]),))


def _layer_weights(full, small, l):
    w_in = full["w_in"][l]
    k_r = w_in[:, 384:416]
    w_in_k = jnp.concatenate(
        [w_in[:, 0:384], _pad_cols(k_r, D_NOPE, HEAD_PAD), _pad_cols(_swap_halves(k_r), D_NOPE, HEAD_PAD), w_in[:, 416:1952]], axis=1)
    w_uq = full["w_uq"][l].reshape(Q_RANK, N_HEADS, D_NOPE + D_ROPE)
    wq = jnp.moveaxis(_pad_cols(w_uq, 0, HEAD_PAD), 1, 0)
    wqs = jnp.moveaxis(_pad_cols(_swap_halves(w_uq[..., D_NOPE:]), D_NOPE, HEAD_PAD), 1, 0)
    w_ukv = full["w_ukv"][l].reshape(KV_RANK, N_HEADS, D_NOPE + D_V)
    wk = jnp.moveaxis(_pad_cols(w_ukv[..., :D_NOPE], 0, HEAD_PAD), 1, 0)
    wv = w_ukv[..., D_NOPE:].reshape(KV_RANK, N_HEADS * D_V)
    tri = jnp.tril(jnp.ones((SG_CHUNK, SG_CHUNK), bool))
    wm = jnp.where(tri, small["sg_w"][l], 0.0)
    pool_w = small["pool_w"][l]
    wp = jnp.zeros((BR_W, BR_W), F32)
    for g in range(len(POOL_WINDOWS)):
        wp = wp.at[g * POOL_GROUP:(g + 1) * POOL_GROUP, g * POOL_GROUP:(g + 1) * POOL_GROUP].set(pool_w[g])
    vec = lambda name: small[name][l][None, :]
    return dict(
        w_in=w_in_k, wg=w_in[:, 1952:], bg=vec("gate_b"), g1=vec("norm_mix_pre"), qn=vec("q_norm"), kvn=vec("kv_norm"),
        wq=wq, wqs=wqs, wk=wk, wv=wv,
        ln_g=vec("sg_ln_g"), ln_b=vec("sg_ln_b"), wm=wm.astype(BF16), wmt=jnp.swapaxes(wm, 1, 2).astype(BF16),
        sgb=jnp.repeat(small["sg_b"][l].T, POOL_GROUP, axis=1), conv_w=full["conv_w"][l],
        wp=wp.astype(BF16), pscale=vec("pool_scale"),
        wbr=jnp.stack([full["w_br_mla"][l], full["w_br_sg"][l], full["w_br_conv"][l], full["w_br_pool"][l]]),
        wout=full["w_out"][l], g2=vec("norm_mix_post"), g3=vec("norm_ffn_pre"), w1=full["w_ff1"][l], w2=full["w_ff2"][l],
        g4=vec("norm_ffn_post"))


def _layer_grads(g):
    dwa = g["dwa"]
    d_kr = dwa[:, 384 + D_NOPE:384 + D_NOPE + D_ROPE] + _swap_halves(dwa[:, 512 + D_NOPE:512 + D_NOPE + D_ROPE])
    w_in = jnp.concatenate([dwa[:, 0:384], d_kr, g["dwm_in"]] + list(g["dwg"]), axis=1)
    dwq = jnp.moveaxis(g["dwq"], 0, 1)
    dwqs = jnp.moveaxis(g["dwqs"], 0, 1)
    d_rope = dwq[..., D_NOPE:D_NOPE + D_ROPE] + _swap_halves(dwqs[..., D_NOPE:D_NOPE + D_ROPE])
    w_uq = jnp.concatenate([dwq[..., :D_NOPE], d_rope], axis=-1).reshape(Q_RANK, N_HEADS * (D_NOPE + D_ROPE))
    dwk = jnp.moveaxis(g["dwk"], 0, 1)[..., :D_NOPE]
    w_ukv = jnp.concatenate([dwk, g["dwv"].reshape(KV_RANK, N_HEADS, D_V)], axis=-1).reshape(KV_RANK, N_HEADS * (D_NOPE + D_V))
    pool_w = jnp.stack([g["dwp"][i * POOL_GROUP:(i + 1) * POOL_GROUP, i * POOL_GROUP:(i + 1) * POOL_GROUP]
                        for i in range(len(POOL_WINDOWS))])
    sg_b = g["dsgb"].reshape(SG_CHUNK, SG_GROUPS, POOL_GROUP).sum(axis=-1).T
    return dict(
        w_in=w_in, w_uq=w_uq, w_ukv=w_ukv, w_br_mla=g["dwbr"][0], w_br_sg=g["dwbr"][1], w_br_conv=g["dwbr"][2],
        w_br_pool=g["dwbr"][3], w_out=g["dwout"], w_ff1=g["dw1"], w_ff2=g["dw2"], conv_w=g["dcw"][:CONV_K],
        norm_mix_pre=g["dg1"][0], gate_b=jnp.concatenate([b[0] for b in g["dbg"]]), q_norm=g["dqn"][0], kv_norm=g["dkvn"][0],
        sg_ln_g=g["dlng"][0], sg_ln_b=g["dlnb"][0], sg_w=g["dwm"], sg_b=sg_b, pool_w=pool_w, pool_scale=g["dps"][0],
        norm_mix_post=g["dg2"][0], norm_ffn_pre=g["dg3"][0], norm_ffn_post=g["dg4"][0])


def _rope_tables(positions):
    inv_freq = ROPE_BASE ** (-jnp.arange(0, D_ROPE, 2, dtype=F32) / D_ROPE)
    ang = positions.astype(F32)[:, None] * inv_freq
    cos, sin = jnp.cos(ang), jnp.sin(ang)
    s = positions.shape[0]
    cosf = jnp.concatenate([jnp.ones((s, D_NOPE), F32), cos, cos, jnp.zeros((s, HEAD_PAD - D_NOPE - D_ROPE), F32)], axis=1)
    sins = jnp.concatenate([jnp.zeros((s, D_NOPE), F32), -sin, sin, jnp.zeros((s, HEAD_PAD - D_NOPE - D_ROPE), F32)], axis=1)
    return cosf, sins


def _layer_fwd(x, w, cosf, sins, tiles):
    t, ta = tiles["tok"], tiles["attn"]
    hb, pa, pm = _inproj_fwd(x, w["g1"], w["w_in"], t)
    q, k, v = _attn_prep_fwd(pa, cosf, sins, w["qn"], w["kvn"], w["wq"], w["wqs"], w["wk"], w["wv"], t)
    o, lse = _flash_fwd(q, k, v, ta)
    bsg, bcv, bpl = _mixers_fwd(pm, w["ln_g"], w["ln_b"], w["wm"], w["sgb"], w["conv_w"], w["wp"], w["pscale"], t)
    x_mid, merged, tt = _merge_fwd(x, hb, (o, bsg, bcv, bpl), w["wg"], w["bg"], w["wbr"], w["wout"], w["g2"], t)
    x_out, f = _ffn_fwd(x_mid, w["g3"], w["w1"], w["w2"], w["g4"], t)
    saved = dict(x=x, hb=hb, pa=pa, pm=pm, q=q, k=k, v=v, o=o, lse=lse, bsg=bsg, bcv=bcv, bpl=bpl, x_mid=x_mid,
                 merged=merged, tt=tt, f=f)
    return x_out, saved


def _layer_bwd(dxo, a, w, cosf, sins, tiles):
    t, ta, tb = tiles["tok"], tiles["attn"], tiles["inproj_bwd"]
    g = {}
    h2, df, g["dg4"] = _ffn_bwd_norms(a["x_mid"], a["f"], dxo, w["g3"], w["g4"], t)
    da, g["dw1"], g["dw2"] = _ffn_bwd_weights(h2, df, w["w1"], w["w2"], t)
    dxmid, g["dg3"] = _ffn_bwd_input(da, w["w1"], a["x_mid"], dxo, w["g3"], t)
    dm, g["dwout"], g["dg2"] = _merge_bwd_out(a["tt"], dxmid, w["g2"], w["wout"], a["merged"], t)
    dbrs, dpres, g["dwg"], g["dbg"], g["dwbr"] = [], [], [], [], []
    for k, br in enumerate((a["o"], a["bsg"], a["bcv"], a["bpl"])):
        dbr, dpre, dwg, dbg, dwbr = _merge_bwd_branch(k, a["hb"], br, dm, w["wg"], w["bg"], w["wbr"], t)
        dbrs.append(dbr); dpres.append(dpre); g["dwg"].append(dwg); g["dbg"].append(dbg); g["dwbr"].append(dwbr)
    dpm, g["dlng"], g["dlnb"], g["dwm"], g["dsgb"], g["dcw"], g["dwp"], g["dps"] = _mixers_bwd(
        a["pm"], dbrs[1], dbrs[2], dbrs[3], w["ln_g"], w["ln_b"], w["wm"], w["wmt"], w["sgb"], w["conv_w"], w["wp"], w["pscale"], t)
    dq = _flash_bwd_dq(a["q"], a["k"], a["v"], a["o"], dbrs[0], a["lse"], ta)
    dk, dv = _flash_bwd_dkv(a["q"], a["k"], a["v"], a["o"], dbrs[0], a["lse"], ta)
    dpa, g["dwq"], g["dwqs"], g["dwk"], g["dwv"], g["dqn"], g["dkvn"] = _attn_prep_bwd(
        a["pa"], dq, dk, dv, cosf, sins, w["qn"], w["kvn"], w["wq"], w["wqs"], w["wk"], w["wv"], t)
    dx, g["dg1"], g["dwa"], g["dwm_in"] = _inproj_bwd(a["x"], w["g1"], dxmid, dpa, dpm, dpres, w["w_in"], w["wg"], tb)
    return dx, _layer_grads(g)


def _step(p, x, positions, loss_target):
    s = x.shape[0]
    depth = p["w_in"][0].shape[0]
    tiles = dict(tok=min(TOK_TILE, s), attn=min(ATTN_TILE, s), inproj_bwd=min(INPROJ_BWD_TILE, s))

    shard_arrays = [lax.bitcast_convert_type(p[n][0], BF16) if n == "conv_w" else p[n][0].astype(BF16) for n in SHARDED]
    packed, offsets = _pack_rows(shard_arrays, BF16)
    gathered = _unpack_rows(_all_gather(packed), offsets, [a.shape for a in shard_arrays])
    full = {}
    for n, gth in zip(SHARDED, gathered):
        if n == "conv_w":
            gth = lax.bitcast_convert_type(gth, F32)
        full[n] = _join_shards(n, gth)
    small = {n: p[n][0] for n in REPLICATED}

    cosf, sins = _rope_tables(positions)
    weights = [_layer_weights(full, small, l) for l in range(depth)]
    acts = []
    h = x
    for l in range(depth):
        h, saved = _layer_fwd(h, weights[l], cosf, sins, tiles)
        acts.append(saved)
    dh, loss_blk = _loss_head(h, loss_target, tiles["tok"])
    loss = lax.psum(loss_blk[0, 0], ("x", "y", "c"))
    layer_grads = [None] * depth
    for l in reversed(range(depth)):
        dh, layer_grads[l] = _layer_bwd(dh, acts[l], weights[l], cosf, sins, tiles)
    grads_full = {n: jnp.stack([layer_grads[l][n] for l in range(depth)]) for n in SHARDED + REPLICATED}

    send_arrays = [_split_shards(n, grads_full[n]) for n in SHARDED]
    send_rows = [_pack_rows([a[d] for a in send_arrays], BF16) for d in range(N_DEV)]
    landed = _all_to_all(jnp.stack([r for r, _ in send_rows]))
    g_rows = _sum_slots(landed, "grad_shard_sum")
    g_shard = dict(zip(SHARDED, _unpack_rows(g_rows, send_rows[0][1], [p[n][0].shape for n in SHARDED])))
    small_rows, small_offsets = _pack_rows([grads_full[n] for n in REPLICATED], F32)
    g_small_rows = _sum_slots(_all_gather(small_rows), "grad_replicated_sum")
    g_small = dict(zip(REPLICATED, _unpack_rows(g_small_rows, small_offsets, [p[n][0].shape for n in REPLICATED])))

    grad, delta, new_m, new_v = {}, {}, {}, {}
    for n in SHARDED:
        w, m, v = p[n]
        shape2 = (math.prod(w.shape[:-1]), w.shape[-1])
        grad[n] = g_shard[n]
        d, nm, nv = _adamw(w.reshape(shape2), grad[n].reshape(shape2), m.reshape(shape2), v.reshape(shape2), f"adamw_{n}")
        delta[n], new_m[n], new_v[n] = d.reshape(w.shape), nm.reshape(w.shape), nv.reshape(w.shape)
    packs = [_pack_rows([p[n][i] for n in REPLICATED], F32)[0] for i in range(3)]
    d, nm, nv = _adamw(packs[0], g_small_rows, packs[1], packs[2], "adamw_replicated")
    shapes = [p[n][0].shape for n in REPLICATED]
    for n, gg, dd, mm, vv in zip(REPLICATED, _unpack_rows(g_small_rows, small_offsets, shapes), _unpack_rows(d, small_offsets, shapes),
                                 _unpack_rows(nm, small_offsets, shapes), _unpack_rows(nv, small_offsets, shapes)):
        grad[n], delta[n], new_m[n], new_v[n] = gg, dd, mm, vv
    return loss, dh, grad, delta, new_m, new_v


WEIGHT_ORDER = ("norm_mix_pre", "w_in", "gate_b", "q_norm", "w_uq", "kv_norm", "w_ukv", "w_br_mla", "sg_ln_g", "sg_ln_b",
                "sg_w", "sg_b", "w_br_sg", "conv_w", "w_br_conv", "pool_w", "pool_scale", "w_br_pool", "w_out",
                "norm_mix_post", "norm_ffn_pre", "w_ff1", "w_ff2", "norm_ffn_post")


def kernel(x, positions, norm_mix_pre, w_in, gate_b, q_norm, w_uq, kv_norm, w_ukv, w_br_mla, sg_ln_g, sg_ln_b, sg_w, sg_b, w_br_sg, conv_w, w_br_conv, pool_w, pool_scale, w_br_pool, w_out, norm_mix_post, norm_ffn_pre, w_ff1, w_ff2, norm_ffn_post, loss_target, m_norm_mix_pre, m_w_in, m_gate_b, m_q_norm, m_w_uq, m_kv_norm, m_w_ukv, m_w_br_mla, m_sg_ln_g, m_sg_ln_b, m_sg_w, m_sg_b, m_w_br_sg, m_conv_w, m_w_br_conv, m_pool_w, m_pool_scale, m_w_br_pool, m_w_out, m_norm_mix_post, m_norm_ffn_pre, m_w_ff1, m_w_ff2, m_norm_ffn_post, v_norm_mix_pre, v_w_in, v_gate_b, v_q_norm, v_w_uq, v_kv_norm, v_w_ukv, v_w_br_mla, v_sg_ln_g, v_sg_ln_b, v_sg_w, v_sg_b, v_w_br_sg, v_conv_w, v_w_br_conv, v_pool_w, v_pool_scale, v_w_br_pool, v_w_out, v_norm_mix_post, v_norm_ffn_pre, v_w_ff1, v_w_ff2, v_norm_ffn_post):
    ws = (norm_mix_pre, w_in, gate_b, q_norm, w_uq, kv_norm, w_ukv, w_br_mla, sg_ln_g, sg_ln_b, sg_w, sg_b, w_br_sg, conv_w,
          w_br_conv, pool_w, pool_scale, w_br_pool, w_out, norm_mix_post, norm_ffn_pre, w_ff1, w_ff2, norm_ffn_post)
    ms = (m_norm_mix_pre, m_w_in, m_gate_b, m_q_norm, m_w_uq, m_kv_norm, m_w_ukv, m_w_br_mla, m_sg_ln_g, m_sg_ln_b, m_sg_w,
          m_sg_b, m_w_br_sg, m_conv_w, m_w_br_conv, m_pool_w, m_pool_scale, m_w_br_pool, m_w_out, m_norm_mix_post,
          m_norm_ffn_pre, m_w_ff1, m_w_ff2, m_norm_ffn_post)
    vs = (v_norm_mix_pre, v_w_in, v_gate_b, v_q_norm, v_w_uq, v_kv_norm, v_w_ukv, v_w_br_mla, v_sg_ln_g, v_sg_ln_b, v_sg_w,
          v_sg_b, v_w_br_sg, v_conv_w, v_w_br_conv, v_pool_w, v_pool_scale, v_w_br_pool, v_w_out, v_norm_mix_post,
          v_norm_ffn_pre, v_w_ff1, v_w_ff2, v_norm_ffn_post)
    p = {n: (w, m, v) for n, w, m, v in zip(WEIGHT_ORDER, ws, ms, vs)}
    loss, grad_x, grad, delta, new_m, new_v = _step(p, x[0], positions[0], loss_target[0])
    return (loss, grad_x[None], *[grad[n] for n in WEIGHT_ORDER], *[delta[n] for n in WEIGHT_ORDER],
            *[new_m[n] for n in WEIGHT_ORDER], *[new_v[n] for n in WEIGHT_ORDER])
```

```python
import functools
import math

import jax
import jax.numpy as jnp
from jax import lax
from jax.experimental import pallas as pl
from jax.experimental.pallas import tpu as pltpu

F32 = jnp.float32
BF16 = jnp.bfloat16

D_MODEL = 1024
N_HEADS = 4
D_NOPE = 64
D_ROPE = 32
D_V = 64
Q_RANK = 256
KV_RANK = 128
BR_W = 256
SG_CHUNK = 128
SG_GROUPS = 4
POOL_WINDOWS = (2, 4, 8, 16)
POOL_GROUP = 64
CONV_K = 3
D_FF = 4096
N_BRANCH = 4
N_IN = 6048
EPS = 1e-6
ROPE_BASE = 10000.0
ADAM_LR = 0.001
ADAM_B1 = 0.9
ADAM_B2 = 0.999
ADAM_EPS = 1e-08
ADAM_WD = 0.01
ADAM_STEP = 10

N_DEV = 8
LANES = 128
HEAD_PAD = 128
HALO = 16
PA_W = 640
PM_W = 6 * BR_W
VMEM_LIMIT = 56 * 1024 * 1024
SM_SCALE = (D_NOPE + D_ROPE) ** -0.5
SM_SCALE_LOG2 = SM_SCALE * math.log2(math.e)
TOK_TILE = 512
ATTN_TILE = 512
INPROJ_BWD_TILE = 256

NN = (((1,), (0,)), ((), ()))
NT = (((1,), (1,)), ((), ()))
TN = (((0,), (0,)), ((), ()))
MESH_ID = pl.DeviceIdType.MESH


def _dot(a, b, dims=NN):
    return lax.dot_general(a, b, dims, preferred_element_type=F32)


def _pcall(body, *, name, grid, in_specs, out_specs, out_shape, scratch=(), vmem=None, prefetch=0):
    params = pltpu.CompilerParams(vmem_limit_bytes=vmem)
    if prefetch:
        spec = pltpu.PrefetchScalarGridSpec(num_scalar_prefetch=prefetch, grid=grid, in_specs=in_specs,
                                            out_specs=out_specs, scratch_shapes=scratch)
        return pl.pallas_call(body, name=name, grid_spec=spec, out_shape=out_shape, compiler_params=params)
    return pl.pallas_call(
        body, name=name, grid=grid, in_specs=in_specs, out_specs=out_specs, out_shape=out_shape,
        scratch_shapes=scratch, compiler_params=params)


def _rows(t, width):
    return pl.BlockSpec((t, width), lambda i: (i, 0))


def _whole(shape):
    nd = len(shape)
    return pl.BlockSpec(tuple(shape), lambda *_: (0,) * nd, pipeline_mode=pl.Buffered(1))


def _acc(shape):
    nd = len(shape)
    return pl.BlockSpec(tuple(shape), lambda *_: (0,) * nd)


def _sds(shape, dtype):
    return jax.ShapeDtypeStruct(tuple(shape), dtype)


def _rms(x, g):
    return x * lax.rsqrt(jnp.mean(x * x, axis=-1, keepdims=True) + EPS) * g


def _rms_bwd(x, g, dy):
    r = lax.rsqrt(jnp.mean(x * x, axis=-1, keepdims=True) + EPS)
    xh = x * r
    dg = jnp.sum(dy * xh, axis=0, keepdims=True)
    dxh = dy * g
    dx = r * (dxh - xh * jnp.mean(dxh * xh, axis=-1, keepdims=True))
    return dx, dg


def _sigmoid(x):
    return 1.0 / (1.0 + jnp.exp(-x))


def _gelu(x):
    return jax.nn.gelu(x, approximate=True)


def _accumulate(ref, val, first):
    @pl.when(first)
    def _():
        ref[...] = val

    @pl.when(jnp.logical_not(first))
    def _():
        ref[...] += val


def _inproj_fwd(x, g1, w_in, t):
    s = x.shape[0]

    def body(x_ref, g_ref, w_ref, hb_ref, pa_ref, pm_ref):
        hb = _rms(x_ref[...], g_ref[...]).astype(BF16)
        hb_ref[...] = hb
        pa_ref[...] = _dot(hb, w_ref[:, :PA_W])
        pm_ref[...] = _dot(hb, w_ref[:, PA_W:])

    return _pcall(
        body, name="inproj_fwd", grid=(s // t,),
        in_specs=[_rows(t, D_MODEL), _whole((1, D_MODEL)), _whole(w_in.shape)],
        out_specs=[_rows(t, D_MODEL), _rows(t, PA_W), _rows(t, PM_W)],
        out_shape=[_sds((s, D_MODEL), BF16), _sds((s, PA_W), F32), _sds((s, PM_W), F32)],
        vmem=VMEM_LIMIT)(x, g1, w_in)


def _attn_prep_fwd(pa, cosf, sins, qn, kvn, wq, wqs, wk, wv, t):
    s = pa.shape[0]

    def body(pa_ref, cos_ref, sin_ref, qn_ref, kvn_ref, wq_ref, wqs_ref, wk_ref, wv_ref, q_ref, k_ref, v_ref):
        cosv, sinv = cos_ref[...], sin_ref[...]
        cqn = _rms(pa_ref[:, 0:Q_RANK], qn_ref[...]).astype(BF16)
        ckvn = _rms(pa_ref[:, Q_RANK:Q_RANK + KV_RANK], kvn_ref[...]).astype(BF16)
        k_rope = pa_ref[:, 384:512] * cosv + pa_ref[:, 512:640] * sinv
        for h in range(N_HEADS):
            q = _dot(cqn, wq_ref[h]) * cosv + _dot(cqn, wqs_ref[h]) * sinv
            q_ref[h] = q.astype(BF16)
            k_ref[h] = (_dot(ckvn, wk_ref[h]) + k_rope).astype(BF16)
        v_ref[...] = _dot(ckvn, wv_ref[...]).astype(BF16)

    head_blk = pl.BlockSpec((N_HEADS, t, HEAD_PAD), lambda i: (0, i, 0))
    return _pcall(
        body, name="attn_prep_fwd", grid=(s // t,),
        in_specs=[_rows(t, PA_W), _rows(t, LANES), _rows(t, LANES), _whole(qn.shape), _whole(kvn.shape),
                  _whole(wq.shape), _whole(wqs.shape), _whole(wk.shape), _whole(wv.shape)],
        out_specs=[head_blk, head_blk, _rows(t, BR_W)],
        out_shape=[_sds((N_HEADS, s, HEAD_PAD), BF16), _sds((N_HEADS, s, HEAD_PAD), BF16), _sds((s, BR_W), BF16)],
        vmem=VMEM_LIMIT)(pa, cosf, sins, qn, kvn, wq, wqs, wk, wv)


def _causal_scores(q, k, masked):
    sc = _dot(q, k, NT) * SM_SCALE_LOG2
    if masked:
        row = lax.broadcasted_iota(jnp.int32, sc.shape, 0)
        col = lax.broadcasted_iota(jnp.int32, sc.shape, 1)
        sc = jnp.where(col <= row, sc, -jnp.inf)
    return sc


def _causal_steps(n, key_major):
    if key_major:
        pairs = [(qi, kj) for kj in range(n) for qi in range(kj, n)]
    else:
        pairs = [(qi, kj) for qi in range(n) for kj in range(qi + 1)]
    return (jnp.asarray([p[0] for p in pairs], jnp.int32), jnp.asarray([p[1] for p in pairs], jnp.int32))


def _flash_fwd(q, k, v, t):
    s = v.shape[0]
    n = s // t
    reps = t // LANES
    qi_tab, kj_tab = _causal_steps(n, key_major=False)

    def body(qi_ref, kj_ref, q_ref, k_ref, v_ref, o_ref, lse_ref, m_scr, l_scr, acc_scr):
        step_id = pl.program_id(0)
        qi, kj = qi_ref[step_id], kj_ref[step_id]

        @pl.when(kj == 0)
        def _():
            m_scr[...] = jnp.full(m_scr.shape, -jnp.inf, F32)
            l_scr[...] = jnp.zeros(l_scr.shape, F32)
            acc_scr[...] = jnp.zeros(acc_scr.shape, F32)

        def step(masked):
            for h in range(N_HEADS):
                sc = _causal_scores(q_ref[h], k_ref[h], masked)
                m_prev = m_scr[h]
                m_next = jnp.maximum(m_prev, jnp.max(sc, axis=1, keepdims=True))
                alpha = jnp.exp2(m_prev - m_next)
                p = jnp.exp2(sc - jnp.tile(m_next, (1, reps)))
                l_scr[h] = alpha * l_scr[h] + jnp.sum(p, axis=1, keepdims=True)
                m_scr[h] = m_next
                pair = (h // 2) * LANES
                acc_scr[h] = acc_scr[h] * alpha + _dot(p.astype(BF16), v_ref[:, pair:pair + LANES])

        @pl.when(kj < qi)
        def _():
            step(False)

        @pl.when(kj == qi)
        def _():
            step(True)
            lane = lax.broadcasted_iota(jnp.int32, (t, LANES), 1)
            for pr in range(N_HEADS // 2):
                o0 = acc_scr[2 * pr] / l_scr[2 * pr]
                o1 = acc_scr[2 * pr + 1] / l_scr[2 * pr + 1]
                o_ref[:, pr * LANES:(pr + 1) * LANES] = jnp.where(lane < D_V, o0, o1).astype(BF16)
            for h in range(N_HEADS):
                lse_ref[h] = m_scr[h] + jnp.log2(l_scr[h])

    q_blk = pl.BlockSpec((N_HEADS, t, HEAD_PAD), lambda i, qi, kj: (0, qi[i], 0))
    k_blk = pl.BlockSpec((N_HEADS, t, HEAD_PAD), lambda i, qi, kj: (0, kj[i], 0))
    v_blk = pl.BlockSpec((t, BR_W), lambda i, qi, kj: (kj[i], 0))
    return _pcall(
        body, name="flash_fwd", grid=(int(qi_tab.shape[0]),), prefetch=2,
        in_specs=[q_blk, k_blk, v_blk],
        out_specs=[pl.BlockSpec((t, BR_W), lambda i, qi, kj: (qi[i], 0)), q_blk],
        out_shape=[_sds((s, BR_W), BF16), _sds((N_HEADS, s, LANES), F32)],
        scratch=[pltpu.VMEM((N_HEADS, t, LANES), F32)] * 3,
        vmem=VMEM_LIMIT)(qi_tab, kj_tab, q, k, v)


def _shift_down(ext, k, t):
    return pltpu.roll(ext, k, 0)[HALO:HALO + t]


def _shift_up(ext, k, t):
    return pltpu.roll(ext, t + HALO - k, 0)[0:t]


def _lane_group(shape):
    return lax.shift_right_logical(lax.broadcasted_iota(jnp.int32, shape, 1), 6)


def _group_select(vals):
    grp = _lane_group(vals[0].shape)
    out = vals[0]
    for g in range(1, len(vals)):
        out = jnp.where(grp == g, vals[g], out)
    return out


def _layernorm(x, g, b):
    mu = jnp.mean(x, axis=-1, keepdims=True)
    xc = x - mu
    return xc * lax.rsqrt(jnp.mean(xc * xc, axis=-1, keepdims=True) + EPS) * g + b


def _sg_mix(wm_ref, vnb, bias):
    return _group_select([_dot(wm_ref[g], vnb) for g in range(SG_GROUPS)]) + bias


def _pool_windows(ext):
    s2 = ext + pltpu.roll(ext, 1, 0)
    s4 = s2 + pltpu.roll(s2, 2, 0)
    s8 = s4 + pltpu.roll(s4, 4, 0)
    s16 = s8 + pltpu.roll(s8, 8, 0)
    return [s2, s4, s8, s16]


def _pool_counts(tok):
    return [jnp.minimum(tok + 1, w).astype(F32) for w in POOL_WINDOWS]


def _halo_before(t):
    return pl.BlockSpec((HALO, PM_W), lambda i: (jnp.maximum(i * (t // HALO) - 1, 0), 0))


def _mixers_fwd(pm, ln_g, ln_b, wm, sgb, conv_w, wp, pscale, t):
    s = pm.shape[0]

    def body(pm_ref, halo_ref, lng_ref, lnb_ref, wm_ref, sgb_ref, cw_ref, wp_ref, ps_ref, bsg_ref, bcv_ref, bpl_ref):
        i = pl.program_id(0)
        halo = jnp.where(i > 0, halo_ref[...], 0.0)
        u = _gelu(pm_ref[:, 0:256])
        vnb = _layernorm(_gelu(pm_ref[:, 256:512]), lng_ref[...], lnb_ref[...]).astype(BF16)
        for c in range(t // SG_CHUNK):
            rows = slice(c * SG_CHUNK, (c + 1) * SG_CHUNK)
            bsg_ref[rows, :] = (u[rows] * _sg_mix(wm_ref, vnb[rows], sgb_ref[...])).astype(BF16)
        z = pm_ref[:, 1024:1280] * pm_ref[:, 512:768]
        zext = jnp.concatenate([halo[:, 1024:1280] * halo[:, 512:768], z], axis=0)
        y = cw_ref[0:1, :] * _shift_down(zext, 2, t) + cw_ref[1:2, :] * _shift_down(zext, 1, t) + cw_ref[2:3, :] * z
        bcv_ref[...] = (pm_ref[:, 768:1024] * y).astype(BF16)
        p = pm_ref[:, 1280:1536]
        sums = _pool_windows(jnp.concatenate([halo[:, 1280:1536], p], axis=0))
        tok = i * t + lax.broadcasted_iota(jnp.int32, (t, 1), 0)
        pooled = _group_select([sw[HALO:HALO + t] / cnt - p for sw, cnt in zip(sums, _pool_counts(tok))])
        bpl_ref[...] = (_dot(pooled.astype(BF16), wp_ref[...]) * ps_ref[...]).astype(BF16)

    return _pcall(
        body, name="mixers_fwd", grid=(s // t,),
        in_specs=[_rows(t, PM_W), _halo_before(t), _whole(ln_g.shape), _whole(ln_b.shape), _whole(wm.shape),
                  _whole(sgb.shape), _whole(conv_w.shape), _whole(wp.shape), _whole(pscale.shape)],
        out_specs=[_rows(t, BR_W)] * 3,
        out_shape=[_sds((s, BR_W), BF16)] * 3,
        vmem=VMEM_LIMIT)(pm, pm, ln_g, ln_b, wm, sgb, conv_w, wp, pscale)


def _merge_fwd(x, hb, branches, wg, bg, wbr, wout, g2, t):
    s = x.shape[0]

    def body(x_ref, hb_ref, b0_ref, b1_ref, b2_ref, b3_ref, wg_ref, bg_ref, wbr_ref, wout_ref, g2_ref,
             xmid_ref, mrg_ref, t_ref):
        hb = hb_ref[...]
        merged = jnp.zeros((t, D_MODEL), F32)
        for k, b_ref in enumerate((b0_ref, b1_ref, b2_ref, b3_ref)):
            cols = slice(k * D_MODEL, (k + 1) * D_MODEL)
            gate = _sigmoid(_dot(hb, wg_ref[:, cols]) + bg_ref[:, cols])
            merged = merged + gate * _dot(b_ref[...], wbr_ref[k])
        mb = merged.astype(BF16)
        mrg_ref[...] = mb
        tt = _dot(mb, wout_ref[...])
        t_ref[...] = tt
        xmid_ref[...] = x_ref[...] + _rms(tt, g2_ref[...])

    return _pcall(
        body, name="merge_fwd", grid=(s // t,),
        in_specs=[_rows(t, D_MODEL), _rows(t, D_MODEL)] + [_rows(t, BR_W)] * 4 +
                 [_whole(wg.shape), _whole(bg.shape), _whole(wbr.shape), _whole(wout.shape), _whole(g2.shape)],
        out_specs=[_rows(t, D_MODEL)] * 3,
        out_shape=[_sds((s, D_MODEL), F32), _sds((s, D_MODEL), BF16), _sds((s, D_MODEL), F32)],
        vmem=VMEM_LIMIT)(x, hb, *branches, wg, bg, wbr, wout, g2)


def _ffn_fwd(x, g3, w1, w2, g4, t):
    s = x.shape[0]

    def body(x_ref, g3_ref, w1_ref, w2_ref, g4_ref, xo_ref, f_ref):
        h = _rms(x_ref[...], g3_ref[...]).astype(BF16)
        f = jnp.zeros((t, D_MODEL), F32)
        for j in range(D_FF // D_MODEL):
            cols = slice(j * D_MODEL, (j + 1) * D_MODEL)
            r = jnp.square(jnp.maximum(_dot(h, w1_ref[:, cols]), 0.0)).astype(BF16)
            f = f + _dot(r, w2_ref[cols, :])
        f_ref[...] = f
        xo_ref[...] = x_ref[...] + _rms(f, g4_ref[...])

    return _pcall(
        body, name="ffn_fwd", grid=(s // t,),
        in_specs=[_rows(t, D_MODEL), _whole(g3.shape), _whole(w1.shape), _whole(w2.shape), _whole(g4.shape)],
        out_specs=[_rows(t, D_MODEL)] * 2,
        out_shape=[_sds((s, D_MODEL), F32)] * 2,
        vmem=VMEM_LIMIT)(x, g3, w1, w2, g4)


def _loss_head(y, target, t):
    s = y.shape[0]
    n = s // t

    def body(y_ref, tg_ref, dy_ref, loss_ref, acc_scr):
        i = pl.program_id(0)
        d = y_ref[...] - tg_ref[...]
        dy_ref[...] = d * (1.0 / D_MODEL)
        _accumulate(acc_scr, jnp.sum(d * d, axis=0, keepdims=True), i == 0)

        @pl.when(i == n - 1)
        def _():
            loss_ref[...] = jnp.full(loss_ref.shape, 0.5 / D_MODEL, F32) * jnp.sum(acc_scr[...])

    return _pcall(
        body, name="loss_head", grid=(n,),
        in_specs=[_rows(t, D_MODEL)] * 2,
        out_specs=[_rows(t, D_MODEL), _acc((8, LANES))],
        out_shape=[_sds((s, D_MODEL), F32), _sds((8, LANES), F32)],
        scratch=[pltpu.VMEM((1, D_MODEL), F32)])(y, target)


def _ffn_bwd_norms(x_mid, f, dxo, g3, g4, t):
    s = x_mid.shape[0]

    def body(x_ref, f_ref, dxo_ref, g3_ref, g4_ref, h2_ref, df_ref, dg4_ref):
        h2_ref[...] = _rms(x_ref[...], g3_ref[...]).astype(BF16)
        df, dg4 = _rms_bwd(f_ref[...], g4_ref[...], dxo_ref[...])
        df_ref[...] = df.astype(BF16)
        _accumulate(dg4_ref, dg4, pl.program_id(0) == 0)

    return _pcall(
        body, name="ffn_bwd_norms", grid=(s // t,),
        in_specs=[_rows(t, D_MODEL)] * 3 + [_whole(g3.shape), _whole(g4.shape)],
        out_specs=[_rows(t, D_MODEL), _rows(t, D_MODEL), _acc((1, D_MODEL))],
        out_shape=[_sds((s, D_MODEL), BF16), _sds((s, D_MODEL), BF16), _sds((1, D_MODEL), F32)])(x_mid, f, dxo, g3, g4)


def _ffn_bwd_weights(h2, df, w1, w2, t):
    s = h2.shape[0]
    blk = D_MODEL

    def body(h2_ref, df_ref, w1_ref, w2_ref, da_ref, dw1_ref, dw2_ref):
        first = pl.program_id(1) == 0
        h2v, dfv = h2_ref[...], df_ref[...]
        rl = jnp.maximum(_dot(h2v, w1_ref[...]), 0.0)
        _accumulate(dw2_ref, _dot((rl * rl).astype(BF16), dfv, TN), first)
        da = (_dot(dfv, w2_ref[...], NT) * (2.0 * rl)).astype(BF16)
        da_ref[...] = da
        _accumulate(dw1_ref, _dot(da, h2v, TN), first)

    tok = pl.BlockSpec((t, D_MODEL), lambda j, i: (i, 0))
    hid = pl.BlockSpec((blk, D_MODEL), lambda j, i: (j, 0))
    return _pcall(
        body, name="ffn_bwd_weights", grid=(D_FF // blk, s // t),
        in_specs=[tok, tok, pl.BlockSpec((D_MODEL, blk), lambda j, i: (0, j)), hid],
        out_specs=[pl.BlockSpec((t, blk), lambda j, i: (i, j)), hid, hid],
        out_shape=[_sds((s, D_FF), BF16), _sds((D_FF, D_MODEL), F32), _sds((D_FF, D_MODEL), F32)],
        vmem=VMEM_LIMIT)(h2, df, w1, w2)


def _ffn_bwd_input(da, w1, x_mid, dxo, g3, t):
    s = x_mid.shape[0]

    def body(da_ref, w1_ref, x_ref, dxo_ref, g3_ref, dx_ref, dg3_ref):
        dx, dg3 = _rms_bwd(x_ref[...], g3_ref[...], _dot(da_ref[...], w1_ref[...], NT))
        dx_ref[...] = dxo_ref[...] + dx
        _accumulate(dg3_ref, dg3, pl.program_id(0) == 0)

    return _pcall(
        body, name="ffn_bwd_input", grid=(s // t,),
        in_specs=[_rows(t, D_FF), _whole(w1.shape), _rows(t, D_MODEL), _rows(t, D_MODEL), _whole(g3.shape)],
        out_specs=[_rows(t, D_MODEL), _acc((1, D_MODEL))],
        out_shape=[_sds((s, D_MODEL), F32), _sds((1, D_MODEL), F32)],
        vmem=VMEM_LIMIT)(da, w1, x_mid, dxo, g3)


def _merge_bwd_out(tt, dxmid, g2, wout, merged, t):
    s = tt.shape[0]

    def body(t_ref, dx_ref, g2_ref, wout_ref, mrg_ref, dm_ref, dwout_ref, dg2_ref):
        first = pl.program_id(0) == 0
        dt, dg2 = _rms_bwd(t_ref[...], g2_ref[...], dx_ref[...])
        dtb = dt.astype(BF16)
        dm_ref[...] = _dot(dtb, wout_ref[...], NT)
        _accumulate(dwout_ref, _dot(mrg_ref[...], dtb, TN), first)
        _accumulate(dg2_ref, dg2, first)

    return _pcall(
        body, name="merge_bwd_out", grid=(s // t,),
        in_specs=[_rows(t, D_MODEL), _rows(t, D_MODEL), _whole(g2.shape), _whole(wout.shape), _rows(t, D_MODEL)],
        out_specs=[_rows(t, D_MODEL), _acc((D_MODEL, D_MODEL)), _acc((1, D_MODEL))],
        out_shape=[_sds((s, D_MODEL), F32), _sds((D_MODEL, D_MODEL), F32), _sds((1, D_MODEL), F32)],
        vmem=VMEM_LIMIT)(tt, dxmid, g2, wout, merged)


def _merge_bwd_branch(k, hb, br, dm, wg, bg, wbr, t):
    s = hb.shape[0]

    def body(hb_ref, br_ref, dm_ref, wg_ref, bg_ref, wbr_ref, dbr_ref, dpre_ref, dwg_ref, dbg_ref, dwbr_ref):
        first = pl.program_id(0) == 0
        hbv, brv, dmv, wbrv = hb_ref[...], br_ref[...], dm_ref[...], wbr_ref[0]
        gate = _sigmoid(_dot(hbv, wg_ref[...]) + bg_ref[...])
        dy = (dmv * gate).astype(BF16)
        dpre = dmv * _dot(brv, wbrv) * gate * (1.0 - gate)
        dpb = dpre.astype(BF16)
        dpre_ref[...] = dpb
        dbr_ref[...] = _dot(dy, wbrv, NT)
        _accumulate(dwbr_ref, _dot(dy, brv, TN), first)
        _accumulate(dwg_ref, _dot(dpb, hbv, TN), first)
        _accumulate(dbg_ref, jnp.sum(dpre, axis=0, keepdims=True), first)

    return _pcall(
        body, name=f"merge_bwd_branch{k}", grid=(s // t,),
        in_specs=[_rows(t, D_MODEL), _rows(t, BR_W), _rows(t, D_MODEL),
                  pl.BlockSpec((D_MODEL, D_MODEL), lambda i: (0, k)), pl.BlockSpec((1, D_MODEL), lambda i: (0, k)),
                  pl.BlockSpec((1, BR_W, D_MODEL), lambda i: (k, 0, 0))],
        out_specs=[_rows(t, BR_W), _rows(t, D_MODEL), _acc((D_MODEL, D_MODEL)), _acc((1, D_MODEL)), _acc((D_MODEL, BR_W))],
        out_shape=[_sds((s, BR_W), F32), _sds((s, D_MODEL), BF16), _sds((D_MODEL, D_MODEL), F32),
                   _sds((1, D_MODEL), F32), _sds((D_MODEL, BR_W), F32)],
        vmem=VMEM_LIMIT)(hb, br, dm, wg, bg, wbr)


def _mixers_bwd(pm, dbsg, dbcv, dbpl, ln_g, ln_b, wm, wmt, sgb, conv_w, wp, pscale, t):
    s = pm.shape[0]
    n = s // t
    nb = t // HALO

    def body(pm_ref, before_ref, after_ref, dsg_ref, dcv_ref, dcva_ref, dpl_ref, dpla_ref, lng_ref, lnb_ref,
             wm_ref, wmt_ref, sgb_ref, cw_ref, wp_ref, ps_ref,
             dpm_ref, dlng_ref, dlnb_ref, dwm_ref, dsgb_ref, dcw_ref, dwp_ref, dps_ref):
        i = pl.program_id(0)
        first = i == 0
        before = jnp.where(i > 0, before_ref[...], 0.0)
        after = jnp.where(i < n - 1, after_ref[...], 0.0)

        u_raw, v_raw = pm_ref[:, 0:256], pm_ref[:, 256:512]
        lng, lnb = lng_ref[...], lnb_ref[...]
        u, gelu_u_vjp = jax.vjp(_gelu, u_raw)
        vn, norm_vjp = jax.vjp(lambda v_, g_, b_: _layernorm(_gelu(v_), g_, b_), v_raw, lng, lnb)
        vnb = vn.astype(BF16)
        dsg = dsg_ref[...]
        grp = _lane_group((SG_CHUNK, BR_W))
        tri = (lax.broadcasted_iota(jnp.int32, (SG_CHUNK, SG_CHUNK), 1)
               <= lax.broadcasted_iota(jnp.int32, (SG_CHUNK, SG_CHUNK), 0))
        du_parts, dvn_parts = [], []
        dsgb = jnp.zeros((SG_CHUNK, BR_W), F32)
        dwm = [jnp.zeros((SG_CHUNK, SG_CHUNK), F32) for _ in range(SG_GROUPS)]
        for c in range(t // SG_CHUNK):
            rows = slice(c * SG_CHUNK, (c + 1) * SG_CHUNK)
            mix = _sg_mix(wm_ref, vnb[rows], sgb_ref[...])
            du_parts.append(dsg[rows] * mix)
            ds = dsg[rows] * u[rows]
            dsgb = dsgb + ds
            dsb = [jnp.where(grp == g, ds, 0.0).astype(BF16) for g in range(SG_GROUPS)]
            for g in range(SG_GROUPS):
                dwm[g] = dwm[g] + _dot(dsb[g], vnb[rows], NT)
            dvn_parts.append(_group_select([_dot(wmt_ref[g], dsb[g]) for g in range(SG_GROUPS)]))
        (du_raw,) = gelu_u_vjp(jnp.concatenate(du_parts, axis=0))
        dv_raw, dlng, dlnb = norm_vjp(jnp.concatenate(dvn_parts, axis=0))
        dpm_ref[:, 0:256] = du_raw.astype(BF16)
        dpm_ref[:, 256:512] = dv_raw.astype(BF16)
        _accumulate(dlng_ref, dlng, first)
        _accumulate(dlnb_ref, dlnb, first)
        _accumulate(dsgb_ref, dsgb, first)
        for g in range(SG_GROUPS):
            _accumulate(dwm_ref.at[g], jnp.where(tri, dwm[g], 0.0), first)

        xin, bg, cg = pm_ref[:, 512:768], pm_ref[:, 768:1024], pm_ref[:, 1024:1280]
        z = cg * xin
        zext = jnp.concatenate([before[:, 1024:1280] * before[:, 512:768], z], axis=0)
        z1, z2 = _shift_down(zext, 1, t), _shift_down(zext, 2, t)
        w0, w1, w2 = cw_ref[0:1, :], cw_ref[1:2, :], cw_ref[2:3, :]
        dcv = dcv_ref[...]
        y = w0 * z2 + w1 * z1 + w2 * z
        dy = dcv * bg
        dyext = jnp.concatenate([dy, jnp.where(i < n - 1, dcva_ref[...], 0.0) * after[:, 768:1024]], axis=0)
        dz = w2 * dy + w1 * _shift_up(dyext, 1, t) + w0 * _shift_up(dyext, 2, t)
        dpm_ref[:, 512:768] = (dz * cg).astype(BF16)
        dpm_ref[:, 768:1024] = (dcv * y).astype(BF16)
        dpm_ref[:, 1024:1280] = (dz * xin).astype(BF16)
        dcw = jnp.concatenate([jnp.sum(dy * z2, axis=0, keepdims=True), jnp.sum(dy * z1, axis=0, keepdims=True),
                               jnp.sum(dy * z, axis=0, keepdims=True)], axis=0)
        _accumulate(dcw_ref, jnp.concatenate([dcw, jnp.zeros((8 - CONV_K, BR_W), F32)], axis=0), first)

        p = pm_ref[:, 1280:1536]
        tok = i * t + lax.broadcasted_iota(jnp.int32, (t, 1), 0)
        sums = _pool_windows(jnp.concatenate([before[:, 1280:1536], p], axis=0))
        pooled = _group_select([sw[HALO:HALO + t] / cnt - p for sw, cnt in zip(sums, _pool_counts(tok))]).astype(BF16)
        mixed = _dot(pooled, wp_ref[...])
        dpl = dpl_ref[...]
        ps = ps_ref[...]
        dmix = (dpl * ps).astype(BF16)
        _accumulate(dps_ref, jnp.sum(dpl * mixed, axis=0, keepdims=True), first)
        _accumulate(dwp_ref, _dot(pooled, dmix, TN), first)
        dmix_after = (jnp.where(i < n - 1, dpla_ref[...], 0.0) * ps).astype(BF16)
        dpo = _dot(dmix, wp_ref[...], NT)
        dpo_ext = jnp.concatenate([dpo, _dot(dmix_after, wp_ref[...], NT)], axis=0)
        tok_ext = i * t + lax.broadcasted_iota(jnp.int32, (t + HALO, 1), 0)
        dp_groups = []
        for g, (win, cnt) in enumerate(zip(POOL_WINDOWS, _pool_counts(tok_ext))):
            e = dpo_ext / cnt
            acc = e
            span = 1
            while span < win:
                acc = acc + pltpu.roll(acc, t + HALO - span, 0)
                span *= 2
            dp_groups.append(acc[0:t] - dpo)
        dpm_ref[:, 1280:1536] = _group_select(dp_groups).astype(BF16)

    row_blk = lambda w: pl.BlockSpec((t, w), lambda i: (i, 0))
    after_blk = lambda w: pl.BlockSpec((HALO, w), lambda i: (jnp.minimum((i + 1) * nb, n * nb - 1), 0))
    return _pcall(
        body, name="mixers_bwd", grid=(n,),
        in_specs=[row_blk(PM_W), _halo_before(t), after_blk(PM_W), row_blk(BR_W), row_blk(BR_W), after_blk(BR_W),
                  row_blk(BR_W), after_blk(BR_W), _whole(ln_g.shape), _whole(ln_b.shape), _whole(wm.shape),
                  _whole(wmt.shape), _whole(sgb.shape), _whole(conv_w.shape), _whole(wp.shape), _whole(pscale.shape)],
        out_specs=[row_blk(PM_W), _acc((1, BR_W)), _acc((1, BR_W)), _acc((SG_GROUPS, SG_CHUNK, SG_CHUNK)),
                   _acc((SG_CHUNK, BR_W)), _acc((8, BR_W)), _acc((BR_W, BR_W)), _acc((1, BR_W))],
        out_shape=[_sds((s, PM_W), BF16), _sds((1, BR_W), F32), _sds((1, BR_W), F32),
                   _sds((SG_GROUPS, SG_CHUNK, SG_CHUNK), F32), _sds((SG_CHUNK, BR_W), F32), _sds((8, BR_W), F32),
                   _sds((BR_W, BR_W), F32), _sds((1, BR_W), F32)],
        vmem=VMEM_LIMIT)(pm, pm, pm, dbsg, dbcv, dbcv, dbpl, dbpl, ln_g, ln_b, wm, wmt, sgb, conv_w, wp, pscale)


def _head_delta(o_ref, do_ref, t):
    prod = o_ref[...].astype(F32) * do_ref[...]
    lane = lax.broadcasted_iota(jnp.int32, (t, LANES), 1)
    out = []
    for h in range(N_HEADS):
        pair = prod[:, (h // 2) * LANES:(h // 2 + 1) * LANES]
        mine = (lane < D_V) if h % 2 == 0 else (lane >= D_V)
        out.append(jnp.broadcast_to(jnp.sum(jnp.where(mine, pair, 0.0), axis=1, keepdims=True), (t, LANES)))
    return out


def _head_do(do_ref, h, t):
    lane = lax.broadcasted_iota(jnp.int32, (t, LANES), 1)
    mine = (lane < D_V) if h % 2 == 0 else (lane >= D_V)
    pair = do_ref[:, (h // 2) * LANES:(h // 2 + 1) * LANES]
    return jnp.where(mine, pair, 0.0).astype(BF16)


def _flash_bwd_dq(q, k, v, o, do, lse, t):
    s = v.shape[0]
    n = s // t
    reps = t // LANES

    qi_tab, kj_tab = _causal_steps(n, key_major=False)

    def body(qi_ref, kj_ref, q_ref, k_ref, v_ref, o_ref, do_ref, lse_ref, dq_ref, delta_scr):
        step_id = pl.program_id(0)
        qi, kj = qi_ref[step_id], kj_ref[step_id]

        @pl.when(kj == 0)
        def _():
            dq_ref[...] = jnp.zeros(dq_ref.shape, F32)
            for h, dlt in enumerate(_head_delta(o_ref, do_ref, t)):
                delta_scr[h] = dlt

        def step(masked):
            for h in range(N_HEADS):
                kh = k_ref[h]
                p = jnp.exp2(_causal_scores(q_ref[h], kh, masked) - jnp.tile(lse_ref[h], (1, reps)))
                pair = (h // 2) * LANES
                dp = _dot(_head_do(do_ref, h, t), v_ref[:, pair:pair + LANES], NT)
                ds = p * (dp - jnp.tile(delta_scr[h], (1, reps))) * SM_SCALE
                dq_ref[h] += _dot(ds.astype(BF16), kh)

        @pl.when(kj < qi)
        def _():
            step(False)

        @pl.when(kj == qi)
        def _():
            step(True)

    q_blk = pl.BlockSpec((N_HEADS, t, HEAD_PAD), lambda i, qi, kj: (0, qi[i], 0))
    k_blk = pl.BlockSpec((N_HEADS, t, HEAD_PAD), lambda i, qi, kj: (0, kj[i], 0))
    v_blk = pl.BlockSpec((t, BR_W), lambda i, qi, kj: (kj[i], 0))
    o_blk = pl.BlockSpec((t, BR_W), lambda i, qi, kj: (qi[i], 0))
    return _pcall(
        body, name="flash_bwd_dq", grid=(int(qi_tab.shape[0]),), prefetch=2,
        in_specs=[q_blk, k_blk, v_blk, o_blk, o_blk, q_blk],
        out_specs=q_blk,
        out_shape=_sds((N_HEADS, s, HEAD_PAD), F32),
        scratch=[pltpu.VMEM((N_HEADS, t, LANES), F32)],
        vmem=VMEM_LIMIT)(qi_tab, kj_tab, q, k, v, o, do, lse)


def _flash_bwd_dkv(q, k, v, o, do, lse, t):
    s = v.shape[0]
    n = s // t
    reps = t // LANES

    qi_tab, kj_tab = _causal_steps(n, key_major=True)

    def body(qi_ref, kj_ref, q_ref, k_ref, v_ref, o_ref, do_ref, lse_ref, dk_ref, dv_ref, dv_scr):
        step_id = pl.program_id(0)
        qi, kj = qi_ref[step_id], kj_ref[step_id]

        @pl.when(qi == kj)
        def _():
            dk_ref[...] = jnp.zeros(dk_ref.shape, F32)
            dv_scr[...] = jnp.zeros(dv_scr.shape, F32)

        def step(masked):
            delta = _head_delta(o_ref, do_ref, t)
            for h in range(N_HEADS):
                qh = q_ref[h]
                p = jnp.exp2(_causal_scores(qh, k_ref[h], masked) - jnp.tile(lse_ref[h], (1, reps)))
                pair = slice((h // 2) * LANES, (h // 2 + 1) * LANES)
                dv_scr[h] += _dot(p.astype(BF16), do_ref[:, pair].astype(BF16), TN)
                dp = _dot(_head_do(do_ref, h, t), v_ref[:, pair], NT)
                ds = p * (dp - jnp.tile(delta[h], (1, reps))) * SM_SCALE
                dk_ref[h] += _dot(ds.astype(BF16), qh, TN)

        @pl.when(qi > kj)
        def _():
            step(False)

        @pl.when(qi == kj)
        def _():
            step(True)

        @pl.when(qi == n - 1)
        def _():
            lane = lax.broadcasted_iota(jnp.int32, (t, LANES), 1)
            for pr in range(N_HEADS // 2):
                dv_ref[:, pr * LANES:(pr + 1) * LANES] = jnp.where(lane < D_V, dv_scr[2 * pr], dv_scr[2 * pr + 1]).astype(BF16)

    q_blk = pl.BlockSpec((N_HEADS, t, HEAD_PAD), lambda i, qi, kj: (0, qi[i], 0))
    k_blk = pl.BlockSpec((N_HEADS, t, HEAD_PAD), lambda i, qi, kj: (0, kj[i], 0))
    v_blk = pl.BlockSpec((t, BR_W), lambda i, qi, kj: (kj[i], 0))
    o_blk = pl.BlockSpec((t, BR_W), lambda i, qi, kj: (qi[i], 0))
    return _pcall(
        body, name="flash_bwd_dkv", grid=(int(qi_tab.shape[0]),), prefetch=2,
        in_specs=[q_blk, k_blk, v_blk, o_blk, o_blk, q_blk],
        out_specs=[k_blk, v_blk],
        out_shape=[_sds((N_HEADS, s, HEAD_PAD), F32), _sds((s, BR_W), BF16)],
        scratch=[pltpu.VMEM((N_HEADS, t, LANES), F32)],
        vmem=VMEM_LIMIT)(qi_tab, kj_tab, q, k, v, o, do, lse)


def _attn_prep_bwd(pa, dq, dk, dv, cosf, sins, qn, kvn, wq, wqs, wk, wv, t):
    s = pa.shape[0]

    def body(pa_ref, dq_ref, dk_ref, dv_ref, cos_ref, sin_ref, qn_ref, kvn_ref, wq_ref, wqs_ref, wk_ref, wv_ref,
             dpa_ref, dwq_ref, dwqs_ref, dwk_ref, dwv_ref, dqn_ref, dkvn_ref):
        first = pl.program_id(0) == 0
        cosv, sinv = cos_ref[...], sin_ref[...]
        cq, ckv = pa_ref[:, 0:Q_RANK], pa_ref[:, Q_RANK:Q_RANK + KV_RANK]
        cqn = _rms(cq, qn_ref[...]).astype(BF16)
        ckvn = _rms(ckv, kvn_ref[...]).astype(BF16)
        dcqn = jnp.zeros((t, Q_RANK), F32)
        dvv = dv_ref[...]
        dckvn = _dot(dvv, wv_ref[...], NT)
        _accumulate(dwv_ref, _dot(dvv, ckvn, TN), first)
        dk_rope = jnp.zeros((t, HEAD_PAD), F32)
        for h in range(N_HEADS):
            dqh = dq_ref[h]
            dqa, dqs = (dqh * cosv).astype(BF16), (dqh * sinv).astype(BF16)
            _accumulate(dwq_ref.at[h], _dot(dqa, cqn, TN), first)
            _accumulate(dwqs_ref.at[h], _dot(dqs, cqn, TN), first)
            dcqn = dcqn + _dot(dqa, wq_ref[h], NT) + _dot(dqs, wqs_ref[h], NT)
            dkh = dk_ref[h]
            dkb = dkh.astype(BF16)
            _accumulate(dwk_ref.at[h], _dot(dkb, ckvn, TN), first)
            dckvn = dckvn + _dot(dkb, wk_ref[h], NT)
            dk_rope = dk_rope + dkh
        dcq, dqn = _rms_bwd(cq, qn_ref[...], dcqn)
        dckv, dkvn = _rms_bwd(ckv, kvn_ref[...], dckvn)
        _accumulate(dqn_ref, dqn, first)
        _accumulate(dkvn_ref, dkvn, first)
        dpa_ref[:, 0:Q_RANK] = dcq.astype(BF16)
        dpa_ref[:, Q_RANK:Q_RANK + KV_RANK] = dckv.astype(BF16)
        dpa_ref[:, 384:512] = (dk_rope * cosv).astype(BF16)
        dpa_ref[:, 512:640] = (dk_rope * sinv).astype(BF16)

    head_blk = pl.BlockSpec((N_HEADS, t, HEAD_PAD), lambda i: (0, i, 0))
    wq_t, wk_t, wv_t = (N_HEADS, HEAD_PAD, Q_RANK), (N_HEADS, HEAD_PAD, KV_RANK), (N_HEADS * D_V, KV_RANK)
    return _pcall(
        body, name="attn_prep_bwd", grid=(s // t,),
        in_specs=[_rows(t, PA_W), head_blk, head_blk, _rows(t, BR_W), _rows(t, LANES), _rows(t, LANES),
                  _whole(qn.shape), _whole(kvn.shape), _whole(wq.shape), _whole(wqs.shape), _whole(wk.shape), _whole(wv.shape)],
        out_specs=[_rows(t, PA_W), _acc(wq_t), _acc(wq_t), _acc(wk_t), _acc(wv_t), _acc(qn.shape), _acc(kvn.shape)],
        out_shape=[_sds((s, PA_W), BF16), _sds(wq_t, F32), _sds(wq_t, F32), _sds(wk_t, F32),
                   _sds(wv_t, F32), _sds(qn.shape, F32), _sds(kvn.shape, F32)],
        vmem=VMEM_LIMIT)(pa, dq, dk, dv, cosf, sins, qn, kvn, wq, wqs, wk, wv)


def _inproj_bwd(x, g1, dxmid, dpa, dpm, dpres, w_in, wg, t):
    s = x.shape[0]

    def body(x_ref, g1_ref, dxm_ref, dpa_ref, dpm_ref, d0_ref, d1_ref, d2_ref, d3_ref, w_ref, wg_ref,
             dx_ref, dg1_ref, dwa_ref, dwm_ref):
        first = pl.program_id(0) == 0
        xv, g1v = x_ref[...], g1_ref[...]
        hb = _rms(xv, g1v).astype(BF16)
        dpav, dpmv = dpa_ref[...], dpm_ref[...]
        dh = _dot(dpav, w_ref[:, :PA_W], NT) + _dot(dpmv, w_ref[:, PA_W:], NT)
        for k, d_ref in enumerate((d0_ref, d1_ref, d2_ref, d3_ref)):
            dh = dh + _dot(d_ref[...], wg_ref[:, k * D_MODEL:(k + 1) * D_MODEL], NT)
        dx, dg1 = _rms_bwd(xv, g1v, dh)
        dx_ref[...] = dxm_ref[...] + dx
        _accumulate(dg1_ref, dg1, first)
        _accumulate(dwa_ref, _dot(dpav, hb, TN), first)
        _accumulate(dwm_ref, _dot(dpmv, hb, TN), first)

    return _pcall(
        body, name="inproj_bwd", grid=(s // t,),
        in_specs=[_rows(t, D_MODEL), _whole(g1.shape), _rows(t, D_MODEL), _rows(t, PA_W), _rows(t, PM_W)] +
                 [_rows(t, D_MODEL)] * 4 + [_whole(w_in.shape), _whole(wg.shape)],
        out_specs=[_rows(t, D_MODEL), _acc((1, D_MODEL)), _acc((PA_W, D_MODEL)), _acc((PM_W, D_MODEL))],
        out_shape=[_sds((s, D_MODEL), F32), _sds((1, D_MODEL), F32), _sds((PA_W, D_MODEL), F32), _sds((PM_W, D_MODEL), F32)],
        vmem=VMEM_LIMIT)(x, g1, dxmid, dpa, dpm, *dpres, w_in, wg)


def _my_place():
    return lax.axis_index("x"), lax.axis_index("y"), lax.axis_index("c")


def _flip(place, k):
    x, y, c = place
    return (1 - x if k & 4 else x, 1 - y if k & 2 else y, 1 - c if k & 1 else c)


def _rank(place):
    x, y, c = place
    return 4 * x + 2 * y + c


def _all_gather(shard):
    r, c = shard.shape
    chips = (4, 2, 6)

    def body(x_ref, out_ref, send_sems, recv_sems, local_sem):
        me = _my_place()

        def copy(k, src_place, to, src=None):
            slot = out_ref.at[_rank(src_place)]
            return pltpu.make_async_remote_copy(
                src_ref=slot if src is None else src, dst_ref=slot, send_sem=send_sems.at[k], recv_sem=recv_sems.at[k],
                device_id=to, device_id_type=MESH_ID)

        mine = pltpu.make_async_copy(x_ref, out_ref.at[_rank(me)], local_sem)
        mine.start()
        sibling = _flip(me, 1)
        first = [copy(0, me, sibling, src=x_ref)] + [copy(1 + j, me, _flip(me, kc), src=x_ref) for j, kc in enumerate(chips)]
        for cp in first:
            cp.start()
        passed = [copy(4 + j, _flip(me, kc), sibling) for j, kc in enumerate(chips)]
        for j, kc in enumerate(chips):
            copy(1 + j, _flip(me, kc), me).wait_recv()
            passed[j].start()
        copy(0, sibling, me).wait_recv()
        for j, kc in enumerate(chips):
            copy(4 + j, _flip(sibling, kc), me).wait_recv()
        for cp in first + passed:
            cp.wait_send()
        mine.wait()

    return pl.pallas_call(
        body, name=f"all_gather_{r}x{c}_{jnp.dtype(shard.dtype).name}",
        out_shape=_sds((N_DEV, r, c), shard.dtype),
        in_specs=[pl.BlockSpec(memory_space=pl.ANY)], out_specs=pl.BlockSpec(memory_space=pl.ANY),
        scratch_shapes=[pltpu.SemaphoreType.DMA((7,)), pltpu.SemaphoreType.DMA((7,)), pltpu.SemaphoreType.DMA],
    )(shard)


def _all_to_all(blocks):
    _, r, c = blocks.shape

    def body(x_ref, out_ref, send_sems, recv_sems, local_sem):
        me = _my_place()
        my_rank = _rank(me)
        mine = pltpu.make_async_copy(x_ref.at[my_rank], out_ref.at[my_rank], local_sem)
        mine.start()
        sends = []
        for k in range(1, N_DEV):
            peer = _flip(me, k)
            sends.append(pltpu.make_async_remote_copy(
                src_ref=x_ref.at[_rank(peer)], dst_ref=out_ref.at[my_rank], send_sem=send_sems.at[k - 1],
                recv_sem=recv_sems.at[k - 1], device_id=peer, device_id_type=MESH_ID))
        for cp in sends:
            cp.start()
        for k in range(1, N_DEV):
            slot = out_ref.at[_rank(_flip(me, k))]
            pltpu.make_async_remote_copy(
                src_ref=slot, dst_ref=slot, send_sem=send_sems.at[k - 1], recv_sem=recv_sems.at[k - 1],
                device_id=_flip(me, k), device_id_type=MESH_ID).wait_recv()
        for cp in sends:
            cp.wait_send()
        mine.wait()

    return pl.pallas_call(
        body, name="grad_all_to_all",
        out_shape=_sds(blocks.shape, blocks.dtype),
        in_specs=[pl.BlockSpec(memory_space=pl.ANY)], out_specs=pl.BlockSpec(memory_space=pl.ANY),
        scratch_shapes=[pltpu.SemaphoreType.DMA((7,)), pltpu.SemaphoreType.DMA((7,)), pltpu.SemaphoreType.DMA],
    )(blocks)


def _sum_slots(parts, name):
    _, r, c = parts.shape
    t = math.gcd(r, ROW_PAD)

    def body(p_ref, o_ref):
        acc = p_ref[0].astype(F32)
        for d in range(1, N_DEV):
            acc = acc + p_ref[d].astype(F32)
        o_ref[...] = acc

    return _pcall(
        body, name=name, grid=(r // t,),
        in_specs=[pl.BlockSpec((N_DEV, t, c), lambda i: (0, i, 0))], out_specs=_rows(t, c),
        out_shape=_sds((r, c), F32))(parts)


def _adamw(w, g, m, v, name):
    r, c = w.shape
    t = r if r <= ROW_PAD else math.gcd(r, ROW_PAD)

    def body(w_ref, g_ref, m_ref, v_ref, d_ref, nm_ref, nv_ref):
        gv = g_ref[...]
        nm = ADAM_B1 * m_ref[...] + (1.0 - ADAM_B1) * gv
        nv = ADAM_B2 * v_ref[...] + (1.0 - ADAM_B2) * jnp.square(gv)
        m_hat = nm / (1.0 - ADAM_B1 ** ADAM_STEP)
        v_hat = nv / (1.0 - ADAM_B2 ** ADAM_STEP)
        d_ref[...] = -ADAM_LR * (m_hat / (jnp.sqrt(v_hat) + ADAM_EPS) + ADAM_WD * w_ref[...])
        nm_ref[...] = nm
        nv_ref[...] = nv

    return _pcall(
        body, name=name, grid=(r // t,),
        in_specs=[_rows(t, c)] * 4, out_specs=[_rows(t, c)] * 3, out_shape=[_sds((r, c), F32)] * 3)(w, g, m, v)


SHARDED = ("w_in", "w_uq", "w_ukv", "w_br_mla", "w_br_sg", "w_br_conv", "w_br_pool", "w_out", "w_ff1", "w_ff2", "conv_w")
ROW_SHARDED = ("w_out", "w_ff2")
REPLICATED = ("norm_mix_pre", "gate_b", "q_norm", "kv_norm", "sg_ln_g", "sg_ln_b", "sg_w", "sg_b", "pool_w",
              "pool_scale", "norm_mix_post", "norm_ffn_pre", "norm_ffn_post")
ROW_PAD = 256
GRAD_ROW_PAD = 64
PART_ROWS = 16


def _pack_rows(arrays, dtype, multiple, lead=()):
    rows, offsets, at = [], [], 0
    zero_pad = ((0, 0),) * len(lead)
    for a in arrays:
        flat = a.reshape(lead + (-1,)).astype(dtype)
        n = -(-flat.shape[-1] // D_MODEL)
        n_pad = -(-n // PART_ROWS) * PART_ROWS
        flat = jnp.pad(flat, zero_pad + ((0, n_pad * D_MODEL - flat.shape[-1]),))
        rows.append(flat.reshape(lead + (n_pad, D_MODEL)))
        offsets.append((at, n))
        at += n_pad
    pad = -at % multiple
    if pad:
        rows.append(jnp.zeros(lead + (pad, D_MODEL), dtype))
    return jnp.concatenate(rows, axis=len(lead)), offsets


def _unpack_rows(buf, offsets, shapes):
    lead = buf.shape[:-2]
    out = []
    for (at, n), shape in zip(offsets, shapes):
        size = math.prod(shape)
        flat = buf[..., at:at + n, :].reshape(lead + (n * D_MODEL,))[..., :size]
        out.append(flat.reshape(lead + tuple(shape)))
    return out


def _join_shards(name, gathered):
    if name in ROW_SHARDED:
        g = jnp.moveaxis(gathered, 0, 1)
        return g.reshape(g.shape[0], g.shape[1] * g.shape[2], g.shape[3])
    g = jnp.moveaxis(gathered, 0, -2)
    return g.reshape(g.shape[:-2] + (g.shape[-2] * g.shape[-1],))


def _owner_major(grad_sharded_dim_first):
    a = grad_sharded_dim_first
    return a.reshape((N_DEV, a.shape[0] // N_DEV) + a.shape[1:])


def _owner_block_shape(name, shard_shape):
    k, n = shard_shape[-2:]
    return (k, n) if name in ROW_SHARDED else (n, k)


def _natural_shard(name, block):
    return block if name in ROW_SHARDED else block.T


def _swap_halves(a, axis=-1):
    lo, hi = jnp.split(a, 2, axis=axis)
    return jnp.concatenate([hi, lo], axis=axis)


def _pad_cols(a, left, total):
    return jnp.pad(a, ((0, 0),) * (a.ndim - 1) + ((left, total - left - a.shape[-1]),))


def _layer_weights(full, small, l):
    w_in = full["w_in"][l]
    k_r = w_in[:, 384:416]
    w_in_k = jnp.concatenate(
        [w_in[:, 0:384], _pad_cols(k_r, D_NOPE, HEAD_PAD), _pad_cols(_swap_halves(k_r), D_NOPE, HEAD_PAD), w_in[:, 416:1952]], axis=1)
    w_uq = full["w_uq"][l].reshape(Q_RANK, N_HEADS, D_NOPE + D_ROPE)
    wq = jnp.moveaxis(_pad_cols(w_uq, 0, HEAD_PAD), 1, 0)
    wqs = jnp.moveaxis(_pad_cols(_swap_halves(w_uq[..., D_NOPE:]), D_NOPE, HEAD_PAD), 1, 0)
    w_ukv = full["w_ukv"][l].reshape(KV_RANK, N_HEADS, D_NOPE + D_V)
    wk = jnp.moveaxis(_pad_cols(w_ukv[..., :D_NOPE], 0, HEAD_PAD), 1, 0)
    wv = w_ukv[..., D_NOPE:].reshape(KV_RANK, N_HEADS * D_V)
    tri = jnp.tril(jnp.ones((SG_CHUNK, SG_CHUNK), bool))
    wm = jnp.where(tri, small["sg_w"][l], 0.0)
    pool_w = small["pool_w"][l]
    wp = jnp.zeros((BR_W, BR_W), F32)
    for g in range(len(POOL_WINDOWS)):
        wp = wp.at[g * POOL_GROUP:(g + 1) * POOL_GROUP, g * POOL_GROUP:(g + 1) * POOL_GROUP].set(pool_w[g])
    vec = lambda name: small[name][l][None, :]
    return dict(
        w_in=w_in_k, wg=w_in[:, 1952:], bg=vec("gate_b"), g1=vec("norm_mix_pre"), qn=vec("q_norm"), kvn=vec("kv_norm"),
        wq=wq, wqs=wqs, wk=wk, wv=wv,
        ln_g=vec("sg_ln_g"), ln_b=vec("sg_ln_b"), wm=wm.astype(BF16), wmt=jnp.swapaxes(wm, 1, 2).astype(BF16),
        sgb=jnp.repeat(small["sg_b"][l].T, POOL_GROUP, axis=1), conv_w=full["conv_w"][l],
        wp=wp.astype(BF16), pscale=vec("pool_scale"),
        wbr=jnp.stack([full["w_br_mla"][l], full["w_br_sg"][l], full["w_br_conv"][l], full["w_br_pool"][l]]),
        wout=full["w_out"][l], g2=vec("norm_mix_post"), g3=vec("norm_ffn_pre"), w1=full["w_ff1"][l], w2=full["w_ff2"][l],
        g4=vec("norm_ffn_post"))


def _layer_grads(g):
    dwa = g["dwa"]
    rope_rows = slice(D_NOPE, D_NOPE + D_ROPE)
    d_kr = dwa[384:512][rope_rows] + _swap_halves(dwa[512:640][rope_rows], axis=0)
    w_in = jnp.concatenate([dwa[0:384], d_kr, g["dwm_in"]] + list(g["dwg"]), axis=0)
    dwq, dwqs = g["dwq"], g["dwqs"]
    d_rope = dwq[:, rope_rows] + _swap_halves(dwqs[:, rope_rows], axis=1)
    w_uq = jnp.concatenate([dwq[:, :D_NOPE], d_rope], axis=1).reshape(N_HEADS * (D_NOPE + D_ROPE), Q_RANK)
    w_ukv = jnp.concatenate([g["dwk"][:, :D_NOPE], g["dwv"].reshape(N_HEADS, D_V, KV_RANK)], axis=1)
    w_ukv = w_ukv.reshape(N_HEADS * (D_NOPE + D_V), KV_RANK)
    pool_w = jnp.stack([g["dwp"][i * POOL_GROUP:(i + 1) * POOL_GROUP, i * POOL_GROUP:(i + 1) * POOL_GROUP]
                        for i in range(len(POOL_WINDOWS))])
    sg_b = g["dsgb"].reshape(SG_CHUNK, SG_GROUPS, POOL_GROUP).sum(axis=-1).T
    own = _owner_major
    return dict(
        w_in=own(w_in), w_uq=own(w_uq), w_ukv=own(w_ukv), w_br_mla=own(g["dwbr"][0]), w_br_sg=own(g["dwbr"][1]),
        w_br_conv=own(g["dwbr"][2]), w_br_pool=own(g["dwbr"][3]), w_out=own(g["dwout"]), w_ff1=own(g["dw1"]),
        w_ff2=own(g["dw2"]), conv_w=own(g["dcw"][:CONV_K].T),
        norm_mix_pre=g["dg1"][0], gate_b=jnp.concatenate([b[0] for b in g["dbg"]]), q_norm=g["dqn"][0], kv_norm=g["dkvn"][0],
        sg_ln_g=g["dlng"][0], sg_ln_b=g["dlnb"][0], sg_w=g["dwm"], sg_b=sg_b, pool_w=pool_w, pool_scale=g["dps"][0],
        norm_mix_post=g["dg2"][0], norm_ffn_pre=g["dg3"][0], norm_ffn_post=g["dg4"][0])


def _rope_tables(positions):
    inv_freq = ROPE_BASE ** (-jnp.arange(0, D_ROPE, 2, dtype=F32) / D_ROPE)
    ang = positions.astype(F32)[:, None] * inv_freq
    cos, sin = jnp.cos(ang), jnp.sin(ang)
    s = positions.shape[0]
    cosf = jnp.concatenate([jnp.ones((s, D_NOPE), F32), cos, cos, jnp.zeros((s, HEAD_PAD - D_NOPE - D_ROPE), F32)], axis=1)
    sins = jnp.concatenate([jnp.zeros((s, D_NOPE), F32), -sin, sin, jnp.zeros((s, HEAD_PAD - D_NOPE - D_ROPE), F32)], axis=1)
    return cosf, sins


def _layer_fwd(x, w, cosf, sins, tiles):
    t, ta = tiles["tok"], tiles["attn"]
    hb, pa, pm = _inproj_fwd(x, w["g1"], w["w_in"], t)
    q, k, v = _attn_prep_fwd(pa, cosf, sins, w["qn"], w["kvn"], w["wq"], w["wqs"], w["wk"], w["wv"], t)
    o, lse = _flash_fwd(q, k, v, ta)
    bsg, bcv, bpl = _mixers_fwd(pm, w["ln_g"], w["ln_b"], w["wm"], w["sgb"], w["conv_w"], w["wp"], w["pscale"], t)
    x_mid, merged, tt = _merge_fwd(x, hb, (o, bsg, bcv, bpl), w["wg"], w["bg"], w["wbr"], w["wout"], w["g2"], t)
    x_out, f = _ffn_fwd(x_mid, w["g3"], w["w1"], w["w2"], w["g4"], t)
    saved = dict(x=x, hb=hb, pa=pa, pm=pm, q=q, k=k, v=v, o=o, lse=lse, bsg=bsg, bcv=bcv, bpl=bpl, x_mid=x_mid,
                 merged=merged, tt=tt, f=f)
    return x_out, saved


def _layer_bwd(dxo, a, w, cosf, sins, tiles):
    t, ta, tb = tiles["tok"], tiles["attn"], tiles["inproj_bwd"]
    g = {}
    h2, df, g["dg4"] = _ffn_bwd_norms(a["x_mid"], a["f"], dxo, w["g3"], w["g4"], t)
    da, g["dw1"], g["dw2"] = _ffn_bwd_weights(h2, df, w["w1"], w["w2"], t)
    dxmid, g["dg3"] = _ffn_bwd_input(da, w["w1"], a["x_mid"], dxo, w["g3"], t)
    dm, g["dwout"], g["dg2"] = _merge_bwd_out(a["tt"], dxmid, w["g2"], w["wout"], a["merged"], t)
    dbrs, dpres, g["dwg"], g["dbg"], g["dwbr"] = [], [], [], [], []
    for k, br in enumerate((a["o"], a["bsg"], a["bcv"], a["bpl"])):
        dbr, dpre, dwg, dbg, dwbr = _merge_bwd_branch(k, a["hb"], br, dm, w["wg"], w["bg"], w["wbr"], t)
        dbrs.append(dbr); dpres.append(dpre); g["dwg"].append(dwg); g["dbg"].append(dbg); g["dwbr"].append(dwbr)
    dpm, g["dlng"], g["dlnb"], g["dwm"], g["dsgb"], g["dcw"], g["dwp"], g["dps"] = _mixers_bwd(
        a["pm"], dbrs[1], dbrs[2], dbrs[3], w["ln_g"], w["ln_b"], w["wm"], w["wmt"], w["sgb"], w["conv_w"], w["wp"], w["pscale"], t)
    dq = _flash_bwd_dq(a["q"], a["k"], a["v"], a["o"], dbrs[0], a["lse"], ta)
    dk, dv = _flash_bwd_dkv(a["q"], a["k"], a["v"], a["o"], dbrs[0], a["lse"], ta)
    dpa, g["dwq"], g["dwqs"], g["dwk"], g["dwv"], g["dqn"], g["dkvn"] = _attn_prep_bwd(
        a["pa"], dq, dk, dv, cosf, sins, w["qn"], w["kvn"], w["wq"], w["wqs"], w["wk"], w["wv"], t)
    dx, g["dg1"], g["dwa"], g["dwm_in"] = _inproj_bwd(a["x"], w["g1"], dxmid, dpa, dpm, dpres, w["w_in"], w["wg"], tb)
    return dx, _layer_grads(g)


def _step(p, x, positions, loss_target):
    s = x.shape[0]
    depth = p["w_in"][0].shape[0]
    tiles = dict(tok=min(TOK_TILE, s), attn=min(ATTN_TILE, s), inproj_bwd=min(INPROJ_BWD_TILE, s))

    shard_arrays = [lax.bitcast_convert_type(p[n][0], BF16) if n == "conv_w" else p[n][0].astype(BF16) for n in SHARDED]
    packed, offsets = _pack_rows(shard_arrays, BF16, ROW_PAD)
    gathered = _unpack_rows(_all_gather(packed), offsets, [a.shape for a in shard_arrays])
    full = {}
    for n, gth in zip(SHARDED, gathered):
        if n == "conv_w":
            gth = lax.bitcast_convert_type(gth, F32)
        full[n] = _join_shards(n, gth)
    small = {n: p[n][0] for n in REPLICATED}

    cosf, sins = _rope_tables(positions)
    weights = [_layer_weights(full, small, l) for l in range(depth)]
    acts = []
    h = x
    for l in range(depth):
        h, saved = _layer_fwd(h, weights[l], cosf, sins, tiles)
        acts.append(saved)
    dh, loss_blk = _loss_head(h, loss_target, tiles["tok"])
    loss = lax.psum(loss_blk[0, 0], ("x", "y", "c"))
    layer_grads = [None] * depth
    shard_blocks = [None] * depth
    block_shapes = [_owner_block_shape(n, p[n][0].shape) for n in SHARDED]
    for l in reversed(range(depth)):
        dh, layer_grads[l] = _layer_bwd(dh, acts[l], weights[l], cosf, sins, tiles)
        send, send_offsets = _pack_rows([layer_grads[l][n] for n in SHARDED], BF16, GRAD_ROW_PAD, lead=(N_DEV,))
        g_rows = _sum_slots(_all_to_all(send), "grad_shard_sum")
        shard_blocks[l] = _unpack_rows(g_rows, send_offsets, block_shapes)
    g_shard = {n: jnp.stack([_natural_shard(n, shard_blocks[l][i]) for l in range(depth)]) for i, n in enumerate(SHARDED)}
    small_grads = [jnp.stack([layer_grads[l][n] for l in range(depth)]) for n in REPLICATED]
    small_rows, small_offsets = _pack_rows(small_grads, F32, ROW_PAD)
    g_small_rows = _sum_slots(_all_gather(small_rows), "grad_replicated_sum")

    grad, delta, new_m, new_v = {}, {}, {}, {}
    for n in SHARDED:
        w, m, v = p[n]
        shape2 = (math.prod(w.shape[:-1]), w.shape[-1])
        grad[n] = g_shard[n]
        d, nm, nv = _adamw(w.reshape(shape2), grad[n].reshape(shape2), m.reshape(shape2), v.reshape(shape2), f"adamw_{n}")
        delta[n], new_m[n], new_v[n] = d.reshape(w.shape), nm.reshape(w.shape), nv.reshape(w.shape)
    packs = [_pack_rows([p[n][i] for n in REPLICATED], F32, ROW_PAD)[0] for i in range(3)]
    d, nm, nv = _adamw(packs[0], g_small_rows, packs[1], packs[2], "adamw_replicated")
    shapes = [p[n][0].shape for n in REPLICATED]
    for n, gg, dd, mm, vv in zip(REPLICATED, _unpack_rows(g_small_rows, small_offsets, shapes), _unpack_rows(d, small_offsets, shapes),
                                 _unpack_rows(nm, small_offsets, shapes), _unpack_rows(nv, small_offsets, shapes)):
        grad[n], delta[n], new_m[n], new_v[n] = gg, dd, mm, vv
    return loss, dh, grad, delta, new_m, new_v


WEIGHT_ORDER = ("norm_mix_pre", "w_in", "gate_b", "q_norm", "w_uq", "kv_norm", "w_ukv", "w_br_mla", "sg_ln_g", "sg_ln_b",
                "sg_w", "sg_b", "w_br_sg", "conv_w", "w_br_conv", "pool_w", "pool_scale", "w_br_pool", "w_out",
                "norm_mix_post", "norm_ffn_pre", "w_ff1", "w_ff2", "norm_ffn_post")


def kernel(x, positions, norm_mix_pre, w_in, gate_b, q_norm, w_uq, kv_norm, w_ukv, w_br_mla, sg_ln_g, sg_ln_b, sg_w, sg_b, w_br_sg, conv_w, w_br_conv, pool_w, pool_scale, w_br_pool, w_out, norm_mix_post, norm_ffn_pre, w_ff1, w_ff2, norm_ffn_post, loss_target, m_norm_mix_pre, m_w_in, m_gate_b, m_q_norm, m_w_uq, m_kv_norm, m_w_ukv, m_w_br_mla, m_sg_ln_g, m_sg_ln_b, m_sg_w, m_sg_b, m_w_br_sg, m_conv_w, m_w_br_conv, m_pool_w, m_pool_scale, m_w_br_pool, m_w_out, m_norm_mix_post, m_norm_ffn_pre, m_w_ff1, m_w_ff2, m_norm_ffn_post, v_norm_mix_pre, v_w_in, v_gate_b, v_q_norm, v_w_uq, v_kv_norm, v_w_ukv, v_w_br_mla, v_sg_ln_g, v_sg_ln_b, v_sg_w, v_sg_b, v_w_br_sg, v_conv_w, v_w_br_conv, v_pool_w, v_pool_scale, v_w_br_pool, v_w_out, v_norm_mix_post, v_norm_ffn_pre, v_w_ff1, v_w_ff2, v_norm_ffn_post):
    ws = (norm_mix_pre, w_in, gate_b, q_norm, w_uq, kv_norm, w_ukv, w_br_mla, sg_ln_g, sg_ln_b, sg_w, sg_b, w_br_sg, conv_w,
          w_br_conv, pool_w, pool_scale, w_br_pool, w_out, norm_mix_post, norm_ffn_pre, w_ff1, w_ff2, norm_ffn_post)
    ms = (m_norm_mix_pre, m_w_in, m_gate_b, m_q_norm, m_w_uq, m_kv_norm, m_w_ukv, m_w_br_mla, m_sg_ln_g, m_sg_ln_b, m_sg_w,
          m_sg_b, m_w_br_sg, m_conv_w, m_w_br_conv, m_pool_w, m_pool_scale, m_w_br_pool, m_w_out, m_norm_mix_post,
          m_norm_ffn_pre, m_w_ff1, m_w_ff2, m_norm_ffn_post)
    vs = (v_norm_mix_pre, v_w_in, v_gate_b, v_q_norm, v_w_uq, v_kv_norm, v_w_ukv, v_w_br_mla, v_sg_ln_g, v_sg_ln_b, v_sg_w,
          v_sg_b, v_w_br_sg, v_conv_w, v_w_br_conv, v_pool_w, v_pool_scale, v_w_br_pool, v_w_out, v_norm_mix_post,
          v_norm_ffn_pre, v_w_ff1, v_w_ff2, v_norm_ffn_post)
    p = {n: (w, m, v) for n, w, m, v in zip(WEIGHT_ORDER, ws, ms, vs)}
    loss, grad_x, grad, delta, new_m, new_v = _step(p, x[0], positions[0], loss_target[0])
    return (loss, grad_x[None], *[grad[n] for n in WEIGHT_ORDER], *[delta[n] for n in WEIGHT_ORDER],
            *[new_m[n] for n in WEIGHT_ORDER], *[new_v[n] for n in WEIGHT_ORDER])
```

```python
import functools
import math

import jax
import jax.numpy as jnp
from jax import lax
from jax.experimental import pallas as pl
from jax.experimental.pallas import tpu as pltpu

F32 = jnp.float32
BF16 = jnp.bfloat16

D_MODEL = 1024
N_HEADS = 4
D_NOPE = 64
D_ROPE = 32
D_V = 64
Q_RANK = 256
KV_RANK = 128
BR_W = 256
SG_CHUNK = 128
SG_GROUPS = 4
POOL_WINDOWS = (2, 4, 8, 16)
POOL_GROUP = 64
CONV_K = 3
D_FF = 4096
N_BRANCH = 4
N_IN = 6048
EPS = 1e-6
ROPE_BASE = 10000.0
ADAM_LR = 0.001
ADAM_B1 = 0.9
ADAM_B2 = 0.999
ADAM_EPS = 1e-08
ADAM_WD = 0.01
ADAM_STEP = 10

N_DEV = 8
LANES = 128
HEAD_PAD = 128
HALO = 16
PA_W = 640
PM_W = 6 * BR_W
VMEM_LIMIT = 56 * 1024 * 1024
SM_SCALE = (D_NOPE + D_ROPE) ** -0.5
SM_SCALE_LOG2 = SM_SCALE * math.log2(math.e)
TOK_TILE = 512
ATTN_TILE = 512
INPROJ_BWD_TILE = 256

NN = (((1,), (0,)), ((), ()))
NT = (((1,), (1,)), ((), ()))
TN = (((0,), (0,)), ((), ()))
MESH_ID = pl.DeviceIdType.MESH


def _dot(a, b, dims=NN):
    return lax.dot_general(a, b, dims, preferred_element_type=F32)


def _pcall(body, *, name, grid, in_specs, out_specs, out_shape, scratch=(), vmem=None, prefetch=0):
    params = pltpu.CompilerParams(vmem_limit_bytes=vmem)
    if prefetch:
        spec = pltpu.PrefetchScalarGridSpec(num_scalar_prefetch=prefetch, grid=grid, in_specs=in_specs,
                                            out_specs=out_specs, scratch_shapes=scratch)
        return pl.pallas_call(body, name=name, grid_spec=spec, out_shape=out_shape, compiler_params=params)
    return pl.pallas_call(
        body, name=name, grid=grid, in_specs=in_specs, out_specs=out_specs, out_shape=out_shape,
        scratch_shapes=scratch, compiler_params=params)


def _rows(t, width):
    return pl.BlockSpec((t, width), lambda i: (i, 0))


def _whole(shape):
    nd = len(shape)
    return pl.BlockSpec(tuple(shape), lambda *_: (0,) * nd, pipeline_mode=pl.Buffered(1))


def _acc(shape):
    nd = len(shape)
    return pl.BlockSpec(tuple(shape), lambda *_: (0,) * nd)


def _sds(shape, dtype):
    return jax.ShapeDtypeStruct(tuple(shape), dtype)


def _rms(x, g):
    return x * lax.rsqrt(jnp.mean(x * x, axis=-1, keepdims=True) + EPS) * g


def _rms_bwd(x, g, dy):
    r = lax.rsqrt(jnp.mean(x * x, axis=-1, keepdims=True) + EPS)
    xh = x * r
    dg = jnp.sum(dy * xh, axis=0, keepdims=True)
    dxh = dy * g
    dx = r * (dxh - xh * jnp.mean(dxh * xh, axis=-1, keepdims=True))
    return dx, dg


def _sigmoid(x):
    return 1.0 / (1.0 + jnp.exp(-x))


def _gelu(x):
    return jax.nn.gelu(x, approximate=True)


def _accumulate(ref, val, first):
    @pl.when(first)
    def _():
        ref[...] = val

    @pl.when(jnp.logical_not(first))
    def _():
        ref[...] += val


def _inproj_fwd(x, g1, w_in, t):
    s = x.shape[0]

    def body(x_ref, g_ref, w_ref, hb_ref, pa_ref, pm_ref):
        hb = _rms(x_ref[...], g_ref[...]).astype(BF16)
        hb_ref[...] = hb
        pa_ref[...] = _dot(hb, w_ref[:, :PA_W])
        pm_ref[...] = _dot(hb, w_ref[:, PA_W:])

    return _pcall(
        body, name="inproj_fwd", grid=(s // t,),
        in_specs=[_rows(t, D_MODEL), _whole((1, D_MODEL)), _whole(w_in.shape)],
        out_specs=[_rows(t, D_MODEL), _rows(t, PA_W), _rows(t, PM_W)],
        out_shape=[_sds((s, D_MODEL), BF16), _sds((s, PA_W), F32), _sds((s, PM_W), F32)],
        vmem=VMEM_LIMIT)(x, g1, w_in)


def _attn_prep_fwd(pa, cosf, sins, qn, kvn, wq, wqs, wk, wv, t):
    s = pa.shape[0]

    def body(pa_ref, cos_ref, sin_ref, qn_ref, kvn_ref, wq_ref, wqs_ref, wk_ref, wv_ref, q_ref, k_ref, v_ref):
        cosv, sinv = cos_ref[...], sin_ref[...]
        cqn = _rms(pa_ref[:, 0:Q_RANK], qn_ref[...]).astype(BF16)
        ckvn = _rms(pa_ref[:, Q_RANK:Q_RANK + KV_RANK], kvn_ref[...]).astype(BF16)
        k_rope = pa_ref[:, 384:512] * cosv + pa_ref[:, 512:640] * sinv
        for h in range(N_HEADS):
            q = _dot(cqn, wq_ref[h]) * cosv + _dot(cqn, wqs_ref[h]) * sinv
            q_ref[h] = q.astype(BF16)
            k_ref[h] = (_dot(ckvn, wk_ref[h]) + k_rope).astype(BF16)
        v_ref[...] = _dot(ckvn, wv_ref[...]).astype(BF16)

    head_blk = pl.BlockSpec((N_HEADS, t, HEAD_PAD), lambda i: (0, i, 0))
    return _pcall(
        body, name="attn_prep_fwd", grid=(s // t,),
        in_specs=[_rows(t, PA_W), _rows(t, LANES), _rows(t, LANES), _whole(qn.shape), _whole(kvn.shape),
                  _whole(wq.shape), _whole(wqs.shape), _whole(wk.shape), _whole(wv.shape)],
        out_specs=[head_blk, head_blk, _rows(t, BR_W)],
        out_shape=[_sds((N_HEADS, s, HEAD_PAD), BF16), _sds((N_HEADS, s, HEAD_PAD), BF16), _sds((s, BR_W), BF16)],
        vmem=VMEM_LIMIT)(pa, cosf, sins, qn, kvn, wq, wqs, wk, wv)


def _causal_scores(q, k, masked):
    sc = _dot(q, k, NT) * SM_SCALE_LOG2
    if masked:
        row = lax.broadcasted_iota(jnp.int32, sc.shape, 0)
        col = lax.broadcasted_iota(jnp.int32, sc.shape, 1)
        sc = jnp.where(col <= row, sc, -jnp.inf)
    return sc


def _causal_steps(n, key_major):
    if key_major:
        pairs = [(qi, kj) for kj in range(n) for qi in range(kj, n)]
    else:
        pairs = [(qi, kj) for qi in range(n) for kj in range(qi + 1)]
    return (jnp.asarray([p[0] for p in pairs], jnp.int32), jnp.asarray([p[1] for p in pairs], jnp.int32))


def _flash_fwd(q, k, v, t, riding=None):
    s = v.shape[0]
    n = s // t
    reps = t // LANES
    qi_tab, kj_tab = _causal_steps(n, key_major=False)
    n_steps = int(qi_tab.shape[0])

    def body(qi_ref, kj_ref, q_ref, k_ref, v_ref, *rest):
        if riding is None:
            o_ref, lse_ref, m_scr, l_scr, acc_scr = rest
        else:
            x_ref, o_ref, lse_ref, land_ref, m_scr, l_scr, acc_scr, send_sems, recv_sems, local_sem = rest
        step_id = pl.program_id(0)
        qi, kj = qi_ref[step_id], kj_ref[step_id]
        if riding is not None:
            at_end = _ride_along(riding, step_id, n_steps, (x_ref, land_ref, send_sems, recv_sems, local_sem))

        @pl.when(kj == 0)
        def _():
            m_scr[...] = jnp.full(m_scr.shape, -jnp.inf, F32)
            l_scr[...] = jnp.zeros(l_scr.shape, F32)
            acc_scr[...] = jnp.zeros(acc_scr.shape, F32)

        def step(masked):
            for h in range(N_HEADS):
                sc = _causal_scores(q_ref[h], k_ref[h], masked)
                m_prev = m_scr[h]
                m_next = jnp.maximum(m_prev, jnp.max(sc, axis=1, keepdims=True))
                alpha = jnp.exp2(m_prev - m_next)
                p = jnp.exp2(sc - jnp.tile(m_next, (1, reps)))
                l_scr[h] = alpha * l_scr[h] + jnp.sum(p, axis=1, keepdims=True)
                m_scr[h] = m_next
                pair = (h // 2) * LANES
                acc_scr[h] = acc_scr[h] * alpha + _dot(p.astype(BF16), v_ref[:, pair:pair + LANES])

        @pl.when(kj < qi)
        def _():
            step(False)

        @pl.when(kj == qi)
        def _():
            step(True)
            lane = lax.broadcasted_iota(jnp.int32, (t, LANES), 1)
            for pr in range(N_HEADS // 2):
                o0 = acc_scr[2 * pr] / l_scr[2 * pr]
                o1 = acc_scr[2 * pr + 1] / l_scr[2 * pr + 1]
                o_ref[:, pr * LANES:(pr + 1) * LANES] = jnp.where(lane < D_V, o0, o1).astype(BF16)
            for h in range(N_HEADS):
                lse_ref[h] = m_scr[h] + jnp.log2(l_scr[h])

        if riding is not None:
            at_end()

    q_blk = pl.BlockSpec((N_HEADS, t, HEAD_PAD), lambda i, qi, kj: (0, qi[i], 0))
    k_blk = pl.BlockSpec((N_HEADS, t, HEAD_PAD), lambda i, qi, kj: (0, kj[i], 0))
    v_blk = pl.BlockSpec((t, BR_W), lambda i, qi, kj: (kj[i], 0))
    extra = riding is not None
    return _pcall(
        body, name="flash_fwd_gathering" if extra else "flash_fwd", grid=(n_steps,), prefetch=2,
        in_specs=[q_blk, k_blk, v_blk] + [HBM_REF] * extra,
        out_specs=[pl.BlockSpec((t, BR_W), lambda i, qi, kj: (qi[i], 0)), q_blk] + [HBM_REF] * extra,
        out_shape=[_sds((s, BR_W), BF16), _sds((N_HEADS, s, LANES), F32)] + ([riding[2]] if extra else []),
        scratch=[pltpu.VMEM((N_HEADS, t, LANES), F32)] * 3 + (list(EXCHANGE_SEMS) if extra else []),
        vmem=VMEM_LIMIT)(qi_tab, kj_tab, q, k, v, *([riding[1]] if extra else []))


def _shift_down(ext, k, t):
    return pltpu.roll(ext, k, 0)[HALO:HALO + t]


def _shift_up(ext, k, t):
    return pltpu.roll(ext, t + HALO - k, 0)[0:t]


def _lane_group(shape):
    return lax.shift_right_logical(lax.broadcasted_iota(jnp.int32, shape, 1), 6)


def _group_select(vals):
    grp = _lane_group(vals[0].shape)
    out = vals[0]
    for g in range(1, len(vals)):
        out = jnp.where(grp == g, vals[g], out)
    return out


def _layernorm(x, g, b):
    mu = jnp.mean(x, axis=-1, keepdims=True)
    xc = x - mu
    return xc * lax.rsqrt(jnp.mean(xc * xc, axis=-1, keepdims=True) + EPS) * g + b


def _sg_mix(wm_ref, vnb, bias):
    return _group_select([_dot(wm_ref[g], vnb) for g in range(SG_GROUPS)]) + bias


def _pool_windows(ext):
    s2 = ext + pltpu.roll(ext, 1, 0)
    s4 = s2 + pltpu.roll(s2, 2, 0)
    s8 = s4 + pltpu.roll(s4, 4, 0)
    s16 = s8 + pltpu.roll(s8, 8, 0)
    return [s2, s4, s8, s16]


def _pool_counts(tok):
    return [jnp.minimum(tok + 1, w).astype(F32) for w in POOL_WINDOWS]


def _halo_before(t):
    return pl.BlockSpec((HALO, PM_W), lambda i: (jnp.maximum(i * (t // HALO) - 1, 0), 0))


def _mixers_fwd(pm, ln_g, ln_b, wm, sgb, conv_w, wp, pscale, t):
    s = pm.shape[0]

    def body(pm_ref, halo_ref, lng_ref, lnb_ref, wm_ref, sgb_ref, cw_ref, wp_ref, ps_ref, bsg_ref, bcv_ref, bpl_ref):
        i = pl.program_id(0)
        halo = jnp.where(i > 0, halo_ref[...], 0.0)
        u = _gelu(pm_ref[:, 0:256])
        vnb = _layernorm(_gelu(pm_ref[:, 256:512]), lng_ref[...], lnb_ref[...]).astype(BF16)
        for c in range(t // SG_CHUNK):
            rows = slice(c * SG_CHUNK, (c + 1) * SG_CHUNK)
            bsg_ref[rows, :] = (u[rows] * _sg_mix(wm_ref, vnb[rows], sgb_ref[...])).astype(BF16)
        z = pm_ref[:, 1024:1280] * pm_ref[:, 512:768]
        zext = jnp.concatenate([halo[:, 1024:1280] * halo[:, 512:768], z], axis=0)
        y = cw_ref[0:1, :] * _shift_down(zext, 2, t) + cw_ref[1:2, :] * _shift_down(zext, 1, t) + cw_ref[2:3, :] * z
        bcv_ref[...] = (pm_ref[:, 768:1024] * y).astype(BF16)
        p = pm_ref[:, 1280:1536]
        sums = _pool_windows(jnp.concatenate([halo[:, 1280:1536], p], axis=0))
        tok = i * t + lax.broadcasted_iota(jnp.int32, (t, 1), 0)
        pooled = _group_select([sw[HALO:HALO + t] / cnt - p for sw, cnt in zip(sums, _pool_counts(tok))])
        bpl_ref[...] = (_dot(pooled.astype(BF16), wp_ref[...]) * ps_ref[...]).astype(BF16)

    return _pcall(
        body, name="mixers_fwd", grid=(s // t,),
        in_specs=[_rows(t, PM_W), _halo_before(t), _whole(ln_g.shape), _whole(ln_b.shape), _whole(wm.shape),
                  _whole(sgb.shape), _whole(conv_w.shape), _whole(wp.shape), _whole(pscale.shape)],
        out_specs=[_rows(t, BR_W)] * 3,
        out_shape=[_sds((s, BR_W), BF16)] * 3,
        vmem=VMEM_LIMIT)(pm, pm, ln_g, ln_b, wm, sgb, conv_w, wp, pscale)


def _merge_fwd(x, hb, branches, wg, bg, wbr, wout, g2, t):
    s = x.shape[0]

    def body(x_ref, hb_ref, b0_ref, b1_ref, b2_ref, b3_ref, wg_ref, bg_ref, wbr_ref, wout_ref, g2_ref,
             xmid_ref, mrg_ref, t_ref):
        hb = hb_ref[...]
        merged = jnp.zeros((t, D_MODEL), F32)
        for k, b_ref in enumerate((b0_ref, b1_ref, b2_ref, b3_ref)):
            cols = slice(k * D_MODEL, (k + 1) * D_MODEL)
            gate = _sigmoid(_dot(hb, wg_ref[:, cols]) + bg_ref[:, cols])
            merged = merged + gate * _dot(b_ref[...], wbr_ref[k])
        mb = merged.astype(BF16)
        mrg_ref[...] = mb
        tt = _dot(mb, wout_ref[...])
        t_ref[...] = tt
        xmid_ref[...] = x_ref[...] + _rms(tt, g2_ref[...])

    return _pcall(
        body, name="merge_fwd", grid=(s // t,),
        in_specs=[_rows(t, D_MODEL), _rows(t, D_MODEL)] + [_rows(t, BR_W)] * 4 +
                 [_whole(wg.shape), _whole(bg.shape), _whole(wbr.shape), _whole(wout.shape), _whole(g2.shape)],
        out_specs=[_rows(t, D_MODEL)] * 3,
        out_shape=[_sds((s, D_MODEL), F32), _sds((s, D_MODEL), BF16), _sds((s, D_MODEL), F32)],
        vmem=VMEM_LIMIT)(x, hb, *branches, wg, bg, wbr, wout, g2)


def _ffn_fwd(x, g3, w1, w2, g4, t):
    s = x.shape[0]

    def body(x_ref, g3_ref, w1_ref, w2_ref, g4_ref, xo_ref, f_ref):
        h = _rms(x_ref[...], g3_ref[...]).astype(BF16)
        f = jnp.zeros((t, D_MODEL), F32)
        for j in range(D_FF // D_MODEL):
            cols = slice(j * D_MODEL, (j + 1) * D_MODEL)
            r = jnp.square(jnp.maximum(_dot(h, w1_ref[:, cols]), 0.0)).astype(BF16)
            f = f + _dot(r, w2_ref[cols, :])
        f_ref[...] = f
        xo_ref[...] = x_ref[...] + _rms(f, g4_ref[...])

    return _pcall(
        body, name="ffn_fwd", grid=(s // t,),
        in_specs=[_rows(t, D_MODEL), _whole(g3.shape), _whole(w1.shape), _whole(w2.shape), _whole(g4.shape)],
        out_specs=[_rows(t, D_MODEL)] * 2,
        out_shape=[_sds((s, D_MODEL), F32)] * 2,
        vmem=VMEM_LIMIT)(x, g3, w1, w2, g4)


def _loss_head(y, target, t):
    s = y.shape[0]
    n = s // t

    def body(y_ref, tg_ref, dy_ref, loss_ref, acc_scr):
        i = pl.program_id(0)
        d = y_ref[...] - tg_ref[...]
        dy_ref[...] = d * (1.0 / D_MODEL)
        _accumulate(acc_scr, jnp.sum(d * d, axis=0, keepdims=True), i == 0)

        @pl.when(i == n - 1)
        def _():
            loss_ref[...] = jnp.full(loss_ref.shape, 0.5 / D_MODEL, F32) * jnp.sum(acc_scr[...])

    return _pcall(
        body, name="loss_head", grid=(n,),
        in_specs=[_rows(t, D_MODEL)] * 2,
        out_specs=[_rows(t, D_MODEL), _acc((8, LANES))],
        out_shape=[_sds((s, D_MODEL), F32), _sds((8, LANES), F32)],
        scratch=[pltpu.VMEM((1, D_MODEL), F32)])(y, target)


def _ffn_bwd_norms(x_mid, f, dxo, g3, g4, t):
    s = x_mid.shape[0]

    def body(x_ref, f_ref, dxo_ref, g3_ref, g4_ref, h2_ref, df_ref, dg4_ref):
        h2_ref[...] = _rms(x_ref[...], g3_ref[...]).astype(BF16)
        df, dg4 = _rms_bwd(f_ref[...], g4_ref[...], dxo_ref[...])
        df_ref[...] = df.astype(BF16)
        _accumulate(dg4_ref, dg4, pl.program_id(0) == 0)

    return _pcall(
        body, name="ffn_bwd_norms", grid=(s // t,),
        in_specs=[_rows(t, D_MODEL)] * 3 + [_whole(g3.shape), _whole(g4.shape)],
        out_specs=[_rows(t, D_MODEL), _rows(t, D_MODEL), _acc((1, D_MODEL))],
        out_shape=[_sds((s, D_MODEL), BF16), _sds((s, D_MODEL), BF16), _sds((1, D_MODEL), F32)])(x_mid, f, dxo, g3, g4)


def _ffn_bwd_weights(h2, df, w1, w2, t):
    s = h2.shape[0]
    blk = D_MODEL

    def body(h2_ref, df_ref, w1_ref, w2_ref, da_ref, dw1_ref, dw2_ref):
        first = pl.program_id(1) == 0
        h2v, dfv = h2_ref[...], df_ref[...]
        rl = jnp.maximum(_dot(h2v, w1_ref[...]), 0.0)
        _accumulate(dw2_ref, _dot((rl * rl).astype(BF16), dfv, TN), first)
        da = (_dot(dfv, w2_ref[...], NT) * (2.0 * rl)).astype(BF16)
        da_ref[...] = da
        _accumulate(dw1_ref, _dot(da, h2v, TN), first)

    tok = pl.BlockSpec((t, D_MODEL), lambda j, i: (i, 0))
    hid = pl.BlockSpec((blk, D_MODEL), lambda j, i: (j, 0))
    return _pcall(
        body, name="ffn_bwd_weights", grid=(D_FF // blk, s // t),
        in_specs=[tok, tok, pl.BlockSpec((D_MODEL, blk), lambda j, i: (0, j)), hid],
        out_specs=[pl.BlockSpec((t, blk), lambda j, i: (i, j)), hid, hid],
        out_shape=[_sds((s, D_FF), BF16), _sds((D_FF, D_MODEL), F32), _sds((D_FF, D_MODEL), F32)],
        vmem=VMEM_LIMIT)(h2, df, w1, w2)


def _ffn_bwd_input(da, w1, x_mid, dxo, g3, t):
    s = x_mid.shape[0]

    def body(da_ref, w1_ref, x_ref, dxo_ref, g3_ref, dx_ref, dg3_ref):
        dx, dg3 = _rms_bwd(x_ref[...], g3_ref[...], _dot(da_ref[...], w1_ref[...], NT))
        dx_ref[...] = dxo_ref[...] + dx
        _accumulate(dg3_ref, dg3, pl.program_id(0) == 0)

    return _pcall(
        body, name="ffn_bwd_input", grid=(s // t,),
        in_specs=[_rows(t, D_FF), _whole(w1.shape), _rows(t, D_MODEL), _rows(t, D_MODEL), _whole(g3.shape)],
        out_specs=[_rows(t, D_MODEL), _acc((1, D_MODEL))],
        out_shape=[_sds((s, D_MODEL), F32), _sds((1, D_MODEL), F32)],
        vmem=VMEM_LIMIT)(da, w1, x_mid, dxo, g3)


def _merge_bwd_out(tt, dxmid, g2, wout, merged, t):
    s = tt.shape[0]

    def body(t_ref, dx_ref, g2_ref, wout_ref, mrg_ref, dm_ref, dwout_ref, dg2_ref):
        first = pl.program_id(0) == 0
        dt, dg2 = _rms_bwd(t_ref[...], g2_ref[...], dx_ref[...])
        dtb = dt.astype(BF16)
        dm_ref[...] = _dot(dtb, wout_ref[...], NT)
        _accumulate(dwout_ref, _dot(mrg_ref[...], dtb, TN), first)
        _accumulate(dg2_ref, dg2, first)

    return _pcall(
        body, name="merge_bwd_out", grid=(s // t,),
        in_specs=[_rows(t, D_MODEL), _rows(t, D_MODEL), _whole(g2.shape), _whole(wout.shape), _rows(t, D_MODEL)],
        out_specs=[_rows(t, D_MODEL), _acc((D_MODEL, D_MODEL)), _acc((1, D_MODEL))],
        out_shape=[_sds((s, D_MODEL), F32), _sds((D_MODEL, D_MODEL), F32), _sds((1, D_MODEL), F32)],
        vmem=VMEM_LIMIT)(tt, dxmid, g2, wout, merged)


def _merge_bwd_branch(k, hb, br, dm, wg, bg, wbr, t):
    s = hb.shape[0]

    def body(hb_ref, br_ref, dm_ref, wg_ref, bg_ref, wbr_ref, dbr_ref, dpre_ref, dwg_ref, dbg_ref, dwbr_ref):
        first = pl.program_id(0) == 0
        hbv, brv, dmv, wbrv = hb_ref[...], br_ref[...], dm_ref[...], wbr_ref[0]
        gate = _sigmoid(_dot(hbv, wg_ref[...]) + bg_ref[...])
        dy = (dmv * gate).astype(BF16)
        dpre = dmv * _dot(brv, wbrv) * gate * (1.0 - gate)
        dpb = dpre.astype(BF16)
        dpre_ref[...] = dpb
        dbr_ref[...] = _dot(dy, wbrv, NT)
        _accumulate(dwbr_ref, _dot(dy, brv, TN), first)
        _accumulate(dwg_ref, _dot(dpb, hbv, TN), first)
        _accumulate(dbg_ref, jnp.sum(dpre, axis=0, keepdims=True), first)

    return _pcall(
        body, name=f"merge_bwd_branch{k}", grid=(s // t,),
        in_specs=[_rows(t, D_MODEL), _rows(t, BR_W), _rows(t, D_MODEL),
                  pl.BlockSpec((D_MODEL, D_MODEL), lambda i: (0, k)), pl.BlockSpec((1, D_MODEL), lambda i: (0, k)),
                  pl.BlockSpec((1, BR_W, D_MODEL), lambda i: (k, 0, 0))],
        out_specs=[_rows(t, BR_W), _rows(t, D_MODEL), _acc((D_MODEL, D_MODEL)), _acc((1, D_MODEL)), _acc((D_MODEL, BR_W))],
        out_shape=[_sds((s, BR_W), F32), _sds((s, D_MODEL), BF16), _sds((D_MODEL, D_MODEL), F32),
                   _sds((1, D_MODEL), F32), _sds((D_MODEL, BR_W), F32)],
        vmem=VMEM_LIMIT)(hb, br, dm, wg, bg, wbr)


def _mixers_bwd(pm, dbsg, dbcv, dbpl, ln_g, ln_b, wm, wmt, sgb, conv_w, wp, pscale, t):
    s = pm.shape[0]
    n = s // t
    nb = t // HALO

    def body(pm_ref, before_ref, after_ref, dsg_ref, dcv_ref, dcva_ref, dpl_ref, dpla_ref, lng_ref, lnb_ref,
             wm_ref, wmt_ref, sgb_ref, cw_ref, wp_ref, ps_ref,
             dpm_ref, dlng_ref, dlnb_ref, dwm_ref, dsgb_ref, dcw_ref, dwp_ref, dps_ref):
        i = pl.program_id(0)
        first = i == 0
        before = jnp.where(i > 0, before_ref[...], 0.0)
        after = jnp.where(i < n - 1, after_ref[...], 0.0)

        u_raw, v_raw = pm_ref[:, 0:256], pm_ref[:, 256:512]
        lng, lnb = lng_ref[...], lnb_ref[...]
        u, gelu_u_vjp = jax.vjp(_gelu, u_raw)
        vn, norm_vjp = jax.vjp(lambda v_, g_, b_: _layernorm(_gelu(v_), g_, b_), v_raw, lng, lnb)
        vnb = vn.astype(BF16)
        dsg = dsg_ref[...]
        grp = _lane_group((SG_CHUNK, BR_W))
        tri = (lax.broadcasted_iota(jnp.int32, (SG_CHUNK, SG_CHUNK), 1)
               <= lax.broadcasted_iota(jnp.int32, (SG_CHUNK, SG_CHUNK), 0))
        du_parts, dvn_parts = [], []
        dsgb = jnp.zeros((SG_CHUNK, BR_W), F32)
        dwm = [jnp.zeros((SG_CHUNK, SG_CHUNK), F32) for _ in range(SG_GROUPS)]
        for c in range(t // SG_CHUNK):
            rows = slice(c * SG_CHUNK, (c + 1) * SG_CHUNK)
            mix = _sg_mix(wm_ref, vnb[rows], sgb_ref[...])
            du_parts.append(dsg[rows] * mix)
            ds = dsg[rows] * u[rows]
            dsgb = dsgb + ds
            dsb = [jnp.where(grp == g, ds, 0.0).astype(BF16) for g in range(SG_GROUPS)]
            for g in range(SG_GROUPS):
                dwm[g] = dwm[g] + _dot(dsb[g], vnb[rows], NT)
            dvn_parts.append(_group_select([_dot(wmt_ref[g], dsb[g]) for g in range(SG_GROUPS)]))
        (du_raw,) = gelu_u_vjp(jnp.concatenate(du_parts, axis=0))
        dv_raw, dlng, dlnb = norm_vjp(jnp.concatenate(dvn_parts, axis=0))
        dpm_ref[:, 0:256] = du_raw.astype(BF16)
        dpm_ref[:, 256:512] = dv_raw.astype(BF16)
        _accumulate(dlng_ref, dlng, first)
        _accumulate(dlnb_ref, dlnb, first)
        _accumulate(dsgb_ref, dsgb, first)
        for g in range(SG_GROUPS):
            _accumulate(dwm_ref.at[g], jnp.where(tri, dwm[g], 0.0), first)

        xin, bg, cg = pm_ref[:, 512:768], pm_ref[:, 768:1024], pm_ref[:, 1024:1280]
        z = cg * xin
        zext = jnp.concatenate([before[:, 1024:1280] * before[:, 512:768], z], axis=0)
        z1, z2 = _shift_down(zext, 1, t), _shift_down(zext, 2, t)
        w0, w1, w2 = cw_ref[0:1, :], cw_ref[1:2, :], cw_ref[2:3, :]
        dcv = dcv_ref[...]
        y = w0 * z2 + w1 * z1 + w2 * z
        dy = dcv * bg
        dyext = jnp.concatenate([dy, jnp.where(i < n - 1, dcva_ref[...], 0.0) * after[:, 768:1024]], axis=0)
        dz = w2 * dy + w1 * _shift_up(dyext, 1, t) + w0 * _shift_up(dyext, 2, t)
        dpm_ref[:, 512:768] = (dz * cg).astype(BF16)
        dpm_ref[:, 768:1024] = (dcv * y).astype(BF16)
        dpm_ref[:, 1024:1280] = (dz * xin).astype(BF16)
        dcw = jnp.concatenate([jnp.sum(dy * z2, axis=0, keepdims=True), jnp.sum(dy * z1, axis=0, keepdims=True),
                               jnp.sum(dy * z, axis=0, keepdims=True)], axis=0)
        _accumulate(dcw_ref, jnp.concatenate([dcw, jnp.zeros((8 - CONV_K, BR_W), F32)], axis=0), first)

        p = pm_ref[:, 1280:1536]
        tok = i * t + lax.broadcasted_iota(jnp.int32, (t, 1), 0)
        sums = _pool_windows(jnp.concatenate([before[:, 1280:1536], p], axis=0))
        pooled = _group_select([sw[HALO:HALO + t] / cnt - p for sw, cnt in zip(sums, _pool_counts(tok))]).astype(BF16)
        mixed = _dot(pooled, wp_ref[...])
        dpl = dpl_ref[...]
        ps = ps_ref[...]
        dmix = (dpl * ps).astype(BF16)
        _accumulate(dps_ref, jnp.sum(dpl * mixed, axis=0, keepdims=True), first)
        _accumulate(dwp_ref, _dot(pooled, dmix, TN), first)
        dmix_after = (jnp.where(i < n - 1, dpla_ref[...], 0.0) * ps).astype(BF16)
        dpo = _dot(dmix, wp_ref[...], NT)
        dpo_ext = jnp.concatenate([dpo, _dot(dmix_after, wp_ref[...], NT)], axis=0)
        tok_ext = i * t + lax.broadcasted_iota(jnp.int32, (t + HALO, 1), 0)
        dp_groups = []
        for g, (win, cnt) in enumerate(zip(POOL_WINDOWS, _pool_counts(tok_ext))):
            e = dpo_ext / cnt
            acc = e
            span = 1
            while span < win:
                acc = acc + pltpu.roll(acc, t + HALO - span, 0)
                span *= 2
            dp_groups.append(acc[0:t] - dpo)
        dpm_ref[:, 1280:1536] = _group_select(dp_groups).astype(BF16)

    row_blk = lambda w: pl.BlockSpec((t, w), lambda i: (i, 0))
    after_blk = lambda w: pl.BlockSpec((HALO, w), lambda i: (jnp.minimum((i + 1) * nb, n * nb - 1), 0))
    return _pcall(
        body, name="mixers_bwd", grid=(n,),
        in_specs=[row_blk(PM_W), _halo_before(t), after_blk(PM_W), row_blk(BR_W), row_blk(BR_W), after_blk(BR_W),
                  row_blk(BR_W), after_blk(BR_W), _whole(ln_g.shape), _whole(ln_b.shape), _whole(wm.shape),
                  _whole(wmt.shape), _whole(sgb.shape), _whole(conv_w.shape), _whole(wp.shape), _whole(pscale.shape)],
        out_specs=[row_blk(PM_W), _acc((1, BR_W)), _acc((1, BR_W)), _acc((SG_GROUPS, SG_CHUNK, SG_CHUNK)),
                   _acc((SG_CHUNK, BR_W)), _acc((8, BR_W)), _acc((BR_W, BR_W)), _acc((1, BR_W))],
        out_shape=[_sds((s, PM_W), BF16), _sds((1, BR_W), F32), _sds((1, BR_W), F32),
                   _sds((SG_GROUPS, SG_CHUNK, SG_CHUNK), F32), _sds((SG_CHUNK, BR_W), F32), _sds((8, BR_W), F32),
                   _sds((BR_W, BR_W), F32), _sds((1, BR_W), F32)],
        vmem=VMEM_LIMIT)(pm, pm, pm, dbsg, dbcv, dbcv, dbpl, dbpl, ln_g, ln_b, wm, wmt, sgb, conv_w, wp, pscale)


def _head_delta(o_ref, do_ref, t):
    prod = o_ref[...].astype(F32) * do_ref[...]
    lane = lax.broadcasted_iota(jnp.int32, (t, LANES), 1)
    out = []
    for h in range(N_HEADS):
        pair = prod[:, (h // 2) * LANES:(h // 2 + 1) * LANES]
        mine = (lane < D_V) if h % 2 == 0 else (lane >= D_V)
        out.append(jnp.broadcast_to(jnp.sum(jnp.where(mine, pair, 0.0), axis=1, keepdims=True), (t, LANES)))
    return out


def _head_do(do_ref, h, t):
    lane = lax.broadcasted_iota(jnp.int32, (t, LANES), 1)
    mine = (lane < D_V) if h % 2 == 0 else (lane >= D_V)
    pair = do_ref[:, (h // 2) * LANES:(h // 2 + 1) * LANES]
    return jnp.where(mine, pair, 0.0).astype(BF16)


def _flash_bwd_dq(q, k, v, o, do, lse, t):
    s = v.shape[0]
    n = s // t
    reps = t // LANES

    qi_tab, kj_tab = _causal_steps(n, key_major=False)

    def body(qi_ref, kj_ref, q_ref, k_ref, v_ref, o_ref, do_ref, lse_ref, dq_ref, delta_scr):
        step_id = pl.program_id(0)
        qi, kj = qi_ref[step_id], kj_ref[step_id]

        @pl.when(kj == 0)
        def _():
            dq_ref[...] = jnp.zeros(dq_ref.shape, F32)
            for h, dlt in enumerate(_head_delta(o_ref, do_ref, t)):
                delta_scr[h] = dlt

        def step(masked):
            for h in range(N_HEADS):
                kh = k_ref[h]
                p = jnp.exp2(_causal_scores(q_ref[h], kh, masked) - jnp.tile(lse_ref[h], (1, reps)))
                pair = (h // 2) * LANES
                dp = _dot(_head_do(do_ref, h, t), v_ref[:, pair:pair + LANES], NT)
                ds = p * (dp - jnp.tile(delta_scr[h], (1, reps))) * SM_SCALE
                dq_ref[h] += _dot(ds.astype(BF16), kh)

        @pl.when(kj < qi)
        def _():
            step(False)

        @pl.when(kj == qi)
        def _():
            step(True)

    q_blk = pl.BlockSpec((N_HEADS, t, HEAD_PAD), lambda i, qi, kj: (0, qi[i], 0))
    k_blk = pl.BlockSpec((N_HEADS, t, HEAD_PAD), lambda i, qi, kj: (0, kj[i], 0))
    v_blk = pl.BlockSpec((t, BR_W), lambda i, qi, kj: (kj[i], 0))
    o_blk = pl.BlockSpec((t, BR_W), lambda i, qi, kj: (qi[i], 0))
    return _pcall(
        body, name="flash_bwd_dq", grid=(int(qi_tab.shape[0]),), prefetch=2,
        in_specs=[q_blk, k_blk, v_blk, o_blk, o_blk, q_blk],
        out_specs=q_blk,
        out_shape=_sds((N_HEADS, s, HEAD_PAD), F32),
        scratch=[pltpu.VMEM((N_HEADS, t, LANES), F32)],
        vmem=VMEM_LIMIT)(qi_tab, kj_tab, q, k, v, o, do, lse)


def _flash_bwd_dkv(q, k, v, o, do, lse, t, riding=None):
    s = v.shape[0]
    n = s // t
    reps = t // LANES

    qi_tab, kj_tab = _causal_steps(n, key_major=True)
    n_steps = int(qi_tab.shape[0])

    def body(qi_ref, kj_ref, q_ref, k_ref, v_ref, o_ref, do_ref, lse_ref, *rest):
        if riding is None:
            dk_ref, dv_ref, dv_scr = rest
        else:
            x_ref, dk_ref, dv_ref, land_ref, dv_scr, send_sems, recv_sems, local_sem = rest
        step_id = pl.program_id(0)
        qi, kj = qi_ref[step_id], kj_ref[step_id]
        if riding is not None:
            at_end = _ride_along(riding, step_id, n_steps, (x_ref, land_ref, send_sems, recv_sems, local_sem))

        @pl.when(qi == kj)
        def _():
            dk_ref[...] = jnp.zeros(dk_ref.shape, F32)
            dv_scr[...] = jnp.zeros(dv_scr.shape, F32)

        def step(masked):
            delta = _head_delta(o_ref, do_ref, t)
            for h in range(N_HEADS):
                qh = q_ref[h]
                p = jnp.exp2(_causal_scores(qh, k_ref[h], masked) - jnp.tile(lse_ref[h], (1, reps)))
                pair = slice((h // 2) * LANES, (h // 2 + 1) * LANES)
                dv_scr[h] += _dot(p.astype(BF16), do_ref[:, pair].astype(BF16), TN)
                dp = _dot(_head_do(do_ref, h, t), v_ref[:, pair], NT)
                ds = p * (dp - jnp.tile(delta[h], (1, reps))) * SM_SCALE
                dk_ref[h] += _dot(ds.astype(BF16), qh, TN)

        @pl.when(qi > kj)
        def _():
            step(False)

        @pl.when(qi == kj)
        def _():
            step(True)

        @pl.when(qi == n - 1)
        def _():
            lane = lax.broadcasted_iota(jnp.int32, (t, LANES), 1)
            for pr in range(N_HEADS // 2):
                dv_ref[:, pr * LANES:(pr + 1) * LANES] = jnp.where(lane < D_V, dv_scr[2 * pr], dv_scr[2 * pr + 1]).astype(BF16)

        if riding is not None:
            at_end()

    q_blk = pl.BlockSpec((N_HEADS, t, HEAD_PAD), lambda i, qi, kj: (0, qi[i], 0))
    k_blk = pl.BlockSpec((N_HEADS, t, HEAD_PAD), lambda i, qi, kj: (0, kj[i], 0))
    v_blk = pl.BlockSpec((t, BR_W), lambda i, qi, kj: (kj[i], 0))
    o_blk = pl.BlockSpec((t, BR_W), lambda i, qi, kj: (qi[i], 0))
    extra = riding is not None
    return _pcall(
        body, name="flash_bwd_dkv_exchanging" if extra else "flash_bwd_dkv", grid=(n_steps,), prefetch=2,
        in_specs=[q_blk, k_blk, v_blk, o_blk, o_blk, q_blk] + [HBM_REF] * extra,
        out_specs=[k_blk, v_blk] + [HBM_REF] * extra,
        out_shape=[_sds((N_HEADS, s, HEAD_PAD), F32), _sds((s, BR_W), BF16)] + ([riding[2]] if extra else []),
        scratch=[pltpu.VMEM((N_HEADS, t, LANES), F32)] + (list(EXCHANGE_SEMS) if extra else []),
        vmem=VMEM_LIMIT)(qi_tab, kj_tab, q, k, v, o, do, lse, *([riding[1]] if extra else []))


def _attn_prep_bwd(pa, dq, dk, dv, cosf, sins, qn, kvn, wq, wqs, wk, wv, t):
    s = pa.shape[0]

    def body(pa_ref, dq_ref, dk_ref, dv_ref, cos_ref, sin_ref, qn_ref, kvn_ref, wq_ref, wqs_ref, wk_ref, wv_ref,
             dpa_ref, dwq_ref, dwqs_ref, dwk_ref, dwv_ref, dqn_ref, dkvn_ref):
        first = pl.program_id(0) == 0
        cosv, sinv = cos_ref[...], sin_ref[...]
        cq, ckv = pa_ref[:, 0:Q_RANK], pa_ref[:, Q_RANK:Q_RANK + KV_RANK]
        cqn = _rms(cq, qn_ref[...]).astype(BF16)
        ckvn = _rms(ckv, kvn_ref[...]).astype(BF16)
        dcqn = jnp.zeros((t, Q_RANK), F32)
        dvv = dv_ref[...]
        dckvn = _dot(dvv, wv_ref[...], NT)
        _accumulate(dwv_ref, _dot(dvv, ckvn, TN), first)
        dk_rope = jnp.zeros((t, HEAD_PAD), F32)
        for h in range(N_HEADS):
            dqh = dq_ref[h]
            dqa, dqs = (dqh * cosv).astype(BF16), (dqh * sinv).astype(BF16)
            _accumulate(dwq_ref.at[h], _dot(dqa, cqn, TN), first)
            _accumulate(dwqs_ref.at[h], _dot(dqs, cqn, TN), first)
            dcqn = dcqn + _dot(dqa, wq_ref[h], NT) + _dot(dqs, wqs_ref[h], NT)
            dkh = dk_ref[h]
            dkb = dkh.astype(BF16)
            _accumulate(dwk_ref.at[h], _dot(dkb, ckvn, TN), first)
            dckvn = dckvn + _dot(dkb, wk_ref[h], NT)
            dk_rope = dk_rope + dkh
        dcq, dqn = _rms_bwd(cq, qn_ref[...], dcqn)
        dckv, dkvn = _rms_bwd(ckv, kvn_ref[...], dckvn)
        _accumulate(dqn_ref, dqn, first)
        _accumulate(dkvn_ref, dkvn, first)
        dpa_ref[:, 0:Q_RANK] = dcq.astype(BF16)
        dpa_ref[:, Q_RANK:Q_RANK + KV_RANK] = dckv.astype(BF16)
        dpa_ref[:, 384:512] = (dk_rope * cosv).astype(BF16)
        dpa_ref[:, 512:640] = (dk_rope * sinv).astype(BF16)

    head_blk = pl.BlockSpec((N_HEADS, t, HEAD_PAD), lambda i: (0, i, 0))
    wq_t, wk_t, wv_t = (N_HEADS, HEAD_PAD, Q_RANK), (N_HEADS, HEAD_PAD, KV_RANK), (N_HEADS * D_V, KV_RANK)
    return _pcall(
        body, name="attn_prep_bwd", grid=(s // t,),
        in_specs=[_rows(t, PA_W), head_blk, head_blk, _rows(t, BR_W), _rows(t, LANES), _rows(t, LANES),
                  _whole(qn.shape), _whole(kvn.shape), _whole(wq.shape), _whole(wqs.shape), _whole(wk.shape), _whole(wv.shape)],
        out_specs=[_rows(t, PA_W), _acc(wq_t), _acc(wq_t), _acc(wk_t), _acc(wv_t), _acc(qn.shape), _acc(kvn.shape)],
        out_shape=[_sds((s, PA_W), BF16), _sds(wq_t, F32), _sds(wq_t, F32), _sds(wk_t, F32),
                   _sds(wv_t, F32), _sds(qn.shape, F32), _sds(kvn.shape, F32)],
        vmem=VMEM_LIMIT)(pa, dq, dk, dv, cosf, sins, qn, kvn, wq, wqs, wk, wv)


def _inproj_bwd(x, g1, dxmid, dpa, dpm, dpres, w_in, wg, t):
    s = x.shape[0]

    def body(x_ref, g1_ref, dxm_ref, dpa_ref, dpm_ref, d0_ref, d1_ref, d2_ref, d3_ref, w_ref, wg_ref,
             dx_ref, dg1_ref, dwa_ref, dwm_ref):
        first = pl.program_id(0) == 0
        xv, g1v = x_ref[...], g1_ref[...]
        hb = _rms(xv, g1v).astype(BF16)
        dpav, dpmv = dpa_ref[...], dpm_ref[...]
        dh = _dot(dpav, w_ref[:, :PA_W], NT) + _dot(dpmv, w_ref[:, PA_W:], NT)
        for k, d_ref in enumerate((d0_ref, d1_ref, d2_ref, d3_ref)):
            dh = dh + _dot(d_ref[...], wg_ref[:, k * D_MODEL:(k + 1) * D_MODEL], NT)
        dx, dg1 = _rms_bwd(xv, g1v, dh)
        dx_ref[...] = dxm_ref[...] + dx
        _accumulate(dg1_ref, dg1, first)
        _accumulate(dwa_ref, _dot(dpav, hb, TN), first)
        _accumulate(dwm_ref, _dot(dpmv, hb, TN), first)

    return _pcall(
        body, name="inproj_bwd", grid=(s // t,),
        in_specs=[_rows(t, D_MODEL), _whole(g1.shape), _rows(t, D_MODEL), _rows(t, PA_W), _rows(t, PM_W)] +
                 [_rows(t, D_MODEL)] * 4 + [_whole(w_in.shape), _whole(wg.shape)],
        out_specs=[_rows(t, D_MODEL), _acc((1, D_MODEL)), _acc((PA_W, D_MODEL)), _acc((PM_W, D_MODEL))],
        out_shape=[_sds((s, D_MODEL), F32), _sds((1, D_MODEL), F32), _sds((PA_W, D_MODEL), F32), _sds((PM_W, D_MODEL), F32)],
        vmem=VMEM_LIMIT)(x, g1, dxmid, dpa, dpm, *dpres, w_in, wg)


def _my_place():
    return lax.axis_index("x"), lax.axis_index("y"), lax.axis_index("c")


def _flip(place, k):
    x, y, c = place
    return (1 - x if k & 4 else x, 1 - y if k & 2 else y, 1 - c if k & 1 else c)


def _rank(place):
    x, y, c = place
    return 4 * x + 2 * y + c


EXCHANGE_SEMS = (pltpu.SemaphoreType.DMA((7,)), pltpu.SemaphoreType.DMA((7,)), pltpu.SemaphoreType.DMA)
HBM_REF = pl.BlockSpec(memory_space=pl.ANY)


def _gather_plan(x_ref, out_ref, send_sems, recv_sems, local_sem):
    me = _my_place()
    sibling = _flip(me, 1)
    chips = (4, 2, 6)

    def copy(k, src_place, to, src=None):
        slot = out_ref.at[_rank(src_place)]
        return pltpu.make_async_remote_copy(
            src_ref=slot if src is None else src, dst_ref=slot, send_sem=send_sems.at[k], recv_sem=recv_sems.at[k],
            device_id=to, device_id_type=MESH_ID)

    mine = pltpu.make_async_copy(x_ref, out_ref.at[_rank(me)], local_sem)
    first = [copy(0, me, sibling, src=x_ref)] + [copy(1 + j, me, _flip(me, kc), src=x_ref) for j, kc in enumerate(chips)]
    passed = [copy(4 + j, _flip(me, kc), sibling) for j, kc in enumerate(chips)]

    def start():
        mine.start()
        for cp in first:
            cp.start()

    def forward():
        for j, kc in enumerate(chips):
            copy(1 + j, _flip(me, kc), me).wait_recv()
            passed[j].start()

    def finish():
        copy(0, sibling, me).wait_recv()
        for j, kc in enumerate(chips):
            copy(4 + j, _flip(sibling, kc), me).wait_recv()
        for cp in first + passed:
            cp.wait_send()
        mine.wait()

    return start, forward, finish


def _exchange_plan(x_ref, out_ref, send_sems, recv_sems, local_sem):
    me = _my_place()
    my_rank = _rank(me)
    mine = pltpu.make_async_copy(x_ref.at[my_rank], out_ref.at[my_rank], local_sem)
    sends = [pltpu.make_async_remote_copy(
        src_ref=x_ref.at[_rank(_flip(me, k))], dst_ref=out_ref.at[my_rank], send_sem=send_sems.at[k - 1],
        recv_sem=recv_sems.at[k - 1], device_id=_flip(me, k), device_id_type=MESH_ID) for k in range(1, N_DEV)]

    def start():
        mine.start()
        for cp in sends:
            cp.start()

    def forward():
        pass

    def finish():
        for k in range(1, N_DEV):
            slot = out_ref.at[_rank(_flip(me, k))]
            pltpu.make_async_remote_copy(
                src_ref=slot, dst_ref=slot, send_sem=send_sems.at[k - 1], recv_sem=recv_sems.at[k - 1],
                device_id=_flip(me, k), device_id_type=MESH_ID).wait_recv()
        for cp in sends:
            cp.wait_send()
        mine.wait()

    return start, forward, finish


def _exchange_alone(plan, src, out_sds, name):
    def body(x_ref, out_ref, send_sems, recv_sems, local_sem):
        start, forward, finish = plan(x_ref, out_ref, send_sems, recv_sems, local_sem)
        start()
        forward()
        finish()

    return pl.pallas_call(body, name=name, out_shape=out_sds, in_specs=[HBM_REF], out_specs=HBM_REF,
                          scratch_shapes=list(EXCHANGE_SEMS))(src)


def _all_gather(shard, name):
    return _exchange_alone(_gather_plan, shard, _sds((N_DEV,) + shard.shape, shard.dtype), name)


def _all_to_all(blocks, name):
    return _exchange_alone(_exchange_plan, blocks, _sds(blocks.shape, blocks.dtype), name)


def _riding(plan, src, out_sds):
    return (plan, src, out_sds)


def _ride_along(riding, step_id, n_steps, refs):
    start, forward, finish = riding[0](*refs)

    @pl.when(step_id == 0)
    def _():
        start()

    @pl.when(step_id == (3 * n_steps) // 4)
    def _():
        forward()

    def at_end():
        @pl.when(step_id == n_steps - 1)
        def _():
            finish()

    return at_end


def _sum_slots(parts, name):
    _, r, c = parts.shape
    t = math.gcd(r, ROW_PAD)

    def body(p_ref, o_ref):
        acc = p_ref[0].astype(F32)
        for d in range(1, N_DEV):
            acc = acc + p_ref[d].astype(F32)
        o_ref[...] = acc

    return _pcall(
        body, name=name, grid=(r // t,),
        in_specs=[pl.BlockSpec((N_DEV, t, c), lambda i: (0, i, 0))], out_specs=_rows(t, c),
        out_shape=_sds((r, c), F32))(parts)


def _adamw(w, g, m, v, name):
    r, c = w.shape
    t = r if r <= ROW_PAD else math.gcd(r, ROW_PAD)

    def body(w_ref, g_ref, m_ref, v_ref, d_ref, nm_ref, nv_ref):
        gv = g_ref[...]
        nm = ADAM_B1 * m_ref[...] + (1.0 - ADAM_B1) * gv
        nv = ADAM_B2 * v_ref[...] + (1.0 - ADAM_B2) * jnp.square(gv)
        m_hat = nm / (1.0 - ADAM_B1 ** ADAM_STEP)
        v_hat = nv / (1.0 - ADAM_B2 ** ADAM_STEP)
        d_ref[...] = -ADAM_LR * (m_hat / (jnp.sqrt(v_hat) + ADAM_EPS) + ADAM_WD * w_ref[...])
        nm_ref[...] = nm
        nv_ref[...] = nv

    return _pcall(
        body, name=name, grid=(r // t,),
        in_specs=[_rows(t, c)] * 4, out_specs=[_rows(t, c)] * 3, out_shape=[_sds((r, c), F32)] * 3)(w, g, m, v)


SHARDED = ("w_in", "w_uq", "w_ukv", "w_br_mla", "w_br_sg", "w_br_conv", "w_br_pool", "w_out", "w_ff1", "w_ff2", "conv_w")
ROW_SHARDED = ("w_out", "w_ff2")
REPLICATED = ("norm_mix_pre", "gate_b", "q_norm", "kv_norm", "sg_ln_g", "sg_ln_b", "sg_w", "sg_b", "pool_w",
              "pool_scale", "norm_mix_post", "norm_ffn_pre", "norm_ffn_post")
ROW_PAD = 256
GRAD_ROW_PAD = 64
PART_ROWS = 16


def _pack_rows(arrays, dtype, multiple, lead=()):
    rows, offsets, at = [], [], 0
    zero_pad = ((0, 0),) * len(lead)
    for a in arrays:
        flat = a.reshape(lead + (-1,)).astype(dtype)
        n = -(-flat.shape[-1] // D_MODEL)
        n_pad = -(-n // PART_ROWS) * PART_ROWS
        flat = jnp.pad(flat, zero_pad + ((0, n_pad * D_MODEL - flat.shape[-1]),))
        rows.append(flat.reshape(lead + (n_pad, D_MODEL)))
        offsets.append((at, n))
        at += n_pad
    pad = -at % multiple
    if pad:
        rows.append(jnp.zeros(lead + (pad, D_MODEL), dtype))
    return jnp.concatenate(rows, axis=len(lead)), offsets


def _unpack_rows(buf, offsets, shapes):
    lead = buf.shape[:-2]
    out = []
    for (at, n), shape in zip(offsets, shapes):
        size = math.prod(shape)
        flat = buf[..., at:at + n, :].reshape(lead + (n * D_MODEL,))[..., :size]
        out.append(flat.reshape(lead + tuple(shape)))
    return out


def _join_shards(name, gathered):
    if name in ROW_SHARDED:
        return gathered.reshape(-1, gathered.shape[-1])
    g = jnp.moveaxis(gathered, 0, -2)
    return g.reshape(g.shape[:-2] + (g.shape[-2] * g.shape[-1],))


def _pack_layer_shards(p, l):
    arrays = [lax.bitcast_convert_type(p[n][0][l], BF16) if n == "conv_w" else p[n][0][l].astype(BF16) for n in SHARDED]
    packed, offsets = _pack_rows(arrays, BF16, GRAD_ROW_PAD)
    return packed, offsets, [a.shape for a in arrays]


def _unpack_layer_weights(gathered, offsets, shapes):
    full = {}
    for n, gth in zip(SHARDED, _unpack_rows(gathered, offsets, shapes)):
        if n == "conv_w":
            gth = lax.bitcast_convert_type(gth, F32)
        full[n] = _join_shards(n, gth)
    return full


def _owner_major(grad_sharded_dim_first):
    a = grad_sharded_dim_first
    return a.reshape((N_DEV, a.shape[0] // N_DEV) + a.shape[1:])


def _owner_block_shape(name, shard_shape):
    k, n = shard_shape[-2:]
    return (k, n) if name in ROW_SHARDED else (n, k)


def _natural_shard(name, block):
    return block if name in ROW_SHARDED else block.T


def _swap_halves(a, axis=-1):
    lo, hi = jnp.split(a, 2, axis=axis)
    return jnp.concatenate([hi, lo], axis=axis)


def _pad_cols(a, left, total):
    return jnp.pad(a, ((0, 0),) * (a.ndim - 1) + ((left, total - left - a.shape[-1]),))


def _layer_weights(full, small, l):
    w_in = full["w_in"]
    k_r = w_in[:, 384:416]
    w_in_k = jnp.concatenate(
        [w_in[:, 0:384], _pad_cols(k_r, D_NOPE, HEAD_PAD), _pad_cols(_swap_halves(k_r), D_NOPE, HEAD_PAD), w_in[:, 416:1952]], axis=1)
    w_uq = full["w_uq"].reshape(Q_RANK, N_HEADS, D_NOPE + D_ROPE)
    wq = jnp.moveaxis(_pad_cols(w_uq, 0, HEAD_PAD), 1, 0)
    wqs = jnp.moveaxis(_pad_cols(_swap_halves(w_uq[..., D_NOPE:]), D_NOPE, HEAD_PAD), 1, 0)
    w_ukv = full["w_ukv"].reshape(KV_RANK, N_HEADS, D_NOPE + D_V)
    wk = jnp.moveaxis(_pad_cols(w_ukv[..., :D_NOPE], 0, HEAD_PAD), 1, 0)
    wv = w_ukv[..., D_NOPE:].reshape(KV_RANK, N_HEADS * D_V)
    tri = jnp.tril(jnp.ones((SG_CHUNK, SG_CHUNK), bool))
    wm = jnp.where(tri, small["sg_w"][l], 0.0)
    pool_w = small["pool_w"][l]
    wp = jnp.zeros((BR_W, BR_W), F32)
    for g in range(len(POOL_WINDOWS)):
        wp = wp.at[g * POOL_GROUP:(g + 1) * POOL_GROUP, g * POOL_GROUP:(g + 1) * POOL_GROUP].set(pool_w[g])
    vec = lambda name: small[name][l][None, :]
    return dict(
        w_in=w_in_k, wg=w_in[:, 1952:], bg=vec("gate_b"), g1=vec("norm_mix_pre"), qn=vec("q_norm"), kvn=vec("kv_norm"),
        wq=wq, wqs=wqs, wk=wk, wv=wv,
        ln_g=vec("sg_ln_g"), ln_b=vec("sg_ln_b"), wm=wm.astype(BF16), wmt=jnp.swapaxes(wm, 1, 2).astype(BF16),
        sgb=jnp.repeat(small["sg_b"][l].T, POOL_GROUP, axis=1), conv_w=full["conv_w"],
        wp=wp.astype(BF16), pscale=vec("pool_scale"),
        wbr=jnp.stack([full["w_br_mla"], full["w_br_sg"], full["w_br_conv"], full["w_br_pool"]]),
        wout=full["w_out"], g2=vec("norm_mix_post"), g3=vec("norm_ffn_pre"), w1=full["w_ff1"], w2=full["w_ff2"],
        g4=vec("norm_ffn_post"))


def _layer_grads(g):
    dwa = g["dwa"]
    rope_rows = slice(D_NOPE, D_NOPE + D_ROPE)
    d_kr = dwa[384:512][rope_rows] + _swap_halves(dwa[512:640][rope_rows], axis=0)
    w_in = jnp.concatenate([dwa[0:384], d_kr, g["dwm_in"]] + list(g["dwg"]), axis=0)
    dwq, dwqs = g["dwq"], g["dwqs"]
    d_rope = dwq[:, rope_rows] + _swap_halves(dwqs[:, rope_rows], axis=1)
    w_uq = jnp.concatenate([dwq[:, :D_NOPE], d_rope], axis=1).reshape(N_HEADS * (D_NOPE + D_ROPE), Q_RANK)
    w_ukv = jnp.concatenate([g["dwk"][:, :D_NOPE], g["dwv"].reshape(N_HEADS, D_V, KV_RANK)], axis=1)
    w_ukv = w_ukv.reshape(N_HEADS * (D_NOPE + D_V), KV_RANK)
    pool_w = jnp.stack([g["dwp"][i * POOL_GROUP:(i + 1) * POOL_GROUP, i * POOL_GROUP:(i + 1) * POOL_GROUP]
                        for i in range(len(POOL_WINDOWS))])
    sg_b = g["dsgb"].reshape(SG_CHUNK, SG_GROUPS, POOL_GROUP).sum(axis=-1).T
    own = _owner_major
    return dict(
        w_in=own(w_in), w_uq=own(w_uq), w_ukv=own(w_ukv), w_br_mla=own(g["dwbr"][0]), w_br_sg=own(g["dwbr"][1]),
        w_br_conv=own(g["dwbr"][2]), w_br_pool=own(g["dwbr"][3]), w_out=own(g["dwout"]), w_ff1=own(g["dw1"]),
        w_ff2=own(g["dw2"]), conv_w=own(g["dcw"][:CONV_K].T),
        norm_mix_pre=g["dg1"][0], gate_b=jnp.concatenate([b[0] for b in g["dbg"]]), q_norm=g["dqn"][0], kv_norm=g["dkvn"][0],
        sg_ln_g=g["dlng"][0], sg_ln_b=g["dlnb"][0], sg_w=g["dwm"], sg_b=sg_b, pool_w=pool_w, pool_scale=g["dps"][0],
        norm_mix_post=g["dg2"][0], norm_ffn_pre=g["dg3"][0], norm_ffn_post=g["dg4"][0])


def _rope_tables(positions):
    inv_freq = ROPE_BASE ** (-jnp.arange(0, D_ROPE, 2, dtype=F32) / D_ROPE)
    ang = positions.astype(F32)[:, None] * inv_freq
    cos, sin = jnp.cos(ang), jnp.sin(ang)
    s = positions.shape[0]
    cosf = jnp.concatenate([jnp.ones((s, D_NOPE), F32), cos, cos, jnp.zeros((s, HEAD_PAD - D_NOPE - D_ROPE), F32)], axis=1)
    sins = jnp.concatenate([jnp.zeros((s, D_NOPE), F32), -sin, sin, jnp.zeros((s, HEAD_PAD - D_NOPE - D_ROPE), F32)], axis=1)
    return cosf, sins


def _layer_fwd(x, w, cosf, sins, tiles, riding=None):
    t, ta = tiles["tok"], tiles["attn"]
    hb, pa, pm = _inproj_fwd(x, w["g1"], w["w_in"], t)
    q, k, v = _attn_prep_fwd(pa, cosf, sins, w["qn"], w["kvn"], w["wq"], w["wqs"], w["wk"], w["wv"], t)
    o, lse, *landed = _flash_fwd(q, k, v, ta, riding)
    bsg, bcv, bpl = _mixers_fwd(pm, w["ln_g"], w["ln_b"], w["wm"], w["sgb"], w["conv_w"], w["wp"], w["pscale"], t)
    x_mid, merged, tt = _merge_fwd(x, hb, (o, bsg, bcv, bpl), w["wg"], w["bg"], w["wbr"], w["wout"], w["g2"], t)
    x_out, f = _ffn_fwd(x_mid, w["g3"], w["w1"], w["w2"], w["g4"], t)
    saved = dict(x=x, hb=hb, pa=pa, pm=pm, q=q, k=k, v=v, o=o, lse=lse, bsg=bsg, bcv=bcv, bpl=bpl, x_mid=x_mid,
                 merged=merged, tt=tt, f=f)
    return x_out, saved, (landed[0] if landed else None)


def _layer_bwd(dxo, a, w, cosf, sins, tiles, riding=None):
    t, ta, tb = tiles["tok"], tiles["attn"], tiles["inproj_bwd"]
    g = {}
    h2, df, g["dg4"] = _ffn_bwd_norms(a["x_mid"], a["f"], dxo, w["g3"], w["g4"], t)
    da, g["dw1"], g["dw2"] = _ffn_bwd_weights(h2, df, w["w1"], w["w2"], t)
    dxmid, g["dg3"] = _ffn_bwd_input(da, w["w1"], a["x_mid"], dxo, w["g3"], t)
    dm, g["dwout"], g["dg2"] = _merge_bwd_out(a["tt"], dxmid, w["g2"], w["wout"], a["merged"], t)
    dbrs, dpres, g["dwg"], g["dbg"], g["dwbr"] = [], [], [], [], []
    for k, br in enumerate((a["o"], a["bsg"], a["bcv"], a["bpl"])):
        dbr, dpre, dwg, dbg, dwbr = _merge_bwd_branch(k, a["hb"], br, dm, w["wg"], w["bg"], w["wbr"], t)
        dbrs.append(dbr); dpres.append(dpre); g["dwg"].append(dwg); g["dbg"].append(dbg); g["dwbr"].append(dwbr)
    dpm, g["dlng"], g["dlnb"], g["dwm"], g["dsgb"], g["dcw"], g["dwp"], g["dps"] = _mixers_bwd(
        a["pm"], dbrs[1], dbrs[2], dbrs[3], w["ln_g"], w["ln_b"], w["wm"], w["wmt"], w["sgb"], w["conv_w"], w["wp"], w["pscale"], t)
    dq = _flash_bwd_dq(a["q"], a["k"], a["v"], a["o"], dbrs[0], a["lse"], ta)
    dk, dv, *landed = _flash_bwd_dkv(a["q"], a["k"], a["v"], a["o"], dbrs[0], a["lse"], ta, riding)
    dpa, g["dwq"], g["dwqs"], g["dwk"], g["dwv"], g["dqn"], g["dkvn"] = _attn_prep_bwd(
        a["pa"], dq, dk, dv, cosf, sins, w["qn"], w["kvn"], w["wq"], w["wqs"], w["wk"], w["wv"], t)
    dx, g["dg1"], g["dwa"], g["dwm_in"] = _inproj_bwd(a["x"], w["g1"], dxmid, dpa, dpm, dpres, w["w_in"], w["wg"], tb)
    return dx, _layer_grads(g), (landed[0] if landed else None)


def _step(p, x, positions, loss_target):
    s = x.shape[0]
    depth = p["w_in"][0].shape[0]
    tiles = dict(tok=min(TOK_TILE, s), attn=min(ATTN_TILE, s), inproj_bwd=min(INPROJ_BWD_TILE, s))

    small = {n: p[n][0] for n in REPLICATED}
    cosf, sins = _rope_tables(positions)

    packed = [_pack_layer_shards(p, l) for l in range(depth)]
    gathered = _all_gather(packed[0][0], "weight_all_gather")
    weights, acts = [], []
    h = x
    for l in range(depth):
        weights.append(_layer_weights(_unpack_layer_weights(gathered, *packed[l][1:]), small, l))
        riding = None
        if l + 1 < depth:
            nxt = packed[l + 1][0]
            riding = _riding(_gather_plan, nxt, _sds((N_DEV,) + nxt.shape, nxt.dtype))
        h, saved, gathered = _layer_fwd(h, weights[l], cosf, sins, tiles, riding)
        acts.append(saved)
    dh, loss_blk = _loss_head(h, loss_target, tiles["tok"])
    loss = lax.psum(loss_blk[0, 0], ("x", "y", "c"))

    layer_grads = [None] * depth
    shard_blocks = [None] * depth
    block_shapes = [_owner_block_shape(n, p[n][0].shape) for n in SHARDED]
    send = None
    for l in reversed(range(depth)):
        riding = None if send is None else _riding(_exchange_plan, send, _sds(send.shape, send.dtype))
        dh, layer_grads[l], landed = _layer_bwd(dh, acts[l], weights[l], cosf, sins, tiles, riding)
        if landed is not None:
            shard_blocks[l + 1] = _unpack_rows(_sum_slots(landed, "grad_shard_sum"), send_offsets, block_shapes)
        send, send_offsets = _pack_rows([layer_grads[l][n] for n in SHARDED], BF16, GRAD_ROW_PAD, lead=(N_DEV,))
    landed = _all_to_all(send, "grad_all_to_all")
    shard_blocks[0] = _unpack_rows(_sum_slots(landed, "grad_shard_sum"), send_offsets, block_shapes)
    g_shard = {n: jnp.stack([_natural_shard(n, shard_blocks[l][i]) for l in range(depth)]) for i, n in enumerate(SHARDED)}
    small_grads = [jnp.stack([layer_grads[l][n] for l in range(depth)]) for n in REPLICATED]
    small_rows, small_offsets = _pack_rows(small_grads, F32, ROW_PAD)
    g_small_rows = _sum_slots(_all_gather(small_rows, "replicated_grad_all_gather"), "grad_replicated_sum")

    grad, delta, new_m, new_v = {}, {}, {}, {}
    for n in SHARDED:
        w, m, v = p[n]
        shape2 = (math.prod(w.shape[:-1]), w.shape[-1])
        grad[n] = g_shard[n]
        d, nm, nv = _adamw(w.reshape(shape2), grad[n].reshape(shape2), m.reshape(shape2), v.reshape(shape2), f"adamw_{n}")
        delta[n], new_m[n], new_v[n] = d.reshape(w.shape), nm.reshape(w.shape), nv.reshape(w.shape)
    packs = [_pack_rows([p[n][i] for n in REPLICATED], F32, ROW_PAD)[0] for i in range(3)]
    d, nm, nv = _adamw(packs[0], g_small_rows, packs[1], packs[2], "adamw_replicated")
    shapes = [p[n][0].shape for n in REPLICATED]
    for n, gg, dd, mm, vv in zip(REPLICATED, _unpack_rows(g_small_rows, small_offsets, shapes), _unpack_rows(d, small_offsets, shapes),
                                 _unpack_rows(nm, small_offsets, shapes), _unpack_rows(nv, small_offsets, shapes)):
        grad[n], delta[n], new_m[n], new_v[n] = gg, dd, mm, vv
    return loss, dh, grad, delta, new_m, new_v


WEIGHT_ORDER = ("norm_mix_pre", "w_in", "gate_b", "q_norm", "w_uq", "kv_norm", "w_ukv", "w_br_mla", "sg_ln_g", "sg_ln_b",
                "sg_w", "sg_b", "w_br_sg", "conv_w", "w_br_conv", "pool_w", "pool_scale", "w_br_pool", "w_out",
                "norm_mix_post", "norm_ffn_pre", "w_ff1", "w_ff2", "norm_ffn_post")


def kernel(x, positions, norm_mix_pre, w_in, gate_b, q_norm, w_uq, kv_norm, w_ukv, w_br_mla, sg_ln_g, sg_ln_b, sg_w, sg_b, w_br_sg, conv_w, w_br_conv, pool_w, pool_scale, w_br_pool, w_out, norm_mix_post, norm_ffn_pre, w_ff1, w_ff2, norm_ffn_post, loss_target, m_norm_mix_pre, m_w_in, m_gate_b, m_q_norm, m_w_uq, m_kv_norm, m_w_ukv, m_w_br_mla, m_sg_ln_g, m_sg_ln_b, m_sg_w, m_sg_b, m_w_br_sg, m_conv_w, m_w_br_conv, m_pool_w, m_pool_scale, m_w_br_pool, m_w_out, m_norm_mix_post, m_norm_ffn_pre, m_w_ff1, m_w_ff2, m_norm_ffn_post, v_norm_mix_pre, v_w_in, v_gate_b, v_q_norm, v_w_uq, v_kv_norm, v_w_ukv, v_w_br_mla, v_sg_ln_g, v_sg_ln_b, v_sg_w, v_sg_b, v_w_br_sg, v_conv_w, v_w_br_conv, v_pool_w, v_pool_scale, v_w_br_pool, v_w_out, v_norm_mix_post, v_norm_ffn_pre, v_w_ff1, v_w_ff2, v_norm_ffn_post):
    ws = (norm_mix_pre, w_in, gate_b, q_norm, w_uq, kv_norm, w_ukv, w_br_mla, sg_ln_g, sg_ln_b, sg_w, sg_b, w_br_sg, conv_w,
          w_br_conv, pool_w, pool_scale, w_br_pool, w_out, norm_mix_post, norm_ffn_pre, w_ff1, w_ff2, norm_ffn_post)
    ms = (m_norm_mix_pre, m_w_in, m_gate_b, m_q_norm, m_w_uq, m_kv_norm, m_w_ukv, m_w_br_mla, m_sg_ln_g, m_sg_ln_b, m_sg_w,
          m_sg_b, m_w_br_sg, m_conv_w, m_w_br_conv, m_pool_w, m_pool_scale, m_w_br_pool, m_w_out, m_norm_mix_post,
          m_norm_ffn_pre, m_w_ff1, m_w_ff2, m_norm_ffn_post)
    vs = (v_norm_mix_pre, v_w_in, v_gate_b, v_q_norm, v_w_uq, v_kv_norm, v_w_ukv, v_w_br_mla, v_sg_ln_g, v_sg_ln_b, v_sg_w,
          v_sg_b, v_w_br_sg, v_conv_w, v_w_br_conv, v_pool_w, v_pool_scale, v_w_br_pool, v_w_out, v_norm_mix_post,
          v_norm_ffn_pre, v_w_ff1, v_w_ff2, v_norm_ffn_post)
    p = {n: (w, m, v) for n, w, m, v in zip(WEIGHT_ORDER, ws, ms, vs)}
    loss, grad_x, grad, delta, new_m, new_v = _step(p, x[0], positions[0], loss_target[0])
    return (loss, grad_x[None], *[grad[n] for n in WEIGHT_ORDER], *[delta[n] for n in WEIGHT_ORDER],
            *[new_m[n] for n in WEIGHT_ORDER], *[new_v[n] for n in WEIGHT_ORDER])
```

```python
import functools
import math

import jax
import jax.numpy as jnp
from jax import lax
from jax.experimental import pallas as pl
from jax.experimental.pallas import tpu as pltpu

F32 = jnp.float32
BF16 = jnp.bfloat16

D_MODEL = 1024
N_HEADS = 4
D_NOPE = 64
D_ROPE = 32
D_V = 64
Q_RANK = 256
KV_RANK = 128
BR_W = 256
SG_CHUNK = 128
SG_GROUPS = 4
POOL_WINDOWS = (2, 4, 8, 16)
POOL_GROUP = 64
CONV_K = 3
D_FF = 4096
N_BRANCH = 4
N_IN = 6048
EPS = 1e-6
ROPE_BASE = 10000.0
ADAM_LR = 0.001
ADAM_B1 = 0.9
ADAM_B2 = 0.999
ADAM_EPS = 1e-08
ADAM_WD = 0.01
ADAM_STEP = 10

N_DEV = 8
LANES = 128
HEAD_PAD = 128
HALO = 16
PA_W = 640
PM_W = 6 * BR_W
VMEM_LIMIT = 56 * 1024 * 1024
SM_SCALE = (D_NOPE + D_ROPE) ** -0.5
SM_SCALE_LOG2 = SM_SCALE * math.log2(math.e)
TOK_TILE = 512
ATTN_TILE = 512
INPROJ_BWD_TILE = 256

NN = (((1,), (0,)), ((), ()))
NT = (((1,), (1,)), ((), ()))
TN = (((0,), (0,)), ((), ()))
MESH_ID = pl.DeviceIdType.MESH


def _dot(a, b, dims=NN):
    return lax.dot_general(a, b, dims, preferred_element_type=F32)


def _pcall(body, *, name, grid, in_specs, out_specs, out_shape, scratch=(), vmem=None, prefetch=0):
    params = pltpu.CompilerParams(vmem_limit_bytes=vmem)
    if prefetch:
        spec = pltpu.PrefetchScalarGridSpec(num_scalar_prefetch=prefetch, grid=grid, in_specs=in_specs,
                                            out_specs=out_specs, scratch_shapes=scratch)
        return pl.pallas_call(body, name=name, grid_spec=spec, out_shape=out_shape, compiler_params=params)
    return pl.pallas_call(
        body, name=name, grid=grid, in_specs=in_specs, out_specs=out_specs, out_shape=out_shape,
        scratch_shapes=scratch, compiler_params=params)


def _rows(t, width):
    return pl.BlockSpec((t, width), lambda i: (i, 0))


def _whole(shape):
    nd = len(shape)
    return pl.BlockSpec(tuple(shape), lambda *_: (0,) * nd, pipeline_mode=pl.Buffered(1))


def _acc(shape):
    nd = len(shape)
    return pl.BlockSpec(tuple(shape), lambda *_: (0,) * nd)


def _sds(shape, dtype):
    return jax.ShapeDtypeStruct(tuple(shape), dtype)


def _rms(x, g):
    return x * lax.rsqrt(jnp.mean(x * x, axis=-1, keepdims=True) + EPS) * g


def _rms_bwd(x, g, dy):
    r = lax.rsqrt(jnp.mean(x * x, axis=-1, keepdims=True) + EPS)
    xh = x * r
    dg = jnp.sum(dy * xh, axis=0, keepdims=True)
    dxh = dy * g
    dx = r * (dxh - xh * jnp.mean(dxh * xh, axis=-1, keepdims=True))
    return dx, dg


def _sigmoid(x):
    return 1.0 / (1.0 + jnp.exp(-x))


def _gelu(x):
    return jax.nn.gelu(x, approximate=True)


def _accumulate(ref, val, first):
    @pl.when(first)
    def _():
        ref[...] = val

    @pl.when(jnp.logical_not(first))
    def _():
        ref[...] += val


def _inproj_fwd(x, g1, w_in, t):
    s = x.shape[0]

    def body(x_ref, g_ref, w_ref, hb_ref, pa_ref, pm_ref):
        hb = _rms(x_ref[...], g_ref[...]).astype(BF16)
        hb_ref[...] = hb
        pa_ref[...] = _dot(hb, w_ref[:, :PA_W])
        pm_ref[...] = _dot(hb, w_ref[:, PA_W:])

    return _pcall(
        body, name="inproj_fwd", grid=(s // t,),
        in_specs=[_rows(t, D_MODEL), _whole((1, D_MODEL)), _whole(w_in.shape)],
        out_specs=[_rows(t, D_MODEL), _rows(t, PA_W), _rows(t, PM_W)],
        out_shape=[_sds((s, D_MODEL), BF16), _sds((s, PA_W), F32), _sds((s, PM_W), F32)],
        vmem=VMEM_LIMIT)(x, g1, w_in)


def _attn_prep_fwd(pa, cosf, sins, qn, kvn, wq, wqs, wk, wv, t):
    s = pa.shape[0]

    def body(pa_ref, cos_ref, sin_ref, qn_ref, kvn_ref, wq_ref, wqs_ref, wk_ref, wv_ref, q_ref, k_ref, v_ref):
        cosv, sinv = cos_ref[...], sin_ref[...]
        cqn = _rms(pa_ref[:, 0:Q_RANK], qn_ref[...]).astype(BF16)
        ckvn = _rms(pa_ref[:, Q_RANK:Q_RANK + KV_RANK], kvn_ref[...]).astype(BF16)
        k_rope = pa_ref[:, 384:512] * cosv + pa_ref[:, 512:640] * sinv
        for h in range(N_HEADS):
            q = _dot(cqn, wq_ref[h]) * cosv + _dot(cqn, wqs_ref[h]) * sinv
            q_ref[h] = q.astype(BF16)
            k_ref[h] = (_dot(ckvn, wk_ref[h]) + k_rope).astype(BF16)
        v_ref[...] = _dot(ckvn, wv_ref[...]).astype(BF16)

    head_blk = pl.BlockSpec((N_HEADS, t, HEAD_PAD), lambda i: (0, i, 0))
    return _pcall(
        body, name="attn_prep_fwd", grid=(s // t,),
        in_specs=[_rows(t, PA_W), _rows(t, LANES), _rows(t, LANES), _whole(qn.shape), _whole(kvn.shape),
                  _whole(wq.shape), _whole(wqs.shape), _whole(wk.shape), _whole(wv.shape)],
        out_specs=[head_blk, head_blk, _rows(t, BR_W)],
        out_shape=[_sds((N_HEADS, s, HEAD_PAD), BF16), _sds((N_HEADS, s, HEAD_PAD), BF16), _sds((s, BR_W), BF16)],
        vmem=VMEM_LIMIT)(pa, cosf, sins, qn, kvn, wq, wqs, wk, wv)


def _causal_scores(q, k, masked):
    sc = _dot(q, k, NT) * SM_SCALE_LOG2
    if masked:
        row = lax.broadcasted_iota(jnp.int32, sc.shape, 0)
        col = lax.broadcasted_iota(jnp.int32, sc.shape, 1)
        sc = jnp.where(col <= row, sc, -jnp.inf)
    return sc


def _causal_steps(n, key_major):
    if key_major:
        pairs = [(qi, kj) for kj in range(n) for qi in range(kj, n)]
    else:
        pairs = [(qi, kj) for qi in range(n) for kj in range(qi + 1)]
    return (jnp.asarray([p[0] for p in pairs], jnp.int32), jnp.asarray([p[1] for p in pairs], jnp.int32))


def _flash_fwd(q, k, v, t, riding=None):
    s = v.shape[0]
    n = s // t
    reps = t // LANES
    qi_tab, kj_tab = _causal_steps(n, key_major=False)
    n_steps = int(qi_tab.shape[0])

    def body(qi_ref, kj_ref, q_ref, k_ref, v_ref, *rest):
        if riding is None:
            o_ref, lse_ref, m_scr, l_scr, acc_scr = rest
        else:
            x_ref, o_ref, lse_ref, land_ref, m_scr, l_scr, acc_scr, send_sems, recv_sems, local_sem = rest
        step_id = pl.program_id(0)
        qi, kj = qi_ref[step_id], kj_ref[step_id]
        if riding is not None:
            at_end = _ride_along(riding, step_id, n_steps, (x_ref, land_ref, send_sems, recv_sems, local_sem))

        @pl.when(kj == 0)
        def _():
            m_scr[...] = jnp.full(m_scr.shape, -jnp.inf, F32)
            l_scr[...] = jnp.zeros(l_scr.shape, F32)
            acc_scr[...] = jnp.zeros(acc_scr.shape, F32)

        def step(masked):
            for h in range(N_HEADS):
                sc = _causal_scores(q_ref[h], k_ref[h], masked)
                m_prev = m_scr[h]
                m_next = jnp.maximum(m_prev, jnp.max(sc, axis=1, keepdims=True))
                alpha = jnp.exp2(m_prev - m_next)
                p = jnp.exp2(sc - jnp.tile(m_next, (1, reps)))
                l_scr[h] = alpha * l_scr[h] + jnp.sum(p, axis=1, keepdims=True)
                m_scr[h] = m_next
                pair = (h // 2) * LANES
                acc_scr[h] = acc_scr[h] * alpha + _dot(p.astype(BF16), v_ref[:, pair:pair + LANES])

        @pl.when(kj < qi)
        def _():
            step(False)

        @pl.when(kj == qi)
        def _():
            step(True)
            lane = lax.broadcasted_iota(jnp.int32, (t, LANES), 1)
            for pr in range(N_HEADS // 2):
                o0 = acc_scr[2 * pr] / l_scr[2 * pr]
                o1 = acc_scr[2 * pr + 1] / l_scr[2 * pr + 1]
                o_ref[:, pr * LANES:(pr + 1) * LANES] = jnp.where(lane < D_V, o0, o1).astype(BF16)
            for h in range(N_HEADS):
                lse_ref[h] = m_scr[h] + jnp.log2(l_scr[h])

        if riding is not None:
            at_end()

    q_blk = pl.BlockSpec((N_HEADS, t, HEAD_PAD), lambda i, qi, kj: (0, qi[i], 0))
    k_blk = pl.BlockSpec((N_HEADS, t, HEAD_PAD), lambda i, qi, kj: (0, kj[i], 0))
    v_blk = pl.BlockSpec((t, BR_W), lambda i, qi, kj: (kj[i], 0))
    extra = riding is not None
    return _pcall(
        body, name="flash_fwd_gathering" if extra else "flash_fwd", grid=(n_steps,), prefetch=2,
        in_specs=[q_blk, k_blk, v_blk] + [HBM_REF] * extra,
        out_specs=[pl.BlockSpec((t, BR_W), lambda i, qi, kj: (qi[i], 0)), q_blk] + [HBM_REF] * extra,
        out_shape=[_sds((s, BR_W), BF16), _sds((N_HEADS, s, LANES), F32)] + ([riding[2]] if extra else []),
        scratch=[pltpu.VMEM((N_HEADS, t, LANES), F32)] * 3 + (list(EXCHANGE_SEMS) if extra else []),
        vmem=VMEM_LIMIT)(qi_tab, kj_tab, q, k, v, *([riding[1]] if extra else []))


def _shift_down(ext, k, t):
    return pltpu.roll(ext, k, 0)[HALO:HALO + t]


def _shift_up(ext, k, t):
    return pltpu.roll(ext, t + HALO - k, 0)[0:t]


def _lane_group(shape):
    return lax.shift_right_logical(lax.broadcasted_iota(jnp.int32, shape, 1), 6)


def _group_select(vals):
    grp = _lane_group(vals[0].shape)
    out = vals[0]
    for g in range(1, len(vals)):
        out = jnp.where(grp == g, vals[g], out)
    return out


def _layernorm(x, g, b):
    mu = jnp.mean(x, axis=-1, keepdims=True)
    xc = x - mu
    return xc * lax.rsqrt(jnp.mean(xc * xc, axis=-1, keepdims=True) + EPS) * g + b


def _sg_mix(wm_ref, vnb, bias):
    return _group_select([_dot(wm_ref[g], vnb) for g in range(SG_GROUPS)]) + bias


def _pool_windows(ext):
    s2 = ext + pltpu.roll(ext, 1, 0)
    s4 = s2 + pltpu.roll(s2, 2, 0)
    s8 = s4 + pltpu.roll(s4, 4, 0)
    s16 = s8 + pltpu.roll(s8, 8, 0)
    return [s2, s4, s8, s16]


def _pool_counts(tok):
    return [jnp.minimum(tok + 1, w).astype(F32) for w in POOL_WINDOWS]


def _halo_before(t):
    return pl.BlockSpec((HALO, PM_W), lambda i: (jnp.maximum(i * (t // HALO) - 1, 0), 0))


def _mixers_fwd(pm, ln_g, ln_b, wm, sgb, conv_w, wp, pscale, t):
    s = pm.shape[0]

    def body(pm_ref, halo_ref, lng_ref, lnb_ref, wm_ref, sgb_ref, cw_ref, wp_ref, ps_ref, bsg_ref, bcv_ref, bpl_ref):
        i = pl.program_id(0)
        halo = jnp.where(i > 0, halo_ref[...], 0.0)
        u = _gelu(pm_ref[:, 0:256])
        vnb = _layernorm(_gelu(pm_ref[:, 256:512]), lng_ref[...], lnb_ref[...]).astype(BF16)
        for c in range(t // SG_CHUNK):
            rows = slice(c * SG_CHUNK, (c + 1) * SG_CHUNK)
            bsg_ref[rows, :] = (u[rows] * _sg_mix(wm_ref, vnb[rows], sgb_ref[...])).astype(BF16)
        z = pm_ref[:, 1024:1280] * pm_ref[:, 512:768]
        zext = jnp.concatenate([halo[:, 1024:1280] * halo[:, 512:768], z], axis=0)
        y = cw_ref[0:1, :] * _shift_down(zext, 2, t) + cw_ref[1:2, :] * _shift_down(zext, 1, t) + cw_ref[2:3, :] * z
        bcv_ref[...] = (pm_ref[:, 768:1024] * y).astype(BF16)
        p = pm_ref[:, 1280:1536]
        sums = _pool_windows(jnp.concatenate([halo[:, 1280:1536], p], axis=0))
        tok = i * t + lax.broadcasted_iota(jnp.int32, (t, 1), 0)
        pooled = _group_select([sw[HALO:HALO + t] / cnt - p for sw, cnt in zip(sums, _pool_counts(tok))])
        bpl_ref[...] = (_dot(pooled.astype(BF16), wp_ref[...]) * ps_ref[...]).astype(BF16)

    return _pcall(
        body, name="mixers_fwd", grid=(s // t,),
        in_specs=[_rows(t, PM_W), _halo_before(t), _whole(ln_g.shape), _whole(ln_b.shape), _whole(wm.shape),
                  _whole(sgb.shape), _whole(conv_w.shape), _whole(wp.shape), _whole(pscale.shape)],
        out_specs=[_rows(t, BR_W)] * 3,
        out_shape=[_sds((s, BR_W), BF16)] * 3,
        vmem=VMEM_LIMIT)(pm, pm, ln_g, ln_b, wm, sgb, conv_w, wp, pscale)


def _merge_fwd(x, hb, branches, wg, bg, wbr, wout, g2, t):
    s = x.shape[0]

    def body(x_ref, hb_ref, b0_ref, b1_ref, b2_ref, b3_ref, wg_ref, bg_ref, wbr_ref, wout_ref, g2_ref,
             xmid_ref, mrg_ref, t_ref):
        hb = hb_ref[...]
        merged = jnp.zeros((t, D_MODEL), F32)
        for k, b_ref in enumerate((b0_ref, b1_ref, b2_ref, b3_ref)):
            cols = slice(k * D_MODEL, (k + 1) * D_MODEL)
            gate = _sigmoid(_dot(hb, wg_ref[:, cols]) + bg_ref[:, cols])
            merged = merged + gate * _dot(b_ref[...], wbr_ref[k])
        mb = merged.astype(BF16)
        mrg_ref[...] = mb
        tt = _dot(mb, wout_ref[...])
        t_ref[...] = tt
        xmid_ref[...] = x_ref[...] + _rms(tt, g2_ref[...])

    return _pcall(
        body, name="merge_fwd", grid=(s // t,),
        in_specs=[_rows(t, D_MODEL), _rows(t, D_MODEL)] + [_rows(t, BR_W)] * 4 +
                 [_whole(wg.shape), _whole(bg.shape), _whole(wbr.shape), _whole(wout.shape), _whole(g2.shape)],
        out_specs=[_rows(t, D_MODEL)] * 3,
        out_shape=[_sds((s, D_MODEL), F32), _sds((s, D_MODEL), BF16), _sds((s, D_MODEL), F32)],
        vmem=VMEM_LIMIT)(x, hb, *branches, wg, bg, wbr, wout, g2)


def _ffn_fwd(x, g3, w1, w2, g4, t):
    s = x.shape[0]

    def body(x_ref, g3_ref, w1_ref, w2_ref, g4_ref, xo_ref, f_ref):
        h = _rms(x_ref[...], g3_ref[...]).astype(BF16)
        f = jnp.zeros((t, D_MODEL), F32)
        for j in range(D_FF // D_MODEL):
            cols = slice(j * D_MODEL, (j + 1) * D_MODEL)
            r = jnp.square(jnp.maximum(_dot(h, w1_ref[:, cols]), 0.0)).astype(BF16)
            f = f + _dot(r, w2_ref[cols, :])
        f_ref[...] = f
        xo_ref[...] = x_ref[...] + _rms(f, g4_ref[...])

    return _pcall(
        body, name="ffn_fwd", grid=(s // t,),
        in_specs=[_rows(t, D_MODEL), _whole(g3.shape), _whole(w1.shape), _whole(w2.shape), _whole(g4.shape)],
        out_specs=[_rows(t, D_MODEL)] * 2,
        out_shape=[_sds((s, D_MODEL), F32)] * 2,
        vmem=VMEM_LIMIT)(x, g3, w1, w2, g4)


def _loss_head(y, target, t):
    s = y.shape[0]
    n = s // t

    def body(y_ref, tg_ref, dy_ref, loss_ref, acc_scr):
        i = pl.program_id(0)
        d = y_ref[...] - tg_ref[...]
        dy_ref[...] = d * (1.0 / D_MODEL)
        _accumulate(acc_scr, jnp.sum(d * d, axis=0, keepdims=True), i == 0)

        @pl.when(i == n - 1)
        def _():
            loss_ref[...] = jnp.full(loss_ref.shape, 0.5 / D_MODEL, F32) * jnp.sum(acc_scr[...])

    return _pcall(
        body, name="loss_head", grid=(n,),
        in_specs=[_rows(t, D_MODEL)] * 2,
        out_specs=[_rows(t, D_MODEL), _acc((8, LANES))],
        out_shape=[_sds((s, D_MODEL), F32), _sds((8, LANES), F32)],
        scratch=[pltpu.VMEM((1, D_MODEL), F32)])(y, target)


def _ffn_bwd_norms(x_mid, f, dxo, g3, g4, t):
    s = x_mid.shape[0]

    def body(x_ref, f_ref, dxo_ref, g3_ref, g4_ref, h2_ref, df_ref, dg4_ref):
        h2_ref[...] = _rms(x_ref[...], g3_ref[...]).astype(BF16)
        df, dg4 = _rms_bwd(f_ref[...], g4_ref[...], dxo_ref[...])
        df_ref[...] = df.astype(BF16)
        _accumulate(dg4_ref, dg4, pl.program_id(0) == 0)

    return _pcall(
        body, name="ffn_bwd_norms", grid=(s // t,),
        in_specs=[_rows(t, D_MODEL)] * 3 + [_whole(g3.shape), _whole(g4.shape)],
        out_specs=[_rows(t, D_MODEL), _rows(t, D_MODEL), _acc((1, D_MODEL))],
        out_shape=[_sds((s, D_MODEL), BF16), _sds((s, D_MODEL), BF16), _sds((1, D_MODEL), F32)])(x_mid, f, dxo, g3, g4)


def _ffn_bwd_weights(h2, df, w1, w2, t):
    s = h2.shape[0]
    blk = D_MODEL

    def body(h2_ref, df_ref, w1_ref, w2_ref, da_ref, dw1_ref, dw2_ref):
        first = pl.program_id(1) == 0
        h2v, dfv = h2_ref[...], df_ref[...]
        rl = jnp.maximum(_dot(h2v, w1_ref[...]), 0.0)
        _accumulate(dw2_ref, _dot((rl * rl).astype(BF16), dfv, TN), first)
        da = (_dot(dfv, w2_ref[...], NT) * (2.0 * rl)).astype(BF16)
        da_ref[...] = da
        _accumulate(dw1_ref, _dot(da, h2v, TN), first)

    tok = pl.BlockSpec((t, D_MODEL), lambda j, i: (i, 0))
    hid = pl.BlockSpec((blk, D_MODEL), lambda j, i: (j, 0))
    return _pcall(
        body, name="ffn_bwd_weights", grid=(D_FF // blk, s // t),
        in_specs=[tok, tok, pl.BlockSpec((D_MODEL, blk), lambda j, i: (0, j)), hid],
        out_specs=[pl.BlockSpec((t, blk), lambda j, i: (i, j)), hid, hid],
        out_shape=[_sds((s, D_FF), BF16), _sds((D_FF, D_MODEL), F32), _sds((D_FF, D_MODEL), F32)],
        vmem=VMEM_LIMIT)(h2, df, w1, w2)


def _ffn_bwd_input(da, w1, x_mid, dxo, g3, t):
    s = x_mid.shape[0]

    def body(da_ref, w1_ref, x_ref, dxo_ref, g3_ref, dx_ref, dg3_ref):
        dx, dg3 = _rms_bwd(x_ref[...], g3_ref[...], _dot(da_ref[...], w1_ref[...], NT))
        dx_ref[...] = dxo_ref[...] + dx
        _accumulate(dg3_ref, dg3, pl.program_id(0) == 0)

    return _pcall(
        body, name="ffn_bwd_input", grid=(s // t,),
        in_specs=[_rows(t, D_FF), _whole(w1.shape), _rows(t, D_MODEL), _rows(t, D_MODEL), _whole(g3.shape)],
        out_specs=[_rows(t, D_MODEL), _acc((1, D_MODEL))],
        out_shape=[_sds((s, D_MODEL), F32), _sds((1, D_MODEL), F32)],
        vmem=VMEM_LIMIT)(da, w1, x_mid, dxo, g3)


def _merge_bwd_out(tt, dxmid, g2, wout, merged, t):
    s = tt.shape[0]

    def body(t_ref, dx_ref, g2_ref, wout_ref, mrg_ref, dm_ref, dwout_ref, dg2_ref):
        first = pl.program_id(0) == 0
        dt, dg2 = _rms_bwd(t_ref[...], g2_ref[...], dx_ref[...])
        dtb = dt.astype(BF16)
        dm_ref[...] = _dot(dtb, wout_ref[...], NT)
        _accumulate(dwout_ref, _dot(mrg_ref[...], dtb, TN), first)
        _accumulate(dg2_ref, dg2, first)

    return _pcall(
        body, name="merge_bwd_out", grid=(s // t,),
        in_specs=[_rows(t, D_MODEL), _rows(t, D_MODEL), _whole(g2.shape), _whole(wout.shape), _rows(t, D_MODEL)],
        out_specs=[_rows(t, D_MODEL), _acc((D_MODEL, D_MODEL)), _acc((1, D_MODEL))],
        out_shape=[_sds((s, D_MODEL), F32), _sds((D_MODEL, D_MODEL), F32), _sds((1, D_MODEL), F32)],
        vmem=VMEM_LIMIT)(tt, dxmid, g2, wout, merged)


def _merge_bwd_branch(k, hb, br, dm, wg, bg, wbr, t):
    s = hb.shape[0]

    def body(hb_ref, br_ref, dm_ref, wg_ref, bg_ref, wbr_ref, dbr_ref, dpre_ref, dwg_ref, dbg_ref, dwbr_ref):
        first = pl.program_id(0) == 0
        hbv, brv, dmv, wbrv = hb_ref[...], br_ref[...], dm_ref[...], wbr_ref[0]
        gate = _sigmoid(_dot(hbv, wg_ref[...]) + bg_ref[...])
        dy = (dmv * gate).astype(BF16)
        dpre = dmv * _dot(brv, wbrv) * gate * (1.0 - gate)
        dpb = dpre.astype(BF16)
        dpre_ref[...] = dpb
        dbr_ref[...] = _dot(dy, wbrv, NT)
        _accumulate(dwbr_ref, _dot(dy, brv, TN), first)
        _accumulate(dwg_ref, _dot(dpb, hbv, TN), first)
        _accumulate(dbg_ref, jnp.sum(dpre, axis=0, keepdims=True), first)

    return _pcall(
        body, name=f"merge_bwd_branch{k}", grid=(s // t,),
        in_specs=[_rows(t, D_MODEL), _rows(t, BR_W), _rows(t, D_MODEL),
                  pl.BlockSpec((D_MODEL, D_MODEL), lambda i: (0, k)), pl.BlockSpec((1, D_MODEL), lambda i: (0, k)),
                  pl.BlockSpec((1, BR_W, D_MODEL), lambda i: (k, 0, 0))],
        out_specs=[_rows(t, BR_W), _rows(t, D_MODEL), _acc((D_MODEL, D_MODEL)), _acc((1, D_MODEL)), _acc((D_MODEL, BR_W))],
        out_shape=[_sds((s, BR_W), F32), _sds((s, D_MODEL), BF16), _sds((D_MODEL, D_MODEL), F32),
                   _sds((1, D_MODEL), F32), _sds((D_MODEL, BR_W), F32)],
        vmem=VMEM_LIMIT)(hb, br, dm, wg, bg, wbr)


def _mixers_bwd(pm, dbsg, dbcv, dbpl, ln_g, ln_b, wm, wmt, sgb, conv_w, wp, pscale, t):
    s = pm.shape[0]
    n = s // t
    nb = t // HALO

    def body(pm_ref, before_ref, after_ref, dsg_ref, dcv_ref, dcva_ref, dpl_ref, dpla_ref, lng_ref, lnb_ref,
             wm_ref, wmt_ref, sgb_ref, cw_ref, wp_ref, ps_ref,
             dpm_ref, dlng_ref, dlnb_ref, dwm_ref, dsgb_ref, dcw_ref, dwp_ref, dps_ref):
        i = pl.program_id(0)
        first = i == 0
        before = jnp.where(i > 0, before_ref[...], 0.0)
        after = jnp.where(i < n - 1, after_ref[...], 0.0)

        u_raw, v_raw = pm_ref[:, 0:256], pm_ref[:, 256:512]
        lng, lnb = lng_ref[...], lnb_ref[...]
        u, gelu_u_vjp = jax.vjp(_gelu, u_raw)
        vn, norm_vjp = jax.vjp(lambda v_, g_, b_: _layernorm(_gelu(v_), g_, b_), v_raw, lng, lnb)
        vnb = vn.astype(BF16)
        dsg = dsg_ref[...]
        grp = _lane_group((SG_CHUNK, BR_W))
        tri = (lax.broadcasted_iota(jnp.int32, (SG_CHUNK, SG_CHUNK), 1)
               <= lax.broadcasted_iota(jnp.int32, (SG_CHUNK, SG_CHUNK), 0))
        du_parts, dvn_parts = [], []
        dsgb = jnp.zeros((SG_CHUNK, BR_W), F32)
        dwm = [jnp.zeros((SG_CHUNK, SG_CHUNK), F32) for _ in range(SG_GROUPS)]
        for c in range(t // SG_CHUNK):
            rows = slice(c * SG_CHUNK, (c + 1) * SG_CHUNK)
            mix = _sg_mix(wm_ref, vnb[rows], sgb_ref[...])
            du_parts.append(dsg[rows] * mix)
            ds = dsg[rows] * u[rows]
            dsgb = dsgb + ds
            dsb = [jnp.where(grp == g, ds, 0.0).astype(BF16) for g in range(SG_GROUPS)]
            for g in range(SG_GROUPS):
                dwm[g] = dwm[g] + _dot(dsb[g], vnb[rows], NT)
            dvn_parts.append(_group_select([_dot(wmt_ref[g], dsb[g]) for g in range(SG_GROUPS)]))
        (du_raw,) = gelu_u_vjp(jnp.concatenate(du_parts, axis=0))
        dv_raw, dlng, dlnb = norm_vjp(jnp.concatenate(dvn_parts, axis=0))
        dpm_ref[:, 0:256] = du_raw.astype(BF16)
        dpm_ref[:, 256:512] = dv_raw.astype(BF16)
        _accumulate(dlng_ref, dlng, first)
        _accumulate(dlnb_ref, dlnb, first)
        _accumulate(dsgb_ref, dsgb, first)
        for g in range(SG_GROUPS):
            _accumulate(dwm_ref.at[g], jnp.where(tri, dwm[g], 0.0), first)

        xin, bg, cg = pm_ref[:, 512:768], pm_ref[:, 768:1024], pm_ref[:, 1024:1280]
        z = cg * xin
        zext = jnp.concatenate([before[:, 1024:1280] * before[:, 512:768], z], axis=0)
        z1, z2 = _shift_down(zext, 1, t), _shift_down(zext, 2, t)
        w0, w1, w2 = cw_ref[0:1, :], cw_ref[1:2, :], cw_ref[2:3, :]
        dcv = dcv_ref[...]
        y = w0 * z2 + w1 * z1 + w2 * z
        dy = dcv * bg
        dyext = jnp.concatenate([dy, jnp.where(i < n - 1, dcva_ref[...], 0.0) * after[:, 768:1024]], axis=0)
        dz = w2 * dy + w1 * _shift_up(dyext, 1, t) + w0 * _shift_up(dyext, 2, t)
        dpm_ref[:, 512:768] = (dz * cg).astype(BF16)
        dpm_ref[:, 768:1024] = (dcv * y).astype(BF16)
        dpm_ref[:, 1024:1280] = (dz * xin).astype(BF16)
        dcw = jnp.concatenate([jnp.sum(dy * z2, axis=0, keepdims=True), jnp.sum(dy * z1, axis=0, keepdims=True),
                               jnp.sum(dy * z, axis=0, keepdims=True)], axis=0)
        _accumulate(dcw_ref, jnp.concatenate([dcw, jnp.zeros((8 - CONV_K, BR_W), F32)], axis=0), first)

        p = pm_ref[:, 1280:1536]
        tok = i * t + lax.broadcasted_iota(jnp.int32, (t, 1), 0)
        sums = _pool_windows(jnp.concatenate([before[:, 1280:1536], p], axis=0))
        pooled = _group_select([sw[HALO:HALO + t] / cnt - p for sw, cnt in zip(sums, _pool_counts(tok))]).astype(BF16)
        mixed = _dot(pooled, wp_ref[...])
        dpl = dpl_ref[...]
        ps = ps_ref[...]
        dmix = (dpl * ps).astype(BF16)
        _accumulate(dps_ref, jnp.sum(dpl * mixed, axis=0, keepdims=True), first)
        _accumulate(dwp_ref, _dot(pooled, dmix, TN), first)
        dmix_after = (jnp.where(i < n - 1, dpla_ref[...], 0.0) * ps).astype(BF16)
        dpo = _dot(dmix, wp_ref[...], NT)
        dpo_ext = jnp.concatenate([dpo, _dot(dmix_after, wp_ref[...], NT)], axis=0)
        tok_ext = i * t + lax.broadcasted_iota(jnp.int32, (t + HALO, 1), 0)
        dp_groups = []
        for g, (win, cnt) in enumerate(zip(POOL_WINDOWS, _pool_counts(tok_ext))):
            e = dpo_ext / cnt
            acc = e
            span = 1
            while span < win:
                acc = acc + pltpu.roll(acc, t + HALO - span, 0)
                span *= 2
            dp_groups.append(acc[0:t] - dpo)
        dpm_ref[:, 1280:1536] = _group_select(dp_groups).astype(BF16)

    row_blk = lambda w: pl.BlockSpec((t, w), lambda i: (i, 0))
    after_blk = lambda w: pl.BlockSpec((HALO, w), lambda i: (jnp.minimum((i + 1) * nb, n * nb - 1), 0))
    return _pcall(
        body, name="mixers_bwd", grid=(n,),
        in_specs=[row_blk(PM_W), _halo_before(t), after_blk(PM_W), row_blk(BR_W), row_blk(BR_W), after_blk(BR_W),
                  row_blk(BR_W), after_blk(BR_W), _whole(ln_g.shape), _whole(ln_b.shape), _whole(wm.shape),
                  _whole(wmt.shape), _whole(sgb.shape), _whole(conv_w.shape), _whole(wp.shape), _whole(pscale.shape)],
        out_specs=[row_blk(PM_W), _acc((1, BR_W)), _acc((1, BR_W)), _acc((SG_GROUPS, SG_CHUNK, SG_CHUNK)),
                   _acc((SG_CHUNK, BR_W)), _acc((8, BR_W)), _acc((BR_W, BR_W)), _acc((1, BR_W))],
        out_shape=[_sds((s, PM_W), BF16), _sds((1, BR_W), F32), _sds((1, BR_W), F32),
                   _sds((SG_GROUPS, SG_CHUNK, SG_CHUNK), F32), _sds((SG_CHUNK, BR_W), F32), _sds((8, BR_W), F32),
                   _sds((BR_W, BR_W), F32), _sds((1, BR_W), F32)],
        vmem=VMEM_LIMIT)(pm, pm, pm, dbsg, dbcv, dbcv, dbpl, dbpl, ln_g, ln_b, wm, wmt, sgb, conv_w, wp, pscale)


def _head_delta(o_ref, do_ref, t):
    prod = o_ref[...].astype(F32) * do_ref[...]
    lane = lax.broadcasted_iota(jnp.int32, (t, LANES), 1)
    out = []
    for h in range(N_HEADS):
        pair = prod[:, (h // 2) * LANES:(h // 2 + 1) * LANES]
        mine = (lane < D_V) if h % 2 == 0 else (lane >= D_V)
        out.append(jnp.broadcast_to(jnp.sum(jnp.where(mine, pair, 0.0), axis=1, keepdims=True), (t, LANES)))
    return out


def _head_do(do_ref, h, t):
    lane = lax.broadcasted_iota(jnp.int32, (t, LANES), 1)
    mine = (lane < D_V) if h % 2 == 0 else (lane >= D_V)
    pair = do_ref[:, (h // 2) * LANES:(h // 2 + 1) * LANES]
    return jnp.where(mine, pair, 0.0).astype(BF16)


def _flash_bwd(q, k, v, o, do, lse, t, riding=None):
    s = v.shape[0]
    n = s // t
    reps = t // LANES

    qi_tab, kj_tab = _causal_steps(n, key_major=True)
    n_steps = int(qi_tab.shape[0])

    def body(qi_ref, kj_ref, q_ref, k_ref, v_ref, o_ref, do_ref, lse_ref, *rest):
        if riding is None:
            dq_ref, dk_ref, dv_ref, dv_scr = rest
        else:
            x_ref, dq_ref, dk_ref, dv_ref, land_ref, dv_scr, send_sems, recv_sems, local_sem = rest
        step_id = pl.program_id(0)
        qi, kj = qi_ref[step_id], kj_ref[step_id]
        if riding is not None:
            at_end = _ride_along(riding, step_id, n_steps, (x_ref, land_ref, send_sems, recv_sems, local_sem))

        @pl.when(step_id == 0)
        def _():
            dq_ref[...] = jnp.zeros(dq_ref.shape, F32)

        @pl.when(qi == kj)
        def _():
            dk_ref[...] = jnp.zeros(dk_ref.shape, F32)
            dv_scr[...] = jnp.zeros(dv_scr.shape, F32)

        def step(masked):
            delta = _head_delta(o_ref, do_ref, t)
            q_rows = pl.ds(pl.multiple_of(qi * t, t), t)
            for h in range(N_HEADS):
                qh, kh = q_ref[h], k_ref[h]
                p = jnp.exp2(_causal_scores(qh, kh, masked) - jnp.tile(lse_ref[h], (1, reps)))
                pair = slice((h // 2) * LANES, (h // 2 + 1) * LANES)
                dv_scr[h] += _dot(p.astype(BF16), do_ref[:, pair].astype(BF16), TN)
                dp = _dot(_head_do(do_ref, h, t), v_ref[:, pair], NT)
                ds = (p * (dp - jnp.tile(delta[h], (1, reps))) * SM_SCALE).astype(BF16)
                dk_ref[h] += _dot(ds, qh, TN)
                dq_ref[h, q_rows, :] += _dot(ds, kh)

        @pl.when(qi > kj)
        def _():
            step(False)

        @pl.when(qi == kj)
        def _():
            step(True)

        @pl.when(qi == n - 1)
        def _():
            lane = lax.broadcasted_iota(jnp.int32, (t, LANES), 1)
            for pr in range(N_HEADS // 2):
                dv_ref[:, pr * LANES:(pr + 1) * LANES] = jnp.where(lane < D_V, dv_scr[2 * pr], dv_scr[2 * pr + 1]).astype(BF16)

        if riding is not None:
            at_end()

    q_blk = pl.BlockSpec((N_HEADS, t, HEAD_PAD), lambda i, qi, kj: (0, qi[i], 0))
    k_blk = pl.BlockSpec((N_HEADS, t, HEAD_PAD), lambda i, qi, kj: (0, kj[i], 0))
    v_blk = pl.BlockSpec((t, BR_W), lambda i, qi, kj: (kj[i], 0))
    o_blk = pl.BlockSpec((t, BR_W), lambda i, qi, kj: (qi[i], 0))
    extra = riding is not None
    dq_sds = _sds((N_HEADS, s, HEAD_PAD), F32)
    return _pcall(
        body, name="flash_bwd_exchanging" if extra else "flash_bwd", grid=(n_steps,), prefetch=2,
        in_specs=[q_blk, k_blk, v_blk, o_blk, o_blk, q_blk] + [HBM_REF] * extra,
        out_specs=[_whole(dq_sds.shape), k_blk, v_blk] + [HBM_REF] * extra,
        out_shape=[dq_sds, dq_sds, _sds((s, BR_W), BF16)] + ([riding[2]] if extra else []),
        scratch=[pltpu.VMEM((N_HEADS, t, LANES), F32)] + (list(EXCHANGE_SEMS) if extra else []),
        vmem=VMEM_LIMIT)(qi_tab, kj_tab, q, k, v, o, do, lse, *([riding[1]] if extra else []))


def _attn_prep_bwd(pa, dq, dk, dv, cosf, sins, qn, kvn, wq, wqs, wk, wv, t):
    s = pa.shape[0]

    def body(pa_ref, dq_ref, dk_ref, dv_ref, cos_ref, sin_ref, qn_ref, kvn_ref, wq_ref, wqs_ref, wk_ref, wv_ref,
             dpa_ref, dwq_ref, dwqs_ref, dwk_ref, dwv_ref, dqn_ref, dkvn_ref):
        first = pl.program_id(0) == 0
        cosv, sinv = cos_ref[...], sin_ref[...]
        cq, ckv = pa_ref[:, 0:Q_RANK], pa_ref[:, Q_RANK:Q_RANK + KV_RANK]
        cqn = _rms(cq, qn_ref[...]).astype(BF16)
        ckvn = _rms(ckv, kvn_ref[...]).astype(BF16)
        dcqn = jnp.zeros((t, Q_RANK), F32)
        dvv = dv_ref[...]
        dckvn = _dot(dvv, wv_ref[...], NT)
        _accumulate(dwv_ref, _dot(dvv, ckvn, TN), first)
        dk_rope = jnp.zeros((t, HEAD_PAD), F32)
        for h in range(N_HEADS):
            dqh = dq_ref[h]
            dqa, dqs = (dqh * cosv).astype(BF16), (dqh * sinv).astype(BF16)
            _accumulate(dwq_ref.at[h], _dot(dqa, cqn, TN), first)
            _accumulate(dwqs_ref.at[h], _dot(dqs, cqn, TN), first)
            dcqn = dcqn + _dot(dqa, wq_ref[h], NT) + _dot(dqs, wqs_ref[h], NT)
            dkh = dk_ref[h]
            dkb = dkh.astype(BF16)
            _accumulate(dwk_ref.at[h], _dot(dkb, ckvn, TN), first)
            dckvn = dckvn + _dot(dkb, wk_ref[h], NT)
            dk_rope = dk_rope + dkh
        dcq, dqn = _rms_bwd(cq, qn_ref[...], dcqn)
        dckv, dkvn = _rms_bwd(ckv, kvn_ref[...], dckvn)
        _accumulate(dqn_ref, dqn, first)
        _accumulate(dkvn_ref, dkvn, first)
        dpa_ref[:, 0:Q_RANK] = dcq.astype(BF16)
        dpa_ref[:, Q_RANK:Q_RANK + KV_RANK] = dckv.astype(BF16)
        dpa_ref[:, 384:512] = (dk_rope * cosv).astype(BF16)
        dpa_ref[:, 512:640] = (dk_rope * sinv).astype(BF16)

    head_blk = pl.BlockSpec((N_HEADS, t, HEAD_PAD), lambda i: (0, i, 0))
    wq_t, wk_t, wv_t = (N_HEADS, HEAD_PAD, Q_RANK), (N_HEADS, HEAD_PAD, KV_RANK), (N_HEADS * D_V, KV_RANK)
    return _pcall(
        body, name="attn_prep_bwd", grid=(s // t,),
        in_specs=[_rows(t, PA_W), head_blk, head_blk, _rows(t, BR_W), _rows(t, LANES), _rows(t, LANES),
                  _whole(qn.shape), _whole(kvn.shape), _whole(wq.shape), _whole(wqs.shape), _whole(wk.shape), _whole(wv.shape)],
        out_specs=[_rows(t, PA_W), _acc(wq_t), _acc(wq_t), _acc(wk_t), _acc(wv_t), _acc(qn.shape), _acc(kvn.shape)],
        out_shape=[_sds((s, PA_W), BF16), _sds(wq_t, F32), _sds(wq_t, F32), _sds(wk_t, F32),
                   _sds(wv_t, F32), _sds(qn.shape, F32), _sds(kvn.shape, F32)],
        vmem=VMEM_LIMIT)(pa, dq, dk, dv, cosf, sins, qn, kvn, wq, wqs, wk, wv)


def _inproj_bwd(x, g1, dxmid, dpa, dpm, dpres, w_in, wg, t):
    s = x.shape[0]

    def body(x_ref, g1_ref, dxm_ref, dpa_ref, dpm_ref, d0_ref, d1_ref, d2_ref, d3_ref, w_ref, wg_ref,
             dx_ref, dg1_ref, dwa_ref, dwm_ref):
        first = pl.program_id(0) == 0
        xv, g1v = x_ref[...], g1_ref[...]
        hb = _rms(xv, g1v).astype(BF16)
        dpav, dpmv = dpa_ref[...], dpm_ref[...]
        dh = _dot(dpav, w_ref[:, :PA_W], NT) + _dot(dpmv, w_ref[:, PA_W:], NT)
        for k, d_ref in enumerate((d0_ref, d1_ref, d2_ref, d3_ref)):
            dh = dh + _dot(d_ref[...], wg_ref[:, k * D_MODEL:(k + 1) * D_MODEL], NT)
        dx, dg1 = _rms_bwd(xv, g1v, dh)
        dx_ref[...] = dxm_ref[...] + dx
        _accumulate(dg1_ref, dg1, first)
        _accumulate(dwa_ref, _dot(dpav, hb, TN), first)
        _accumulate(dwm_ref, _dot(dpmv, hb, TN), first)

    return _pcall(
        body, name="inproj_bwd", grid=(s // t,),
        in_specs=[_rows(t, D_MODEL), _whole(g1.shape), _rows(t, D_MODEL), _rows(t, PA_W), _rows(t, PM_W)] +
                 [_rows(t, D_MODEL)] * 4 + [_whole(w_in.shape), _whole(wg.shape)],
        out_specs=[_rows(t, D_MODEL), _acc((1, D_MODEL)), _acc((PA_W, D_MODEL)), _acc((PM_W, D_MODEL))],
        out_shape=[_sds((s, D_MODEL), F32), _sds((1, D_MODEL), F32), _sds((PA_W, D_MODEL), F32), _sds((PM_W, D_MODEL), F32)],
        vmem=VMEM_LIMIT)(x, g1, dxmid, dpa, dpm, *dpres, w_in, wg)


def _my_place():
    return lax.axis_index("x"), lax.axis_index("y"), lax.axis_index("c")


def _flip(place, k):
    x, y, c = place
    return (1 - x if k & 4 else x, 1 - y if k & 2 else y, 1 - c if k & 1 else c)


def _rank(place):
    x, y, c = place
    return 4 * x + 2 * y + c


EXCHANGE_SEMS = (pltpu.SemaphoreType.DMA((7,)), pltpu.SemaphoreType.DMA((7,)), pltpu.SemaphoreType.DMA)
HBM_REF = pl.BlockSpec(memory_space=pl.ANY)


def _gather_plan(x_ref, out_ref, send_sems, recv_sems, local_sem):
    me = _my_place()
    sibling = _flip(me, 1)
    chips = (4, 2, 6)

    def copy(k, src_place, to, src=None):
        slot = out_ref.at[_rank(src_place)]
        return pltpu.make_async_remote_copy(
            src_ref=slot if src is None else src, dst_ref=slot, send_sem=send_sems.at[k], recv_sem=recv_sems.at[k],
            device_id=to, device_id_type=MESH_ID)

    mine = pltpu.make_async_copy(x_ref, out_ref.at[_rank(me)], local_sem)
    first = [copy(0, me, sibling, src=x_ref)] + [copy(1 + j, me, _flip(me, kc), src=x_ref) for j, kc in enumerate(chips)]
    passed = [copy(4 + j, _flip(me, kc), sibling) for j, kc in enumerate(chips)]

    def start():
        mine.start()
        for cp in first:
            cp.start()

    def forward():
        for j, kc in enumerate(chips):
            copy(1 + j, _flip(me, kc), me).wait_recv()
            passed[j].start()

    def finish():
        copy(0, sibling, me).wait_recv()
        for j, kc in enumerate(chips):
            copy(4 + j, _flip(sibling, kc), me).wait_recv()
        for cp in first + passed:
            cp.wait_send()
        mine.wait()

    return start, forward, finish


def _exchange_plan(x_ref, out_ref, send_sems, recv_sems, local_sem):
    me = _my_place()
    my_rank = _rank(me)
    mine = pltpu.make_async_copy(x_ref.at[my_rank], out_ref.at[my_rank], local_sem)
    sends = [pltpu.make_async_remote_copy(
        src_ref=x_ref.at[_rank(_flip(me, k))], dst_ref=out_ref.at[my_rank], send_sem=send_sems.at[k - 1],
        recv_sem=recv_sems.at[k - 1], device_id=_flip(me, k), device_id_type=MESH_ID) for k in range(1, N_DEV)]

    def start():
        mine.start()
        for cp in sends:
            cp.start()

    def forward():
        pass

    def finish():
        for k in range(1, N_DEV):
            slot = out_ref.at[_rank(_flip(me, k))]
            pltpu.make_async_remote_copy(
                src_ref=slot, dst_ref=slot, send_sem=send_sems.at[k - 1], recv_sem=recv_sems.at[k - 1],
                device_id=_flip(me, k), device_id_type=MESH_ID).wait_recv()
        for cp in sends:
            cp.wait_send()
        mine.wait()

    return start, forward, finish


def _exchange_alone(plan, src, out_sds, name):
    def body(x_ref, out_ref, send_sems, recv_sems, local_sem):
        start, forward, finish = plan(x_ref, out_ref, send_sems, recv_sems, local_sem)
        start()
        forward()
        finish()

    return pl.pallas_call(body, name=name, out_shape=out_sds, in_specs=[HBM_REF], out_specs=HBM_REF,
                          scratch_shapes=list(EXCHANGE_SEMS))(src)


def _all_gather(shard, name):
    return _exchange_alone(_gather_plan, shard, _sds((N_DEV,) + shard.shape, shard.dtype), name)


def _all_to_all(blocks, name):
    return _exchange_alone(_exchange_plan, blocks, _sds(blocks.shape, blocks.dtype), name)


def _riding(plan, src, out_sds):
    return (plan, src, out_sds)


def _ride_along(riding, step_id, n_steps, refs):
    start, forward, finish = riding[0](*refs)

    @pl.when(step_id == 0)
    def _():
        start()

    @pl.when(step_id == (3 * n_steps) // 4)
    def _():
        forward()

    def at_end():
        @pl.when(step_id == n_steps - 1)
        def _():
            finish()

    return at_end


def _sum_slots(parts, name):
    _, r, c = parts.shape
    t = math.gcd(r, ROW_PAD)

    def body(p_ref, o_ref):
        acc = p_ref[0].astype(F32)
        for d in range(1, N_DEV):
            acc = acc + p_ref[d].astype(F32)
        o_ref[...] = acc

    return _pcall(
        body, name=name, grid=(r // t,),
        in_specs=[pl.BlockSpec((N_DEV, t, c), lambda i: (0, i, 0))], out_specs=_rows(t, c),
        out_shape=_sds((r, c), F32))(parts)


def _adamw(w, g, m, v, name):
    r, c = w.shape
    t = r if r <= ROW_PAD else math.gcd(r, ROW_PAD)

    def body(w_ref, g_ref, m_ref, v_ref, d_ref, nm_ref, nv_ref):
        gv = g_ref[...]
        nm = ADAM_B1 * m_ref[...] + (1.0 - ADAM_B1) * gv
        nv = ADAM_B2 * v_ref[...] + (1.0 - ADAM_B2) * jnp.square(gv)
        m_hat = nm / (1.0 - ADAM_B1 ** ADAM_STEP)
        v_hat = nv / (1.0 - ADAM_B2 ** ADAM_STEP)
        d_ref[...] = -ADAM_LR * (m_hat / (jnp.sqrt(v_hat) + ADAM_EPS) + ADAM_WD * w_ref[...])
        nm_ref[...] = nm
        nv_ref[...] = nv

    return _pcall(
        body, name=name, grid=(r // t,),
        in_specs=[_rows(t, c)] * 4, out_specs=[_rows(t, c)] * 3, out_shape=[_sds((r, c), F32)] * 3)(w, g, m, v)


SHARDED = ("w_in", "w_uq", "w_ukv", "w_br_mla", "w_br_sg", "w_br_conv", "w_br_pool", "w_out", "w_ff1", "w_ff2", "conv_w")
ROW_SHARDED = ("w_out", "w_ff2")
REPLICATED = ("norm_mix_pre", "gate_b", "q_norm", "kv_norm", "sg_ln_g", "sg_ln_b", "sg_w", "sg_b", "pool_w",
              "pool_scale", "norm_mix_post", "norm_ffn_pre", "norm_ffn_post")
ROW_PAD = 256
GRAD_ROW_PAD = 64
PART_ROWS = 16


def _pack_rows(arrays, dtype, multiple, lead=()):
    rows, offsets, at = [], [], 0
    zero_pad = ((0, 0),) * len(lead)
    for a in arrays:
        flat = a.reshape(lead + (-1,)).astype(dtype)
        n = -(-flat.shape[-1] // D_MODEL)
        n_pad = -(-n // PART_ROWS) * PART_ROWS
        flat = jnp.pad(flat, zero_pad + ((0, n_pad * D_MODEL - flat.shape[-1]),))
        rows.append(flat.reshape(lead + (n_pad, D_MODEL)))
        offsets.append((at, n))
        at += n_pad
    pad = -at % multiple
    if pad:
        rows.append(jnp.zeros(lead + (pad, D_MODEL), dtype))
    return jnp.concatenate(rows, axis=len(lead)), offsets


def _unpack_rows(buf, offsets, shapes):
    lead = buf.shape[:-2]
    out = []
    for (at, n), shape in zip(offsets, shapes):
        size = math.prod(shape)
        flat = buf[..., at:at + n, :].reshape(lead + (n * D_MODEL,))[..., :size]
        out.append(flat.reshape(lead + tuple(shape)))
    return out


def _join_shards(name, gathered):
    if name in ROW_SHARDED:
        return gathered.reshape(-1, gathered.shape[-1])
    g = jnp.moveaxis(gathered, 0, -2)
    return g.reshape(g.shape[:-2] + (g.shape[-2] * g.shape[-1],))


def _pack_layer_shards(p, l):
    arrays = [lax.bitcast_convert_type(p[n][0][l], BF16) if n == "conv_w" else p[n][0][l].astype(BF16) for n in SHARDED]
    packed, offsets = _pack_rows(arrays, BF16, GRAD_ROW_PAD)
    return packed, offsets, [a.shape for a in arrays]


def _unpack_layer_weights(gathered, offsets, shapes):
    full = {}
    for n, gth in zip(SHARDED, _unpack_rows(gathered, offsets, shapes)):
        if n == "conv_w":
            gth = lax.bitcast_convert_type(gth, F32)
        full[n] = _join_shards(n, gth)
    return full


def _owner_major(grad_sharded_dim_first):
    a = grad_sharded_dim_first
    return a.reshape((N_DEV, a.shape[0] // N_DEV) + a.shape[1:])


def _owner_block_shape(name, shard_shape):
    k, n = shard_shape[-2:]
    return (k, n) if name in ROW_SHARDED else (n, k)


def _natural_shard(name, block):
    return block if name in ROW_SHARDED else block.T


def _swap_halves(a, axis=-1):
    lo, hi = jnp.split(a, 2, axis=axis)
    return jnp.concatenate([hi, lo], axis=axis)


def _pad_cols(a, left, total):
    return jnp.pad(a, ((0, 0),) * (a.ndim - 1) + ((left, total - left - a.shape[-1]),))


def _layer_weights(full, small, l):
    w_in = full["w_in"]
    k_r = w_in[:, 384:416]
    w_in_k = jnp.concatenate(
        [w_in[:, 0:384], _pad_cols(k_r, D_NOPE, HEAD_PAD), _pad_cols(_swap_halves(k_r), D_NOPE, HEAD_PAD), w_in[:, 416:1952]], axis=1)
    w_uq = full["w_uq"].reshape(Q_RANK, N_HEADS, D_NOPE + D_ROPE)
    wq = jnp.moveaxis(_pad_cols(w_uq, 0, HEAD_PAD), 1, 0)
    wqs = jnp.moveaxis(_pad_cols(_swap_halves(w_uq[..., D_NOPE:]), D_NOPE, HEAD_PAD), 1, 0)
    w_ukv = full["w_ukv"].reshape(KV_RANK, N_HEADS, D_NOPE + D_V)
    wk = jnp.moveaxis(_pad_cols(w_ukv[..., :D_NOPE], 0, HEAD_PAD), 1, 0)
    wv = w_ukv[..., D_NOPE:].reshape(KV_RANK, N_HEADS * D_V)
    tri = jnp.tril(jnp.ones((SG_CHUNK, SG_CHUNK), bool))
    wm = jnp.where(tri, small["sg_w"][l], 0.0)
    pool_w = small["pool_w"][l]
    wp = jnp.zeros((BR_W, BR_W), F32)
    for g in range(len(POOL_WINDOWS)):
        wp = wp.at[g * POOL_GROUP:(g + 1) * POOL_GROUP, g * POOL_GROUP:(g + 1) * POOL_GROUP].set(pool_w[g])
    vec = lambda name: small[name][l][None, :]
    return dict(
        w_in=w_in_k, wg=w_in[:, 1952:], bg=vec("gate_b"), g1=vec("norm_mix_pre"), qn=vec("q_norm"), kvn=vec("kv_norm"),
        wq=wq, wqs=wqs, wk=wk, wv=wv,
        ln_g=vec("sg_ln_g"), ln_b=vec("sg_ln_b"), wm=wm.astype(BF16), wmt=jnp.swapaxes(wm, 1, 2).astype(BF16),
        sgb=jnp.repeat(small["sg_b"][l].T, POOL_GROUP, axis=1), conv_w=full["conv_w"],
        wp=wp.astype(BF16), pscale=vec("pool_scale"),
        wbr=jnp.stack([full["w_br_mla"], full["w_br_sg"], full["w_br_conv"], full["w_br_pool"]]),
        wout=full["w_out"], g2=vec("norm_mix_post"), g3=vec("norm_ffn_pre"), w1=full["w_ff1"], w2=full["w_ff2"],
        g4=vec("norm_ffn_post"))


def _layer_grads(g):
    dwa = g["dwa"]
    rope_rows = slice(D_NOPE, D_NOPE + D_ROPE)
    d_kr = dwa[384:512][rope_rows] + _swap_halves(dwa[512:640][rope_rows], axis=0)
    w_in = jnp.concatenate([dwa[0:384], d_kr, g["dwm_in"]] + list(g["dwg"]), axis=0)
    dwq, dwqs = g["dwq"], g["dwqs"]
    d_rope = dwq[:, rope_rows] + _swap_halves(dwqs[:, rope_rows], axis=1)
    w_uq = jnp.concatenate([dwq[:, :D_NOPE], d_rope], axis=1).reshape(N_HEADS * (D_NOPE + D_ROPE), Q_RANK)
    w_ukv = jnp.concatenate([g["dwk"][:, :D_NOPE], g["dwv"].reshape(N_HEADS, D_V, KV_RANK)], axis=1)
    w_ukv = w_ukv.reshape(N_HEADS * (D_NOPE + D_V), KV_RANK)
    pool_w = jnp.stack([g["dwp"][i * POOL_GROUP:(i + 1) * POOL_GROUP, i * POOL_GROUP:(i + 1) * POOL_GROUP]
                        for i in range(len(POOL_WINDOWS))])
    sg_b = g["dsgb"].reshape(SG_CHUNK, SG_GROUPS, POOL_GROUP).sum(axis=-1).T
    own = _owner_major
    return dict(
        w_in=own(w_in), w_uq=own(w_uq), w_ukv=own(w_ukv), w_br_mla=own(g["dwbr"][0]), w_br_sg=own(g["dwbr"][1]),
        w_br_conv=own(g["dwbr"][2]), w_br_pool=own(g["dwbr"][3]), w_out=own(g["dwout"]), w_ff1=own(g["dw1"]),
        w_ff2=own(g["dw2"]), conv_w=own(g["dcw"][:CONV_K].T),
        norm_mix_pre=g["dg1"][0], gate_b=jnp.concatenate([b[0] for b in g["dbg"]]), q_norm=g["dqn"][0], kv_norm=g["dkvn"][0],
        sg_ln_g=g["dlng"][0], sg_ln_b=g["dlnb"][0], sg_w=g["dwm"], sg_b=sg_b, pool_w=pool_w, pool_scale=g["dps"][0],
        norm_mix_post=g["dg2"][0], norm_ffn_pre=g["dg3"][0], norm_ffn_post=g["dg4"][0])


def _rope_tables(positions):
    inv_freq = ROPE_BASE ** (-jnp.arange(0, D_ROPE, 2, dtype=F32) / D_ROPE)
    ang = positions.astype(F32)[:, None] * inv_freq
    cos, sin = jnp.cos(ang), jnp.sin(ang)
    s = positions.shape[0]
    cosf = jnp.concatenate([jnp.ones((s, D_NOPE), F32), cos, cos, jnp.zeros((s, HEAD_PAD - D_NOPE - D_ROPE), F32)], axis=1)
    sins = jnp.concatenate([jnp.zeros((s, D_NOPE), F32), -sin, sin, jnp.zeros((s, HEAD_PAD - D_NOPE - D_ROPE), F32)], axis=1)
    return cosf, sins


def _layer_fwd(x, w, cosf, sins, tiles, riding=None):
    t, ta = tiles["tok"], tiles["attn"]
    hb, pa, pm = _inproj_fwd(x, w["g1"], w["w_in"], t)
    q, k, v = _attn_prep_fwd(pa, cosf, sins, w["qn"], w["kvn"], w["wq"], w["wqs"], w["wk"], w["wv"], t)
    o, lse, *landed = _flash_fwd(q, k, v, ta, riding)
    bsg, bcv, bpl = _mixers_fwd(pm, w["ln_g"], w["ln_b"], w["wm"], w["sgb"], w["conv_w"], w["wp"], w["pscale"], t)
    x_mid, merged, tt = _merge_fwd(x, hb, (o, bsg, bcv, bpl), w["wg"], w["bg"], w["wbr"], w["wout"], w["g2"], t)
    x_out, f = _ffn_fwd(x_mid, w["g3"], w["w1"], w["w2"], w["g4"], t)
    saved = dict(x=x, hb=hb, pa=pa, pm=pm, q=q, k=k, v=v, o=o, lse=lse, bsg=bsg, bcv=bcv, bpl=bpl, x_mid=x_mid,
                 merged=merged, tt=tt, f=f)
    return x_out, saved, (landed[0] if landed else None)


def _layer_bwd(dxo, a, w, cosf, sins, tiles, riding=None):
    t, ta, tb = tiles["tok"], tiles["attn"], tiles["inproj_bwd"]
    g = {}
    h2, df, g["dg4"] = _ffn_bwd_norms(a["x_mid"], a["f"], dxo, w["g3"], w["g4"], t)
    da, g["dw1"], g["dw2"] = _ffn_bwd_weights(h2, df, w["w1"], w["w2"], t)
    dxmid, g["dg3"] = _ffn_bwd_input(da, w["w1"], a["x_mid"], dxo, w["g3"], t)
    dm, g["dwout"], g["dg2"] = _merge_bwd_out(a["tt"], dxmid, w["g2"], w["wout"], a["merged"], t)
    dbrs, dpres, g["dwg"], g["dbg"], g["dwbr"] = [], [], [], [], []
    for k, br in enumerate((a["o"], a["bsg"], a["bcv"], a["bpl"])):
        dbr, dpre, dwg, dbg, dwbr = _merge_bwd_branch(k, a["hb"], br, dm, w["wg"], w["bg"], w["wbr"], t)
        dbrs.append(dbr); dpres.append(dpre); g["dwg"].append(dwg); g["dbg"].append(dbg); g["dwbr"].append(dwbr)
    dpm, g["dlng"], g["dlnb"], g["dwm"], g["dsgb"], g["dcw"], g["dwp"], g["dps"] = _mixers_bwd(
        a["pm"], dbrs[1], dbrs[2], dbrs[3], w["ln_g"], w["ln_b"], w["wm"], w["wmt"], w["sgb"], w["conv_w"], w["wp"], w["pscale"], t)
    dq, dk, dv, *landed = _flash_bwd(a["q"], a["k"], a["v"], a["o"], dbrs[0], a["lse"], ta, riding)
    dpa, g["dwq"], g["dwqs"], g["dwk"], g["dwv"], g["dqn"], g["dkvn"] = _attn_prep_bwd(
        a["pa"], dq, dk, dv, cosf, sins, w["qn"], w["kvn"], w["wq"], w["wqs"], w["wk"], w["wv"], t)
    dx, g["dg1"], g["dwa"], g["dwm_in"] = _inproj_bwd(a["x"], w["g1"], dxmid, dpa, dpm, dpres, w["w_in"], w["wg"], tb)
    return dx, _layer_grads(g), (landed[0] if landed else None)


def _step(p, x, positions, loss_target):
    s = x.shape[0]
    depth = p["w_in"][0].shape[0]
    tiles = dict(tok=min(TOK_TILE, s), attn=min(ATTN_TILE, s), inproj_bwd=min(INPROJ_BWD_TILE, s))

    small = {n: p[n][0] for n in REPLICATED}
    cosf, sins = _rope_tables(positions)

    packed = [_pack_layer_shards(p, l) for l in range(depth)]
    gathered = _all_gather(packed[0][0], "weight_all_gather")
    weights, acts = [], []
    h = x
    for l in range(depth):
        weights.append(_layer_weights(_unpack_layer_weights(gathered, *packed[l][1:]), small, l))
        riding = None
        if l + 1 < depth:
            nxt = packed[l + 1][0]
            riding = _riding(_gather_plan, nxt, _sds((N_DEV,) + nxt.shape, nxt.dtype))
        h, saved, gathered = _layer_fwd(h, weights[l], cosf, sins, tiles, riding)
        acts.append(saved)
    dh, loss_blk = _loss_head(h, loss_target, tiles["tok"])
    loss = lax.psum(loss_blk[0, 0], ("x", "y", "c"))

    layer_grads = [None] * depth
    shard_blocks = [None] * depth
    block_shapes = [_owner_block_shape(n, p[n][0].shape) for n in SHARDED]
    send = None
    for l in reversed(range(depth)):
        riding = None if send is None else _riding(_exchange_plan, send, _sds(send.shape, send.dtype))
        dh, layer_grads[l], landed = _layer_bwd(dh, acts[l], weights[l], cosf, sins, tiles, riding)
        if landed is not None:
            shard_blocks[l + 1] = _unpack_rows(_sum_slots(landed, "grad_shard_sum"), send_offsets, block_shapes)
        send, send_offsets = _pack_rows([layer_grads[l][n] for n in SHARDED], BF16, GRAD_ROW_PAD, lead=(N_DEV,))
    landed = _all_to_all(send, "grad_all_to_all")
    shard_blocks[0] = _unpack_rows(_sum_slots(landed, "grad_shard_sum"), send_offsets, block_shapes)
    g_shard = {n: jnp.stack([_natural_shard(n, shard_blocks[l][i]) for l in range(depth)]) for i, n in enumerate(SHARDED)}
    small_grads = [jnp.stack([layer_grads[l][n] for l in range(depth)]) for n in REPLICATED]
    small_rows, small_offsets = _pack_rows(small_grads, F32, ROW_PAD)
    g_small_rows = _sum_slots(_all_gather(small_rows, "replicated_grad_all_gather"), "grad_replicated_sum")

    grad, delta, new_m, new_v = {}, {}, {}, {}
    for n in SHARDED:
        w, m, v = p[n]
        shape2 = (math.prod(w.shape[:-1]), w.shape[-1])
        grad[n] = g_shard[n]
        d, nm, nv = _adamw(w.reshape(shape2), grad[n].reshape(shape2), m.reshape(shape2), v.reshape(shape2), f"adamw_{n}")
        delta[n], new_m[n], new_v[n] = d.reshape(w.shape), nm.reshape(w.shape), nv.reshape(w.shape)
    packs = [_pack_rows([p[n][i] for n in REPLICATED], F32, ROW_PAD)[0] for i in range(3)]
    d, nm, nv = _adamw(packs[0], g_small_rows, packs[1], packs[2], "adamw_replicated")
    shapes = [p[n][0].shape for n in REPLICATED]
    for n, gg, dd, mm, vv in zip(REPLICATED, _unpack_rows(g_small_rows, small_offsets, shapes), _unpack_rows(d, small_offsets, shapes),
                                 _unpack_rows(nm, small_offsets, shapes), _unpack_rows(nv, small_offsets, shapes)):
        grad[n], delta[n], new_m[n], new_v[n] = gg, dd, mm, vv
    return loss, dh, grad, delta, new_m, new_v


WEIGHT_ORDER = ("norm_mix_pre", "w_in", "gate_b", "q_norm", "w_uq", "kv_norm", "w_ukv", "w_br_mla", "sg_ln_g", "sg_ln_b",
                "sg_w", "sg_b", "w_br_sg", "conv_w", "w_br_conv", "pool_w", "pool_scale", "w_br_pool", "w_out",
                "norm_mix_post", "norm_ffn_pre", "w_ff1", "w_ff2", "norm_ffn_post")


def kernel(x, positions, norm_mix_pre, w_in, gate_b, q_norm, w_uq, kv_norm, w_ukv, w_br_mla, sg_ln_g, sg_ln_b, sg_w, sg_b, w_br_sg, conv_w, w_br_conv, pool_w, pool_scale, w_br_pool, w_out, norm_mix_post, norm_ffn_pre, w_ff1, w_ff2, norm_ffn_post, loss_target, m_norm_mix_pre, m_w_in, m_gate_b, m_q_norm, m_w_uq, m_kv_norm, m_w_ukv, m_w_br_mla, m_sg_ln_g, m_sg_ln_b, m_sg_w, m_sg_b, m_w_br_sg, m_conv_w, m_w_br_conv, m_pool_w, m_pool_scale, m_w_br_pool, m_w_out, m_norm_mix_post, m_norm_ffn_pre, m_w_ff1, m_w_ff2, m_norm_ffn_post, v_norm_mix_pre, v_w_in, v_gate_b, v_q_norm, v_w_uq, v_kv_norm, v_w_ukv, v_w_br_mla, v_sg_ln_g, v_sg_ln_b, v_sg_w, v_sg_b, v_w_br_sg, v_conv_w, v_w_br_conv, v_pool_w, v_pool_scale, v_w_br_pool, v_w_out, v_norm_mix_post, v_norm_ffn_pre, v_w_ff1, v_w_ff2, v_norm_ffn_post):
    ws = (norm_mix_pre, w_in, gate_b, q_norm, w_uq, kv_norm, w_ukv, w_br_mla, sg_ln_g, sg_ln_b, sg_w, sg_b, w_br_sg, conv_w,
          w_br_conv, pool_w, pool_scale, w_br_pool, w_out, norm_mix_post, norm_ffn_pre, w_ff1, w_ff2, norm_ffn_post)
    ms = (m_norm_mix_pre, m_w_in, m_gate_b, m_q_norm, m_w_uq, m_kv_norm, m_w_ukv, m_w_br_mla, m_sg_ln_g, m_sg_ln_b, m_sg_w,
          m_sg_b, m_w_br_sg, m_conv_w, m_w_br_conv, m_pool_w, m_pool_scale, m_w_br_pool, m_w_out, m_norm_mix_post,
          m_norm_ffn_pre, m_w_ff1, m_w_ff2, m_norm_ffn_post)
    vs = (v_norm_mix_pre, v_w_in, v_gate_b, v_q_norm, v_w_uq, v_kv_norm, v_w_ukv, v_w_br_mla, v_sg_ln_g, v_sg_ln_b, v_sg_w,
          v_sg_b, v_w_br_sg, v_conv_w, v_w_br_conv, v_pool_w, v_pool_scale, v_w_br_pool, v_w_out, v_norm_mix_post,
          v_norm_ffn_pre, v_w_ff1, v_w_ff2, v_norm_ffn_post)
    p = {n: (w, m, v) for n, w, m, v in zip(WEIGHT_ORDER, ws, ms, vs)}
    loss, grad_x, grad, delta, new_m, new_v = _step(p, x[0], positions[0], loss_target[0])
    return (loss, grad_x[None], *[grad[n] for n in WEIGHT_ORDER], *[delta[n] for n in WEIGHT_ORDER],
            *[new_m[n] for n in WEIGHT_ORDER], *[new_v[n] for n in WEIGHT_ORDER])
```

```python
import functools
import math

import jax
import jax.numpy as jnp
from jax import lax
from jax.experimental import pallas as pl
from jax.experimental.pallas import tpu as pltpu

F32 = jnp.float32
BF16 = jnp.bfloat16

D_MODEL = 1024
N_HEADS = 4
D_NOPE = 64
D_ROPE = 32
D_V = 64
Q_RANK = 256
KV_RANK = 128
BR_W = 256
SG_CHUNK = 128
SG_GROUPS = 4
POOL_WINDOWS = (2, 4, 8, 16)
POOL_GROUP = 64
CONV_K = 3
D_FF = 4096
N_BRANCH = 4
N_IN = 6048
EPS = 1e-6
ROPE_BASE = 10000.0
ADAM_LR = 0.001
ADAM_B1 = 0.9
ADAM_B2 = 0.999
ADAM_EPS = 1e-08
ADAM_WD = 0.01
ADAM_STEP = 10

N_DEV = 8
LANES = 128
HEAD_PAD = 128
HALO = 16
PA_W = 640
PM_W = 6 * BR_W
VMEM_LIMIT = 56 * 1024 * 1024
SM_SCALE = (D_NOPE + D_ROPE) ** -0.5
SM_SCALE_LOG2 = SM_SCALE * math.log2(math.e)
TOK_TILE = 512
ATTN_TILE = 512
WGRAD_TILE = 1024
INPROJ_BWD_TILE = 512

NN = (((1,), (0,)), ((), ()))
NT = (((1,), (1,)), ((), ()))
TN = (((0,), (0,)), ((), ()))
MESH_ID = pl.DeviceIdType.MESH


def _dot(a, b, dims=NN):
    return lax.dot_general(a, b, dims, preferred_element_type=F32)


def _pcall(body, *, name, grid, in_specs, out_specs, out_shape, scratch=(), vmem=None, prefetch=0):
    params = pltpu.CompilerParams(vmem_limit_bytes=vmem)
    if prefetch:
        spec = pltpu.PrefetchScalarGridSpec(num_scalar_prefetch=prefetch, grid=grid, in_specs=in_specs,
                                            out_specs=out_specs, scratch_shapes=scratch)
        return pl.pallas_call(body, name=name, grid_spec=spec, out_shape=out_shape, compiler_params=params)
    return pl.pallas_call(
        body, name=name, grid=grid, in_specs=in_specs, out_specs=out_specs, out_shape=out_shape,
        scratch_shapes=scratch, compiler_params=params)


def _rows(t, width):
    return pl.BlockSpec((t, width), lambda i: (i, 0))


def _whole(shape):
    nd = len(shape)
    return pl.BlockSpec(tuple(shape), lambda *_: (0,) * nd, pipeline_mode=pl.Buffered(1))


def _acc(shape):
    nd = len(shape)
    return pl.BlockSpec(tuple(shape), lambda *_: (0,) * nd)


def _sds(shape, dtype):
    return jax.ShapeDtypeStruct(tuple(shape), dtype)


def _rms(x, g):
    return x * lax.rsqrt(jnp.mean(x * x, axis=-1, keepdims=True) + EPS) * g


def _rms_bwd(x, g, dy):
    r = lax.rsqrt(jnp.mean(x * x, axis=-1, keepdims=True) + EPS)
    xh = x * r
    dg = jnp.sum(dy * xh, axis=0, keepdims=True)
    dxh = dy * g
    dx = r * (dxh - xh * jnp.mean(dxh * xh, axis=-1, keepdims=True))
    return dx, dg


def _sigmoid(x):
    return 1.0 / (1.0 + jnp.exp(-x))


def _gelu(x):
    return jax.nn.gelu(x, approximate=True)


def _accumulate(ref, val, first):
    @pl.when(first)
    def _():
        ref[...] = val

    @pl.when(jnp.logical_not(first))
    def _():
        ref[...] += val


def _accumulate_out(acc_ref, out_ref, val, first, last):
    _accumulate(acc_ref, val, first)

    @pl.when(last)
    def _():
        out_ref[...] = acc_ref[...].astype(out_ref.dtype)


def _inproj_fwd(x, g1, w_in, t):
    s = x.shape[0]

    def body(x_ref, g_ref, w_ref, hb_ref, pa_ref, pm_ref):
        hb = _rms(x_ref[...], g_ref[...]).astype(BF16)
        hb_ref[...] = hb
        pa_ref[...] = _dot(hb, w_ref[:, :PA_W])
        pm_ref[...] = _dot(hb, w_ref[:, PA_W:])

    return _pcall(
        body, name="inproj_fwd", grid=(s // t,),
        in_specs=[_rows(t, D_MODEL), _whole((1, D_MODEL)), _whole(w_in.shape)],
        out_specs=[_rows(t, D_MODEL), _rows(t, PA_W), _rows(t, PM_W)],
        out_shape=[_sds((s, D_MODEL), BF16), _sds((s, PA_W), F32), _sds((s, PM_W), F32)],
        vmem=VMEM_LIMIT)(x, g1, w_in)


def _attn_prep_fwd(pa, cosf, sins, qn, kvn, wq, wqs, wk, wv, t):
    s = pa.shape[0]

    def body(pa_ref, cos_ref, sin_ref, qn_ref, kvn_ref, wq_ref, wqs_ref, wk_ref, wv_ref, q_ref, k_ref, v_ref):
        cosv, sinv = cos_ref[...], sin_ref[...]
        cqn = _rms(pa_ref[:, 0:Q_RANK], qn_ref[...]).astype(BF16)
        ckvn = _rms(pa_ref[:, Q_RANK:Q_RANK + KV_RANK], kvn_ref[...]).astype(BF16)
        k_rope = pa_ref[:, 384:512] * cosv + pa_ref[:, 512:640] * sinv
        for h in range(N_HEADS):
            q = _dot(cqn, wq_ref[h]) * cosv + _dot(cqn, wqs_ref[h]) * sinv
            q_ref[h] = q.astype(BF16)
            k_ref[h] = (_dot(ckvn, wk_ref[h]) + k_rope).astype(BF16)
        v_ref[...] = _dot(ckvn, wv_ref[...]).astype(BF16)

    head_blk = pl.BlockSpec((N_HEADS, t, HEAD_PAD), lambda i: (0, i, 0))
    return _pcall(
        body, name="attn_prep_fwd", grid=(s // t,),
        in_specs=[_rows(t, PA_W), _rows(t, LANES), _rows(t, LANES), _whole(qn.shape), _whole(kvn.shape),
                  _whole(wq.shape), _whole(wqs.shape), _whole(wk.shape), _whole(wv.shape)],
        out_specs=[head_blk, head_blk, _rows(t, BR_W)],
        out_shape=[_sds((N_HEADS, s, HEAD_PAD), BF16), _sds((N_HEADS, s, HEAD_PAD), BF16), _sds((s, BR_W), BF16)],
        vmem=VMEM_LIMIT)(pa, cosf, sins, qn, kvn, wq, wqs, wk, wv)


def _causal_scores(q, k, masked):
    sc = _dot(q, k, NT) * SM_SCALE_LOG2
    if masked:
        row = lax.broadcasted_iota(jnp.int32, sc.shape, 0)
        col = lax.broadcasted_iota(jnp.int32, sc.shape, 1)
        sc = jnp.where(col <= row, sc, -jnp.inf)
    return sc


def _causal_steps(n, key_major):
    if key_major:
        pairs = [(qi, kj) for kj in range(n) for qi in range(kj, n)]
    else:
        pairs = [(qi, kj) for qi in range(n) for kj in range(qi + 1)]
    return (jnp.asarray([p[0] for p in pairs], jnp.int32), jnp.asarray([p[1] for p in pairs], jnp.int32))


def _flash_fwd(q, k, v, t, riding=None):
    s = v.shape[0]
    n = s // t
    reps = t // LANES
    qi_tab, kj_tab = _causal_steps(n, key_major=False)
    n_steps = int(qi_tab.shape[0])

    def body(qi_ref, kj_ref, q_ref, k_ref, v_ref, *rest):
        if riding is None:
            o_ref, lse_ref, m_scr, l_scr, acc_scr = rest
        else:
            x_ref, o_ref, lse_ref, land_ref, m_scr, l_scr, acc_scr, send_sems, recv_sems, local_sem = rest
        step_id = pl.program_id(0)
        qi, kj = qi_ref[step_id], kj_ref[step_id]
        if riding is not None:
            at_end = _ride_along(riding, step_id, n_steps, (x_ref, land_ref, send_sems, recv_sems, local_sem))

        @pl.when(kj == 0)
        def _():
            m_scr[...] = jnp.full(m_scr.shape, -jnp.inf, F32)
            l_scr[...] = jnp.zeros(l_scr.shape, F32)
            acc_scr[...] = jnp.zeros(acc_scr.shape, F32)

        def step(masked):
            for h in range(N_HEADS):
                sc = _causal_scores(q_ref[h], k_ref[h], masked)
                m_prev = m_scr[h]
                m_next = jnp.maximum(m_prev, jnp.max(sc, axis=1, keepdims=True))
                alpha = jnp.exp2(m_prev - m_next)
                p = jnp.exp2(sc - jnp.tile(m_next, (1, reps)))
                l_scr[h] = alpha * l_scr[h] + jnp.sum(p, axis=1, keepdims=True)
                m_scr[h] = m_next
                pair = (h // 2) * LANES
                acc_scr[h] = acc_scr[h] * alpha + _dot(p.astype(BF16), v_ref[:, pair:pair + LANES])

        @pl.when(kj < qi)
        def _():
            step(False)

        @pl.when(kj == qi)
        def _():
            step(True)
            lane = lax.broadcasted_iota(jnp.int32, (t, LANES), 1)
            for pr in range(N_HEADS // 2):
                o0 = acc_scr[2 * pr] / l_scr[2 * pr]
                o1 = acc_scr[2 * pr + 1] / l_scr[2 * pr + 1]
                o_ref[:, pr * LANES:(pr + 1) * LANES] = jnp.where(lane < D_V, o0, o1).astype(BF16)
            for h in range(N_HEADS):
                lse_ref[h] = m_scr[h] + jnp.log2(l_scr[h])

        if riding is not None:
            at_end()

    q_blk = pl.BlockSpec((N_HEADS, t, HEAD_PAD), lambda i, qi, kj: (0, qi[i], 0))
    k_blk = pl.BlockSpec((N_HEADS, t, HEAD_PAD), lambda i, qi, kj: (0, kj[i], 0))
    v_blk = pl.BlockSpec((t, BR_W), lambda i, qi, kj: (kj[i], 0))
    extra = riding is not None
    return _pcall(
        body, name="flash_fwd_gathering" if extra else "flash_fwd", grid=(n_steps,), prefetch=2,
        in_specs=[q_blk, k_blk, v_blk] + [HBM_REF] * extra,
        out_specs=[pl.BlockSpec((t, BR_W), lambda i, qi, kj: (qi[i], 0)), q_blk] + [HBM_REF] * extra,
        out_shape=[_sds((s, BR_W), BF16), _sds((N_HEADS, s, LANES), F32)] + ([riding[2]] if extra else []),
        scratch=[pltpu.VMEM((N_HEADS, t, LANES), F32)] * 3 + (list(EXCHANGE_SEMS) if extra else []),
        vmem=VMEM_LIMIT)(qi_tab, kj_tab, q, k, v, *([riding[1]] if extra else []))


def _shift_down(ext, k, t):
    return pltpu.roll(ext, k, 0)[HALO:HALO + t]


def _shift_up(ext, k, t):
    return pltpu.roll(ext, t + HALO - k, 0)[0:t]


def _lane_group(shape):
    return lax.shift_right_logical(lax.broadcasted_iota(jnp.int32, shape, 1), 6)


def _group_select(vals):
    grp = _lane_group(vals[0].shape)
    out = vals[0]
    for g in range(1, len(vals)):
        out = jnp.where(grp == g, vals[g], out)
    return out


def _layernorm(x, g, b):
    mu = jnp.mean(x, axis=-1, keepdims=True)
    xc = x - mu
    return xc * lax.rsqrt(jnp.mean(xc * xc, axis=-1, keepdims=True) + EPS) * g + b


def _sg_mix(wm_ref, vnb, bias):
    return _group_select([_dot(wm_ref[g], vnb) for g in range(SG_GROUPS)]) + bias


def _pool_windows(ext):
    s2 = ext + pltpu.roll(ext, 1, 0)
    s4 = s2 + pltpu.roll(s2, 2, 0)
    s8 = s4 + pltpu.roll(s4, 4, 0)
    s16 = s8 + pltpu.roll(s8, 8, 0)
    return [s2, s4, s8, s16]


def _pool_counts(tok):
    return [jnp.minimum(tok + 1, w).astype(F32) for w in POOL_WINDOWS]


def _halo_before(t):
    return pl.BlockSpec((HALO, PM_W), lambda i: (jnp.maximum(i * (t // HALO) - 1, 0), 0))


def _mixers_fwd(pm, ln_g, ln_b, wm, sgb, conv_w, wp, pscale, t):
    s = pm.shape[0]

    def body(pm_ref, halo_ref, lng_ref, lnb_ref, wm_ref, sgb_ref, cw_ref, wp_ref, ps_ref, bsg_ref, bcv_ref, bpl_ref):
        i = pl.program_id(0)
        halo = jnp.where(i > 0, halo_ref[...], 0.0)
        u = _gelu(pm_ref[:, 0:256])
        vnb = _layernorm(_gelu(pm_ref[:, 256:512]), lng_ref[...], lnb_ref[...]).astype(BF16)
        for c in range(t // SG_CHUNK):
            rows = slice(c * SG_CHUNK, (c + 1) * SG_CHUNK)
            bsg_ref[rows, :] = (u[rows] * _sg_mix(wm_ref, vnb[rows], sgb_ref[...])).astype(BF16)
        z = pm_ref[:, 1024:1280] * pm_ref[:, 512:768]
        zext = jnp.concatenate([halo[:, 1024:1280] * halo[:, 512:768], z], axis=0)
        y = cw_ref[0:1, :] * _shift_down(zext, 2, t) + cw_ref[1:2, :] * _shift_down(zext, 1, t) + cw_ref[2:3, :] * z
        bcv_ref[...] = (pm_ref[:, 768:1024] * y).astype(BF16)
        p = pm_ref[:, 1280:1536]
        sums = _pool_windows(jnp.concatenate([halo[:, 1280:1536], p], axis=0))
        tok = i * t + lax.broadcasted_iota(jnp.int32, (t, 1), 0)
        pooled = _group_select([sw[HALO:HALO + t] / cnt - p for sw, cnt in zip(sums, _pool_counts(tok))])
        bpl_ref[...] = (_dot(pooled.astype(BF16), wp_ref[...]) * ps_ref[...]).astype(BF16)

    return _pcall(
        body, name="mixers_fwd", grid=(s // t,),
        in_specs=[_rows(t, PM_W), _halo_before(t), _whole(ln_g.shape), _whole(ln_b.shape), _whole(wm.shape),
                  _whole(sgb.shape), _whole(conv_w.shape), _whole(wp.shape), _whole(pscale.shape)],
        out_specs=[_rows(t, BR_W)] * 3,
        out_shape=[_sds((s, BR_W), BF16)] * 3,
        vmem=VMEM_LIMIT)(pm, pm, ln_g, ln_b, wm, sgb, conv_w, wp, pscale)


def _merge_fwd(x, hb, branches, wg, bg, wbr, wout, g2, t):
    s = x.shape[0]

    def body(x_ref, hb_ref, b0_ref, b1_ref, b2_ref, b3_ref, wg_ref, bg_ref, wbr_ref, wout_ref, g2_ref,
             xmid_ref, mrg_ref, t_ref):
        hb = hb_ref[...]
        merged = jnp.zeros((t, D_MODEL), F32)
        for k, b_ref in enumerate((b0_ref, b1_ref, b2_ref, b3_ref)):
            cols = slice(k * D_MODEL, (k + 1) * D_MODEL)
            gate = _sigmoid(_dot(hb, wg_ref[:, cols]) + bg_ref[:, cols])
            merged = merged + gate * _dot(b_ref[...], wbr_ref[k])
        mb = merged.astype(BF16)
        mrg_ref[...] = mb
        tt = _dot(mb, wout_ref[...])
        t_ref[...] = tt
        xmid_ref[...] = x_ref[...] + _rms(tt, g2_ref[...])

    return _pcall(
        body, name="merge_fwd", grid=(s // t,),
        in_specs=[_rows(t, D_MODEL), _rows(t, D_MODEL)] + [_rows(t, BR_W)] * 4 +
                 [_whole(wg.shape), _whole(bg.shape), _whole(wbr.shape), _whole(wout.shape), _whole(g2.shape)],
        out_specs=[_rows(t, D_MODEL)] * 3,
        out_shape=[_sds((s, D_MODEL), F32), _sds((s, D_MODEL), BF16), _sds((s, D_MODEL), F32)],
        vmem=VMEM_LIMIT)(x, hb, *branches, wg, bg, wbr, wout, g2)


def _ffn_fwd(x, g3, w1, w2, g4, t):
    s = x.shape[0]

    def body(x_ref, g3_ref, w1_ref, w2_ref, g4_ref, xo_ref, f_ref):
        h = _rms(x_ref[...], g3_ref[...]).astype(BF16)
        f = jnp.zeros((t, D_MODEL), F32)
        for j in range(D_FF // D_MODEL):
            cols = slice(j * D_MODEL, (j + 1) * D_MODEL)
            r = jnp.square(jnp.maximum(_dot(h, w1_ref[:, cols]), 0.0)).astype(BF16)
            f = f + _dot(r, w2_ref[cols, :])
        f_ref[...] = f
        xo_ref[...] = x_ref[...] + _rms(f, g4_ref[...])

    return _pcall(
        body, name="ffn_fwd", grid=(s // t,),
        in_specs=[_rows(t, D_MODEL), _whole(g3.shape), _whole(w1.shape), _whole(w2.shape), _whole(g4.shape)],
        out_specs=[_rows(t, D_MODEL)] * 2,
        out_shape=[_sds((s, D_MODEL), F32)] * 2,
        vmem=VMEM_LIMIT)(x, g3, w1, w2, g4)


def _loss_head(y, target, t):
    s = y.shape[0]
    n = s // t

    def body(y_ref, tg_ref, dy_ref, loss_ref, acc_scr):
        i = pl.program_id(0)
        d = y_ref[...] - tg_ref[...]
        dy_ref[...] = d * (1.0 / D_MODEL)
        _accumulate(acc_scr, jnp.sum(d * d, axis=0, keepdims=True), i == 0)

        @pl.when(i == n - 1)
        def _():
            loss_ref[...] = jnp.full(loss_ref.shape, 0.5 / D_MODEL, F32) * jnp.sum(acc_scr[...])

    return _pcall(
        body, name="loss_head", grid=(n,),
        in_specs=[_rows(t, D_MODEL)] * 2,
        out_specs=[_rows(t, D_MODEL), _acc((8, LANES))],
        out_shape=[_sds((s, D_MODEL), F32), _sds((8, LANES), F32)],
        scratch=[pltpu.VMEM((1, D_MODEL), F32)])(y, target)


def _ffn_bwd_norms(x_mid, f, dxo, g3, g4, t):
    s = x_mid.shape[0]

    def body(x_ref, f_ref, dxo_ref, g3_ref, g4_ref, h2_ref, df_ref, dg4_ref):
        h2_ref[...] = _rms(x_ref[...], g3_ref[...]).astype(BF16)
        df, dg4 = _rms_bwd(f_ref[...], g4_ref[...], dxo_ref[...])
        df_ref[...] = df.astype(BF16)
        _accumulate(dg4_ref, dg4, pl.program_id(0) == 0)

    return _pcall(
        body, name="ffn_bwd_norms", grid=(s // t,),
        in_specs=[_rows(t, D_MODEL)] * 3 + [_whole(g3.shape), _whole(g4.shape)],
        out_specs=[_rows(t, D_MODEL), _rows(t, D_MODEL), _acc((1, D_MODEL))],
        out_shape=[_sds((s, D_MODEL), BF16), _sds((s, D_MODEL), BF16), _sds((1, D_MODEL), F32)])(x_mid, f, dxo, g3, g4)


def _ffn_bwd_weights(h2, df, w1, w2, t):
    s = h2.shape[0]
    blk = D_MODEL
    n = s // t

    def body(h2_ref, df_ref, w1_ref, w2_ref, da_ref, dw1_ref, dw2_ref, acc1, acc2):
        first, last = pl.program_id(1) == 0, pl.program_id(1) == n - 1
        h2v, dfv = h2_ref[...], df_ref[...]
        rl = jnp.maximum(_dot(h2v, w1_ref[...]), 0.0)
        _accumulate_out(acc2, dw2_ref, _dot((rl * rl).astype(BF16), dfv, TN), first, last)
        da = (_dot(dfv, w2_ref[...], NT) * (2.0 * rl)).astype(BF16)
        da_ref[...] = da
        _accumulate_out(acc1, dw1_ref, _dot(da, h2v, TN), first, last)

    tok = pl.BlockSpec((t, D_MODEL), lambda j, i: (i, 0))
    hid = pl.BlockSpec((blk, D_MODEL), lambda j, i: (j, 0))
    return _pcall(
        body, name="ffn_bwd_weights", grid=(D_FF // blk, n),
        in_specs=[tok, tok, pl.BlockSpec((D_MODEL, blk), lambda j, i: (0, j)), hid],
        out_specs=[pl.BlockSpec((t, blk), lambda j, i: (i, j)), hid, hid],
        out_shape=[_sds((s, D_FF), BF16), _sds((D_FF, D_MODEL), BF16), _sds((D_FF, D_MODEL), BF16)],
        scratch=[pltpu.VMEM((blk, D_MODEL), F32)] * 2,
        vmem=VMEM_LIMIT)(h2, df, w1, w2)


def _ffn_bwd_input(da, w1, x_mid, dxo, g3, t):
    s = x_mid.shape[0]

    def body(da_ref, w1_ref, x_ref, dxo_ref, g3_ref, dx_ref, dg3_ref):
        dx, dg3 = _rms_bwd(x_ref[...], g3_ref[...], _dot(da_ref[...], w1_ref[...], NT))
        dx_ref[...] = dxo_ref[...] + dx
        _accumulate(dg3_ref, dg3, pl.program_id(0) == 0)

    return _pcall(
        body, name="ffn_bwd_input", grid=(s // t,),
        in_specs=[_rows(t, D_FF), _whole(w1.shape), _rows(t, D_MODEL), _rows(t, D_MODEL), _whole(g3.shape)],
        out_specs=[_rows(t, D_MODEL), _acc((1, D_MODEL))],
        out_shape=[_sds((s, D_MODEL), F32), _sds((1, D_MODEL), F32)],
        vmem=VMEM_LIMIT)(da, w1, x_mid, dxo, g3)


def _merge_bwd_out(tt, dxmid, g2, wout, merged, t):
    s = tt.shape[0]
    n = s // t

    def body(t_ref, dx_ref, g2_ref, wout_ref, mrg_ref, dm_ref, dwout_ref, dg2_ref, acc):
        first, last = pl.program_id(0) == 0, pl.program_id(0) == n - 1
        dt, dg2 = _rms_bwd(t_ref[...], g2_ref[...], dx_ref[...])
        dtb = dt.astype(BF16)
        dm_ref[...] = _dot(dtb, wout_ref[...], NT)
        _accumulate_out(acc, dwout_ref, _dot(mrg_ref[...], dtb, TN), first, last)
        _accumulate(dg2_ref, dg2, first)

    return _pcall(
        body, name="merge_bwd_out", grid=(n,),
        in_specs=[_rows(t, D_MODEL), _rows(t, D_MODEL), _whole(g2.shape), _whole(wout.shape), _rows(t, D_MODEL)],
        out_specs=[_rows(t, D_MODEL), _acc((D_MODEL, D_MODEL)), _acc((1, D_MODEL))],
        out_shape=[_sds((s, D_MODEL), F32), _sds((D_MODEL, D_MODEL), BF16), _sds((1, D_MODEL), F32)],
        scratch=[pltpu.VMEM((D_MODEL, D_MODEL), F32)],
        vmem=VMEM_LIMIT)(tt, dxmid, g2, wout, merged)


def _merge_bwd_branch(k, hb, br, dm, wg, bg, wbr, t):
    s = hb.shape[0]
    n = s // t

    def body(hb_ref, br_ref, dm_ref, wg_ref, bg_ref, wbr_ref, dbr_ref, dpre_ref, dwg_ref, dbg_ref, dwbr_ref,
             acc_g, acc_br):
        first, last = pl.program_id(0) == 0, pl.program_id(0) == n - 1
        hbv, brv, dmv, wbrv = hb_ref[...], br_ref[...], dm_ref[...], wbr_ref[0]
        gate = _sigmoid(_dot(hbv, wg_ref[...]) + bg_ref[...])
        dy = (dmv * gate).astype(BF16)
        dpre = dmv * _dot(brv, wbrv) * gate * (1.0 - gate)
        dpb = dpre.astype(BF16)
        dpre_ref[...] = dpb
        dbr_ref[...] = _dot(dy, wbrv, NT)
        _accumulate_out(acc_br, dwbr_ref, _dot(dy, brv, TN), first, last)
        _accumulate_out(acc_g, dwg_ref, _dot(dpb, hbv, TN), first, last)
        _accumulate(dbg_ref, jnp.sum(dpre, axis=0, keepdims=True), first)

    return _pcall(
        body, name=f"merge_bwd_branch{k}", grid=(n,),
        in_specs=[_rows(t, D_MODEL), _rows(t, BR_W), _rows(t, D_MODEL),
                  pl.BlockSpec((D_MODEL, D_MODEL), lambda i: (0, k)), pl.BlockSpec((1, D_MODEL), lambda i: (0, k)),
                  pl.BlockSpec((1, BR_W, D_MODEL), lambda i: (k, 0, 0))],
        out_specs=[_rows(t, BR_W), _rows(t, D_MODEL), _acc((D_MODEL, D_MODEL)), _acc((1, D_MODEL)), _acc((D_MODEL, BR_W))],
        out_shape=[_sds((s, BR_W), F32), _sds((s, D_MODEL), BF16), _sds((D_MODEL, D_MODEL), BF16),
                   _sds((1, D_MODEL), F32), _sds((D_MODEL, BR_W), BF16)],
        scratch=[pltpu.VMEM((D_MODEL, D_MODEL), F32), pltpu.VMEM((D_MODEL, BR_W), F32)],
        vmem=VMEM_LIMIT)(hb, br, dm, wg, bg, wbr)


def _mixers_bwd(pm, dbsg, dbcv, dbpl, ln_g, ln_b, wm, wmt, sgb, conv_w, wp, pscale, t):
    s = pm.shape[0]
    n = s // t
    nb = t // HALO

    def body(pm_ref, before_ref, after_ref, dsg_ref, dcv_ref, dcva_ref, dpl_ref, dpla_ref, lng_ref, lnb_ref,
             wm_ref, wmt_ref, sgb_ref, cw_ref, wp_ref, ps_ref,
             dpm_ref, dlng_ref, dlnb_ref, dwm_ref, dsgb_ref, dcw_ref, dwp_ref, dps_ref):
        i = pl.program_id(0)
        first = i == 0
        before = jnp.where(i > 0, before_ref[...], 0.0)
        after = jnp.where(i < n - 1, after_ref[...], 0.0)

        u_raw, v_raw = pm_ref[:, 0:256], pm_ref[:, 256:512]
        lng, lnb = lng_ref[...], lnb_ref[...]
        u, gelu_u_vjp = jax.vjp(_gelu, u_raw)
        vn, norm_vjp = jax.vjp(lambda v_, g_, b_: _layernorm(_gelu(v_), g_, b_), v_raw, lng, lnb)
        vnb = vn.astype(BF16)
        dsg = dsg_ref[...]
        grp = _lane_group((SG_CHUNK, BR_W))
        tri = (lax.broadcasted_iota(jnp.int32, (SG_CHUNK, SG_CHUNK), 1)
               <= lax.broadcasted_iota(jnp.int32, (SG_CHUNK, SG_CHUNK), 0))
        du_parts, dvn_parts = [], []
        dsgb = jnp.zeros((SG_CHUNK, BR_W), F32)
        dwm = [jnp.zeros((SG_CHUNK, SG_CHUNK), F32) for _ in range(SG_GROUPS)]
        for c in range(t // SG_CHUNK):
            rows = slice(c * SG_CHUNK, (c + 1) * SG_CHUNK)
            mix = _sg_mix(wm_ref, vnb[rows], sgb_ref[...])
            du_parts.append(dsg[rows] * mix)
            ds = dsg[rows] * u[rows]
            dsgb = dsgb + ds
            dsb = [jnp.where(grp == g, ds, 0.0).astype(BF16) for g in range(SG_GROUPS)]
            for g in range(SG_GROUPS):
                dwm[g] = dwm[g] + _dot(dsb[g], vnb[rows], NT)
            dvn_parts.append(_group_select([_dot(wmt_ref[g], dsb[g]) for g in range(SG_GROUPS)]))
        (du_raw,) = gelu_u_vjp(jnp.concatenate(du_parts, axis=0))
        dv_raw, dlng, dlnb = norm_vjp(jnp.concatenate(dvn_parts, axis=0))
        dpm_ref[:, 0:256] = du_raw.astype(BF16)
        dpm_ref[:, 256:512] = dv_raw.astype(BF16)
        _accumulate(dlng_ref, dlng, first)
        _accumulate(dlnb_ref, dlnb, first)
        _accumulate(dsgb_ref, dsgb, first)
        for g in range(SG_GROUPS):
            _accumulate(dwm_ref.at[g], jnp.where(tri, dwm[g], 0.0), first)

        xin, bg, cg = pm_ref[:, 512:768], pm_ref[:, 768:1024], pm_ref[:, 1024:1280]
        z = cg * xin
        zext = jnp.concatenate([before[:, 1024:1280] * before[:, 512:768], z], axis=0)
        z1, z2 = _shift_down(zext, 1, t), _shift_down(zext, 2, t)
        w0, w1, w2 = cw_ref[0:1, :], cw_ref[1:2, :], cw_ref[2:3, :]
        dcv = dcv_ref[...]
        y = w0 * z2 + w1 * z1 + w2 * z
        dy = dcv * bg
        dyext = jnp.concatenate([dy, jnp.where(i < n - 1, dcva_ref[...], 0.0) * after[:, 768:1024]], axis=0)
        dz = w2 * dy + w1 * _shift_up(dyext, 1, t) + w0 * _shift_up(dyext, 2, t)
        dpm_ref[:, 512:768] = (dz * cg).astype(BF16)
        dpm_ref[:, 768:1024] = (dcv * y).astype(BF16)
        dpm_ref[:, 1024:1280] = (dz * xin).astype(BF16)
        dcw = jnp.concatenate([jnp.sum(dy * z2, axis=0, keepdims=True), jnp.sum(dy * z1, axis=0, keepdims=True),
                               jnp.sum(dy * z, axis=0, keepdims=True)], axis=0)
        _accumulate(dcw_ref, jnp.concatenate([dcw, jnp.zeros((8 - CONV_K, BR_W), F32)], axis=0), first)

        p = pm_ref[:, 1280:1536]
        tok = i * t + lax.broadcasted_iota(jnp.int32, (t, 1), 0)
        sums = _pool_windows(jnp.concatenate([before[:, 1280:1536], p], axis=0))
        pooled = _group_select([sw[HALO:HALO + t] / cnt - p for sw, cnt in zip(sums, _pool_counts(tok))]).astype(BF16)
        mixed = _dot(pooled, wp_ref[...])
        dpl = dpl_ref[...]
        ps = ps_ref[...]
        dmix = (dpl * ps).astype(BF16)
        _accumulate(dps_ref, jnp.sum(dpl * mixed, axis=0, keepdims=True), first)
        _accumulate(dwp_ref, _dot(pooled, dmix, TN), first)
        dmix_after = (jnp.where(i < n - 1, dpla_ref[...], 0.0) * ps).astype(BF16)
        dpo = _dot(dmix, wp_ref[...], NT)
        dpo_ext = jnp.concatenate([dpo, _dot(dmix_after, wp_ref[...], NT)], axis=0)
        tok_ext = i * t + lax.broadcasted_iota(jnp.int32, (t + HALO, 1), 0)
        dp_groups = []
        for g, (win, cnt) in enumerate(zip(POOL_WINDOWS, _pool_counts(tok_ext))):
            e = dpo_ext / cnt
            acc = e
            span = 1
            while span < win:
                acc = acc + pltpu.roll(acc, t + HALO - span, 0)
                span *= 2
            dp_groups.append(acc[0:t] - dpo)
        dpm_ref[:, 1280:1536] = _group_select(dp_groups).astype(BF16)

    row_blk = lambda w: pl.BlockSpec((t, w), lambda i: (i, 0))
    after_blk = lambda w: pl.BlockSpec((HALO, w), lambda i: (jnp.minimum((i + 1) * nb, n * nb - 1), 0))
    return _pcall(
        body, name="mixers_bwd", grid=(n,),
        in_specs=[row_blk(PM_W), _halo_before(t), after_blk(PM_W), row_blk(BR_W), row_blk(BR_W), after_blk(BR_W),
                  row_blk(BR_W), after_blk(BR_W), _whole(ln_g.shape), _whole(ln_b.shape), _whole(wm.shape),
                  _whole(wmt.shape), _whole(sgb.shape), _whole(conv_w.shape), _whole(wp.shape), _whole(pscale.shape)],
        out_specs=[row_blk(PM_W), _acc((1, BR_W)), _acc((1, BR_W)), _acc((SG_GROUPS, SG_CHUNK, SG_CHUNK)),
                   _acc((SG_CHUNK, BR_W)), _acc((8, BR_W)), _acc((BR_W, BR_W)), _acc((1, BR_W))],
        out_shape=[_sds((s, PM_W), BF16), _sds((1, BR_W), F32), _sds((1, BR_W), F32),
                   _sds((SG_GROUPS, SG_CHUNK, SG_CHUNK), F32), _sds((SG_CHUNK, BR_W), F32), _sds((8, BR_W), F32),
                   _sds((BR_W, BR_W), F32), _sds((1, BR_W), F32)],
        vmem=VMEM_LIMIT)(pm, pm, pm, dbsg, dbcv, dbcv, dbpl, dbpl, ln_g, ln_b, wm, wmt, sgb, conv_w, wp, pscale)


def _head_delta(o_ref, do_ref, t):
    prod = o_ref[...].astype(F32) * do_ref[...]
    lane = lax.broadcasted_iota(jnp.int32, (t, LANES), 1)
    out = []
    for h in range(N_HEADS):
        pair = prod[:, (h // 2) * LANES:(h // 2 + 1) * LANES]
        mine = (lane < D_V) if h % 2 == 0 else (lane >= D_V)
        out.append(jnp.broadcast_to(jnp.sum(jnp.where(mine, pair, 0.0), axis=1, keepdims=True), (t, LANES)))
    return out


def _head_do(do_ref, h, t):
    lane = lax.broadcasted_iota(jnp.int32, (t, LANES), 1)
    mine = (lane < D_V) if h % 2 == 0 else (lane >= D_V)
    pair = do_ref[:, (h // 2) * LANES:(h // 2 + 1) * LANES]
    return jnp.where(mine, pair, 0.0).astype(BF16)


def _flash_bwd(q, k, v, o, do, lse, t, riding=None):
    s = v.shape[0]
    n = s // t
    reps = t // LANES

    qi_tab, kj_tab = _causal_steps(n, key_major=True)
    n_steps = int(qi_tab.shape[0])

    def body(qi_ref, kj_ref, q_ref, k_ref, v_ref, o_ref, do_ref, lse_ref, *rest):
        if riding is None:
            dq_ref, dk_ref, dv_ref, dv_scr = rest
        else:
            x_ref, dq_ref, dk_ref, dv_ref, land_ref, dv_scr, send_sems, recv_sems, local_sem = rest
        step_id = pl.program_id(0)
        qi, kj = qi_ref[step_id], kj_ref[step_id]
        if riding is not None:
            at_end = _ride_along(riding, step_id, n_steps, (x_ref, land_ref, send_sems, recv_sems, local_sem))

        @pl.when(step_id == 0)
        def _():
            dq_ref[...] = jnp.zeros(dq_ref.shape, F32)

        @pl.when(qi == kj)
        def _():
            dk_ref[...] = jnp.zeros(dk_ref.shape, F32)
            dv_scr[...] = jnp.zeros(dv_scr.shape, F32)

        def step(masked):
            delta = _head_delta(o_ref, do_ref, t)
            q_rows = pl.ds(pl.multiple_of(qi * t, t), t)
            for h in range(N_HEADS):
                qh, kh = q_ref[h], k_ref[h]
                p = jnp.exp2(_causal_scores(qh, kh, masked) - jnp.tile(lse_ref[h], (1, reps)))
                pair = slice((h // 2) * LANES, (h // 2 + 1) * LANES)
                dv_scr[h] += _dot(p.astype(BF16), do_ref[:, pair].astype(BF16), TN)
                dp = _dot(_head_do(do_ref, h, t), v_ref[:, pair], NT)
                ds = (p * (dp - jnp.tile(delta[h], (1, reps))) * SM_SCALE).astype(BF16)
                dk_ref[h] += _dot(ds, qh, TN)
                dq_ref[h, q_rows, :] += _dot(ds, kh)

        @pl.when(qi > kj)
        def _():
            step(False)

        @pl.when(qi == kj)
        def _():
            step(True)

        @pl.when(qi == n - 1)
        def _():
            lane = lax.broadcasted_iota(jnp.int32, (t, LANES), 1)
            for pr in range(N_HEADS // 2):
                dv_ref[:, pr * LANES:(pr + 1) * LANES] = jnp.where(lane < D_V, dv_scr[2 * pr], dv_scr[2 * pr + 1]).astype(BF16)

        if riding is not None:
            at_end()

    q_blk = pl.BlockSpec((N_HEADS, t, HEAD_PAD), lambda i, qi, kj: (0, qi[i], 0))
    k_blk = pl.BlockSpec((N_HEADS, t, HEAD_PAD), lambda i, qi, kj: (0, kj[i], 0))
    v_blk = pl.BlockSpec((t, BR_W), lambda i, qi, kj: (kj[i], 0))
    o_blk = pl.BlockSpec((t, BR_W), lambda i, qi, kj: (qi[i], 0))
    extra = riding is not None
    dq_sds = _sds((N_HEADS, s, HEAD_PAD), F32)
    return _pcall(
        body, name="flash_bwd_exchanging" if extra else "flash_bwd", grid=(n_steps,), prefetch=2,
        in_specs=[q_blk, k_blk, v_blk, o_blk, o_blk, q_blk] + [HBM_REF] * extra,
        out_specs=[_whole(dq_sds.shape), k_blk, v_blk] + [HBM_REF] * extra,
        out_shape=[dq_sds, dq_sds, _sds((s, BR_W), BF16)] + ([riding[2]] if extra else []),
        scratch=[pltpu.VMEM((N_HEADS, t, LANES), F32)] + (list(EXCHANGE_SEMS) if extra else []),
        vmem=VMEM_LIMIT)(qi_tab, kj_tab, q, k, v, o, do, lse, *([riding[1]] if extra else []))


def _attn_prep_bwd(pa, dq, dk, dv, cosf, sins, qn, kvn, wq, wqs, wk, wv, t):
    s = pa.shape[0]

    def body(pa_ref, dq_ref, dk_ref, dv_ref, cos_ref, sin_ref, qn_ref, kvn_ref, wq_ref, wqs_ref, wk_ref, wv_ref,
             dpa_ref, dwq_ref, dwqs_ref, dwk_ref, dwv_ref, dqn_ref, dkvn_ref):
        first = pl.program_id(0) == 0
        cosv, sinv = cos_ref[...], sin_ref[...]
        cq, ckv = pa_ref[:, 0:Q_RANK], pa_ref[:, Q_RANK:Q_RANK + KV_RANK]
        cqn = _rms(cq, qn_ref[...]).astype(BF16)
        ckvn = _rms(ckv, kvn_ref[...]).astype(BF16)
        dcqn = jnp.zeros((t, Q_RANK), F32)
        dvv = dv_ref[...]
        dckvn = _dot(dvv, wv_ref[...], NT)
        _accumulate(dwv_ref, _dot(dvv, ckvn, TN), first)
        dk_rope = jnp.zeros((t, HEAD_PAD), F32)
        for h in range(N_HEADS):
            dqh = dq_ref[h]
            dqa, dqs = (dqh * cosv).astype(BF16), (dqh * sinv).astype(BF16)
            _accumulate(dwq_ref.at[h], _dot(dqa, cqn, TN), first)
            _accumulate(dwqs_ref.at[h], _dot(dqs, cqn, TN), first)
            dcqn = dcqn + _dot(dqa, wq_ref[h], NT) + _dot(dqs, wqs_ref[h], NT)
            dkh = dk_ref[h]
            dkb = dkh.astype(BF16)
            _accumulate(dwk_ref.at[h], _dot(dkb, ckvn, TN), first)
            dckvn = dckvn + _dot(dkb, wk_ref[h], NT)
            dk_rope = dk_rope + dkh
        dcq, dqn = _rms_bwd(cq, qn_ref[...], dcqn)
        dckv, dkvn = _rms_bwd(ckv, kvn_ref[...], dckvn)
        _accumulate(dqn_ref, dqn, first)
        _accumulate(dkvn_ref, dkvn, first)
        dpa_ref[:, 0:Q_RANK] = dcq.astype(BF16)
        dpa_ref[:, Q_RANK:Q_RANK + KV_RANK] = dckv.astype(BF16)
        dpa_ref[:, 384:512] = (dk_rope * cosv).astype(BF16)
        dpa_ref[:, 512:640] = (dk_rope * sinv).astype(BF16)

    head_blk = pl.BlockSpec((N_HEADS, t, HEAD_PAD), lambda i: (0, i, 0))
    wq_t, wk_t, wv_t = (N_HEADS, HEAD_PAD, Q_RANK), (N_HEADS, HEAD_PAD, KV_RANK), (N_HEADS * D_V, KV_RANK)
    return _pcall(
        body, name="attn_prep_bwd", grid=(s // t,),
        in_specs=[_rows(t, PA_W), head_blk, head_blk, _rows(t, BR_W), _rows(t, LANES), _rows(t, LANES),
                  _whole(qn.shape), _whole(kvn.shape), _whole(wq.shape), _whole(wqs.shape), _whole(wk.shape), _whole(wv.shape)],
        out_specs=[_rows(t, PA_W), _acc(wq_t), _acc(wq_t), _acc(wk_t), _acc(wv_t), _acc(qn.shape), _acc(kvn.shape)],
        out_shape=[_sds((s, PA_W), BF16), _sds(wq_t, F32), _sds(wq_t, F32), _sds(wk_t, F32),
                   _sds(wv_t, F32), _sds(qn.shape, F32), _sds(kvn.shape, F32)],
        vmem=VMEM_LIMIT)(pa, dq, dk, dv, cosf, sins, qn, kvn, wq, wqs, wk, wv)


def _inproj_bwd(x, g1, dxmid, dpa, dpm, dpres, w_in, wg, t):
    s = x.shape[0]
    n = s // t

    def body(x_ref, g1_ref, dxm_ref, dpa_ref, dpm_ref, d0_ref, d1_ref, d2_ref, d3_ref, w_ref, wg_ref,
             dx_ref, dg1_ref, dwa_ref, dwm_ref, acc_a, acc_m):
        first, last = pl.program_id(0) == 0, pl.program_id(0) == n - 1
        xv, g1v = x_ref[...], g1_ref[...]
        hb = _rms(xv, g1v).astype(BF16)
        dpav, dpmv = dpa_ref[...], dpm_ref[...]
        dh = _dot(dpav, w_ref[:, :PA_W], NT) + _dot(dpmv, w_ref[:, PA_W:], NT)
        for k, d_ref in enumerate((d0_ref, d1_ref, d2_ref, d3_ref)):
            dh = dh + _dot(d_ref[...], wg_ref[:, k * D_MODEL:(k + 1) * D_MODEL], NT)
        dx, dg1 = _rms_bwd(xv, g1v, dh)
        dx_ref[...] = dxm_ref[...] + dx
        _accumulate(dg1_ref, dg1, first)
        _accumulate_out(acc_a, dwa_ref, _dot(dpav, hb, TN), first, last)
        _accumulate_out(acc_m, dwm_ref, _dot(dpmv, hb, TN), first, last)

    return _pcall(
        body, name="inproj_bwd", grid=(n,),
        in_specs=[_rows(t, D_MODEL), _whole(g1.shape), _rows(t, D_MODEL), _rows(t, PA_W), _rows(t, PM_W)] +
                 [_rows(t, D_MODEL)] * 4 + [_whole(w_in.shape), _whole(wg.shape)],
        out_specs=[_rows(t, D_MODEL), _acc((1, D_MODEL)), _acc((PA_W, D_MODEL)), _acc((PM_W, D_MODEL))],
        out_shape=[_sds((s, D_MODEL), F32), _sds((1, D_MODEL), F32), _sds((PA_W, D_MODEL), BF16), _sds((PM_W, D_MODEL), BF16)],
        scratch=[pltpu.VMEM((PA_W, D_MODEL), F32), pltpu.VMEM((PM_W, D_MODEL), F32)],
        vmem=VMEM_LIMIT)(x, g1, dxmid, dpa, dpm, *dpres, w_in, wg)


def _my_place():
    return lax.axis_index("x"), lax.axis_index("y"), lax.axis_index("c")


def _flip(place, k):
    x, y, c = place
    return (1 - x if k & 4 else x, 1 - y if k & 2 else y, 1 - c if k & 1 else c)


def _rank(place):
    x, y, c = place
    return 4 * x + 2 * y + c


EXCHANGE_SEMS = (pltpu.SemaphoreType.DMA((7,)), pltpu.SemaphoreType.DMA((7,)), pltpu.SemaphoreType.DMA)
HBM_REF = pl.BlockSpec(memory_space=pl.ANY)


def _gather_plan(x_ref, out_ref, send_sems, recv_sems, local_sem):
    me = _my_place()
    sibling = _flip(me, 1)
    chips = (4, 2, 6)

    def copy(k, src_place, to, src=None):
        slot = out_ref.at[_rank(src_place)]
        return pltpu.make_async_remote_copy(
            src_ref=slot if src is None else src, dst_ref=slot, send_sem=send_sems.at[k], recv_sem=recv_sems.at[k],
            device_id=to, device_id_type=MESH_ID)

    mine = pltpu.make_async_copy(x_ref, out_ref.at[_rank(me)], local_sem)
    first = [copy(0, me, sibling, src=x_ref)] + [copy(1 + j, me, _flip(me, kc), src=x_ref) for j, kc in enumerate(chips)]
    passed = [copy(4 + j, _flip(me, kc), sibling) for j, kc in enumerate(chips)]

    def start():
        mine.start()
        for cp in first:
            cp.start()

    def forward():
        for j, kc in enumerate(chips):
            copy(1 + j, _flip(me, kc), me).wait_recv()
            passed[j].start()

    def finish():
        copy(0, sibling, me).wait_recv()
        for j, kc in enumerate(chips):
            copy(4 + j, _flip(sibling, kc), me).wait_recv()
        for cp in first + passed:
            cp.wait_send()
        mine.wait()

    return start, forward, finish


def _exchange_plan(x_ref, out_ref, send_sems, recv_sems, local_sem):
    me = _my_place()
    my_rank = _rank(me)
    mine = pltpu.make_async_copy(x_ref.at[my_rank], out_ref.at[my_rank], local_sem)
    sends = [pltpu.make_async_remote_copy(
        src_ref=x_ref.at[_rank(_flip(me, k))], dst_ref=out_ref.at[my_rank], send_sem=send_sems.at[k - 1],
        recv_sem=recv_sems.at[k - 1], device_id=_flip(me, k), device_id_type=MESH_ID) for k in range(1, N_DEV)]

    def start():
        mine.start()
        for cp in sends:
            cp.start()

    def forward():
        pass

    def finish():
        for k in range(1, N_DEV):
            slot = out_ref.at[_rank(_flip(me, k))]
            pltpu.make_async_remote_copy(
                src_ref=slot, dst_ref=slot, send_sem=send_sems.at[k - 1], recv_sem=recv_sems.at[k - 1],
                device_id=_flip(me, k), device_id_type=MESH_ID).wait_recv()
        for cp in sends:
            cp.wait_send()
        mine.wait()

    return start, forward, finish


def _exchange_alone(plan, src, out_sds, name):
    def body(x_ref, out_ref, send_sems, recv_sems, local_sem):
        start, forward, finish = plan(x_ref, out_ref, send_sems, recv_sems, local_sem)
        start()
        forward()
        finish()

    return pl.pallas_call(body, name=name, out_shape=out_sds, in_specs=[HBM_REF], out_specs=HBM_REF,
                          scratch_shapes=list(EXCHANGE_SEMS))(src)


def _all_gather(shard, name):
    return _exchange_alone(_gather_plan, shard, _sds((N_DEV,) + shard.shape, shard.dtype), name)


def _all_to_all(blocks, name):
    return _exchange_alone(_exchange_plan, blocks, _sds(blocks.shape, blocks.dtype), name)


def _riding(plan, src, out_sds):
    return (plan, src, out_sds)


def _ride_along(riding, step_id, n_steps, refs):
    start, forward, finish = riding[0](*refs)

    @pl.when(step_id == 0)
    def _():
        start()

    @pl.when(step_id == (3 * n_steps) // 4)
    def _():
        forward()

    def at_end():
        @pl.when(step_id == n_steps - 1)
        def _():
            finish()

    return at_end


def _sum_slots(parts, name):
    _, r, c = parts.shape
    t = math.gcd(r, ROW_PAD)

    def body(p_ref, o_ref):
        acc = p_ref[0].astype(F32)
        for d in range(1, N_DEV):
            acc = acc + p_ref[d].astype(F32)
        o_ref[...] = acc

    return _pcall(
        body, name=name, grid=(r // t,),
        in_specs=[pl.BlockSpec((N_DEV, t, c), lambda i: (0, i, 0))], out_specs=_rows(t, c),
        out_shape=_sds((r, c), F32))(parts)


def _adamw(w, g, m, v, name):
    r, c = w.shape
    t = r if r <= ROW_PAD else math.gcd(r, ROW_PAD)

    def body(w_ref, g_ref, m_ref, v_ref, d_ref, nm_ref, nv_ref):
        gv = g_ref[...]
        nm = ADAM_B1 * m_ref[...] + (1.0 - ADAM_B1) * gv
        nv = ADAM_B2 * v_ref[...] + (1.0 - ADAM_B2) * jnp.square(gv)
        m_hat = nm / (1.0 - ADAM_B1 ** ADAM_STEP)
        v_hat = nv / (1.0 - ADAM_B2 ** ADAM_STEP)
        d_ref[...] = -ADAM_LR * (m_hat / (jnp.sqrt(v_hat) + ADAM_EPS) + ADAM_WD * w_ref[...])
        nm_ref[...] = nm
        nv_ref[...] = nv

    return _pcall(
        body, name=name, grid=(r // t,),
        in_specs=[_rows(t, c)] * 4, out_specs=[_rows(t, c)] * 3, out_shape=[_sds((r, c), F32)] * 3)(w, g, m, v)


SHARDED = ("w_in", "w_uq", "w_ukv", "w_br_mla", "w_br_sg", "w_br_conv", "w_br_pool", "w_out", "w_ff1", "w_ff2", "conv_w")
ROW_SHARDED = ("w_out", "w_ff2")
REPLICATED = ("norm_mix_pre", "gate_b", "q_norm", "kv_norm", "sg_ln_g", "sg_ln_b", "sg_w", "sg_b", "pool_w",
              "pool_scale", "norm_mix_post", "norm_ffn_pre", "norm_ffn_post")
ROW_PAD = 256
GRAD_ROW_PAD = 64
PART_ROWS = 16


def _pack_rows(arrays, dtype, multiple, lead=()):
    rows, offsets, at = [], [], 0
    zero_pad = ((0, 0),) * len(lead)
    for a in arrays:
        flat = a.reshape(lead + (-1,)).astype(dtype)
        n = -(-flat.shape[-1] // D_MODEL)
        n_pad = -(-n // PART_ROWS) * PART_ROWS
        flat = jnp.pad(flat, zero_pad + ((0, n_pad * D_MODEL - flat.shape[-1]),))
        rows.append(flat.reshape(lead + (n_pad, D_MODEL)))
        offsets.append((at, n))
        at += n_pad
    pad = -at % multiple
    if pad:
        rows.append(jnp.zeros(lead + (pad, D_MODEL), dtype))
    return jnp.concatenate(rows, axis=len(lead)), offsets


def _unpack_rows(buf, offsets, shapes):
    lead = buf.shape[:-2]
    out = []
    for (at, n), shape in zip(offsets, shapes):
        size = math.prod(shape)
        flat = buf[..., at:at + n, :].reshape(lead + (n * D_MODEL,))[..., :size]
        out.append(flat.reshape(lead + tuple(shape)))
    return out


def _join_shards(name, gathered):
    if name in ROW_SHARDED:
        return gathered.reshape(-1, gathered.shape[-1])
    g = jnp.moveaxis(gathered, 0, -2)
    return g.reshape(g.shape[:-2] + (g.shape[-2] * g.shape[-1],))


def _pack_shards(p, depth):
    arrays = [lax.bitcast_convert_type(p[n][0], BF16) if n == "conv_w" else p[n][0].astype(BF16) for n in SHARDED]
    packed, offsets = _pack_rows(arrays, BF16, GRAD_ROW_PAD, lead=(depth,))
    return packed, offsets, [a.shape[1:] for a in arrays]


def _unpack_layer_weights(gathered, offsets, shapes):
    full = {}
    for n, gth in zip(SHARDED, _unpack_rows(gathered, offsets, shapes)):
        if n == "conv_w":
            gth = lax.bitcast_convert_type(gth, F32)
        full[n] = _join_shards(n, gth)
    return full


def _owner_major(grad_sharded_dim_first):
    a = grad_sharded_dim_first
    return a.reshape((N_DEV, a.shape[0] // N_DEV) + a.shape[1:])


def _owner_block_shape(name, shard_shape):
    k, n = shard_shape[-2:]
    return (k, n) if name in ROW_SHARDED else (n, k)


def _natural_shard(name, blocks):
    return blocks if name in ROW_SHARDED else jnp.swapaxes(blocks, -1, -2)


def _swap_halves(a, axis=-1):
    lo, hi = jnp.split(a, 2, axis=axis)
    return jnp.concatenate([hi, lo], axis=axis)


def _pad_cols(a, left, total):
    return jnp.pad(a, ((0, 0),) * (a.ndim - 1) + ((left, total - left - a.shape[-1]),))


def _layer_weights(full, small, l):
    w_in = full["w_in"]
    k_r = w_in[:, 384:416]
    w_in_k = jnp.concatenate(
        [w_in[:, 0:384], _pad_cols(k_r, D_NOPE, HEAD_PAD), _pad_cols(_swap_halves(k_r), D_NOPE, HEAD_PAD), w_in[:, 416:1952]], axis=1)
    w_uq = full["w_uq"].reshape(Q_RANK, N_HEADS, D_NOPE + D_ROPE)
    wq = jnp.moveaxis(_pad_cols(w_uq, 0, HEAD_PAD), 1, 0)
    wqs = jnp.moveaxis(_pad_cols(_swap_halves(w_uq[..., D_NOPE:]), D_NOPE, HEAD_PAD), 1, 0)
    w_ukv = full["w_ukv"].reshape(KV_RANK, N_HEADS, D_NOPE + D_V)
    wk = jnp.moveaxis(_pad_cols(w_ukv[..., :D_NOPE], 0, HEAD_PAD), 1, 0)
    wv = w_ukv[..., D_NOPE:].reshape(KV_RANK, N_HEADS * D_V)
    tri = jnp.tril(jnp.ones((SG_CHUNK, SG_CHUNK), bool))
    wm = jnp.where(tri, small["sg_w"][l], 0.0)
    pool_w = small["pool_w"][l]
    wp = jnp.zeros((BR_W, BR_W), F32)
    for g in range(len(POOL_WINDOWS)):
        wp = wp.at[g * POOL_GROUP:(g + 1) * POOL_GROUP, g * POOL_GROUP:(g + 1) * POOL_GROUP].set(pool_w[g])
    vec = lambda name: small[name][l][None, :]
    return dict(
        w_in=w_in_k, wg=w_in[:, 1952:], bg=vec("gate_b"), g1=vec("norm_mix_pre"), qn=vec("q_norm"), kvn=vec("kv_norm"),
        wq=wq, wqs=wqs, wk=wk, wv=wv,
        ln_g=vec("sg_ln_g"), ln_b=vec("sg_ln_b"), wm=wm.astype(BF16), wmt=jnp.swapaxes(wm, 1, 2).astype(BF16),
        sgb=jnp.repeat(small["sg_b"][l].T, POOL_GROUP, axis=1), conv_w=full["conv_w"],
        wp=wp.astype(BF16), pscale=vec("pool_scale"),
        wbr=jnp.stack([full["w_br_mla"], full["w_br_sg"], full["w_br_conv"], full["w_br_pool"]]),
        wout=full["w_out"], g2=vec("norm_mix_post"), g3=vec("norm_ffn_pre"), w1=full["w_ff1"], w2=full["w_ff2"],
        g4=vec("norm_ffn_post"))


def _layer_grads(g):
    dwa = g["dwa"]
    rope_rows = slice(D_NOPE, D_NOPE + D_ROPE)
    d_kr = dwa[384:512][rope_rows].astype(F32) + _swap_halves(dwa[512:640][rope_rows], axis=0).astype(F32)
    w_in = jnp.concatenate([dwa[0:384], d_kr.astype(BF16), g["dwm_in"]] + list(g["dwg"]), axis=0)
    dwq, dwqs = g["dwq"], g["dwqs"]
    d_rope = dwq[:, rope_rows] + _swap_halves(dwqs[:, rope_rows], axis=1)
    w_uq = jnp.concatenate([dwq[:, :D_NOPE], d_rope], axis=1).reshape(N_HEADS * (D_NOPE + D_ROPE), Q_RANK)
    w_ukv = jnp.concatenate([g["dwk"][:, :D_NOPE], g["dwv"].reshape(N_HEADS, D_V, KV_RANK)], axis=1)
    w_ukv = w_ukv.reshape(N_HEADS * (D_NOPE + D_V), KV_RANK)
    pool_w = jnp.stack([g["dwp"][i * POOL_GROUP:(i + 1) * POOL_GROUP, i * POOL_GROUP:(i + 1) * POOL_GROUP]
                        for i in range(len(POOL_WINDOWS))])
    sg_b = g["dsgb"].reshape(SG_CHUNK, SG_GROUPS, POOL_GROUP).sum(axis=-1).T
    own = _owner_major
    return dict(
        w_in=own(w_in), w_uq=own(w_uq), w_ukv=own(w_ukv), w_br_mla=own(g["dwbr"][0]), w_br_sg=own(g["dwbr"][1]),
        w_br_conv=own(g["dwbr"][2]), w_br_pool=own(g["dwbr"][3]), w_out=own(g["dwout"]), w_ff1=own(g["dw1"]),
        w_ff2=own(g["dw2"]), conv_w=own(g["dcw"][:CONV_K].T),
        norm_mix_pre=g["dg1"][0], gate_b=jnp.concatenate([b[0] for b in g["dbg"]]), q_norm=g["dqn"][0], kv_norm=g["dkvn"][0],
        sg_ln_g=g["dlng"][0], sg_ln_b=g["dlnb"][0], sg_w=g["dwm"], sg_b=sg_b, pool_w=pool_w, pool_scale=g["dps"][0],
        norm_mix_post=g["dg2"][0], norm_ffn_pre=g["dg3"][0], norm_ffn_post=g["dg4"][0])


def _rope_tables(positions):
    inv_freq = ROPE_BASE ** (-jnp.arange(0, D_ROPE, 2, dtype=F32) / D_ROPE)
    ang = positions.astype(F32)[:, None] * inv_freq
    cos, sin = jnp.cos(ang), jnp.sin(ang)
    s = positions.shape[0]
    cosf = jnp.concatenate([jnp.ones((s, D_NOPE), F32), cos, cos, jnp.zeros((s, HEAD_PAD - D_NOPE - D_ROPE), F32)], axis=1)
    sins = jnp.concatenate([jnp.zeros((s, D_NOPE), F32), -sin, sin, jnp.zeros((s, HEAD_PAD - D_NOPE - D_ROPE), F32)], axis=1)
    return cosf, sins


def _layer_fwd(x, w, cosf, sins, tiles, riding=None):
    t, ta = tiles["tok"], tiles["attn"]
    hb, pa, pm = _inproj_fwd(x, w["g1"], w["w_in"], tiles["wgrad"])
    q, k, v = _attn_prep_fwd(pa, cosf, sins, w["qn"], w["kvn"], w["wq"], w["wqs"], w["wk"], w["wv"], t)
    o, lse, *landed = _flash_fwd(q, k, v, ta, riding)
    bsg, bcv, bpl = _mixers_fwd(pm, w["ln_g"], w["ln_b"], w["wm"], w["sgb"], w["conv_w"], w["wp"], w["pscale"], t)
    x_mid, merged, tt = _merge_fwd(x, hb, (o, bsg, bcv, bpl), w["wg"], w["bg"], w["wbr"], w["wout"], w["g2"], t)
    x_out, f = _ffn_fwd(x_mid, w["g3"], w["w1"], w["w2"], w["g4"], t)
    saved = dict(x=x, hb=hb, pa=pa, pm=pm, q=q, k=k, v=v, o=o, lse=lse, bsg=bsg, bcv=bcv, bpl=bpl, x_mid=x_mid,
                 merged=merged, tt=tt, f=f)
    return x_out, saved, (landed[0] if landed else None)


def _layer_bwd(dxo, a, w, cosf, sins, tiles, riding=None):
    t, ta, tb, tw = tiles["tok"], tiles["attn"], tiles["inproj_bwd"], tiles["wgrad"]
    g = {}
    h2, df, g["dg4"] = _ffn_bwd_norms(a["x_mid"], a["f"], dxo, w["g3"], w["g4"], t)
    da, g["dw1"], g["dw2"] = _ffn_bwd_weights(h2, df, w["w1"], w["w2"], tw)
    dxmid, g["dg3"] = _ffn_bwd_input(da, w["w1"], a["x_mid"], dxo, w["g3"], t)
    dm, g["dwout"], g["dg2"] = _merge_bwd_out(a["tt"], dxmid, w["g2"], w["wout"], a["merged"], tw)
    dbrs, dpres, g["dwg"], g["dbg"], g["dwbr"] = [], [], [], [], []
    for k, br in enumerate((a["o"], a["bsg"], a["bcv"], a["bpl"])):
        dbr, dpre, dwg, dbg, dwbr = _merge_bwd_branch(k, a["hb"], br, dm, w["wg"], w["bg"], w["wbr"], tw)
        dbrs.append(dbr); dpres.append(dpre); g["dwg"].append(dwg); g["dbg"].append(dbg); g["dwbr"].append(dwbr)
    dpm, g["dlng"], g["dlnb"], g["dwm"], g["dsgb"], g["dcw"], g["dwp"], g["dps"] = _mixers_bwd(
        a["pm"], dbrs[1], dbrs[2], dbrs[3], w["ln_g"], w["ln_b"], w["wm"], w["wmt"], w["sgb"], w["conv_w"], w["wp"], w["pscale"], t)
    dq, dk, dv, *landed = _flash_bwd(a["q"], a["k"], a["v"], a["o"], dbrs[0], a["lse"], ta, riding)
    dpa, g["dwq"], g["dwqs"], g["dwk"], g["dwv"], g["dqn"], g["dkvn"] = _attn_prep_bwd(
        a["pa"], dq, dk, dv, cosf, sins, w["qn"], w["kvn"], w["wq"], w["wqs"], w["wk"], w["wv"], t)
    dx, g["dg1"], g["dwa"], g["dwm_in"] = _inproj_bwd(a["x"], w["g1"], dxmid, dpa, dpm, dpres, w["w_in"], w["wg"], tb)
    return dx, _layer_grads(g), (landed[0] if landed else None)


def _step(p, x, positions, loss_target):
    s = x.shape[0]
    depth = p["w_in"][0].shape[0]
    tiles = dict(tok=min(TOK_TILE, s), attn=min(ATTN_TILE, s), inproj_bwd=min(INPROJ_BWD_TILE, s),
                 wgrad=min(WGRAD_TILE, s))

    small = {n: p[n][0] for n in REPLICATED}
    cosf, sins = _rope_tables(positions)

    packed, w_offsets, w_shapes = _pack_shards(p, depth)
    gathered = _all_gather(packed[0], "weight_all_gather")
    weights, acts = [], []
    h = x
    for l in range(depth):
        weights.append(_layer_weights(_unpack_layer_weights(gathered, w_offsets, w_shapes), small, l))
        riding = None
        if l + 1 < depth:
            nxt = packed[l + 1]
            riding = _riding(_gather_plan, nxt, _sds((N_DEV,) + nxt.shape, nxt.dtype))
        h, saved, gathered = _layer_fwd(h, weights[l], cosf, sins, tiles, riding)
        acts.append(saved)
    dh, loss_blk = _loss_head(h, loss_target, tiles["tok"])
    loss = lax.psum(loss_blk[0, 0], ("x", "y", "c"))

    layer_grads = [None] * depth
    summed = [None] * depth
    send = None
    for l in reversed(range(depth)):
        riding = None if send is None else _riding(_exchange_plan, send, _sds(send.shape, send.dtype))
        dh, layer_grads[l], landed = _layer_bwd(dh, acts[l], weights[l], cosf, sins, tiles, riding)
        if landed is not None:
            summed[l + 1] = _sum_slots(landed, "grad_shard_sum")
        send, send_offsets = _pack_rows([layer_grads[l][n] for n in SHARDED], BF16, GRAD_ROW_PAD, lead=(N_DEV,))
    summed[0] = _sum_slots(_all_to_all(send, "grad_all_to_all"), "grad_shard_sum")
    block_shapes = [_owner_block_shape(n, p[n][0].shape) for n in SHARDED]
    blocks = _unpack_rows(jnp.stack(summed), send_offsets, block_shapes)
    g_shard = {n: _natural_shard(n, b) for n, b in zip(SHARDED, blocks)}
    small_grads = [jnp.stack([layer_grads[l][n] for l in range(depth)]) for n in REPLICATED]
    small_rows, small_offsets = _pack_rows(small_grads, F32, ROW_PAD)
    g_small_rows = _sum_slots(_all_gather(small_rows, "replicated_grad_all_gather"), "grad_replicated_sum")

    grad, delta, new_m, new_v = {}, {}, {}, {}
    for n in SHARDED:
        w, m, v = p[n]
        shape2 = (math.prod(w.shape[:-1]), w.shape[-1])
        grad[n] = g_shard[n]
        d, nm, nv = _adamw(w.reshape(shape2), grad[n].reshape(shape2), m.reshape(shape2), v.reshape(shape2), f"adamw_{n}")
        delta[n], new_m[n], new_v[n] = d.reshape(w.shape), nm.reshape(w.shape), nv.reshape(w.shape)
    packs = [_pack_rows([p[n][i] for n in REPLICATED], F32, ROW_PAD)[0] for i in range(3)]
    d, nm, nv = _adamw(packs[0], g_small_rows, packs[1], packs[2], "adamw_replicated")
    shapes = [p[n][0].shape for n in REPLICATED]
    for n, gg, dd, mm, vv in zip(REPLICATED, _unpack_rows(g_small_rows, small_offsets, shapes), _unpack_rows(d, small_offsets, shapes),
                                 _unpack_rows(nm, small_offsets, shapes), _unpack_rows(nv, small_offsets, shapes)):
        grad[n], delta[n], new_m[n], new_v[n] = gg, dd, mm, vv
    return loss, dh, grad, delta, new_m, new_v


WEIGHT_ORDER = ("norm_mix_pre", "w_in", "gate_b", "q_norm", "w_uq", "kv_norm", "w_ukv", "w_br_mla", "sg_ln_g", "sg_ln_b",
                "sg_w", "sg_b", "w_br_sg", "conv_w", "w_br_conv", "pool_w", "pool_scale", "w_br_pool", "w_out",
                "norm_mix_post", "norm_ffn_pre", "w_ff1", "w_ff2", "norm_ffn_post")


def kernel(x, positions, norm_mix_pre, w_in, gate_b, q_norm, w_uq, kv_norm, w_ukv, w_br_mla, sg_ln_g, sg_ln_b, sg_w, sg_b, w_br_sg, conv_w, w_br_conv, pool_w, pool_scale, w_br_pool, w_out, norm_mix_post, norm_ffn_pre, w_ff1, w_ff2, norm_ffn_post, loss_target, m_norm_mix_pre, m_w_in, m_gate_b, m_q_norm, m_w_uq, m_kv_norm, m_w_ukv, m_w_br_mla, m_sg_ln_g, m_sg_ln_b, m_sg_w, m_sg_b, m_w_br_sg, m_conv_w, m_w_br_conv, m_pool_w, m_pool_scale, m_w_br_pool, m_w_out, m_norm_mix_post, m_norm_ffn_pre, m_w_ff1, m_w_ff2, m_norm_ffn_post, v_norm_mix_pre, v_w_in, v_gate_b, v_q_norm, v_w_uq, v_kv_norm, v_w_ukv, v_w_br_mla, v_sg_ln_g, v_sg_ln_b, v_sg_w, v_sg_b, v_w_br_sg, v_conv_w, v_w_br_conv, v_pool_w, v_pool_scale, v_w_br_pool, v_w_out, v_norm_mix_post, v_norm_ffn_pre, v_w_ff1, v_w_ff2, v_norm_ffn_post):
    ws = (norm_mix_pre, w_in, gate_b, q_norm, w_uq, kv_norm, w_ukv, w_br_mla, sg_ln_g, sg_ln_b, sg_w, sg_b, w_br_sg, conv_w,
          w_br_conv, pool_w, pool_scale, w_br_pool, w_out, norm_mix_post, norm_ffn_pre, w_ff1, w_ff2, norm_ffn_post)
    ms = (m_norm_mix_pre, m_w_in, m_gate_b, m_q_norm, m_w_uq, m_kv_norm, m_w_ukv, m_w_br_mla, m_sg_ln_g, m_sg_ln_b, m_sg_w,
          m_sg_b, m_w_br_sg, m_conv_w, m_w_br_conv, m_pool_w, m_pool_scale, m_w_br_pool, m_w_out, m_norm_mix_post,
          m_norm_ffn_pre, m_w_ff1, m_w_ff2, m_norm_ffn_post)
    vs = (v_norm_mix_pre, v_w_in, v_gate_b, v_q_norm, v_w_uq, v_kv_norm, v_w_ukv, v_w_br_mla, v_sg_ln_g, v_sg_ln_b, v_sg_w,
          v_sg_b, v_w_br_sg, v_conv_w, v_w_br_conv, v_pool_w, v_pool_scale, v_w_br_pool, v_w_out, v_norm_mix_post,
          v_norm_ffn_pre, v_w_ff1, v_w_ff2, v_norm_ffn_post)
    p = {n: (w, m, v) for n, w, m, v in zip(WEIGHT_ORDER, ws, ms, vs)}
    loss, grad_x, grad, delta, new_m, new_v = _step(p, x[0], positions[0], loss_target[0])
    return (loss, grad_x[None], *[grad[n] for n in WEIGHT_ORDER], *[delta[n] for n in WEIGHT_ORDER],
            *[new_m[n] for n in WEIGHT_ORDER], *[new_v[n] for n in WEIGHT_ORDER])
```

```python
import functools
import math

import jax
import jax.numpy as jnp
from jax import lax
from jax.experimental import pallas as pl
from jax.experimental.pallas import tpu as pltpu

F32 = jnp.float32
BF16 = jnp.bfloat16

D_MODEL = 1024
N_HEADS = 4
D_NOPE = 64
D_ROPE = 32
D_V = 64
Q_RANK = 256
KV_RANK = 128
BR_W = 256
SG_CHUNK = 128
SG_GROUPS = 4
POOL_WINDOWS = (2, 4, 8, 16)
POOL_GROUP = 64
CONV_K = 3
D_FF = 4096
N_BRANCH = 4
N_IN = 6048
EPS = 1e-6
ROPE_BASE = 10000.0
ADAM_LR = 0.001
ADAM_B1 = 0.9
ADAM_B2 = 0.999
ADAM_EPS = 1e-08
ADAM_WD = 0.01
ADAM_STEP = 10

N_DEV = 8
LANES = 128
HEAD_PAD = 128
HALO = 16
PA_W = 640
PM_W = 6 * BR_W
VMEM_LIMIT = 56 * 1024 * 1024
SM_SCALE = (D_NOPE + D_ROPE) ** -0.5
SM_SCALE_LOG2 = SM_SCALE * math.log2(math.e)
TOK_TILE = 512
ATTN_TILE = 512
WGRAD_TILE = 1024
INPROJ_BWD_TILE = 512

NN = (((1,), (0,)), ((), ()))
NT = (((1,), (1,)), ((), ()))
TN = (((0,), (0,)), ((), ()))
MESH_ID = pl.DeviceIdType.MESH


def _dot(a, b, dims=NN):
    return lax.dot_general(a, b, dims, preferred_element_type=F32)


def _pcall(body, *, name, grid, in_specs, out_specs, out_shape, scratch=(), vmem=None, prefetch=0):
    params = pltpu.CompilerParams(vmem_limit_bytes=vmem)
    if prefetch:
        spec = pltpu.PrefetchScalarGridSpec(num_scalar_prefetch=prefetch, grid=grid, in_specs=in_specs,
                                            out_specs=out_specs, scratch_shapes=scratch)
        return pl.pallas_call(body, name=name, grid_spec=spec, out_shape=out_shape, compiler_params=params)
    return pl.pallas_call(
        body, name=name, grid=grid, in_specs=in_specs, out_specs=out_specs, out_shape=out_shape,
        scratch_shapes=scratch, compiler_params=params)


def _rows(t, width):
    return pl.BlockSpec((t, width), lambda i: (i, 0))


def _whole(shape):
    nd = len(shape)
    return pl.BlockSpec(tuple(shape), lambda *_: (0,) * nd, pipeline_mode=pl.Buffered(1))


def _acc(shape):
    nd = len(shape)
    return pl.BlockSpec(tuple(shape), lambda *_: (0,) * nd)


def _sds(shape, dtype):
    return jax.ShapeDtypeStruct(tuple(shape), dtype)


def _rms(x, g):
    return x * lax.rsqrt(jnp.mean(x * x, axis=-1, keepdims=True) + EPS) * g


def _rms_bwd(x, g, dy):
    r = lax.rsqrt(jnp.mean(x * x, axis=-1, keepdims=True) + EPS)
    xh = x * r
    dg = jnp.sum(dy * xh, axis=0, keepdims=True)
    dxh = dy * g
    dx = r * (dxh - xh * jnp.mean(dxh * xh, axis=-1, keepdims=True))
    return dx, dg


def _sigmoid(x):
    return 1.0 / (1.0 + jnp.exp(-x))


def _gelu(x):
    return jax.nn.gelu(x, approximate=True)


def _accumulate(ref, val, first):
    @pl.when(first)
    def _():
        ref[...] = val

    @pl.when(jnp.logical_not(first))
    def _():
        ref[...] += val


def _accumulate_out(acc_ref, out_ref, val, first, last):
    _accumulate(acc_ref, val, first)

    @pl.when(last)
    def _():
        out_ref[...] = acc_ref[...].astype(out_ref.dtype)


def _inproj_fwd(x, g1, w_in, t):
    s = x.shape[0]

    def body(x_ref, g_ref, w_ref, hb_ref, pa_ref, pm_ref):
        hb = _rms(x_ref[...], g_ref[...]).astype(BF16)
        hb_ref[...] = hb
        pa_ref[...] = _dot(hb, w_ref[:PA_W, :], NT)
        pm_ref[...] = _dot(hb, w_ref[PA_W:, :], NT)

    return _pcall(
        body, name="inproj_fwd", grid=(s // t,),
        in_specs=[_rows(t, D_MODEL), _whole((1, D_MODEL)), _whole(w_in.shape)],
        out_specs=[_rows(t, D_MODEL), _rows(t, PA_W), _rows(t, PM_W)],
        out_shape=[_sds((s, D_MODEL), BF16), _sds((s, PA_W), F32), _sds((s, PM_W), F32)],
        vmem=VMEM_LIMIT)(x, g1, w_in)


def _attn_prep_fwd(pa, cosf, sins, qn, kvn, wq, wqs, wk, wv, t):
    s = pa.shape[0]

    def body(pa_ref, cos_ref, sin_ref, qn_ref, kvn_ref, wq_ref, wqs_ref, wk_ref, wv_ref, q_ref, k_ref, v_ref):
        cosv, sinv = cos_ref[...], sin_ref[...]
        cqn = _rms(pa_ref[:, 0:Q_RANK], qn_ref[...]).astype(BF16)
        ckvn = _rms(pa_ref[:, Q_RANK:Q_RANK + KV_RANK], kvn_ref[...]).astype(BF16)
        k_rope = pa_ref[:, 384:512] * cosv + pa_ref[:, 512:640] * sinv
        q_all, qs_all = _dot(cqn, wq_ref[...], NT), _dot(cqn, wqs_ref[...], NT)
        k_all = _dot(ckvn, wk_ref[...], NT)
        for h in range(N_HEADS):
            lanes = slice(h * HEAD_PAD, (h + 1) * HEAD_PAD)
            q_ref[h] = (q_all[:, lanes] * cosv + qs_all[:, lanes] * sinv).astype(BF16)
            k_ref[h] = (k_all[:, lanes] + k_rope).astype(BF16)
        v_ref[...] = _dot(ckvn, wv_ref[...], NT).astype(BF16)

    head_blk = pl.BlockSpec((N_HEADS, t, HEAD_PAD), lambda i: (0, i, 0))
    return _pcall(
        body, name="attn_prep_fwd", grid=(s // t,),
        in_specs=[_rows(t, PA_W), _rows(t, LANES), _rows(t, LANES), _whole(qn.shape), _whole(kvn.shape),
                  _whole(wq.shape), _whole(wqs.shape), _whole(wk.shape), _whole(wv.shape)],
        out_specs=[head_blk, head_blk, _rows(t, BR_W)],
        out_shape=[_sds((N_HEADS, s, HEAD_PAD), BF16), _sds((N_HEADS, s, HEAD_PAD), BF16), _sds((s, BR_W), BF16)],
        vmem=VMEM_LIMIT)(pa, cosf, sins, qn, kvn, wq, wqs, wk, wv)


def _causal_scores(q, k, masked):
    sc = _dot(q, k, NT) * SM_SCALE_LOG2
    if masked:
        row = lax.broadcasted_iota(jnp.int32, sc.shape, 0)
        col = lax.broadcasted_iota(jnp.int32, sc.shape, 1)
        sc = jnp.where(col <= row, sc, -jnp.inf)
    return sc


def _causal_steps(n, key_major):
    if key_major:
        pairs = [(qi, kj) for kj in range(n) for qi in range(kj, n)]
    else:
        pairs = [(qi, kj) for qi in range(n) for kj in range(qi + 1)]
    return (jnp.asarray([p[0] for p in pairs], jnp.int32), jnp.asarray([p[1] for p in pairs], jnp.int32))


def _flash_fwd(q, k, v, t, riding=None):
    s = v.shape[0]
    n = s // t
    reps = t // LANES
    qi_tab, kj_tab = _causal_steps(n, key_major=False)
    n_steps = int(qi_tab.shape[0])

    def body(qi_ref, kj_ref, q_ref, k_ref, v_ref, *rest):
        if riding is None:
            o_ref, lse_ref, m_scr, l_scr, acc_scr = rest
        else:
            x_ref, o_ref, lse_ref, land_ref, m_scr, l_scr, acc_scr, send_sems, recv_sems, local_sem = rest
        step_id = pl.program_id(0)
        qi, kj = qi_ref[step_id], kj_ref[step_id]
        if riding is not None:
            at_end = _ride_along(riding, step_id, n_steps, (x_ref, land_ref, send_sems, recv_sems, local_sem))

        @pl.when(kj == 0)
        def _():
            m_scr[...] = jnp.full(m_scr.shape, -jnp.inf, F32)
            l_scr[...] = jnp.zeros(l_scr.shape, F32)
            acc_scr[...] = jnp.zeros(acc_scr.shape, F32)

        def step(masked):
            for h in range(N_HEADS):
                sc = _causal_scores(q_ref[h], k_ref[h], masked)
                m_prev = m_scr[h]
                m_next = jnp.maximum(m_prev, jnp.max(sc, axis=1, keepdims=True))
                alpha = jnp.exp2(m_prev - m_next)
                p = jnp.exp2(sc - jnp.tile(m_next, (1, reps)))
                l_scr[h] = alpha * l_scr[h] + jnp.sum(p, axis=1, keepdims=True)
                m_scr[h] = m_next
                pair = (h // 2) * LANES
                acc_scr[h] = acc_scr[h] * alpha + _dot(p.astype(BF16), v_ref[:, pair:pair + LANES])

        @pl.when(kj < qi)
        def _():
            step(False)

        @pl.when(kj == qi)
        def _():
            step(True)
            lane = lax.broadcasted_iota(jnp.int32, (t, LANES), 1)
            for pr in range(N_HEADS // 2):
                o0 = acc_scr[2 * pr] / l_scr[2 * pr]
                o1 = acc_scr[2 * pr + 1] / l_scr[2 * pr + 1]
                o_ref[:, pr * LANES:(pr + 1) * LANES] = jnp.where(lane < D_V, o0, o1).astype(BF16)
            for h in range(N_HEADS):
                lse_ref[h] = m_scr[h] + jnp.log2(l_scr[h])

        if riding is not None:
            at_end()

    q_blk = pl.BlockSpec((N_HEADS, t, HEAD_PAD), lambda i, qi, kj: (0, qi[i], 0))
    k_blk = pl.BlockSpec((N_HEADS, t, HEAD_PAD), lambda i, qi, kj: (0, kj[i], 0))
    v_blk = pl.BlockSpec((t, BR_W), lambda i, qi, kj: (kj[i], 0))
    extra = riding is not None
    return _pcall(
        body, name="flash_fwd_gathering" if extra else "flash_fwd", grid=(n_steps,), prefetch=2,
        in_specs=[q_blk, k_blk, v_blk] + [HBM_REF] * extra,
        out_specs=[pl.BlockSpec((t, BR_W), lambda i, qi, kj: (qi[i], 0)), q_blk] + [HBM_REF] * extra,
        out_shape=[_sds((s, BR_W), BF16), _sds((N_HEADS, s, LANES), F32)] + ([riding[2]] if extra else []),
        scratch=[pltpu.VMEM((N_HEADS, t, LANES), F32)] * 3 + (list(EXCHANGE_SEMS) if extra else []),
        vmem=VMEM_LIMIT)(qi_tab, kj_tab, q, k, v, *([riding[1]] if extra else []))


def _shift_down(ext, k, t):
    return pltpu.roll(ext, k, 0)[HALO:HALO + t]


def _shift_up(ext, k, t):
    return pltpu.roll(ext, t + HALO - k, 0)[0:t]


def _lane_group(shape):
    return lax.shift_right_logical(lax.broadcasted_iota(jnp.int32, shape, 1), 6)


def _group_select(vals):
    grp = _lane_group(vals[0].shape)
    out = vals[0]
    for g in range(1, len(vals)):
        out = jnp.where(grp == g, vals[g], out)
    return out


def _layernorm(x, g, b):
    mu = jnp.mean(x, axis=-1, keepdims=True)
    xc = x - mu
    return xc * lax.rsqrt(jnp.mean(xc * xc, axis=-1, keepdims=True) + EPS) * g + b


def _sg_mix(wm_ref, vnb, bias):
    return _group_select([_dot(wm_ref[g], vnb) for g in range(SG_GROUPS)]) + bias


def _pool_windows(ext):
    s2 = ext + pltpu.roll(ext, 1, 0)
    s4 = s2 + pltpu.roll(s2, 2, 0)
    s8 = s4 + pltpu.roll(s4, 4, 0)
    s16 = s8 + pltpu.roll(s8, 8, 0)
    return [s2, s4, s8, s16]


def _pool_counts(tok):
    return [jnp.minimum(tok + 1, w).astype(F32) for w in POOL_WINDOWS]


def _halo_before(t):
    return pl.BlockSpec((HALO, PM_W), lambda i: (jnp.maximum(i * (t // HALO) - 1, 0), 0))


def _mixers_fwd(pm, ln_g, ln_b, wm, sgb, conv_w, wp, pscale, t):
    s = pm.shape[0]

    def body(pm_ref, halo_ref, lng_ref, lnb_ref, wm_ref, sgb_ref, cw_ref, wp_ref, ps_ref, bsg_ref, bcv_ref, bpl_ref):
        i = pl.program_id(0)
        halo = jnp.where(i > 0, halo_ref[...], 0.0)
        u = _gelu(pm_ref[:, 0:256])
        vnb = _layernorm(_gelu(pm_ref[:, 256:512]), lng_ref[...], lnb_ref[...]).astype(BF16)
        for c in range(t // SG_CHUNK):
            rows = slice(c * SG_CHUNK, (c + 1) * SG_CHUNK)
            bsg_ref[rows, :] = (u[rows] * _sg_mix(wm_ref, vnb[rows], sgb_ref[...])).astype(BF16)
        z = pm_ref[:, 1024:1280] * pm_ref[:, 512:768]
        zext = jnp.concatenate([halo[:, 1024:1280] * halo[:, 512:768], z], axis=0)
        y = cw_ref[0:1, :] * _shift_down(zext, 2, t) + cw_ref[1:2, :] * _shift_down(zext, 1, t) + cw_ref[2:3, :] * z
        bcv_ref[...] = (pm_ref[:, 768:1024] * y).astype(BF16)
        p = pm_ref[:, 1280:1536]
        sums = _pool_windows(jnp.concatenate([halo[:, 1280:1536], p], axis=0))
        tok = i * t + lax.broadcasted_iota(jnp.int32, (t, 1), 0)
        pooled = _group_select([sw[HALO:HALO + t] / cnt - p for sw, cnt in zip(sums, _pool_counts(tok))])
        bpl_ref[...] = (_dot(pooled.astype(BF16), wp_ref[...]) * ps_ref[...]).astype(BF16)

    return _pcall(
        body, name="mixers_fwd", grid=(s // t,),
        in_specs=[_rows(t, PM_W), _halo_before(t), _whole(ln_g.shape), _whole(ln_b.shape), _whole(wm.shape),
                  _whole(sgb.shape), _whole(conv_w.shape), _whole(wp.shape), _whole(pscale.shape)],
        out_specs=[_rows(t, BR_W)] * 3,
        out_shape=[_sds((s, BR_W), BF16)] * 3,
        vmem=VMEM_LIMIT)(pm, pm, ln_g, ln_b, wm, sgb, conv_w, wp, pscale)


def _merge_fwd(x, hb, branches, wg, bg, wbr, wout, g2, t):
    s = x.shape[0]

    def body(x_ref, hb_ref, b0_ref, b1_ref, b2_ref, b3_ref, wg_ref, bg_ref, wbr_ref, wout_ref, g2_ref,
             xmid_ref, mrg_ref, t_ref):
        hb = hb_ref[...]
        merged = jnp.zeros((t, D_MODEL), F32)
        for k, b_ref in enumerate((b0_ref, b1_ref, b2_ref, b3_ref)):
            cols = slice(k * D_MODEL, (k + 1) * D_MODEL)
            gate = _sigmoid(_dot(hb, wg_ref[cols, :], NT) + bg_ref[:, cols])
            merged = merged + gate * _dot(b_ref[...], wbr_ref[k], NT)
        mb = merged.astype(BF16)
        mrg_ref[...] = mb
        tt = _dot(mb, wout_ref[...])
        t_ref[...] = tt
        xmid_ref[...] = x_ref[...] + _rms(tt, g2_ref[...])

    return _pcall(
        body, name="merge_fwd", grid=(s // t,),
        in_specs=[_rows(t, D_MODEL), _rows(t, D_MODEL)] + [_rows(t, BR_W)] * 4 +
                 [_whole(wg.shape), _whole(bg.shape), _whole(wbr.shape), _whole(wout.shape), _whole(g2.shape)],
        out_specs=[_rows(t, D_MODEL)] * 3,
        out_shape=[_sds((s, D_MODEL), F32), _sds((s, D_MODEL), BF16), _sds((s, D_MODEL), F32)],
        vmem=VMEM_LIMIT)(x, hb, *branches, wg, bg, wbr, wout, g2)


def _ffn_fwd(x, g3, w1, w2, g4, t):
    s = x.shape[0]

    def body(x_ref, g3_ref, w1_ref, w2_ref, g4_ref, xo_ref, f_ref):
        h = _rms(x_ref[...], g3_ref[...]).astype(BF16)
        f = jnp.zeros((t, D_MODEL), F32)
        for j in range(D_FF // D_MODEL):
            cols = slice(j * D_MODEL, (j + 1) * D_MODEL)
            r = jnp.square(jnp.maximum(_dot(h, w1_ref[cols, :], NT), 0.0)).astype(BF16)
            f = f + _dot(r, w2_ref[cols, :])
        f_ref[...] = f
        xo_ref[...] = x_ref[...] + _rms(f, g4_ref[...])

    return _pcall(
        body, name="ffn_fwd", grid=(s // t,),
        in_specs=[_rows(t, D_MODEL), _whole(g3.shape), _whole(w1.shape), _whole(w2.shape), _whole(g4.shape)],
        out_specs=[_rows(t, D_MODEL)] * 2,
        out_shape=[_sds((s, D_MODEL), F32)] * 2,
        vmem=VMEM_LIMIT)(x, g3, w1, w2, g4)


def _loss_head(y, target, t):
    s = y.shape[0]
    n = s // t

    def body(y_ref, tg_ref, dy_ref, loss_ref, acc_scr):
        i = pl.program_id(0)
        d = y_ref[...] - tg_ref[...]
        dy_ref[...] = d * (1.0 / D_MODEL)
        _accumulate(acc_scr, jnp.sum(d * d, axis=0, keepdims=True), i == 0)

        @pl.when(i == n - 1)
        def _():
            loss_ref[...] = jnp.full(loss_ref.shape, 0.5 / D_MODEL, F32) * jnp.sum(acc_scr[...])

    return _pcall(
        body, name="loss_head", grid=(n,),
        in_specs=[_rows(t, D_MODEL)] * 2,
        out_specs=[_rows(t, D_MODEL), _acc((8, LANES))],
        out_shape=[_sds((s, D_MODEL), F32), _sds((8, LANES), F32)],
        scratch=[pltpu.VMEM((1, D_MODEL), F32)])(y, target)


def _ffn_bwd_norms(x_mid, f, dxo, g3, g4, t):
    s = x_mid.shape[0]

    def body(x_ref, f_ref, dxo_ref, g3_ref, g4_ref, h2_ref, df_ref, dg4_ref):
        h2_ref[...] = _rms(x_ref[...], g3_ref[...]).astype(BF16)
        df, dg4 = _rms_bwd(f_ref[...], g4_ref[...], dxo_ref[...])
        df_ref[...] = df.astype(BF16)
        _accumulate(dg4_ref, dg4, pl.program_id(0) == 0)

    return _pcall(
        body, name="ffn_bwd_norms", grid=(s // t,),
        in_specs=[_rows(t, D_MODEL)] * 3 + [_whole(g3.shape), _whole(g4.shape)],
        out_specs=[_rows(t, D_MODEL), _rows(t, D_MODEL), _acc((1, D_MODEL))],
        out_shape=[_sds((s, D_MODEL), BF16), _sds((s, D_MODEL), BF16), _sds((1, D_MODEL), F32)])(x_mid, f, dxo, g3, g4)


def _ffn_bwd_weights(h2, df, w1, w2, t):
    s = h2.shape[0]
    blk = D_MODEL
    n = s // t

    def body(h2_ref, df_ref, w1_ref, w2_ref, da_ref, dw1_ref, dw2_ref, acc1, acc2):
        first, last = pl.program_id(1) == 0, pl.program_id(1) == n - 1
        h2v, dfv = h2_ref[...], df_ref[...]
        rl = jnp.maximum(_dot(h2v, w1_ref[...], NT), 0.0)
        _accumulate_out(acc2, dw2_ref, _dot((rl * rl).astype(BF16), dfv, TN), first, last)
        da = (_dot(dfv, w2_ref[...], NT) * (2.0 * rl)).astype(BF16)
        da_ref[...] = da
        _accumulate_out(acc1, dw1_ref, _dot(da, h2v, TN), first, last)

    tok = pl.BlockSpec((t, D_MODEL), lambda j, i: (i, 0))
    hid = pl.BlockSpec((blk, D_MODEL), lambda j, i: (j, 0))
    return _pcall(
        body, name="ffn_bwd_weights", grid=(D_FF // blk, n),
        in_specs=[tok, tok, hid, hid],
        out_specs=[pl.BlockSpec((t, blk), lambda j, i: (i, j)), hid, hid],
        out_shape=[_sds((s, D_FF), BF16), _sds((D_FF, D_MODEL), BF16), _sds((D_FF, D_MODEL), BF16)],
        scratch=[pltpu.VMEM((blk, D_MODEL), F32)] * 2,
        vmem=VMEM_LIMIT)(h2, df, w1, w2)


def _ffn_bwd_input(da, w1, x_mid, dxo, g3, t):
    s = x_mid.shape[0]

    def body(da_ref, w1_ref, x_ref, dxo_ref, g3_ref, dx_ref, dg3_ref):
        dx, dg3 = _rms_bwd(x_ref[...], g3_ref[...], _dot(da_ref[...], w1_ref[...]))
        dx_ref[...] = dxo_ref[...] + dx
        _accumulate(dg3_ref, dg3, pl.program_id(0) == 0)

    return _pcall(
        body, name="ffn_bwd_input", grid=(s // t,),
        in_specs=[_rows(t, D_FF), _whole(w1.shape), _rows(t, D_MODEL), _rows(t, D_MODEL), _whole(g3.shape)],
        out_specs=[_rows(t, D_MODEL), _acc((1, D_MODEL))],
        out_shape=[_sds((s, D_MODEL), F32), _sds((1, D_MODEL), F32)],
        vmem=VMEM_LIMIT)(da, w1, x_mid, dxo, g3)


def _merge_bwd_out(tt, dxmid, g2, wout, merged, t):
    s = tt.shape[0]
    n = s // t

    def body(t_ref, dx_ref, g2_ref, wout_ref, mrg_ref, dm_ref, dwout_ref, dg2_ref, acc):
        first, last = pl.program_id(0) == 0, pl.program_id(0) == n - 1
        dt, dg2 = _rms_bwd(t_ref[...], g2_ref[...], dx_ref[...])
        dtb = dt.astype(BF16)
        dm_ref[...] = _dot(dtb, wout_ref[...], NT)
        _accumulate_out(acc, dwout_ref, _dot(mrg_ref[...], dtb, TN), first, last)
        _accumulate(dg2_ref, dg2, first)

    return _pcall(
        body, name="merge_bwd_out", grid=(n,),
        in_specs=[_rows(t, D_MODEL), _rows(t, D_MODEL), _whole(g2.shape), _whole(wout.shape), _rows(t, D_MODEL)],
        out_specs=[_rows(t, D_MODEL), _acc((D_MODEL, D_MODEL)), _acc((1, D_MODEL))],
        out_shape=[_sds((s, D_MODEL), F32), _sds((D_MODEL, D_MODEL), BF16), _sds((1, D_MODEL), F32)],
        scratch=[pltpu.VMEM((D_MODEL, D_MODEL), F32)],
        vmem=VMEM_LIMIT)(tt, dxmid, g2, wout, merged)


def _merge_bwd_branch(k, hb, br, dm, wg, bg, wbr, t):
    s = hb.shape[0]
    n = s // t

    def body(hb_ref, br_ref, dm_ref, wg_ref, bg_ref, wbr_ref, dbr_ref, dpre_ref, dwg_ref, dbg_ref, dwbr_ref,
             acc_g, acc_br):
        first, last = pl.program_id(0) == 0, pl.program_id(0) == n - 1
        hbv, brv, dmv, wbrv = hb_ref[...], br_ref[...], dm_ref[...], wbr_ref[0]
        gate = _sigmoid(_dot(hbv, wg_ref[...], NT) + bg_ref[...])
        dy = (dmv * gate).astype(BF16)
        dpre = dmv * _dot(brv, wbrv, NT) * gate * (1.0 - gate)
        dpb = dpre.astype(BF16)
        dpre_ref[...] = dpb
        dbr_ref[...] = _dot(dy, wbrv)
        _accumulate_out(acc_br, dwbr_ref, _dot(dy, brv, TN), first, last)
        _accumulate_out(acc_g, dwg_ref, _dot(dpb, hbv, TN), first, last)
        _accumulate(dbg_ref, jnp.sum(dpre, axis=0, keepdims=True), first)

    return _pcall(
        body, name=f"merge_bwd_branch{k}", grid=(n,),
        in_specs=[_rows(t, D_MODEL), _rows(t, BR_W), _rows(t, D_MODEL),
                  pl.BlockSpec((D_MODEL, D_MODEL), lambda i: (k, 0)), pl.BlockSpec((1, D_MODEL), lambda i: (0, k)),
                  pl.BlockSpec((1, D_MODEL, BR_W), lambda i: (k, 0, 0))],
        out_specs=[_rows(t, BR_W), _rows(t, D_MODEL), _acc((D_MODEL, D_MODEL)), _acc((1, D_MODEL)), _acc((D_MODEL, BR_W))],
        out_shape=[_sds((s, BR_W), F32), _sds((s, D_MODEL), BF16), _sds((D_MODEL, D_MODEL), BF16),
                   _sds((1, D_MODEL), F32), _sds((D_MODEL, BR_W), BF16)],
        scratch=[pltpu.VMEM((D_MODEL, D_MODEL), F32), pltpu.VMEM((D_MODEL, BR_W), F32)],
        vmem=VMEM_LIMIT)(hb, br, dm, wg, bg, wbr)


def _mixers_bwd(pm, dbsg, dbcv, dbpl, ln_g, ln_b, wm, wmt, sgb, conv_w, wp, pscale, t):
    s = pm.shape[0]
    n = s // t
    nb = t // HALO

    def body(pm_ref, before_ref, after_ref, dsg_ref, dcv_ref, dcva_ref, dpl_ref, dpla_ref, lng_ref, lnb_ref,
             wm_ref, wmt_ref, sgb_ref, cw_ref, wp_ref, ps_ref,
             dpm_ref, dlng_ref, dlnb_ref, dwm_ref, dsgb_ref, dcw_ref, dwp_ref, dps_ref):
        i = pl.program_id(0)
        first = i == 0
        before = jnp.where(i > 0, before_ref[...], 0.0)
        after = jnp.where(i < n - 1, after_ref[...], 0.0)

        u_raw, v_raw = pm_ref[:, 0:256], pm_ref[:, 256:512]
        lng, lnb = lng_ref[...], lnb_ref[...]
        u, gelu_u_vjp = jax.vjp(_gelu, u_raw)
        vn, norm_vjp = jax.vjp(lambda v_, g_, b_: _layernorm(_gelu(v_), g_, b_), v_raw, lng, lnb)
        vnb = vn.astype(BF16)
        dsg = dsg_ref[...]
        grp = _lane_group((SG_CHUNK, BR_W))
        tri = (lax.broadcasted_iota(jnp.int32, (SG_CHUNK, SG_CHUNK), 1)
               <= lax.broadcasted_iota(jnp.int32, (SG_CHUNK, SG_CHUNK), 0))
        du_parts, dvn_parts = [], []
        dsgb = jnp.zeros((SG_CHUNK, BR_W), F32)
        dwm = [jnp.zeros((SG_CHUNK, SG_CHUNK), F32) for _ in range(SG_GROUPS)]
        for c in range(t // SG_CHUNK):
            rows = slice(c * SG_CHUNK, (c + 1) * SG_CHUNK)
            mix = _sg_mix(wm_ref, vnb[rows], sgb_ref[...])
            du_parts.append(dsg[rows] * mix)
            ds = dsg[rows] * u[rows]
            dsgb = dsgb + ds
            dsb = [jnp.where(grp == g, ds, 0.0).astype(BF16) for g in range(SG_GROUPS)]
            for g in range(SG_GROUPS):
                dwm[g] = dwm[g] + _dot(dsb[g], vnb[rows], NT)
            dvn_parts.append(_group_select([_dot(wmt_ref[g], dsb[g]) for g in range(SG_GROUPS)]))
        (du_raw,) = gelu_u_vjp(jnp.concatenate(du_parts, axis=0))
        dv_raw, dlng, dlnb = norm_vjp(jnp.concatenate(dvn_parts, axis=0))
        dpm_ref[:, 0:256] = du_raw.astype(BF16)
        dpm_ref[:, 256:512] = dv_raw.astype(BF16)
        _accumulate(dlng_ref, dlng, first)
        _accumulate(dlnb_ref, dlnb, first)
        _accumulate(dsgb_ref, dsgb, first)
        for g in range(SG_GROUPS):
            _accumulate(dwm_ref.at[g], jnp.where(tri, dwm[g], 0.0), first)

        xin, bg, cg = pm_ref[:, 512:768], pm_ref[:, 768:1024], pm_ref[:, 1024:1280]
        z = cg * xin
        zext = jnp.concatenate([before[:, 1024:1280] * before[:, 512:768], z], axis=0)
        z1, z2 = _shift_down(zext, 1, t), _shift_down(zext, 2, t)
        w0, w1, w2 = cw_ref[0:1, :], cw_ref[1:2, :], cw_ref[2:3, :]
        dcv = dcv_ref[...]
        y = w0 * z2 + w1 * z1 + w2 * z
        dy = dcv * bg
        dyext = jnp.concatenate([dy, jnp.where(i < n - 1, dcva_ref[...], 0.0) * after[:, 768:1024]], axis=0)
        dz = w2 * dy + w1 * _shift_up(dyext, 1, t) + w0 * _shift_up(dyext, 2, t)
        dpm_ref[:, 512:768] = (dz * cg).astype(BF16)
        dpm_ref[:, 768:1024] = (dcv * y).astype(BF16)
        dpm_ref[:, 1024:1280] = (dz * xin).astype(BF16)
        dcw = jnp.concatenate([jnp.sum(dy * z2, axis=0, keepdims=True), jnp.sum(dy * z1, axis=0, keepdims=True),
                               jnp.sum(dy * z, axis=0, keepdims=True)], axis=0)
        _accumulate(dcw_ref, jnp.concatenate([dcw, jnp.zeros((8 - CONV_K, BR_W), F32)], axis=0), first)

        p = pm_ref[:, 1280:1536]
        tok = i * t + lax.broadcasted_iota(jnp.int32, (t, 1), 0)
        sums = _pool_windows(jnp.concatenate([before[:, 1280:1536], p], axis=0))
        pooled = _group_select([sw[HALO:HALO + t] / cnt - p for sw, cnt in zip(sums, _pool_counts(tok))]).astype(BF16)
        mixed = _dot(pooled, wp_ref[...])
        dpl = dpl_ref[...]
        ps = ps_ref[...]
        dmix = (dpl * ps).astype(BF16)
        _accumulate(dps_ref, jnp.sum(dpl * mixed, axis=0, keepdims=True), first)
        _accumulate(dwp_ref, _dot(pooled, dmix, TN), first)
        dmix_after = (jnp.where(i < n - 1, dpla_ref[...], 0.0) * ps).astype(BF16)
        dpo = _dot(dmix, wp_ref[...], NT)
        dpo_ext = jnp.concatenate([dpo, _dot(dmix_after, wp_ref[...], NT)], axis=0)
        tok_ext = i * t + lax.broadcasted_iota(jnp.int32, (t + HALO, 1), 0)
        dp_groups = []
        for g, (win, cnt) in enumerate(zip(POOL_WINDOWS, _pool_counts(tok_ext))):
            e = dpo_ext / cnt
            acc = e
            span = 1
            while span < win:
                acc = acc + pltpu.roll(acc, t + HALO - span, 0)
                span *= 2
            dp_groups.append(acc[0:t] - dpo)
        dpm_ref[:, 1280:1536] = _group_select(dp_groups).astype(BF16)

    row_blk = lambda w: pl.BlockSpec((t, w), lambda i: (i, 0))
    after_blk = lambda w: pl.BlockSpec((HALO, w), lambda i: (jnp.minimum((i + 1) * nb, n * nb - 1), 0))
    return _pcall(
        body, name="mixers_bwd", grid=(n,),
        in_specs=[row_blk(PM_W), _halo_before(t), after_blk(PM_W), row_blk(BR_W), row_blk(BR_W), after_blk(BR_W),
                  row_blk(BR_W), after_blk(BR_W), _whole(ln_g.shape), _whole(ln_b.shape), _whole(wm.shape),
                  _whole(wmt.shape), _whole(sgb.shape), _whole(conv_w.shape), _whole(wp.shape), _whole(pscale.shape)],
        out_specs=[row_blk(PM_W), _acc((1, BR_W)), _acc((1, BR_W)), _acc((SG_GROUPS, SG_CHUNK, SG_CHUNK)),
                   _acc((SG_CHUNK, BR_W)), _acc((8, BR_W)), _acc((BR_W, BR_W)), _acc((1, BR_W))],
        out_shape=[_sds((s, PM_W), BF16), _sds((1, BR_W), F32), _sds((1, BR_W), F32),
                   _sds((SG_GROUPS, SG_CHUNK, SG_CHUNK), F32), _sds((SG_CHUNK, BR_W), F32), _sds((8, BR_W), F32),
                   _sds((BR_W, BR_W), F32), _sds((1, BR_W), F32)],
        vmem=VMEM_LIMIT)(pm, pm, pm, dbsg, dbcv, dbcv, dbpl, dbpl, ln_g, ln_b, wm, wmt, sgb, conv_w, wp, pscale)


def _head_delta(o_ref, do_ref, t):
    prod = o_ref[...].astype(F32) * do_ref[...]
    lane = lax.broadcasted_iota(jnp.int32, (t, LANES), 1)
    out = []
    for h in range(N_HEADS):
        pair = prod[:, (h // 2) * LANES:(h // 2 + 1) * LANES]
        mine = (lane < D_V) if h % 2 == 0 else (lane >= D_V)
        out.append(jnp.broadcast_to(jnp.sum(jnp.where(mine, pair, 0.0), axis=1, keepdims=True), (t, LANES)))
    return out


def _head_do(do_ref, h, t):
    lane = lax.broadcasted_iota(jnp.int32, (t, LANES), 1)
    mine = (lane < D_V) if h % 2 == 0 else (lane >= D_V)
    pair = do_ref[:, (h // 2) * LANES:(h // 2 + 1) * LANES]
    return jnp.where(mine, pair, 0.0).astype(BF16)


def _flash_bwd(q, k, v, o, do, lse, t, riding=None):
    s = v.shape[0]
    n = s // t
    reps = t // LANES

    qi_tab, kj_tab = _causal_steps(n, key_major=True)
    n_steps = int(qi_tab.shape[0])

    def body(qi_ref, kj_ref, q_ref, k_ref, v_ref, o_ref, do_ref, lse_ref, *rest):
        if riding is None:
            dq_ref, dk_ref, dv_ref, dv_scr = rest
        else:
            x_ref, dq_ref, dk_ref, dv_ref, land_ref, dv_scr, send_sems, recv_sems, local_sem = rest
        step_id = pl.program_id(0)
        qi, kj = qi_ref[step_id], kj_ref[step_id]
        if riding is not None:
            at_end = _ride_along(riding, step_id, n_steps, (x_ref, land_ref, send_sems, recv_sems, local_sem))

        @pl.when(step_id == 0)
        def _():
            dq_ref[...] = jnp.zeros(dq_ref.shape, F32)

        @pl.when(qi == kj)
        def _():
            dk_ref[...] = jnp.zeros(dk_ref.shape, F32)
            dv_scr[...] = jnp.zeros(dv_scr.shape, F32)

        def step(masked):
            delta = _head_delta(o_ref, do_ref, t)
            q_rows = pl.ds(pl.multiple_of(qi * t, t), t)
            for h in range(N_HEADS):
                qh, kh = q_ref[h], k_ref[h]
                p = jnp.exp2(_causal_scores(qh, kh, masked) - jnp.tile(lse_ref[h], (1, reps)))
                pair = slice((h // 2) * LANES, (h // 2 + 1) * LANES)
                dv_scr[h] += _dot(p.astype(BF16), do_ref[:, pair].astype(BF16), TN)
                dp = _dot(_head_do(do_ref, h, t), v_ref[:, pair], NT)
                ds = (p * (dp - jnp.tile(delta[h], (1, reps))) * SM_SCALE).astype(BF16)
                dk_ref[h] += _dot(ds, qh, TN)
                dq_ref[h, q_rows, :] += _dot(ds, kh)

        @pl.when(qi > kj)
        def _():
            step(False)

        @pl.when(qi == kj)
        def _():
            step(True)

        @pl.when(qi == n - 1)
        def _():
            lane = lax.broadcasted_iota(jnp.int32, (t, LANES), 1)
            for pr in range(N_HEADS // 2):
                dv_ref[:, pr * LANES:(pr + 1) * LANES] = jnp.where(lane < D_V, dv_scr[2 * pr], dv_scr[2 * pr + 1]).astype(BF16)

        if riding is not None:
            at_end()

    q_blk = pl.BlockSpec((N_HEADS, t, HEAD_PAD), lambda i, qi, kj: (0, qi[i], 0))
    k_blk = pl.BlockSpec((N_HEADS, t, HEAD_PAD), lambda i, qi, kj: (0, kj[i], 0))
    v_blk = pl.BlockSpec((t, BR_W), lambda i, qi, kj: (kj[i], 0))
    o_blk = pl.BlockSpec((t, BR_W), lambda i, qi, kj: (qi[i], 0))
    extra = riding is not None
    dq_sds = _sds((N_HEADS, s, HEAD_PAD), F32)
    return _pcall(
        body, name="flash_bwd_exchanging" if extra else "flash_bwd", grid=(n_steps,), prefetch=2,
        in_specs=[q_blk, k_blk, v_blk, o_blk, o_blk, q_blk] + [HBM_REF] * extra,
        out_specs=[_whole(dq_sds.shape), k_blk, v_blk] + [HBM_REF] * extra,
        out_shape=[dq_sds, dq_sds, _sds((s, BR_W), BF16)] + ([riding[2]] if extra else []),
        scratch=[pltpu.VMEM((N_HEADS, t, LANES), F32)] + (list(EXCHANGE_SEMS) if extra else []),
        vmem=VMEM_LIMIT)(qi_tab, kj_tab, q, k, v, o, do, lse, *([riding[1]] if extra else []))


def _attn_prep_bwd(pa, dq, dk, dv, cosf, sins, qn, kvn, wq, wqs, wk, wv, t):
    s = pa.shape[0]

    def body(pa_ref, dq_ref, dk_ref, dv_ref, cos_ref, sin_ref, qn_ref, kvn_ref, wq_ref, wqs_ref, wk_ref, wv_ref,
             dpa_ref, dwq_ref, dwqs_ref, dwk_ref, dwv_ref, dqn_ref, dkvn_ref):
        first = pl.program_id(0) == 0
        cosv, sinv = cos_ref[...], sin_ref[...]
        cq, ckv = pa_ref[:, 0:Q_RANK], pa_ref[:, Q_RANK:Q_RANK + KV_RANK]
        cqn = _rms(cq, qn_ref[...]).astype(BF16)
        ckvn = _rms(ckv, kvn_ref[...]).astype(BF16)
        dvv = dv_ref[...]
        _accumulate(dwv_ref, _dot(dvv, ckvn, TN), first)
        dk_rope = dk_ref[0]
        for h in range(1, N_HEADS):
            dk_rope = dk_rope + dk_ref[h]
        dqa = jnp.concatenate([(dq_ref[h] * cosv).astype(BF16) for h in range(N_HEADS)], axis=1)
        dqs = jnp.concatenate([(dq_ref[h] * sinv).astype(BF16) for h in range(N_HEADS)], axis=1)
        dkb = jnp.concatenate([dk_ref[h].astype(BF16) for h in range(N_HEADS)], axis=1)
        _accumulate(dwq_ref, _dot(dqa, cqn, TN), first)
        _accumulate(dwqs_ref, _dot(dqs, cqn, TN), first)
        _accumulate(dwk_ref, _dot(dkb, ckvn, TN), first)
        dcqn = _dot(dqa, wq_ref[...]) + _dot(dqs, wqs_ref[...])
        dckvn = _dot(dvv, wv_ref[...]) + _dot(dkb, wk_ref[...])
        dcq, dqn = _rms_bwd(cq, qn_ref[...], dcqn)
        dckv, dkvn = _rms_bwd(ckv, kvn_ref[...], dckvn)
        _accumulate(dqn_ref, dqn, first)
        _accumulate(dkvn_ref, dkvn, first)
        dpa_ref[:, 0:Q_RANK] = dcq.astype(BF16)
        dpa_ref[:, Q_RANK:Q_RANK + KV_RANK] = dckv.astype(BF16)
        dpa_ref[:, 384:512] = (dk_rope * cosv).astype(BF16)
        dpa_ref[:, 512:640] = (dk_rope * sinv).astype(BF16)

    head_blk = pl.BlockSpec((N_HEADS, t, HEAD_PAD), lambda i: (0, i, 0))
    return _pcall(
        body, name="attn_prep_bwd", grid=(s // t,),
        in_specs=[_rows(t, PA_W), head_blk, head_blk, _rows(t, BR_W), _rows(t, LANES), _rows(t, LANES),
                  _whole(qn.shape), _whole(kvn.shape), _whole(wq.shape), _whole(wqs.shape), _whole(wk.shape), _whole(wv.shape)],
        out_specs=[_rows(t, PA_W), _acc(wq.shape), _acc(wqs.shape), _acc(wk.shape), _acc(wv.shape),
                   _acc(qn.shape), _acc(kvn.shape)],
        out_shape=[_sds((s, PA_W), BF16), _sds(wq.shape, F32), _sds(wqs.shape, F32), _sds(wk.shape, F32),
                   _sds(wv.shape, F32), _sds(qn.shape, F32), _sds(kvn.shape, F32)],
        vmem=VMEM_LIMIT)(pa, dq, dk, dv, cosf, sins, qn, kvn, wq, wqs, wk, wv)


def _inproj_bwd(x, g1, dxmid, dpa, dpm, dpres, w_in, wg, t):
    s = x.shape[0]
    n = s // t

    def body(x_ref, g1_ref, dxm_ref, dpa_ref, dpm_ref, d0_ref, d1_ref, d2_ref, d3_ref, w_ref, wg_ref,
             dx_ref, dg1_ref, dwa_ref, dwm_ref, acc_a, acc_m):
        first, last = pl.program_id(0) == 0, pl.program_id(0) == n - 1
        xv, g1v = x_ref[...], g1_ref[...]
        hb = _rms(xv, g1v).astype(BF16)
        dpav, dpmv = dpa_ref[...], dpm_ref[...]
        dh = _dot(dpav, w_ref[:PA_W, :]) + _dot(dpmv, w_ref[PA_W:, :])
        for k, d_ref in enumerate((d0_ref, d1_ref, d2_ref, d3_ref)):
            dh = dh + _dot(d_ref[...], wg_ref[k * D_MODEL:(k + 1) * D_MODEL, :])
        dx, dg1 = _rms_bwd(xv, g1v, dh)
        dx_ref[...] = dxm_ref[...] + dx
        _accumulate(dg1_ref, dg1, first)
        _accumulate_out(acc_a, dwa_ref, _dot(dpav, hb, TN), first, last)
        _accumulate_out(acc_m, dwm_ref, _dot(dpmv, hb, TN), first, last)

    return _pcall(
        body, name="inproj_bwd", grid=(n,),
        in_specs=[_rows(t, D_MODEL), _whole(g1.shape), _rows(t, D_MODEL), _rows(t, PA_W), _rows(t, PM_W)] +
                 [_rows(t, D_MODEL)] * 4 + [_whole(w_in.shape), _whole(wg.shape)],
        out_specs=[_rows(t, D_MODEL), _acc((1, D_MODEL)), _acc((PA_W, D_MODEL)), _acc((PM_W, D_MODEL))],
        out_shape=[_sds((s, D_MODEL), F32), _sds((1, D_MODEL), F32), _sds((PA_W, D_MODEL), BF16), _sds((PM_W, D_MODEL), BF16)],
        scratch=[pltpu.VMEM((PA_W, D_MODEL), F32), pltpu.VMEM((PM_W, D_MODEL), F32)],
        vmem=VMEM_LIMIT)(x, g1, dxmid, dpa, dpm, *dpres, w_in, wg)


def _my_place():
    return lax.axis_index("x"), lax.axis_index("y"), lax.axis_index("c")


def _flip(place, k):
    x, y, c = place
    return (1 - x if k & 4 else x, 1 - y if k & 2 else y, 1 - c if k & 1 else c)


def _rank(place):
    x, y, c = place
    return 4 * x + 2 * y + c


EXCHANGE_SEMS = (pltpu.SemaphoreType.DMA((7,)), pltpu.SemaphoreType.DMA((7,)), pltpu.SemaphoreType.DMA)
HBM_REF = pl.BlockSpec(memory_space=pl.ANY)


def _gather_plan(x_ref, out_ref, send_sems, recv_sems, local_sem):
    me = _my_place()
    sibling = _flip(me, 1)
    chips = (4, 2, 6)

    def copy(k, src_place, to, src=None):
        slot = out_ref.at[_rank(src_place)]
        return pltpu.make_async_remote_copy(
            src_ref=slot if src is None else src, dst_ref=slot, send_sem=send_sems.at[k], recv_sem=recv_sems.at[k],
            device_id=to, device_id_type=MESH_ID)

    mine = pltpu.make_async_copy(x_ref, out_ref.at[_rank(me)], local_sem)
    first = [copy(0, me, sibling, src=x_ref)] + [copy(1 + j, me, _flip(me, kc), src=x_ref) for j, kc in enumerate(chips)]
    passed = [copy(4 + j, _flip(me, kc), sibling) for j, kc in enumerate(chips)]

    def start():
        mine.start()
        for cp in first:
            cp.start()

    def forward():
        for j, kc in enumerate(chips):
            copy(1 + j, _flip(me, kc), me).wait_recv()
            passed[j].start()

    def finish():
        copy(0, sibling, me).wait_recv()
        for j, kc in enumerate(chips):
            copy(4 + j, _flip(sibling, kc), me).wait_recv()
        for cp in first + passed:
            cp.wait_send()
        mine.wait()

    return start, forward, finish


def _exchange_plan(x_ref, out_ref, send_sems, recv_sems, local_sem):
    me = _my_place()
    my_rank = _rank(me)
    mine = pltpu.make_async_copy(x_ref.at[my_rank], out_ref.at[my_rank], local_sem)
    sends = [pltpu.make_async_remote_copy(
        src_ref=x_ref.at[_rank(_flip(me, k))], dst_ref=out_ref.at[my_rank], send_sem=send_sems.at[k - 1],
        recv_sem=recv_sems.at[k - 1], device_id=_flip(me, k), device_id_type=MESH_ID) for k in range(1, N_DEV)]

    def start():
        mine.start()
        for cp in sends:
            cp.start()

    def forward():
        pass

    def finish():
        for k in range(1, N_DEV):
            slot = out_ref.at[_rank(_flip(me, k))]
            pltpu.make_async_remote_copy(
                src_ref=slot, dst_ref=slot, send_sem=send_sems.at[k - 1], recv_sem=recv_sems.at[k - 1],
                device_id=_flip(me, k), device_id_type=MESH_ID).wait_recv()
        for cp in sends:
            cp.wait_send()
        mine.wait()

    return start, forward, finish


def _exchange_alone(plan, src, out_sds, name):
    def body(x_ref, out_ref, send_sems, recv_sems, local_sem):
        start, forward, finish = plan(x_ref, out_ref, send_sems, recv_sems, local_sem)
        start()
        forward()
        finish()

    return pl.pallas_call(body, name=name, out_shape=out_sds, in_specs=[HBM_REF], out_specs=HBM_REF,
                          scratch_shapes=list(EXCHANGE_SEMS))(src)


def _all_gather(shard, name):
    return _exchange_alone(_gather_plan, shard, _sds((N_DEV,) + shard.shape, shard.dtype), name)


def _all_to_all(blocks, name):
    return _exchange_alone(_exchange_plan, blocks, _sds(blocks.shape, blocks.dtype), name)


def _riding(plan, src, out_sds):
    return (plan, src, out_sds)


def _ride_along(riding, step_id, n_steps, refs):
    start, forward, finish = riding[0](*refs)

    @pl.when(step_id == 0)
    def _():
        start()

    @pl.when(step_id == (3 * n_steps) // 4)
    def _():
        forward()

    def at_end():
        @pl.when(step_id == n_steps - 1)
        def _():
            finish()

    return at_end


def _sum_slots(parts, name):
    _, r, c = parts.shape
    t = math.gcd(r, ROW_PAD)

    def body(p_ref, o_ref):
        acc = p_ref[0].astype(F32)
        for d in range(1, N_DEV):
            acc = acc + p_ref[d].astype(F32)
        o_ref[...] = acc

    return _pcall(
        body, name=name, grid=(r // t,),
        in_specs=[pl.BlockSpec((N_DEV, t, c), lambda i: (0, i, 0))], out_specs=_rows(t, c),
        out_shape=_sds((r, c), F32))(parts)


def _adamw(w, g, m, v, name):
    r, c = w.shape
    t = r if r <= ROW_PAD else math.gcd(r, ROW_PAD)

    def body(w_ref, g_ref, m_ref, v_ref, d_ref, nm_ref, nv_ref):
        gv = g_ref[...]
        nm = ADAM_B1 * m_ref[...] + (1.0 - ADAM_B1) * gv
        nv = ADAM_B2 * v_ref[...] + (1.0 - ADAM_B2) * jnp.square(gv)
        m_hat = nm / (1.0 - ADAM_B1 ** ADAM_STEP)
        v_hat = nv / (1.0 - ADAM_B2 ** ADAM_STEP)
        d_ref[...] = -ADAM_LR * (m_hat / (jnp.sqrt(v_hat) + ADAM_EPS) + ADAM_WD * w_ref[...])
        nm_ref[...] = nm
        nv_ref[...] = nv

    return _pcall(
        body, name=name, grid=(r // t,),
        in_specs=[_rows(t, c)] * 4, out_specs=[_rows(t, c)] * 3, out_shape=[_sds((r, c), F32)] * 3)(w, g, m, v)


SHARDED = ("w_in", "w_uq", "w_ukv", "w_br_mla", "w_br_sg", "w_br_conv", "w_br_pool", "w_out", "w_ff1", "w_ff2", "conv_w")
ROW_SHARDED = ("w_out", "w_ff2")
REPLICATED = ("norm_mix_pre", "gate_b", "q_norm", "kv_norm", "sg_ln_g", "sg_ln_b", "sg_w", "sg_b", "pool_w",
              "pool_scale", "norm_mix_post", "norm_ffn_pre", "norm_ffn_post")
ROW_PAD = 256
GRAD_ROW_PAD = 64
PART_ROWS = 16


def _pack_rows(arrays, dtype, multiple, lead=()):
    rows, offsets, at = [], [], 0
    zero_pad = ((0, 0),) * len(lead)
    for a in arrays:
        flat = a.reshape(lead + (-1,)).astype(dtype)
        n = -(-flat.shape[-1] // D_MODEL)
        n_pad = -(-n // PART_ROWS) * PART_ROWS
        flat = jnp.pad(flat, zero_pad + ((0, n_pad * D_MODEL - flat.shape[-1]),))
        rows.append(flat.reshape(lead + (n_pad, D_MODEL)))
        offsets.append((at, n))
        at += n_pad
    pad = -at % multiple
    if pad:
        rows.append(jnp.zeros(lead + (pad, D_MODEL), dtype))
    return jnp.concatenate(rows, axis=len(lead)), offsets


def _unpack_rows(buf, offsets, shapes):
    lead = buf.shape[:-2]
    out = []
    for (at, n), shape in zip(offsets, shapes):
        size = math.prod(shape)
        flat = buf[..., at:at + n, :].reshape(lead + (n * D_MODEL,))[..., :size]
        out.append(flat.reshape(lead + tuple(shape)))
    return out


def _pack_shards(p, depth):
    arrays = []
    for n in SHARDED:
        a = p[n][0] if n in ROW_SHARDED else jnp.swapaxes(p[n][0], -1, -2)
        arrays.append(lax.bitcast_convert_type(a, BF16) if n == "conv_w" else a.astype(BF16))
    packed, offsets = _pack_rows(arrays, BF16, GRAD_ROW_PAD, lead=(depth,))
    return packed, offsets, [a.shape[1:] for a in arrays]


def _unpack_layer_weights(gathered, offsets, shapes):
    full = {}
    for n, gth in zip(SHARDED, _unpack_rows(gathered, offsets, shapes)):
        if n == "conv_w":
            gth = lax.bitcast_convert_type(gth, F32)
        full[n] = gth.reshape((-1,) + gth.shape[2:])
    full["conv_w"] = full["conv_w"].T
    return full


def _owner_major(grad_sharded_dim_first):
    a = grad_sharded_dim_first
    return a.reshape((N_DEV, a.shape[0] // N_DEV) + a.shape[1:])


def _owner_block_shape(name, shard_shape):
    k, n = shard_shape[-2:]
    return (k, n) if name in ROW_SHARDED else (n, k)


def _natural_shard(name, blocks):
    return blocks if name in ROW_SHARDED else jnp.swapaxes(blocks, -1, -2)


def _swap_halves(a, axis=-1):
    lo, hi = jnp.split(a, 2, axis=axis)
    return jnp.concatenate([hi, lo], axis=axis)


def _pad_rows(a, top, total):
    return jnp.pad(a, ((0, 0),) * (a.ndim - 2) + ((top, total - top - a.shape[-2]), (0, 0)))


def _layer_weights(full, small, l):
    w_in = full["w_in"]
    k_r = w_in[384:416]
    w_in_k = jnp.concatenate(
        [w_in[0:384], _pad_rows(k_r, D_NOPE, HEAD_PAD), _pad_rows(_swap_halves(k_r, axis=0), D_NOPE, HEAD_PAD), w_in[416:1952]], axis=0)
    w_uq = full["w_uq"].reshape(N_HEADS, D_NOPE + D_ROPE, Q_RANK)
    wq = _pad_rows(w_uq, 0, HEAD_PAD).reshape(N_HEADS * HEAD_PAD, Q_RANK)
    wqs = _pad_rows(_swap_halves(w_uq[:, D_NOPE:], axis=1), D_NOPE, HEAD_PAD).reshape(N_HEADS * HEAD_PAD, Q_RANK)
    w_ukv = full["w_ukv"].reshape(N_HEADS, D_NOPE + D_V, KV_RANK)
    wk = _pad_rows(w_ukv[:, :D_NOPE], 0, HEAD_PAD).reshape(N_HEADS * HEAD_PAD, KV_RANK)
    wv = w_ukv[:, D_NOPE:].reshape(N_HEADS * D_V, KV_RANK)
    tri = jnp.tril(jnp.ones((SG_CHUNK, SG_CHUNK), bool))
    wm = jnp.where(tri, small["sg_w"][l], 0.0)
    pool_w = small["pool_w"][l]
    wp = jnp.zeros((BR_W, BR_W), F32)
    for g in range(len(POOL_WINDOWS)):
        wp = wp.at[g * POOL_GROUP:(g + 1) * POOL_GROUP, g * POOL_GROUP:(g + 1) * POOL_GROUP].set(pool_w[g])
    vec = lambda name: small[name][l][None, :]
    return dict(
        w_in=w_in_k, wg=w_in[1952:], bg=vec("gate_b"), g1=vec("norm_mix_pre"), qn=vec("q_norm"), kvn=vec("kv_norm"),
        wq=wq, wqs=wqs, wk=wk, wv=wv,
        ln_g=vec("sg_ln_g"), ln_b=vec("sg_ln_b"), wm=wm.astype(BF16), wmt=jnp.swapaxes(wm, 1, 2).astype(BF16),
        sgb=jnp.repeat(small["sg_b"][l].T, POOL_GROUP, axis=1), conv_w=full["conv_w"],
        wp=wp.astype(BF16), pscale=vec("pool_scale"),
        wbr=jnp.stack([full["w_br_mla"], full["w_br_sg"], full["w_br_conv"], full["w_br_pool"]]),
        wout=full["w_out"], g2=vec("norm_mix_post"), g3=vec("norm_ffn_pre"), w1=full["w_ff1"], w2=full["w_ff2"],
        g4=vec("norm_ffn_post"))


def _layer_grads(g):
    dwa = g["dwa"]
    rope_rows = slice(D_NOPE, D_NOPE + D_ROPE)
    d_kr = dwa[384:512][rope_rows].astype(F32) + _swap_halves(dwa[512:640][rope_rows], axis=0).astype(F32)
    w_in = jnp.concatenate([dwa[0:384], d_kr.astype(BF16), g["dwm_in"]] + list(g["dwg"]), axis=0)
    dwq, dwqs = (g[n].reshape(N_HEADS, HEAD_PAD, Q_RANK) for n in ("dwq", "dwqs"))
    d_rope = dwq[:, rope_rows] + _swap_halves(dwqs[:, rope_rows], axis=1)
    w_uq = jnp.concatenate([dwq[:, :D_NOPE], d_rope], axis=1).reshape(N_HEADS * (D_NOPE + D_ROPE), Q_RANK)
    dwk = g["dwk"].reshape(N_HEADS, HEAD_PAD, KV_RANK)
    w_ukv = jnp.concatenate([dwk[:, :D_NOPE], g["dwv"].reshape(N_HEADS, D_V, KV_RANK)], axis=1)
    w_ukv = w_ukv.reshape(N_HEADS * (D_NOPE + D_V), KV_RANK)
    pool_w = jnp.stack([g["dwp"][i * POOL_GROUP:(i + 1) * POOL_GROUP, i * POOL_GROUP:(i + 1) * POOL_GROUP]
                        for i in range(len(POOL_WINDOWS))])
    sg_b = g["dsgb"].reshape(SG_CHUNK, SG_GROUPS, POOL_GROUP).sum(axis=-1).T
    own = _owner_major
    return dict(
        w_in=own(w_in), w_uq=own(w_uq), w_ukv=own(w_ukv), w_br_mla=own(g["dwbr"][0]), w_br_sg=own(g["dwbr"][1]),
        w_br_conv=own(g["dwbr"][2]), w_br_pool=own(g["dwbr"][3]), w_out=own(g["dwout"]), w_ff1=own(g["dw1"]),
        w_ff2=own(g["dw2"]), conv_w=own(g["dcw"][:CONV_K].T),
        norm_mix_pre=g["dg1"][0], gate_b=jnp.concatenate([b[0] for b in g["dbg"]]), q_norm=g["dqn"][0], kv_norm=g["dkvn"][0],
        sg_ln_g=g["dlng"][0], sg_ln_b=g["dlnb"][0], sg_w=g["dwm"], sg_b=sg_b, pool_w=pool_w, pool_scale=g["dps"][0],
        norm_mix_post=g["dg2"][0], norm_ffn_pre=g["dg3"][0], norm_ffn_post=g["dg4"][0])


def _rope_tables(positions):
    inv_freq = ROPE_BASE ** (-jnp.arange(0, D_ROPE, 2, dtype=F32) / D_ROPE)
    ang = positions.astype(F32)[:, None] * inv_freq
    cos, sin = jnp.cos(ang), jnp.sin(ang)
    s = positions.shape[0]
    cosf = jnp.concatenate([jnp.ones((s, D_NOPE), F32), cos, cos, jnp.zeros((s, HEAD_PAD - D_NOPE - D_ROPE), F32)], axis=1)
    sins = jnp.concatenate([jnp.zeros((s, D_NOPE), F32), -sin, sin, jnp.zeros((s, HEAD_PAD - D_NOPE - D_ROPE), F32)], axis=1)
    return cosf, sins


def _layer_fwd(x, w, cosf, sins, tiles, riding=None):
    t, ta = tiles["tok"], tiles["attn"]
    hb, pa, pm = _inproj_fwd(x, w["g1"], w["w_in"], tiles["wgrad"])
    q, k, v = _attn_prep_fwd(pa, cosf, sins, w["qn"], w["kvn"], w["wq"], w["wqs"], w["wk"], w["wv"], t)
    o, lse, *landed = _flash_fwd(q, k, v, ta, riding)
    bsg, bcv, bpl = _mixers_fwd(pm, w["ln_g"], w["ln_b"], w["wm"], w["sgb"], w["conv_w"], w["wp"], w["pscale"], t)
    x_mid, merged, tt = _merge_fwd(x, hb, (o, bsg, bcv, bpl), w["wg"], w["bg"], w["wbr"], w["wout"], w["g2"], t)
    x_out, f = _ffn_fwd(x_mid, w["g3"], w["w1"], w["w2"], w["g4"], t)
    saved = dict(x=x, hb=hb, pa=pa, pm=pm, q=q, k=k, v=v, o=o, lse=lse, bsg=bsg, bcv=bcv, bpl=bpl, x_mid=x_mid,
                 merged=merged, tt=tt, f=f)
    return x_out, saved, (landed[0] if landed else None)


def _layer_bwd(dxo, a, w, cosf, sins, tiles, riding=None):
    t, ta, tb, tw = tiles["tok"], tiles["attn"], tiles["inproj_bwd"], tiles["wgrad"]
    g = {}
    h2, df, g["dg4"] = _ffn_bwd_norms(a["x_mid"], a["f"], dxo, w["g3"], w["g4"], t)
    da, g["dw1"], g["dw2"] = _ffn_bwd_weights(h2, df, w["w1"], w["w2"], tw)
    dxmid, g["dg3"] = _ffn_bwd_input(da, w["w1"], a["x_mid"], dxo, w["g3"], t)
    dm, g["dwout"], g["dg2"] = _merge_bwd_out(a["tt"], dxmid, w["g2"], w["wout"], a["merged"], tw)
    dbrs, dpres, g["dwg"], g["dbg"], g["dwbr"] = [], [], [], [], []
    for k, br in enumerate((a["o"], a["bsg"], a["bcv"], a["bpl"])):
        dbr, dpre, dwg, dbg, dwbr = _merge_bwd_branch(k, a["hb"], br, dm, w["wg"], w["bg"], w["wbr"], tw)
        dbrs.append(dbr); dpres.append(dpre); g["dwg"].append(dwg); g["dbg"].append(dbg); g["dwbr"].append(dwbr)
    dpm, g["dlng"], g["dlnb"], g["dwm"], g["dsgb"], g["dcw"], g["dwp"], g["dps"] = _mixers_bwd(
        a["pm"], dbrs[1], dbrs[2], dbrs[3], w["ln_g"], w["ln_b"], w["wm"], w["wmt"], w["sgb"], w["conv_w"], w["wp"], w["pscale"], t)
    dq, dk, dv, *landed = _flash_bwd(a["q"], a["k"], a["v"], a["o"], dbrs[0], a["lse"], ta, riding)
    dpa, g["dwq"], g["dwqs"], g["dwk"], g["dwv"], g["dqn"], g["dkvn"] = _attn_prep_bwd(
        a["pa"], dq, dk, dv, cosf, sins, w["qn"], w["kvn"], w["wq"], w["wqs"], w["wk"], w["wv"], t)
    dx, g["dg1"], g["dwa"], g["dwm_in"] = _inproj_bwd(a["x"], w["g1"], dxmid, dpa, dpm, dpres, w["w_in"], w["wg"], tb)
    return dx, _layer_grads(g), (landed[0] if landed else None)


def _step(p, x, positions, loss_target):
    s = x.shape[0]
    depth = p["w_in"][0].shape[0]
    tiles = dict(tok=min(TOK_TILE, s), attn=min(ATTN_TILE, s), inproj_bwd=min(INPROJ_BWD_TILE, s),
                 wgrad=min(WGRAD_TILE, s))

    small = {n: p[n][0] for n in REPLICATED}
    cosf, sins = _rope_tables(positions)

    packed, w_offsets, w_shapes = _pack_shards(p, depth)
    gathered = _all_gather(packed[0], "weight_all_gather")
    weights, acts = [], []
    h = x
    for l in range(depth):
        weights.append(_layer_weights(_unpack_layer_weights(gathered, w_offsets, w_shapes), small, l))
        riding = None
        if l + 1 < depth:
            nxt = packed[l + 1]
            riding = _riding(_gather_plan, nxt, _sds((N_DEV,) + nxt.shape, nxt.dtype))
        h, saved, gathered = _layer_fwd(h, weights[l], cosf, sins, tiles, riding)
        acts.append(saved)
    dh, loss_blk = _loss_head(h, loss_target, tiles["tok"])
    loss = lax.psum(loss_blk[0, 0], ("x", "y", "c"))

    layer_grads = [None] * depth
    summed = [None] * depth
    send = None
    for l in reversed(range(depth)):
        riding = None if send is None else _riding(_exchange_plan, send, _sds(send.shape, send.dtype))
        dh, layer_grads[l], landed = _layer_bwd(dh, acts[l], weights[l], cosf, sins, tiles, riding)
        if landed is not None:
            summed[l + 1] = _sum_slots(landed, "grad_shard_sum")
        send, send_offsets = _pack_rows([layer_grads[l][n] for n in SHARDED], BF16, GRAD_ROW_PAD, lead=(N_DEV,))
    summed[0] = _sum_slots(_all_to_all(send, "grad_all_to_all"), "grad_shard_sum")
    block_shapes = [_owner_block_shape(n, p[n][0].shape) for n in SHARDED]
    blocks = _unpack_rows(jnp.stack(summed), send_offsets, block_shapes)
    g_shard = {n: _natural_shard(n, b) for n, b in zip(SHARDED, blocks)}
    small_grads = [jnp.stack([layer_grads[l][n] for l in range(depth)]) for n in REPLICATED]
    small_rows, small_offsets = _pack_rows(small_grads, F32, ROW_PAD)
    g_small_rows = _sum_slots(_all_gather(small_rows, "replicated_grad_all_gather"), "grad_replicated_sum")

    grad, delta, new_m, new_v = {}, {}, {}, {}
    for n in SHARDED:
        w, m, v = p[n]
        shape2 = (math.prod(w.shape[:-1]), w.shape[-1])
        grad[n] = g_shard[n]
        d, nm, nv = _adamw(w.reshape(shape2), grad[n].reshape(shape2), m.reshape(shape2), v.reshape(shape2), f"adamw_{n}")
        delta[n], new_m[n], new_v[n] = d.reshape(w.shape), nm.reshape(w.shape), nv.reshape(w.shape)
    packs = [_pack_rows([p[n][i] for n in REPLICATED], F32, ROW_PAD)[0] for i in range(3)]
    d, nm, nv = _adamw(packs[0], g_small_rows, packs[1], packs[2], "adamw_replicated")
    shapes = [p[n][0].shape for n in REPLICATED]
    for n, gg, dd, mm, vv in zip(REPLICATED, _unpack_rows(g_small_rows, small_offsets, shapes), _unpack_rows(d, small_offsets, shapes),
                                 _unpack_rows(nm, small_offsets, shapes), _unpack_rows(nv, small_offsets, shapes)):
        grad[n], delta[n], new_m[n], new_v[n] = gg, dd, mm, vv
    return loss, dh, grad, delta, new_m, new_v


WEIGHT_ORDER = ("norm_mix_pre", "w_in", "gate_b", "q_norm", "w_uq", "kv_norm", "w_ukv", "w_br_mla", "sg_ln_g", "sg_ln_b",
                "sg_w", "sg_b", "w_br_sg", "conv_w", "w_br_conv", "pool_w", "pool_scale", "w_br_pool", "w_out",
                "norm_mix_post", "norm_ffn_pre", "w_ff1", "w_ff2", "norm_ffn_post")


def kernel(x, positions, norm_mix_pre, w_in, gate_b, q_norm, w_uq, kv_norm, w_ukv, w_br_mla, sg_ln_g, sg_ln_b, sg_w, sg_b, w_br_sg, conv_w, w_br_conv, pool_w, pool_scale, w_br_pool, w_out, norm_mix_post, norm_ffn_pre, w_ff1, w_ff2, norm_ffn_post, loss_target, m_norm_mix_pre, m_w_in, m_gate_b, m_q_norm, m_w_uq, m_kv_norm, m_w_ukv, m_w_br_mla, m_sg_ln_g, m_sg_ln_b, m_sg_w, m_sg_b, m_w_br_sg, m_conv_w, m_w_br_conv, m_pool_w, m_pool_scale, m_w_br_pool, m_w_out, m_norm_mix_post, m_norm_ffn_pre, m_w_ff1, m_w_ff2, m_norm_ffn_post, v_norm_mix_pre, v_w_in, v_gate_b, v_q_norm, v_w_uq, v_kv_norm, v_w_ukv, v_w_br_mla, v_sg_ln_g, v_sg_ln_b, v_sg_w, v_sg_b, v_w_br_sg, v_conv_w, v_w_br_conv, v_pool_w, v_pool_scale, v_w_br_pool, v_w_out, v_norm_mix_post, v_norm_ffn_pre, v_w_ff1, v_w_ff2, v_norm_ffn_post):
    ws = (norm_mix_pre, w_in, gate_b, q_norm, w_uq, kv_norm, w_ukv, w_br_mla, sg_ln_g, sg_ln_b, sg_w, sg_b, w_br_sg, conv_w,
          w_br_conv, pool_w, pool_scale, w_br_pool, w_out, norm_mix_post, norm_ffn_pre, w_ff1, w_ff2, norm_ffn_post)
    ms = (m_norm_mix_pre, m_w_in, m_gate_b, m_q_norm, m_w_uq, m_kv_norm, m_w_ukv, m_w_br_mla, m_sg_ln_g, m_sg_ln_b, m_sg_w,
          m_sg_b, m_w_br_sg, m_conv_w, m_w_br_conv, m_pool_w, m_pool_scale, m_w_br_pool, m_w_out, m_norm_mix_post,
          m_norm_ffn_pre, m_w_ff1, m_w_ff2, m_norm_ffn_post)
    vs = (v_norm_mix_pre, v_w_in, v_gate_b, v_q_norm, v_w_uq, v_kv_norm, v_w_ukv, v_w_br_mla, v_sg_ln_g, v_sg_ln_b, v_sg_w,
          v_sg_b, v_w_br_sg, v_conv_w, v_w_br_conv, v_pool_w, v_pool_scale, v_w_br_pool, v_w_out, v_norm_mix_post,
          v_norm_ffn_pre, v_w_ff1, v_w_ff2, v_norm_ffn_post)
    p = {n: (w, m, v) for n, w, m, v in zip(WEIGHT_ORDER, ws, ms, vs)}
    loss, grad_x, grad, delta, new_m, new_v = _step(p, x[0], positions[0], loss_target[0])
    return (loss, grad_x[None], *[grad[n] for n in WEIGHT_ORDER], *[delta[n] for n in WEIGHT_ORDER],
            *[new_m[n] for n in WEIGHT_ORDER], *[new_v[n] for n in WEIGHT_ORDER])
```

```python
import functools
import math

import jax
import jax.numpy as jnp
from jax import lax
from jax.experimental import pallas as pl
from jax.experimental.pallas import tpu as pltpu

F32 = jnp.float32
BF16 = jnp.bfloat16

D_MODEL = 1024
N_HEADS = 4
D_NOPE = 64
D_ROPE = 32
D_V = 64
Q_RANK = 256
KV_RANK = 128
BR_W = 256
SG_CHUNK = 128
SG_GROUPS = 4
POOL_WINDOWS = (2, 4, 8, 16)
POOL_GROUP = 64
CONV_K = 3
D_FF = 4096
N_BRANCH = 4
N_IN = 6048
EPS = 1e-6
ROPE_BASE = 10000.0
ADAM_LR = 0.001
ADAM_B1 = 0.9
ADAM_B2 = 0.999
ADAM_EPS = 1e-08
ADAM_WD = 0.01
ADAM_STEP = 10

N_DEV = 8
LANES = 128
HEAD_PAD = 128
HALO = 16
PA_W = 640
PM_W = 6 * BR_W
VMEM_LIMIT = 56 * 1024 * 1024
SM_SCALE = (D_NOPE + D_ROPE) ** -0.5
TOK_TILE = 512
ATTN_TILE = 512
WGRAD_TILE = 1024
INPROJ_BWD_TILE = 512

NN = (((1,), (0,)), ((), ()))
NT = (((1,), (1,)), ((), ()))
TN = (((0,), (0,)), ((), ()))
MESH_ID = pl.DeviceIdType.MESH


def _dot(a, b, dims=NN):
    return lax.dot_general(a, b, dims, preferred_element_type=F32)


def _pcall(body, *, name, grid, in_specs, out_specs, out_shape, scratch=(), vmem=None, prefetch=0):
    params = pltpu.CompilerParams(vmem_limit_bytes=vmem)
    if prefetch:
        spec = pltpu.PrefetchScalarGridSpec(num_scalar_prefetch=prefetch, grid=grid, in_specs=in_specs,
                                            out_specs=out_specs, scratch_shapes=scratch)
        return pl.pallas_call(body, name=name, grid_spec=spec, out_shape=out_shape, compiler_params=params)
    return pl.pallas_call(
        body, name=name, grid=grid, in_specs=in_specs, out_specs=out_specs, out_shape=out_shape,
        scratch_shapes=scratch, compiler_params=params)


def _rows(t, width):
    return pl.BlockSpec((t, width), lambda i: (i, 0))


def _whole(shape):
    nd = len(shape)
    return pl.BlockSpec(tuple(shape), lambda *_: (0,) * nd, pipeline_mode=pl.Buffered(1))


def _acc(shape):
    nd = len(shape)
    return pl.BlockSpec(tuple(shape), lambda *_: (0,) * nd)


def _sds(shape, dtype):
    return jax.ShapeDtypeStruct(tuple(shape), dtype)


def _rms(x, g):
    return x * lax.rsqrt(jnp.mean(x * x, axis=-1, keepdims=True) + EPS) * g


def _rms_bwd(x, g, dy):
    r = lax.rsqrt(jnp.mean(x * x, axis=-1, keepdims=True) + EPS)
    xh = x * r
    dg = jnp.sum(dy * xh, axis=0, keepdims=True)
    dxh = dy * g
    dx = r * (dxh - xh * jnp.mean(dxh * xh, axis=-1, keepdims=True))
    return dx, dg


def _sigmoid(x):
    return 1.0 / (1.0 + jnp.exp(-x))


def _gelu(x):
    return jax.nn.gelu(x, approximate=True)


def _accumulate(ref, val, first):
    @pl.when(first)
    def _():
        ref[...] = val

    @pl.when(jnp.logical_not(first))
    def _():
        ref[...] += val


def _accumulate_out(acc_ref, out_ref, val, first, last):
    _accumulate(acc_ref, val, first)

    @pl.when(last)
    def _():
        out_ref[...] = acc_ref[...].astype(out_ref.dtype)


def _inproj_fwd(x, g1, w_in, t):
    s = x.shape[0]

    def body(x_ref, g_ref, w_ref, hb_ref, pa_ref, pm_ref):
        hb = _rms(x_ref[...], g_ref[...]).astype(BF16)
        hb_ref[...] = hb
        pa_ref[...] = _dot(hb, w_ref[:PA_W, :], NT)
        pm_ref[...] = _dot(hb, w_ref[PA_W:, :], NT)

    return _pcall(
        body, name="inproj_fwd", grid=(s // t,),
        in_specs=[_rows(t, D_MODEL), _whole((1, D_MODEL)), _whole(w_in.shape)],
        out_specs=[_rows(t, D_MODEL), _rows(t, PA_W), _rows(t, PM_W)],
        out_shape=[_sds((s, D_MODEL), BF16), _sds((s, PA_W), F32), _sds((s, PM_W), F32)],
        vmem=VMEM_LIMIT)(x, g1, w_in)


def _attn_prep_fwd(pa, cosf, sins, qn, kvn, wq, wqs, wk, wv, t):
    s = pa.shape[0]

    def body(pa_ref, cos_ref, sin_ref, qn_ref, kvn_ref, wq_ref, wqs_ref, wk_ref, wv_ref, q_ref, k_ref, v_ref):
        cosv, sinv = cos_ref[...], sin_ref[...]
        cqn = _rms(pa_ref[:, 0:Q_RANK], qn_ref[...]).astype(BF16)
        ckvn = _rms(pa_ref[:, Q_RANK:Q_RANK + KV_RANK], kvn_ref[...]).astype(BF16)
        k_rope = pa_ref[:, 384:512] * cosv + pa_ref[:, 512:640] * sinv
        q_all, qs_all = _dot(cqn, wq_ref[...], NT), _dot(cqn, wqs_ref[...], NT)
        k_all = _dot(ckvn, wk_ref[...], NT)
        for h in range(N_HEADS):
            lanes = slice(h * HEAD_PAD, (h + 1) * HEAD_PAD)
            q_ref[h] = (q_all[:, lanes] * cosv + qs_all[:, lanes] * sinv).astype(BF16)
            k_ref[h] = (k_all[:, lanes] + k_rope).astype(BF16)
        v_ref[...] = _dot(ckvn, wv_ref[...], NT).astype(BF16)

    head_blk = pl.BlockSpec((N_HEADS, t, HEAD_PAD), lambda i: (0, i, 0))
    return _pcall(
        body, name="attn_prep_fwd", grid=(s // t,),
        in_specs=[_rows(t, PA_W), _rows(t, LANES), _rows(t, LANES), _whole(qn.shape), _whole(kvn.shape),
                  _whole(wq.shape), _whole(wqs.shape), _whole(wk.shape), _whole(wv.shape)],
        out_specs=[head_blk, head_blk, _rows(t, BR_W)],
        out_shape=[_sds((N_HEADS, s, HEAD_PAD), BF16), _sds((N_HEADS, s, HEAD_PAD), BF16), _sds((s, BR_W), BF16)],
        vmem=VMEM_LIMIT)(pa, cosf, sins, qn, kvn, wq, wqs, wk, wv)


def _causal_scores(q, k, masked):
    sc = _dot(q, k, NT) * SM_SCALE
    if masked:
        row = lax.broadcasted_iota(jnp.int32, sc.shape, 0)
        col = lax.broadcasted_iota(jnp.int32, sc.shape, 1)
        sc = jnp.where(col <= row, sc, -jnp.inf)
    return sc


def _causal_steps(n, key_major):
    if key_major:
        pairs = [(qi, kj) for kj in range(n) for qi in range(kj, n)]
    else:
        pairs = [(qi, kj) for qi in range(n) for kj in range(qi + 1)]
    return (jnp.asarray([p[0] for p in pairs], jnp.int32), jnp.asarray([p[1] for p in pairs], jnp.int32))


def _flash_fwd(q, k, v, t, riding):
    s = v.shape[0]
    n = s // t
    reps = t // LANES
    qi_tab, kj_tab = _causal_steps(n, key_major=False)
    n_steps = int(qi_tab.shape[0])

    def body(qi_ref, kj_ref, q_ref, k_ref, v_ref, x_ref, o_ref, lse_ref, land_ref, m_scr, l_scr, acc_scr,
             send_sems, recv_sems, local_sem):
        step_id = pl.program_id(0)
        qi, kj = qi_ref[step_id], kj_ref[step_id]
        at_end = _ride_along(riding, step_id, n_steps, (x_ref, land_ref, send_sems, recv_sems, local_sem))

        @pl.when(kj == 0)
        def _():
            m_scr[...] = jnp.full(m_scr.shape, -jnp.inf, F32)
            l_scr[...] = jnp.zeros(l_scr.shape, F32)
            acc_scr[...] = jnp.zeros(acc_scr.shape, F32)

        def step(masked):
            for h in range(N_HEADS):
                sc = _causal_scores(q_ref[h], k_ref[h], masked)
                m_prev = m_scr[h]
                m_next = jnp.maximum(m_prev, jnp.max(sc, axis=1, keepdims=True))
                alpha = jnp.exp(m_prev - m_next)
                p = jnp.exp(sc - jnp.tile(m_next, (1, reps)))
                l_scr[h] = alpha * l_scr[h] + jnp.sum(p, axis=1, keepdims=True)
                m_scr[h] = m_next
                pair = (h // 2) * LANES
                acc_scr[h] = acc_scr[h] * alpha + _dot(p.astype(BF16), v_ref[:, pair:pair + LANES])

        @pl.when(kj < qi)
        def _():
            step(False)

        @pl.when(kj == qi)
        def _():
            step(True)
            lane = lax.broadcasted_iota(jnp.int32, (t, LANES), 1)
            for pr in range(N_HEADS // 2):
                o0 = acc_scr[2 * pr] / l_scr[2 * pr]
                o1 = acc_scr[2 * pr + 1] / l_scr[2 * pr + 1]
                o_ref[:, pr * LANES:(pr + 1) * LANES] = jnp.where(lane < D_V, o0, o1).astype(BF16)
            for h in range(N_HEADS):
                lse_ref[h] = m_scr[h] + jnp.log(l_scr[h])

        at_end()

    q_blk = pl.BlockSpec((N_HEADS, t, HEAD_PAD), lambda i, qi, kj: (0, qi[i], 0))
    k_blk = pl.BlockSpec((N_HEADS, t, HEAD_PAD), lambda i, qi, kj: (0, kj[i], 0))
    v_blk = pl.BlockSpec((t, BR_W), lambda i, qi, kj: (kj[i], 0))
    return _pcall(
        body, name="flash_fwd_gathering", grid=(n_steps,), prefetch=2,
        in_specs=[q_blk, k_blk, v_blk, HBM_REF],
        out_specs=[pl.BlockSpec((t, BR_W), lambda i, qi, kj: (qi[i], 0)), q_blk, HBM_REF],
        out_shape=[_sds((s, BR_W), BF16), _sds((N_HEADS, s, LANES), F32), riding[2]],
        scratch=[pltpu.VMEM((N_HEADS, t, LANES), F32)] * 3 + list(EXCHANGE_SEMS),
        vmem=VMEM_LIMIT)(qi_tab, kj_tab, q, k, v, riding[1])


def _shift_down(ext, k, t):
    return pltpu.roll(ext, k, 0)[HALO:HALO + t]


def _shift_up(ext, k, t):
    return pltpu.roll(ext, t + HALO - k, 0)[0:t]


def _lane_group(shape):
    return lax.shift_right_logical(lax.broadcasted_iota(jnp.int32, shape, 1), 6)


def _group_select(vals):
    grp = _lane_group(vals[0].shape)
    out = vals[0]
    for g in range(1, len(vals)):
        out = jnp.where(grp == g, vals[g], out)
    return out


def _layernorm(x, g, b):
    mu = jnp.mean(x, axis=-1, keepdims=True)
    xc = x - mu
    return xc * lax.rsqrt(jnp.mean(xc * xc, axis=-1, keepdims=True) + EPS) * g + b


def _sg_mix(wm_ref, vnb, bias):
    return _group_select([_dot(wm_ref[g], vnb) for g in range(SG_GROUPS)]) + bias


def _pool_windows(ext):
    s2 = ext + pltpu.roll(ext, 1, 0)
    s4 = s2 + pltpu.roll(s2, 2, 0)
    s8 = s4 + pltpu.roll(s4, 4, 0)
    s16 = s8 + pltpu.roll(s8, 8, 0)
    return [s2, s4, s8, s16]


def _pool_counts(tok):
    return [jnp.minimum(tok + 1, w).astype(F32) for w in POOL_WINDOWS]


def _halo_before(t):
    return pl.BlockSpec((HALO, PM_W), lambda i: (jnp.maximum(i * (t // HALO) - 1, 0), 0))


def _mixers_fwd(pm, ln_g, ln_b, wm, sgb, conv_w, wp, pscale, t):
    s = pm.shape[0]

    def body(pm_ref, halo_ref, lng_ref, lnb_ref, wm_ref, sgb_ref, cw_ref, wp_ref, ps_ref, bsg_ref, bcv_ref, bpl_ref):
        i = pl.program_id(0)
        halo = jnp.where(i > 0, halo_ref[...], 0.0)
        u = _gelu(pm_ref[:, 0:256])
        vnb = _layernorm(_gelu(pm_ref[:, 256:512]), lng_ref[...], lnb_ref[...]).astype(BF16)
        for c in range(t // SG_CHUNK):
            rows = slice(c * SG_CHUNK, (c + 1) * SG_CHUNK)
            bsg_ref[rows, :] = (u[rows] * _sg_mix(wm_ref, vnb[rows], sgb_ref[...])).astype(BF16)
        z = pm_ref[:, 1024:1280] * pm_ref[:, 512:768]
        zext = jnp.concatenate([halo[:, 1024:1280] * halo[:, 512:768], z], axis=0)
        y = cw_ref[0:1, :] * _shift_down(zext, 2, t) + cw_ref[1:2, :] * _shift_down(zext, 1, t) + cw_ref[2:3, :] * z
        bcv_ref[...] = (pm_ref[:, 768:1024] * y).astype(BF16)
        p = pm_ref[:, 1280:1536]
        sums = _pool_windows(jnp.concatenate([halo[:, 1280:1536], p], axis=0))
        tok = i * t + lax.broadcasted_iota(jnp.int32, (t, 1), 0)
        pooled = _group_select([sw[HALO:HALO + t] / cnt - p for sw, cnt in zip(sums, _pool_counts(tok))])
        bpl_ref[...] = (_dot(pooled.astype(BF16), wp_ref[...]) * ps_ref[...]).astype(BF16)

    return _pcall(
        body, name="mixers_fwd", grid=(s // t,),
        in_specs=[_rows(t, PM_W), _halo_before(t), _whole(ln_g.shape), _whole(ln_b.shape), _whole(wm.shape),
                  _whole(sgb.shape), _whole(conv_w.shape), _whole(wp.shape), _whole(pscale.shape)],
        out_specs=[_rows(t, BR_W)] * 3,
        out_shape=[_sds((s, BR_W), BF16)] * 3,
        vmem=VMEM_LIMIT)(pm, pm, ln_g, ln_b, wm, sgb, conv_w, wp, pscale)


def _merge_fwd(x, hb, branches, wg, bg, wbr, wout, g2, t):
    s = x.shape[0]

    def body(x_ref, hb_ref, b0_ref, b1_ref, b2_ref, b3_ref, wg_ref, bg_ref, wbr_ref, wout_ref, g2_ref,
             xmid_ref, mrg_ref, t_ref):
        hb = hb_ref[...]
        merged = jnp.zeros((t, D_MODEL), F32)
        for k, b_ref in enumerate((b0_ref, b1_ref, b2_ref, b3_ref)):
            cols = slice(k * D_MODEL, (k + 1) * D_MODEL)
            gate = _sigmoid(_dot(hb, wg_ref[cols, :], NT) + bg_ref[:, cols])
            merged = merged + gate * _dot(b_ref[...], wbr_ref[k], NT)
        mb = merged.astype(BF16)
        mrg_ref[...] = mb
        tt = _dot(mb, wout_ref[...])
        t_ref[...] = tt
        xmid_ref[...] = x_ref[...] + _rms(tt, g2_ref[...])

    return _pcall(
        body, name="merge_fwd", grid=(s // t,),
        in_specs=[_rows(t, D_MODEL), _rows(t, D_MODEL)] + [_rows(t, BR_W)] * 4 +
                 [_whole(wg.shape), _whole(bg.shape), _whole(wbr.shape), _whole(wout.shape), _whole(g2.shape)],
        out_specs=[_rows(t, D_MODEL)] * 3,
        out_shape=[_sds((s, D_MODEL), F32), _sds((s, D_MODEL), BF16), _sds((s, D_MODEL), F32)],
        vmem=VMEM_LIMIT)(x, hb, *branches, wg, bg, wbr, wout, g2)


def _ffn_fwd(x, g3, w1, w2, g4, t):
    s = x.shape[0]

    def body(x_ref, g3_ref, w1_ref, w2_ref, g4_ref, xo_ref, f_ref):
        h = _rms(x_ref[...], g3_ref[...]).astype(BF16)
        f = jnp.zeros((t, D_MODEL), F32)
        for j in range(D_FF // D_MODEL):
            cols = slice(j * D_MODEL, (j + 1) * D_MODEL)
            r = jnp.square(jnp.maximum(_dot(h, w1_ref[cols, :], NT), 0.0)).astype(BF16)
            f = f + _dot(r, w2_ref[cols, :])
        f_ref[...] = f
        xo_ref[...] = x_ref[...] + _rms(f, g4_ref[...])

    return _pcall(
        body, name="ffn_fwd", grid=(s // t,),
        in_specs=[_rows(t, D_MODEL), _whole(g3.shape), _whole(w1.shape), _whole(w2.shape), _whole(g4.shape)],
        out_specs=[_rows(t, D_MODEL)] * 2,
        out_shape=[_sds((s, D_MODEL), F32)] * 2,
        vmem=VMEM_LIMIT)(x, g3, w1, w2, g4)


def _loss_head(y, target, t):
    s = y.shape[0]
    n = s // t

    def body(y_ref, tg_ref, dy_ref, loss_ref, acc_scr):
        i = pl.program_id(0)
        d = y_ref[...] - tg_ref[...]
        dy_ref[...] = d * (1.0 / D_MODEL)
        _accumulate(acc_scr, jnp.sum(d * d, axis=0, keepdims=True), i == 0)

        @pl.when(i == n - 1)
        def _():
            loss_ref[...] = jnp.full(loss_ref.shape, 0.5 / D_MODEL, F32) * jnp.sum(acc_scr[...])

    return _pcall(
        body, name="loss_head", grid=(n,),
        in_specs=[_rows(t, D_MODEL)] * 2,
        out_specs=[_rows(t, D_MODEL), _acc((8, LANES))],
        out_shape=[_sds((s, D_MODEL), F32), _sds((8, LANES), F32)],
        scratch=[pltpu.VMEM((1, D_MODEL), F32)])(y, target)


def _ffn_bwd_norms(x_mid, f, dxo, g3, g4, t):
    s = x_mid.shape[0]

    def body(x_ref, f_ref, dxo_ref, g3_ref, g4_ref, h2_ref, df_ref, dg4_ref):
        h2_ref[...] = _rms(x_ref[...], g3_ref[...]).astype(BF16)
        df, dg4 = _rms_bwd(f_ref[...], g4_ref[...], dxo_ref[...])
        df_ref[...] = df.astype(BF16)
        _accumulate(dg4_ref, dg4, pl.program_id(0) == 0)

    return _pcall(
        body, name="ffn_bwd_norms", grid=(s // t,),
        in_specs=[_rows(t, D_MODEL)] * 3 + [_whole(g3.shape), _whole(g4.shape)],
        out_specs=[_rows(t, D_MODEL), _rows(t, D_MODEL), _acc((1, D_MODEL))],
        out_shape=[_sds((s, D_MODEL), BF16), _sds((s, D_MODEL), BF16), _sds((1, D_MODEL), F32)])(x_mid, f, dxo, g3, g4)


def _ffn_bwd_weights(h2, df, w1, w2, t):
    s = h2.shape[0]
    blk = D_MODEL
    n = s // t

    def body(h2_ref, df_ref, w1_ref, w2_ref, da_ref, dw1_ref, dw2_ref, acc1, acc2):
        first, last = pl.program_id(1) == 0, pl.program_id(1) == n - 1
        h2v, dfv = h2_ref[...], df_ref[...]
        rl = jnp.maximum(_dot(h2v, w1_ref[...], NT), 0.0)
        _accumulate_out(acc2, dw2_ref, _dot((rl * rl).astype(BF16), dfv, TN), first, last)
        da = (_dot(dfv, w2_ref[...], NT) * (2.0 * rl)).astype(BF16)
        da_ref[...] = da
        _accumulate_out(acc1, dw1_ref, _dot(da, h2v, TN), first, last)

    tok = pl.BlockSpec((t, D_MODEL), lambda j, i: (i, 0))
    hid = pl.BlockSpec((blk, D_MODEL), lambda j, i: (j, 0))
    return _pcall(
        body, name="ffn_bwd_weights", grid=(D_FF // blk, n),
        in_specs=[tok, tok, hid, hid],
        out_specs=[pl.BlockSpec((t, blk), lambda j, i: (i, j)), hid, hid],
        out_shape=[_sds((s, D_FF), BF16), _sds((D_FF, D_MODEL), BF16), _sds((D_FF, D_MODEL), BF16)],
        scratch=[pltpu.VMEM((blk, D_MODEL), F32)] * 2,
        vmem=VMEM_LIMIT)(h2, df, w1, w2)


def _ffn_bwd_input(da, w1, x_mid, dxo, g3, t):
    s = x_mid.shape[0]

    def body(da_ref, w1_ref, x_ref, dxo_ref, g3_ref, dx_ref, dg3_ref):
        dx, dg3 = _rms_bwd(x_ref[...], g3_ref[...], _dot(da_ref[...], w1_ref[...]))
        dx_ref[...] = dxo_ref[...] + dx
        _accumulate(dg3_ref, dg3, pl.program_id(0) == 0)

    return _pcall(
        body, name="ffn_bwd_input", grid=(s // t,),
        in_specs=[_rows(t, D_FF), _whole(w1.shape), _rows(t, D_MODEL), _rows(t, D_MODEL), _whole(g3.shape)],
        out_specs=[_rows(t, D_MODEL), _acc((1, D_MODEL))],
        out_shape=[_sds((s, D_MODEL), F32), _sds((1, D_MODEL), F32)],
        vmem=VMEM_LIMIT)(da, w1, x_mid, dxo, g3)


def _merge_bwd_out(tt, dxmid, g2, wout, merged, t):
    s = tt.shape[0]
    n = s // t

    def body(t_ref, dx_ref, g2_ref, wout_ref, mrg_ref, dm_ref, dwout_ref, dg2_ref, acc):
        first, last = pl.program_id(0) == 0, pl.program_id(0) == n - 1
        dt, dg2 = _rms_bwd(t_ref[...], g2_ref[...], dx_ref[...])
        dtb = dt.astype(BF16)
        dm_ref[...] = _dot(dtb, wout_ref[...], NT)
        _accumulate_out(acc, dwout_ref, _dot(mrg_ref[...], dtb, TN), first, last)
        _accumulate(dg2_ref, dg2, first)

    return _pcall(
        body, name="merge_bwd_out", grid=(n,),
        in_specs=[_rows(t, D_MODEL), _rows(t, D_MODEL), _whole(g2.shape), _whole(wout.shape), _rows(t, D_MODEL)],
        out_specs=[_rows(t, D_MODEL), _acc((D_MODEL, D_MODEL)), _acc((1, D_MODEL))],
        out_shape=[_sds((s, D_MODEL), F32), _sds((D_MODEL, D_MODEL), BF16), _sds((1, D_MODEL), F32)],
        scratch=[pltpu.VMEM((D_MODEL, D_MODEL), F32)],
        vmem=VMEM_LIMIT)(tt, dxmid, g2, wout, merged)


def _merge_bwd_branch(k, hb, br, dm, wg, bg, wbr, t):
    s = hb.shape[0]
    n = s // t

    def body(hb_ref, br_ref, dm_ref, wg_ref, bg_ref, wbr_ref, dbr_ref, dpre_ref, dwg_ref, dbg_ref, dwbr_ref,
             acc_g, acc_br):
        first, last = pl.program_id(0) == 0, pl.program_id(0) == n - 1
        hbv, brv, dmv, wbrv = hb_ref[...], br_ref[...], dm_ref[...], wbr_ref[0]
        gate = _sigmoid(_dot(hbv, wg_ref[...], NT) + bg_ref[...])
        dy = (dmv * gate).astype(BF16)
        dpre = dmv * _dot(brv, wbrv, NT) * gate * (1.0 - gate)
        dpb = dpre.astype(BF16)
        dpre_ref[...] = dpb
        dbr_ref[...] = _dot(dy, wbrv)
        _accumulate_out(acc_br, dwbr_ref, _dot(dy, brv, TN), first, last)
        _accumulate_out(acc_g, dwg_ref, _dot(dpb, hbv, TN), first, last)
        _accumulate(dbg_ref, jnp.sum(dpre, axis=0, keepdims=True), first)

    return _pcall(
        body, name=f"merge_bwd_branch{k}", grid=(n,),
        in_specs=[_rows(t, D_MODEL), _rows(t, BR_W), _rows(t, D_MODEL),
                  pl.BlockSpec((D_MODEL, D_MODEL), lambda i: (k, 0)), pl.BlockSpec((1, D_MODEL), lambda i: (0, k)),
                  pl.BlockSpec((1, D_MODEL, BR_W), lambda i: (k, 0, 0))],
        out_specs=[_rows(t, BR_W), _rows(t, D_MODEL), _acc((D_MODEL, D_MODEL)), _acc((1, D_MODEL)), _acc((D_MODEL, BR_W))],
        out_shape=[_sds((s, BR_W), F32), _sds((s, D_MODEL), BF16), _sds((D_MODEL, D_MODEL), BF16),
                   _sds((1, D_MODEL), F32), _sds((D_MODEL, BR_W), BF16)],
        scratch=[pltpu.VMEM((D_MODEL, D_MODEL), F32), pltpu.VMEM((D_MODEL, BR_W), F32)],
        vmem=VMEM_LIMIT)(hb, br, dm, wg, bg, wbr)


def _mixers_bwd(pm, dbsg, dbcv, dbpl, ln_g, ln_b, wm, wmt, sgb, conv_w, wp, pscale, t):
    s = pm.shape[0]
    n = s // t
    nb = t // HALO

    def body(pm_ref, before_ref, after_ref, dsg_ref, dcv_ref, dcva_ref, dpl_ref, dpla_ref, lng_ref, lnb_ref,
             wm_ref, wmt_ref, sgb_ref, cw_ref, wp_ref, ps_ref,
             dpm_ref, dlng_ref, dlnb_ref, dwm_ref, dsgb_ref, dcw_ref, dwp_ref, dps_ref):
        i = pl.program_id(0)
        first = i == 0
        before = jnp.where(i > 0, before_ref[...], 0.0)
        after = jnp.where(i < n - 1, after_ref[...], 0.0)

        u_raw, v_raw = pm_ref[:, 0:256], pm_ref[:, 256:512]
        lng, lnb = lng_ref[...], lnb_ref[...]
        u, gelu_u_vjp = jax.vjp(_gelu, u_raw)
        vn, norm_vjp = jax.vjp(lambda v_, g_, b_: _layernorm(_gelu(v_), g_, b_), v_raw, lng, lnb)
        vnb = vn.astype(BF16)
        dsg = dsg_ref[...]
        grp = _lane_group((SG_CHUNK, BR_W))
        tri = (lax.broadcasted_iota(jnp.int32, (SG_CHUNK, SG_CHUNK), 1)
               <= lax.broadcasted_iota(jnp.int32, (SG_CHUNK, SG_CHUNK), 0))
        du_parts, dvn_parts = [], []
        dsgb = jnp.zeros((SG_CHUNK, BR_W), F32)
        dwm = [jnp.zeros((SG_CHUNK, SG_CHUNK), F32) for _ in range(SG_GROUPS)]
        for c in range(t // SG_CHUNK):
            rows = slice(c * SG_CHUNK, (c + 1) * SG_CHUNK)
            mix = _sg_mix(wm_ref, vnb[rows], sgb_ref[...])
            du_parts.append(dsg[rows] * mix)
            ds = dsg[rows] * u[rows]
            dsgb = dsgb + ds
            dsb = [jnp.where(grp == g, ds, 0.0).astype(BF16) for g in range(SG_GROUPS)]
            for g in range(SG_GROUPS):
                dwm[g] = dwm[g] + _dot(dsb[g], vnb[rows], NT)
            dvn_parts.append(_group_select([_dot(wmt_ref[g], dsb[g]) for g in range(SG_GROUPS)]))
        (du_raw,) = gelu_u_vjp(jnp.concatenate(du_parts, axis=0))
        dv_raw, dlng, dlnb = norm_vjp(jnp.concatenate(dvn_parts, axis=0))
        dpm_ref[:, 0:256] = du_raw.astype(BF16)
        dpm_ref[:, 256:512] = dv_raw.astype(BF16)
        _accumulate(dlng_ref, dlng, first)
        _accumulate(dlnb_ref, dlnb, first)
        _accumulate(dsgb_ref, dsgb, first)
        for g in range(SG_GROUPS):
            _accumulate(dwm_ref.at[g], jnp.where(tri, dwm[g], 0.0), first)

        xin, bg, cg = pm_ref[:, 512:768], pm_ref[:, 768:1024], pm_ref[:, 1024:1280]
        z = cg * xin
        zext = jnp.concatenate([before[:, 1024:1280] * before[:, 512:768], z], axis=0)
        z1, z2 = _shift_down(zext, 1, t), _shift_down(zext, 2, t)
        w0, w1, w2 = cw_ref[0:1, :], cw_ref[1:2, :], cw_ref[2:3, :]
        dcv = dcv_ref[...]
        y = w0 * z2 + w1 * z1 + w2 * z
        dy = dcv * bg
        dyext = jnp.concatenate([dy, jnp.where(i < n - 1, dcva_ref[...], 0.0) * after[:, 768:1024]], axis=0)
        dz = w2 * dy + w1 * _shift_up(dyext, 1, t) + w0 * _shift_up(dyext, 2, t)
        dpm_ref[:, 512:768] = (dz * cg).astype(BF16)
        dpm_ref[:, 768:1024] = (dcv * y).astype(BF16)
        dpm_ref[:, 1024:1280] = (dz * xin).astype(BF16)
        dcw = jnp.concatenate([jnp.sum(dy * z2, axis=0, keepdims=True), jnp.sum(dy * z1, axis=0, keepdims=True),
                               jnp.sum(dy * z, axis=0, keepdims=True)], axis=0)
        _accumulate(dcw_ref, jnp.concatenate([dcw, jnp.zeros((8 - CONV_K, BR_W), F32)], axis=0), first)

        p = pm_ref[:, 1280:1536]
        tok = i * t + lax.broadcasted_iota(jnp.int32, (t, 1), 0)
        sums = _pool_windows(jnp.concatenate([before[:, 1280:1536], p], axis=0))
        pooled = _group_select([sw[HALO:HALO + t] / cnt - p for sw, cnt in zip(sums, _pool_counts(tok))]).astype(BF16)
        mixed = _dot(pooled, wp_ref[...])
        dpl = dpl_ref[...]
        ps = ps_ref[...]
        dmix = (dpl * ps).astype(BF16)
        _accumulate(dps_ref, jnp.sum(dpl * mixed, axis=0, keepdims=True), first)
        _accumulate(dwp_ref, _dot(pooled, dmix, TN), first)
        dmix_after = (jnp.where(i < n - 1, dpla_ref[...], 0.0) * ps).astype(BF16)
        dpo = _dot(dmix, wp_ref[...], NT)
        dpo_ext = jnp.concatenate([dpo, _dot(dmix_after, wp_ref[...], NT)], axis=0)
        tok_ext = i * t + lax.broadcasted_iota(jnp.int32, (t + HALO, 1), 0)
        dp_groups = []
        for g, (win, cnt) in enumerate(zip(POOL_WINDOWS, _pool_counts(tok_ext))):
            e = dpo_ext / cnt
            acc = e
            span = 1
            while span < win:
                acc = acc + pltpu.roll(acc, t + HALO - span, 0)
                span *= 2
            dp_groups.append(acc[0:t] - dpo)
        dpm_ref[:, 1280:1536] = _group_select(dp_groups).astype(BF16)

    row_blk = lambda w: pl.BlockSpec((t, w), lambda i: (i, 0))
    after_blk = lambda w: pl.BlockSpec((HALO, w), lambda i: (jnp.minimum((i + 1) * nb, n * nb - 1), 0))
    return _pcall(
        body, name="mixers_bwd", grid=(n,),
        in_specs=[row_blk(PM_W), _halo_before(t), after_blk(PM_W), row_blk(BR_W), row_blk(BR_W), after_blk(BR_W),
                  row_blk(BR_W), after_blk(BR_W), _whole(ln_g.shape), _whole(ln_b.shape), _whole(wm.shape),
                  _whole(wmt.shape), _whole(sgb.shape), _whole(conv_w.shape), _whole(wp.shape), _whole(pscale.shape)],
        out_specs=[row_blk(PM_W), _acc((1, BR_W)), _acc((1, BR_W)), _acc((SG_GROUPS, SG_CHUNK, SG_CHUNK)),
                   _acc((SG_CHUNK, BR_W)), _acc((8, BR_W)), _acc((BR_W, BR_W)), _acc((1, BR_W))],
        out_shape=[_sds((s, PM_W), BF16), _sds((1, BR_W), F32), _sds((1, BR_W), F32),
                   _sds((SG_GROUPS, SG_CHUNK, SG_CHUNK), F32), _sds((SG_CHUNK, BR_W), F32), _sds((8, BR_W), F32),
                   _sds((BR_W, BR_W), F32), _sds((1, BR_W), F32)],
        vmem=VMEM_LIMIT)(pm, pm, pm, dbsg, dbcv, dbcv, dbpl, dbpl, ln_g, ln_b, wm, wmt, sgb, conv_w, wp, pscale)


def _head_delta(o_ref, do_ref, t):
    prod = o_ref[...].astype(F32) * do_ref[...]
    lane = lax.broadcasted_iota(jnp.int32, (t, LANES), 1)
    out = []
    for h in range(N_HEADS):
        pair = prod[:, (h // 2) * LANES:(h // 2 + 1) * LANES]
        mine = (lane < D_V) if h % 2 == 0 else (lane >= D_V)
        out.append(jnp.broadcast_to(jnp.sum(jnp.where(mine, pair, 0.0), axis=1, keepdims=True), (t, LANES)))
    return out


def _head_do(do_ref, h, t):
    lane = lax.broadcasted_iota(jnp.int32, (t, LANES), 1)
    mine = (lane < D_V) if h % 2 == 0 else (lane >= D_V)
    pair = do_ref[:, (h // 2) * LANES:(h // 2 + 1) * LANES]
    return jnp.where(mine, pair, 0.0).astype(BF16)


def _flash_bwd(q, k, v, o, do, lse, t, riding):
    s = v.shape[0]
    n = s // t
    reps = t // LANES

    qi_tab, kj_tab = _causal_steps(n, key_major=True)
    n_steps = int(qi_tab.shape[0])

    def body(qi_ref, kj_ref, q_ref, k_ref, v_ref, o_ref, do_ref, lse_ref, x_ref, dq_ref, dk_ref, dv_ref, land_ref,
             dv_scr, send_sems, recv_sems, local_sem):
        step_id = pl.program_id(0)
        qi, kj = qi_ref[step_id], kj_ref[step_id]
        at_end = _ride_along(riding, step_id, n_steps, (x_ref, land_ref, send_sems, recv_sems, local_sem))

        @pl.when(step_id == 0)
        def _():
            dq_ref[...] = jnp.zeros(dq_ref.shape, F32)

        @pl.when(qi == kj)
        def _():
            dk_ref[...] = jnp.zeros(dk_ref.shape, F32)
            dv_scr[...] = jnp.zeros(dv_scr.shape, F32)

        def step(masked):
            delta = _head_delta(o_ref, do_ref, t)
            q_rows = pl.ds(pl.multiple_of(qi * t, t), t)
            for h in range(N_HEADS):
                qh, kh = q_ref[h], k_ref[h]
                p = jnp.exp(_causal_scores(qh, kh, masked) - jnp.tile(lse_ref[h], (1, reps)))
                pair = slice((h // 2) * LANES, (h // 2 + 1) * LANES)
                dv_scr[h] += _dot(p.astype(BF16), do_ref[:, pair].astype(BF16), TN)
                dp = _dot(_head_do(do_ref, h, t), v_ref[:, pair], NT)
                ds = (p * (dp - jnp.tile(delta[h], (1, reps))) * SM_SCALE).astype(BF16)
                dk_ref[h] += _dot(ds, qh, TN)
                dq_ref[h, q_rows, :] += _dot(ds, kh)

        @pl.when(qi > kj)
        def _():
            step(False)

        @pl.when(qi == kj)
        def _():
            step(True)

        @pl.when(qi == n - 1)
        def _():
            lane = lax.broadcasted_iota(jnp.int32, (t, LANES), 1)
            for pr in range(N_HEADS // 2):
                dv_ref[:, pr * LANES:(pr + 1) * LANES] = jnp.where(lane < D_V, dv_scr[2 * pr], dv_scr[2 * pr + 1]).astype(BF16)

        at_end()

    q_blk = pl.BlockSpec((N_HEADS, t, HEAD_PAD), lambda i, qi, kj: (0, qi[i], 0))
    k_blk = pl.BlockSpec((N_HEADS, t, HEAD_PAD), lambda i, qi, kj: (0, kj[i], 0))
    v_blk = pl.BlockSpec((t, BR_W), lambda i, qi, kj: (kj[i], 0))
    o_blk = pl.BlockSpec((t, BR_W), lambda i, qi, kj: (qi[i], 0))
    dq_sds = _sds((N_HEADS, s, HEAD_PAD), F32)
    return _pcall(
        body, name="flash_bwd_exchanging", grid=(n_steps,), prefetch=2,
        in_specs=[q_blk, k_blk, v_blk, o_blk, o_blk, q_blk, HBM_REF],
        out_specs=[_whole(dq_sds.shape), k_blk, v_blk, HBM_REF],
        out_shape=[dq_sds, dq_sds, _sds((s, BR_W), BF16), riding[2]],
        scratch=[pltpu.VMEM((N_HEADS, t, LANES), F32)] + list(EXCHANGE_SEMS),
        vmem=VMEM_LIMIT)(qi_tab, kj_tab, q, k, v, o, do, lse, riding[1])


def _attn_prep_bwd(pa, dq, dk, dv, cosf, sins, qn, kvn, wq, wqs, wk, wv, t):
    s = pa.shape[0]

    def body(pa_ref, dq_ref, dk_ref, dv_ref, cos_ref, sin_ref, qn_ref, kvn_ref, wq_ref, wqs_ref, wk_ref, wv_ref,
             dpa_ref, dwq_ref, dwqs_ref, dwk_ref, dwv_ref, dqn_ref, dkvn_ref):
        first = pl.program_id(0) == 0
        cosv, sinv = cos_ref[...], sin_ref[...]
        cq, ckv = pa_ref[:, 0:Q_RANK], pa_ref[:, Q_RANK:Q_RANK + KV_RANK]
        cqn = _rms(cq, qn_ref[...]).astype(BF16)
        ckvn = _rms(ckv, kvn_ref[...]).astype(BF16)
        dvv = dv_ref[...]
        _accumulate(dwv_ref, _dot(dvv, ckvn, TN), first)
        dk_rope = dk_ref[0]
        for h in range(1, N_HEADS):
            dk_rope = dk_rope + dk_ref[h]
        dqa = jnp.concatenate([(dq_ref[h] * cosv).astype(BF16) for h in range(N_HEADS)], axis=1)
        dqs = jnp.concatenate([(dq_ref[h] * sinv).astype(BF16) for h in range(N_HEADS)], axis=1)
        dkb = jnp.concatenate([dk_ref[h].astype(BF16) for h in range(N_HEADS)], axis=1)
        _accumulate(dwq_ref, _dot(dqa, cqn, TN), first)
        _accumulate(dwqs_ref, _dot(dqs, cqn, TN), first)
        _accumulate(dwk_ref, _dot(dkb, ckvn, TN), first)
        dcqn = _dot(dqa, wq_ref[...]) + _dot(dqs, wqs_ref[...])
        dckvn = _dot(dvv, wv_ref[...]) + _dot(dkb, wk_ref[...])
        dcq, dqn = _rms_bwd(cq, qn_ref[...], dcqn)
        dckv, dkvn = _rms_bwd(ckv, kvn_ref[...], dckvn)
        _accumulate(dqn_ref, dqn, first)
        _accumulate(dkvn_ref, dkvn, first)
        dpa_ref[:, 0:Q_RANK] = dcq.astype(BF16)
        dpa_ref[:, Q_RANK:Q_RANK + KV_RANK] = dckv.astype(BF16)
        dpa_ref[:, 384:512] = (dk_rope * cosv).astype(BF16)
        dpa_ref[:, 512:640] = (dk_rope * sinv).astype(BF16)

    head_blk = pl.BlockSpec((N_HEADS, t, HEAD_PAD), lambda i: (0, i, 0))
    return _pcall(
        body, name="attn_prep_bwd", grid=(s // t,),
        in_specs=[_rows(t, PA_W), head_blk, head_blk, _rows(t, BR_W), _rows(t, LANES), _rows(t, LANES),
                  _whole(qn.shape), _whole(kvn.shape), _whole(wq.shape), _whole(wqs.shape), _whole(wk.shape), _whole(wv.shape)],
        out_specs=[_rows(t, PA_W), _acc(wq.shape), _acc(wqs.shape), _acc(wk.shape), _acc(wv.shape),
                   _acc(qn.shape), _acc(kvn.shape)],
        out_shape=[_sds((s, PA_W), BF16), _sds(wq.shape, F32), _sds(wqs.shape, F32), _sds(wk.shape, F32),
                   _sds(wv.shape, F32), _sds(qn.shape, F32), _sds(kvn.shape, F32)],
        vmem=VMEM_LIMIT)(pa, dq, dk, dv, cosf, sins, qn, kvn, wq, wqs, wk, wv)


def _inproj_bwd(x, g1, dxmid, dpa, dpm, dpres, w_in, wg, t):
    s = x.shape[0]
    n = s // t

    def body(x_ref, g1_ref, dxm_ref, dpa_ref, dpm_ref, d0_ref, d1_ref, d2_ref, d3_ref, w_ref, wg_ref,
             dx_ref, dg1_ref, dwa_ref, dwm_ref, acc_a, acc_m):
        first, last = pl.program_id(0) == 0, pl.program_id(0) == n - 1
        xv, g1v = x_ref[...], g1_ref[...]
        hb = _rms(xv, g1v).astype(BF16)
        dpav, dpmv = dpa_ref[...], dpm_ref[...]
        dh = _dot(dpav, w_ref[:PA_W, :]) + _dot(dpmv, w_ref[PA_W:, :])
        for k, d_ref in enumerate((d0_ref, d1_ref, d2_ref, d3_ref)):
            dh = dh + _dot(d_ref[...], wg_ref[k * D_MODEL:(k + 1) * D_MODEL, :])
        dx, dg1 = _rms_bwd(xv, g1v, dh)
        dx_ref[...] = dxm_ref[...] + dx
        _accumulate(dg1_ref, dg1, first)
        _accumulate_out(acc_a, dwa_ref, _dot(dpav, hb, TN), first, last)
        _accumulate_out(acc_m, dwm_ref, _dot(dpmv, hb, TN), first, last)

    return _pcall(
        body, name="inproj_bwd", grid=(n,),
        in_specs=[_rows(t, D_MODEL), _whole(g1.shape), _rows(t, D_MODEL), _rows(t, PA_W), _rows(t, PM_W)] +
                 [_rows(t, D_MODEL)] * 4 + [_whole(w_in.shape), _whole(wg.shape)],
        out_specs=[_rows(t, D_MODEL), _acc((1, D_MODEL)), _acc((PA_W, D_MODEL)), _acc((PM_W, D_MODEL))],
        out_shape=[_sds((s, D_MODEL), F32), _sds((1, D_MODEL), F32), _sds((PA_W, D_MODEL), BF16), _sds((PM_W, D_MODEL), BF16)],
        scratch=[pltpu.VMEM((PA_W, D_MODEL), F32), pltpu.VMEM((PM_W, D_MODEL), F32)],
        vmem=VMEM_LIMIT)(x, g1, dxmid, dpa, dpm, *dpres, w_in, wg)


def _my_place():
    return lax.axis_index("x"), lax.axis_index("y"), lax.axis_index("c")


def _flip(place, k):
    x, y, c = place
    return (1 - x if k & 4 else x, 1 - y if k & 2 else y, 1 - c if k & 1 else c)


def _rank(place):
    x, y, c = place
    return 4 * x + 2 * y + c


EXCHANGE_SEMS = (pltpu.SemaphoreType.DMA((7,)), pltpu.SemaphoreType.DMA((7,)), pltpu.SemaphoreType.DMA)
HBM_REF = pl.BlockSpec(memory_space=pl.ANY)


def _gather_plan(x_ref, out_ref, send_sems, recv_sems, local_sem):
    me = _my_place()
    sibling = _flip(me, 1)
    chips = (4, 2, 6)

    def copy(k, src_place, to, src=None):
        slot = out_ref.at[_rank(src_place)]
        return pltpu.make_async_remote_copy(
            src_ref=slot if src is None else src, dst_ref=slot, send_sem=send_sems.at[k], recv_sem=recv_sems.at[k],
            device_id=to, device_id_type=MESH_ID)

    mine = pltpu.make_async_copy(x_ref, out_ref.at[_rank(me)], local_sem)
    first = [copy(0, me, sibling, src=x_ref)] + [copy(1 + j, me, _flip(me, kc), src=x_ref) for j, kc in enumerate(chips)]
    passed = [copy(4 + j, _flip(me, kc), sibling) for j, kc in enumerate(chips)]

    def start():
        mine.start()
        for cp in first:
            cp.start()

    def forward():
        for j, kc in enumerate(chips):
            copy(1 + j, _flip(me, kc), me).wait_recv()
            passed[j].start()

    def finish():
        copy(0, sibling, me).wait_recv()
        for j, kc in enumerate(chips):
            copy(4 + j, _flip(sibling, kc), me).wait_recv()
        for cp in first + passed:
            cp.wait_send()
        mine.wait()

    return start, forward, finish


def _exchange_plan(x_ref, out_ref, send_sems, recv_sems, local_sem):
    me = _my_place()
    my_rank = _rank(me)
    mine = pltpu.make_async_copy(x_ref.at[my_rank], out_ref.at[my_rank], local_sem)
    sends = [pltpu.make_async_remote_copy(
        src_ref=x_ref.at[_rank(_flip(me, k))], dst_ref=out_ref.at[my_rank], send_sem=send_sems.at[k - 1],
        recv_sem=recv_sems.at[k - 1], device_id=_flip(me, k), device_id_type=MESH_ID) for k in range(1, N_DEV)]

    def start():
        mine.start()
        for cp in sends:
            cp.start()

    def forward():
        pass

    def finish():
        for k in range(1, N_DEV):
            slot = out_ref.at[_rank(_flip(me, k))]
            pltpu.make_async_remote_copy(
                src_ref=slot, dst_ref=slot, send_sem=send_sems.at[k - 1], recv_sem=recv_sems.at[k - 1],
                device_id=_flip(me, k), device_id_type=MESH_ID).wait_recv()
        for cp in sends:
            cp.wait_send()
        mine.wait()

    return start, forward, finish


def _exchange_alone(plan, src, out_sds, name):
    def body(x_ref, out_ref, send_sems, recv_sems, local_sem):
        start, forward, finish = plan(x_ref, out_ref, send_sems, recv_sems, local_sem)
        start()
        forward()
        finish()

    return pl.pallas_call(body, name=name, out_shape=out_sds, in_specs=[HBM_REF], out_specs=HBM_REF,
                          scratch_shapes=list(EXCHANGE_SEMS))(src)


def _all_gather(shard, name):
    return _exchange_alone(_gather_plan, shard, _sds((N_DEV,) + shard.shape, shard.dtype), name)


def _all_to_all(blocks, name):
    return _exchange_alone(_exchange_plan, blocks, _sds(blocks.shape, blocks.dtype), name)


def _riding(plan, src, out_sds):
    return (plan, src, out_sds)


def _ride_along(riding, step_id, n_steps, refs):
    start, forward, finish = riding[0](*refs)

    @pl.when(step_id == 0)
    def _():
        start()

    @pl.when(step_id == (3 * n_steps) // 4)
    def _():
        forward()

    def at_end():
        @pl.when(step_id == n_steps - 1)
        def _():
            finish()

    return at_end


def _sum_slots(parts, name):
    _, r, c = parts.shape
    t = math.gcd(r, ROW_PAD)

    def body(p_ref, o_ref):
        acc = p_ref[0].astype(F32)
        for d in range(1, N_DEV):
            acc = acc + p_ref[d].astype(F32)
        o_ref[...] = acc

    return _pcall(
        body, name=name, grid=(r // t,),
        in_specs=[pl.BlockSpec((N_DEV, t, c), lambda i: (0, i, 0))], out_specs=_rows(t, c),
        out_shape=_sds((r, c), F32))(parts)


def _adamw(w, g, m, v, name):
    r, c = w.shape
    t = r if r <= ROW_PAD else math.gcd(r, ROW_PAD)

    def body(w_ref, g_ref, m_ref, v_ref, d_ref, nm_ref, nv_ref):
        gv = g_ref[...]
        nm = ADAM_B1 * m_ref[...] + (1.0 - ADAM_B1) * gv
        nv = ADAM_B2 * v_ref[...] + (1.0 - ADAM_B2) * jnp.square(gv)
        m_hat = nm / (1.0 - ADAM_B1 ** ADAM_STEP)
        v_hat = nv / (1.0 - ADAM_B2 ** ADAM_STEP)
        d_ref[...] = -ADAM_LR * (m_hat / (jnp.sqrt(v_hat) + ADAM_EPS) + ADAM_WD * w_ref[...])
        nm_ref[...] = nm
        nv_ref[...] = nv

    return _pcall(
        body, name=name, grid=(r // t,),
        in_specs=[_rows(t, c)] * 4, out_specs=[_rows(t, c)] * 3, out_shape=[_sds((r, c), F32)] * 3)(w, g, m, v)


BEFORE_ATTN = ("w_in", "w_uq", "w_ukv")
AFTER_ATTN = ("w_br_mla", "w_br_sg", "w_br_conv", "w_br_pool", "w_out", "w_ff1", "w_ff2", "conv_w")
SHARDED = BEFORE_ATTN + AFTER_ATTN
ROW_SHARDED = ("w_out", "w_ff2")
REPLICATED = ("norm_mix_pre", "gate_b", "q_norm", "kv_norm", "sg_ln_g", "sg_ln_b", "sg_w", "sg_b", "pool_w",
              "pool_scale", "norm_mix_post", "norm_ffn_pre", "norm_ffn_post")
ROW_PAD = 256
GRAD_ROW_PAD = 64
PART_ROWS = 16


def _pack_rows(arrays, dtype, multiple, lead=()):
    rows, offsets, at = [], [], 0
    zero_pad = ((0, 0),) * len(lead)
    for a in arrays:
        flat = a.reshape(lead + (-1,)).astype(dtype)
        n = -(-flat.shape[-1] // D_MODEL)
        n_pad = -(-n // PART_ROWS) * PART_ROWS
        flat = jnp.pad(flat, zero_pad + ((0, n_pad * D_MODEL - flat.shape[-1]),))
        rows.append(flat.reshape(lead + (n_pad, D_MODEL)))
        offsets.append((at, n))
        at += n_pad
    pad = -at % multiple
    if pad:
        rows.append(jnp.zeros(lead + (pad, D_MODEL), dtype))
    return jnp.concatenate(rows, axis=len(lead)), offsets


def _unpack_rows(buf, offsets, shapes):
    lead = buf.shape[:-2]
    out = []
    for (at, n), shape in zip(offsets, shapes):
        size = math.prod(shape)
        flat = buf[..., at:at + n, :].reshape(lead + (n * D_MODEL,))[..., :size]
        out.append(flat.reshape(lead + tuple(shape)))
    return out


def _pack_shards(p, depth, names):
    arrays = []
    for n in names:
        a = p[n][0] if n in ROW_SHARDED else jnp.swapaxes(p[n][0], -1, -2)
        arrays.append(lax.bitcast_convert_type(a, BF16) if n == "conv_w" else a.astype(BF16))
    packed, offsets = _pack_rows(arrays, BF16, GRAD_ROW_PAD, lead=(depth,))
    return packed, offsets, [a.shape[1:] for a in arrays]


def _unpack_layer_weights(names, gathered, offsets, shapes):
    full = {}
    for n, gth in zip(names, _unpack_rows(gathered, offsets, shapes)):
        if n == "conv_w":
            gth = lax.bitcast_convert_type(gth, F32)
        full[n] = gth.reshape((-1,) + gth.shape[2:])
        if n == "conv_w":
            full[n] = full[n].T
    return full


def _owner_major(grad_sharded_dim_first):
    a = grad_sharded_dim_first
    return a.reshape((N_DEV, a.shape[0] // N_DEV) + a.shape[1:])


def _owner_block_shape(name, shard_shape):
    k, n = shard_shape[-2:]
    return (k, n) if name in ROW_SHARDED else (n, k)


def _natural_shard(name, blocks):
    return blocks if name in ROW_SHARDED else jnp.swapaxes(blocks, -1, -2)


def _swap_halves(a, axis=-1):
    lo, hi = jnp.split(a, 2, axis=axis)
    return jnp.concatenate([hi, lo], axis=axis)


def _pad_rows(a, top, total):
    return jnp.pad(a, ((0, 0),) * (a.ndim - 2) + ((top, total - top - a.shape[-2]), (0, 0)))


def _weights_before_attn(full, small, l):
    w_in = full["w_in"]
    k_r = w_in[384:416]
    w_in_k = jnp.concatenate(
        [w_in[0:384], _pad_rows(k_r, D_NOPE, HEAD_PAD), _pad_rows(_swap_halves(k_r, axis=0), D_NOPE, HEAD_PAD), w_in[416:1952]], axis=0)
    w_uq = full["w_uq"].reshape(N_HEADS, D_NOPE + D_ROPE, Q_RANK)
    wq = _pad_rows(w_uq, 0, HEAD_PAD).reshape(N_HEADS * HEAD_PAD, Q_RANK)
    wqs = _pad_rows(_swap_halves(w_uq[:, D_NOPE:], axis=1), D_NOPE, HEAD_PAD).reshape(N_HEADS * HEAD_PAD, Q_RANK)
    w_ukv = full["w_ukv"].reshape(N_HEADS, D_NOPE + D_V, KV_RANK)
    wk = _pad_rows(w_ukv[:, :D_NOPE], 0, HEAD_PAD).reshape(N_HEADS * HEAD_PAD, KV_RANK)
    wv = w_ukv[:, D_NOPE:].reshape(N_HEADS * D_V, KV_RANK)
    tri = jnp.tril(jnp.ones((SG_CHUNK, SG_CHUNK), bool))
    wm = jnp.where(tri, small["sg_w"][l], 0.0)
    pool_w = small["pool_w"][l]
    wp = jnp.zeros((BR_W, BR_W), F32)
    for g in range(len(POOL_WINDOWS)):
        wp = wp.at[g * POOL_GROUP:(g + 1) * POOL_GROUP, g * POOL_GROUP:(g + 1) * POOL_GROUP].set(pool_w[g])
    vec = lambda name: small[name][l][None, :]
    return dict(
        w_in=w_in_k, wg=w_in[1952:], bg=vec("gate_b"), g1=vec("norm_mix_pre"), qn=vec("q_norm"), kvn=vec("kv_norm"),
        wq=wq, wqs=wqs, wk=wk, wv=wv,
        ln_g=vec("sg_ln_g"), ln_b=vec("sg_ln_b"), wm=wm.astype(BF16), wmt=jnp.swapaxes(wm, 1, 2).astype(BF16),
        sgb=jnp.repeat(small["sg_b"][l].T, POOL_GROUP, axis=1), wp=wp.astype(BF16), pscale=vec("pool_scale"),
        g2=vec("norm_mix_post"), g3=vec("norm_ffn_pre"), g4=vec("norm_ffn_post"))


def _weights_after_attn(full):
    return dict(
        conv_w=full["conv_w"], wbr=jnp.stack([full["w_br_mla"], full["w_br_sg"], full["w_br_conv"], full["w_br_pool"]]),
        wout=full["w_out"], w1=full["w_ff1"], w2=full["w_ff2"])


def _grads_after_attn(g):
    own = _owner_major
    return dict(
        w_br_mla=own(g["dwbr"][0]), w_br_sg=own(g["dwbr"][1]), w_br_conv=own(g["dwbr"][2]), w_br_pool=own(g["dwbr"][3]),
        w_out=own(g["dwout"]), w_ff1=own(g["dw1"]), w_ff2=own(g["dw2"]), conv_w=own(g["dcw"][:CONV_K].T))


def _grads_before_attn(g):
    dwa = g["dwa"]
    rope_rows = slice(D_NOPE, D_NOPE + D_ROPE)
    d_kr = dwa[384:512][rope_rows].astype(F32) + _swap_halves(dwa[512:640][rope_rows], axis=0).astype(F32)
    w_in = jnp.concatenate([dwa[0:384], d_kr.astype(BF16), g["dwm_in"]] + list(g["dwg"]), axis=0)
    dwq, dwqs = (g[n].reshape(N_HEADS, HEAD_PAD, Q_RANK) for n in ("dwq", "dwqs"))
    d_rope = dwq[:, rope_rows] + _swap_halves(dwqs[:, rope_rows], axis=1)
    w_uq = jnp.concatenate([dwq[:, :D_NOPE], d_rope], axis=1).reshape(N_HEADS * (D_NOPE + D_ROPE), Q_RANK)
    dwk = g["dwk"].reshape(N_HEADS, HEAD_PAD, KV_RANK)
    w_ukv = jnp.concatenate([dwk[:, :D_NOPE], g["dwv"].reshape(N_HEADS, D_V, KV_RANK)], axis=1)
    w_ukv = w_ukv.reshape(N_HEADS * (D_NOPE + D_V), KV_RANK)
    pool_w = jnp.stack([g["dwp"][i * POOL_GROUP:(i + 1) * POOL_GROUP, i * POOL_GROUP:(i + 1) * POOL_GROUP]
                        for i in range(len(POOL_WINDOWS))])
    sg_b = g["dsgb"].reshape(SG_CHUNK, SG_GROUPS, POOL_GROUP).sum(axis=-1).T
    own = _owner_major
    return dict(
        w_in=own(w_in), w_uq=own(w_uq), w_ukv=own(w_ukv),
        norm_mix_pre=g["dg1"][0], gate_b=jnp.concatenate([b[0] for b in g["dbg"]]), q_norm=g["dqn"][0], kv_norm=g["dkvn"][0],
        sg_ln_g=g["dlng"][0], sg_ln_b=g["dlnb"][0], sg_w=g["dwm"], sg_b=sg_b, pool_w=pool_w, pool_scale=g["dps"][0],
        norm_mix_post=g["dg2"][0], norm_ffn_pre=g["dg3"][0], norm_ffn_post=g["dg4"][0])


def _rope_tables(positions):
    inv_freq = ROPE_BASE ** (-jnp.arange(0, D_ROPE, 2, dtype=F32) / D_ROPE)
    ang = positions.astype(F32)[:, None] * inv_freq
    cos, sin = jnp.cos(ang), jnp.sin(ang)
    s = positions.shape[0]
    cosf = jnp.concatenate([jnp.ones((s, D_NOPE), F32), cos, cos, jnp.zeros((s, HEAD_PAD - D_NOPE - D_ROPE), F32)], axis=1)
    sins = jnp.concatenate([jnp.zeros((s, D_NOPE), F32), -sin, sin, jnp.zeros((s, HEAD_PAD - D_NOPE - D_ROPE), F32)], axis=1)
    return cosf, sins


def _layer_fwd(x, w, weights_after_attn, cosf, sins, tiles, riding):
    t, ta = tiles["tok"], tiles["attn"]
    hb, pa, pm = _inproj_fwd(x, w["g1"], w["w_in"], tiles["wgrad"])
    q, k, v = _attn_prep_fwd(pa, cosf, sins, w["qn"], w["kvn"], w["wq"], w["wqs"], w["wk"], w["wv"], t)
    o, lse, landed = _flash_fwd(q, k, v, ta, riding)
    w = {**w, **weights_after_attn(landed)}
    bsg, bcv, bpl = _mixers_fwd(pm, w["ln_g"], w["ln_b"], w["wm"], w["sgb"], w["conv_w"], w["wp"], w["pscale"], t)
    x_mid, merged, tt = _merge_fwd(x, hb, (o, bsg, bcv, bpl), w["wg"], w["bg"], w["wbr"], w["wout"], w["g2"], t)
    x_out, f = _ffn_fwd(x_mid, w["g3"], w["w1"], w["w2"], w["g4"], t)
    saved = dict(x=x, hb=hb, pa=pa, pm=pm, q=q, k=k, v=v, o=o, lse=lse, bsg=bsg, bcv=bcv, bpl=bpl, x_mid=x_mid,
                 merged=merged, tt=tt, f=f)
    return x_out, saved, w, landed


def _layer_bwd(dxo, a, w, cosf, sins, tiles, pending):
    t, ta, tb, tw = tiles["tok"], tiles["attn"], tiles["inproj_bwd"], tiles["wgrad"]
    g = {}
    h2, df, g["dg4"] = _ffn_bwd_norms(a["x_mid"], a["f"], dxo, w["g3"], w["g4"], t)
    da, g["dw1"], g["dw2"] = _ffn_bwd_weights(h2, df, w["w1"], w["w2"], tw)
    dxmid, g["dg3"] = _ffn_bwd_input(da, w["w1"], a["x_mid"], dxo, w["g3"], t)
    dm, g["dwout"], g["dg2"] = _merge_bwd_out(a["tt"], dxmid, w["g2"], w["wout"], a["merged"], tw)
    dbrs, dpres, g["dwg"], g["dbg"], g["dwbr"] = [], [], [], [], []
    for k, br in enumerate((a["o"], a["bsg"], a["bcv"], a["bpl"])):
        dbr, dpre, dwg, dbg, dwbr = _merge_bwd_branch(k, a["hb"], br, dm, w["wg"], w["bg"], w["wbr"], tw)
        dbrs.append(dbr); dpres.append(dpre); g["dwg"].append(dwg); g["dbg"].append(dbg); g["dwbr"].append(dwbr)
    dpm, g["dlng"], g["dlnb"], g["dwm"], g["dsgb"], g["dcw"], g["dwp"], g["dps"] = _mixers_bwd(
        a["pm"], dbrs[1], dbrs[2], dbrs[3], w["ln_g"], w["ln_b"], w["wm"], w["wmt"], w["sgb"], w["conv_w"], w["wp"], w["pscale"], t)
    after = _grads_after_attn(g)
    send, after_offsets = _pack_rows([after[n] for n in AFTER_ATTN], BF16, GRAD_ROW_PAD, lead=(N_DEV,))
    after_rows = send.shape[1]
    if pending is not None:
        send = jnp.concatenate([send, pending], axis=1)
    riding = _riding(_exchange_plan, send, _sds(send.shape, send.dtype))
    dq, dk, dv, landed = _flash_bwd(a["q"], a["k"], a["v"], a["o"], dbrs[0], a["lse"], ta, riding)
    dpa, g["dwq"], g["dwqs"], g["dwk"], g["dwv"], g["dqn"], g["dkvn"] = _attn_prep_bwd(
        a["pa"], dq, dk, dv, cosf, sins, w["qn"], w["kvn"], w["wq"], w["wqs"], w["wk"], w["wv"], t)
    dx, g["dg1"], g["dwa"], g["dwm_in"] = _inproj_bwd(a["x"], w["g1"], dxmid, dpa, dpm, dpres, w["w_in"], w["wg"], tb)
    return dx, _grads_before_attn(g), landed, (after_rows, after_offsets)


def _step(p, x, positions, loss_target):
    s = x.shape[0]
    depth = p["w_in"][0].shape[0]
    tiles = dict(tok=min(TOK_TILE, s), attn=min(ATTN_TILE, s), inproj_bwd=min(INPROJ_BWD_TILE, s),
                 wgrad=min(WGRAD_TILE, s))

    small = {n: p[n][0] for n in REPLICATED}
    cosf, sins = _rope_tables(positions)

    pack_b, off_b, shapes_b = _pack_shards(p, depth, BEFORE_ATTN)
    pack_a, off_a, shapes_a = _pack_shards(p, depth, AFTER_ATTN)
    rows_a = pack_a.shape[1]
    before = _all_gather(pack_b[0], "weight_all_gather")
    weights, acts = [], []
    h = x
    for l in range(depth):
        w = _weights_before_attn(_unpack_layer_weights(BEFORE_ATTN, before, off_b, shapes_b), small, l)
        src = pack_a[l] if l + 1 == depth else jnp.concatenate([pack_a[l], pack_b[l + 1]], axis=0)
        riding = _riding(_gather_plan, src, _sds((N_DEV,) + src.shape, src.dtype))

        def after(landed):
            return _weights_after_attn(_unpack_layer_weights(AFTER_ATTN, landed[:, :rows_a], off_a, shapes_a))

        h, saved, w, landed = _layer_fwd(h, w, after, cosf, sins, tiles, riding)
        weights.append(w)
        acts.append(saved)
        before = landed[:, rows_a:] if l + 1 < depth else None
    dh, loss_blk = _loss_head(h, loss_target, tiles["tok"])
    loss = lax.psum(loss_blk[0, 0], ("x", "y", "c"))

    grads_b = [None] * depth
    sum_a, sum_b = [None] * depth, [None] * depth
    pending = None
    for l in reversed(range(depth)):
        dh, grads_b[l], landed, (rows_ga, off_ga) = _layer_bwd(dh, acts[l], weights[l], cosf, sins, tiles, pending)
        summed = _sum_slots(landed, "grad_shard_sum")
        sum_a[l] = summed[:rows_ga]
        if pending is not None:
            sum_b[l + 1] = summed[rows_ga:]
        pending, off_gb = _pack_rows([grads_b[l][n] for n in BEFORE_ATTN], BF16, GRAD_ROW_PAD, lead=(N_DEV,))
    sum_b[0] = _sum_slots(_all_to_all(pending, "grad_all_to_all"), "grad_shard_sum")
    g_shard = {}
    for names, sums, offsets in ((AFTER_ATTN, sum_a, off_ga), (BEFORE_ATTN, sum_b, off_gb)):
        block_shapes = [_owner_block_shape(n, p[n][0].shape) for n in names]
        for n, b in zip(names, _unpack_rows(jnp.stack(sums), offsets, block_shapes)):
            g_shard[n] = _natural_shard(n, b)
    small_grads = [jnp.stack([grads_b[l][n] for l in range(depth)]) for n in REPLICATED]
    small_rows, small_offsets = _pack_rows(small_grads, F32, ROW_PAD)
    g_small_rows = _sum_slots(_all_gather(small_rows, "replicated_grad_all_gather"), "grad_replicated_sum")

    grad, delta, new_m, new_v = {}, {}, {}, {}
    for n in SHARDED:
        w, m, v = p[n]
        shape2 = (math.prod(w.shape[:-1]), w.shape[-1])
        grad[n] = g_shard[n]
        d, nm, nv = _adamw(w.reshape(shape2), grad[n].reshape(shape2), m.reshape(shape2), v.reshape(shape2), f"adamw_{n}")
        delta[n], new_m[n], new_v[n] = d.reshape(w.shape), nm.reshape(w.shape), nv.reshape(w.shape)
    packs = [_pack_rows([p[n][i] for n in REPLICATED], F32, ROW_PAD)[0] for i in range(3)]
    d, nm, nv = _adamw(packs[0], g_small_rows, packs[1], packs[2], "adamw_replicated")
    shapes = [p[n][0].shape for n in REPLICATED]
    for n, gg, dd, mm, vv in zip(REPLICATED, _unpack_rows(g_small_rows, small_offsets, shapes), _unpack_rows(d, small_offsets, shapes),
                                 _unpack_rows(nm, small_offsets, shapes), _unpack_rows(nv, small_offsets, shapes)):
        grad[n], delta[n], new_m[n], new_v[n] = gg, dd, mm, vv
    return loss, dh, grad, delta, new_m, new_v


WEIGHT_ORDER = ("norm_mix_pre", "w_in", "gate_b", "q_norm", "w_uq", "kv_norm", "w_ukv", "w_br_mla", "sg_ln_g", "sg_ln_b",
                "sg_w", "sg_b", "w_br_sg", "conv_w", "w_br_conv", "pool_w", "pool_scale", "w_br_pool", "w_out",
                "norm_mix_post", "norm_ffn_pre", "w_ff1", "w_ff2", "norm_ffn_post")


def kernel(x, positions, norm_mix_pre, w_in, gate_b, q_norm, w_uq, kv_norm, w_ukv, w_br_mla, sg_ln_g, sg_ln_b, sg_w, sg_b, w_br_sg, conv_w, w_br_conv, pool_w, pool_scale, w_br_pool, w_out, norm_mix_post, norm_ffn_pre, w_ff1, w_ff2, norm_ffn_post, loss_target, m_norm_mix_pre, m_w_in, m_gate_b, m_q_norm, m_w_uq, m_kv_norm, m_w_ukv, m_w_br_mla, m_sg_ln_g, m_sg_ln_b, m_sg_w, m_sg_b, m_w_br_sg, m_conv_w, m_w_br_conv, m_pool_w, m_pool_scale, m_w_br_pool, m_w_out, m_norm_mix_post, m_norm_ffn_pre, m_w_ff1, m_w_ff2, m_norm_ffn_post, v_norm_mix_pre, v_w_in, v_gate_b, v_q_norm, v_w_uq, v_kv_norm, v_w_ukv, v_w_br_mla, v_sg_ln_g, v_sg_ln_b, v_sg_w, v_sg_b, v_w_br_sg, v_conv_w, v_w_br_conv, v_pool_w, v_pool_scale, v_w_br_pool, v_w_out, v_norm_mix_post, v_norm_ffn_pre, v_w_ff1, v_w_ff2, v_norm_ffn_post):
    ws = (norm_mix_pre, w_in, gate_b, q_norm, w_uq, kv_norm, w_ukv, w_br_mla, sg_ln_g, sg_ln_b, sg_w, sg_b, w_br_sg, conv_w,
          w_br_conv, pool_w, pool_scale, w_br_pool, w_out, norm_mix_post, norm_ffn_pre, w_ff1, w_ff2, norm_ffn_post)
    ms = (m_norm_mix_pre, m_w_in, m_gate_b, m_q_norm, m_w_uq, m_kv_norm, m_w_ukv, m_w_br_mla, m_sg_ln_g, m_sg_ln_b, m_sg_w,
          m_sg_b, m_w_br_sg, m_conv_w, m_w_br_conv, m_pool_w, m_pool_scale, m_w_br_pool, m_w_out, m_norm_mix_post,
          m_norm_ffn_pre, m_w_ff1, m_w_ff2, m_norm_ffn_post)
    vs = (v_norm_mix_pre, v_w_in, v_gate_b, v_q_norm, v_w_uq, v_kv_norm, v_w_ukv, v_w_br_mla, v_sg_ln_g, v_sg_ln_b, v_sg_w,
          v_sg_b, v_w_br_sg, v_conv_w, v_w_br_conv, v_pool_w, v_pool_scale, v_w_br_pool, v_w_out, v_norm_mix_post,
          v_norm_ffn_pre, v_w_ff1, v_w_ff2, v_norm_ffn_post)
    p = {n: (w, m, v) for n, w, m, v in zip(WEIGHT_ORDER, ws, ms, vs)}
    loss, grad_x, grad, delta, new_m, new_v = _step(p, x[0], positions[0], loss_target[0])
    return (loss, grad_x[None], *[grad[n] for n in WEIGHT_ORDER], *[delta[n] for n in WEIGHT_ORDER],
            *[new_m[n] for n in WEIGHT_ORDER], *[new_v[n] for n in WEIGHT_ORDER])
```

```python
import functools
import math

import jax
import jax.numpy as jnp
from jax import lax
from jax.experimental import pallas as pl
from jax.experimental.pallas import tpu as pltpu

F32 = jnp.float32
BF16 = jnp.bfloat16

D_MODEL = 1024
N_HEADS = 4
D_NOPE = 64
D_ROPE = 32
D_V = 64
Q_RANK = 256
KV_RANK = 128
BR_W = 256
SG_CHUNK = 128
SG_GROUPS = 4
POOL_WINDOWS = (2, 4, 8, 16)
POOL_GROUP = 64
CONV_K = 3
D_FF = 4096
N_BRANCH = 4
N_IN = 6048
EPS = 1e-6
ROPE_BASE = 10000.0
ADAM_LR = 0.001
ADAM_B1 = 0.9
ADAM_B2 = 0.999
ADAM_EPS = 1e-08
ADAM_WD = 0.01
ADAM_STEP = 10

N_DEV = 8
LANES = 128
HEAD_PAD = 128
HALO = 16
PA_W = 640
PM_W = 6 * BR_W
VMEM_LIMIT = 56 * 1024 * 1024
SM_SCALE = (D_NOPE + D_ROPE) ** -0.5
TOK_TILE = 512
ATTN_TILE = 1024
WGRAD_TILE = 1024
INPROJ_BWD_TILE = 512

NN = (((1,), (0,)), ((), ()))
NT = (((1,), (1,)), ((), ()))
TN = (((0,), (0,)), ((), ()))
MESH_ID = pl.DeviceIdType.MESH


def _dot(a, b, dims=NN):
    return lax.dot_general(a, b, dims, preferred_element_type=F32)


def _pcall(body, *, name, grid, in_specs, out_specs, out_shape, scratch=(), vmem=None, prefetch=0):
    params = pltpu.CompilerParams(vmem_limit_bytes=vmem)
    if prefetch:
        spec = pltpu.PrefetchScalarGridSpec(num_scalar_prefetch=prefetch, grid=grid, in_specs=in_specs,
                                            out_specs=out_specs, scratch_shapes=scratch)
        return pl.pallas_call(body, name=name, grid_spec=spec, out_shape=out_shape, compiler_params=params)
    return pl.pallas_call(
        body, name=name, grid=grid, in_specs=in_specs, out_specs=out_specs, out_shape=out_shape,
        scratch_shapes=scratch, compiler_params=params)


def _rows(t, width):
    return pl.BlockSpec((t, width), lambda i: (i, 0))


def _whole(shape):
    nd = len(shape)
    return pl.BlockSpec(tuple(shape), lambda *_: (0,) * nd, pipeline_mode=pl.Buffered(1))


def _acc(shape):
    nd = len(shape)
    return pl.BlockSpec(tuple(shape), lambda *_: (0,) * nd)


def _sds(shape, dtype):
    return jax.ShapeDtypeStruct(tuple(shape), dtype)


def _rms(x, g):
    return x * lax.rsqrt(jnp.mean(x * x, axis=-1, keepdims=True) + EPS) * g


def _rms_bwd(x, g, dy):
    r = lax.rsqrt(jnp.mean(x * x, axis=-1, keepdims=True) + EPS)
    xh = x * r
    dg = jnp.sum(dy * xh, axis=0, keepdims=True)
    dxh = dy * g
    dx = r * (dxh - xh * jnp.mean(dxh * xh, axis=-1, keepdims=True))
    return dx, dg


def _sigmoid(x):
    return 1.0 / (1.0 + jnp.exp(-x))


def _gelu(x):
    return jax.nn.gelu(x, approximate=True)


def _accumulate(ref, val, first):
    @pl.when(first)
    def _():
        ref[...] = val

    @pl.when(jnp.logical_not(first))
    def _():
        ref[...] += val


def _accumulate_out(acc_ref, out_ref, val, first, last):
    _accumulate(acc_ref, val, first)

    @pl.when(last)
    def _():
        out_ref[...] = acc_ref[...].astype(out_ref.dtype)


def _inproj_fwd(x, g1, w_in, t):
    s = x.shape[0]

    def body(x_ref, g_ref, w_ref, hb_ref, pa_ref, pm_ref):
        hb = _rms(x_ref[...], g_ref[...]).astype(BF16)
        hb_ref[...] = hb
        pa_ref[...] = _dot(hb, w_ref[:PA_W, :], NT)
        pm_ref[...] = _dot(hb, w_ref[PA_W:, :], NT)

    return _pcall(
        body, name="inproj_fwd", grid=(s // t,),
        in_specs=[_rows(t, D_MODEL), _whole((1, D_MODEL)), _whole(w_in.shape)],
        out_specs=[_rows(t, D_MODEL), _rows(t, PA_W), _rows(t, PM_W)],
        out_shape=[_sds((s, D_MODEL), BF16), _sds((s, PA_W), F32), _sds((s, PM_W), F32)],
        vmem=VMEM_LIMIT)(x, g1, w_in)


def _attn_prep_fwd(pa, cosf, sins, qn, kvn, wq, wqs, wk, wv, t):
    s = pa.shape[0]

    def body(pa_ref, cos_ref, sin_ref, qn_ref, kvn_ref, wq_ref, wqs_ref, wk_ref, wv_ref, q_ref, k_ref, v_ref):
        cosv, sinv = cos_ref[...], sin_ref[...]
        cqn = _rms(pa_ref[:, 0:Q_RANK], qn_ref[...]).astype(BF16)
        ckvn = _rms(pa_ref[:, Q_RANK:Q_RANK + KV_RANK], kvn_ref[...]).astype(BF16)
        k_rope = pa_ref[:, 384:512] * cosv + pa_ref[:, 512:640] * sinv
        q_all, qs_all = _dot(cqn, wq_ref[...], NT), _dot(cqn, wqs_ref[...], NT)
        k_all = _dot(ckvn, wk_ref[...], NT)
        for h in range(N_HEADS):
            lanes = slice(h * HEAD_PAD, (h + 1) * HEAD_PAD)
            q_ref[h] = (q_all[:, lanes] * cosv + qs_all[:, lanes] * sinv).astype(BF16)
            k_ref[h] = (k_all[:, lanes] + k_rope).astype(BF16)
        v_ref[...] = _dot(ckvn, wv_ref[...], NT).astype(BF16)

    head_blk = pl.BlockSpec((N_HEADS, t, HEAD_PAD), lambda i: (0, i, 0))
    return _pcall(
        body, name="attn_prep_fwd", grid=(s // t,),
        in_specs=[_rows(t, PA_W), _rows(t, LANES), _rows(t, LANES), _whole(qn.shape), _whole(kvn.shape),
                  _whole(wq.shape), _whole(wqs.shape), _whole(wk.shape), _whole(wv.shape)],
        out_specs=[head_blk, head_blk, _rows(t, BR_W)],
        out_shape=[_sds((N_HEADS, s, HEAD_PAD), BF16), _sds((N_HEADS, s, HEAD_PAD), BF16), _sds((s, BR_W), BF16)],
        vmem=VMEM_LIMIT)(pa, cosf, sins, qn, kvn, wq, wqs, wk, wv)


def _causal_scores(q, k, masked):
    sc = _dot(q, k, NT) * SM_SCALE
    if masked:
        row = lax.broadcasted_iota(jnp.int32, sc.shape, 0)
        col = lax.broadcasted_iota(jnp.int32, sc.shape, 1)
        sc = jnp.where(col <= row, sc, -jnp.inf)
    return sc


def _causal_steps(n, key_major):
    if key_major:
        pairs = [(qi, kj) for kj in range(n) for qi in range(kj, n)]
    else:
        pairs = [(qi, kj) for qi in range(n) for kj in range(qi + 1)]
    return (jnp.asarray([p[0] for p in pairs], jnp.int32), jnp.asarray([p[1] for p in pairs], jnp.int32))


def _flash_fwd(q, k, v, t, riding):
    s = v.shape[0]
    n = s // t
    reps = t // LANES
    qi_tab, kj_tab = _causal_steps(n, key_major=False)
    n_steps = int(qi_tab.shape[0])

    def body(qi_ref, kj_ref, q_ref, k_ref, v_ref, x_ref, o_ref, lse_ref, land_ref, m_scr, l_scr, acc_scr,
             send_sems, recv_sems, local_sem):
        step_id = pl.program_id(0)
        qi, kj = qi_ref[step_id], kj_ref[step_id]
        at_end = _ride_along(riding, step_id, n_steps, (x_ref, land_ref, send_sems, recv_sems, local_sem))

        @pl.when(kj == 0)
        def _():
            m_scr[...] = jnp.full(m_scr.shape, -jnp.inf, F32)
            l_scr[...] = jnp.zeros(l_scr.shape, F32)
            acc_scr[...] = jnp.zeros(acc_scr.shape, F32)

        def step(masked):
            for h in range(N_HEADS):
                sc = _causal_scores(q_ref[h], k_ref[h], masked)
                m_prev = m_scr[h]
                m_next = jnp.maximum(m_prev, jnp.max(sc, axis=1, keepdims=True))
                alpha = jnp.exp(m_prev - m_next)
                p = jnp.exp(sc - jnp.tile(m_next, (1, reps)))
                l_scr[h] = alpha * l_scr[h] + jnp.sum(p, axis=1, keepdims=True)
                m_scr[h] = m_next
                pair = (h // 2) * LANES
                acc_scr[h] = acc_scr[h] * alpha + _dot(p.astype(BF16), v_ref[:, pair:pair + LANES])

        @pl.when(kj < qi)
        def _():
            step(False)

        @pl.when(kj == qi)
        def _():
            step(True)
            lane = lax.broadcasted_iota(jnp.int32, (t, LANES), 1)
            for pr in range(N_HEADS // 2):
                o0 = acc_scr[2 * pr] / l_scr[2 * pr]
                o1 = acc_scr[2 * pr + 1] / l_scr[2 * pr + 1]
                o_ref[:, pr * LANES:(pr + 1) * LANES] = jnp.where(lane < D_V, o0, o1).astype(BF16)
            for h in range(N_HEADS):
                lse_ref[h] = m_scr[h] + jnp.log(l_scr[h])

        at_end()

    q_blk = pl.BlockSpec((N_HEADS, t, HEAD_PAD), lambda i, qi, kj: (0, qi[i], 0))
    k_blk = pl.BlockSpec((N_HEADS, t, HEAD_PAD), lambda i, qi, kj: (0, kj[i], 0))
    v_blk = pl.BlockSpec((t, BR_W), lambda i, qi, kj: (kj[i], 0))
    return _pcall(
        body, name="flash_fwd_gathering", grid=(n_steps,), prefetch=2,
        in_specs=[q_blk, k_blk, v_blk, HBM_REF],
        out_specs=[pl.BlockSpec((t, BR_W), lambda i, qi, kj: (qi[i], 0)), q_blk, HBM_REF],
        out_shape=[_sds((s, BR_W), BF16), _sds((N_HEADS, s, LANES), F32), riding[2]],
        scratch=[pltpu.VMEM((N_HEADS, t, LANES), F32)] * 3 + list(EXCHANGE_SEMS),
        vmem=VMEM_LIMIT)(qi_tab, kj_tab, q, k, v, riding[1])


def _shift_down(ext, k, t):
    return pltpu.roll(ext, k, 0)[HALO:HALO + t]


def _shift_up(ext, k, t):
    return pltpu.roll(ext, t + HALO - k, 0)[0:t]


def _lane_group(shape):
    return lax.shift_right_logical(lax.broadcasted_iota(jnp.int32, shape, 1), 6)


def _group_select(vals):
    grp = _lane_group(vals[0].shape)
    out = vals[0]
    for g in range(1, len(vals)):
        out = jnp.where(grp == g, vals[g], out)
    return out


def _layernorm(x, g, b):
    mu = jnp.mean(x, axis=-1, keepdims=True)
    xc = x - mu
    return xc * lax.rsqrt(jnp.mean(xc * xc, axis=-1, keepdims=True) + EPS) * g + b


def _sg_mix(wm_ref, vnb, bias):
    return _group_select([_dot(wm_ref[g], vnb) for g in range(SG_GROUPS)]) + bias


def _pool_windows(ext):
    s2 = ext + pltpu.roll(ext, 1, 0)
    s4 = s2 + pltpu.roll(s2, 2, 0)
    s8 = s4 + pltpu.roll(s4, 4, 0)
    s16 = s8 + pltpu.roll(s8, 8, 0)
    return [s2, s4, s8, s16]


def _pool_counts(tok):
    return [jnp.minimum(tok + 1, w).astype(F32) for w in POOL_WINDOWS]


def _halo_before(t):
    return pl.BlockSpec((HALO, PM_W), lambda i: (jnp.maximum(i * (t // HALO) - 1, 0), 0))


def _mixers_fwd(pm, ln_g, ln_b, wm, sgb, conv_w, wp, pscale, t):
    s = pm.shape[0]

    def body(pm_ref, halo_ref, lng_ref, lnb_ref, wm_ref, sgb_ref, cw_ref, wp_ref, ps_ref, bsg_ref, bcv_ref, bpl_ref):
        i = pl.program_id(0)
        halo = jnp.where(i > 0, halo_ref[...], 0.0)
        u = _gelu(pm_ref[:, 0:256])
        vnb = _layernorm(_gelu(pm_ref[:, 256:512]), lng_ref[...], lnb_ref[...]).astype(BF16)
        for c in range(t // SG_CHUNK):
            rows = slice(c * SG_CHUNK, (c + 1) * SG_CHUNK)
            bsg_ref[rows, :] = (u[rows] * _sg_mix(wm_ref, vnb[rows], sgb_ref[...])).astype(BF16)
        z = pm_ref[:, 1024:1280] * pm_ref[:, 512:768]
        zext = jnp.concatenate([halo[:, 1024:1280] * halo[:, 512:768], z], axis=0)
        y = cw_ref[0:1, :] * _shift_down(zext, 2, t) + cw_ref[1:2, :] * _shift_down(zext, 1, t) + cw_ref[2:3, :] * z
        bcv_ref[...] = (pm_ref[:, 768:1024] * y).astype(BF16)
        p = pm_ref[:, 1280:1536]
        sums = _pool_windows(jnp.concatenate([halo[:, 1280:1536], p], axis=0))
        tok = i * t + lax.broadcasted_iota(jnp.int32, (t, 1), 0)
        pooled = _group_select([sw[HALO:HALO + t] / cnt - p for sw, cnt in zip(sums, _pool_counts(tok))])
        bpl_ref[...] = (_dot(pooled.astype(BF16), wp_ref[...]) * ps_ref[...]).astype(BF16)

    return _pcall(
        body, name="mixers_fwd", grid=(s // t,),
        in_specs=[_rows(t, PM_W), _halo_before(t), _whole(ln_g.shape), _whole(ln_b.shape), _whole(wm.shape),
                  _whole(sgb.shape), _whole(conv_w.shape), _whole(wp.shape), _whole(pscale.shape)],
        out_specs=[_rows(t, BR_W)] * 3,
        out_shape=[_sds((s, BR_W), BF16)] * 3,
        vmem=VMEM_LIMIT)(pm, pm, ln_g, ln_b, wm, sgb, conv_w, wp, pscale)


def _merge_fwd(x, hb, branches, wg, bg, wbr, wout, g2, t):
    s = x.shape[0]

    def body(x_ref, hb_ref, b0_ref, b1_ref, b2_ref, b3_ref, wg_ref, bg_ref, wbr_ref, wout_ref, g2_ref,
             xmid_ref, mrg_ref, t_ref):
        hb = hb_ref[...]
        merged = jnp.zeros((t, D_MODEL), F32)
        for k, b_ref in enumerate((b0_ref, b1_ref, b2_ref, b3_ref)):
            cols = slice(k * D_MODEL, (k + 1) * D_MODEL)
            gate = _sigmoid(_dot(hb, wg_ref[cols, :], NT) + bg_ref[:, cols])
            merged = merged + gate * _dot(b_ref[...], wbr_ref[k], NT)
        mb = merged.astype(BF16)
        mrg_ref[...] = mb
        tt = _dot(mb, wout_ref[...])
        t_ref[...] = tt
        xmid_ref[...] = x_ref[...] + _rms(tt, g2_ref[...])

    return _pcall(
        body, name="merge_fwd", grid=(s // t,),
        in_specs=[_rows(t, D_MODEL), _rows(t, D_MODEL)] + [_rows(t, BR_W)] * 4 +
                 [_whole(wg.shape), _whole(bg.shape), _whole(wbr.shape), _whole(wout.shape), _whole(g2.shape)],
        out_specs=[_rows(t, D_MODEL)] * 3,
        out_shape=[_sds((s, D_MODEL), F32), _sds((s, D_MODEL), BF16), _sds((s, D_MODEL), F32)],
        vmem=VMEM_LIMIT)(x, hb, *branches, wg, bg, wbr, wout, g2)


def _ffn_fwd(x, g3, w1, w2, g4, t):
    s = x.shape[0]

    def body(x_ref, g3_ref, w1_ref, w2_ref, g4_ref, xo_ref, f_ref):
        h = _rms(x_ref[...], g3_ref[...]).astype(BF16)
        f = jnp.zeros((t, D_MODEL), F32)
        for j in range(D_FF // D_MODEL):
            cols = slice(j * D_MODEL, (j + 1) * D_MODEL)
            r = jnp.square(jnp.maximum(_dot(h, w1_ref[cols, :], NT), 0.0)).astype(BF16)
            f = f + _dot(r, w2_ref[cols, :])
        f_ref[...] = f
        xo_ref[...] = x_ref[...] + _rms(f, g4_ref[...])

    return _pcall(
        body, name="ffn_fwd", grid=(s // t,),
        in_specs=[_rows(t, D_MODEL), _whole(g3.shape), _whole(w1.shape), _whole(w2.shape), _whole(g4.shape)],
        out_specs=[_rows(t, D_MODEL)] * 2,
        out_shape=[_sds((s, D_MODEL), F32)] * 2,
        vmem=VMEM_LIMIT)(x, g3, w1, w2, g4)


def _loss_head(y, target, t):
    s = y.shape[0]
    n = s // t

    def body(y_ref, tg_ref, dy_ref, loss_ref, acc_scr):
        i = pl.program_id(0)
        d = y_ref[...] - tg_ref[...]
        dy_ref[...] = d * (1.0 / D_MODEL)
        _accumulate(acc_scr, jnp.sum(d * d, axis=0, keepdims=True), i == 0)

        @pl.when(i == n - 1)
        def _():
            loss_ref[...] = jnp.full(loss_ref.shape, 0.5 / D_MODEL, F32) * jnp.sum(acc_scr[...])

    return _pcall(
        body, name="loss_head", grid=(n,),
        in_specs=[_rows(t, D_MODEL)] * 2,
        out_specs=[_rows(t, D_MODEL), _acc((8, LANES))],
        out_shape=[_sds((s, D_MODEL), F32), _sds((8, LANES), F32)],
        scratch=[pltpu.VMEM((1, D_MODEL), F32)])(y, target)


def _ffn_bwd_norms(x_mid, f, dxo, g3, g4, t):
    s = x_mid.shape[0]

    def body(x_ref, f_ref, dxo_ref, g3_ref, g4_ref, h2_ref, df_ref, dg4_ref):
        h2_ref[...] = _rms(x_ref[...], g3_ref[...]).astype(BF16)
        df, dg4 = _rms_bwd(f_ref[...], g4_ref[...], dxo_ref[...])
        df_ref[...] = df.astype(BF16)
        _accumulate(dg4_ref, dg4, pl.program_id(0) == 0)

    return _pcall(
        body, name="ffn_bwd_norms", grid=(s // t,),
        in_specs=[_rows(t, D_MODEL)] * 3 + [_whole(g3.shape), _whole(g4.shape)],
        out_specs=[_rows(t, D_MODEL), _rows(t, D_MODEL), _acc((1, D_MODEL))],
        out_shape=[_sds((s, D_MODEL), BF16), _sds((s, D_MODEL), BF16), _sds((1, D_MODEL), F32)])(x_mid, f, dxo, g3, g4)


def _ffn_bwd_weights(h2, df, w1, w2, t):
    s = h2.shape[0]
    blk = D_MODEL
    n = s // t

    def body(h2_ref, df_ref, w1_ref, w2_ref, da_ref, dw1_ref, dw2_ref, acc1, acc2):
        first, last = pl.program_id(1) == 0, pl.program_id(1) == n - 1
        h2v, dfv = h2_ref[...], df_ref[...]
        rl = jnp.maximum(_dot(h2v, w1_ref[...], NT), 0.0)
        _accumulate_out(acc2, dw2_ref, _dot((rl * rl).astype(BF16), dfv, TN), first, last)
        da = (_dot(dfv, w2_ref[...], NT) * (2.0 * rl)).astype(BF16)
        da_ref[...] = da
        _accumulate_out(acc1, dw1_ref, _dot(da, h2v, TN), first, last)

    tok = pl.BlockSpec((t, D_MODEL), lambda j, i: (i, 0))
    hid = pl.BlockSpec((blk, D_MODEL), lambda j, i: (j, 0))
    return _pcall(
        body, name="ffn_bwd_weights", grid=(D_FF // blk, n),
        in_specs=[tok, tok, hid, hid],
        out_specs=[pl.BlockSpec((t, blk), lambda j, i: (i, j)), hid, hid],
        out_shape=[_sds((s, D_FF), BF16), _sds((D_FF, D_MODEL), BF16), _sds((D_FF, D_MODEL), BF16)],
        scratch=[pltpu.VMEM((blk, D_MODEL), F32)] * 2,
        vmem=VMEM_LIMIT)(h2, df, w1, w2)


def _ffn_bwd_input(da, w1, x_mid, dxo, g3, t):
    s = x_mid.shape[0]

    def body(da_ref, w1_ref, x_ref, dxo_ref, g3_ref, dx_ref, dg3_ref):
        dx, dg3 = _rms_bwd(x_ref[...], g3_ref[...], _dot(da_ref[...], w1_ref[...]))
        dx_ref[...] = dxo_ref[...] + dx
        _accumulate(dg3_ref, dg3, pl.program_id(0) == 0)

    return _pcall(
        body, name="ffn_bwd_input", grid=(s // t,),
        in_specs=[_rows(t, D_FF), _whole(w1.shape), _rows(t, D_MODEL), _rows(t, D_MODEL), _whole(g3.shape)],
        out_specs=[_rows(t, D_MODEL), _acc((1, D_MODEL))],
        out_shape=[_sds((s, D_MODEL), F32), _sds((1, D_MODEL), F32)],
        vmem=VMEM_LIMIT)(da, w1, x_mid, dxo, g3)


def _merge_bwd_out(tt, dxmid, g2, wout, merged, t):
    s = tt.shape[0]
    n = s // t

    def body(t_ref, dx_ref, g2_ref, wout_ref, mrg_ref, dm_ref, dwout_ref, dg2_ref, acc):
        first, last = pl.program_id(0) == 0, pl.program_id(0) == n - 1
        dt, dg2 = _rms_bwd(t_ref[...], g2_ref[...], dx_ref[...])
        dtb = dt.astype(BF16)
        dm_ref[...] = _dot(dtb, wout_ref[...], NT)
        _accumulate_out(acc, dwout_ref, _dot(mrg_ref[...], dtb, TN), first, last)
        _accumulate(dg2_ref, dg2, first)

    return _pcall(
        body, name="merge_bwd_out", grid=(n,),
        in_specs=[_rows(t, D_MODEL), _rows(t, D_MODEL), _whole(g2.shape), _whole(wout.shape), _rows(t, D_MODEL)],
        out_specs=[_rows(t, D_MODEL), _acc((D_MODEL, D_MODEL)), _acc((1, D_MODEL))],
        out_shape=[_sds((s, D_MODEL), F32), _sds((D_MODEL, D_MODEL), BF16), _sds((1, D_MODEL), F32)],
        scratch=[pltpu.VMEM((D_MODEL, D_MODEL), F32)],
        vmem=VMEM_LIMIT)(tt, dxmid, g2, wout, merged)


def _merge_bwd_branch(k, hb, br, dm, wg, bg, wbr, t):
    s = hb.shape[0]
    n = s // t

    def body(hb_ref, br_ref, dm_ref, wg_ref, bg_ref, wbr_ref, dbr_ref, dpre_ref, dwg_ref, dbg_ref, dwbr_ref,
             acc_g, acc_br):
        first, last = pl.program_id(0) == 0, pl.program_id(0) == n - 1
        hbv, brv, dmv, wbrv = hb_ref[...], br_ref[...], dm_ref[...], wbr_ref[0]
        gate = _sigmoid(_dot(hbv, wg_ref[...], NT) + bg_ref[...])
        dy = (dmv * gate).astype(BF16)
        dpre = dmv * _dot(brv, wbrv, NT) * gate * (1.0 - gate)
        dpb = dpre.astype(BF16)
        dpre_ref[...] = dpb
        dbr_ref[...] = _dot(dy, wbrv)
        _accumulate_out(acc_br, dwbr_ref, _dot(dy, brv, TN), first, last)
        _accumulate_out(acc_g, dwg_ref, _dot(dpb, hbv, TN), first, last)
        _accumulate(dbg_ref, jnp.sum(dpre, axis=0, keepdims=True), first)

    return _pcall(
        body, name=f"merge_bwd_branch{k}", grid=(n,),
        in_specs=[_rows(t, D_MODEL), _rows(t, BR_W), _rows(t, D_MODEL),
                  pl.BlockSpec((D_MODEL, D_MODEL), lambda i: (k, 0)), pl.BlockSpec((1, D_MODEL), lambda i: (0, k)),
                  pl.BlockSpec((1, D_MODEL, BR_W), lambda i: (k, 0, 0))],
        out_specs=[_rows(t, BR_W), _rows(t, D_MODEL), _acc((D_MODEL, D_MODEL)), _acc((1, D_MODEL)), _acc((D_MODEL, BR_W))],
        out_shape=[_sds((s, BR_W), F32), _sds((s, D_MODEL), BF16), _sds((D_MODEL, D_MODEL), BF16),
                   _sds((1, D_MODEL), F32), _sds((D_MODEL, BR_W), BF16)],
        scratch=[pltpu.VMEM((D_MODEL, D_MODEL), F32), pltpu.VMEM((D_MODEL, BR_W), F32)],
        vmem=VMEM_LIMIT)(hb, br, dm, wg, bg, wbr)


def _mixers_bwd(pm, dbsg, dbcv, dbpl, ln_g, ln_b, wm, wmt, sgb, conv_w, wp, pscale, t):
    s = pm.shape[0]
    n = s // t
    nb = t // HALO

    def body(pm_ref, before_ref, after_ref, dsg_ref, dcv_ref, dcva_ref, dpl_ref, dpla_ref, lng_ref, lnb_ref,
             wm_ref, wmt_ref, sgb_ref, cw_ref, wp_ref, ps_ref,
             dpm_ref, dlng_ref, dlnb_ref, dwm_ref, dsgb_ref, dcw_ref, dwp_ref, dps_ref):
        i = pl.program_id(0)
        first = i == 0
        before = jnp.where(i > 0, before_ref[...], 0.0)
        after = jnp.where(i < n - 1, after_ref[...], 0.0)

        u_raw, v_raw = pm_ref[:, 0:256], pm_ref[:, 256:512]
        lng, lnb = lng_ref[...], lnb_ref[...]
        u, gelu_u_vjp = jax.vjp(_gelu, u_raw)
        vn, norm_vjp = jax.vjp(lambda v_, g_, b_: _layernorm(_gelu(v_), g_, b_), v_raw, lng, lnb)
        vnb = vn.astype(BF16)
        dsg = dsg_ref[...]
        grp = _lane_group((SG_CHUNK, BR_W))
        tri = (lax.broadcasted_iota(jnp.int32, (SG_CHUNK, SG_CHUNK), 1)
               <= lax.broadcasted_iota(jnp.int32, (SG_CHUNK, SG_CHUNK), 0))
        du_parts, dvn_parts = [], []
        dsgb = jnp.zeros((SG_CHUNK, BR_W), F32)
        dwm = [jnp.zeros((SG_CHUNK, SG_CHUNK), F32) for _ in range(SG_GROUPS)]
        for c in range(t // SG_CHUNK):
            rows = slice(c * SG_CHUNK, (c + 1) * SG_CHUNK)
            mix = _sg_mix(wm_ref, vnb[rows], sgb_ref[...])
            du_parts.append(dsg[rows] * mix)
            ds = dsg[rows] * u[rows]
            dsgb = dsgb + ds
            dsb = [jnp.where(grp == g, ds, 0.0).astype(BF16) for g in range(SG_GROUPS)]
            for g in range(SG_GROUPS):
                dwm[g] = dwm[g] + _dot(dsb[g], vnb[rows], NT)
            dvn_parts.append(_group_select([_dot(wmt_ref[g], dsb[g]) for g in range(SG_GROUPS)]))
        (du_raw,) = gelu_u_vjp(jnp.concatenate(du_parts, axis=0))
        dv_raw, dlng, dlnb = norm_vjp(jnp.concatenate(dvn_parts, axis=0))
        dpm_ref[:, 0:256] = du_raw.astype(BF16)
        dpm_ref[:, 256:512] = dv_raw.astype(BF16)
        _accumulate(dlng_ref, dlng, first)
        _accumulate(dlnb_ref, dlnb, first)
        _accumulate(dsgb_ref, dsgb, first)
        for g in range(SG_GROUPS):
            _accumulate(dwm_ref.at[g], jnp.where(tri, dwm[g], 0.0), first)

        xin, bg, cg = pm_ref[:, 512:768], pm_ref[:, 768:1024], pm_ref[:, 1024:1280]
        z = cg * xin
        zext = jnp.concatenate([before[:, 1024:1280] * before[:, 512:768], z], axis=0)
        z1, z2 = _shift_down(zext, 1, t), _shift_down(zext, 2, t)
        w0, w1, w2 = cw_ref[0:1, :], cw_ref[1:2, :], cw_ref[2:3, :]
        dcv = dcv_ref[...]
        y = w0 * z2 + w1 * z1 + w2 * z
        dy = dcv * bg
        dyext = jnp.concatenate([dy, jnp.where(i < n - 1, dcva_ref[...], 0.0) * after[:, 768:1024]], axis=0)
        dz = w2 * dy + w1 * _shift_up(dyext, 1, t) + w0 * _shift_up(dyext, 2, t)
        dpm_ref[:, 512:768] = (dz * cg).astype(BF16)
        dpm_ref[:, 768:1024] = (dcv * y).astype(BF16)
        dpm_ref[:, 1024:1280] = (dz * xin).astype(BF16)
        dcw = jnp.concatenate([jnp.sum(dy * z2, axis=0, keepdims=True), jnp.sum(dy * z1, axis=0, keepdims=True),
                               jnp.sum(dy * z, axis=0, keepdims=True)], axis=0)
        _accumulate(dcw_ref, jnp.concatenate([dcw, jnp.zeros((8 - CONV_K, BR_W), F32)], axis=0), first)

        p = pm_ref[:, 1280:1536]
        tok = i * t + lax.broadcasted_iota(jnp.int32, (t, 1), 0)
        sums = _pool_windows(jnp.concatenate([before[:, 1280:1536], p], axis=0))
        pooled = _group_select([sw[HALO:HALO + t] / cnt - p for sw, cnt in zip(sums, _pool_counts(tok))]).astype(BF16)
        mixed = _dot(pooled, wp_ref[...])
        dpl = dpl_ref[...]
        ps = ps_ref[...]
        dmix = (dpl * ps).astype(BF16)
        _accumulate(dps_ref, jnp.sum(dpl * mixed, axis=0, keepdims=True), first)
        _accumulate(dwp_ref, _dot(pooled, dmix, TN), first)
        dmix_after = (jnp.where(i < n - 1, dpla_ref[...], 0.0) * ps).astype(BF16)
        dpo = _dot(dmix, wp_ref[...], NT)
        dpo_ext = jnp.concatenate([dpo, _dot(dmix_after, wp_ref[...], NT)], axis=0)
        tok_ext = i * t + lax.broadcasted_iota(jnp.int32, (t + HALO, 1), 0)
        dp_groups = []
        for g, (win, cnt) in enumerate(zip(POOL_WINDOWS, _pool_counts(tok_ext))):
            e = dpo_ext / cnt
            acc = e
            span = 1
            while span < win:
                acc = acc + pltpu.roll(acc, t + HALO - span, 0)
                span *= 2
            dp_groups.append(acc[0:t] - dpo)
        dpm_ref[:, 1280:1536] = _group_select(dp_groups).astype(BF16)

    row_blk = lambda w: pl.BlockSpec((t, w), lambda i: (i, 0))
    after_blk = lambda w: pl.BlockSpec((HALO, w), lambda i: (jnp.minimum((i + 1) * nb, n * nb - 1), 0))
    return _pcall(
        body, name="mixers_bwd", grid=(n,),
        in_specs=[row_blk(PM_W), _halo_before(t), after_blk(PM_W), row_blk(BR_W), row_blk(BR_W), after_blk(BR_W),
                  row_blk(BR_W), after_blk(BR_W), _whole(ln_g.shape), _whole(ln_b.shape), _whole(wm.shape),
                  _whole(wmt.shape), _whole(sgb.shape), _whole(conv_w.shape), _whole(wp.shape), _whole(pscale.shape)],
        out_specs=[row_blk(PM_W), _acc((1, BR_W)), _acc((1, BR_W)), _acc((SG_GROUPS, SG_CHUNK, SG_CHUNK)),
                   _acc((SG_CHUNK, BR_W)), _acc((8, BR_W)), _acc((BR_W, BR_W)), _acc((1, BR_W))],
        out_shape=[_sds((s, PM_W), BF16), _sds((1, BR_W), F32), _sds((1, BR_W), F32),
                   _sds((SG_GROUPS, SG_CHUNK, SG_CHUNK), F32), _sds((SG_CHUNK, BR_W), F32), _sds((8, BR_W), F32),
                   _sds((BR_W, BR_W), F32), _sds((1, BR_W), F32)],
        vmem=VMEM_LIMIT)(pm, pm, pm, dbsg, dbcv, dbcv, dbpl, dbpl, ln_g, ln_b, wm, wmt, sgb, conv_w, wp, pscale)


def _head_delta(o_ref, do_ref, t):
    prod = o_ref[...].astype(F32) * do_ref[...]
    lane = lax.broadcasted_iota(jnp.int32, (t, LANES), 1)
    out = []
    for h in range(N_HEADS):
        pair = prod[:, (h // 2) * LANES:(h // 2 + 1) * LANES]
        mine = (lane < D_V) if h % 2 == 0 else (lane >= D_V)
        out.append(jnp.broadcast_to(jnp.sum(jnp.where(mine, pair, 0.0), axis=1, keepdims=True), (t, LANES)))
    return out


def _head_do(do_ref, h, t):
    lane = lax.broadcasted_iota(jnp.int32, (t, LANES), 1)
    mine = (lane < D_V) if h % 2 == 0 else (lane >= D_V)
    pair = do_ref[:, (h // 2) * LANES:(h // 2 + 1) * LANES]
    return jnp.where(mine, pair, 0.0).astype(BF16)


def _flash_bwd(q, k, v, o, do, lse, t, riding):
    s = v.shape[0]
    n = s // t
    reps = t // LANES

    qi_tab, kj_tab = _causal_steps(n, key_major=True)
    n_steps = int(qi_tab.shape[0])

    def body(qi_ref, kj_ref, q_ref, k_ref, v_ref, o_ref, do_ref, lse_ref, x_ref, dq_ref, dk_ref, dv_ref, land_ref,
             dv_scr, send_sems, recv_sems, local_sem):
        step_id = pl.program_id(0)
        qi, kj = qi_ref[step_id], kj_ref[step_id]
        at_end = _ride_along(riding, step_id, n_steps, (x_ref, land_ref, send_sems, recv_sems, local_sem))

        @pl.when(step_id == 0)
        def _():
            dq_ref[...] = jnp.zeros(dq_ref.shape, F32)

        @pl.when(qi == kj)
        def _():
            dk_ref[...] = jnp.zeros(dk_ref.shape, F32)
            dv_scr[...] = jnp.zeros(dv_scr.shape, F32)

        def step(masked):
            delta = _head_delta(o_ref, do_ref, t)
            q_rows = pl.ds(pl.multiple_of(qi * t, t), t)
            for h in range(N_HEADS):
                qh, kh = q_ref[h], k_ref[h]
                p = jnp.exp(_causal_scores(qh, kh, masked) - jnp.tile(lse_ref[h], (1, reps)))
                pair = slice((h // 2) * LANES, (h // 2 + 1) * LANES)
                dv_scr[h] += _dot(p.astype(BF16), do_ref[:, pair].astype(BF16), TN)
                dp = _dot(_head_do(do_ref, h, t), v_ref[:, pair], NT)
                ds = (p * (dp - jnp.tile(delta[h], (1, reps))) * SM_SCALE).astype(BF16)
                dk_ref[h] += _dot(ds, qh, TN)
                dq_ref[h, q_rows, :] += _dot(ds, kh)

        @pl.when(qi > kj)
        def _():
            step(False)

        @pl.when(qi == kj)
        def _():
            step(True)

        @pl.when(qi == n - 1)
        def _():
            lane = lax.broadcasted_iota(jnp.int32, (t, LANES), 1)
            for pr in range(N_HEADS // 2):
                dv_ref[:, pr * LANES:(pr + 1) * LANES] = jnp.where(lane < D_V, dv_scr[2 * pr], dv_scr[2 * pr + 1]).astype(BF16)

        at_end()

    q_blk = pl.BlockSpec((N_HEADS, t, HEAD_PAD), lambda i, qi, kj: (0, qi[i], 0))
    k_blk = pl.BlockSpec((N_HEADS, t, HEAD_PAD), lambda i, qi, kj: (0, kj[i], 0))
    v_blk = pl.BlockSpec((t, BR_W), lambda i, qi, kj: (kj[i], 0))
    o_blk = pl.BlockSpec((t, BR_W), lambda i, qi, kj: (qi[i], 0))
    dq_sds = _sds((N_HEADS, s, HEAD_PAD), F32)
    return _pcall(
        body, name="flash_bwd_exchanging", grid=(n_steps,), prefetch=2,
        in_specs=[q_blk, k_blk, v_blk, o_blk, o_blk, q_blk, HBM_REF],
        out_specs=[_whole(dq_sds.shape), k_blk, v_blk, HBM_REF],
        out_shape=[dq_sds, dq_sds, _sds((s, BR_W), BF16), riding[2]],
        scratch=[pltpu.VMEM((N_HEADS, t, LANES), F32)] + list(EXCHANGE_SEMS),
        vmem=VMEM_LIMIT)(qi_tab, kj_tab, q, k, v, o, do, lse, riding[1])


def _attn_prep_bwd(pa, dq, dk, dv, cosf, sins, qn, kvn, wq, wqs, wk, wv, t):
    s = pa.shape[0]

    def body(pa_ref, dq_ref, dk_ref, dv_ref, cos_ref, sin_ref, qn_ref, kvn_ref, wq_ref, wqs_ref, wk_ref, wv_ref,
             dpa_ref, dwq_ref, dwqs_ref, dwk_ref, dwv_ref, dqn_ref, dkvn_ref):
        first = pl.program_id(0) == 0
        cosv, sinv = cos_ref[...], sin_ref[...]
        cq, ckv = pa_ref[:, 0:Q_RANK], pa_ref[:, Q_RANK:Q_RANK + KV_RANK]
        cqn = _rms(cq, qn_ref[...]).astype(BF16)
        ckvn = _rms(ckv, kvn_ref[...]).astype(BF16)
        dvv = dv_ref[...]
        _accumulate(dwv_ref, _dot(dvv, ckvn, TN), first)
        dk_rope = dk_ref[0]
        for h in range(1, N_HEADS):
            dk_rope = dk_rope + dk_ref[h]
        dqa = jnp.concatenate([(dq_ref[h] * cosv).astype(BF16) for h in range(N_HEADS)], axis=1)
        dqs = jnp.concatenate([(dq_ref[h] * sinv).astype(BF16) for h in range(N_HEADS)], axis=1)
        dkb = jnp.concatenate([dk_ref[h].astype(BF16) for h in range(N_HEADS)], axis=1)
        _accumulate(dwq_ref, _dot(dqa, cqn, TN), first)
        _accumulate(dwqs_ref, _dot(dqs, cqn, TN), first)
        _accumulate(dwk_ref, _dot(dkb, ckvn, TN), first)
        dcqn = _dot(dqa, wq_ref[...]) + _dot(dqs, wqs_ref[...])
        dckvn = _dot(dvv, wv_ref[...]) + _dot(dkb, wk_ref[...])
        dcq, dqn = _rms_bwd(cq, qn_ref[...], dcqn)
        dckv, dkvn = _rms_bwd(ckv, kvn_ref[...], dckvn)
        _accumulate(dqn_ref, dqn, first)
        _accumulate(dkvn_ref, dkvn, first)
        dpa_ref[:, 0:Q_RANK] = dcq.astype(BF16)
        dpa_ref[:, Q_RANK:Q_RANK + KV_RANK] = dckv.astype(BF16)
        dpa_ref[:, 384:512] = (dk_rope * cosv).astype(BF16)
        dpa_ref[:, 512:640] = (dk_rope * sinv).astype(BF16)

    head_blk = pl.BlockSpec((N_HEADS, t, HEAD_PAD), lambda i: (0, i, 0))
    return _pcall(
        body, name="attn_prep_bwd", grid=(s // t,),
        in_specs=[_rows(t, PA_W), head_blk, head_blk, _rows(t, BR_W), _rows(t, LANES), _rows(t, LANES),
                  _whole(qn.shape), _whole(kvn.shape), _whole(wq.shape), _whole(wqs.shape), _whole(wk.shape), _whole(wv.shape)],
        out_specs=[_rows(t, PA_W), _acc(wq.shape), _acc(wqs.shape), _acc(wk.shape), _acc(wv.shape),
                   _acc(qn.shape), _acc(kvn.shape)],
        out_shape=[_sds((s, PA_W), BF16), _sds(wq.shape, F32), _sds(wqs.shape, F32), _sds(wk.shape, F32),
                   _sds(wv.shape, F32), _sds(qn.shape, F32), _sds(kvn.shape, F32)],
        vmem=VMEM_LIMIT)(pa, dq, dk, dv, cosf, sins, qn, kvn, wq, wqs, wk, wv)


def _inproj_bwd(x, g1, dxmid, dpa, dpm, dpres, w_in, wg, t):
    s = x.shape[0]
    n = s // t

    def body(x_ref, g1_ref, dxm_ref, dpa_ref, dpm_ref, d0_ref, d1_ref, d2_ref, d3_ref, w_ref, wg_ref,
             dx_ref, dg1_ref, dwa_ref, dwm_ref, acc_a, acc_m):
        first, last = pl.program_id(0) == 0, pl.program_id(0) == n - 1
        xv, g1v = x_ref[...], g1_ref[...]
        hb = _rms(xv, g1v).astype(BF16)
        dpav, dpmv = dpa_ref[...], dpm_ref[...]
        dh = _dot(dpav, w_ref[:PA_W, :]) + _dot(dpmv, w_ref[PA_W:, :])
        for k, d_ref in enumerate((d0_ref, d1_ref, d2_ref, d3_ref)):
            dh = dh + _dot(d_ref[...], wg_ref[k * D_MODEL:(k + 1) * D_MODEL, :])
        dx, dg1 = _rms_bwd(xv, g1v, dh)
        dx_ref[...] = dxm_ref[...] + dx
        _accumulate(dg1_ref, dg1, first)
        _accumulate_out(acc_a, dwa_ref, _dot(dpav, hb, TN), first, last)
        _accumulate_out(acc_m, dwm_ref, _dot(dpmv, hb, TN), first, last)

    return _pcall(
        body, name="inproj_bwd", grid=(n,),
        in_specs=[_rows(t, D_MODEL), _whole(g1.shape), _rows(t, D_MODEL), _rows(t, PA_W), _rows(t, PM_W)] +
                 [_rows(t, D_MODEL)] * 4 + [_whole(w_in.shape), _whole(wg.shape)],
        out_specs=[_rows(t, D_MODEL), _acc((1, D_MODEL)), _acc((PA_W, D_MODEL)), _acc((PM_W, D_MODEL))],
        out_shape=[_sds((s, D_MODEL), F32), _sds((1, D_MODEL), F32), _sds((PA_W, D_MODEL), BF16), _sds((PM_W, D_MODEL), BF16)],
        scratch=[pltpu.VMEM((PA_W, D_MODEL), F32), pltpu.VMEM((PM_W, D_MODEL), F32)],
        vmem=VMEM_LIMIT)(x, g1, dxmid, dpa, dpm, *dpres, w_in, wg)


def _my_place():
    return lax.axis_index("x"), lax.axis_index("y"), lax.axis_index("c")


def _flip(place, k):
    x, y, c = place
    return (1 - x if k & 4 else x, 1 - y if k & 2 else y, 1 - c if k & 1 else c)


def _rank(place):
    x, y, c = place
    return 4 * x + 2 * y + c


EXCHANGE_SEMS = (pltpu.SemaphoreType.DMA((7,)), pltpu.SemaphoreType.DMA((7,)), pltpu.SemaphoreType.DMA)
HBM_REF = pl.BlockSpec(memory_space=pl.ANY)


def _gather_plan(x_ref, out_ref, send_sems, recv_sems, local_sem):
    me = _my_place()
    sibling = _flip(me, 1)
    chips = (4, 2, 6)

    def copy(k, src_place, to, src=None):
        slot = out_ref.at[_rank(src_place)]
        return pltpu.make_async_remote_copy(
            src_ref=slot if src is None else src, dst_ref=slot, send_sem=send_sems.at[k], recv_sem=recv_sems.at[k],
            device_id=to, device_id_type=MESH_ID)

    mine = pltpu.make_async_copy(x_ref, out_ref.at[_rank(me)], local_sem)
    first = [copy(0, me, sibling, src=x_ref)] + [copy(1 + j, me, _flip(me, kc), src=x_ref) for j, kc in enumerate(chips)]
    passed = [copy(4 + j, _flip(me, kc), sibling) for j, kc in enumerate(chips)]

    def start():
        mine.start()
        for cp in first:
            cp.start()

    def forward():
        for j, kc in enumerate(chips):
            copy(1 + j, _flip(me, kc), me).wait_recv()
            passed[j].start()

    def finish():
        copy(0, sibling, me).wait_recv()
        for j, kc in enumerate(chips):
            copy(4 + j, _flip(sibling, kc), me).wait_recv()
        for cp in first + passed:
            cp.wait_send()
        mine.wait()

    return start, forward, finish


def _exchange_plan(x_ref, out_ref, send_sems, recv_sems, local_sem):
    me = _my_place()
    my_rank = _rank(me)
    mine = pltpu.make_async_copy(x_ref.at[my_rank], out_ref.at[my_rank], local_sem)
    sends = [pltpu.make_async_remote_copy(
        src_ref=x_ref.at[_rank(_flip(me, k))], dst_ref=out_ref.at[my_rank], send_sem=send_sems.at[k - 1],
        recv_sem=recv_sems.at[k - 1], device_id=_flip(me, k), device_id_type=MESH_ID) for k in range(1, N_DEV)]

    def start():
        mine.start()
        for cp in sends:
            cp.start()

    def forward():
        pass

    def finish():
        for k in range(1, N_DEV):
            slot = out_ref.at[_rank(_flip(me, k))]
            pltpu.make_async_remote_copy(
                src_ref=slot, dst_ref=slot, send_sem=send_sems.at[k - 1], recv_sem=recv_sems.at[k - 1],
                device_id=_flip(me, k), device_id_type=MESH_ID).wait_recv()
        for cp in sends:
            cp.wait_send()
        mine.wait()

    return start, forward, finish


def _exchange_alone(plan, src, out_sds, name):
    def body(x_ref, out_ref, send_sems, recv_sems, local_sem):
        start, forward, finish = plan(x_ref, out_ref, send_sems, recv_sems, local_sem)
        start()
        forward()
        finish()

    return pl.pallas_call(body, name=name, out_shape=out_sds, in_specs=[HBM_REF], out_specs=HBM_REF,
                          scratch_shapes=list(EXCHANGE_SEMS))(src)


def _all_gather(shard, name):
    return _exchange_alone(_gather_plan, shard, _sds((N_DEV,) + shard.shape, shard.dtype), name)


def _all_to_all(blocks, name):
    return _exchange_alone(_exchange_plan, blocks, _sds(blocks.shape, blocks.dtype), name)


def _riding(plan, src, out_sds):
    return (plan, src, out_sds)


def _ride_along(riding, step_id, n_steps, refs):
    start, forward, finish = riding[0](*refs)

    @pl.when(step_id == 0)
    def _():
        start()

    @pl.when(step_id == (3 * n_steps) // 4)
    def _():
        forward()

    def at_end():
        @pl.when(step_id == n_steps - 1)
        def _():
            finish()

    return at_end


def _sum_slots(parts, name):
    _, r, c = parts.shape
    t = math.gcd(r, ROW_PAD)

    def body(p_ref, o_ref):
        acc = p_ref[0].astype(F32)
        for d in range(1, N_DEV):
            acc = acc + p_ref[d].astype(F32)
        o_ref[...] = acc

    return _pcall(
        body, name=name, grid=(r // t,),
        in_specs=[pl.BlockSpec((N_DEV, t, c), lambda i: (0, i, 0))], out_specs=_rows(t, c),
        out_shape=_sds((r, c), F32))(parts)


def _adamw(w, g, m, v, name):
    r, c = w.shape
    t = r if r <= ROW_PAD else math.gcd(r, ROW_PAD)

    def body(w_ref, g_ref, m_ref, v_ref, d_ref, nm_ref, nv_ref):
        gv = g_ref[...]
        nm = ADAM_B1 * m_ref[...] + (1.0 - ADAM_B1) * gv
        nv = ADAM_B2 * v_ref[...] + (1.0 - ADAM_B2) * jnp.square(gv)
        m_hat = nm / (1.0 - ADAM_B1 ** ADAM_STEP)
        v_hat = nv / (1.0 - ADAM_B2 ** ADAM_STEP)
        d_ref[...] = -ADAM_LR * (m_hat / (jnp.sqrt(v_hat) + ADAM_EPS) + ADAM_WD * w_ref[...])
        nm_ref[...] = nm
        nv_ref[...] = nv

    return _pcall(
        body, name=name, grid=(r // t,),
        in_specs=[_rows(t, c)] * 4, out_specs=[_rows(t, c)] * 3, out_shape=[_sds((r, c), F32)] * 3)(w, g, m, v)


BEFORE_ATTN = ("w_in", "w_uq", "w_ukv")
AFTER_ATTN = ("w_br_mla", "w_br_sg", "w_br_conv", "w_br_pool", "w_out", "w_ff1", "w_ff2", "conv_w")
SHARDED = BEFORE_ATTN + AFTER_ATTN
ROW_SHARDED = ("w_out", "w_ff2")
REPLICATED = ("norm_mix_pre", "gate_b", "q_norm", "kv_norm", "sg_ln_g", "sg_ln_b", "sg_w", "sg_b", "pool_w",
              "pool_scale", "norm_mix_post", "norm_ffn_pre", "norm_ffn_post")
ROW_PAD = 256
GRAD_ROW_PAD = 64
PART_ROWS = 16


def _pack_rows(arrays, dtype, multiple, lead=()):
    rows, offsets, at = [], [], 0
    zero_pad = ((0, 0),) * len(lead)
    for a in arrays:
        flat = a.reshape(lead + (-1,)).astype(dtype)
        n = -(-flat.shape[-1] // D_MODEL)
        n_pad = -(-n // PART_ROWS) * PART_ROWS
        flat = jnp.pad(flat, zero_pad + ((0, n_pad * D_MODEL - flat.shape[-1]),))
        rows.append(flat.reshape(lead + (n_pad, D_MODEL)))
        offsets.append((at, n))
        at += n_pad
    pad = -at % multiple
    if pad:
        rows.append(jnp.zeros(lead + (pad, D_MODEL), dtype))
    return jnp.concatenate(rows, axis=len(lead)), offsets


def _unpack_rows(buf, offsets, shapes):
    lead = buf.shape[:-2]
    out = []
    for (at, n), shape in zip(offsets, shapes):
        size = math.prod(shape)
        flat = buf[..., at:at + n, :].reshape(lead + (n * D_MODEL,))[..., :size]
        out.append(flat.reshape(lead + tuple(shape)))
    return out


def _pack_shards(p, depth, names):
    arrays = []
    for n in names:
        a = p[n][0] if n in ROW_SHARDED else jnp.swapaxes(p[n][0], -1, -2)
        arrays.append(lax.bitcast_convert_type(a, BF16) if n == "conv_w" else a.astype(BF16))
    packed, offsets = _pack_rows(arrays, BF16, GRAD_ROW_PAD, lead=(depth,))
    return packed, offsets, [a.shape[1:] for a in arrays]


def _unpack_layer_weights(names, gathered, offsets, shapes):
    full = {}
    for n, gth in zip(names, _unpack_rows(gathered, offsets, shapes)):
        if n == "conv_w":
            gth = lax.bitcast_convert_type(gth, F32)
        full[n] = gth.reshape((-1,) + gth.shape[2:])
        if n == "conv_w":
            full[n] = full[n].T
    return full


def _owner_major(grad_sharded_dim_first):
    a = grad_sharded_dim_first
    return a.reshape((N_DEV, a.shape[0] // N_DEV) + a.shape[1:])


def _owner_block_shape(name, shard_shape):
    k, n = shard_shape[-2:]
    return (k, n) if name in ROW_SHARDED else (n, k)


def _natural_shard(name, blocks):
    return blocks if name in ROW_SHARDED else jnp.swapaxes(blocks, -1, -2)


def _swap_halves(a, axis=-1):
    lo, hi = jnp.split(a, 2, axis=axis)
    return jnp.concatenate([hi, lo], axis=axis)


def _pad_rows(a, top, total):
    return jnp.pad(a, ((0, 0),) * (a.ndim - 2) + ((top, total - top - a.shape[-2]), (0, 0)))


def _weights_before_attn(full, small, l):
    w_in = full["w_in"]
    k_r = w_in[384:416]
    w_in_k = jnp.concatenate(
        [w_in[0:384], _pad_rows(k_r, D_NOPE, HEAD_PAD), _pad_rows(_swap_halves(k_r, axis=0), D_NOPE, HEAD_PAD), w_in[416:1952]], axis=0)
    w_uq = full["w_uq"].reshape(N_HEADS, D_NOPE + D_ROPE, Q_RANK)
    wq = _pad_rows(w_uq, 0, HEAD_PAD).reshape(N_HEADS * HEAD_PAD, Q_RANK)
    wqs = _pad_rows(_swap_halves(w_uq[:, D_NOPE:], axis=1), D_NOPE, HEAD_PAD).reshape(N_HEADS * HEAD_PAD, Q_RANK)
    w_ukv = full["w_ukv"].reshape(N_HEADS, D_NOPE + D_V, KV_RANK)
    wk = _pad_rows(w_ukv[:, :D_NOPE], 0, HEAD_PAD).reshape(N_HEADS * HEAD_PAD, KV_RANK)
    wv = w_ukv[:, D_NOPE:].reshape(N_HEADS * D_V, KV_RANK)
    tri = jnp.tril(jnp.ones((SG_CHUNK, SG_CHUNK), bool))
    wm = jnp.where(tri, small["sg_w"][l], 0.0)
    pool_w = small["pool_w"][l]
    wp = jnp.zeros((BR_W, BR_W), F32)
    for g in range(len(POOL_WINDOWS)):
        wp = wp.at[g * POOL_GROUP:(g + 1) * POOL_GROUP, g * POOL_GROUP:(g + 1) * POOL_GROUP].set(pool_w[g])
    vec = lambda name: small[name][l][None, :]
    return dict(
        w_in=w_in_k, wg=w_in[1952:], bg=vec("gate_b"), g1=vec("norm_mix_pre"), qn=vec("q_norm"), kvn=vec("kv_norm"),
        wq=wq, wqs=wqs, wk=wk, wv=wv,
        ln_g=vec("sg_ln_g"), ln_b=vec("sg_ln_b"), wm=wm.astype(BF16), wmt=jnp.swapaxes(wm, 1, 2).astype(BF16),
        sgb=jnp.repeat(small["sg_b"][l].T, POOL_GROUP, axis=1), wp=wp.astype(BF16), pscale=vec("pool_scale"),
        g2=vec("norm_mix_post"), g3=vec("norm_ffn_pre"), g4=vec("norm_ffn_post"))


def _weights_after_attn(full):
    return dict(
        conv_w=full["conv_w"], wbr=jnp.stack([full["w_br_mla"], full["w_br_sg"], full["w_br_conv"], full["w_br_pool"]]),
        wout=full["w_out"], w1=full["w_ff1"], w2=full["w_ff2"])


def _grads_after_attn(g):
    own = _owner_major
    return dict(
        w_br_mla=own(g["dwbr"][0]), w_br_sg=own(g["dwbr"][1]), w_br_conv=own(g["dwbr"][2]), w_br_pool=own(g["dwbr"][3]),
        w_out=own(g["dwout"]), w_ff1=own(g["dw1"]), w_ff2=own(g["dw2"]), conv_w=own(g["dcw"][:CONV_K].T))


def _grads_before_attn(g):
    dwa = g["dwa"]
    rope_rows = slice(D_NOPE, D_NOPE + D_ROPE)
    d_kr = dwa[384:512][rope_rows].astype(F32) + _swap_halves(dwa[512:640][rope_rows], axis=0).astype(F32)
    w_in = jnp.concatenate([dwa[0:384], d_kr.astype(BF16), g["dwm_in"]] + list(g["dwg"]), axis=0)
    dwq, dwqs = (g[n].reshape(N_HEADS, HEAD_PAD, Q_RANK) for n in ("dwq", "dwqs"))
    d_rope = dwq[:, rope_rows] + _swap_halves(dwqs[:, rope_rows], axis=1)
    w_uq = jnp.concatenate([dwq[:, :D_NOPE], d_rope], axis=1).reshape(N_HEADS * (D_NOPE + D_ROPE), Q_RANK)
    dwk = g["dwk"].reshape(N_HEADS, HEAD_PAD, KV_RANK)
    w_ukv = jnp.concatenate([dwk[:, :D_NOPE], g["dwv"].reshape(N_HEADS, D_V, KV_RANK)], axis=1)
    w_ukv = w_ukv.reshape(N_HEADS * (D_NOPE + D_V), KV_RANK)
    pool_w = jnp.stack([g["dwp"][i * POOL_GROUP:(i + 1) * POOL_GROUP, i * POOL_GROUP:(i + 1) * POOL_GROUP]
                        for i in range(len(POOL_WINDOWS))])
    sg_b = g["dsgb"].reshape(SG_CHUNK, SG_GROUPS, POOL_GROUP).sum(axis=-1).T
    own = _owner_major
    return dict(
        w_in=own(w_in), w_uq=own(w_uq), w_ukv=own(w_ukv),
        norm_mix_pre=g["dg1"][0], gate_b=jnp.concatenate([b[0] for b in g["dbg"]]), q_norm=g["dqn"][0], kv_norm=g["dkvn"][0],
        sg_ln_g=g["dlng"][0], sg_ln_b=g["dlnb"][0], sg_w=g["dwm"], sg_b=sg_b, pool_w=pool_w, pool_scale=g["dps"][0],
        norm_mix_post=g["dg2"][0], norm_ffn_pre=g["dg3"][0], norm_ffn_post=g["dg4"][0])


def _rope_tables(positions):
    inv_freq = ROPE_BASE ** (-jnp.arange(0, D_ROPE, 2, dtype=F32) / D_ROPE)
    ang = positions.astype(F32)[:, None] * inv_freq
    cos, sin = jnp.cos(ang), jnp.sin(ang)
    s = positions.shape[0]
    cosf = jnp.concatenate([jnp.ones((s, D_NOPE), F32), cos, cos, jnp.zeros((s, HEAD_PAD - D_NOPE - D_ROPE), F32)], axis=1)
    sins = jnp.concatenate([jnp.zeros((s, D_NOPE), F32), -sin, sin, jnp.zeros((s, HEAD_PAD - D_NOPE - D_ROPE), F32)], axis=1)
    return cosf, sins


def _layer_fwd(x, w, weights_after_attn, cosf, sins, tiles, riding):
    t, ta = tiles["tok"], tiles["attn"]
    hb, pa, pm = _inproj_fwd(x, w["g1"], w["w_in"], tiles["wgrad"])
    q, k, v = _attn_prep_fwd(pa, cosf, sins, w["qn"], w["kvn"], w["wq"], w["wqs"], w["wk"], w["wv"], t)
    o, lse, landed = _flash_fwd(q, k, v, ta, riding)
    w = {**w, **weights_after_attn(landed)}
    bsg, bcv, bpl = _mixers_fwd(pm, w["ln_g"], w["ln_b"], w["wm"], w["sgb"], w["conv_w"], w["wp"], w["pscale"], t)
    x_mid, merged, tt = _merge_fwd(x, hb, (o, bsg, bcv, bpl), w["wg"], w["bg"], w["wbr"], w["wout"], w["g2"], t)
    x_out, f = _ffn_fwd(x_mid, w["g3"], w["w1"], w["w2"], w["g4"], t)
    saved = dict(x=x, hb=hb, pa=pa, pm=pm, q=q, k=k, v=v, o=o, lse=lse, bsg=bsg, bcv=bcv, bpl=bpl, x_mid=x_mid,
                 merged=merged, tt=tt, f=f)
    return x_out, saved, w, landed


def _layer_bwd(dxo, a, w, cosf, sins, tiles, pending):
    t, ta, tb, tw = tiles["tok"], tiles["attn"], tiles["inproj_bwd"], tiles["wgrad"]
    g = {}
    h2, df, g["dg4"] = _ffn_bwd_norms(a["x_mid"], a["f"], dxo, w["g3"], w["g4"], t)
    da, g["dw1"], g["dw2"] = _ffn_bwd_weights(h2, df, w["w1"], w["w2"], tw)
    dxmid, g["dg3"] = _ffn_bwd_input(da, w["w1"], a["x_mid"], dxo, w["g3"], t)
    dm, g["dwout"], g["dg2"] = _merge_bwd_out(a["tt"], dxmid, w["g2"], w["wout"], a["merged"], tw)
    dbrs, dpres, g["dwg"], g["dbg"], g["dwbr"] = [], [], [], [], []
    for k, br in enumerate((a["o"], a["bsg"], a["bcv"], a["bpl"])):
        dbr, dpre, dwg, dbg, dwbr = _merge_bwd_branch(k, a["hb"], br, dm, w["wg"], w["bg"], w["wbr"], tw)
        dbrs.append(dbr); dpres.append(dpre); g["dwg"].append(dwg); g["dbg"].append(dbg); g["dwbr"].append(dwbr)
    dpm, g["dlng"], g["dlnb"], g["dwm"], g["dsgb"], g["dcw"], g["dwp"], g["dps"] = _mixers_bwd(
        a["pm"], dbrs[1], dbrs[2], dbrs[3], w["ln_g"], w["ln_b"], w["wm"], w["wmt"], w["sgb"], w["conv_w"], w["wp"], w["pscale"], t)
    after = _grads_after_attn(g)
    send, after_offsets = _pack_rows([after[n] for n in AFTER_ATTN], BF16, GRAD_ROW_PAD, lead=(N_DEV,))
    after_rows = send.shape[1]
    if pending is not None:
        send = jnp.concatenate([send, pending], axis=1)
    riding = _riding(_exchange_plan, send, _sds(send.shape, send.dtype))
    dq, dk, dv, landed = _flash_bwd(a["q"], a["k"], a["v"], a["o"], dbrs[0], a["lse"], ta, riding)
    dpa, g["dwq"], g["dwqs"], g["dwk"], g["dwv"], g["dqn"], g["dkvn"] = _attn_prep_bwd(
        a["pa"], dq, dk, dv, cosf, sins, w["qn"], w["kvn"], w["wq"], w["wqs"], w["wk"], w["wv"], t)
    dx, g["dg1"], g["dwa"], g["dwm_in"] = _inproj_bwd(a["x"], w["g1"], dxmid, dpa, dpm, dpres, w["w_in"], w["wg"], tb)
    return dx, _grads_before_attn(g), landed, (after_rows, after_offsets)


def _step(p, x, positions, loss_target):
    s = x.shape[0]
    depth = p["w_in"][0].shape[0]
    tiles = dict(tok=min(TOK_TILE, s), attn=min(ATTN_TILE, s), inproj_bwd=min(INPROJ_BWD_TILE, s),
                 wgrad=min(WGRAD_TILE, s))

    small = {n: p[n][0] for n in REPLICATED}
    cosf, sins = _rope_tables(positions)

    pack_b, off_b, shapes_b = _pack_shards(p, depth, BEFORE_ATTN)
    pack_a, off_a, shapes_a = _pack_shards(p, depth, AFTER_ATTN)
    rows_a = pack_a.shape[1]
    before = _all_gather(pack_b[0], "weight_all_gather")
    weights, acts = [], []
    h = x
    for l in range(depth):
        w = _weights_before_attn(_unpack_layer_weights(BEFORE_ATTN, before, off_b, shapes_b), small, l)
        src = pack_a[l] if l + 1 == depth else jnp.concatenate([pack_a[l], pack_b[l + 1]], axis=0)
        riding = _riding(_gather_plan, src, _sds((N_DEV,) + src.shape, src.dtype))

        def after(landed):
            return _weights_after_attn(_unpack_layer_weights(AFTER_ATTN, landed[:, :rows_a], off_a, shapes_a))

        h, saved, w, landed = _layer_fwd(h, w, after, cosf, sins, tiles, riding)
        weights.append(w)
        acts.append(saved)
        before = landed[:, rows_a:] if l + 1 < depth else None
    dh, loss_blk = _loss_head(h, loss_target, tiles["tok"])
    loss = lax.psum(loss_blk[0, 0], ("x", "y", "c"))

    grads_b = [None] * depth
    sum_a, sum_b = [None] * depth, [None] * depth
    pending = None
    for l in reversed(range(depth)):
        dh, grads_b[l], landed, (rows_ga, off_ga) = _layer_bwd(dh, acts[l], weights[l], cosf, sins, tiles, pending)
        summed = _sum_slots(landed, "grad_shard_sum")
        sum_a[l] = summed[:rows_ga]
        if pending is not None:
            sum_b[l + 1] = summed[rows_ga:]
        pending, off_gb = _pack_rows([grads_b[l][n] for n in BEFORE_ATTN], BF16, GRAD_ROW_PAD, lead=(N_DEV,))
    sum_b[0] = _sum_slots(_all_to_all(pending, "grad_all_to_all"), "grad_shard_sum")
    g_shard = {}
    for names, sums, offsets in ((AFTER_ATTN, sum_a, off_ga), (BEFORE_ATTN, sum_b, off_gb)):
        block_shapes = [_owner_block_shape(n, p[n][0].shape) for n in names]
        for n, b in zip(names, _unpack_rows(jnp.stack(sums), offsets, block_shapes)):
            g_shard[n] = _natural_shard(n, b)
    small_grads = [jnp.stack([grads_b[l][n] for l in range(depth)]) for n in REPLICATED]
    small_rows, small_offsets = _pack_rows(small_grads, F32, ROW_PAD)
    g_small_rows = _sum_slots(_all_gather(small_rows, "replicated_grad_all_gather"), "grad_replicated_sum")

    grad, delta, new_m, new_v = {}, {}, {}, {}
    for n in SHARDED:
        w, m, v = p[n]
        shape2 = (math.prod(w.shape[:-1]), w.shape[-1])
        grad[n] = g_shard[n]
        d, nm, nv = _adamw(w.reshape(shape2), grad[n].reshape(shape2), m.reshape(shape2), v.reshape(shape2), f"adamw_{n}")
        delta[n], new_m[n], new_v[n] = d.reshape(w.shape), nm.reshape(w.shape), nv.reshape(w.shape)
    packs = [_pack_rows([p[n][i] for n in REPLICATED], F32, ROW_PAD)[0] for i in range(3)]
    d, nm, nv = _adamw(packs[0], g_small_rows, packs[1], packs[2], "adamw_replicated")
    shapes = [p[n][0].shape for n in REPLICATED]
    for n, gg, dd, mm, vv in zip(REPLICATED, _unpack_rows(g_small_rows, small_offsets, shapes), _unpack_rows(d, small_offsets, shapes),
                                 _unpack_rows(nm, small_offsets, shapes), _unpack_rows(nv, small_offsets, shapes)):
        grad[n], delta[n], new_m[n], new_v[n] = gg, dd, mm, vv
    return loss, dh, grad, delta, new_m, new_v


WEIGHT_ORDER = ("norm_mix_pre", "w_in", "gate_b", "q_norm", "w_uq", "kv_norm", "w_ukv", "w_br_mla", "sg_ln_g", "sg_ln_b",
                "sg_w", "sg_b", "w_br_sg", "conv_w", "w_br_conv", "pool_w", "pool_scale", "w_br_pool", "w_out",
                "norm_mix_post", "norm_ffn_pre", "w_ff1", "w_ff2", "norm_ffn_post")


def kernel(x, positions, norm_mix_pre, w_in, gate_b, q_norm, w_uq, kv_norm, w_ukv, w_br_mla, sg_ln_g, sg_ln_b, sg_w, sg_b, w_br_sg, conv_w, w_br_conv, pool_w, pool_scale, w_br_pool, w_out, norm_mix_post, norm_ffn_pre, w_ff1, w_ff2, norm_ffn_post, loss_target, m_norm_mix_pre, m_w_in, m_gate_b, m_q_norm, m_w_uq, m_kv_norm, m_w_ukv, m_w_br_mla, m_sg_ln_g, m_sg_ln_b, m_sg_w, m_sg_b, m_w_br_sg, m_conv_w, m_w_br_conv, m_pool_w, m_pool_scale, m_w_br_pool, m_w_out, m_norm_mix_post, m_norm_ffn_pre, m_w_ff1, m_w_ff2, m_norm_ffn_post, v_norm_mix_pre, v_w_in, v_gate_b, v_q_norm, v_w_uq, v_kv_norm, v_w_ukv, v_w_br_mla, v_sg_ln_g, v_sg_ln_b, v_sg_w, v_sg_b, v_w_br_sg, v_conv_w, v_w_br_conv, v_pool_w, v_pool_scale, v_w_br_pool, v_w_out, v_norm_mix_post, v_norm_ffn_pre, v_w_ff1, v_w_ff2, v_norm_ffn_post):
    ws = (norm_mix_pre, w_in, gate_b, q_norm, w_uq, kv_norm, w_ukv, w_br_mla, sg_ln_g, sg_ln_b, sg_w, sg_b, w_br_sg, conv_w,
          w_br_conv, pool_w, pool_scale, w_br_pool, w_out, norm_mix_post, norm_ffn_pre, w_ff1, w_ff2, norm_ffn_post)
    ms = (m_norm_mix_pre, m_w_in, m_gate_b, m_q_norm, m_w_uq, m_kv_norm, m_w_ukv, m_w_br_mla, m_sg_ln_g, m_sg_ln_b, m_sg_w,
          m_sg_b, m_w_br_sg, m_conv_w, m_w_br_conv, m_pool_w, m_pool_scale, m_w_br_pool, m_w_out, m_norm_mix_post,
          m_norm_ffn_pre, m_w_ff1, m_w_ff2, m_norm_ffn_post)
    vs = (v_norm_mix_pre, v_w_in, v_gate_b, v_q_norm, v_w_uq, v_kv_norm, v_w_ukv, v_w_br_mla, v_sg_ln_g, v_sg_ln_b, v_sg_w,
          v_sg_b, v_w_br_sg, v_conv_w, v_w_br_conv, v_pool_w, v_pool_scale, v_w_br_pool, v_w_out, v_norm_mix_post,
          v_norm_ffn_pre, v_w_ff1, v_w_ff2, v_norm_ffn_post)
    p = {n: (w, m, v) for n, w, m, v in zip(WEIGHT_ORDER, ws, ms, vs)}
    loss, grad_x, grad, delta, new_m, new_v = _step(p, x[0], positions[0], loss_target[0])
    return (loss, grad_x[None], *[grad[n] for n in WEIGHT_ORDER], *[delta[n] for n in WEIGHT_ORDER],
            *[new_m[n] for n in WEIGHT_ORDER], *[new_v[n] for n in WEIGHT_ORDER])
```

```python
import functools
import math

import jax
import jax.numpy as jnp
from jax import lax
from jax.experimental import pallas as pl
from jax.experimental.pallas import tpu as pltpu

F32 = jnp.float32
BF16 = jnp.bfloat16

D_MODEL = 1024
N_HEADS = 4
D_NOPE = 64
D_ROPE = 32
D_V = 64
Q_RANK = 256
KV_RANK = 128
BR_W = 256
SG_CHUNK = 128
SG_GROUPS = 4
POOL_WINDOWS = (2, 4, 8, 16)
POOL_GROUP = 64
CONV_K = 3
D_FF = 4096
N_BRANCH = 4
N_IN = 6048
EPS = 1e-6
ROPE_BASE = 10000.0
ADAM_LR = 0.001
ADAM_B1 = 0.9
ADAM_B2 = 0.999
ADAM_EPS = 1e-08
ADAM_WD = 0.01
ADAM_STEP = 10

N_DEV = 8
LANES = 128
HEAD_PAD = 128
HALO = 16
PA_W = 640
PM_W = 6 * BR_W
VMEM_LIMIT = 56 * 1024 * 1024
SM_SCALE = (D_NOPE + D_ROPE) ** -0.5
TOK_TILE = 512
ATTN_TILE = 1024
WGRAD_TILE = 1024
INPROJ_BWD_TILE = 512

NN = (((1,), (0,)), ((), ()))
NT = (((1,), (1,)), ((), ()))
TN = (((0,), (0,)), ((), ()))
MESH_ID = pl.DeviceIdType.MESH


def _dot(a, b, dims=NN):
    return lax.dot_general(a, b, dims, preferred_element_type=F32)


def _pcall(body, *, name, grid, in_specs, out_specs, out_shape, scratch=(), vmem=None, prefetch=0):
    params = pltpu.CompilerParams(vmem_limit_bytes=vmem)
    if prefetch:
        spec = pltpu.PrefetchScalarGridSpec(num_scalar_prefetch=prefetch, grid=grid, in_specs=in_specs,
                                            out_specs=out_specs, scratch_shapes=scratch)
        return pl.pallas_call(body, name=name, grid_spec=spec, out_shape=out_shape, compiler_params=params)
    return pl.pallas_call(
        body, name=name, grid=grid, in_specs=in_specs, out_specs=out_specs, out_shape=out_shape,
        scratch_shapes=scratch, compiler_params=params)


def _rows(t, width):
    return pl.BlockSpec((t, width), lambda i: (i, 0))


def _whole(shape):
    nd = len(shape)
    return pl.BlockSpec(tuple(shape), lambda *_: (0,) * nd, pipeline_mode=pl.Buffered(1))


def _acc(shape):
    nd = len(shape)
    return pl.BlockSpec(tuple(shape), lambda *_: (0,) * nd)


def _sds(shape, dtype):
    return jax.ShapeDtypeStruct(tuple(shape), dtype)


def _rms(x, g):
    return x * lax.rsqrt(jnp.mean(x * x, axis=-1, keepdims=True) + EPS) * g


def _rms_bwd(x, g, dy):
    r = lax.rsqrt(jnp.mean(x * x, axis=-1, keepdims=True) + EPS)
    xh = x * r
    dg = jnp.sum(dy * xh, axis=0, keepdims=True)
    dxh = dy * g
    dx = r * (dxh - xh * jnp.mean(dxh * xh, axis=-1, keepdims=True))
    return dx, dg


def _sigmoid(x):
    return 1.0 / (1.0 + jnp.exp(-x))


def _gelu(x):
    return jax.nn.gelu(x, approximate=True)


def _accumulate(ref, val, first):
    @pl.when(first)
    def _():
        ref[...] = val

    @pl.when(jnp.logical_not(first))
    def _():
        ref[...] += val


def _accumulate_out(acc_ref, out_ref, val, first, last):
    _accumulate(acc_ref, val, first)

    @pl.when(last)
    def _():
        out_ref[...] = acc_ref[...].astype(out_ref.dtype)


def _inproj_fwd(x, g1, w_in, t):
    s = x.shape[0]

    def body(x_ref, g_ref, w_ref, hb_ref, pa_ref, pm_ref):
        hb = _rms(x_ref[...], g_ref[...]).astype(BF16)
        hb_ref[...] = hb
        pa_ref[...] = _dot(hb, w_ref[:PA_W, :], NT)
        pm_ref[...] = _dot(hb, w_ref[PA_W:, :], NT)

    return _pcall(
        body, name="inproj_fwd", grid=(s // t,),
        in_specs=[_rows(t, D_MODEL), _whole((1, D_MODEL)), _whole(w_in.shape)],
        out_specs=[_rows(t, D_MODEL), _rows(t, PA_W), _rows(t, PM_W)],
        out_shape=[_sds((s, D_MODEL), BF16), _sds((s, PA_W), F32), _sds((s, PM_W), F32)],
        vmem=VMEM_LIMIT)(x, g1, w_in)


def _attn_prep_fwd(pa, cosf, sins, qn, kvn, wq, wqs, wk, wv, t):
    s = pa.shape[0]

    def body(pa_ref, cos_ref, sin_ref, qn_ref, kvn_ref, wq_ref, wqs_ref, wk_ref, wv_ref, q_ref, k_ref, v_ref):
        cosv, sinv = cos_ref[...], sin_ref[...]
        cqn = _rms(pa_ref[:, 0:Q_RANK], qn_ref[...]).astype(BF16)
        ckvn = _rms(pa_ref[:, Q_RANK:Q_RANK + KV_RANK], kvn_ref[...]).astype(BF16)
        k_rope = pa_ref[:, 384:512] * cosv + pa_ref[:, 512:640] * sinv
        q_all, qs_all = _dot(cqn, wq_ref[...], NT), _dot(cqn, wqs_ref[...], NT)
        k_all = _dot(ckvn, wk_ref[...], NT)
        for h in range(N_HEADS):
            lanes = slice(h * HEAD_PAD, (h + 1) * HEAD_PAD)
            q_ref[h] = (q_all[:, lanes] * cosv + qs_all[:, lanes] * sinv).astype(BF16)
            k_ref[h] = (k_all[:, lanes] + k_rope).astype(BF16)
        v_ref[...] = _dot(ckvn, wv_ref[...], NT).astype(BF16)

    head_blk = pl.BlockSpec((N_HEADS, t, HEAD_PAD), lambda i: (0, i, 0))
    return _pcall(
        body, name="attn_prep_fwd", grid=(s // t,),
        in_specs=[_rows(t, PA_W), _rows(t, LANES), _rows(t, LANES), _whole(qn.shape), _whole(kvn.shape),
                  _whole(wq.shape), _whole(wqs.shape), _whole(wk.shape), _whole(wv.shape)],
        out_specs=[head_blk, head_blk, _rows(t, BR_W)],
        out_shape=[_sds((N_HEADS, s, HEAD_PAD), BF16), _sds((N_HEADS, s, HEAD_PAD), BF16), _sds((s, BR_W), BF16)],
        vmem=VMEM_LIMIT)(pa, cosf, sins, qn, kvn, wq, wqs, wk, wv)


def _causal_scores(q, k, masked):
    sc = _dot(q, k, NT) * SM_SCALE
    if masked:
        row = lax.broadcasted_iota(jnp.int32, sc.shape, 0)
        col = lax.broadcasted_iota(jnp.int32, sc.shape, 1)
        sc = jnp.where(col <= row, sc, -jnp.inf)
    return sc


def _diagonal_parts(t):
    half = t // 2
    if half % LANES:
        return [(True, slice(0, t), slice(0, t))]
    lower, upper = slice(0, half), slice(half, t)
    return [(True, lower, lower), (False, upper, lower), (True, upper, upper)]


def _causal_steps(n, key_major):
    if key_major:
        pairs = [(qi, kj) for kj in range(n) for qi in range(kj, n)]
    else:
        pairs = [(qi, kj) for qi in range(n) for kj in range(qi + 1)]
    return (jnp.asarray([p[0] for p in pairs], jnp.int32), jnp.asarray([p[1] for p in pairs], jnp.int32))


def _flash_fwd(q, k, v, t, riding):
    s = v.shape[0]
    n = s // t
    qi_tab, kj_tab = _causal_steps(n, key_major=False)
    n_steps = int(qi_tab.shape[0])
    whole = slice(0, t)

    def body(qi_ref, kj_ref, q_ref, k_ref, v_ref, x_ref, o_ref, lse_ref, land_ref, m_scr, l_scr, acc_scr,
             send_sems, recv_sems, local_sem):
        step_id = pl.program_id(0)
        qi, kj = qi_ref[step_id], kj_ref[step_id]
        at_end = _ride_along(riding, step_id, n_steps, (x_ref, land_ref, send_sems, recv_sems, local_sem))

        @pl.when(kj == 0)
        def _():
            m_scr[...] = jnp.full(m_scr.shape, -jnp.inf, F32)
            l_scr[...] = jnp.zeros(l_scr.shape, F32)
            acc_scr[...] = jnp.zeros(acc_scr.shape, F32)

        def step(masked, qr, kr):
            for h in range(N_HEADS):
                sc = _causal_scores(q_ref[h, qr, :], k_ref[h, kr, :], masked)
                m_prev = m_scr[h, qr, :]
                m_next = jnp.maximum(m_prev, jnp.max(sc, axis=1, keepdims=True))
                alpha = jnp.exp(m_prev - m_next)
                p = jnp.exp(sc - jnp.tile(m_next, (1, sc.shape[1] // LANES)))
                l_scr[h, qr, :] = alpha * l_scr[h, qr, :] + jnp.sum(p, axis=1, keepdims=True)
                m_scr[h, qr, :] = m_next
                pair = slice((h // 2) * LANES, (h // 2 + 1) * LANES)
                acc_scr[h, qr, :] = acc_scr[h, qr, :] * alpha + _dot(p.astype(BF16), v_ref[kr, pair])

        @pl.when(kj < qi)
        def _():
            step(False, whole, whole)

        @pl.when(kj == qi)
        def _():
            for masked, qr, kr in _diagonal_parts(t):
                step(masked, qr, kr)
            lane = lax.broadcasted_iota(jnp.int32, (t, LANES), 1)
            for pr in range(N_HEADS // 2):
                o0 = acc_scr[2 * pr] / l_scr[2 * pr]
                o1 = acc_scr[2 * pr + 1] / l_scr[2 * pr + 1]
                o_ref[:, pr * LANES:(pr + 1) * LANES] = jnp.where(lane < D_V, o0, o1).astype(BF16)
            for h in range(N_HEADS):
                lse_ref[h] = m_scr[h] + jnp.log(l_scr[h])

        at_end()

    q_blk = pl.BlockSpec((N_HEADS, t, HEAD_PAD), lambda i, qi, kj: (0, qi[i], 0))
    k_blk = pl.BlockSpec((N_HEADS, t, HEAD_PAD), lambda i, qi, kj: (0, kj[i], 0))
    v_blk = pl.BlockSpec((t, BR_W), lambda i, qi, kj: (kj[i], 0))
    return _pcall(
        body, name="flash_fwd_gathering", grid=(n_steps,), prefetch=2,
        in_specs=[q_blk, k_blk, v_blk, HBM_REF],
        out_specs=[pl.BlockSpec((t, BR_W), lambda i, qi, kj: (qi[i], 0)), q_blk, HBM_REF],
        out_shape=[_sds((s, BR_W), BF16), _sds((N_HEADS, s, LANES), F32), riding[2]],
        scratch=[pltpu.VMEM((N_HEADS, t, LANES), F32)] * 3 + list(EXCHANGE_SEMS),
        vmem=VMEM_LIMIT)(qi_tab, kj_tab, q, k, v, riding[1])


def _shift_down(ext, k, t):
    return pltpu.roll(ext, k, 0)[HALO:HALO + t]


def _shift_up(ext, k, t):
    return pltpu.roll(ext, t + HALO - k, 0)[0:t]


def _lane_group(shape):
    return lax.shift_right_logical(lax.broadcasted_iota(jnp.int32, shape, 1), 6)


def _group_select(vals):
    grp = _lane_group(vals[0].shape)
    out = vals[0]
    for g in range(1, len(vals)):
        out = jnp.where(grp == g, vals[g], out)
    return out


def _layernorm(x, g, b):
    mu = jnp.mean(x, axis=-1, keepdims=True)
    xc = x - mu
    return xc * lax.rsqrt(jnp.mean(xc * xc, axis=-1, keepdims=True) + EPS) * g + b


def _sg_mix(wm_ref, vnb, bias):
    return _group_select([_dot(wm_ref[g], vnb) for g in range(SG_GROUPS)]) + bias


def _pool_windows(ext):
    s2 = ext + pltpu.roll(ext, 1, 0)
    s4 = s2 + pltpu.roll(s2, 2, 0)
    s8 = s4 + pltpu.roll(s4, 4, 0)
    s16 = s8 + pltpu.roll(s8, 8, 0)
    return [s2, s4, s8, s16]


def _pool_counts(tok):
    return [jnp.minimum(tok + 1, w).astype(F32) for w in POOL_WINDOWS]


def _halo_before(t):
    return pl.BlockSpec((HALO, PM_W), lambda i: (jnp.maximum(i * (t // HALO) - 1, 0), 0))


def _mixers_fwd(pm, ln_g, ln_b, wm, sgb, conv_w, wp, pscale, t):
    s = pm.shape[0]

    def body(pm_ref, halo_ref, lng_ref, lnb_ref, wm_ref, sgb_ref, cw_ref, wp_ref, ps_ref, bsg_ref, bcv_ref, bpl_ref):
        i = pl.program_id(0)
        halo = jnp.where(i > 0, halo_ref[...], 0.0)
        u = _gelu(pm_ref[:, 0:256])
        vnb = _layernorm(_gelu(pm_ref[:, 256:512]), lng_ref[...], lnb_ref[...]).astype(BF16)
        for c in range(t // SG_CHUNK):
            rows = slice(c * SG_CHUNK, (c + 1) * SG_CHUNK)
            bsg_ref[rows, :] = (u[rows] * _sg_mix(wm_ref, vnb[rows], sgb_ref[...])).astype(BF16)
        z = pm_ref[:, 1024:1280] * pm_ref[:, 512:768]
        zext = jnp.concatenate([halo[:, 1024:1280] * halo[:, 512:768], z], axis=0)
        y = cw_ref[0:1, :] * _shift_down(zext, 2, t) + cw_ref[1:2, :] * _shift_down(zext, 1, t) + cw_ref[2:3, :] * z
        bcv_ref[...] = (pm_ref[:, 768:1024] * y).astype(BF16)
        p = pm_ref[:, 1280:1536]
        sums = _pool_windows(jnp.concatenate([halo[:, 1280:1536], p], axis=0))
        tok = i * t + lax.broadcasted_iota(jnp.int32, (t, 1), 0)
        pooled = _group_select([sw[HALO:HALO + t] / cnt - p for sw, cnt in zip(sums, _pool_counts(tok))])
        bpl_ref[...] = (_dot(pooled.astype(BF16), wp_ref[...]) * ps_ref[...]).astype(BF16)

    return _pcall(
        body, name="mixers_fwd", grid=(s // t,),
        in_specs=[_rows(t, PM_W), _halo_before(t), _whole(ln_g.shape), _whole(ln_b.shape), _whole(wm.shape),
                  _whole(sgb.shape), _whole(conv_w.shape), _whole(wp.shape), _whole(pscale.shape)],
        out_specs=[_rows(t, BR_W)] * 3,
        out_shape=[_sds((s, BR_W), BF16)] * 3,
        vmem=VMEM_LIMIT)(pm, pm, ln_g, ln_b, wm, sgb, conv_w, wp, pscale)


def _merge_fwd(x, hb, branches, wg, bg, wbr, wout, g2, t):
    s = x.shape[0]

    def body(x_ref, hb_ref, b0_ref, b1_ref, b2_ref, b3_ref, wg_ref, bg_ref, wbr_ref, wout_ref, g2_ref,
             xmid_ref, mrg_ref, t_ref):
        hb = hb_ref[...]
        merged = jnp.zeros((t, D_MODEL), F32)
        for k, b_ref in enumerate((b0_ref, b1_ref, b2_ref, b3_ref)):
            cols = slice(k * D_MODEL, (k + 1) * D_MODEL)
            gate = _sigmoid(_dot(hb, wg_ref[cols, :], NT) + bg_ref[:, cols])
            merged = merged + gate * _dot(b_ref[...], wbr_ref[k], NT)
        mb = merged.astype(BF16)
        mrg_ref[...] = mb
        tt = _dot(mb, wout_ref[...])
        t_ref[...] = tt
        xmid_ref[...] = x_ref[...] + _rms(tt, g2_ref[...])

    return _pcall(
        body, name="merge_fwd", grid=(s // t,),
        in_specs=[_rows(t, D_MODEL), _rows(t, D_MODEL)] + [_rows(t, BR_W)] * 4 +
                 [_whole(wg.shape), _whole(bg.shape), _whole(wbr.shape), _whole(wout.shape), _whole(g2.shape)],
        out_specs=[_rows(t, D_MODEL)] * 3,
        out_shape=[_sds((s, D_MODEL), F32), _sds((s, D_MODEL), BF16), _sds((s, D_MODEL), F32)],
        vmem=VMEM_LIMIT)(x, hb, *branches, wg, bg, wbr, wout, g2)


def _ffn_fwd(x, g3, w1, w2, g4, t):
    s = x.shape[0]

    def body(x_ref, g3_ref, w1_ref, w2_ref, g4_ref, xo_ref, f_ref):
        h = _rms(x_ref[...], g3_ref[...]).astype(BF16)
        f = jnp.zeros((t, D_MODEL), F32)
        for j in range(D_FF // D_MODEL):
            cols = slice(j * D_MODEL, (j + 1) * D_MODEL)
            r = jnp.square(jnp.maximum(_dot(h, w1_ref[cols, :], NT), 0.0)).astype(BF16)
            f = f + _dot(r, w2_ref[cols, :])
        f_ref[...] = f
        xo_ref[...] = x_ref[...] + _rms(f, g4_ref[...])

    return _pcall(
        body, name="ffn_fwd", grid=(s // t,),
        in_specs=[_rows(t, D_MODEL), _whole(g3.shape), _whole(w1.shape), _whole(w2.shape), _whole(g4.shape)],
        out_specs=[_rows(t, D_MODEL)] * 2,
        out_shape=[_sds((s, D_MODEL), F32)] * 2,
        vmem=VMEM_LIMIT)(x, g3, w1, w2, g4)


def _loss_head(y, target, t):
    s = y.shape[0]
    n = s // t

    def body(y_ref, tg_ref, dy_ref, loss_ref, acc_scr):
        i = pl.program_id(0)
        d = y_ref[...] - tg_ref[...]
        dy_ref[...] = d * (1.0 / D_MODEL)
        _accumulate(acc_scr, jnp.sum(d * d, axis=0, keepdims=True), i == 0)

        @pl.when(i == n - 1)
        def _():
            loss_ref[...] = jnp.full(loss_ref.shape, 0.5 / D_MODEL, F32) * jnp.sum(acc_scr[...])

    return _pcall(
        body, name="loss_head", grid=(n,),
        in_specs=[_rows(t, D_MODEL)] * 2,
        out_specs=[_rows(t, D_MODEL), _acc((8, LANES))],
        out_shape=[_sds((s, D_MODEL), F32), _sds((8, LANES), F32)],
        scratch=[pltpu.VMEM((1, D_MODEL), F32)])(y, target)


def _ffn_bwd_norms(x_mid, f, dxo, g3, g4, t):
    s = x_mid.shape[0]

    def body(x_ref, f_ref, dxo_ref, g3_ref, g4_ref, h2_ref, df_ref, dg4_ref):
        h2_ref[...] = _rms(x_ref[...], g3_ref[...]).astype(BF16)
        df, dg4 = _rms_bwd(f_ref[...], g4_ref[...], dxo_ref[...])
        df_ref[...] = df.astype(BF16)
        _accumulate(dg4_ref, dg4, pl.program_id(0) == 0)

    return _pcall(
        body, name="ffn_bwd_norms", grid=(s // t,),
        in_specs=[_rows(t, D_MODEL)] * 3 + [_whole(g3.shape), _whole(g4.shape)],
        out_specs=[_rows(t, D_MODEL), _rows(t, D_MODEL), _acc((1, D_MODEL))],
        out_shape=[_sds((s, D_MODEL), BF16), _sds((s, D_MODEL), BF16), _sds((1, D_MODEL), F32)])(x_mid, f, dxo, g3, g4)


def _ffn_bwd_weights(h2, df, w1, w2, t):
    s = h2.shape[0]
    blk = D_MODEL
    n = s // t

    def body(h2_ref, df_ref, w1_ref, w2_ref, da_ref, dw1_ref, dw2_ref, acc1, acc2):
        first, last = pl.program_id(1) == 0, pl.program_id(1) == n - 1
        h2v, dfv = h2_ref[...], df_ref[...]
        rl = jnp.maximum(_dot(h2v, w1_ref[...], NT), 0.0)
        _accumulate_out(acc2, dw2_ref, _dot((rl * rl).astype(BF16), dfv, TN), first, last)
        da = (_dot(dfv, w2_ref[...], NT) * (2.0 * rl)).astype(BF16)
        da_ref[...] = da
        _accumulate_out(acc1, dw1_ref, _dot(da, h2v, TN), first, last)

    tok = pl.BlockSpec((t, D_MODEL), lambda j, i: (i, 0))
    hid = pl.BlockSpec((blk, D_MODEL), lambda j, i: (j, 0))
    return _pcall(
        body, name="ffn_bwd_weights", grid=(D_FF // blk, n),
        in_specs=[tok, tok, hid, hid],
        out_specs=[pl.BlockSpec((t, blk), lambda j, i: (i, j)), hid, hid],
        out_shape=[_sds((s, D_FF), BF16), _sds((D_FF, D_MODEL), BF16), _sds((D_FF, D_MODEL), BF16)],
        scratch=[pltpu.VMEM((blk, D_MODEL), F32)] * 2,
        vmem=VMEM_LIMIT)(h2, df, w1, w2)


def _ffn_bwd_input(da, w1, x_mid, dxo, g3, t):
    s = x_mid.shape[0]

    def body(da_ref, w1_ref, x_ref, dxo_ref, g3_ref, dx_ref, dg3_ref):
        dx, dg3 = _rms_bwd(x_ref[...], g3_ref[...], _dot(da_ref[...], w1_ref[...]))
        dx_ref[...] = dxo_ref[...] + dx
        _accumulate(dg3_ref, dg3, pl.program_id(0) == 0)

    return _pcall(
        body, name="ffn_bwd_input", grid=(s // t,),
        in_specs=[_rows(t, D_FF), _whole(w1.shape), _rows(t, D_MODEL), _rows(t, D_MODEL), _whole(g3.shape)],
        out_specs=[_rows(t, D_MODEL), _acc((1, D_MODEL))],
        out_shape=[_sds((s, D_MODEL), F32), _sds((1, D_MODEL), F32)],
        vmem=VMEM_LIMIT)(da, w1, x_mid, dxo, g3)


def _merge_bwd_out(tt, dxmid, g2, wout, merged, t):
    s = tt.shape[0]
    n = s // t

    def body(t_ref, dx_ref, g2_ref, wout_ref, mrg_ref, dm_ref, dwout_ref, dg2_ref, acc):
        first, last = pl.program_id(0) == 0, pl.program_id(0) == n - 1
        dt, dg2 = _rms_bwd(t_ref[...], g2_ref[...], dx_ref[...])
        dtb = dt.astype(BF16)
        dm_ref[...] = _dot(dtb, wout_ref[...], NT)
        _accumulate_out(acc, dwout_ref, _dot(mrg_ref[...], dtb, TN), first, last)
        _accumulate(dg2_ref, dg2, first)

    return _pcall(
        body, name="merge_bwd_out", grid=(n,),
        in_specs=[_rows(t, D_MODEL), _rows(t, D_MODEL), _whole(g2.shape), _whole(wout.shape), _rows(t, D_MODEL)],
        out_specs=[_rows(t, D_MODEL), _acc((D_MODEL, D_MODEL)), _acc((1, D_MODEL))],
        out_shape=[_sds((s, D_MODEL), F32), _sds((D_MODEL, D_MODEL), BF16), _sds((1, D_MODEL), F32)],
        scratch=[pltpu.VMEM((D_MODEL, D_MODEL), F32)],
        vmem=VMEM_LIMIT)(tt, dxmid, g2, wout, merged)


def _merge_bwd_branch(k, hb, br, dm, wg, bg, wbr, t):
    s = hb.shape[0]
    n = s // t

    def body(hb_ref, br_ref, dm_ref, wg_ref, bg_ref, wbr_ref, dbr_ref, dpre_ref, dwg_ref, dbg_ref, dwbr_ref,
             acc_g, acc_br):
        first, last = pl.program_id(0) == 0, pl.program_id(0) == n - 1
        hbv, brv, dmv, wbrv = hb_ref[...], br_ref[...], dm_ref[...], wbr_ref[0]
        gate = _sigmoid(_dot(hbv, wg_ref[...], NT) + bg_ref[...])
        dy = (dmv * gate).astype(BF16)
        dpre = dmv * _dot(brv, wbrv, NT) * gate * (1.0 - gate)
        dpb = dpre.astype(BF16)
        dpre_ref[...] = dpb
        dbr_ref[...] = _dot(dy, wbrv)
        _accumulate_out(acc_br, dwbr_ref, _dot(dy, brv, TN), first, last)
        _accumulate_out(acc_g, dwg_ref, _dot(dpb, hbv, TN), first, last)
        _accumulate(dbg_ref, jnp.sum(dpre, axis=0, keepdims=True), first)

    return _pcall(
        body, name=f"merge_bwd_branch{k}", grid=(n,),
        in_specs=[_rows(t, D_MODEL), _rows(t, BR_W), _rows(t, D_MODEL),
                  pl.BlockSpec((D_MODEL, D_MODEL), lambda i: (k, 0)), pl.BlockSpec((1, D_MODEL), lambda i: (0, k)),
                  pl.BlockSpec((1, D_MODEL, BR_W), lambda i: (k, 0, 0))],
        out_specs=[_rows(t, BR_W), _rows(t, D_MODEL), _acc((D_MODEL, D_MODEL)), _acc((1, D_MODEL)), _acc((D_MODEL, BR_W))],
        out_shape=[_sds((s, BR_W), F32), _sds((s, D_MODEL), BF16), _sds((D_MODEL, D_MODEL), BF16),
                   _sds((1, D_MODEL), F32), _sds((D_MODEL, BR_W), BF16)],
        scratch=[pltpu.VMEM((D_MODEL, D_MODEL), F32), pltpu.VMEM((D_MODEL, BR_W), F32)],
        vmem=VMEM_LIMIT)(hb, br, dm, wg, bg, wbr)


def _mixers_bwd(pm, dbsg, dbcv, dbpl, ln_g, ln_b, wm, wmt, sgb, conv_w, wp, pscale, t):
    s = pm.shape[0]
    n = s // t
    nb = t // HALO

    def body(pm_ref, before_ref, after_ref, dsg_ref, dcv_ref, dcva_ref, dpl_ref, dpla_ref, lng_ref, lnb_ref,
             wm_ref, wmt_ref, sgb_ref, cw_ref, wp_ref, ps_ref,
             dpm_ref, dlng_ref, dlnb_ref, dwm_ref, dsgb_ref, dcw_ref, dwp_ref, dps_ref):
        i = pl.program_id(0)
        first = i == 0
        before = jnp.where(i > 0, before_ref[...], 0.0)
        after = jnp.where(i < n - 1, after_ref[...], 0.0)

        u_raw, v_raw = pm_ref[:, 0:256], pm_ref[:, 256:512]
        lng, lnb = lng_ref[...], lnb_ref[...]
        u, gelu_u_vjp = jax.vjp(_gelu, u_raw)
        vn, norm_vjp = jax.vjp(lambda v_, g_, b_: _layernorm(_gelu(v_), g_, b_), v_raw, lng, lnb)
        vnb = vn.astype(BF16)
        dsg = dsg_ref[...]
        grp = _lane_group((SG_CHUNK, BR_W))
        tri = (lax.broadcasted_iota(jnp.int32, (SG_CHUNK, SG_CHUNK), 1)
               <= lax.broadcasted_iota(jnp.int32, (SG_CHUNK, SG_CHUNK), 0))
        du_parts, dvn_parts = [], []
        dsgb = jnp.zeros((SG_CHUNK, BR_W), F32)
        dwm = [jnp.zeros((SG_CHUNK, SG_CHUNK), F32) for _ in range(SG_GROUPS)]
        for c in range(t // SG_CHUNK):
            rows = slice(c * SG_CHUNK, (c + 1) * SG_CHUNK)
            mix = _sg_mix(wm_ref, vnb[rows], sgb_ref[...])
            du_parts.append(dsg[rows] * mix)
            ds = dsg[rows] * u[rows]
            dsgb = dsgb + ds
            dsb = [jnp.where(grp == g, ds, 0.0).astype(BF16) for g in range(SG_GROUPS)]
            for g in range(SG_GROUPS):
                dwm[g] = dwm[g] + _dot(dsb[g], vnb[rows], NT)
            dvn_parts.append(_group_select([_dot(wmt_ref[g], dsb[g]) for g in range(SG_GROUPS)]))
        (du_raw,) = gelu_u_vjp(jnp.concatenate(du_parts, axis=0))
        dv_raw, dlng, dlnb = norm_vjp(jnp.concatenate(dvn_parts, axis=0))
        dpm_ref[:, 0:256] = du_raw.astype(BF16)
        dpm_ref[:, 256:512] = dv_raw.astype(BF16)
        _accumulate(dlng_ref, dlng, first)
        _accumulate(dlnb_ref, dlnb, first)
        _accumulate(dsgb_ref, dsgb, first)
        for g in range(SG_GROUPS):
            _accumulate(dwm_ref.at[g], jnp.where(tri, dwm[g], 0.0), first)

        xin, bg, cg = pm_ref[:, 512:768], pm_ref[:, 768:1024], pm_ref[:, 1024:1280]
        z = cg * xin
        zext = jnp.concatenate([before[:, 1024:1280] * before[:, 512:768], z], axis=0)
        z1, z2 = _shift_down(zext, 1, t), _shift_down(zext, 2, t)
        w0, w1, w2 = cw_ref[0:1, :], cw_ref[1:2, :], cw_ref[2:3, :]
        dcv = dcv_ref[...]
        y = w0 * z2 + w1 * z1 + w2 * z
        dy = dcv * bg
        dyext = jnp.concatenate([dy, jnp.where(i < n - 1, dcva_ref[...], 0.0) * after[:, 768:1024]], axis=0)
        dz = w2 * dy + w1 * _shift_up(dyext, 1, t) + w0 * _shift_up(dyext, 2, t)
        dpm_ref[:, 512:768] = (dz * cg).astype(BF16)
        dpm_ref[:, 768:1024] = (dcv * y).astype(BF16)
        dpm_ref[:, 1024:1280] = (dz * xin).astype(BF16)
        dcw = jnp.concatenate([jnp.sum(dy * z2, axis=0, keepdims=True), jnp.sum(dy * z1, axis=0, keepdims=True),
                               jnp.sum(dy * z, axis=0, keepdims=True)], axis=0)
        _accumulate(dcw_ref, jnp.concatenate([dcw, jnp.zeros((8 - CONV_K, BR_W), F32)], axis=0), first)

        p = pm_ref[:, 1280:1536]
        tok = i * t + lax.broadcasted_iota(jnp.int32, (t, 1), 0)
        sums = _pool_windows(jnp.concatenate([before[:, 1280:1536], p], axis=0))
        pooled = _group_select([sw[HALO:HALO + t] / cnt - p for sw, cnt in zip(sums, _pool_counts(tok))]).astype(BF16)
        mixed = _dot(pooled, wp_ref[...])
        dpl = dpl_ref[...]
        ps = ps_ref[...]
        dmix = (dpl * ps).astype(BF16)
        _accumulate(dps_ref, jnp.sum(dpl * mixed, axis=0, keepdims=True), first)
        _accumulate(dwp_ref, _dot(pooled, dmix, TN), first)
        dmix_after = (jnp.where(i < n - 1, dpla_ref[...], 0.0) * ps).astype(BF16)
        dpo = _dot(dmix, wp_ref[...], NT)
        dpo_ext = jnp.concatenate([dpo, _dot(dmix_after, wp_ref[...], NT)], axis=0)
        tok_ext = i * t + lax.broadcasted_iota(jnp.int32, (t + HALO, 1), 0)
        dp_groups = []
        for g, (win, cnt) in enumerate(zip(POOL_WINDOWS, _pool_counts(tok_ext))):
            e = dpo_ext / cnt
            acc = e
            span = 1
            while span < win:
                acc = acc + pltpu.roll(acc, t + HALO - span, 0)
                span *= 2
            dp_groups.append(acc[0:t] - dpo)
        dpm_ref[:, 1280:1536] = _group_select(dp_groups).astype(BF16)

    row_blk = lambda w: pl.BlockSpec((t, w), lambda i: (i, 0))
    after_blk = lambda w: pl.BlockSpec((HALO, w), lambda i: (jnp.minimum((i + 1) * nb, n * nb - 1), 0))
    return _pcall(
        body, name="mixers_bwd", grid=(n,),
        in_specs=[row_blk(PM_W), _halo_before(t), after_blk(PM_W), row_blk(BR_W), row_blk(BR_W), after_blk(BR_W),
                  row_blk(BR_W), after_blk(BR_W), _whole(ln_g.shape), _whole(ln_b.shape), _whole(wm.shape),
                  _whole(wmt.shape), _whole(sgb.shape), _whole(conv_w.shape), _whole(wp.shape), _whole(pscale.shape)],
        out_specs=[row_blk(PM_W), _acc((1, BR_W)), _acc((1, BR_W)), _acc((SG_GROUPS, SG_CHUNK, SG_CHUNK)),
                   _acc((SG_CHUNK, BR_W)), _acc((8, BR_W)), _acc((BR_W, BR_W)), _acc((1, BR_W))],
        out_shape=[_sds((s, PM_W), BF16), _sds((1, BR_W), F32), _sds((1, BR_W), F32),
                   _sds((SG_GROUPS, SG_CHUNK, SG_CHUNK), F32), _sds((SG_CHUNK, BR_W), F32), _sds((8, BR_W), F32),
                   _sds((BR_W, BR_W), F32), _sds((1, BR_W), F32)],
        vmem=VMEM_LIMIT)(pm, pm, pm, dbsg, dbcv, dbcv, dbpl, dbpl, ln_g, ln_b, wm, wmt, sgb, conv_w, wp, pscale)


def _head_delta(o, do):
    rows = o.shape[0]
    prod = o.astype(F32) * do
    lane = lax.broadcasted_iota(jnp.int32, (rows, LANES), 1)
    out = []
    for h in range(N_HEADS):
        pair = prod[:, (h // 2) * LANES:(h // 2 + 1) * LANES]
        mine = (lane < D_V) if h % 2 == 0 else (lane >= D_V)
        out.append(jnp.broadcast_to(jnp.sum(jnp.where(mine, pair, 0.0), axis=1, keepdims=True), (rows, LANES)))
    return out


def _head_do(do, h):
    lane = lax.broadcasted_iota(jnp.int32, (do.shape[0], LANES), 1)
    mine = (lane < D_V) if h % 2 == 0 else (lane >= D_V)
    pair = do[:, (h // 2) * LANES:(h // 2 + 1) * LANES]
    return jnp.where(mine, pair, 0.0).astype(BF16)


def _flash_bwd(q, k, v, o, do, lse, t, riding):
    s = v.shape[0]
    n = s // t
    qi_tab, kj_tab = _causal_steps(n, key_major=True)
    n_steps = int(qi_tab.shape[0])
    whole = slice(0, t)

    def body(qi_ref, kj_ref, q_ref, k_ref, v_ref, o_ref, do_ref, lse_ref, x_ref, dq_ref, dk_ref, dv_ref, land_ref,
             dv_scr, send_sems, recv_sems, local_sem):
        step_id = pl.program_id(0)
        qi, kj = qi_ref[step_id], kj_ref[step_id]
        at_end = _ride_along(riding, step_id, n_steps, (x_ref, land_ref, send_sems, recv_sems, local_sem))

        @pl.when(step_id == 0)
        def _():
            dq_ref[...] = jnp.zeros(dq_ref.shape, F32)

        @pl.when(qi == kj)
        def _():
            dk_ref[...] = jnp.zeros(dk_ref.shape, F32)
            dv_scr[...] = jnp.zeros(dv_scr.shape, F32)

        def step(masked, qr, kr):
            dov = do_ref[qr, :]
            delta = _head_delta(o_ref[qr, :], dov)
            reps = (kr.stop - kr.start) // LANES
            q_rows = pl.ds(pl.multiple_of(qi * t + qr.start, t // 2), qr.stop - qr.start)
            for h in range(N_HEADS):
                qh, kh = q_ref[h, qr, :], k_ref[h, kr, :]
                p = jnp.exp(_causal_scores(qh, kh, masked) - jnp.tile(lse_ref[h, qr, :], (1, reps)))
                pair = slice((h // 2) * LANES, (h // 2 + 1) * LANES)
                dv_scr[h, kr, :] += _dot(p.astype(BF16), dov[:, pair].astype(BF16), TN)
                dp = _dot(_head_do(dov, h), v_ref[kr, pair], NT)
                ds = (p * (dp - jnp.tile(delta[h], (1, reps))) * SM_SCALE).astype(BF16)
                dk_ref[h, kr, :] += _dot(ds, qh, TN)
                dq_ref[h, q_rows, :] += _dot(ds, kh)

        @pl.when(qi > kj)
        def _():
            step(False, whole, whole)

        @pl.when(qi == kj)
        def _():
            for masked, qr, kr in _diagonal_parts(t):
                step(masked, qr, kr)

        @pl.when(qi == n - 1)
        def _():
            lane = lax.broadcasted_iota(jnp.int32, (t, LANES), 1)
            for pr in range(N_HEADS // 2):
                dv_ref[:, pr * LANES:(pr + 1) * LANES] = jnp.where(lane < D_V, dv_scr[2 * pr], dv_scr[2 * pr + 1]).astype(BF16)

        at_end()

    q_blk = pl.BlockSpec((N_HEADS, t, HEAD_PAD), lambda i, qi, kj: (0, qi[i], 0))
    k_blk = pl.BlockSpec((N_HEADS, t, HEAD_PAD), lambda i, qi, kj: (0, kj[i], 0))
    v_blk = pl.BlockSpec((t, BR_W), lambda i, qi, kj: (kj[i], 0))
    o_blk = pl.BlockSpec((t, BR_W), lambda i, qi, kj: (qi[i], 0))
    dq_sds = _sds((N_HEADS, s, HEAD_PAD), F32)
    return _pcall(
        body, name="flash_bwd_exchanging", grid=(n_steps,), prefetch=2,
        in_specs=[q_blk, k_blk, v_blk, o_blk, o_blk, q_blk, HBM_REF],
        out_specs=[_whole(dq_sds.shape), k_blk, v_blk, HBM_REF],
        out_shape=[dq_sds, dq_sds, _sds((s, BR_W), BF16), riding[2]],
        scratch=[pltpu.VMEM((N_HEADS, t, LANES), F32)] + list(EXCHANGE_SEMS),
        vmem=VMEM_LIMIT)(qi_tab, kj_tab, q, k, v, o, do, lse, riding[1])


def _attn_prep_bwd(pa, dq, dk, dv, cosf, sins, qn, kvn, wq, wqs, wk, wv, t):
    s = pa.shape[0]

    def body(pa_ref, dq_ref, dk_ref, dv_ref, cos_ref, sin_ref, qn_ref, kvn_ref, wq_ref, wqs_ref, wk_ref, wv_ref,
             dpa_ref, dwq_ref, dwqs_ref, dwk_ref, dwv_ref, dqn_ref, dkvn_ref):
        first = pl.program_id(0) == 0
        cosv, sinv = cos_ref[...], sin_ref[...]
        cq, ckv = pa_ref[:, 0:Q_RANK], pa_ref[:, Q_RANK:Q_RANK + KV_RANK]
        cqn = _rms(cq, qn_ref[...]).astype(BF16)
        ckvn = _rms(ckv, kvn_ref[...]).astype(BF16)
        dvv = dv_ref[...]
        _accumulate(dwv_ref, _dot(dvv, ckvn, TN), first)
        dk_rope = dk_ref[0]
        for h in range(1, N_HEADS):
            dk_rope = dk_rope + dk_ref[h]
        dqa = jnp.concatenate([(dq_ref[h] * cosv).astype(BF16) for h in range(N_HEADS)], axis=1)
        dqs = jnp.concatenate([(dq_ref[h] * sinv).astype(BF16) for h in range(N_HEADS)], axis=1)
        dkb = jnp.concatenate([dk_ref[h].astype(BF16) for h in range(N_HEADS)], axis=1)
        _accumulate(dwq_ref, _dot(dqa, cqn, TN), first)
        _accumulate(dwqs_ref, _dot(dqs, cqn, TN), first)
        _accumulate(dwk_ref, _dot(dkb, ckvn, TN), first)
        dcqn = _dot(dqa, wq_ref[...]) + _dot(dqs, wqs_ref[...])
        dckvn = _dot(dvv, wv_ref[...]) + _dot(dkb, wk_ref[...])
        dcq, dqn = _rms_bwd(cq, qn_ref[...], dcqn)
        dckv, dkvn = _rms_bwd(ckv, kvn_ref[...], dckvn)
        _accumulate(dqn_ref, dqn, first)
        _accumulate(dkvn_ref, dkvn, first)
        dpa_ref[:, 0:Q_RANK] = dcq.astype(BF16)
        dpa_ref[:, Q_RANK:Q_RANK + KV_RANK] = dckv.astype(BF16)
        dpa_ref[:, 384:512] = (dk_rope * cosv).astype(BF16)
        dpa_ref[:, 512:640] = (dk_rope * sinv).astype(BF16)

    head_blk = pl.BlockSpec((N_HEADS, t, HEAD_PAD), lambda i: (0, i, 0))
    return _pcall(
        body, name="attn_prep_bwd", grid=(s // t,),
        in_specs=[_rows(t, PA_W), head_blk, head_blk, _rows(t, BR_W), _rows(t, LANES), _rows(t, LANES),
                  _whole(qn.shape), _whole(kvn.shape), _whole(wq.shape), _whole(wqs.shape), _whole(wk.shape), _whole(wv.shape)],
        out_specs=[_rows(t, PA_W), _acc(wq.shape), _acc(wqs.shape), _acc(wk.shape), _acc(wv.shape),
                   _acc(qn.shape), _acc(kvn.shape)],
        out_shape=[_sds((s, PA_W), BF16), _sds(wq.shape, F32), _sds(wqs.shape, F32), _sds(wk.shape, F32),
                   _sds(wv.shape, F32), _sds(qn.shape, F32), _sds(kvn.shape, F32)],
        vmem=VMEM_LIMIT)(pa, dq, dk, dv, cosf, sins, qn, kvn, wq, wqs, wk, wv)


def _inproj_bwd(x, g1, dxmid, dpa, dpm, dpres, w_in, wg, t):
    s = x.shape[0]
    n = s // t

    def body(x_ref, g1_ref, dxm_ref, dpa_ref, dpm_ref, d0_ref, d1_ref, d2_ref, d3_ref, w_ref, wg_ref,
             dx_ref, dg1_ref, dwa_ref, dwm_ref, acc_a, acc_m):
        first, last = pl.program_id(0) == 0, pl.program_id(0) == n - 1
        xv, g1v = x_ref[...], g1_ref[...]
        hb = _rms(xv, g1v).astype(BF16)
        dpav, dpmv = dpa_ref[...], dpm_ref[...]
        dh = _dot(dpav, w_ref[:PA_W, :]) + _dot(dpmv, w_ref[PA_W:, :])
        for k, d_ref in enumerate((d0_ref, d1_ref, d2_ref, d3_ref)):
            dh = dh + _dot(d_ref[...], wg_ref[k * D_MODEL:(k + 1) * D_MODEL, :])
        dx, dg1 = _rms_bwd(xv, g1v, dh)
        dx_ref[...] = dxm_ref[...] + dx
        _accumulate(dg1_ref, dg1, first)
        _accumulate_out(acc_a, dwa_ref, _dot(dpav, hb, TN), first, last)
        _accumulate_out(acc_m, dwm_ref, _dot(dpmv, hb, TN), first, last)

    return _pcall(
        body, name="inproj_bwd", grid=(n,),
        in_specs=[_rows(t, D_MODEL), _whole(g1.shape), _rows(t, D_MODEL), _rows(t, PA_W), _rows(t, PM_W)] +
                 [_rows(t, D_MODEL)] * 4 + [_whole(w_in.shape), _whole(wg.shape)],
        out_specs=[_rows(t, D_MODEL), _acc((1, D_MODEL)), _acc((PA_W, D_MODEL)), _acc((PM_W, D_MODEL))],
        out_shape=[_sds((s, D_MODEL), F32), _sds((1, D_MODEL), F32), _sds((PA_W, D_MODEL), BF16), _sds((PM_W, D_MODEL), BF16)],
        scratch=[pltpu.VMEM((PA_W, D_MODEL), F32), pltpu.VMEM((PM_W, D_MODEL), F32)],
        vmem=VMEM_LIMIT)(x, g1, dxmid, dpa, dpm, *dpres, w_in, wg)


def _my_place():
    return lax.axis_index("x"), lax.axis_index("y"), lax.axis_index("c")


def _flip(place, k):
    x, y, c = place
    return (1 - x if k & 4 else x, 1 - y if k & 2 else y, 1 - c if k & 1 else c)


def _rank(place):
    x, y, c = place
    return 4 * x + 2 * y + c


EXCHANGE_SEMS = (pltpu.SemaphoreType.DMA((7,)), pltpu.SemaphoreType.DMA((7,)), pltpu.SemaphoreType.DMA)
HBM_REF = pl.BlockSpec(memory_space=pl.ANY)


def _gather_plan(x_ref, out_ref, send_sems, recv_sems, local_sem):
    me = _my_place()
    sibling = _flip(me, 1)
    chips = (4, 2, 6)

    def copy(k, src_place, to, src=None):
        slot = out_ref.at[_rank(src_place)]
        return pltpu.make_async_remote_copy(
            src_ref=slot if src is None else src, dst_ref=slot, send_sem=send_sems.at[k], recv_sem=recv_sems.at[k],
            device_id=to, device_id_type=MESH_ID)

    mine = pltpu.make_async_copy(x_ref, out_ref.at[_rank(me)], local_sem)
    first = [copy(0, me, sibling, src=x_ref)] + [copy(1 + j, me, _flip(me, kc), src=x_ref) for j, kc in enumerate(chips)]
    passed = [copy(4 + j, _flip(me, kc), sibling) for j, kc in enumerate(chips)]

    def start():
        mine.start()
        for cp in first:
            cp.start()

    def forward():
        for j, kc in enumerate(chips):
            copy(1 + j, _flip(me, kc), me).wait_recv()
            passed[j].start()

    def finish():
        copy(0, sibling, me).wait_recv()
        for j, kc in enumerate(chips):
            copy(4 + j, _flip(sibling, kc), me).wait_recv()
        for cp in first + passed:
            cp.wait_send()
        mine.wait()

    return start, forward, finish


def _exchange_plan(x_ref, out_ref, send_sems, recv_sems, local_sem):
    me = _my_place()
    my_rank = _rank(me)
    mine = pltpu.make_async_copy(x_ref.at[my_rank], out_ref.at[my_rank], local_sem)
    sends = [pltpu.make_async_remote_copy(
        src_ref=x_ref.at[_rank(_flip(me, k))], dst_ref=out_ref.at[my_rank], send_sem=send_sems.at[k - 1],
        recv_sem=recv_sems.at[k - 1], device_id=_flip(me, k), device_id_type=MESH_ID) for k in range(1, N_DEV)]

    def start():
        mine.start()
        for cp in sends:
            cp.start()

    def forward():
        pass

    def finish():
        for k in range(1, N_DEV):
            slot = out_ref.at[_rank(_flip(me, k))]
            pltpu.make_async_remote_copy(
                src_ref=slot, dst_ref=slot, send_sem=send_sems.at[k - 1], recv_sem=recv_sems.at[k - 1],
                device_id=_flip(me, k), device_id_type=MESH_ID).wait_recv()
        for cp in sends:
            cp.wait_send()
        mine.wait()

    return start, forward, finish


def _exchange_alone(plan, src, out_sds, name):
    def body(x_ref, out_ref, send_sems, recv_sems, local_sem):
        start, forward, finish = plan(x_ref, out_ref, send_sems, recv_sems, local_sem)
        start()
        forward()
        finish()

    return pl.pallas_call(body, name=name, out_shape=out_sds, in_specs=[HBM_REF], out_specs=HBM_REF,
                          scratch_shapes=list(EXCHANGE_SEMS))(src)


def _all_gather(shard, name):
    return _exchange_alone(_gather_plan, shard, _sds((N_DEV,) + shard.shape, shard.dtype), name)


def _all_to_all(blocks, name):
    return _exchange_alone(_exchange_plan, blocks, _sds(blocks.shape, blocks.dtype), name)


def _riding(plan, src, out_sds):
    return (plan, src, out_sds)


def _ride_along(riding, step_id, n_steps, refs):
    start, forward, finish = riding[0](*refs)

    @pl.when(step_id == 0)
    def _():
        start()

    @pl.when(step_id == (3 * n_steps) // 4)
    def _():
        forward()

    def at_end():
        @pl.when(step_id == n_steps - 1)
        def _():
            finish()

    return at_end


def _sum_slots(parts, name):
    _, r, c = parts.shape
    t = math.gcd(r, ROW_PAD)

    def body(p_ref, o_ref):
        acc = p_ref[0].astype(F32)
        for d in range(1, N_DEV):
            acc = acc + p_ref[d].astype(F32)
        o_ref[...] = acc

    return _pcall(
        body, name=name, grid=(r // t,),
        in_specs=[pl.BlockSpec((N_DEV, t, c), lambda i: (0, i, 0))], out_specs=_rows(t, c),
        out_shape=_sds((r, c), F32))(parts)


def _adamw(w, g, m, v, name):
    r, c = w.shape
    t = r if r <= ROW_PAD else math.gcd(r, ROW_PAD)

    def body(w_ref, g_ref, m_ref, v_ref, d_ref, nm_ref, nv_ref):
        gv = g_ref[...]
        nm = ADAM_B1 * m_ref[...] + (1.0 - ADAM_B1) * gv
        nv = ADAM_B2 * v_ref[...] + (1.0 - ADAM_B2) * jnp.square(gv)
        m_hat = nm / (1.0 - ADAM_B1 ** ADAM_STEP)
        v_hat = nv / (1.0 - ADAM_B2 ** ADAM_STEP)
        d_ref[...] = -ADAM_LR * (m_hat / (jnp.sqrt(v_hat) + ADAM_EPS) + ADAM_WD * w_ref[...])
        nm_ref[...] = nm
        nv_ref[...] = nv

    return _pcall(
        body, name=name, grid=(r // t,),
        in_specs=[_rows(t, c)] * 4, out_specs=[_rows(t, c)] * 3, out_shape=[_sds((r, c), F32)] * 3)(w, g, m, v)


BEFORE_ATTN = ("w_in", "w_uq", "w_ukv")
AFTER_ATTN = ("w_br_mla", "w_br_sg", "w_br_conv", "w_br_pool", "w_out", "w_ff1", "w_ff2", "conv_w")
SHARDED = BEFORE_ATTN + AFTER_ATTN
ROW_SHARDED = ("w_out", "w_ff2")
REPLICATED = ("norm_mix_pre", "gate_b", "q_norm", "kv_norm", "sg_ln_g", "sg_ln_b", "sg_w", "sg_b", "pool_w",
              "pool_scale", "norm_mix_post", "norm_ffn_pre", "norm_ffn_post")
ROW_PAD = 256
GRAD_ROW_PAD = 64
PART_ROWS = 16


def _pack_rows(arrays, dtype, multiple, lead=()):
    rows, offsets, at = [], [], 0
    zero_pad = ((0, 0),) * len(lead)
    for a in arrays:
        flat = a.reshape(lead + (-1,)).astype(dtype)
        n = -(-flat.shape[-1] // D_MODEL)
        n_pad = -(-n // PART_ROWS) * PART_ROWS
        flat = jnp.pad(flat, zero_pad + ((0, n_pad * D_MODEL - flat.shape[-1]),))
        rows.append(flat.reshape(lead + (n_pad, D_MODEL)))
        offsets.append((at, n))
        at += n_pad
    pad = -at % multiple
    if pad:
        rows.append(jnp.zeros(lead + (pad, D_MODEL), dtype))
    return jnp.concatenate(rows, axis=len(lead)), offsets


def _unpack_rows(buf, offsets, shapes):
    lead = buf.shape[:-2]
    out = []
    for (at, n), shape in zip(offsets, shapes):
        size = math.prod(shape)
        flat = buf[..., at:at + n, :].reshape(lead + (n * D_MODEL,))[..., :size]
        out.append(flat.reshape(lead + tuple(shape)))
    return out


def _pack_shards(p, depth, names):
    arrays = []
    for n in names:
        a = p[n][0] if n in ROW_SHARDED else jnp.swapaxes(p[n][0], -1, -2)
        arrays.append(lax.bitcast_convert_type(a, BF16) if n == "conv_w" else a.astype(BF16))
    packed, offsets = _pack_rows(arrays, BF16, GRAD_ROW_PAD, lead=(depth,))
    return packed, offsets, [a.shape[1:] for a in arrays]


def _unpack_layer_weights(names, gathered, offsets, shapes):
    full = {}
    for n, gth in zip(names, _unpack_rows(gathered, offsets, shapes)):
        if n == "conv_w":
            gth = lax.bitcast_convert_type(gth, F32)
        full[n] = gth.reshape((-1,) + gth.shape[2:])
        if n == "conv_w":
            full[n] = full[n].T
    return full


def _owner_major(grad_sharded_dim_first):
    a = grad_sharded_dim_first
    return a.reshape((N_DEV, a.shape[0] // N_DEV) + a.shape[1:])


def _owner_block_shape(name, shard_shape):
    k, n = shard_shape[-2:]
    return (k, n) if name in ROW_SHARDED else (n, k)


def _natural_shard(name, blocks):
    return blocks if name in ROW_SHARDED else jnp.swapaxes(blocks, -1, -2)


def _swap_halves(a, axis=-1):
    lo, hi = jnp.split(a, 2, axis=axis)
    return jnp.concatenate([hi, lo], axis=axis)


def _pad_rows(a, top, total):
    return jnp.pad(a, ((0, 0),) * (a.ndim - 2) + ((top, total - top - a.shape[-2]), (0, 0)))


def _weights_before_attn(full, small, l):
    w_in = full["w_in"]
    k_r = w_in[384:416]
    w_in_k = jnp.concatenate(
        [w_in[0:384], _pad_rows(k_r, D_NOPE, HEAD_PAD), _pad_rows(_swap_halves(k_r, axis=0), D_NOPE, HEAD_PAD), w_in[416:1952]], axis=0)
    w_uq = full["w_uq"].reshape(N_HEADS, D_NOPE + D_ROPE, Q_RANK)
    wq = _pad_rows(w_uq, 0, HEAD_PAD).reshape(N_HEADS * HEAD_PAD, Q_RANK)
    wqs = _pad_rows(_swap_halves(w_uq[:, D_NOPE:], axis=1), D_NOPE, HEAD_PAD).reshape(N_HEADS * HEAD_PAD, Q_RANK)
    w_ukv = full["w_ukv"].reshape(N_HEADS, D_NOPE + D_V, KV_RANK)
    wk = _pad_rows(w_ukv[:, :D_NOPE], 0, HEAD_PAD).reshape(N_HEADS * HEAD_PAD, KV_RANK)
    wv = w_ukv[:, D_NOPE:].reshape(N_HEADS * D_V, KV_RANK)
    tri = jnp.tril(jnp.ones((SG_CHUNK, SG_CHUNK), bool))
    wm = jnp.where(tri, small["sg_w"][l], 0.0)
    pool_w = small["pool_w"][l]
    wp = jnp.zeros((BR_W, BR_W), F32)
    for g in range(len(POOL_WINDOWS)):
        wp = wp.at[g * POOL_GROUP:(g + 1) * POOL_GROUP, g * POOL_GROUP:(g + 1) * POOL_GROUP].set(pool_w[g])
    vec = lambda name: small[name][l][None, :]
    return dict(
        w_in=w_in_k, wg=w_in[1952:], bg=vec("gate_b"), g1=vec("norm_mix_pre"), qn=vec("q_norm"), kvn=vec("kv_norm"),
        wq=wq, wqs=wqs, wk=wk, wv=wv,
        ln_g=vec("sg_ln_g"), ln_b=vec("sg_ln_b"), wm=wm.astype(BF16), wmt=jnp.swapaxes(wm, 1, 2).astype(BF16),
        sgb=jnp.repeat(small["sg_b"][l].T, POOL_GROUP, axis=1), wp=wp.astype(BF16), pscale=vec("pool_scale"),
        g2=vec("norm_mix_post"), g3=vec("norm_ffn_pre"), g4=vec("norm_ffn_post"))


def _weights_after_attn(full):
    return dict(
        conv_w=full["conv_w"], wbr=jnp.stack([full["w_br_mla"], full["w_br_sg"], full["w_br_conv"], full["w_br_pool"]]),
        wout=full["w_out"], w1=full["w_ff1"], w2=full["w_ff2"])


def _grads_after_attn(g):
    own = _owner_major
    return dict(
        w_br_mla=own(g["dwbr"][0]), w_br_sg=own(g["dwbr"][1]), w_br_conv=own(g["dwbr"][2]), w_br_pool=own(g["dwbr"][3]),
        w_out=own(g["dwout"]), w_ff1=own(g["dw1"]), w_ff2=own(g["dw2"]), conv_w=own(g["dcw"][:CONV_K].T))


def _grads_before_attn(g):
    dwa = g["dwa"]
    rope_rows = slice(D_NOPE, D_NOPE + D_ROPE)
    d_kr = dwa[384:512][rope_rows].astype(F32) + _swap_halves(dwa[512:640][rope_rows], axis=0).astype(F32)
    w_in = jnp.concatenate([dwa[0:384], d_kr.astype(BF16), g["dwm_in"]] + list(g["dwg"]), axis=0)
    dwq, dwqs = (g[n].reshape(N_HEADS, HEAD_PAD, Q_RANK) for n in ("dwq", "dwqs"))
    d_rope = dwq[:, rope_rows] + _swap_halves(dwqs[:, rope_rows], axis=1)
    w_uq = jnp.concatenate([dwq[:, :D_NOPE], d_rope], axis=1).reshape(N_HEADS * (D_NOPE + D_ROPE), Q_RANK)
    dwk = g["dwk"].reshape(N_HEADS, HEAD_PAD, KV_RANK)
    w_ukv = jnp.concatenate([dwk[:, :D_NOPE], g["dwv"].reshape(N_HEADS, D_V, KV_RANK)], axis=1)
    w_ukv = w_ukv.reshape(N_HEADS * (D_NOPE + D_V), KV_RANK)
    pool_w = jnp.stack([g["dwp"][i * POOL_GROUP:(i + 1) * POOL_GROUP, i * POOL_GROUP:(i + 1) * POOL_GROUP]
                        for i in range(len(POOL_WINDOWS))])
    sg_b = g["dsgb"].reshape(SG_CHUNK, SG_GROUPS, POOL_GROUP).sum(axis=-1).T
    own = _owner_major
    return dict(
        w_in=own(w_in), w_uq=own(w_uq), w_ukv=own(w_ukv),
        norm_mix_pre=g["dg1"][0], gate_b=jnp.concatenate([b[0] for b in g["dbg"]]), q_norm=g["dqn"][0], kv_norm=g["dkvn"][0],
        sg_ln_g=g["dlng"][0], sg_ln_b=g["dlnb"][0], sg_w=g["dwm"], sg_b=sg_b, pool_w=pool_w, pool_scale=g["dps"][0],
        norm_mix_post=g["dg2"][0], norm_ffn_pre=g["dg3"][0], norm_ffn_post=g["dg4"][0])


def _rope_tables(positions):
    inv_freq = ROPE_BASE ** (-jnp.arange(0, D_ROPE, 2, dtype=F32) / D_ROPE)
    ang = positions.astype(F32)[:, None] * inv_freq
    cos, sin = jnp.cos(ang), jnp.sin(ang)
    s = positions.shape[0]
    cosf = jnp.concatenate([jnp.ones((s, D_NOPE), F32), cos, cos, jnp.zeros((s, HEAD_PAD - D_NOPE - D_ROPE), F32)], axis=1)
    sins = jnp.concatenate([jnp.zeros((s, D_NOPE), F32), -sin, sin, jnp.zeros((s, HEAD_PAD - D_NOPE - D_ROPE), F32)], axis=1)
    return cosf, sins


def _layer_fwd(x, w, weights_after_attn, cosf, sins, tiles, riding):
    t, ta = tiles["tok"], tiles["attn"]
    hb, pa, pm = _inproj_fwd(x, w["g1"], w["w_in"], tiles["wgrad"])
    q, k, v = _attn_prep_fwd(pa, cosf, sins, w["qn"], w["kvn"], w["wq"], w["wqs"], w["wk"], w["wv"], tiles["wgrad"])
    o, lse, landed = _flash_fwd(q, k, v, ta, riding)
    w = {**w, **weights_after_attn(landed)}
    bsg, bcv, bpl = _mixers_fwd(pm, w["ln_g"], w["ln_b"], w["wm"], w["sgb"], w["conv_w"], w["wp"], w["pscale"], tiles["wgrad"])
    x_mid, merged, tt = _merge_fwd(x, hb, (o, bsg, bcv, bpl), w["wg"], w["bg"], w["wbr"], w["wout"], w["g2"], t)
    x_out, f = _ffn_fwd(x_mid, w["g3"], w["w1"], w["w2"], w["g4"], t)
    saved = dict(x=x, hb=hb, pa=pa, pm=pm, q=q, k=k, v=v, o=o, lse=lse, bsg=bsg, bcv=bcv, bpl=bpl, x_mid=x_mid,
                 merged=merged, tt=tt, f=f)
    return x_out, saved, w, landed


def _layer_bwd(dxo, a, w, cosf, sins, tiles, pending):
    t, ta, tb, tw = tiles["tok"], tiles["attn"], tiles["inproj_bwd"], tiles["wgrad"]
    g = {}
    h2, df, g["dg4"] = _ffn_bwd_norms(a["x_mid"], a["f"], dxo, w["g3"], w["g4"], tw)
    da, g["dw1"], g["dw2"] = _ffn_bwd_weights(h2, df, w["w1"], w["w2"], tw)
    dxmid, g["dg3"] = _ffn_bwd_input(da, w["w1"], a["x_mid"], dxo, w["g3"], t)
    dm, g["dwout"], g["dg2"] = _merge_bwd_out(a["tt"], dxmid, w["g2"], w["wout"], a["merged"], tw)
    dbrs, dpres, g["dwg"], g["dbg"], g["dwbr"] = [], [], [], [], []
    for k, br in enumerate((a["o"], a["bsg"], a["bcv"], a["bpl"])):
        dbr, dpre, dwg, dbg, dwbr = _merge_bwd_branch(k, a["hb"], br, dm, w["wg"], w["bg"], w["wbr"], tw)
        dbrs.append(dbr); dpres.append(dpre); g["dwg"].append(dwg); g["dbg"].append(dbg); g["dwbr"].append(dwbr)
    dpm, g["dlng"], g["dlnb"], g["dwm"], g["dsgb"], g["dcw"], g["dwp"], g["dps"] = _mixers_bwd(
        a["pm"], dbrs[1], dbrs[2], dbrs[3], w["ln_g"], w["ln_b"], w["wm"], w["wmt"], w["sgb"], w["conv_w"], w["wp"], w["pscale"], tw)
    after = _grads_after_attn(g)
    send, after_offsets = _pack_rows([after[n] for n in AFTER_ATTN], BF16, GRAD_ROW_PAD, lead=(N_DEV,))
    after_rows = send.shape[1]
    if pending is not None:
        send = jnp.concatenate([send, pending], axis=1)
    riding = _riding(_exchange_plan, send, _sds(send.shape, send.dtype))
    dq, dk, dv, landed = _flash_bwd(a["q"], a["k"], a["v"], a["o"], dbrs[0], a["lse"], ta, riding)
    dpa, g["dwq"], g["dwqs"], g["dwk"], g["dwv"], g["dqn"], g["dkvn"] = _attn_prep_bwd(
        a["pa"], dq, dk, dv, cosf, sins, w["qn"], w["kvn"], w["wq"], w["wqs"], w["wk"], w["wv"], tw)
    dx, g["dg1"], g["dwa"], g["dwm_in"] = _inproj_bwd(a["x"], w["g1"], dxmid, dpa, dpm, dpres, w["w_in"], w["wg"], tb)
    return dx, _grads_before_attn(g), landed, (after_rows, after_offsets)


def _step(p, x, positions, loss_target):
    s = x.shape[0]
    depth = p["w_in"][0].shape[0]
    tiles = dict(tok=min(TOK_TILE, s), attn=min(ATTN_TILE, s), inproj_bwd=min(INPROJ_BWD_TILE, s),
                 wgrad=min(WGRAD_TILE, s))

    small = {n: p[n][0] for n in REPLICATED}
    cosf, sins = _rope_tables(positions)

    pack_b, off_b, shapes_b = _pack_shards(p, depth, BEFORE_ATTN)
    pack_a, off_a, shapes_a = _pack_shards(p, depth, AFTER_ATTN)
    rows_a = pack_a.shape[1]
    before = _all_gather(pack_b[0], "weight_all_gather")
    weights, acts = [], []
    h = x
    for l in range(depth):
        w = _weights_before_attn(_unpack_layer_weights(BEFORE_ATTN, before, off_b, shapes_b), small, l)
        src = pack_a[l] if l + 1 == depth else jnp.concatenate([pack_a[l], pack_b[l + 1]], axis=0)
        riding = _riding(_gather_plan, src, _sds((N_DEV,) + src.shape, src.dtype))

        def after(landed):
            return _weights_after_attn(_unpack_layer_weights(AFTER_ATTN, landed[:, :rows_a], off_a, shapes_a))

        h, saved, w, landed = _layer_fwd(h, w, after, cosf, sins, tiles, riding)
        weights.append(w)
        acts.append(saved)
        before = landed[:, rows_a:] if l + 1 < depth else None
    dh, loss_blk = _loss_head(h, loss_target, tiles["tok"])
    loss = lax.psum(loss_blk[0, 0], ("x", "y", "c"))

    grads_b = [None] * depth
    sum_a, sum_b = [None] * depth, [None] * depth
    pending = None
    for l in reversed(range(depth)):
        dh, grads_b[l], landed, (rows_ga, off_ga) = _layer_bwd(dh, acts[l], weights[l], cosf, sins, tiles, pending)
        summed = _sum_slots(landed, "grad_shard_sum")
        sum_a[l] = summed[:rows_ga]
        if pending is not None:
            sum_b[l + 1] = summed[rows_ga:]
        pending, off_gb = _pack_rows([grads_b[l][n] for n in BEFORE_ATTN], BF16, GRAD_ROW_PAD, lead=(N_DEV,))
    sum_b[0] = _sum_slots(_all_to_all(pending, "grad_all_to_all"), "grad_shard_sum")
    g_shard = {}
    for names, sums, offsets in ((AFTER_ATTN, sum_a, off_ga), (BEFORE_ATTN, sum_b, off_gb)):
        block_shapes = [_owner_block_shape(n, p[n][0].shape) for n in names]
        for n, b in zip(names, _unpack_rows(jnp.stack(sums), offsets, block_shapes)):
            g_shard[n] = _natural_shard(n, b)
    small_grads = [jnp.stack([grads_b[l][n] for l in range(depth)]) for n in REPLICATED]
    small_rows, small_offsets = _pack_rows(small_grads, F32, ROW_PAD)
    g_small_rows = _sum_slots(_all_gather(small_rows, "replicated_grad_all_gather"), "grad_replicated_sum")

    grad, delta, new_m, new_v = {}, {}, {}, {}
    for n in SHARDED:
        w, m, v = p[n]
        shape2 = (math.prod(w.shape[:-1]), w.shape[-1])
        grad[n] = g_shard[n]
        d, nm, nv = _adamw(w.reshape(shape2), grad[n].reshape(shape2), m.reshape(shape2), v.reshape(shape2), f"adamw_{n}")
        delta[n], new_m[n], new_v[n] = d.reshape(w.shape), nm.reshape(w.shape), nv.reshape(w.shape)
    packs = [_pack_rows([p[n][i] for n in REPLICATED], F32, ROW_PAD)[0] for i in range(3)]
    d, nm, nv = _adamw(packs[0], g_small_rows, packs[1], packs[2], "adamw_replicated")
    shapes = [p[n][0].shape for n in REPLICATED]
    for n, gg, dd, mm, vv in zip(REPLICATED, _unpack_rows(g_small_rows, small_offsets, shapes), _unpack_rows(d, small_offsets, shapes),
                                 _unpack_rows(nm, small_offsets, shapes), _unpack_rows(nv, small_offsets, shapes)):
        grad[n], delta[n], new_m[n], new_v[n] = gg, dd, mm, vv
    return loss, dh, grad, delta, new_m, new_v


WEIGHT_ORDER = ("norm_mix_pre", "w_in", "gate_b", "q_norm", "w_uq", "kv_norm", "w_ukv", "w_br_mla", "sg_ln_g", "sg_ln_b",
                "sg_w", "sg_b", "w_br_sg", "conv_w", "w_br_conv", "pool_w", "pool_scale", "w_br_pool", "w_out",
                "norm_mix_post", "norm_ffn_pre", "w_ff1", "w_ff2", "norm_ffn_post")


def kernel(x, positions, norm_mix_pre, w_in, gate_b, q_norm, w_uq, kv_norm, w_ukv, w_br_mla, sg_ln_g, sg_ln_b, sg_w, sg_b, w_br_sg, conv_w, w_br_conv, pool_w, pool_scale, w_br_pool, w_out, norm_mix_post, norm_ffn_pre, w_ff1, w_ff2, norm_ffn_post, loss_target, m_norm_mix_pre, m_w_in, m_gate_b, m_q_norm, m_w_uq, m_kv_norm, m_w_ukv, m_w_br_mla, m_sg_ln_g, m_sg_ln_b, m_sg_w, m_sg_b, m_w_br_sg, m_conv_w, m_w_br_conv, m_pool_w, m_pool_scale, m_w_br_pool, m_w_out, m_norm_mix_post, m_norm_ffn_pre, m_w_ff1, m_w_ff2, m_norm_ffn_post, v_norm_mix_pre, v_w_in, v_gate_b, v_q_norm, v_w_uq, v_kv_norm, v_w_ukv, v_w_br_mla, v_sg_ln_g, v_sg_ln_b, v_sg_w, v_sg_b, v_w_br_sg, v_conv_w, v_w_br_conv, v_pool_w, v_pool_scale, v_w_br_pool, v_w_out, v_norm_mix_post, v_norm_ffn_pre, v_w_ff1, v_w_ff2, v_norm_ffn_post):
    ws = (norm_mix_pre, w_in, gate_b, q_norm, w_uq, kv_norm, w_ukv, w_br_mla, sg_ln_g, sg_ln_b, sg_w, sg_b, w_br_sg, conv_w,
          w_br_conv, pool_w, pool_scale, w_br_pool, w_out, norm_mix_post, norm_ffn_pre, w_ff1, w_ff2, norm_ffn_post)
    ms = (m_norm_mix_pre, m_w_in, m_gate_b, m_q_norm, m_w_uq, m_kv_norm, m_w_ukv, m_w_br_mla, m_sg_ln_g, m_sg_ln_b, m_sg_w,
          m_sg_b, m_w_br_sg, m_conv_w, m_w_br_conv, m_pool_w, m_pool_scale, m_w_br_pool, m_w_out, m_norm_mix_post,
          m_norm_ffn_pre, m_w_ff1, m_w_ff2, m_norm_ffn_post)
    vs = (v_norm_mix_pre, v_w_in, v_gate_b, v_q_norm, v_w_uq, v_kv_norm, v_w_ukv, v_w_br_mla, v_sg_ln_g, v_sg_ln_b, v_sg_w,
          v_sg_b, v_w_br_sg, v_conv_w, v_w_br_conv, v_pool_w, v_pool_scale, v_w_br_pool, v_w_out, v_norm_mix_post,
          v_norm_ffn_pre, v_w_ff1, v_w_ff2, v_norm_ffn_post)
    p = {n: (w, m, v) for n, w, m, v in zip(WEIGHT_ORDER, ws, ms, vs)}
    loss, grad_x, grad, delta, new_m, new_v = _step(p, x[0], positions[0], loss_target[0])
    return (loss, grad_x[None], *[grad[n] for n in WEIGHT_ORDER], *[delta[n] for n in WEIGHT_ORDER],
            *[new_m[n] for n in WEIGHT_ORDER], *[new_v[n] for n in WEIGHT_ORDER])
```

```python
import functools
import math

import jax
import jax.numpy as jnp
from jax import lax
from jax.experimental import pallas as pl
from jax.experimental.pallas import tpu as pltpu

F32 = jnp.float32
BF16 = jnp.bfloat16

D_MODEL = 1024
N_HEADS = 4
D_NOPE = 64
D_ROPE = 32
D_V = 64
Q_RANK = 256
KV_RANK = 128
BR_W = 256
SG_CHUNK = 128
SG_GROUPS = 4
POOL_WINDOWS = (2, 4, 8, 16)
POOL_GROUP = 64
CONV_K = 3
D_FF = 4096
N_BRANCH = 4
N_IN = 6048
EPS = 1e-6
ROPE_BASE = 10000.0
ADAM_LR = 0.001
ADAM_B1 = 0.9
ADAM_B2 = 0.999
ADAM_EPS = 1e-08
ADAM_WD = 0.01
ADAM_STEP = 10

N_DEV = 8
LANES = 128
HEAD_PAD = 128
HALO = 16
PA_W = 640
PM_W = 6 * BR_W
VMEM_LIMIT = 56 * 1024 * 1024
SM_SCALE = (D_NOPE + D_ROPE) ** -0.5
TOK_TILE = 512
ATTN_TILE = 1024
WGRAD_TILE = 1024
INPROJ_BWD_TILE = 512

NN = (((1,), (0,)), ((), ()))
NT = (((1,), (1,)), ((), ()))
TN = (((0,), (0,)), ((), ()))
MESH_ID = pl.DeviceIdType.MESH


def _dot(a, b, dims=NN):
    return lax.dot_general(a, b, dims, preferred_element_type=F32)


def _pcall(body, *, name, grid, in_specs, out_specs, out_shape, scratch=(), vmem=None, prefetch=0):
    params = pltpu.CompilerParams(vmem_limit_bytes=vmem)
    if prefetch:
        spec = pltpu.PrefetchScalarGridSpec(num_scalar_prefetch=prefetch, grid=grid, in_specs=in_specs,
                                            out_specs=out_specs, scratch_shapes=scratch)
        return pl.pallas_call(body, name=name, grid_spec=spec, out_shape=out_shape, compiler_params=params)
    return pl.pallas_call(
        body, name=name, grid=grid, in_specs=in_specs, out_specs=out_specs, out_shape=out_shape,
        scratch_shapes=scratch, compiler_params=params)


def _rows(t, width):
    return pl.BlockSpec((t, width), lambda i: (i, 0))


def _whole(shape):
    nd = len(shape)
    return pl.BlockSpec(tuple(shape), lambda *_: (0,) * nd, pipeline_mode=pl.Buffered(1))


def _acc(shape):
    nd = len(shape)
    return pl.BlockSpec(tuple(shape), lambda *_: (0,) * nd)


def _sds(shape, dtype):
    return jax.ShapeDtypeStruct(tuple(shape), dtype)


def _rms(x, g):
    return x * lax.rsqrt(jnp.mean(x * x, axis=-1, keepdims=True) + EPS) * g


def _rms_bwd(x, g, dy):
    r = lax.rsqrt(jnp.mean(x * x, axis=-1, keepdims=True) + EPS)
    xh = x * r
    dg = jnp.sum(dy * xh, axis=0, keepdims=True)
    dxh = dy * g
    dx = r * (dxh - xh * jnp.mean(dxh * xh, axis=-1, keepdims=True))
    return dx, dg


def _sigmoid(x):
    return 1.0 / (1.0 + jnp.exp(-x))


def _gelu(x):
    return jax.nn.gelu(x, approximate=True)


def _accumulate(ref, val, first):
    @pl.when(first)
    def _():
        ref[...] = val

    @pl.when(jnp.logical_not(first))
    def _():
        ref[...] += val


def _accumulate_out(acc_ref, out_ref, val, first, last):
    _accumulate(acc_ref, val, first)

    @pl.when(last)
    def _():
        out_ref[...] = acc_ref[...].astype(out_ref.dtype)


def _inproj_fwd(x, g1, w_in, t):
    s = x.shape[0]

    def body(x_ref, g_ref, w_ref, hb_ref, pa_ref, pm_ref):
        hb = _rms(x_ref[...], g_ref[...]).astype(BF16)
        hb_ref[...] = hb
        pa_ref[...] = _dot(hb, w_ref[:PA_W, :], NT)
        pm_ref[...] = _dot(hb, w_ref[PA_W:, :], NT)

    return _pcall(
        body, name="inproj_fwd", grid=(s // t,),
        in_specs=[_rows(t, D_MODEL), _whole((1, D_MODEL)), _whole(w_in.shape)],
        out_specs=[_rows(t, D_MODEL), _rows(t, PA_W), _rows(t, PM_W)],
        out_shape=[_sds((s, D_MODEL), BF16), _sds((s, PA_W), F32), _sds((s, PM_W), F32)],
        vmem=VMEM_LIMIT)(x, g1, w_in)


def _attn_prep_fwd(pa, cosf, sins, qn, kvn, wq, wqs, wk, wv, t):
    s = pa.shape[0]

    def body(pa_ref, cos_ref, sin_ref, qn_ref, kvn_ref, wq_ref, wqs_ref, wk_ref, wv_ref, q_ref, k_ref, v_ref):
        cosv, sinv = cos_ref[...], sin_ref[...]
        cqn = _rms(pa_ref[:, 0:Q_RANK], qn_ref[...]).astype(BF16)
        ckvn = _rms(pa_ref[:, Q_RANK:Q_RANK + KV_RANK], kvn_ref[...]).astype(BF16)
        k_rope = pa_ref[:, 384:512] * cosv + pa_ref[:, 512:640] * sinv
        q_all, qs_all = _dot(cqn, wq_ref[...], NT), _dot(cqn, wqs_ref[...], NT)
        k_all = _dot(ckvn, wk_ref[...], NT)
        for h in range(N_HEADS):
            lanes = slice(h * HEAD_PAD, (h + 1) * HEAD_PAD)
            q_ref[h] = (q_all[:, lanes] * cosv + qs_all[:, lanes] * sinv).astype(BF16)
            k_ref[h] = (k_all[:, lanes] + k_rope).astype(BF16)
        v_ref[...] = _dot(ckvn, wv_ref[...], NT).astype(BF16)

    head_blk = pl.BlockSpec((N_HEADS, t, HEAD_PAD), lambda i: (0, i, 0))
    return _pcall(
        body, name="attn_prep_fwd", grid=(s // t,),
        in_specs=[_rows(t, PA_W), _rows(t, LANES), _rows(t, LANES), _whole(qn.shape), _whole(kvn.shape),
                  _whole(wq.shape), _whole(wqs.shape), _whole(wk.shape), _whole(wv.shape)],
        out_specs=[head_blk, head_blk, _rows(t, BR_W)],
        out_shape=[_sds((N_HEADS, s, HEAD_PAD), BF16), _sds((N_HEADS, s, HEAD_PAD), BF16), _sds((s, BR_W), BF16)],
        vmem=VMEM_LIMIT)(pa, cosf, sins, qn, kvn, wq, wqs, wk, wv)


def _causal_scores(q, k, masked):
    sc = _dot(q, k, NT) * SM_SCALE
    if masked:
        row = lax.broadcasted_iota(jnp.int32, sc.shape, 0)
        col = lax.broadcasted_iota(jnp.int32, sc.shape, 1)
        sc = jnp.where(col <= row, sc, -jnp.inf)
    return sc


def _diagonal_parts(t):
    half = t // 2
    if half % LANES:
        return [(True, slice(0, t), slice(0, t))]
    lower, upper = slice(0, half), slice(half, t)
    return [(True, lower, lower), (False, upper, lower), (True, upper, upper)]


def _causal_steps(n, key_major):
    if key_major:
        pairs = [(qi, kj) for kj in range(n) for qi in range(kj, n)]
    else:
        pairs = [(qi, kj) for qi in range(n) for kj in range(qi + 1)]
    return (jnp.asarray([p[0] for p in pairs], jnp.int32), jnp.asarray([p[1] for p in pairs], jnp.int32))


def _flash_fwd(q, k, v, t, riding):
    s = v.shape[0]
    n = s // t
    qi_tab, kj_tab = _causal_steps(n, key_major=False)
    n_steps = int(qi_tab.shape[0])
    whole = slice(0, t)

    def body(qi_ref, kj_ref, q_ref, k_ref, v_ref, x_ref, o_ref, lse_ref, land_ref, m_scr, l_scr, acc_scr,
             send_sems, recv_sems, local_sem):
        step_id = pl.program_id(0)
        qi, kj = qi_ref[step_id], kj_ref[step_id]
        at_end = _ride_along(riding, step_id, n_steps, (x_ref, land_ref, send_sems, recv_sems, local_sem))

        @pl.when(kj == 0)
        def _():
            m_scr[...] = jnp.full(m_scr.shape, -jnp.inf, F32)
            l_scr[...] = jnp.zeros(l_scr.shape, F32)
            acc_scr[...] = jnp.zeros(acc_scr.shape, F32)

        def step(masked, qr, kr):
            for h in range(N_HEADS):
                sc = _causal_scores(q_ref[h, qr, :], k_ref[h, kr, :], masked)
                m_prev = m_scr[h, qr, :]
                m_next = jnp.maximum(m_prev, jnp.max(sc, axis=1, keepdims=True))
                alpha = jnp.exp(m_prev - m_next)
                p = jnp.exp(sc - jnp.tile(m_next, (1, sc.shape[1] // LANES)))
                l_scr[h, qr, :] = alpha * l_scr[h, qr, :] + jnp.sum(p, axis=1, keepdims=True)
                m_scr[h, qr, :] = m_next
                pair = slice((h // 2) * LANES, (h // 2 + 1) * LANES)
                acc_scr[h, qr, :] = acc_scr[h, qr, :] * alpha + _dot(p.astype(BF16), v_ref[kr, pair])

        @pl.when(kj < qi)
        def _():
            step(False, whole, whole)

        @pl.when(kj == qi)
        def _():
            for masked, qr, kr in _diagonal_parts(t):
                step(masked, qr, kr)
            lane = lax.broadcasted_iota(jnp.int32, (t, LANES), 1)
            for pr in range(N_HEADS // 2):
                o0 = acc_scr[2 * pr] / l_scr[2 * pr]
                o1 = acc_scr[2 * pr + 1] / l_scr[2 * pr + 1]
                o_ref[:, pr * LANES:(pr + 1) * LANES] = jnp.where(lane < D_V, o0, o1).astype(BF16)
            for h in range(N_HEADS):
                lse_ref[h] = m_scr[h] + jnp.log(l_scr[h])

        at_end()

    q_blk = pl.BlockSpec((N_HEADS, t, HEAD_PAD), lambda i, qi, kj: (0, qi[i], 0))
    k_blk = pl.BlockSpec((N_HEADS, t, HEAD_PAD), lambda i, qi, kj: (0, kj[i], 0))
    v_blk = pl.BlockSpec((t, BR_W), lambda i, qi, kj: (kj[i], 0))
    return _pcall(
        body, name="flash_fwd_gathering", grid=(n_steps,), prefetch=2,
        in_specs=[q_blk, k_blk, v_blk, HBM_REF],
        out_specs=[pl.BlockSpec((t, BR_W), lambda i, qi, kj: (qi[i], 0)), q_blk, HBM_REF],
        out_shape=[_sds((s, BR_W), BF16), _sds((N_HEADS, s, LANES), F32), riding[2]],
        scratch=[pltpu.VMEM((N_HEADS, t, LANES), F32)] * 3 + list(EXCHANGE_SEMS),
        vmem=VMEM_LIMIT)(qi_tab, kj_tab, q, k, v, riding[1])


def _shift_down(ext, k, t):
    return pltpu.roll(ext, k, 0)[HALO:HALO + t]


def _shift_up(ext, k, t):
    return pltpu.roll(ext, t + HALO - k, 0)[0:t]


def _lane_group(shape):
    return lax.shift_right_logical(lax.broadcasted_iota(jnp.int32, shape, 1), 6)


def _group_select(vals):
    grp = _lane_group(vals[0].shape)
    out = vals[0]
    for g in range(1, len(vals)):
        out = jnp.where(grp == g, vals[g], out)
    return out


def _layernorm(x, g, b):
    mu = jnp.mean(x, axis=-1, keepdims=True)
    xc = x - mu
    return xc * lax.rsqrt(jnp.mean(xc * xc, axis=-1, keepdims=True) + EPS) * g + b


def _sg_mix(wm_ref, vnb, bias):
    return _group_select([_dot(wm_ref[g], vnb) for g in range(SG_GROUPS)]) + bias


def _pool_windows(ext):
    s2 = ext + pltpu.roll(ext, 1, 0)
    s4 = s2 + pltpu.roll(s2, 2, 0)
    s8 = s4 + pltpu.roll(s4, 4, 0)
    s16 = s8 + pltpu.roll(s8, 8, 0)
    return [s2, s4, s8, s16]


def _pool_counts(tok):
    return [jnp.minimum(tok + 1, w).astype(F32) for w in POOL_WINDOWS]


def _halo_before(t):
    return pl.BlockSpec((HALO, PM_W), lambda i: (jnp.maximum(i * (t // HALO) - 1, 0), 0))


def _mixers_fwd(pm, ln_g, ln_b, wm, sgb, conv_w, wp, pscale, t):
    s = pm.shape[0]

    def body(pm_ref, halo_ref, lng_ref, lnb_ref, wm_ref, sgb_ref, cw_ref, wp_ref, ps_ref, bsg_ref, bcv_ref, bpl_ref):
        i = pl.program_id(0)
        halo = jnp.where(i > 0, halo_ref[...], 0.0)
        u = _gelu(pm_ref[:, 0:256])
        vnb = _layernorm(_gelu(pm_ref[:, 256:512]), lng_ref[...], lnb_ref[...]).astype(BF16)
        for c in range(t // SG_CHUNK):
            rows = slice(c * SG_CHUNK, (c + 1) * SG_CHUNK)
            bsg_ref[rows, :] = (u[rows] * _sg_mix(wm_ref, vnb[rows], sgb_ref[...])).astype(BF16)
        z = pm_ref[:, 1024:1280] * pm_ref[:, 512:768]
        zext = jnp.concatenate([halo[:, 1024:1280] * halo[:, 512:768], z], axis=0)
        y = cw_ref[0:1, :] * _shift_down(zext, 2, t) + cw_ref[1:2, :] * _shift_down(zext, 1, t) + cw_ref[2:3, :] * z
        bcv_ref[...] = (pm_ref[:, 768:1024] * y).astype(BF16)
        p = pm_ref[:, 1280:1536]
        sums = _pool_windows(jnp.concatenate([halo[:, 1280:1536], p], axis=0))
        tok = i * t + lax.broadcasted_iota(jnp.int32, (t, 1), 0)
        pooled = _group_select([sw[HALO:HALO + t] / cnt - p for sw, cnt in zip(sums, _pool_counts(tok))])
        bpl_ref[...] = (_dot(pooled.astype(BF16), wp_ref[...]) * ps_ref[...]).astype(BF16)

    return _pcall(
        body, name="mixers_fwd", grid=(s // t,),
        in_specs=[_rows(t, PM_W), _halo_before(t), _whole(ln_g.shape), _whole(ln_b.shape), _whole(wm.shape),
                  _whole(sgb.shape), _whole(conv_w.shape), _whole(wp.shape), _whole(pscale.shape)],
        out_specs=[_rows(t, BR_W)] * 3,
        out_shape=[_sds((s, BR_W), BF16)] * 3,
        vmem=VMEM_LIMIT)(pm, pm, ln_g, ln_b, wm, sgb, conv_w, wp, pscale)


def _merge_fwd(x, hb, branches, wg, bg, wbr, wout, g2, t):
    s = x.shape[0]

    def body(x_ref, hb_ref, b0_ref, b1_ref, b2_ref, b3_ref, wg_ref, bg_ref, wbr_ref, wout_ref, g2_ref,
             xmid_ref, mrg_ref, t_ref):
        hb = hb_ref[...]
        merged = jnp.zeros((t, D_MODEL), F32)
        for k, b_ref in enumerate((b0_ref, b1_ref, b2_ref, b3_ref)):
            cols = slice(k * D_MODEL, (k + 1) * D_MODEL)
            gate = _sigmoid(_dot(hb, wg_ref[cols, :], NT) + bg_ref[:, cols])
            merged = merged + gate * _dot(b_ref[...], wbr_ref[k], NT)
        mb = merged.astype(BF16)
        mrg_ref[...] = mb
        tt = _dot(mb, wout_ref[...])
        t_ref[...] = tt
        xmid_ref[...] = x_ref[...] + _rms(tt, g2_ref[...])

    return _pcall(
        body, name="merge_fwd", grid=(s // t,),
        in_specs=[_rows(t, D_MODEL), _rows(t, D_MODEL)] + [_rows(t, BR_W)] * 4 +
                 [_whole(wg.shape), _whole(bg.shape), _whole(wbr.shape), _whole(wout.shape), _whole(g2.shape)],
        out_specs=[_rows(t, D_MODEL)] * 3,
        out_shape=[_sds((s, D_MODEL), F32), _sds((s, D_MODEL), BF16), _sds((s, D_MODEL), F32)],
        vmem=VMEM_LIMIT)(x, hb, *branches, wg, bg, wbr, wout, g2)


def _ffn_fwd(x, g3, w1, w2, g4, t):
    s = x.shape[0]

    def body(x_ref, g3_ref, w1_ref, w2_ref, g4_ref, xo_ref, f_ref):
        h = _rms(x_ref[...], g3_ref[...]).astype(BF16)
        f = jnp.zeros((t, D_MODEL), F32)
        for j in range(D_FF // D_MODEL):
            cols = slice(j * D_MODEL, (j + 1) * D_MODEL)
            r = jnp.square(jnp.maximum(_dot(h, w1_ref[cols, :], NT), 0.0)).astype(BF16)
            f = f + _dot(r, w2_ref[cols, :])
        f_ref[...] = f
        xo_ref[...] = x_ref[...] + _rms(f, g4_ref[...])

    return _pcall(
        body, name="ffn_fwd", grid=(s // t,),
        in_specs=[_rows(t, D_MODEL), _whole(g3.shape), _whole(w1.shape), _whole(w2.shape), _whole(g4.shape)],
        out_specs=[_rows(t, D_MODEL)] * 2,
        out_shape=[_sds((s, D_MODEL), F32)] * 2,
        vmem=VMEM_LIMIT)(x, g3, w1, w2, g4)


def _loss_head(y, target, t):
    s = y.shape[0]
    n = s // t

    def body(y_ref, tg_ref, dy_ref, loss_ref, acc_scr):
        i = pl.program_id(0)
        d = y_ref[...] - tg_ref[...]
        dy_ref[...] = d * (1.0 / D_MODEL)
        _accumulate(acc_scr, jnp.sum(d * d, axis=0, keepdims=True), i == 0)

        @pl.when(i == n - 1)
        def _():
            loss_ref[...] = jnp.full(loss_ref.shape, 0.5 / D_MODEL, F32) * jnp.sum(acc_scr[...])

    return _pcall(
        body, name="loss_head", grid=(n,),
        in_specs=[_rows(t, D_MODEL)] * 2,
        out_specs=[_rows(t, D_MODEL), _acc((8, LANES))],
        out_shape=[_sds((s, D_MODEL), F32), _sds((8, LANES), F32)],
        scratch=[pltpu.VMEM((1, D_MODEL), F32)])(y, target)


def _ffn_bwd_norms(x_mid, f, dxo, g3, g4, t):
    s = x_mid.shape[0]

    def body(x_ref, f_ref, dxo_ref, g3_ref, g4_ref, h2_ref, df_ref, dg4_ref):
        h2_ref[...] = _rms(x_ref[...], g3_ref[...]).astype(BF16)
        df, dg4 = _rms_bwd(f_ref[...], g4_ref[...], dxo_ref[...])
        df_ref[...] = df.astype(BF16)
        _accumulate(dg4_ref, dg4, pl.program_id(0) == 0)

    return _pcall(
        body, name="ffn_bwd_norms", grid=(s // t,),
        in_specs=[_rows(t, D_MODEL)] * 3 + [_whole(g3.shape), _whole(g4.shape)],
        out_specs=[_rows(t, D_MODEL), _rows(t, D_MODEL), _acc((1, D_MODEL))],
        out_shape=[_sds((s, D_MODEL), BF16), _sds((s, D_MODEL), BF16), _sds((1, D_MODEL), F32)])(x_mid, f, dxo, g3, g4)


def _ffn_bwd_weights(h2, df, w1, w2, t):
    s = h2.shape[0]
    blk = D_MODEL
    n = s // t

    def body(h2_ref, df_ref, w1_ref, w2_ref, da_ref, dw1_ref, dw2_ref, acc1, acc2):
        first, last = pl.program_id(1) == 0, pl.program_id(1) == n - 1
        h2v, dfv = h2_ref[...], df_ref[...]
        rl = jnp.maximum(_dot(h2v, w1_ref[...], NT), 0.0)
        _accumulate_out(acc2, dw2_ref, _dot((rl * rl).astype(BF16), dfv, TN), first, last)
        da = (_dot(dfv, w2_ref[...], NT) * (2.0 * rl)).astype(BF16)
        da_ref[...] = da
        _accumulate_out(acc1, dw1_ref, _dot(da, h2v, TN), first, last)

    tok = pl.BlockSpec((t, D_MODEL), lambda j, i: (i, 0))
    hid = pl.BlockSpec((blk, D_MODEL), lambda j, i: (j, 0))
    return _pcall(
        body, name="ffn_bwd_weights", grid=(D_FF // blk, n),
        in_specs=[tok, tok, hid, hid],
        out_specs=[pl.BlockSpec((t, blk), lambda j, i: (i, j)), hid, hid],
        out_shape=[_sds((s, D_FF), BF16), _sds((D_FF, D_MODEL), BF16), _sds((D_FF, D_MODEL), BF16)],
        scratch=[pltpu.VMEM((blk, D_MODEL), F32)] * 2,
        vmem=VMEM_LIMIT)(h2, df, w1, w2)


def _ffn_bwd_input(da, w1, x_mid, dxo, g3, t):
    s = x_mid.shape[0]

    def body(da_ref, w1_ref, x_ref, dxo_ref, g3_ref, dx_ref, dg3_ref):
        dx, dg3 = _rms_bwd(x_ref[...], g3_ref[...], _dot(da_ref[...], w1_ref[...]))
        dx_ref[...] = dxo_ref[...] + dx
        _accumulate(dg3_ref, dg3, pl.program_id(0) == 0)

    return _pcall(
        body, name="ffn_bwd_input", grid=(s // t,),
        in_specs=[_rows(t, D_FF), _whole(w1.shape), _rows(t, D_MODEL), _rows(t, D_MODEL), _whole(g3.shape)],
        out_specs=[_rows(t, D_MODEL), _acc((1, D_MODEL))],
        out_shape=[_sds((s, D_MODEL), F32), _sds((1, D_MODEL), F32)],
        vmem=VMEM_LIMIT)(da, w1, x_mid, dxo, g3)


def _merge_bwd_out(tt, dxmid, g2, wout, merged, t):
    s = tt.shape[0]
    n = s // t

    def body(t_ref, dx_ref, g2_ref, wout_ref, mrg_ref, dm_ref, dwout_ref, dg2_ref, acc):
        first, last = pl.program_id(0) == 0, pl.program_id(0) == n - 1
        dt, dg2 = _rms_bwd(t_ref[...], g2_ref[...], dx_ref[...])
        dtb = dt.astype(BF16)
        dm_ref[...] = _dot(dtb, wout_ref[...], NT)
        _accumulate_out(acc, dwout_ref, _dot(mrg_ref[...], dtb, TN), first, last)
        _accumulate(dg2_ref, dg2, first)

    return _pcall(
        body, name="merge_bwd_out", grid=(n,),
        in_specs=[_rows(t, D_MODEL), _rows(t, D_MODEL), _whole(g2.shape), _whole(wout.shape), _rows(t, D_MODEL)],
        out_specs=[_rows(t, D_MODEL), _acc((D_MODEL, D_MODEL)), _acc((1, D_MODEL))],
        out_shape=[_sds((s, D_MODEL), F32), _sds((D_MODEL, D_MODEL), BF16), _sds((1, D_MODEL), F32)],
        scratch=[pltpu.VMEM((D_MODEL, D_MODEL), F32)],
        vmem=VMEM_LIMIT)(tt, dxmid, g2, wout, merged)


def _merge_bwd_branch(k, hb, br, dm, wg, bg, wbr, t):
    s = hb.shape[0]
    n = s // t

    def body(hb_ref, br_ref, dm_ref, wg_ref, bg_ref, wbr_ref, dbr_ref, dpre_ref, dwg_ref, dbg_ref, dwbr_ref,
             acc_g, acc_br):
        first, last = pl.program_id(0) == 0, pl.program_id(0) == n - 1
        hbv, brv, dmv, wbrv = hb_ref[...], br_ref[...], dm_ref[...], wbr_ref[0]
        gate = _sigmoid(_dot(hbv, wg_ref[...], NT) + bg_ref[...])
        dy = (dmv * gate).astype(BF16)
        dpre = dmv * _dot(brv, wbrv, NT) * gate * (1.0 - gate)
        dpb = dpre.astype(BF16)
        dpre_ref[...] = dpb
        dbr_ref[...] = _dot(dy, wbrv)
        _accumulate_out(acc_br, dwbr_ref, _dot(dy, brv, TN), first, last)
        _accumulate_out(acc_g, dwg_ref, _dot(dpb, hbv, TN), first, last)
        _accumulate(dbg_ref, jnp.sum(dpre, axis=0, keepdims=True), first)

    return _pcall(
        body, name=f"merge_bwd_branch{k}", grid=(n,),
        in_specs=[_rows(t, D_MODEL), _rows(t, BR_W), _rows(t, D_MODEL),
                  pl.BlockSpec((D_MODEL, D_MODEL), lambda i: (k, 0)), pl.BlockSpec((1, D_MODEL), lambda i: (0, k)),
                  pl.BlockSpec((1, D_MODEL, BR_W), lambda i: (k, 0, 0))],
        out_specs=[_rows(t, BR_W), _rows(t, D_MODEL), _acc((D_MODEL, D_MODEL)), _acc((1, D_MODEL)), _acc((D_MODEL, BR_W))],
        out_shape=[_sds((s, BR_W), F32), _sds((s, D_MODEL), BF16), _sds((D_MODEL, D_MODEL), BF16),
                   _sds((1, D_MODEL), F32), _sds((D_MODEL, BR_W), BF16)],
        scratch=[pltpu.VMEM((D_MODEL, D_MODEL), F32), pltpu.VMEM((D_MODEL, BR_W), F32)],
        vmem=VMEM_LIMIT)(hb, br, dm, wg, bg, wbr)


def _mixers_bwd(pm, dbsg, dbcv, dbpl, ln_g, ln_b, wm, wmt, sgb, conv_w, wp, pscale, t):
    s = pm.shape[0]
    n = s // t
    nb = t // HALO

    def body(pm_ref, before_ref, after_ref, dsg_ref, dcv_ref, dcva_ref, dpl_ref, dpla_ref, lng_ref, lnb_ref,
             wm_ref, wmt_ref, sgb_ref, cw_ref, wp_ref, ps_ref,
             dpm_ref, dlng_ref, dlnb_ref, dwm_ref, dsgb_ref, dcw_ref, dwp_ref, dps_ref):
        i = pl.program_id(0)
        first = i == 0
        before = jnp.where(i > 0, before_ref[...], 0.0)
        after = jnp.where(i < n - 1, after_ref[...], 0.0)

        u_raw, v_raw = pm_ref[:, 0:256], pm_ref[:, 256:512]
        lng, lnb = lng_ref[...], lnb_ref[...]
        u, gelu_u_vjp = jax.vjp(_gelu, u_raw)
        vn, norm_vjp = jax.vjp(lambda v_, g_, b_: _layernorm(_gelu(v_), g_, b_), v_raw, lng, lnb)
        vnb = vn.astype(BF16)
        dsg = dsg_ref[...]
        grp = _lane_group((SG_CHUNK, BR_W))
        tri = (lax.broadcasted_iota(jnp.int32, (SG_CHUNK, SG_CHUNK), 1)
               <= lax.broadcasted_iota(jnp.int32, (SG_CHUNK, SG_CHUNK), 0))
        du_parts, dvn_parts = [], []
        dsgb = jnp.zeros((SG_CHUNK, BR_W), F32)
        dwm = [jnp.zeros((SG_CHUNK, SG_CHUNK), F32) for _ in range(SG_GROUPS)]
        for c in range(t // SG_CHUNK):
            rows = slice(c * SG_CHUNK, (c + 1) * SG_CHUNK)
            mix = _sg_mix(wm_ref, vnb[rows], sgb_ref[...])
            du_parts.append(dsg[rows] * mix)
            ds = dsg[rows] * u[rows]
            dsgb = dsgb + ds
            dsb = [jnp.where(grp == g, ds, 0.0).astype(BF16) for g in range(SG_GROUPS)]
            for g in range(SG_GROUPS):
                dwm[g] = dwm[g] + _dot(dsb[g], vnb[rows], NT)
            dvn_parts.append(_group_select([_dot(wmt_ref[g], dsb[g]) for g in range(SG_GROUPS)]))
        (du_raw,) = gelu_u_vjp(jnp.concatenate(du_parts, axis=0))
        dv_raw, dlng, dlnb = norm_vjp(jnp.concatenate(dvn_parts, axis=0))
        dpm_ref[:, 0:256] = du_raw.astype(BF16)
        dpm_ref[:, 256:512] = dv_raw.astype(BF16)
        _accumulate(dlng_ref, dlng, first)
        _accumulate(dlnb_ref, dlnb, first)
        _accumulate(dsgb_ref, dsgb, first)
        for g in range(SG_GROUPS):
            _accumulate(dwm_ref.at[g], jnp.where(tri, dwm[g], 0.0), first)

        xin, bg, cg = pm_ref[:, 512:768], pm_ref[:, 768:1024], pm_ref[:, 1024:1280]
        z = cg * xin
        zext = jnp.concatenate([before[:, 1024:1280] * before[:, 512:768], z], axis=0)
        z1, z2 = _shift_down(zext, 1, t), _shift_down(zext, 2, t)
        w0, w1, w2 = cw_ref[0:1, :], cw_ref[1:2, :], cw_ref[2:3, :]
        dcv = dcv_ref[...]
        y = w0 * z2 + w1 * z1 + w2 * z
        dy = dcv * bg
        dyext = jnp.concatenate([dy, jnp.where(i < n - 1, dcva_ref[...], 0.0) * after[:, 768:1024]], axis=0)
        dz = w2 * dy + w1 * _shift_up(dyext, 1, t) + w0 * _shift_up(dyext, 2, t)
        dpm_ref[:, 512:768] = (dz * cg).astype(BF16)
        dpm_ref[:, 768:1024] = (dcv * y).astype(BF16)
        dpm_ref[:, 1024:1280] = (dz * xin).astype(BF16)
        dcw = jnp.concatenate([jnp.sum(dy * z2, axis=0, keepdims=True), jnp.sum(dy * z1, axis=0, keepdims=True),
                               jnp.sum(dy * z, axis=0, keepdims=True)], axis=0)
        _accumulate(dcw_ref, jnp.concatenate([dcw, jnp.zeros((8 - CONV_K, BR_W), F32)], axis=0), first)

        p = pm_ref[:, 1280:1536]
        tok = i * t + lax.broadcasted_iota(jnp.int32, (t, 1), 0)
        sums = _pool_windows(jnp.concatenate([before[:, 1280:1536], p], axis=0))
        pooled = _group_select([sw[HALO:HALO + t] / cnt - p for sw, cnt in zip(sums, _pool_counts(tok))]).astype(BF16)
        mixed = _dot(pooled, wp_ref[...])
        dpl = dpl_ref[...]
        ps = ps_ref[...]
        dmix = (dpl * ps).astype(BF16)
        _accumulate(dps_ref, jnp.sum(dpl * mixed, axis=0, keepdims=True), first)
        _accumulate(dwp_ref, _dot(pooled, dmix, TN), first)
        dmix_after = (jnp.where(i < n - 1, dpla_ref[...], 0.0) * ps).astype(BF16)
        dpo = _dot(dmix, wp_ref[...], NT)
        dpo_ext = jnp.concatenate([dpo, _dot(dmix_after, wp_ref[...], NT)], axis=0)
        tok_ext = i * t + lax.broadcasted_iota(jnp.int32, (t + HALO, 1), 0)
        dp_groups = []
        for g, (win, cnt) in enumerate(zip(POOL_WINDOWS, _pool_counts(tok_ext))):
            e = dpo_ext / cnt
            acc = e
            span = 1
            while span < win:
                acc = acc + pltpu.roll(acc, t + HALO - span, 0)
                span *= 2
            dp_groups.append(acc[0:t] - dpo)
        dpm_ref[:, 1280:1536] = _group_select(dp_groups).astype(BF16)

    row_blk = lambda w: pl.BlockSpec((t, w), lambda i: (i, 0))
    after_blk = lambda w: pl.BlockSpec((HALO, w), lambda i: (jnp.minimum((i + 1) * nb, n * nb - 1), 0))
    return _pcall(
        body, name="mixers_bwd", grid=(n,),
        in_specs=[row_blk(PM_W), _halo_before(t), after_blk(PM_W), row_blk(BR_W), row_blk(BR_W), after_blk(BR_W),
                  row_blk(BR_W), after_blk(BR_W), _whole(ln_g.shape), _whole(ln_b.shape), _whole(wm.shape),
                  _whole(wmt.shape), _whole(sgb.shape), _whole(conv_w.shape), _whole(wp.shape), _whole(pscale.shape)],
        out_specs=[row_blk(PM_W), _acc((1, BR_W)), _acc((1, BR_W)), _acc((SG_GROUPS, SG_CHUNK, SG_CHUNK)),
                   _acc((SG_CHUNK, BR_W)), _acc((8, BR_W)), _acc((BR_W, BR_W)), _acc((1, BR_W))],
        out_shape=[_sds((s, PM_W), BF16), _sds((1, BR_W), F32), _sds((1, BR_W), F32),
                   _sds((SG_GROUPS, SG_CHUNK, SG_CHUNK), F32), _sds((SG_CHUNK, BR_W), F32), _sds((8, BR_W), F32),
                   _sds((BR_W, BR_W), F32), _sds((1, BR_W), F32)],
        vmem=VMEM_LIMIT)(pm, pm, pm, dbsg, dbcv, dbcv, dbpl, dbpl, ln_g, ln_b, wm, wmt, sgb, conv_w, wp, pscale)


def _head_delta(o, do):
    rows = o.shape[0]
    prod = o.astype(F32) * do
    lane = lax.broadcasted_iota(jnp.int32, (rows, LANES), 1)
    out = []
    for h in range(N_HEADS):
        pair = prod[:, (h // 2) * LANES:(h // 2 + 1) * LANES]
        mine = (lane < D_V) if h % 2 == 0 else (lane >= D_V)
        out.append(jnp.broadcast_to(jnp.sum(jnp.where(mine, pair, 0.0), axis=1, keepdims=True), (rows, LANES)))
    return out


def _head_do(do, h):
    lane = lax.broadcasted_iota(jnp.int32, (do.shape[0], LANES), 1)
    mine = (lane < D_V) if h % 2 == 0 else (lane >= D_V)
    pair = do[:, (h // 2) * LANES:(h // 2 + 1) * LANES]
    return jnp.where(mine, pair, 0.0).astype(BF16)


def _flash_bwd(q, k, v, o, do, lse, t, riding):
    s = v.shape[0]
    n = s // t
    qi_tab, kj_tab = _causal_steps(n, key_major=True)
    n_steps = int(qi_tab.shape[0])
    whole = slice(0, t)

    def body(qi_ref, kj_ref, q_ref, k_ref, v_ref, o_ref, do_ref, lse_ref, x_ref, dq_ref, dk_ref, dv_ref, land_ref,
             dv_scr, send_sems, recv_sems, local_sem):
        step_id = pl.program_id(0)
        qi, kj = qi_ref[step_id], kj_ref[step_id]
        at_end = _ride_along(riding, step_id, n_steps, (x_ref, land_ref, send_sems, recv_sems, local_sem))

        @pl.when(step_id == 0)
        def _():
            dq_ref[...] = jnp.zeros(dq_ref.shape, F32)

        @pl.when(qi == kj)
        def _():
            dk_ref[...] = jnp.zeros(dk_ref.shape, F32)
            dv_scr[...] = jnp.zeros(dv_scr.shape, F32)

        def step(masked, qr, kr):
            dov = do_ref[qr, :]
            delta = _head_delta(o_ref[qr, :], dov)
            reps = (kr.stop - kr.start) // LANES
            q_rows = pl.ds(pl.multiple_of(qi * t + qr.start, t // 2), qr.stop - qr.start)
            for h in range(N_HEADS):
                qh, kh = q_ref[h, qr, :], k_ref[h, kr, :]
                p = jnp.exp(_causal_scores(qh, kh, masked) - jnp.tile(lse_ref[h, qr, :], (1, reps)))
                pair = slice((h // 2) * LANES, (h // 2 + 1) * LANES)
                dv_scr[h, kr, :] += _dot(p.astype(BF16), dov[:, pair].astype(BF16), TN)
                dp = _dot(_head_do(dov, h), v_ref[kr, pair], NT)
                ds = (p * (dp - jnp.tile(delta[h], (1, reps))) * SM_SCALE).astype(BF16)
                dk_ref[h, kr, :] += _dot(ds, qh, TN)
                dq_ref[h, q_rows, :] += _dot(ds, kh)

        @pl.when(qi > kj)
        def _():
            step(False, whole, whole)

        @pl.when(qi == kj)
        def _():
            for masked, qr, kr in _diagonal_parts(t):
                step(masked, qr, kr)

        @pl.when(qi == n - 1)
        def _():
            lane = lax.broadcasted_iota(jnp.int32, (t, LANES), 1)
            for pr in range(N_HEADS // 2):
                dv_ref[:, pr * LANES:(pr + 1) * LANES] = jnp.where(lane < D_V, dv_scr[2 * pr], dv_scr[2 * pr + 1]).astype(BF16)

        at_end()

    q_blk = pl.BlockSpec((N_HEADS, t, HEAD_PAD), lambda i, qi, kj: (0, qi[i], 0))
    k_blk = pl.BlockSpec((N_HEADS, t, HEAD_PAD), lambda i, qi, kj: (0, kj[i], 0))
    v_blk = pl.BlockSpec((t, BR_W), lambda i, qi, kj: (kj[i], 0))
    o_blk = pl.BlockSpec((t, BR_W), lambda i, qi, kj: (qi[i], 0))
    dq_sds = _sds((N_HEADS, s, HEAD_PAD), F32)
    return _pcall(
        body, name="flash_bwd_exchanging", grid=(n_steps,), prefetch=2,
        in_specs=[q_blk, k_blk, v_blk, o_blk, o_blk, q_blk, HBM_REF],
        out_specs=[_whole(dq_sds.shape), k_blk, v_blk, HBM_REF],
        out_shape=[dq_sds, dq_sds, _sds((s, BR_W), BF16), riding[2]],
        scratch=[pltpu.VMEM((N_HEADS, t, LANES), F32)] + list(EXCHANGE_SEMS),
        vmem=VMEM_LIMIT)(qi_tab, kj_tab, q, k, v, o, do, lse, riding[1])


def _attn_prep_bwd(pa, dq, dk, dv, cosf, sins, qn, kvn, wq, wqs, wk, wv, t):
    s = pa.shape[0]

    def body(pa_ref, dq_ref, dk_ref, dv_ref, cos_ref, sin_ref, qn_ref, kvn_ref, wq_ref, wqs_ref, wk_ref, wv_ref,
             dpa_ref, dwq_ref, dwqs_ref, dwk_ref, dwv_ref, dqn_ref, dkvn_ref):
        first = pl.program_id(0) == 0
        cosv, sinv = cos_ref[...], sin_ref[...]
        cq, ckv = pa_ref[:, 0:Q_RANK], pa_ref[:, Q_RANK:Q_RANK + KV_RANK]
        cqn = _rms(cq, qn_ref[...]).astype(BF16)
        ckvn = _rms(ckv, kvn_ref[...]).astype(BF16)
        dvv = dv_ref[...]
        _accumulate(dwv_ref, _dot(dvv, ckvn, TN), first)
        dk_rope = dk_ref[0]
        for h in range(1, N_HEADS):
            dk_rope = dk_rope + dk_ref[h]
        dqa = jnp.concatenate([(dq_ref[h] * cosv).astype(BF16) for h in range(N_HEADS)], axis=1)
        dqs = jnp.concatenate([(dq_ref[h] * sinv).astype(BF16) for h in range(N_HEADS)], axis=1)
        dkb = jnp.concatenate([dk_ref[h].astype(BF16) for h in range(N_HEADS)], axis=1)
        _accumulate(dwq_ref, _dot(dqa, cqn, TN), first)
        _accumulate(dwqs_ref, _dot(dqs, cqn, TN), first)
        _accumulate(dwk_ref, _dot(dkb, ckvn, TN), first)
        dcqn = _dot(dqa, wq_ref[...]) + _dot(dqs, wqs_ref[...])
        dckvn = _dot(dvv, wv_ref[...]) + _dot(dkb, wk_ref[...])
        dcq, dqn = _rms_bwd(cq, qn_ref[...], dcqn)
        dckv, dkvn = _rms_bwd(ckv, kvn_ref[...], dckvn)
        _accumulate(dqn_ref, dqn, first)
        _accumulate(dkvn_ref, dkvn, first)
        dpa_ref[:, 0:Q_RANK] = dcq.astype(BF16)
        dpa_ref[:, Q_RANK:Q_RANK + KV_RANK] = dckv.astype(BF16)
        dpa_ref[:, 384:512] = (dk_rope * cosv).astype(BF16)
        dpa_ref[:, 512:640] = (dk_rope * sinv).astype(BF16)

    head_blk = pl.BlockSpec((N_HEADS, t, HEAD_PAD), lambda i: (0, i, 0))
    return _pcall(
        body, name="attn_prep_bwd", grid=(s // t,),
        in_specs=[_rows(t, PA_W), head_blk, head_blk, _rows(t, BR_W), _rows(t, LANES), _rows(t, LANES),
                  _whole(qn.shape), _whole(kvn.shape), _whole(wq.shape), _whole(wqs.shape), _whole(wk.shape), _whole(wv.shape)],
        out_specs=[_rows(t, PA_W), _acc(wq.shape), _acc(wqs.shape), _acc(wk.shape), _acc(wv.shape),
                   _acc(qn.shape), _acc(kvn.shape)],
        out_shape=[_sds((s, PA_W), BF16), _sds(wq.shape, F32), _sds(wqs.shape, F32), _sds(wk.shape, F32),
                   _sds(wv.shape, F32), _sds(qn.shape, F32), _sds(kvn.shape, F32)],
        vmem=VMEM_LIMIT)(pa, dq, dk, dv, cosf, sins, qn, kvn, wq, wqs, wk, wv)


def _inproj_bwd(x, g1, dxmid, dpa, dpm, dpres, w_in, wg, t):
    s = x.shape[0]
    n = s // t

    def body(x_ref, g1_ref, dxm_ref, dpa_ref, dpm_ref, d0_ref, d1_ref, d2_ref, d3_ref, w_ref, wg_ref,
             dx_ref, dg1_ref, dwa_ref, dwm_ref, acc_a, acc_m):
        first, last = pl.program_id(0) == 0, pl.program_id(0) == n - 1
        xv, g1v = x_ref[...], g1_ref[...]
        hb = _rms(xv, g1v).astype(BF16)
        dpav, dpmv = dpa_ref[...], dpm_ref[...]
        dh = _dot(dpav, w_ref[:PA_W, :]) + _dot(dpmv, w_ref[PA_W:, :])
        for k, d_ref in enumerate((d0_ref, d1_ref, d2_ref, d3_ref)):
            dh = dh + _dot(d_ref[...], wg_ref[k * D_MODEL:(k + 1) * D_MODEL, :])
        dx, dg1 = _rms_bwd(xv, g1v, dh)
        dx_ref[...] = dxm_ref[...] + dx
        _accumulate(dg1_ref, dg1, first)
        _accumulate_out(acc_a, dwa_ref, _dot(dpav, hb, TN), first, last)
        _accumulate_out(acc_m, dwm_ref, _dot(dpmv, hb, TN), first, last)

    return _pcall(
        body, name="inproj_bwd", grid=(n,),
        in_specs=[_rows(t, D_MODEL), _whole(g1.shape), _rows(t, D_MODEL), _rows(t, PA_W), _rows(t, PM_W)] +
                 [_rows(t, D_MODEL)] * 4 + [_whole(w_in.shape), _whole(wg.shape)],
        out_specs=[_rows(t, D_MODEL), _acc((1, D_MODEL)), _acc((PA_W, D_MODEL)), _acc((PM_W, D_MODEL))],
        out_shape=[_sds((s, D_MODEL), F32), _sds((1, D_MODEL), F32), _sds((PA_W, D_MODEL), BF16), _sds((PM_W, D_MODEL), BF16)],
        scratch=[pltpu.VMEM((PA_W, D_MODEL), F32), pltpu.VMEM((PM_W, D_MODEL), F32)],
        vmem=VMEM_LIMIT)(x, g1, dxmid, dpa, dpm, *dpres, w_in, wg)


def _my_place():
    return lax.axis_index("x"), lax.axis_index("y"), lax.axis_index("c")


def _flip(place, k):
    x, y, c = place
    return (1 - x if k & 4 else x, 1 - y if k & 2 else y, 1 - c if k & 1 else c)


def _rank(place):
    x, y, c = place
    return 4 * x + 2 * y + c


EXCHANGE_SEMS = (pltpu.SemaphoreType.DMA((7,)), pltpu.SemaphoreType.DMA((7,)), pltpu.SemaphoreType.DMA)
HBM_REF = pl.BlockSpec(memory_space=pl.ANY)


def _gather_plan(x_ref, out_ref, send_sems, recv_sems, local_sem):
    me = _my_place()
    sibling = _flip(me, 1)
    chips = (4, 2, 6)

    def copy(k, src_place, to, src=None):
        slot = out_ref.at[_rank(src_place)]
        return pltpu.make_async_remote_copy(
            src_ref=slot if src is None else src, dst_ref=slot, send_sem=send_sems.at[k], recv_sem=recv_sems.at[k],
            device_id=to, device_id_type=MESH_ID)

    mine = pltpu.make_async_copy(x_ref, out_ref.at[_rank(me)], local_sem)
    first = [copy(0, me, sibling, src=x_ref)] + [copy(1 + j, me, _flip(me, kc), src=x_ref) for j, kc in enumerate(chips)]
    passed = [copy(4 + j, _flip(me, kc), sibling) for j, kc in enumerate(chips)]

    def start():
        mine.start()
        for cp in first:
            cp.start()

    def forward():
        for j, kc in enumerate(chips):
            copy(1 + j, _flip(me, kc), me).wait_recv()
            passed[j].start()

    def finish():
        copy(0, sibling, me).wait_recv()
        for j, kc in enumerate(chips):
            copy(4 + j, _flip(sibling, kc), me).wait_recv()
        for cp in first + passed:
            cp.wait_send()
        mine.wait()

    return start, forward, finish


def _exchange_plan(x_ref, out_ref, send_sems, recv_sems, local_sem):
    me = _my_place()
    my_rank = _rank(me)
    mine = pltpu.make_async_copy(x_ref.at[my_rank], out_ref.at[my_rank], local_sem)
    sends = [pltpu.make_async_remote_copy(
        src_ref=x_ref.at[_rank(_flip(me, k))], dst_ref=out_ref.at[my_rank], send_sem=send_sems.at[k - 1],
        recv_sem=recv_sems.at[k - 1], device_id=_flip(me, k), device_id_type=MESH_ID) for k in range(1, N_DEV)]

    def start():
        mine.start()
        for cp in sends:
            cp.start()

    def forward():
        pass

    def finish():
        for k in range(1, N_DEV):
            slot = out_ref.at[_rank(_flip(me, k))]
            pltpu.make_async_remote_copy(
                src_ref=slot, dst_ref=slot, send_sem=send_sems.at[k - 1], recv_sem=recv_sems.at[k - 1],
                device_id=_flip(me, k), device_id_type=MESH_ID).wait_recv()
        for cp in sends:
            cp.wait_send()
        mine.wait()

    return start, forward, finish


def _exchange_alone(plan, src, out_sds, name):
    def body(x_ref, out_ref, send_sems, recv_sems, local_sem):
        start, forward, finish = plan(x_ref, out_ref, send_sems, recv_sems, local_sem)
        start()
        forward()
        finish()

    return pl.pallas_call(body, name=name, out_shape=out_sds, in_specs=[HBM_REF], out_specs=HBM_REF,
                          scratch_shapes=list(EXCHANGE_SEMS))(src)


def _all_gather(shard, name):
    return _exchange_alone(_gather_plan, shard, _sds((N_DEV,) + shard.shape, shard.dtype), name)


def _all_to_all(blocks, name):
    return _exchange_alone(_exchange_plan, blocks, _sds(blocks.shape, blocks.dtype), name)


def _riding(plan, src, out_sds):
    return (plan, src, out_sds)


def _ride_along(riding, step_id, n_steps, refs):
    start, forward, finish = riding[0](*refs)

    @pl.when(step_id == 0)
    def _():
        start()

    @pl.when(step_id == (3 * n_steps) // 4)
    def _():
        forward()

    def at_end():
        @pl.when(step_id == n_steps - 1)
        def _():
            finish()

    return at_end


def _sum_slots(parts, name):
    _, r, c = parts.shape
    t = math.gcd(r, ROW_PAD)

    def body(p_ref, o_ref):
        acc = p_ref[0].astype(F32)
        for d in range(1, N_DEV):
            acc = acc + p_ref[d].astype(F32)
        o_ref[...] = acc

    return _pcall(
        body, name=name, grid=(r // t,),
        in_specs=[pl.BlockSpec((N_DEV, t, c), lambda i: (0, i, 0))], out_specs=_rows(t, c),
        out_shape=_sds((r, c), F32))(parts)


def _adamw(w, g, m, v, name):
    depth, r, c = w.shape
    t = r if r <= ROW_PAD else math.gcd(r, ROW_PAD)
    blk = pl.BlockSpec((1, t, c), lambda l, i: (l, i, 0))

    def body(w_ref, g_ref, m_ref, v_ref, d_ref, nm_ref, nv_ref):
        gv = g_ref[...]
        nm = ADAM_B1 * m_ref[...] + (1.0 - ADAM_B1) * gv
        nv = ADAM_B2 * v_ref[...] + (1.0 - ADAM_B2) * jnp.square(gv)
        m_hat = nm / (1.0 - ADAM_B1 ** ADAM_STEP)
        v_hat = nv / (1.0 - ADAM_B2 ** ADAM_STEP)
        d_ref[...] = -ADAM_LR * (m_hat / (jnp.sqrt(v_hat) + ADAM_EPS) + ADAM_WD * w_ref[...])
        nm_ref[...] = nm
        nv_ref[...] = nv

    return _pcall(
        body, name=name, grid=(depth, r // t),
        in_specs=[blk] * 4, out_specs=[blk] * 3, out_shape=[_sds(w.shape, F32)] * 3)(w, g, m, v)


BEFORE_ATTN = ("w_in", "w_uq", "w_ukv")
AFTER_ATTN = ("w_br_mla", "w_br_sg", "w_br_conv", "w_br_pool", "w_out", "w_ff1", "w_ff2", "conv_w")
SHARDED = BEFORE_ATTN + AFTER_ATTN
ROW_SHARDED = ("w_out", "w_ff2")
REPLICATED = ("norm_mix_pre", "gate_b", "q_norm", "kv_norm", "sg_ln_g", "sg_ln_b", "sg_w", "sg_b", "pool_w",
              "pool_scale", "norm_mix_post", "norm_ffn_pre", "norm_ffn_post")
ROW_PAD = 256
GRAD_ROW_PAD = 64
PART_ROWS = 16


def _pack_rows(arrays, dtype, multiple, lead=()):
    rows, offsets, at = [], [], 0
    zero_pad = ((0, 0),) * len(lead)
    for a in arrays:
        if a.ndim == len(lead) + 2 and a.shape[-1] == D_MODEL:
            n = a.shape[-2]
            n_pad = -(-n // PART_ROWS) * PART_ROWS
            rows.append(jnp.pad(a.astype(dtype), zero_pad + ((0, n_pad - n), (0, 0))))
        else:
            flat = a.reshape(lead + (-1,)).astype(dtype)
            n = -(-flat.shape[-1] // D_MODEL)
            n_pad = -(-n // PART_ROWS) * PART_ROWS
            flat = jnp.pad(flat, zero_pad + ((0, n_pad * D_MODEL - flat.shape[-1]),))
            rows.append(flat.reshape(lead + (n_pad, D_MODEL)))
        offsets.append((at, n))
        at += n_pad
    pad = -at % multiple
    if pad:
        rows.append(jnp.zeros(lead + (pad, D_MODEL), dtype))
    return jnp.concatenate(rows, axis=len(lead)), offsets


def _unpack_rows(buf, offsets, shapes):
    lead = buf.shape[:-2]
    out = []
    for (at, n), shape in zip(offsets, shapes):
        if tuple(shape) == (n, D_MODEL):
            out.append(buf[..., at:at + n, :])
            continue
        size = math.prod(shape)
        flat = buf[..., at:at + n, :].reshape(lead + (n * D_MODEL,))[..., :size]
        out.append(flat.reshape(lead + tuple(shape)))
    return out


def _pack_shards(p, depth, names):
    arrays = []
    for n in names:
        a = p[n][0] if n in ROW_SHARDED else jnp.swapaxes(p[n][0], -1, -2)
        arrays.append(lax.bitcast_convert_type(a, BF16) if n == "conv_w" else a.astype(BF16))
    packed, offsets = _pack_rows(arrays, BF16, GRAD_ROW_PAD, lead=(depth,))
    return packed, offsets, [a.shape[1:] for a in arrays]


def _unpack_layer_weights(names, gathered, offsets, shapes):
    full = {}
    for n, gth in zip(names, _unpack_rows(gathered, offsets, shapes)):
        if n == "conv_w":
            gth = lax.bitcast_convert_type(gth, F32)
        full[n] = gth.reshape((-1,) + gth.shape[2:])
        if n == "conv_w":
            full[n] = full[n].T
    return full


def _owner_major(grad_sharded_dim_first):
    a = grad_sharded_dim_first
    return a.reshape((N_DEV, a.shape[0] // N_DEV) + a.shape[1:])


def _owner_block_shape(name, shard_shape):
    k, n = shard_shape[-2:]
    return (k, n) if name in ROW_SHARDED else (n, k)


def _natural_shard(name, blocks):
    return blocks if name in ROW_SHARDED else jnp.swapaxes(blocks, -1, -2)


def _swap_halves(a, axis=-1):
    lo, hi = jnp.split(a, 2, axis=axis)
    return jnp.concatenate([hi, lo], axis=axis)


def _pad_rows(a, top, total):
    return jnp.pad(a, ((0, 0),) * (a.ndim - 2) + ((top, total - top - a.shape[-2]), (0, 0)))


def _weights_before_attn(full, small, l):
    w_in = full["w_in"]
    k_r = w_in[384:416]
    w_in_k = jnp.concatenate(
        [w_in[0:384], _pad_rows(k_r, D_NOPE, HEAD_PAD), _pad_rows(_swap_halves(k_r, axis=0), D_NOPE, HEAD_PAD), w_in[416:1952]], axis=0)
    w_uq = full["w_uq"].reshape(N_HEADS, D_NOPE + D_ROPE, Q_RANK)
    wq = _pad_rows(w_uq, 0, HEAD_PAD).reshape(N_HEADS * HEAD_PAD, Q_RANK)
    wqs = _pad_rows(_swap_halves(w_uq[:, D_NOPE:], axis=1), D_NOPE, HEAD_PAD).reshape(N_HEADS * HEAD_PAD, Q_RANK)
    w_ukv = full["w_ukv"].reshape(N_HEADS, D_NOPE + D_V, KV_RANK)
    wk = _pad_rows(w_ukv[:, :D_NOPE], 0, HEAD_PAD).reshape(N_HEADS * HEAD_PAD, KV_RANK)
    wv = w_ukv[:, D_NOPE:].reshape(N_HEADS * D_V, KV_RANK)
    tri = jnp.tril(jnp.ones((SG_CHUNK, SG_CHUNK), bool))
    wm = jnp.where(tri, small["sg_w"][l], 0.0)
    pool_w = small["pool_w"][l]
    wp = jnp.zeros((BR_W, BR_W), F32)
    for g in range(len(POOL_WINDOWS)):
        wp = wp.at[g * POOL_GROUP:(g + 1) * POOL_GROUP, g * POOL_GROUP:(g + 1) * POOL_GROUP].set(pool_w[g])
    vec = lambda name: small[name][l][None, :]
    return dict(
        w_in=w_in_k, wg=w_in[1952:], bg=vec("gate_b"), g1=vec("norm_mix_pre"), qn=vec("q_norm"), kvn=vec("kv_norm"),
        wq=wq, wqs=wqs, wk=wk, wv=wv,
        ln_g=vec("sg_ln_g"), ln_b=vec("sg_ln_b"), wm=wm.astype(BF16), wmt=jnp.swapaxes(wm, 1, 2).astype(BF16),
        sgb=jnp.repeat(small["sg_b"][l].T, POOL_GROUP, axis=1), wp=wp.astype(BF16), pscale=vec("pool_scale"),
        g2=vec("norm_mix_post"), g3=vec("norm_ffn_pre"), g4=vec("norm_ffn_post"))


def _weights_after_attn(full):
    return dict(
        conv_w=full["conv_w"], wbr=jnp.stack([full["w_br_mla"], full["w_br_sg"], full["w_br_conv"], full["w_br_pool"]]),
        wout=full["w_out"], w1=full["w_ff1"], w2=full["w_ff2"])


def _grads_after_attn(g):
    own = _owner_major
    return dict(
        w_br_mla=own(g["dwbr"][0]), w_br_sg=own(g["dwbr"][1]), w_br_conv=own(g["dwbr"][2]), w_br_pool=own(g["dwbr"][3]),
        w_out=own(g["dwout"]), w_ff1=own(g["dw1"]), w_ff2=own(g["dw2"]), conv_w=own(g["dcw"][:CONV_K].T))


def _grads_before_attn(g):
    dwa = g["dwa"]
    rope_rows = slice(D_NOPE, D_NOPE + D_ROPE)
    d_kr = dwa[384:512][rope_rows].astype(F32) + _swap_halves(dwa[512:640][rope_rows], axis=0).astype(F32)
    w_in = jnp.concatenate([dwa[0:384], d_kr.astype(BF16), g["dwm_in"]] + list(g["dwg"]), axis=0)
    dwq, dwqs = (g[n].reshape(N_HEADS, HEAD_PAD, Q_RANK) for n in ("dwq", "dwqs"))
    d_rope = dwq[:, rope_rows] + _swap_halves(dwqs[:, rope_rows], axis=1)
    w_uq = jnp.concatenate([dwq[:, :D_NOPE], d_rope], axis=1).reshape(N_HEADS * (D_NOPE + D_ROPE), Q_RANK)
    dwk = g["dwk"].reshape(N_HEADS, HEAD_PAD, KV_RANK)
    w_ukv = jnp.concatenate([dwk[:, :D_NOPE], g["dwv"].reshape(N_HEADS, D_V, KV_RANK)], axis=1)
    w_ukv = w_ukv.reshape(N_HEADS * (D_NOPE + D_V), KV_RANK)
    pool_w = jnp.stack([g["dwp"][i * POOL_GROUP:(i + 1) * POOL_GROUP, i * POOL_GROUP:(i + 1) * POOL_GROUP]
                        for i in range(len(POOL_WINDOWS))])
    sg_b = g["dsgb"].reshape(SG_CHUNK, SG_GROUPS, POOL_GROUP).sum(axis=-1).T
    own = _owner_major
    return dict(
        w_in=own(w_in), w_uq=own(w_uq), w_ukv=own(w_ukv),
        norm_mix_pre=g["dg1"][0], gate_b=jnp.concatenate([b[0] for b in g["dbg"]]), q_norm=g["dqn"][0], kv_norm=g["dkvn"][0],
        sg_ln_g=g["dlng"][0], sg_ln_b=g["dlnb"][0], sg_w=g["dwm"], sg_b=sg_b, pool_w=pool_w, pool_scale=g["dps"][0],
        norm_mix_post=g["dg2"][0], norm_ffn_pre=g["dg3"][0], norm_ffn_post=g["dg4"][0])


def _rope_tables(positions):
    inv_freq = ROPE_BASE ** (-jnp.arange(0, D_ROPE, 2, dtype=F32) / D_ROPE)
    ang = positions.astype(F32)[:, None] * inv_freq
    cos, sin = jnp.cos(ang), jnp.sin(ang)
    s = positions.shape[0]
    cosf = jnp.concatenate([jnp.ones((s, D_NOPE), F32), cos, cos, jnp.zeros((s, HEAD_PAD - D_NOPE - D_ROPE), F32)], axis=1)
    sins = jnp.concatenate([jnp.zeros((s, D_NOPE), F32), -sin, sin, jnp.zeros((s, HEAD_PAD - D_NOPE - D_ROPE), F32)], axis=1)
    return cosf, sins


def _layer_fwd(x, w, weights_after_attn, cosf, sins, tiles, riding):
    t, ta = tiles["tok"], tiles["attn"]
    hb, pa, pm = _inproj_fwd(x, w["g1"], w["w_in"], tiles["wgrad"])
    q, k, v = _attn_prep_fwd(pa, cosf, sins, w["qn"], w["kvn"], w["wq"], w["wqs"], w["wk"], w["wv"], tiles["wgrad"])
    o, lse, landed = _flash_fwd(q, k, v, ta, riding)
    w = {**w, **weights_after_attn(landed)}
    bsg, bcv, bpl = _mixers_fwd(pm, w["ln_g"], w["ln_b"], w["wm"], w["sgb"], w["conv_w"], w["wp"], w["pscale"], tiles["wgrad"])
    x_mid, merged, tt = _merge_fwd(x, hb, (o, bsg, bcv, bpl), w["wg"], w["bg"], w["wbr"], w["wout"], w["g2"], t)
    x_out, f = _ffn_fwd(x_mid, w["g3"], w["w1"], w["w2"], w["g4"], t)
    saved = dict(x=x, hb=hb, pa=pa, pm=pm, q=q, k=k, v=v, o=o, lse=lse, bsg=bsg, bcv=bcv, bpl=bpl, x_mid=x_mid,
                 merged=merged, tt=tt, f=f)
    return x_out, saved, w, landed


def _layer_bwd(dxo, a, w, cosf, sins, tiles, pending):
    t, ta, tb, tw = tiles["tok"], tiles["attn"], tiles["inproj_bwd"], tiles["wgrad"]
    g = {}
    h2, df, g["dg4"] = _ffn_bwd_norms(a["x_mid"], a["f"], dxo, w["g3"], w["g4"], tw)
    da, g["dw1"], g["dw2"] = _ffn_bwd_weights(h2, df, w["w1"], w["w2"], tw)
    dxmid, g["dg3"] = _ffn_bwd_input(da, w["w1"], a["x_mid"], dxo, w["g3"], t)
    dm, g["dwout"], g["dg2"] = _merge_bwd_out(a["tt"], dxmid, w["g2"], w["wout"], a["merged"], tw)
    dbrs, dpres, g["dwg"], g["dbg"], g["dwbr"] = [], [], [], [], []
    for k, br in enumerate((a["o"], a["bsg"], a["bcv"], a["bpl"])):
        dbr, dpre, dwg, dbg, dwbr = _merge_bwd_branch(k, a["hb"], br, dm, w["wg"], w["bg"], w["wbr"], tw)
        dbrs.append(dbr); dpres.append(dpre); g["dwg"].append(dwg); g["dbg"].append(dbg); g["dwbr"].append(dwbr)
    dpm, g["dlng"], g["dlnb"], g["dwm"], g["dsgb"], g["dcw"], g["dwp"], g["dps"] = _mixers_bwd(
        a["pm"], dbrs[1], dbrs[2], dbrs[3], w["ln_g"], w["ln_b"], w["wm"], w["wmt"], w["sgb"], w["conv_w"], w["wp"], w["pscale"], tw)
    after = _grads_after_attn(g)
    send, after_offsets = _pack_rows([after[n] for n in AFTER_ATTN], BF16, GRAD_ROW_PAD, lead=(N_DEV,))
    after_rows = send.shape[1]
    if pending is not None:
        send = jnp.concatenate([send, pending], axis=1)
    riding = _riding(_exchange_plan, send, _sds(send.shape, send.dtype))
    dq, dk, dv, landed = _flash_bwd(a["q"], a["k"], a["v"], a["o"], dbrs[0], a["lse"], ta, riding)
    dpa, g["dwq"], g["dwqs"], g["dwk"], g["dwv"], g["dqn"], g["dkvn"] = _attn_prep_bwd(
        a["pa"], dq, dk, dv, cosf, sins, w["qn"], w["kvn"], w["wq"], w["wqs"], w["wk"], w["wv"], tw)
    dx, g["dg1"], g["dwa"], g["dwm_in"] = _inproj_bwd(a["x"], w["g1"], dxmid, dpa, dpm, dpres, w["w_in"], w["wg"], tb)
    return dx, _grads_before_attn(g), landed, (after_rows, after_offsets)


def _step(p, x, positions, loss_target):
    s = x.shape[0]
    depth = p["w_in"][0].shape[0]
    tiles = dict(tok=min(TOK_TILE, s), attn=min(ATTN_TILE, s), inproj_bwd=min(INPROJ_BWD_TILE, s),
                 wgrad=min(WGRAD_TILE, s))

    small = {n: p[n][0] for n in REPLICATED}
    cosf, sins = _rope_tables(positions)

    pack_b, off_b, shapes_b = _pack_shards(p, depth, BEFORE_ATTN)
    pack_a, off_a, shapes_a = _pack_shards(p, depth, AFTER_ATTN)
    rows_a = pack_a.shape[1]
    before = _all_gather(pack_b[0], "weight_all_gather")
    weights, acts = [], []
    h = x
    for l in range(depth):
        w = _weights_before_attn(_unpack_layer_weights(BEFORE_ATTN, before, off_b, shapes_b), small, l)
        src = pack_a[l] if l + 1 == depth else jnp.concatenate([pack_a[l], pack_b[l + 1]], axis=0)
        riding = _riding(_gather_plan, src, _sds((N_DEV,) + src.shape, src.dtype))

        def after(landed):
            return _weights_after_attn(_unpack_layer_weights(AFTER_ATTN, landed[:, :rows_a], off_a, shapes_a))

        h, saved, w, landed = _layer_fwd(h, w, after, cosf, sins, tiles, riding)
        weights.append(w)
        acts.append(saved)
        before = landed[:, rows_a:] if l + 1 < depth else None
    dh, loss_blk = _loss_head(h, loss_target, tiles["tok"])
    loss = lax.psum(loss_blk[0, 0], ("x", "y", "c"))

    grads_b = [None] * depth
    sum_a, sum_b = [None] * depth, [None] * depth
    pending = None
    for l in reversed(range(depth)):
        dh, grads_b[l], landed, (rows_ga, off_ga) = _layer_bwd(dh, acts[l], weights[l], cosf, sins, tiles, pending)
        summed = _sum_slots(landed, "grad_shard_sum")
        sum_a[l] = summed[:rows_ga]
        if pending is not None:
            sum_b[l + 1] = summed[rows_ga:]
        pending, off_gb = _pack_rows([grads_b[l][n] for n in BEFORE_ATTN], BF16, GRAD_ROW_PAD, lead=(N_DEV,))
    sum_b[0] = _sum_slots(_all_to_all(pending, "grad_all_to_all"), "grad_shard_sum")
    g_shard = {}
    for names, sums, offsets in ((AFTER_ATTN, sum_a, off_ga), (BEFORE_ATTN, sum_b, off_gb)):
        block_shapes = [_owner_block_shape(n, p[n][0].shape) for n in names]
        for n, b in zip(names, _unpack_rows(jnp.stack(sums), offsets, block_shapes)):
            g_shard[n] = _natural_shard(n, b)
    small_grads = [jnp.stack([grads_b[l][n] for l in range(depth)]) for n in REPLICATED]
    small_rows, small_offsets = _pack_rows(small_grads, F32, ROW_PAD)
    g_small_rows = _sum_slots(_all_gather(small_rows, "replicated_grad_all_gather"), "grad_replicated_sum")

    grad, delta, new_m, new_v = {}, {}, {}, {}
    for n in SHARDED:
        w, m, v = p[n]
        grad[n] = g_shard[n]
        delta[n], new_m[n], new_v[n] = _adamw(w, grad[n], m, v, f"adamw_{n}")
    packs = [_pack_rows([p[n][i] for n in REPLICATED], F32, ROW_PAD)[0][None] for i in range(3)]
    d, nm, nv = (a[0] for a in _adamw(packs[0], g_small_rows[None], packs[1], packs[2], "adamw_replicated"))
    shapes = [p[n][0].shape for n in REPLICATED]
    for n, gg, dd, mm, vv in zip(REPLICATED, _unpack_rows(g_small_rows, small_offsets, shapes), _unpack_rows(d, small_offsets, shapes),
                                 _unpack_rows(nm, small_offsets, shapes), _unpack_rows(nv, small_offsets, shapes)):
        grad[n], delta[n], new_m[n], new_v[n] = gg, dd, mm, vv
    return loss, dh, grad, delta, new_m, new_v


WEIGHT_ORDER = ("norm_mix_pre", "w_in", "gate_b", "q_norm", "w_uq", "kv_norm", "w_ukv", "w_br_mla", "sg_ln_g", "sg_ln_b",
                "sg_w", "sg_b", "w_br_sg", "conv_w", "w_br_conv", "pool_w", "pool_scale", "w_br_pool", "w_out",
                "norm_mix_post", "norm_ffn_pre", "w_ff1", "w_ff2", "norm_ffn_post")


def kernel(x, positions, norm_mix_pre, w_in, gate_b, q_norm, w_uq, kv_norm, w_ukv, w_br_mla, sg_ln_g, sg_ln_b, sg_w, sg_b, w_br_sg, conv_w, w_br_conv, pool_w, pool_scale, w_br_pool, w_out, norm_mix_post, norm_ffn_pre, w_ff1, w_ff2, norm_ffn_post, loss_target, m_norm_mix_pre, m_w_in, m_gate_b, m_q_norm, m_w_uq, m_kv_norm, m_w_ukv, m_w_br_mla, m_sg_ln_g, m_sg_ln_b, m_sg_w, m_sg_b, m_w_br_sg, m_conv_w, m_w_br_conv, m_pool_w, m_pool_scale, m_w_br_pool, m_w_out, m_norm_mix_post, m_norm_ffn_pre, m_w_ff1, m_w_ff2, m_norm_ffn_post, v_norm_mix_pre, v_w_in, v_gate_b, v_q_norm, v_w_uq, v_kv_norm, v_w_ukv, v_w_br_mla, v_sg_ln_g, v_sg_ln_b, v_sg_w, v_sg_b, v_w_br_sg, v_conv_w, v_w_br_conv, v_pool_w, v_pool_scale, v_w_br_pool, v_w_out, v_norm_mix_post, v_norm_ffn_pre, v_w_ff1, v_w_ff2, v_norm_ffn_post):
    ws = (norm_mix_pre, w_in, gate_b, q_norm, w_uq, kv_norm, w_ukv, w_br_mla, sg_ln_g, sg_ln_b, sg_w, sg_b, w_br_sg, conv_w,
          w_br_conv, pool_w, pool_scale, w_br_pool, w_out, norm_mix_post, norm_ffn_pre, w_ff1, w_ff2, norm_ffn_post)
    ms = (m_norm_mix_pre, m_w_in, m_gate_b, m_q_norm, m_w_uq, m_kv_norm, m_w_ukv, m_w_br_mla, m_sg_ln_g, m_sg_ln_b, m_sg_w,
          m_sg_b, m_w_br_sg, m_conv_w, m_w_br_conv, m_pool_w, m_pool_scale, m_w_br_pool, m_w_out, m_norm_mix_post,
          m_norm_ffn_pre, m_w_ff1, m_w_ff2, m_norm_ffn_post)
    vs = (v_norm_mix_pre, v_w_in, v_gate_b, v_q_norm, v_w_uq, v_kv_norm, v_w_ukv, v_w_br_mla, v_sg_ln_g, v_sg_ln_b, v_sg_w,
          v_sg_b, v_w_br_sg, v_conv_w, v_w_br_conv, v_pool_w, v_pool_scale, v_w_br_pool, v_w_out, v_norm_mix_post,
          v_norm_ffn_pre, v_w_ff1, v_w_ff2, v_norm_ffn_post)
    p = {n: (w, m, v) for n, w, m, v in zip(WEIGHT_ORDER, ws, ms, vs)}
    loss, grad_x, grad, delta, new_m, new_v = _step(p, x[0], positions[0], loss_target[0])
    return (loss, grad_x[None], *[grad[n] for n in WEIGHT_ORDER], *[delta[n] for n in WEIGHT_ORDER],
            *[new_m[n] for n in WEIGHT_ORDER], *[new_v[n] for n in WEIGHT_ORDER])
```

```python
import functools
import math

import jax
import jax.numpy as jnp
from jax import lax
from jax.experimental import pallas as pl
from jax.experimental.pallas import tpu as pltpu

F32 = jnp.float32
BF16 = jnp.bfloat16

D_MODEL = 1024
N_HEADS = 4
D_NOPE = 64
D_ROPE = 32
D_V = 64
Q_RANK = 256
KV_RANK = 128
BR_W = 256
SG_CHUNK = 128
SG_GROUPS = 4
POOL_WINDOWS = (2, 4, 8, 16)
POOL_GROUP = 64
CONV_K = 3
D_FF = 4096
N_BRANCH = 4
N_IN = 6048
EPS = 1e-6
ROPE_BASE = 10000.0
ADAM_LR = 0.001
ADAM_B1 = 0.9
ADAM_B2 = 0.999
ADAM_EPS = 1e-08
ADAM_WD = 0.01
ADAM_STEP = 10

N_DEV = 8
LANES = 128
HEAD_PAD = 128
HALO = 16
PA_W = 640
PM_W = 6 * BR_W
VMEM_LIMIT = 56 * 1024 * 1024
SM_SCALE = (D_NOPE + D_ROPE) ** -0.5
TOK_TILE = 512
ATTN_TILE = 1024
WGRAD_TILE = 1024
INPROJ_BWD_TILE = 512

NN = (((1,), (0,)), ((), ()))
NT = (((1,), (1,)), ((), ()))
TN = (((0,), (0,)), ((), ()))
MESH_ID = pl.DeviceIdType.MESH


def _dot(a, b, dims=NN):
    return lax.dot_general(a, b, dims, preferred_element_type=F32)


def _pcall(body, *, name, grid, in_specs, out_specs, out_shape, scratch=(), vmem=None, prefetch=0):
    params = pltpu.CompilerParams(vmem_limit_bytes=vmem)
    if prefetch:
        spec = pltpu.PrefetchScalarGridSpec(num_scalar_prefetch=prefetch, grid=grid, in_specs=in_specs,
                                            out_specs=out_specs, scratch_shapes=scratch)
        return pl.pallas_call(body, name=name, grid_spec=spec, out_shape=out_shape, compiler_params=params)
    return pl.pallas_call(
        body, name=name, grid=grid, in_specs=in_specs, out_specs=out_specs, out_shape=out_shape,
        scratch_shapes=scratch, compiler_params=params)


def _rows(t, width):
    return pl.BlockSpec((t, width), lambda i: (i, 0))


def _whole(shape):
    nd = len(shape)
    return pl.BlockSpec(tuple(shape), lambda *_: (0,) * nd, pipeline_mode=pl.Buffered(1))


def _acc(shape):
    nd = len(shape)
    return pl.BlockSpec(tuple(shape), lambda *_: (0,) * nd)


def _sds(shape, dtype):
    return jax.ShapeDtypeStruct(tuple(shape), dtype)


def _rms(x, g):
    return x * lax.rsqrt(jnp.mean(x * x, axis=-1, keepdims=True) + EPS) * g


def _rms_bwd(x, g, dy):
    r = lax.rsqrt(jnp.mean(x * x, axis=-1, keepdims=True) + EPS)
    xh = x * r
    dg = jnp.sum(dy * xh, axis=0, keepdims=True)
    dxh = dy * g
    dx = r * (dxh - xh * jnp.mean(dxh * xh, axis=-1, keepdims=True))
    return dx, dg


def _sigmoid(x):
    return 1.0 / (1.0 + jnp.exp(-x))


def _gelu(x):
    return jax.nn.gelu(x, approximate=True)


def _accumulate(ref, val, first):
    @pl.when(first)
    def _():
        ref[...] = val

    @pl.when(jnp.logical_not(first))
    def _():
        ref[...] += val


def _accumulate_out(acc_ref, out_ref, val, first, last):
    _accumulate(acc_ref, val, first)

    @pl.when(last)
    def _():
        out_ref[...] = acc_ref[...].astype(out_ref.dtype)


def _inproj_fwd(x, g1, w_in, t):
    s = x.shape[0]

    def body(x_ref, g_ref, w_ref, hb_ref, pa_ref, pm_ref):
        hb = _rms(x_ref[...], g_ref[...]).astype(BF16)
        hb_ref[...] = hb
        pa_ref[...] = _dot(hb, w_ref[:PA_W, :], NT)
        pm_ref[...] = _dot(hb, w_ref[PA_W:, :], NT)

    return _pcall(
        body, name="inproj_fwd", grid=(s // t,),
        in_specs=[_rows(t, D_MODEL), _whole((1, D_MODEL)), _whole(w_in.shape)],
        out_specs=[_rows(t, D_MODEL), _rows(t, PA_W), _rows(t, PM_W)],
        out_shape=[_sds((s, D_MODEL), BF16), _sds((s, PA_W), F32), _sds((s, PM_W), F32)],
        vmem=VMEM_LIMIT)(x, g1, w_in)


def _attn_prep_fwd(pa, cosf, sins, qn, kvn, wq, wqs, wk, wv, t):
    s = pa.shape[0]

    def body(pa_ref, cos_ref, sin_ref, qn_ref, kvn_ref, wq_ref, wqs_ref, wk_ref, wv_ref, q_ref, k_ref, v_ref):
        cosv, sinv = cos_ref[...], sin_ref[...]
        cqn = _rms(pa_ref[:, 0:Q_RANK], qn_ref[...]).astype(BF16)
        ckvn = _rms(pa_ref[:, Q_RANK:Q_RANK + KV_RANK], kvn_ref[...]).astype(BF16)
        k_rope = pa_ref[:, 384:512] * cosv + pa_ref[:, 512:640] * sinv
        q_all, qs_all = _dot(cqn, wq_ref[...], NT), _dot(cqn, wqs_ref[...], NT)
        k_all = _dot(ckvn, wk_ref[...], NT)
        for h in range(N_HEADS):
            lanes = slice(h * HEAD_PAD, (h + 1) * HEAD_PAD)
            q_ref[h] = (q_all[:, lanes] * cosv + qs_all[:, lanes] * sinv).astype(BF16)
            k_ref[h] = (k_all[:, lanes] + k_rope).astype(BF16)
        v_ref[...] = _dot(ckvn, wv_ref[...], NT).astype(BF16)

    head_blk = pl.BlockSpec((N_HEADS, t, HEAD_PAD), lambda i: (0, i, 0))
    return _pcall(
        body, name="attn_prep_fwd", grid=(s // t,),
        in_specs=[_rows(t, PA_W), _rows(t, LANES), _rows(t, LANES), _whole(qn.shape), _whole(kvn.shape),
                  _whole(wq.shape), _whole(wqs.shape), _whole(wk.shape), _whole(wv.shape)],
        out_specs=[head_blk, head_blk, _rows(t, BR_W)],
        out_shape=[_sds((N_HEADS, s, HEAD_PAD), BF16), _sds((N_HEADS, s, HEAD_PAD), BF16), _sds((s, BR_W), BF16)],
        vmem=VMEM_LIMIT)(pa, cosf, sins, qn, kvn, wq, wqs, wk, wv)


def _causal_scores(q, k, masked):
    sc = _dot(q, k, NT) * SM_SCALE
    if masked:
        row = lax.broadcasted_iota(jnp.int32, sc.shape, 0)
        col = lax.broadcasted_iota(jnp.int32, sc.shape, 1)
        sc = jnp.where(col <= row, sc, -jnp.inf)
    return sc


def _diagonal_parts(t):
    half = t // 2
    if half % LANES:
        return [(True, slice(0, t), slice(0, t))]
    lower, upper = slice(0, half), slice(half, t)
    return [(True, lower, lower), (False, upper, lower), (True, upper, upper)]


def _causal_steps(n, key_major):
    if key_major:
        pairs = [(qi, kj) for kj in range(n) for qi in range(kj, n)]
    else:
        pairs = [(qi, kj) for qi in range(n) for kj in range(qi + 1)]
    return (jnp.asarray([p[0] for p in pairs], jnp.int32), jnp.asarray([p[1] for p in pairs], jnp.int32))


def _flash_fwd(q, k, v, t, riding):
    s = v.shape[0]
    n = s // t
    qi_tab, kj_tab = _causal_steps(n, key_major=False)
    n_steps = int(qi_tab.shape[0])
    whole = slice(0, t)

    def body(qi_ref, kj_ref, q_ref, k_ref, v_ref, x_ref, o_ref, lse_ref, land_ref, m_scr, l_scr, acc_scr,
             send_sems, recv_sems, local_sem):
        step_id = pl.program_id(0)
        qi, kj = qi_ref[step_id], kj_ref[step_id]
        at_end = _ride_along(riding, step_id, n_steps, (x_ref, land_ref, send_sems, recv_sems, local_sem))

        @pl.when(kj == 0)
        def _():
            m_scr[...] = jnp.full(m_scr.shape, -jnp.inf, F32)
            l_scr[...] = jnp.zeros(l_scr.shape, F32)
            acc_scr[...] = jnp.zeros(acc_scr.shape, F32)

        def step(masked, qr, kr):
            for h in range(N_HEADS):
                sc = _causal_scores(q_ref[h, qr, :], k_ref[h, kr, :], masked)
                m_prev = m_scr[h, qr, :]
                m_next = jnp.maximum(m_prev, jnp.max(sc, axis=1, keepdims=True))
                alpha = jnp.exp(m_prev - m_next)
                p = jnp.exp(sc - jnp.tile(m_next, (1, sc.shape[1] // LANES)))
                l_scr[h, qr, :] = alpha * l_scr[h, qr, :] + jnp.sum(p, axis=1, keepdims=True)
                m_scr[h, qr, :] = m_next
                pair = slice((h // 2) * LANES, (h // 2 + 1) * LANES)
                acc_scr[h, qr, :] = acc_scr[h, qr, :] * alpha + _dot(p.astype(BF16), v_ref[kr, pair])

        @pl.when(kj < qi)
        def _():
            step(False, whole, whole)

        @pl.when(kj == qi)
        def _():
            for masked, qr, kr in _diagonal_parts(t):
                step(masked, qr, kr)
            lane = lax.broadcasted_iota(jnp.int32, (t, LANES), 1)
            for pr in range(N_HEADS // 2):
                o0 = acc_scr[2 * pr] / l_scr[2 * pr]
                o1 = acc_scr[2 * pr + 1] / l_scr[2 * pr + 1]
                o_ref[:, pr * LANES:(pr + 1) * LANES] = jnp.where(lane < D_V, o0, o1).astype(BF16)
            for h in range(N_HEADS):
                lse_ref[h] = m_scr[h] + jnp.log(l_scr[h])

        at_end()

    q_blk = pl.BlockSpec((N_HEADS, t, HEAD_PAD), lambda i, qi, kj: (0, qi[i], 0))
    k_blk = pl.BlockSpec((N_HEADS, t, HEAD_PAD), lambda i, qi, kj: (0, kj[i], 0))
    v_blk = pl.BlockSpec((t, BR_W), lambda i, qi, kj: (kj[i], 0))
    return _pcall(
        body, name="flash_fwd_gathering", grid=(n_steps,), prefetch=2,
        in_specs=[q_blk, k_blk, v_blk, HBM_REF],
        out_specs=[pl.BlockSpec((t, BR_W), lambda i, qi, kj: (qi[i], 0)), q_blk, HBM_REF],
        out_shape=[_sds((s, BR_W), BF16), _sds((N_HEADS, s, LANES), F32), riding[2]],
        scratch=[pltpu.VMEM((N_HEADS, t, LANES), F32)] * 3 + list(EXCHANGE_SEMS),
        vmem=VMEM_LIMIT)(qi_tab, kj_tab, q, k, v, riding[1])


def _shift_down(ext, k, t):
    return pltpu.roll(ext, k, 0)[HALO:HALO + t]


def _shift_up(ext, k, t):
    return pltpu.roll(ext, t + HALO - k, 0)[0:t]


def _lane_group(shape):
    return lax.shift_right_logical(lax.broadcasted_iota(jnp.int32, shape, 1), 6)


def _group_select(vals):
    grp = _lane_group(vals[0].shape)
    out = vals[0]
    for g in range(1, len(vals)):
        out = jnp.where(grp == g, vals[g], out)
    return out


def _layernorm(x, g, b):
    mu = jnp.mean(x, axis=-1, keepdims=True)
    xc = x - mu
    return xc * lax.rsqrt(jnp.mean(xc * xc, axis=-1, keepdims=True) + EPS) * g + b


def _sg_mix(wm_ref, vnb, bias):
    return _group_select([_dot(wm_ref[g], vnb) for g in range(SG_GROUPS)]) + bias


def _pool_windows(ext):
    s2 = ext + pltpu.roll(ext, 1, 0)
    s4 = s2 + pltpu.roll(s2, 2, 0)
    s8 = s4 + pltpu.roll(s4, 4, 0)
    s16 = s8 + pltpu.roll(s8, 8, 0)
    return [s2, s4, s8, s16]


def _pool_counts(tok):
    return [jnp.minimum(tok + 1, w).astype(F32) for w in POOL_WINDOWS]


def _halo_before(t):
    return pl.BlockSpec((HALO, PM_W), lambda i: (jnp.maximum(i * (t // HALO) - 1, 0), 0))


def _mixers_fwd(pm, ln_g, ln_b, wm, sgb, conv_w, wp, pscale, t):
    s = pm.shape[0]

    def body(pm_ref, halo_ref, lng_ref, lnb_ref, wm_ref, sgb_ref, cw_ref, wp_ref, ps_ref, bsg_ref, bcv_ref, bpl_ref):
        i = pl.program_id(0)
        halo = jnp.where(i > 0, halo_ref[...], 0.0)
        u = _gelu(pm_ref[:, 0:256])
        vnb = _layernorm(_gelu(pm_ref[:, 256:512]), lng_ref[...], lnb_ref[...]).astype(BF16)
        for c in range(t // SG_CHUNK):
            rows = slice(c * SG_CHUNK, (c + 1) * SG_CHUNK)
            bsg_ref[rows, :] = (u[rows] * _sg_mix(wm_ref, vnb[rows], sgb_ref[...])).astype(BF16)
        z = pm_ref[:, 1024:1280] * pm_ref[:, 512:768]
        zext = jnp.concatenate([halo[:, 1024:1280] * halo[:, 512:768], z], axis=0)
        y = cw_ref[0:1, :] * _shift_down(zext, 2, t) + cw_ref[1:2, :] * _shift_down(zext, 1, t) + cw_ref[2:3, :] * z
        bcv_ref[...] = (pm_ref[:, 768:1024] * y).astype(BF16)
        p = pm_ref[:, 1280:1536]
        sums = _pool_windows(jnp.concatenate([halo[:, 1280:1536], p], axis=0))
        tok = i * t + lax.broadcasted_iota(jnp.int32, (t, 1), 0)
        pooled = _group_select([sw[HALO:HALO + t] / cnt - p for sw, cnt in zip(sums, _pool_counts(tok))])
        bpl_ref[...] = (_dot(pooled.astype(BF16), wp_ref[...]) * ps_ref[...]).astype(BF16)

    return _pcall(
        body, name="mixers_fwd", grid=(s // t,),
        in_specs=[_rows(t, PM_W), _halo_before(t), _whole(ln_g.shape), _whole(ln_b.shape), _whole(wm.shape),
                  _whole(sgb.shape), _whole(conv_w.shape), _whole(wp.shape), _whole(pscale.shape)],
        out_specs=[_rows(t, BR_W)] * 3,
        out_shape=[_sds((s, BR_W), BF16)] * 3,
        vmem=VMEM_LIMIT)(pm, pm, ln_g, ln_b, wm, sgb, conv_w, wp, pscale)


def _merge_fwd(x, hb, branches, wg, bg, wbr, wout, g2, t):
    s = x.shape[0]

    def body(x_ref, hb_ref, b0_ref, b1_ref, b2_ref, b3_ref, wg_ref, bg_ref, wbr_ref, wout_ref, g2_ref,
             xmid_ref, mrg_ref, t_ref):
        hb = hb_ref[...]
        merged = jnp.zeros((t, D_MODEL), F32)
        for k, b_ref in enumerate((b0_ref, b1_ref, b2_ref, b3_ref)):
            cols = slice(k * D_MODEL, (k + 1) * D_MODEL)
            gate = _sigmoid(_dot(hb, wg_ref[cols, :], NT) + bg_ref[:, cols])
            merged = merged + gate * _dot(b_ref[...], wbr_ref[k], NT)
        mb = merged.astype(BF16)
        mrg_ref[...] = mb
        tt = _dot(mb, wout_ref[...])
        t_ref[...] = tt
        xmid_ref[...] = x_ref[...] + _rms(tt, g2_ref[...])

    return _pcall(
        body, name="merge_fwd", grid=(s // t,),
        in_specs=[_rows(t, D_MODEL), _rows(t, D_MODEL)] + [_rows(t, BR_W)] * 4 +
                 [_whole(wg.shape), _whole(bg.shape), _whole(wbr.shape), _whole(wout.shape), _whole(g2.shape)],
        out_specs=[_rows(t, D_MODEL)] * 3,
        out_shape=[_sds((s, D_MODEL), F32), _sds((s, D_MODEL), BF16), _sds((s, D_MODEL), F32)],
        vmem=VMEM_LIMIT)(x, hb, *branches, wg, bg, wbr, wout, g2)


def _ffn_fwd(x, g3, w1, w2, g4, t):
    s = x.shape[0]

    def body(x_ref, g3_ref, w1_ref, w2_ref, g4_ref, xo_ref, f_ref):
        h = _rms(x_ref[...], g3_ref[...]).astype(BF16)
        f = jnp.zeros((t, D_MODEL), F32)
        for j in range(D_FF // D_MODEL):
            cols = slice(j * D_MODEL, (j + 1) * D_MODEL)
            r = jnp.square(jnp.maximum(_dot(h, w1_ref[cols, :], NT), 0.0)).astype(BF16)
            f = f + _dot(r, w2_ref[cols, :])
        f_ref[...] = f
        xo_ref[...] = x_ref[...] + _rms(f, g4_ref[...])

    return _pcall(
        body, name="ffn_fwd", grid=(s // t,),
        in_specs=[_rows(t, D_MODEL), _whole(g3.shape), _whole(w1.shape), _whole(w2.shape), _whole(g4.shape)],
        out_specs=[_rows(t, D_MODEL)] * 2,
        out_shape=[_sds((s, D_MODEL), F32)] * 2,
        vmem=VMEM_LIMIT)(x, g3, w1, w2, g4)


def _loss_head(y, target, t):
    s = y.shape[0]
    n = s // t

    def body(y_ref, tg_ref, dy_ref, loss_ref, acc_scr):
        i = pl.program_id(0)
        d = y_ref[...] - tg_ref[...]
        dy_ref[...] = d * (1.0 / D_MODEL)
        _accumulate(acc_scr, jnp.sum(d * d, axis=0, keepdims=True), i == 0)

        @pl.when(i == n - 1)
        def _():
            loss_ref[...] = jnp.full(loss_ref.shape, 0.5 / D_MODEL, F32) * jnp.sum(acc_scr[...])

    return _pcall(
        body, name="loss_head", grid=(n,),
        in_specs=[_rows(t, D_MODEL)] * 2,
        out_specs=[_rows(t, D_MODEL), _acc((8, LANES))],
        out_shape=[_sds((s, D_MODEL), F32), _sds((8, LANES), F32)],
        scratch=[pltpu.VMEM((1, D_MODEL), F32)])(y, target)


def _ffn_bwd_norms(x_mid, f, dxo, g3, g4, t):
    s = x_mid.shape[0]

    def body(x_ref, f_ref, dxo_ref, g3_ref, g4_ref, h2_ref, df_ref, dg4_ref):
        h2_ref[...] = _rms(x_ref[...], g3_ref[...]).astype(BF16)
        df, dg4 = _rms_bwd(f_ref[...], g4_ref[...], dxo_ref[...])
        df_ref[...] = df.astype(BF16)
        _accumulate(dg4_ref, dg4, pl.program_id(0) == 0)

    return _pcall(
        body, name="ffn_bwd_norms", grid=(s // t,),
        in_specs=[_rows(t, D_MODEL)] * 3 + [_whole(g3.shape), _whole(g4.shape)],
        out_specs=[_rows(t, D_MODEL), _rows(t, D_MODEL), _acc((1, D_MODEL))],
        out_shape=[_sds((s, D_MODEL), BF16), _sds((s, D_MODEL), BF16), _sds((1, D_MODEL), F32)])(x_mid, f, dxo, g3, g4)


def _ffn_bwd_weights(h2, df, w1, w2, t):
    s = h2.shape[0]
    blk = D_MODEL
    n = s // t

    def body(h2_ref, df_ref, w1_ref, w2_ref, da_ref, dw1_ref, dw2_ref, acc1, acc2):
        first, last = pl.program_id(1) == 0, pl.program_id(1) == n - 1
        h2v, dfv = h2_ref[...], df_ref[...]
        rl = jnp.maximum(_dot(h2v, w1_ref[...], NT), 0.0)
        _accumulate_out(acc2, dw2_ref, _dot((rl * rl).astype(BF16), dfv, TN), first, last)
        da = (_dot(dfv, w2_ref[...], NT) * (2.0 * rl)).astype(BF16)
        da_ref[...] = da
        _accumulate_out(acc1, dw1_ref, _dot(da, h2v, TN), first, last)

    tok = pl.BlockSpec((t, D_MODEL), lambda j, i: (i, 0))
    hid = pl.BlockSpec((blk, D_MODEL), lambda j, i: (j, 0))
    return _pcall(
        body, name="ffn_bwd_weights", grid=(D_FF // blk, n),
        in_specs=[tok, tok, hid, hid],
        out_specs=[pl.BlockSpec((t, blk), lambda j, i: (i, j)), hid, hid],
        out_shape=[_sds((s, D_FF), BF16), _sds((D_FF, D_MODEL), BF16), _sds((D_FF, D_MODEL), BF16)],
        scratch=[pltpu.VMEM((blk, D_MODEL), F32)] * 2,
        vmem=VMEM_LIMIT)(h2, df, w1, w2)


def _ffn_bwd_input(da, w1, x_mid, dxo, g3, t):
    s = x_mid.shape[0]

    def body(da_ref, w1_ref, x_ref, dxo_ref, g3_ref, dx_ref, dg3_ref):
        dx, dg3 = _rms_bwd(x_ref[...], g3_ref[...], _dot(da_ref[...], w1_ref[...]))
        dx_ref[...] = dxo_ref[...] + dx
        _accumulate(dg3_ref, dg3, pl.program_id(0) == 0)

    return _pcall(
        body, name="ffn_bwd_input", grid=(s // t,),
        in_specs=[_rows(t, D_FF), _whole(w1.shape), _rows(t, D_MODEL), _rows(t, D_MODEL), _whole(g3.shape)],
        out_specs=[_rows(t, D_MODEL), _acc((1, D_MODEL))],
        out_shape=[_sds((s, D_MODEL), F32), _sds((1, D_MODEL), F32)],
        vmem=VMEM_LIMIT)(da, w1, x_mid, dxo, g3)


def _merge_bwd_out(tt, dxmid, g2, wout, merged, t):
    s = tt.shape[0]
    n = s // t

    def body(t_ref, dx_ref, g2_ref, wout_ref, mrg_ref, dm_ref, dwout_ref, dg2_ref, acc):
        first, last = pl.program_id(0) == 0, pl.program_id(0) == n - 1
        dt, dg2 = _rms_bwd(t_ref[...], g2_ref[...], dx_ref[...])
        dtb = dt.astype(BF16)
        dm_ref[...] = _dot(dtb, wout_ref[...], NT)
        _accumulate_out(acc, dwout_ref, _dot(mrg_ref[...], dtb, TN), first, last)
        _accumulate(dg2_ref, dg2, first)

    return _pcall(
        body, name="merge_bwd_out", grid=(n,),
        in_specs=[_rows(t, D_MODEL), _rows(t, D_MODEL), _whole(g2.shape), _whole(wout.shape), _rows(t, D_MODEL)],
        out_specs=[_rows(t, D_MODEL), _acc((D_MODEL, D_MODEL)), _acc((1, D_MODEL))],
        out_shape=[_sds((s, D_MODEL), F32), _sds((D_MODEL, D_MODEL), BF16), _sds((1, D_MODEL), F32)],
        scratch=[pltpu.VMEM((D_MODEL, D_MODEL), F32)],
        vmem=VMEM_LIMIT)(tt, dxmid, g2, wout, merged)


def _merge_bwd_branch(k, hb, br, dm, wg, bg, wbr, t):
    s = hb.shape[0]
    n = s // t

    def body(hb_ref, br_ref, dm_ref, wg_ref, bg_ref, wbr_ref, dbr_ref, dpre_ref, dwg_ref, dbg_ref, dwbr_ref,
             acc_g, acc_br):
        first, last = pl.program_id(0) == 0, pl.program_id(0) == n - 1
        hbv, brv, dmv, wbrv = hb_ref[...], br_ref[...], dm_ref[...], wbr_ref[0]
        gate = _sigmoid(_dot(hbv, wg_ref[...], NT) + bg_ref[...])
        dy = (dmv * gate).astype(BF16)
        dpre = dmv * _dot(brv, wbrv, NT) * gate * (1.0 - gate)
        dpb = dpre.astype(BF16)
        dpre_ref[...] = dpb
        dbr_ref[...] = _dot(dy, wbrv)
        _accumulate_out(acc_br, dwbr_ref, _dot(dy, brv, TN), first, last)
        _accumulate_out(acc_g, dwg_ref, _dot(dpb, hbv, TN), first, last)
        _accumulate(dbg_ref, jnp.sum(dpre, axis=0, keepdims=True), first)

    return _pcall(
        body, name=f"merge_bwd_branch{k}", grid=(n,),
        in_specs=[_rows(t, D_MODEL), _rows(t, BR_W), _rows(t, D_MODEL),
                  pl.BlockSpec((D_MODEL, D_MODEL), lambda i: (k, 0)), pl.BlockSpec((1, D_MODEL), lambda i: (0, k)),
                  pl.BlockSpec((1, D_MODEL, BR_W), lambda i: (k, 0, 0))],
        out_specs=[_rows(t, BR_W), _rows(t, D_MODEL), _acc((D_MODEL, D_MODEL)), _acc((1, D_MODEL)), _acc((D_MODEL, BR_W))],
        out_shape=[_sds((s, BR_W), F32), _sds((s, D_MODEL), BF16), _sds((D_MODEL, D_MODEL), BF16),
                   _sds((1, D_MODEL), F32), _sds((D_MODEL, BR_W), BF16)],
        scratch=[pltpu.VMEM((D_MODEL, D_MODEL), F32), pltpu.VMEM((D_MODEL, BR_W), F32)],
        vmem=VMEM_LIMIT)(hb, br, dm, wg, bg, wbr)


def _mixers_bwd(pm, dbsg, dbcv, dbpl, ln_g, ln_b, wm, wmt, sgb, conv_w, wp, pscale, t):
    s = pm.shape[0]
    n = s // t
    nb = t // HALO

    def body(pm_ref, before_ref, after_ref, dsg_ref, dcv_ref, dcva_ref, dpl_ref, dpla_ref, lng_ref, lnb_ref,
             wm_ref, wmt_ref, sgb_ref, cw_ref, wp_ref, ps_ref,
             dpm_ref, dlng_ref, dlnb_ref, dwm_ref, dsgb_ref, dcw_ref, dwp_ref, dps_ref):
        i = pl.program_id(0)
        first = i == 0
        before = jnp.where(i > 0, before_ref[...], 0.0)
        after = jnp.where(i < n - 1, after_ref[...], 0.0)

        u_raw, v_raw = pm_ref[:, 0:256], pm_ref[:, 256:512]
        lng, lnb = lng_ref[...], lnb_ref[...]
        u, gelu_u_vjp = jax.vjp(_gelu, u_raw)
        vn, norm_vjp = jax.vjp(lambda v_, g_, b_: _layernorm(_gelu(v_), g_, b_), v_raw, lng, lnb)
        vnb = vn.astype(BF16)
        dsg = dsg_ref[...]
        grp = _lane_group((SG_CHUNK, BR_W))
        tri = (lax.broadcasted_iota(jnp.int32, (SG_CHUNK, SG_CHUNK), 1)
               <= lax.broadcasted_iota(jnp.int32, (SG_CHUNK, SG_CHUNK), 0))
        du_parts, dvn_parts = [], []
        dsgb = jnp.zeros((SG_CHUNK, BR_W), F32)
        dwm = [jnp.zeros((SG_CHUNK, SG_CHUNK), F32) for _ in range(SG_GROUPS)]
        for c in range(t // SG_CHUNK):
            rows = slice(c * SG_CHUNK, (c + 1) * SG_CHUNK)
            mix = _sg_mix(wm_ref, vnb[rows], sgb_ref[...])
            du_parts.append(dsg[rows] * mix)
            ds = dsg[rows] * u[rows]
            dsgb = dsgb + ds
            dsb = [jnp.where(grp == g, ds, 0.0).astype(BF16) for g in range(SG_GROUPS)]
            for g in range(SG_GROUPS):
                dwm[g] = dwm[g] + _dot(dsb[g], vnb[rows], NT)
            dvn_parts.append(_group_select([_dot(wmt_ref[g], dsb[g]) for g in range(SG_GROUPS)]))
        (du_raw,) = gelu_u_vjp(jnp.concatenate(du_parts, axis=0))
        dv_raw, dlng, dlnb = norm_vjp(jnp.concatenate(dvn_parts, axis=0))
        dpm_ref[:, 0:256] = du_raw.astype(BF16)
        dpm_ref[:, 256:512] = dv_raw.astype(BF16)
        _accumulate(dlng_ref, dlng, first)
        _accumulate(dlnb_ref, dlnb, first)
        _accumulate(dsgb_ref, dsgb, first)
        for g in range(SG_GROUPS):
            _accumulate(dwm_ref.at[g], jnp.where(tri, dwm[g], 0.0), first)

        xin, bg, cg = pm_ref[:, 512:768], pm_ref[:, 768:1024], pm_ref[:, 1024:1280]
        z = cg * xin
        zext = jnp.concatenate([before[:, 1024:1280] * before[:, 512:768], z], axis=0)
        z1, z2 = _shift_down(zext, 1, t), _shift_down(zext, 2, t)
        w0, w1, w2 = cw_ref[0:1, :], cw_ref[1:2, :], cw_ref[2:3, :]
        dcv = dcv_ref[...]
        y = w0 * z2 + w1 * z1 + w2 * z
        dy = dcv * bg
        dyext = jnp.concatenate([dy, jnp.where(i < n - 1, dcva_ref[...], 0.0) * after[:, 768:1024]], axis=0)
        dz = w2 * dy + w1 * _shift_up(dyext, 1, t) + w0 * _shift_up(dyext, 2, t)
        dpm_ref[:, 512:768] = (dz * cg).astype(BF16)
        dpm_ref[:, 768:1024] = (dcv * y).astype(BF16)
        dpm_ref[:, 1024:1280] = (dz * xin).astype(BF16)
        dcw = jnp.concatenate([jnp.sum(dy * z2, axis=0, keepdims=True), jnp.sum(dy * z1, axis=0, keepdims=True),
                               jnp.sum(dy * z, axis=0, keepdims=True)], axis=0)
        _accumulate(dcw_ref, jnp.concatenate([dcw, jnp.zeros((8 - CONV_K, BR_W), F32)], axis=0), first)

        p = pm_ref[:, 1280:1536]
        tok = i * t + lax.broadcasted_iota(jnp.int32, (t, 1), 0)
        sums = _pool_windows(jnp.concatenate([before[:, 1280:1536], p], axis=0))
        pooled = _group_select([sw[HALO:HALO + t] / cnt - p for sw, cnt in zip(sums, _pool_counts(tok))]).astype(BF16)
        mixed = _dot(pooled, wp_ref[...])
        dpl = dpl_ref[...]
        ps = ps_ref[...]
        dmix = (dpl * ps).astype(BF16)
        _accumulate(dps_ref, jnp.sum(dpl * mixed, axis=0, keepdims=True), first)
        _accumulate(dwp_ref, _dot(pooled, dmix, TN), first)
        dmix_after = (jnp.where(i < n - 1, dpla_ref[...], 0.0) * ps).astype(BF16)
        dpo = _dot(dmix, wp_ref[...], NT)
        dpo_ext = jnp.concatenate([dpo, _dot(dmix_after, wp_ref[...], NT)], axis=0)
        tok_ext = i * t + lax.broadcasted_iota(jnp.int32, (t + HALO, 1), 0)
        dp_groups = []
        for g, (win, cnt) in enumerate(zip(POOL_WINDOWS, _pool_counts(tok_ext))):
            e = dpo_ext / cnt
            acc = e
            span = 1
            while span < win:
                acc = acc + pltpu.roll(acc, t + HALO - span, 0)
                span *= 2
            dp_groups.append(acc[0:t] - dpo)
        dpm_ref[:, 1280:1536] = _group_select(dp_groups).astype(BF16)

    row_blk = lambda w: pl.BlockSpec((t, w), lambda i: (i, 0))
    after_blk = lambda w: pl.BlockSpec((HALO, w), lambda i: (jnp.minimum((i + 1) * nb, n * nb - 1), 0))
    return _pcall(
        body, name="mixers_bwd", grid=(n,),
        in_specs=[row_blk(PM_W), _halo_before(t), after_blk(PM_W), row_blk(BR_W), row_blk(BR_W), after_blk(BR_W),
                  row_blk(BR_W), after_blk(BR_W), _whole(ln_g.shape), _whole(ln_b.shape), _whole(wm.shape),
                  _whole(wmt.shape), _whole(sgb.shape), _whole(conv_w.shape), _whole(wp.shape), _whole(pscale.shape)],
        out_specs=[row_blk(PM_W), _acc((1, BR_W)), _acc((1, BR_W)), _acc((SG_GROUPS, SG_CHUNK, SG_CHUNK)),
                   _acc((SG_CHUNK, BR_W)), _acc((8, BR_W)), _acc((BR_W, BR_W)), _acc((1, BR_W))],
        out_shape=[_sds((s, PM_W), BF16), _sds((1, BR_W), F32), _sds((1, BR_W), F32),
                   _sds((SG_GROUPS, SG_CHUNK, SG_CHUNK), F32), _sds((SG_CHUNK, BR_W), F32), _sds((8, BR_W), F32),
                   _sds((BR_W, BR_W), F32), _sds((1, BR_W), F32)],
        vmem=VMEM_LIMIT)(pm, pm, pm, dbsg, dbcv, dbcv, dbpl, dbpl, ln_g, ln_b, wm, wmt, sgb, conv_w, wp, pscale)


def _head_delta(o, do):
    rows = o.shape[0]
    prod = o.astype(F32) * do
    lane = lax.broadcasted_iota(jnp.int32, (rows, LANES), 1)
    out = []
    for h in range(N_HEADS):
        pair = prod[:, (h // 2) * LANES:(h // 2 + 1) * LANES]
        mine = (lane < D_V) if h % 2 == 0 else (lane >= D_V)
        out.append(jnp.broadcast_to(jnp.sum(jnp.where(mine, pair, 0.0), axis=1, keepdims=True), (rows, LANES)))
    return out


def _head_do(do, h):
    lane = lax.broadcasted_iota(jnp.int32, (do.shape[0], LANES), 1)
    mine = (lane < D_V) if h % 2 == 0 else (lane >= D_V)
    pair = do[:, (h // 2) * LANES:(h // 2 + 1) * LANES]
    return jnp.where(mine, pair, 0.0).astype(BF16)


def _flash_bwd(q, k, v, o, do, lse, t, riding):
    s = v.shape[0]
    n = s // t
    qi_tab, kj_tab = _causal_steps(n, key_major=True)
    n_steps = int(qi_tab.shape[0])
    whole = slice(0, t)

    def body(qi_ref, kj_ref, q_ref, k_ref, v_ref, o_ref, do_ref, lse_ref, x_ref, dq_ref, dk_ref, dv_ref, land_ref,
             dv_scr, send_sems, recv_sems, local_sem):
        step_id = pl.program_id(0)
        qi, kj = qi_ref[step_id], kj_ref[step_id]
        at_end = _ride_along(riding, step_id, n_steps, (x_ref, land_ref, send_sems, recv_sems, local_sem))

        @pl.when(step_id == 0)
        def _():
            dq_ref[...] = jnp.zeros(dq_ref.shape, F32)

        @pl.when(qi == kj)
        def _():
            dk_ref[...] = jnp.zeros(dk_ref.shape, F32)
            dv_scr[...] = jnp.zeros(dv_scr.shape, F32)

        def step(masked, qr, kr):
            dov = do_ref[qr, :]
            delta = _head_delta(o_ref[qr, :], dov)
            reps = (kr.stop - kr.start) // LANES
            q_rows = pl.ds(pl.multiple_of(qi * t + qr.start, t // 2), qr.stop - qr.start)
            for h in range(N_HEADS):
                qh, kh = q_ref[h, qr, :], k_ref[h, kr, :]
                p = jnp.exp(_causal_scores(qh, kh, masked) - jnp.tile(lse_ref[h, qr, :], (1, reps)))
                pair = slice((h // 2) * LANES, (h // 2 + 1) * LANES)
                dv_scr[h, kr, :] += _dot(p.astype(BF16), dov[:, pair].astype(BF16), TN)
                dp = _dot(_head_do(dov, h), v_ref[kr, pair], NT)
                ds = (p * (dp - jnp.tile(delta[h], (1, reps))) * SM_SCALE).astype(BF16)
                dk_ref[h, kr, :] += _dot(ds, qh, TN)
                dq_ref[h, q_rows, :] += _dot(ds, kh)

        @pl.when(qi > kj)
        def _():
            step(False, whole, whole)

        @pl.when(qi == kj)
        def _():
            for masked, qr, kr in _diagonal_parts(t):
                step(masked, qr, kr)

        @pl.when(qi == n - 1)
        def _():
            lane = lax.broadcasted_iota(jnp.int32, (t, LANES), 1)
            for pr in range(N_HEADS // 2):
                dv_ref[:, pr * LANES:(pr + 1) * LANES] = jnp.where(lane < D_V, dv_scr[2 * pr], dv_scr[2 * pr + 1]).astype(BF16)

        at_end()

    q_blk = pl.BlockSpec((N_HEADS, t, HEAD_PAD), lambda i, qi, kj: (0, qi[i], 0))
    k_blk = pl.BlockSpec((N_HEADS, t, HEAD_PAD), lambda i, qi, kj: (0, kj[i], 0))
    v_blk = pl.BlockSpec((t, BR_W), lambda i, qi, kj: (kj[i], 0))
    o_blk = pl.BlockSpec((t, BR_W), lambda i, qi, kj: (qi[i], 0))
    dq_sds = _sds((N_HEADS, s, HEAD_PAD), F32)
    return _pcall(
        body, name="flash_bwd_exchanging", grid=(n_steps,), prefetch=2,
        in_specs=[q_blk, k_blk, v_blk, o_blk, o_blk, q_blk, HBM_REF],
        out_specs=[_whole(dq_sds.shape), k_blk, v_blk, HBM_REF],
        out_shape=[dq_sds, dq_sds, _sds((s, BR_W), BF16), riding[2]],
        scratch=[pltpu.VMEM((N_HEADS, t, LANES), F32)] + list(EXCHANGE_SEMS),
        vmem=VMEM_LIMIT)(qi_tab, kj_tab, q, k, v, o, do, lse, riding[1])


def _attn_prep_bwd(pa, dq, dk, dv, cosf, sins, qn, kvn, wq, wqs, wk, wv, t):
    s = pa.shape[0]

    def body(pa_ref, dq_ref, dk_ref, dv_ref, cos_ref, sin_ref, qn_ref, kvn_ref, wq_ref, wqs_ref, wk_ref, wv_ref,
             dpa_ref, dwq_ref, dwqs_ref, dwk_ref, dwv_ref, dqn_ref, dkvn_ref):
        first = pl.program_id(0) == 0
        cosv, sinv = cos_ref[...], sin_ref[...]
        cq, ckv = pa_ref[:, 0:Q_RANK], pa_ref[:, Q_RANK:Q_RANK + KV_RANK]
        cqn = _rms(cq, qn_ref[...]).astype(BF16)
        ckvn = _rms(ckv, kvn_ref[...]).astype(BF16)
        dvv = dv_ref[...]
        _accumulate(dwv_ref, _dot(dvv, ckvn, TN), first)
        dk_rope = dk_ref[0]
        for h in range(1, N_HEADS):
            dk_rope = dk_rope + dk_ref[h]
        dqa = jnp.concatenate([(dq_ref[h] * cosv).astype(BF16) for h in range(N_HEADS)], axis=1)
        dqs = jnp.concatenate([(dq_ref[h] * sinv).astype(BF16) for h in range(N_HEADS)], axis=1)
        dkb = jnp.concatenate([dk_ref[h].astype(BF16) for h in range(N_HEADS)], axis=1)
        _accumulate(dwq_ref, _dot(dqa, cqn, TN), first)
        _accumulate(dwqs_ref, _dot(dqs, cqn, TN), first)
        _accumulate(dwk_ref, _dot(dkb, ckvn, TN), first)
        dcqn = _dot(dqa, wq_ref[...]) + _dot(dqs, wqs_ref[...])
        dckvn = _dot(dvv, wv_ref[...]) + _dot(dkb, wk_ref[...])
        dcq, dqn = _rms_bwd(cq, qn_ref[...], dcqn)
        dckv, dkvn = _rms_bwd(ckv, kvn_ref[...], dckvn)
        _accumulate(dqn_ref, dqn, first)
        _accumulate(dkvn_ref, dkvn, first)
        dpa_ref[:, 0:Q_RANK] = dcq.astype(BF16)
        dpa_ref[:, Q_RANK:Q_RANK + KV_RANK] = dckv.astype(BF16)
        dpa_ref[:, 384:512] = (dk_rope * cosv).astype(BF16)
        dpa_ref[:, 512:640] = (dk_rope * sinv).astype(BF16)

    head_blk = pl.BlockSpec((N_HEADS, t, HEAD_PAD), lambda i: (0, i, 0))
    return _pcall(
        body, name="attn_prep_bwd", grid=(s // t,),
        in_specs=[_rows(t, PA_W), head_blk, head_blk, _rows(t, BR_W), _rows(t, LANES), _rows(t, LANES),
                  _whole(qn.shape), _whole(kvn.shape), _whole(wq.shape), _whole(wqs.shape), _whole(wk.shape), _whole(wv.shape)],
        out_specs=[_rows(t, PA_W), _acc(wq.shape), _acc(wqs.shape), _acc(wk.shape), _acc(wv.shape),
                   _acc(qn.shape), _acc(kvn.shape)],
        out_shape=[_sds((s, PA_W), BF16), _sds(wq.shape, F32), _sds(wqs.shape, F32), _sds(wk.shape, F32),
                   _sds(wv.shape, F32), _sds(qn.shape, F32), _sds(kvn.shape, F32)],
        vmem=VMEM_LIMIT)(pa, dq, dk, dv, cosf, sins, qn, kvn, wq, wqs, wk, wv)


def _inproj_bwd(x, g1, dxmid, dpa, dpm, dpres, w_in, wg, t):
    s = x.shape[0]
    n = s // t

    def body(x_ref, g1_ref, dxm_ref, dpa_ref, dpm_ref, d0_ref, d1_ref, d2_ref, d3_ref, w_ref, wg_ref,
             dx_ref, dg1_ref, dwa_ref, dwm_ref, acc_a, acc_m):
        first, last = pl.program_id(0) == 0, pl.program_id(0) == n - 1
        xv, g1v = x_ref[...], g1_ref[...]
        hb = _rms(xv, g1v).astype(BF16)
        dpav, dpmv = dpa_ref[...], dpm_ref[...]
        dh = _dot(dpav, w_ref[:PA_W, :]) + _dot(dpmv, w_ref[PA_W:, :])
        for k, d_ref in enumerate((d0_ref, d1_ref, d2_ref, d3_ref)):
            dh = dh + _dot(d_ref[...], wg_ref[k * D_MODEL:(k + 1) * D_MODEL, :])
        dx, dg1 = _rms_bwd(xv, g1v, dh)
        dx_ref[...] = dxm_ref[...] + dx
        _accumulate(dg1_ref, dg1, first)
        _accumulate_out(acc_a, dwa_ref, _dot(dpav, hb, TN), first, last)
        _accumulate_out(acc_m, dwm_ref, _dot(dpmv, hb, TN), first, last)

    return _pcall(
        body, name="inproj_bwd", grid=(n,),
        in_specs=[_rows(t, D_MODEL), _whole(g1.shape), _rows(t, D_MODEL), _rows(t, PA_W), _rows(t, PM_W)] +
                 [_rows(t, D_MODEL)] * 4 + [_whole(w_in.shape), _whole(wg.shape)],
        out_specs=[_rows(t, D_MODEL), _acc((1, D_MODEL)), _acc((PA_W, D_MODEL)), _acc((PM_W, D_MODEL))],
        out_shape=[_sds((s, D_MODEL), F32), _sds((1, D_MODEL), F32), _sds((PA_W, D_MODEL), BF16), _sds((PM_W, D_MODEL), BF16)],
        scratch=[pltpu.VMEM((PA_W, D_MODEL), F32), pltpu.VMEM((PM_W, D_MODEL), F32)],
        vmem=VMEM_LIMIT)(x, g1, dxmid, dpa, dpm, *dpres, w_in, wg)


def _my_place():
    return lax.axis_index("x"), lax.axis_index("y"), lax.axis_index("c")


def _flip(place, k):
    x, y, c = place
    return (1 - x if k & 4 else x, 1 - y if k & 2 else y, 1 - c if k & 1 else c)


def _rank(place):
    x, y, c = place
    return 4 * x + 2 * y + c


EXCHANGE_SEMS = (pltpu.SemaphoreType.DMA((7,)), pltpu.SemaphoreType.DMA((7,)), pltpu.SemaphoreType.DMA)
HBM_REF = pl.BlockSpec(memory_space=pl.ANY)


def _gather_plan(x_ref, out_ref, send_sems, recv_sems, local_sem):
    me = _my_place()
    sibling = _flip(me, 1)
    chips = (4, 2, 6)

    def copy(k, src_place, to, src=None):
        slot = out_ref.at[_rank(src_place)]
        return pltpu.make_async_remote_copy(
            src_ref=slot if src is None else src, dst_ref=slot, send_sem=send_sems.at[k], recv_sem=recv_sems.at[k],
            device_id=to, device_id_type=MESH_ID)

    mine = pltpu.make_async_copy(x_ref, out_ref.at[_rank(me)], local_sem)
    first = [copy(0, me, sibling, src=x_ref)] + [copy(1 + j, me, _flip(me, kc), src=x_ref) for j, kc in enumerate(chips)]
    passed = [copy(4 + j, _flip(me, kc), sibling) for j, kc in enumerate(chips)]

    def start():
        mine.start()
        for cp in first:
            cp.start()

    def forward():
        for j, kc in enumerate(chips):
            copy(1 + j, _flip(me, kc), me).wait_recv()
            passed[j].start()

    def finish():
        copy(0, sibling, me).wait_recv()
        for j, kc in enumerate(chips):
            copy(4 + j, _flip(sibling, kc), me).wait_recv()
        for cp in first + passed:
            cp.wait_send()
        mine.wait()

    return start, forward, finish


def _exchange_plan(x_ref, out_ref, send_sems, recv_sems, local_sem):
    me = _my_place()
    my_rank = _rank(me)
    mine = pltpu.make_async_copy(x_ref.at[my_rank], out_ref.at[my_rank], local_sem)
    sends = [pltpu.make_async_remote_copy(
        src_ref=x_ref.at[_rank(_flip(me, k))], dst_ref=out_ref.at[my_rank], send_sem=send_sems.at[k - 1],
        recv_sem=recv_sems.at[k - 1], device_id=_flip(me, k), device_id_type=MESH_ID) for k in range(1, N_DEV)]

    def start():
        mine.start()
        for cp in sends:
            cp.start()

    def forward():
        pass

    def finish():
        for k in range(1, N_DEV):
            slot = out_ref.at[_rank(_flip(me, k))]
            pltpu.make_async_remote_copy(
                src_ref=slot, dst_ref=slot, send_sem=send_sems.at[k - 1], recv_sem=recv_sems.at[k - 1],
                device_id=_flip(me, k), device_id_type=MESH_ID).wait_recv()
        for cp in sends:
            cp.wait_send()
        mine.wait()

    return start, forward, finish


def _exchange_alone(plan, src, out_sds, name):
    def body(x_ref, out_ref, send_sems, recv_sems, local_sem):
        start, forward, finish = plan(x_ref, out_ref, send_sems, recv_sems, local_sem)
        start()
        forward()
        finish()

    return pl.pallas_call(body, name=name, out_shape=out_sds, in_specs=[HBM_REF], out_specs=HBM_REF,
                          scratch_shapes=list(EXCHANGE_SEMS))(src)


def _all_gather(shard, name):
    return _exchange_alone(_gather_plan, shard, _sds((N_DEV,) + shard.shape, shard.dtype), name)


def _all_to_all(blocks, name):
    return _exchange_alone(_exchange_plan, blocks, _sds(blocks.shape, blocks.dtype), name)


def _riding(plan, src, out_sds):
    return (plan, src, out_sds)


def _ride_along(riding, step_id, n_steps, refs):
    start, forward, finish = riding[0](*refs)

    @pl.when(step_id == 0)
    def _():
        start()

    @pl.when(step_id == (3 * n_steps) // 4)
    def _():
        forward()

    def at_end():
        @pl.when(step_id == n_steps - 1)
        def _():
            finish()

    return at_end


def _sum_slots(parts, name):
    _, r, c = parts.shape
    t = math.gcd(r, ROW_PAD)

    def body(p_ref, o_ref):
        acc = p_ref[0].astype(F32)
        for d in range(1, N_DEV):
            acc = acc + p_ref[d].astype(F32)
        o_ref[...] = acc

    return _pcall(
        body, name=name, grid=(r // t,),
        in_specs=[pl.BlockSpec((N_DEV, t, c), lambda i: (0, i, 0))], out_specs=_rows(t, c),
        out_shape=_sds((r, c), F32))(parts)


def _adamw(w, g, m, v, name):
    depth, r, c = w.shape
    t = r if r <= ROW_PAD else math.gcd(r, ROW_PAD)
    blk = pl.BlockSpec((1, t, c), lambda l, i: (l, i, 0))

    def body(w_ref, g_ref, m_ref, v_ref, d_ref, nm_ref, nv_ref):
        gv = g_ref[...]
        nm = ADAM_B1 * m_ref[...] + (1.0 - ADAM_B1) * gv
        nv = ADAM_B2 * v_ref[...] + (1.0 - ADAM_B2) * jnp.square(gv)
        m_hat = nm / (1.0 - ADAM_B1 ** ADAM_STEP)
        v_hat = nv / (1.0 - ADAM_B2 ** ADAM_STEP)
        d_ref[...] = -ADAM_LR * (m_hat / (jnp.sqrt(v_hat) + ADAM_EPS) + ADAM_WD * w_ref[...])
        nm_ref[...] = nm
        nv_ref[...] = nv

    return _pcall(
        body, name=name, grid=(depth, r // t),
        in_specs=[blk] * 4, out_specs=[blk] * 3, out_shape=[_sds(w.shape, F32)] * 3)(w, g, m, v)


BEFORE_ATTN = ("w_in", "w_uq", "w_ukv")
AFTER_ATTN = ("w_br_mla", "w_br_sg", "w_br_conv", "w_br_pool", "w_out", "w_ff1", "w_ff2", "conv_w")
SHARDED = BEFORE_ATTN + AFTER_ATTN
ROW_SHARDED = ("w_out", "w_ff2")
REPLICATED = ("norm_mix_pre", "gate_b", "q_norm", "kv_norm", "sg_ln_g", "sg_ln_b", "sg_w", "sg_b", "pool_w",
              "pool_scale", "norm_mix_post", "norm_ffn_pre", "norm_ffn_post")
ROW_PAD = 256
GRAD_ROW_PAD = 64
PART_ROWS = 16


def _pack_rows(arrays, dtype, multiple, lead=()):
    rows, offsets, at = [], [], 0
    zero_pad = ((0, 0),) * len(lead)
    for a in arrays:
        if a.ndim == len(lead) + 2 and a.shape[-1] == D_MODEL:
            n = a.shape[-2]
            n_pad = -(-n // PART_ROWS) * PART_ROWS
            rows.append(jnp.pad(a.astype(dtype), zero_pad + ((0, n_pad - n), (0, 0))))
        else:
            flat = a.reshape(lead + (-1,)).astype(dtype)
            n = -(-flat.shape[-1] // D_MODEL)
            n_pad = -(-n // PART_ROWS) * PART_ROWS
            flat = jnp.pad(flat, zero_pad + ((0, n_pad * D_MODEL - flat.shape[-1]),))
            rows.append(flat.reshape(lead + (n_pad, D_MODEL)))
        offsets.append((at, n))
        at += n_pad
    pad = -at % multiple
    if pad:
        rows.append(jnp.zeros(lead + (pad, D_MODEL), dtype))
    return jnp.concatenate(rows, axis=len(lead)), offsets


def _unpack_rows(buf, offsets, shapes):
    lead = buf.shape[:-2]
    out = []
    for (at, n), shape in zip(offsets, shapes):
        if tuple(shape) == (n, D_MODEL):
            out.append(buf[..., at:at + n, :])
            continue
        size = math.prod(shape)
        flat = buf[..., at:at + n, :].reshape(lead + (n * D_MODEL,))[..., :size]
        out.append(flat.reshape(lead + tuple(shape)))
    return out


def _pack_shards(p, depth, names):
    arrays = []
    for n in names:
        a = p[n][0] if n in ROW_SHARDED else jnp.swapaxes(p[n][0], -1, -2)
        arrays.append(lax.bitcast_convert_type(a, BF16) if n == "conv_w" else a.astype(BF16))
    packed, offsets = _pack_rows(arrays, BF16, GRAD_ROW_PAD, lead=(depth,))
    return packed, offsets, [a.shape[1:] for a in arrays]


def _unpack_layer_weights(names, gathered, offsets, shapes):
    full = {}
    for n, gth in zip(names, _unpack_rows(gathered, offsets, shapes)):
        if n == "conv_w":
            gth = lax.bitcast_convert_type(gth, F32)
        full[n] = gth if n == "w_in" else gth.reshape((-1,) + gth.shape[2:])
        if n == "conv_w":
            full[n] = full[n].T
    return full


def _rows_between(blocks, start, stop):
    r = blocks.shape[1]
    out = []
    for d in range(start // r, (stop - 1) // r + 1):
        out.append(blocks[d, max(start - d * r, 0):min(stop - d * r, r)])
    return out


def _regroup_rows(pieces, group):
    edges = [0]
    for a in pieces:
        edges.append(edges[-1] + a.shape[0])
    blocks = []
    for lo in range(0, edges[-1], group):
        hi = lo + group
        parts = [a[max(lo - e, 0):min(hi - e, a.shape[0])] for a, e in zip(pieces, edges) if e < hi and e + a.shape[0] > lo]
        blocks.append(jnp.concatenate(parts, axis=0))
    return jnp.stack(blocks)


def _owner_major(grad_sharded_dim_first):
    a = grad_sharded_dim_first
    return a.reshape((N_DEV, a.shape[0] // N_DEV) + a.shape[1:])


def _owner_block_shape(name, shard_shape):
    k, n = shard_shape[-2:]
    return (k, n) if name in ROW_SHARDED else (n, k)


def _natural_shard(name, blocks):
    return blocks if name in ROW_SHARDED else jnp.swapaxes(blocks, -1, -2)


def _swap_halves(a, axis=-1):
    lo, hi = jnp.split(a, 2, axis=axis)
    return jnp.concatenate([hi, lo], axis=axis)


def _pad_rows(a, top, total):
    return jnp.pad(a, ((0, 0),) * (a.ndim - 2) + ((top, total - top - a.shape[-2]), (0, 0)))


def _weights_before_attn(full, small, l):
    w_in = full["w_in"]
    (k_r,) = _rows_between(w_in, 384, 416)
    w_in_k = jnp.concatenate(
        _rows_between(w_in, 0, 384) + [_pad_rows(k_r, D_NOPE, HEAD_PAD), _pad_rows(_swap_halves(k_r, axis=0), D_NOPE, HEAD_PAD)]
        + _rows_between(w_in, 416, 1952), axis=0)
    w_uq = full["w_uq"].reshape(N_HEADS, D_NOPE + D_ROPE, Q_RANK)
    wq = _pad_rows(w_uq, 0, HEAD_PAD).reshape(N_HEADS * HEAD_PAD, Q_RANK)
    wqs = _pad_rows(_swap_halves(w_uq[:, D_NOPE:], axis=1), D_NOPE, HEAD_PAD).reshape(N_HEADS * HEAD_PAD, Q_RANK)
    w_ukv = full["w_ukv"].reshape(N_HEADS, D_NOPE + D_V, KV_RANK)
    wk = _pad_rows(w_ukv[:, :D_NOPE], 0, HEAD_PAD).reshape(N_HEADS * HEAD_PAD, KV_RANK)
    wv = w_ukv[:, D_NOPE:].reshape(N_HEADS * D_V, KV_RANK)
    tri = jnp.tril(jnp.ones((SG_CHUNK, SG_CHUNK), bool))
    wm = jnp.where(tri, small["sg_w"][l], 0.0)
    pool_w = small["pool_w"][l]
    wp = jnp.zeros((BR_W, BR_W), F32)
    for g in range(len(POOL_WINDOWS)):
        wp = wp.at[g * POOL_GROUP:(g + 1) * POOL_GROUP, g * POOL_GROUP:(g + 1) * POOL_GROUP].set(pool_w[g])
    vec = lambda name: small[name][l][None, :]
    return dict(
        w_in=w_in_k, wg=jnp.concatenate(_rows_between(w_in, 1952, N_IN), axis=0), bg=vec("gate_b"), g1=vec("norm_mix_pre"), qn=vec("q_norm"), kvn=vec("kv_norm"),
        wq=wq, wqs=wqs, wk=wk, wv=wv,
        ln_g=vec("sg_ln_g"), ln_b=vec("sg_ln_b"), wm=wm.astype(BF16), wmt=jnp.swapaxes(wm, 1, 2).astype(BF16),
        sgb=jnp.repeat(small["sg_b"][l].T, POOL_GROUP, axis=1), wp=wp.astype(BF16), pscale=vec("pool_scale"),
        g2=vec("norm_mix_post"), g3=vec("norm_ffn_pre"), g4=vec("norm_ffn_post"))


def _weights_after_attn(full):
    return dict(
        conv_w=full["conv_w"], wbr=jnp.stack([full["w_br_mla"], full["w_br_sg"], full["w_br_conv"], full["w_br_pool"]]),
        wout=full["w_out"], w1=full["w_ff1"], w2=full["w_ff2"])


def _grads_after_attn(g):
    own = _owner_major
    return dict(
        w_br_mla=own(g["dwbr"][0]), w_br_sg=own(g["dwbr"][1]), w_br_conv=own(g["dwbr"][2]), w_br_pool=own(g["dwbr"][3]),
        w_out=own(g["dwout"]), w_ff1=own(g["dw1"]), w_ff2=own(g["dw2"]), conv_w=own(g["dcw"][:CONV_K].T))


def _grads_before_attn(g):
    dwa = g["dwa"]
    rope_rows = slice(D_NOPE, D_NOPE + D_ROPE)
    d_kr = dwa[384:512][rope_rows].astype(F32) + _swap_halves(dwa[512:640][rope_rows], axis=0).astype(F32)
    w_in = _regroup_rows([dwa[0:384], d_kr.astype(BF16), g["dwm_in"]] + list(g["dwg"]), N_IN // N_DEV)
    dwq, dwqs = (g[n].reshape(N_HEADS, HEAD_PAD, Q_RANK) for n in ("dwq", "dwqs"))
    d_rope = dwq[:, rope_rows] + _swap_halves(dwqs[:, rope_rows], axis=1)
    w_uq = jnp.concatenate([dwq[:, :D_NOPE], d_rope], axis=1).reshape(N_HEADS * (D_NOPE + D_ROPE), Q_RANK)
    dwk = g["dwk"].reshape(N_HEADS, HEAD_PAD, KV_RANK)
    w_ukv = jnp.concatenate([dwk[:, :D_NOPE], g["dwv"].reshape(N_HEADS, D_V, KV_RANK)], axis=1)
    w_ukv = w_ukv.reshape(N_HEADS * (D_NOPE + D_V), KV_RANK)
    pool_w = jnp.stack([g["dwp"][i * POOL_GROUP:(i + 1) * POOL_GROUP, i * POOL_GROUP:(i + 1) * POOL_GROUP]
                        for i in range(len(POOL_WINDOWS))])
    sg_b = g["dsgb"].reshape(SG_CHUNK, SG_GROUPS, POOL_GROUP).sum(axis=-1).T
    own = _owner_major
    return dict(
        w_in=w_in, w_uq=own(w_uq), w_ukv=own(w_ukv),
        norm_mix_pre=g["dg1"][0], gate_b=jnp.concatenate([b[0] for b in g["dbg"]]), q_norm=g["dqn"][0], kv_norm=g["dkvn"][0],
        sg_ln_g=g["dlng"][0], sg_ln_b=g["dlnb"][0], sg_w=g["dwm"], sg_b=sg_b, pool_w=pool_w, pool_scale=g["dps"][0],
        norm_mix_post=g["dg2"][0], norm_ffn_pre=g["dg3"][0], norm_ffn_post=g["dg4"][0])


def _rope_tables(positions):
    inv_freq = ROPE_BASE ** (-jnp.arange(0, D_ROPE, 2, dtype=F32) / D_ROPE)
    ang = positions.astype(F32)[:, None] * inv_freq
    cos, sin = jnp.cos(ang), jnp.sin(ang)
    s = positions.shape[0]
    cosf = jnp.concatenate([jnp.ones((s, D_NOPE), F32), cos, cos, jnp.zeros((s, HEAD_PAD - D_NOPE - D_ROPE), F32)], axis=1)
    sins = jnp.concatenate([jnp.zeros((s, D_NOPE), F32), -sin, sin, jnp.zeros((s, HEAD_PAD - D_NOPE - D_ROPE), F32)], axis=1)
    return cosf, sins


def _layer_fwd(x, w, weights_after_attn, cosf, sins, tiles, riding):
    t, ta = tiles["tok"], tiles["attn"]
    hb, pa, pm = _inproj_fwd(x, w["g1"], w["w_in"], tiles["wgrad"])
    q, k, v = _attn_prep_fwd(pa, cosf, sins, w["qn"], w["kvn"], w["wq"], w["wqs"], w["wk"], w["wv"], tiles["wgrad"])
    o, lse, landed = _flash_fwd(q, k, v, ta, riding)
    w = {**w, **weights_after_attn(landed)}
    bsg, bcv, bpl = _mixers_fwd(pm, w["ln_g"], w["ln_b"], w["wm"], w["sgb"], w["conv_w"], w["wp"], w["pscale"], tiles["wgrad"])
    x_mid, merged, tt = _merge_fwd(x, hb, (o, bsg, bcv, bpl), w["wg"], w["bg"], w["wbr"], w["wout"], w["g2"], t)
    x_out, f = _ffn_fwd(x_mid, w["g3"], w["w1"], w["w2"], w["g4"], t)
    saved = dict(x=x, hb=hb, pa=pa, pm=pm, q=q, k=k, v=v, o=o, lse=lse, bsg=bsg, bcv=bcv, bpl=bpl, x_mid=x_mid,
                 merged=merged, tt=tt, f=f)
    return x_out, saved, w, landed


def _layer_bwd(dxo, a, w, cosf, sins, tiles, pending):
    t, ta, tb, tw = tiles["tok"], tiles["attn"], tiles["inproj_bwd"], tiles["wgrad"]
    g = {}
    h2, df, g["dg4"] = _ffn_bwd_norms(a["x_mid"], a["f"], dxo, w["g3"], w["g4"], tw)
    da, g["dw1"], g["dw2"] = _ffn_bwd_weights(h2, df, w["w1"], w["w2"], tw)
    dxmid, g["dg3"] = _ffn_bwd_input(da, w["w1"], a["x_mid"], dxo, w["g3"], t)
    dm, g["dwout"], g["dg2"] = _merge_bwd_out(a["tt"], dxmid, w["g2"], w["wout"], a["merged"], tw)
    dbrs, dpres, g["dwg"], g["dbg"], g["dwbr"] = [], [], [], [], []
    for k, br in enumerate((a["o"], a["bsg"], a["bcv"], a["bpl"])):
        dbr, dpre, dwg, dbg, dwbr = _merge_bwd_branch(k, a["hb"], br, dm, w["wg"], w["bg"], w["wbr"], tw)
        dbrs.append(dbr); dpres.append(dpre); g["dwg"].append(dwg); g["dbg"].append(dbg); g["dwbr"].append(dwbr)
    dpm, g["dlng"], g["dlnb"], g["dwm"], g["dsgb"], g["dcw"], g["dwp"], g["dps"] = _mixers_bwd(
        a["pm"], dbrs[1], dbrs[2], dbrs[3], w["ln_g"], w["ln_b"], w["wm"], w["wmt"], w["sgb"], w["conv_w"], w["wp"], w["pscale"], tw)
    after = _grads_after_attn(g)
    send, after_offsets = _pack_rows([after[n] for n in AFTER_ATTN], BF16, GRAD_ROW_PAD, lead=(N_DEV,))
    after_rows = send.shape[1]
    if pending is not None:
        send = jnp.concatenate([send, pending], axis=1)
    riding = _riding(_exchange_plan, send, _sds(send.shape, send.dtype))
    dq, dk, dv, landed = _flash_bwd(a["q"], a["k"], a["v"], a["o"], dbrs[0], a["lse"], ta, riding)
    dpa, g["dwq"], g["dwqs"], g["dwk"], g["dwv"], g["dqn"], g["dkvn"] = _attn_prep_bwd(
        a["pa"], dq, dk, dv, cosf, sins, w["qn"], w["kvn"], w["wq"], w["wqs"], w["wk"], w["wv"], tw)
    dx, g["dg1"], g["dwa"], g["dwm_in"] = _inproj_bwd(a["x"], w["g1"], dxmid, dpa, dpm, dpres, w["w_in"], w["wg"], tb)
    return dx, _grads_before_attn(g), landed, (after_rows, after_offsets)


def _step(p, x, positions, loss_target):
    s = x.shape[0]
    depth = p["w_in"][0].shape[0]
    tiles = dict(tok=min(TOK_TILE, s), attn=min(ATTN_TILE, s), inproj_bwd=min(INPROJ_BWD_TILE, s),
                 wgrad=min(WGRAD_TILE, s))

    small = {n: p[n][0] for n in REPLICATED}
    cosf, sins = _rope_tables(positions)

    pack_b, off_b, shapes_b = _pack_shards(p, depth, BEFORE_ATTN)
    pack_a, off_a, shapes_a = _pack_shards(p, depth, AFTER_ATTN)
    rows_a = pack_a.shape[1]
    before = _all_gather(pack_b[0], "weight_all_gather")
    weights, acts = [], []
    h = x
    for l in range(depth):
        w = _weights_before_attn(_unpack_layer_weights(BEFORE_ATTN, before, off_b, shapes_b), small, l)
        src = pack_a[l] if l + 1 == depth else jnp.concatenate([pack_a[l], pack_b[l + 1]], axis=0)
        riding = _riding(_gather_plan, src, _sds((N_DEV,) + src.shape, src.dtype))

        def after(landed):
            return _weights_after_attn(_unpack_layer_weights(AFTER_ATTN, landed[:, :rows_a], off_a, shapes_a))

        h, saved, w, landed = _layer_fwd(h, w, after, cosf, sins, tiles, riding)
        weights.append(w)
        acts.append(saved)
        before = landed[:, rows_a:] if l + 1 < depth else None
    dh, loss_blk = _loss_head(h, loss_target, tiles["tok"])
    loss = lax.psum(loss_blk[0, 0], ("x", "y", "c"))

    grads_b = [None] * depth
    sum_a, sum_b = [None] * depth, [None] * depth
    pending = None
    for l in reversed(range(depth)):
        dh, grads_b[l], landed, (rows_ga, off_ga) = _layer_bwd(dh, acts[l], weights[l], cosf, sins, tiles, pending)
        summed = _sum_slots(landed, "grad_shard_sum")
        sum_a[l] = summed[:rows_ga]
        if pending is not None:
            sum_b[l + 1] = summed[rows_ga:]
        pending, off_gb = _pack_rows([grads_b[l][n] for n in BEFORE_ATTN], BF16, GRAD_ROW_PAD, lead=(N_DEV,))
    sum_b[0] = _sum_slots(_all_to_all(pending, "grad_all_to_all"), "grad_shard_sum")
    g_shard = {}
    for names, sums, offsets in ((AFTER_ATTN, sum_a, off_ga), (BEFORE_ATTN, sum_b, off_gb)):
        block_shapes = [_owner_block_shape(n, p[n][0].shape) for n in names]
        for n, b in zip(names, _unpack_rows(jnp.stack(sums), offsets, block_shapes)):
            g_shard[n] = _natural_shard(n, b)
    small_grads = [jnp.stack([grads_b[l][n] for l in range(depth)]) for n in REPLICATED]
    small_rows, small_offsets = _pack_rows(small_grads, F32, ROW_PAD)
    g_small_rows = _sum_slots(_all_gather(small_rows, "replicated_grad_all_gather"), "grad_replicated_sum")

    grad, delta, new_m, new_v = {}, {}, {}, {}
    for n in SHARDED:
        w, m, v = p[n]
        grad[n] = g_shard[n]
        delta[n], new_m[n], new_v[n] = _adamw(w, grad[n], m, v, f"adamw_{n}")
    packs = [_pack_rows([p[n][i] for n in REPLICATED], F32, ROW_PAD)[0][None] for i in range(3)]
    d, nm, nv = (a[0] for a in _adamw(packs[0], g_small_rows[None], packs[1], packs[2], "adamw_replicated"))
    shapes = [p[n][0].shape for n in REPLICATED]
    for n, gg, dd, mm, vv in zip(REPLICATED, _unpack_rows(g_small_rows, small_offsets, shapes), _unpack_rows(d, small_offsets, shapes),
                                 _unpack_rows(nm, small_offsets, shapes), _unpack_rows(nv, small_offsets, shapes)):
        grad[n], delta[n], new_m[n], new_v[n] = gg, dd, mm, vv
    return loss, dh, grad, delta, new_m, new_v


WEIGHT_ORDER = ("norm_mix_pre", "w_in", "gate_b", "q_norm", "w_uq", "kv_norm", "w_ukv", "w_br_mla", "sg_ln_g", "sg_ln_b",
                "sg_w", "sg_b", "w_br_sg", "conv_w", "w_br_conv", "pool_w", "pool_scale", "w_br_pool", "w_out",
                "norm_mix_post", "norm_ffn_pre", "w_ff1", "w_ff2", "norm_ffn_post")


def kernel(x, positions, norm_mix_pre, w_in, gate_b, q_norm, w_uq, kv_norm, w_ukv, w_br_mla, sg_ln_g, sg_ln_b, sg_w, sg_b, w_br_sg, conv_w, w_br_conv, pool_w, pool_scale, w_br_pool, w_out, norm_mix_post, norm_ffn_pre, w_ff1, w_ff2, norm_ffn_post, loss_target, m_norm_mix_pre, m_w_in, m_gate_b, m_q_norm, m_w_uq, m_kv_norm, m_w_ukv, m_w_br_mla, m_sg_ln_g, m_sg_ln_b, m_sg_w, m_sg_b, m_w_br_sg, m_conv_w, m_w_br_conv, m_pool_w, m_pool_scale, m_w_br_pool, m_w_out, m_norm_mix_post, m_norm_ffn_pre, m_w_ff1, m_w_ff2, m_norm_ffn_post, v_norm_mix_pre, v_w_in, v_gate_b, v_q_norm, v_w_uq, v_kv_norm, v_w_ukv, v_w_br_mla, v_sg_ln_g, v_sg_ln_b, v_sg_w, v_sg_b, v_w_br_sg, v_conv_w, v_w_br_conv, v_pool_w, v_pool_scale, v_w_br_pool, v_w_out, v_norm_mix_post, v_norm_ffn_pre, v_w_ff1, v_w_ff2, v_norm_ffn_post):
    ws = (norm_mix_pre, w_in, gate_b, q_norm, w_uq, kv_norm, w_ukv, w_br_mla, sg_ln_g, sg_ln_b, sg_w, sg_b, w_br_sg, conv_w,
          w_br_conv, pool_w, pool_scale, w_br_pool, w_out, norm_mix_post, norm_ffn_pre, w_ff1, w_ff2, norm_ffn_post)
    ms = (m_norm_mix_pre, m_w_in, m_gate_b, m_q_norm, m_w_uq, m_kv_norm, m_w_ukv, m_w_br_mla, m_sg_ln_g, m_sg_ln_b, m_sg_w,
          m_sg_b, m_w_br_sg, m_conv_w, m_w_br_conv, m_pool_w, m_pool_scale, m_w_br_pool, m_w_out, m_norm_mix_post,
          m_norm_ffn_pre, m_w_ff1, m_w_ff2, m_norm_ffn_post)
    vs = (v_norm_mix_pre, v_w_in, v_gate_b, v_q_norm, v_w_uq, v_kv_norm, v_w_ukv, v_w_br_mla, v_sg_ln_g, v_sg_ln_b, v_sg_w,
          v_sg_b, v_w_br_sg, v_conv_w, v_w_br_conv, v_pool_w, v_pool_scale, v_w_br_pool, v_w_out, v_norm_mix_post,
          v_norm_ffn_pre, v_w_ff1, v_w_ff2, v_norm_ffn_post)
    p = {n: (w, m, v) for n, w, m, v in zip(WEIGHT_ORDER, ws, ms, vs)}
    loss, grad_x, grad, delta, new_m, new_v = _step(p, x[0], positions[0], loss_target[0])
    return (loss, grad_x[None], *[grad[n] for n in WEIGHT_ORDER], *[delta[n] for n in WEIGHT_ORDER],
            *[new_m[n] for n in WEIGHT_ORDER], *[new_v[n] for n in WEIGHT_ORDER])
```

```python
import functools
import math

import jax
import jax.numpy as jnp
from jax import lax
from jax.experimental import pallas as pl
from jax.experimental.pallas import tpu as pltpu

F32 = jnp.float32
BF16 = jnp.bfloat16

D_MODEL = 1024
N_HEADS = 4
D_NOPE = 64
D_ROPE = 32
D_V = 64
Q_RANK = 256
KV_RANK = 128
BR_W = 256
SG_CHUNK = 128
SG_GROUPS = 4
POOL_WINDOWS = (2, 4, 8, 16)
POOL_GROUP = 64
CONV_K = 3
D_FF = 4096
N_BRANCH = 4
N_IN = 6048
EPS = 1e-6
ROPE_BASE = 10000.0
ADAM_LR = 0.001
ADAM_B1 = 0.9
ADAM_B2 = 0.999
ADAM_EPS = 1e-08
ADAM_WD = 0.01
ADAM_STEP = 10

N_DEV = 8
LANES = 128
HEAD_PAD = 128
HALO = 16
PA_W = 640
PM_W = 6 * BR_W
VMEM_LIMIT = 56 * 1024 * 1024
SM_SCALE = (D_NOPE + D_ROPE) ** -0.5
TOK_TILE = 512
ATTN_TILE = 1024
WGRAD_TILE = 1024
INPROJ_BWD_TILE = 512

NN = (((1,), (0,)), ((), ()))
NT = (((1,), (1,)), ((), ()))
TN = (((0,), (0,)), ((), ()))
MESH_ID = pl.DeviceIdType.MESH


def _dot(a, b, dims=NN):
    return lax.dot_general(a, b, dims, preferred_element_type=F32)


def _pcall(body, *, name, grid, in_specs, out_specs, out_shape, scratch=(), vmem=None, prefetch=0):
    params = pltpu.CompilerParams(vmem_limit_bytes=vmem)
    if prefetch:
        spec = pltpu.PrefetchScalarGridSpec(num_scalar_prefetch=prefetch, grid=grid, in_specs=in_specs,
                                            out_specs=out_specs, scratch_shapes=scratch)
        return pl.pallas_call(body, name=name, grid_spec=spec, out_shape=out_shape, compiler_params=params)
    return pl.pallas_call(
        body, name=name, grid=grid, in_specs=in_specs, out_specs=out_specs, out_shape=out_shape,
        scratch_shapes=scratch, compiler_params=params)


def _rows(t, width):
    return pl.BlockSpec((t, width), lambda i: (i, 0))


def _whole(shape):
    nd = len(shape)
    return pl.BlockSpec(tuple(shape), lambda *_: (0,) * nd, pipeline_mode=pl.Buffered(1))


def _acc(shape):
    nd = len(shape)
    return pl.BlockSpec(tuple(shape), lambda *_: (0,) * nd)


def _sds(shape, dtype):
    return jax.ShapeDtypeStruct(tuple(shape), dtype)


def _rms(x, g):
    return x * lax.rsqrt(jnp.mean(x * x, axis=-1, keepdims=True) + EPS) * g


def _rms_bwd(x, g, dy):
    r = lax.rsqrt(jnp.mean(x * x, axis=-1, keepdims=True) + EPS)
    xh = x * r
    dg = jnp.sum(dy * xh, axis=0, keepdims=True)
    dxh = dy * g
    dx = r * (dxh - xh * jnp.mean(dxh * xh, axis=-1, keepdims=True))
    return dx, dg


def _sigmoid(x):
    return 1.0 / (1.0 + jnp.exp(-x))


def _gelu(x):
    return jax.nn.gelu(x, approximate=True)


def _accumulate(ref, val, first):
    @pl.when(first)
    def _():
        ref[...] = val

    @pl.when(jnp.logical_not(first))
    def _():
        ref[...] += val


def _accumulate_out(acc_ref, out_ref, val, first, last):
    _accumulate(acc_ref, val, first)

    @pl.when(last)
    def _():
        out_ref[...] = acc_ref[...].astype(out_ref.dtype)


def _inproj_fwd(x, g1, w_in, t):
    s = x.shape[0]

    def body(x_ref, g_ref, w_ref, hb_ref, pa_ref, pm_ref):
        hb = _rms(x_ref[...], g_ref[...]).astype(BF16)
        hb_ref[...] = hb
        pa_ref[...] = _dot(hb, w_ref[:PA_W, :], NT)
        pm_ref[...] = _dot(hb, w_ref[PA_W:, :], NT)

    return _pcall(
        body, name="inproj_fwd", grid=(s // t,),
        in_specs=[_rows(t, D_MODEL), _whole((1, D_MODEL)), _whole(w_in.shape)],
        out_specs=[_rows(t, D_MODEL), _rows(t, PA_W), _rows(t, PM_W)],
        out_shape=[_sds((s, D_MODEL), BF16), _sds((s, PA_W), F32), _sds((s, PM_W), F32)],
        vmem=VMEM_LIMIT)(x, g1, w_in)


def _attn_prep_fwd(pa, cosf, sins, qn, kvn, wq, wqs, wk, wv, t):
    s = pa.shape[0]

    def body(pa_ref, cos_ref, sin_ref, qn_ref, kvn_ref, wq_ref, wqs_ref, wk_ref, wv_ref, q_ref, k_ref, v_ref):
        cosv, sinv = cos_ref[...], sin_ref[...]
        cqn = _rms(pa_ref[:, 0:Q_RANK], qn_ref[...]).astype(BF16)
        ckvn = _rms(pa_ref[:, Q_RANK:Q_RANK + KV_RANK], kvn_ref[...]).astype(BF16)
        k_rope = pa_ref[:, 384:512] * cosv + pa_ref[:, 512:640] * sinv
        q_all, qs_all = _dot(cqn, wq_ref[...], NT), _dot(cqn, wqs_ref[...], NT)
        k_all = _dot(ckvn, wk_ref[...], NT)
        for h in range(N_HEADS):
            lanes = slice(h * HEAD_PAD, (h + 1) * HEAD_PAD)
            q_ref[h] = ((q_all[:, lanes] * cosv + qs_all[:, lanes] * sinv) * SM_SCALE).astype(BF16)
            k_ref[h] = (k_all[:, lanes] + k_rope).astype(BF16)
        v_ref[...] = _dot(ckvn, wv_ref[...], NT).astype(BF16)

    head_blk = pl.BlockSpec((N_HEADS, t, HEAD_PAD), lambda i: (0, i, 0))
    return _pcall(
        body, name="attn_prep_fwd", grid=(s // t,),
        in_specs=[_rows(t, PA_W), _rows(t, LANES), _rows(t, LANES), _whole(qn.shape), _whole(kvn.shape),
                  _whole(wq.shape), _whole(wqs.shape), _whole(wk.shape), _whole(wv.shape)],
        out_specs=[head_blk, head_blk, _rows(t, BR_W)],
        out_shape=[_sds((N_HEADS, s, HEAD_PAD), BF16), _sds((N_HEADS, s, HEAD_PAD), BF16), _sds((s, BR_W), BF16)],
        vmem=VMEM_LIMIT)(pa, cosf, sins, qn, kvn, wq, wqs, wk, wv)


def _causal_scores(q, k, masked):
    sc = _dot(q, k, NT)
    if masked:
        row = lax.broadcasted_iota(jnp.int32, sc.shape, 0)
        col = lax.broadcasted_iota(jnp.int32, sc.shape, 1)
        sc = jnp.where(col <= row, sc, -jnp.inf)
    return sc


def _diagonal_parts(t):
    half = t // 2
    if half % LANES:
        return [(True, slice(0, t), slice(0, t))]
    lower, upper = slice(0, half), slice(half, t)
    return [(True, lower, lower), (False, upper, lower), (True, upper, upper)]


def _causal_steps(n, key_major):
    if key_major:
        pairs = [(qi, kj) for kj in range(n) for qi in range(kj, n)]
    else:
        pairs = [(qi, kj) for qi in range(n) for kj in range(qi + 1)]
    return (jnp.asarray([p[0] for p in pairs], jnp.int32), jnp.asarray([p[1] for p in pairs], jnp.int32))


def _flash_fwd(q, k, v, t, riding):
    s = v.shape[0]
    n = s // t
    qi_tab, kj_tab = _causal_steps(n, key_major=False)
    n_steps = int(qi_tab.shape[0])
    whole = slice(0, t)

    def body(qi_ref, kj_ref, q_ref, k_ref, v_ref, x_ref, o_ref, lse_ref, land_ref, m_scr, l_scr, acc_scr,
             send_sems, recv_sems, local_sem):
        step_id = pl.program_id(0)
        qi, kj = qi_ref[step_id], kj_ref[step_id]
        at_end = _ride_along(riding, step_id, n_steps, (x_ref, land_ref, send_sems, recv_sems, local_sem))

        @pl.when(kj == 0)
        def _():
            m_scr[...] = jnp.full(m_scr.shape, -jnp.inf, F32)
            l_scr[...] = jnp.zeros(l_scr.shape, F32)
            acc_scr[...] = jnp.zeros(acc_scr.shape, F32)

        def step(masked, qr, kr):
            for h in range(N_HEADS):
                sc = _causal_scores(q_ref[h, qr, :], k_ref[h, kr, :], masked)
                m_prev = m_scr[h, qr, :]
                m_next = jnp.maximum(m_prev, jnp.max(sc, axis=1, keepdims=True))
                alpha = jnp.exp(m_prev - m_next)
                p = jnp.exp(sc - jnp.tile(m_next, (1, sc.shape[1] // LANES)))
                l_scr[h, qr, :] = alpha * l_scr[h, qr, :] + jnp.sum(p, axis=1, keepdims=True)
                m_scr[h, qr, :] = m_next
                pair = slice((h // 2) * LANES, (h // 2 + 1) * LANES)
                acc_scr[h, qr, :] = acc_scr[h, qr, :] * alpha + _dot(p.astype(BF16), v_ref[kr, pair])

        @pl.when(kj < qi)
        def _():
            step(False, whole, whole)

        @pl.when(kj == qi)
        def _():
            for masked, qr, kr in _diagonal_parts(t):
                step(masked, qr, kr)
            lane = lax.broadcasted_iota(jnp.int32, (t, LANES), 1)
            for pr in range(N_HEADS // 2):
                o0 = acc_scr[2 * pr] / l_scr[2 * pr]
                o1 = acc_scr[2 * pr + 1] / l_scr[2 * pr + 1]
                o_ref[:, pr * LANES:(pr + 1) * LANES] = jnp.where(lane < D_V, o0, o1).astype(BF16)
            for h in range(N_HEADS):
                lse_ref[h] = m_scr[h] + jnp.log(l_scr[h])

        at_end()

    q_blk = pl.BlockSpec((N_HEADS, t, HEAD_PAD), lambda i, qi, kj: (0, qi[i], 0))
    k_blk = pl.BlockSpec((N_HEADS, t, HEAD_PAD), lambda i, qi, kj: (0, kj[i], 0))
    v_blk = pl.BlockSpec((t, BR_W), lambda i, qi, kj: (kj[i], 0))
    return _pcall(
        body, name="flash_fwd_gathering", grid=(n_steps,), prefetch=2,
        in_specs=[q_blk, k_blk, v_blk, HBM_REF],
        out_specs=[pl.BlockSpec((t, BR_W), lambda i, qi, kj: (qi[i], 0)), q_blk, HBM_REF],
        out_shape=[_sds((s, BR_W), BF16), _sds((N_HEADS, s, LANES), F32), riding[2]],
        scratch=[pltpu.VMEM((N_HEADS, t, LANES), F32)] * 3 + list(EXCHANGE_SEMS),
        vmem=VMEM_LIMIT)(qi_tab, kj_tab, q, k, v, riding[1])


def _shift_down(ext, k, t):
    return pltpu.roll(ext, k, 0)[HALO:HALO + t]


def _shift_up(ext, k, t):
    return pltpu.roll(ext, t + HALO - k, 0)[0:t]


def _lane_group(shape):
    return lax.shift_right_logical(lax.broadcasted_iota(jnp.int32, shape, 1), 6)


def _group_select(vals):
    grp = _lane_group(vals[0].shape)
    out = vals[0]
    for g in range(1, len(vals)):
        out = jnp.where(grp == g, vals[g], out)
    return out


def _layernorm(x, g, b):
    mu = jnp.mean(x, axis=-1, keepdims=True)
    xc = x - mu
    return xc * lax.rsqrt(jnp.mean(xc * xc, axis=-1, keepdims=True) + EPS) * g + b


def _sg_mix(wm_ref, vnb, bias):
    return _group_select([_dot(wm_ref[g], vnb) for g in range(SG_GROUPS)]) + bias


def _pool_windows(ext):
    s2 = ext + pltpu.roll(ext, 1, 0)
    s4 = s2 + pltpu.roll(s2, 2, 0)
    s8 = s4 + pltpu.roll(s4, 4, 0)
    s16 = s8 + pltpu.roll(s8, 8, 0)
    return [s2, s4, s8, s16]


def _pool_counts(tok):
    return [jnp.minimum(tok + 1, w).astype(F32) for w in POOL_WINDOWS]


def _halo_before(t):
    return pl.BlockSpec((HALO, PM_W), lambda i: (jnp.maximum(i * (t // HALO) - 1, 0), 0))


def _mixers_fwd(pm, ln_g, ln_b, wm, sgb, conv_w, wp, pscale, t):
    s = pm.shape[0]

    def body(pm_ref, halo_ref, lng_ref, lnb_ref, wm_ref, sgb_ref, cw_ref, wp_ref, ps_ref, bsg_ref, bcv_ref, bpl_ref):
        i = pl.program_id(0)
        halo = jnp.where(i > 0, halo_ref[...], 0.0)
        u = _gelu(pm_ref[:, 0:256])
        vnb = _layernorm(_gelu(pm_ref[:, 256:512]), lng_ref[...], lnb_ref[...]).astype(BF16)
        for c in range(t // SG_CHUNK):
            rows = slice(c * SG_CHUNK, (c + 1) * SG_CHUNK)
            bsg_ref[rows, :] = (u[rows] * _sg_mix(wm_ref, vnb[rows], sgb_ref[...])).astype(BF16)
        z = pm_ref[:, 1024:1280] * pm_ref[:, 512:768]
        zext = jnp.concatenate([halo[:, 1024:1280] * halo[:, 512:768], z], axis=0)
        y = cw_ref[0:1, :] * _shift_down(zext, 2, t) + cw_ref[1:2, :] * _shift_down(zext, 1, t) + cw_ref[2:3, :] * z
        bcv_ref[...] = (pm_ref[:, 768:1024] * y).astype(BF16)
        p = pm_ref[:, 1280:1536]
        sums = _pool_windows(jnp.concatenate([halo[:, 1280:1536], p], axis=0))
        tok = i * t + lax.broadcasted_iota(jnp.int32, (t, 1), 0)
        pooled = _group_select([sw[HALO:HALO + t] / cnt - p for sw, cnt in zip(sums, _pool_counts(tok))])
        bpl_ref[...] = (_dot(pooled.astype(BF16), wp_ref[...]) * ps_ref[...]).astype(BF16)

    return _pcall(
        body, name="mixers_fwd", grid=(s // t,),
        in_specs=[_rows(t, PM_W), _halo_before(t), _whole(ln_g.shape), _whole(ln_b.shape), _whole(wm.shape),
                  _whole(sgb.shape), _whole(conv_w.shape), _whole(wp.shape), _whole(pscale.shape)],
        out_specs=[_rows(t, BR_W)] * 3,
        out_shape=[_sds((s, BR_W), BF16)] * 3,
        vmem=VMEM_LIMIT)(pm, pm, ln_g, ln_b, wm, sgb, conv_w, wp, pscale)


def _merge_fwd(x, hb, branches, wg, bg, wbr, wout, g2, t):
    s = x.shape[0]

    def body(x_ref, hb_ref, b0_ref, b1_ref, b2_ref, b3_ref, wg_ref, bg_ref, wbr_ref, wout_ref, g2_ref,
             xmid_ref, mrg_ref, t_ref):
        hb = hb_ref[...]
        merged = jnp.zeros((t, D_MODEL), F32)
        for k, b_ref in enumerate((b0_ref, b1_ref, b2_ref, b3_ref)):
            cols = slice(k * D_MODEL, (k + 1) * D_MODEL)
            gate = _sigmoid(_dot(hb, wg_ref[cols, :], NT) + bg_ref[:, cols])
            merged = merged + gate * _dot(b_ref[...], wbr_ref[k], NT)
        mb = merged.astype(BF16)
        mrg_ref[...] = mb
        tt = _dot(mb, wout_ref[...])
        t_ref[...] = tt
        xmid_ref[...] = x_ref[...] + _rms(tt, g2_ref[...])

    return _pcall(
        body, name="merge_fwd", grid=(s // t,),
        in_specs=[_rows(t, D_MODEL), _rows(t, D_MODEL)] + [_rows(t, BR_W)] * 4 +
                 [_whole(wg.shape), _whole(bg.shape), _whole(wbr.shape), _whole(wout.shape), _whole(g2.shape)],
        out_specs=[_rows(t, D_MODEL)] * 3,
        out_shape=[_sds((s, D_MODEL), F32), _sds((s, D_MODEL), BF16), _sds((s, D_MODEL), F32)],
        vmem=VMEM_LIMIT)(x, hb, *branches, wg, bg, wbr, wout, g2)


def _ffn_fwd(x, g3, w1, w2, g4, t):
    s = x.shape[0]

    def body(x_ref, g3_ref, w1_ref, w2_ref, g4_ref, xo_ref, f_ref):
        h = _rms(x_ref[...], g3_ref[...]).astype(BF16)
        f = jnp.zeros((t, D_MODEL), F32)
        for j in range(D_FF // D_MODEL):
            cols = slice(j * D_MODEL, (j + 1) * D_MODEL)
            r = jnp.square(jnp.maximum(_dot(h, w1_ref[cols, :], NT), 0.0)).astype(BF16)
            f = f + _dot(r, w2_ref[cols, :])
        f_ref[...] = f
        xo_ref[...] = x_ref[...] + _rms(f, g4_ref[...])

    return _pcall(
        body, name="ffn_fwd", grid=(s // t,),
        in_specs=[_rows(t, D_MODEL), _whole(g3.shape), _whole(w1.shape), _whole(w2.shape), _whole(g4.shape)],
        out_specs=[_rows(t, D_MODEL)] * 2,
        out_shape=[_sds((s, D_MODEL), F32)] * 2,
        vmem=VMEM_LIMIT)(x, g3, w1, w2, g4)


def _loss_head(y, target, t):
    s = y.shape[0]
    n = s // t

    def body(y_ref, tg_ref, dy_ref, loss_ref, acc_scr):
        i = pl.program_id(0)
        d = y_ref[...] - tg_ref[...]
        dy_ref[...] = d * (1.0 / D_MODEL)
        _accumulate(acc_scr, jnp.sum(d * d, axis=0, keepdims=True), i == 0)

        @pl.when(i == n - 1)
        def _():
            loss_ref[...] = jnp.full(loss_ref.shape, 0.5 / D_MODEL, F32) * jnp.sum(acc_scr[...])

    return _pcall(
        body, name="loss_head", grid=(n,),
        in_specs=[_rows(t, D_MODEL)] * 2,
        out_specs=[_rows(t, D_MODEL), _acc((8, LANES))],
        out_shape=[_sds((s, D_MODEL), F32), _sds((8, LANES), F32)],
        scratch=[pltpu.VMEM((1, D_MODEL), F32)])(y, target)


def _ffn_bwd_norms(x_mid, f, dxo, g3, g4, t):
    s = x_mid.shape[0]

    def body(x_ref, f_ref, dxo_ref, g3_ref, g4_ref, h2_ref, df_ref, dg4_ref):
        h2_ref[...] = _rms(x_ref[...], g3_ref[...]).astype(BF16)
        df, dg4 = _rms_bwd(f_ref[...], g4_ref[...], dxo_ref[...])
        df_ref[...] = df.astype(BF16)
        _accumulate(dg4_ref, dg4, pl.program_id(0) == 0)

    return _pcall(
        body, name="ffn_bwd_norms", grid=(s // t,),
        in_specs=[_rows(t, D_MODEL)] * 3 + [_whole(g3.shape), _whole(g4.shape)],
        out_specs=[_rows(t, D_MODEL), _rows(t, D_MODEL), _acc((1, D_MODEL))],
        out_shape=[_sds((s, D_MODEL), BF16), _sds((s, D_MODEL), BF16), _sds((1, D_MODEL), F32)])(x_mid, f, dxo, g3, g4)


def _ffn_bwd_weights(h2, df, w1, w2, t):
    s = h2.shape[0]
    blk = D_MODEL
    n = s // t

    def body(h2_ref, df_ref, w1_ref, w2_ref, da_ref, dw1_ref, dw2_ref, acc1, acc2):
        first, last = pl.program_id(1) == 0, pl.program_id(1) == n - 1
        h2v, dfv = h2_ref[...], df_ref[...]
        rl = jnp.maximum(_dot(h2v, w1_ref[...], NT), 0.0)
        _accumulate_out(acc2, dw2_ref, _dot((rl * rl).astype(BF16), dfv, TN), first, last)
        da = (_dot(dfv, w2_ref[...], NT) * (2.0 * rl)).astype(BF16)
        da_ref[...] = da
        _accumulate_out(acc1, dw1_ref, _dot(da, h2v, TN), first, last)

    tok = pl.BlockSpec((t, D_MODEL), lambda j, i: (i, 0))
    hid = pl.BlockSpec((blk, D_MODEL), lambda j, i: (j, 0))
    return _pcall(
        body, name="ffn_bwd_weights", grid=(D_FF // blk, n),
        in_specs=[tok, tok, hid, hid],
        out_specs=[pl.BlockSpec((t, blk), lambda j, i: (i, j)), hid, hid],
        out_shape=[_sds((s, D_FF), BF16), _sds((D_FF, D_MODEL), BF16), _sds((D_FF, D_MODEL), BF16)],
        scratch=[pltpu.VMEM((blk, D_MODEL), F32)] * 2,
        vmem=VMEM_LIMIT)(h2, df, w1, w2)


def _ffn_bwd_input(da, w1, x_mid, dxo, g3, t):
    s = x_mid.shape[0]

    def body(da_ref, w1_ref, x_ref, dxo_ref, g3_ref, dx_ref, dg3_ref):
        dx, dg3 = _rms_bwd(x_ref[...], g3_ref[...], _dot(da_ref[...], w1_ref[...]))
        dx_ref[...] = dxo_ref[...] + dx
        _accumulate(dg3_ref, dg3, pl.program_id(0) == 0)

    return _pcall(
        body, name="ffn_bwd_input", grid=(s // t,),
        in_specs=[_rows(t, D_FF), _whole(w1.shape), _rows(t, D_MODEL), _rows(t, D_MODEL), _whole(g3.shape)],
        out_specs=[_rows(t, D_MODEL), _acc((1, D_MODEL))],
        out_shape=[_sds((s, D_MODEL), F32), _sds((1, D_MODEL), F32)],
        vmem=VMEM_LIMIT)(da, w1, x_mid, dxo, g3)


def _merge_bwd_out(tt, dxmid, g2, wout, merged, t):
    s = tt.shape[0]
    n = s // t

    def body(t_ref, dx_ref, g2_ref, wout_ref, mrg_ref, dm_ref, dwout_ref, dg2_ref, acc):
        first, last = pl.program_id(0) == 0, pl.program_id(0) == n - 1
        dt, dg2 = _rms_bwd(t_ref[...], g2_ref[...], dx_ref[...])
        dtb = dt.astype(BF16)
        dm_ref[...] = _dot(dtb, wout_ref[...], NT)
        _accumulate_out(acc, dwout_ref, _dot(mrg_ref[...], dtb, TN), first, last)
        _accumulate(dg2_ref, dg2, first)

    return _pcall(
        body, name="merge_bwd_out", grid=(n,),
        in_specs=[_rows(t, D_MODEL), _rows(t, D_MODEL), _whole(g2.shape), _whole(wout.shape), _rows(t, D_MODEL)],
        out_specs=[_rows(t, D_MODEL), _acc((D_MODEL, D_MODEL)), _acc((1, D_MODEL))],
        out_shape=[_sds((s, D_MODEL), F32), _sds((D_MODEL, D_MODEL), BF16), _sds((1, D_MODEL), F32)],
        scratch=[pltpu.VMEM((D_MODEL, D_MODEL), F32)],
        vmem=VMEM_LIMIT)(tt, dxmid, g2, wout, merged)


def _merge_bwd_branch(k, hb, br, dm, wg, bg, wbr, t):
    s = hb.shape[0]
    n = s // t

    def body(hb_ref, br_ref, dm_ref, wg_ref, bg_ref, wbr_ref, dbr_ref, dpre_ref, dwg_ref, dbg_ref, dwbr_ref,
             acc_g, acc_br):
        first, last = pl.program_id(0) == 0, pl.program_id(0) == n - 1
        hbv, brv, dmv, wbrv = hb_ref[...], br_ref[...], dm_ref[...], wbr_ref[0]
        gate = _sigmoid(_dot(hbv, wg_ref[...], NT) + bg_ref[...])
        dy = (dmv * gate).astype(BF16)
        dpre = dmv * _dot(brv, wbrv, NT) * gate * (1.0 - gate)
        dpb = dpre.astype(BF16)
        dpre_ref[...] = dpb
        dbr_ref[...] = _dot(dy, wbrv)
        _accumulate_out(acc_br, dwbr_ref, _dot(dy, brv, TN), first, last)
        _accumulate_out(acc_g, dwg_ref, _dot(dpb, hbv, TN), first, last)
        _accumulate(dbg_ref, jnp.sum(dpre, axis=0, keepdims=True), first)

    return _pcall(
        body, name=f"merge_bwd_branch{k}", grid=(n,),
        in_specs=[_rows(t, D_MODEL), _rows(t, BR_W), _rows(t, D_MODEL),
                  pl.BlockSpec((D_MODEL, D_MODEL), lambda i: (k, 0)), pl.BlockSpec((1, D_MODEL), lambda i: (0, k)),
                  pl.BlockSpec((1, D_MODEL, BR_W), lambda i: (k, 0, 0))],
        out_specs=[_rows(t, BR_W), _rows(t, D_MODEL), _acc((D_MODEL, D_MODEL)), _acc((1, D_MODEL)), _acc((D_MODEL, BR_W))],
        out_shape=[_sds((s, BR_W), F32), _sds((s, D_MODEL), BF16), _sds((D_MODEL, D_MODEL), BF16),
                   _sds((1, D_MODEL), F32), _sds((D_MODEL, BR_W), BF16)],
        scratch=[pltpu.VMEM((D_MODEL, D_MODEL), F32), pltpu.VMEM((D_MODEL, BR_W), F32)],
        vmem=VMEM_LIMIT)(hb, br, dm, wg, bg, wbr)


def _mixers_bwd(pm, dbsg, dbcv, dbpl, ln_g, ln_b, wm, wmt, sgb, conv_w, wp, pscale, t):
    s = pm.shape[0]
    n = s // t
    nb = t // HALO

    def body(pm_ref, before_ref, after_ref, dsg_ref, dcv_ref, dcva_ref, dpl_ref, dpla_ref, lng_ref, lnb_ref,
             wm_ref, wmt_ref, sgb_ref, cw_ref, wp_ref, ps_ref,
             dpm_ref, dlng_ref, dlnb_ref, dwm_ref, dsgb_ref, dcw_ref, dwp_ref, dps_ref):
        i = pl.program_id(0)
        first = i == 0
        before = jnp.where(i > 0, before_ref[...], 0.0)
        after = jnp.where(i < n - 1, after_ref[...], 0.0)

        u_raw, v_raw = pm_ref[:, 0:256], pm_ref[:, 256:512]
        lng, lnb = lng_ref[...], lnb_ref[...]
        u, gelu_u_vjp = jax.vjp(_gelu, u_raw)
        vn, norm_vjp = jax.vjp(lambda v_, g_, b_: _layernorm(_gelu(v_), g_, b_), v_raw, lng, lnb)
        vnb = vn.astype(BF16)
        dsg = dsg_ref[...]
        grp = _lane_group((SG_CHUNK, BR_W))
        tri = (lax.broadcasted_iota(jnp.int32, (SG_CHUNK, SG_CHUNK), 1)
               <= lax.broadcasted_iota(jnp.int32, (SG_CHUNK, SG_CHUNK), 0))
        du_parts, dvn_parts = [], []
        dsgb = jnp.zeros((SG_CHUNK, BR_W), F32)
        dwm = [jnp.zeros((SG_CHUNK, SG_CHUNK), F32) for _ in range(SG_GROUPS)]
        for c in range(t // SG_CHUNK):
            rows = slice(c * SG_CHUNK, (c + 1) * SG_CHUNK)
            mix = _sg_mix(wm_ref, vnb[rows], sgb_ref[...])
            du_parts.append(dsg[rows] * mix)
            ds = dsg[rows] * u[rows]
            dsgb = dsgb + ds
            dsb = [jnp.where(grp == g, ds, 0.0).astype(BF16) for g in range(SG_GROUPS)]
            for g in range(SG_GROUPS):
                dwm[g] = dwm[g] + _dot(dsb[g], vnb[rows], NT)
            dvn_parts.append(_group_select([_dot(wmt_ref[g], dsb[g]) for g in range(SG_GROUPS)]))
        (du_raw,) = gelu_u_vjp(jnp.concatenate(du_parts, axis=0))
        dv_raw, dlng, dlnb = norm_vjp(jnp.concatenate(dvn_parts, axis=0))
        dpm_ref[:, 0:256] = du_raw.astype(BF16)
        dpm_ref[:, 256:512] = dv_raw.astype(BF16)
        _accumulate(dlng_ref, dlng, first)
        _accumulate(dlnb_ref, dlnb, first)
        _accumulate(dsgb_ref, dsgb, first)
        for g in range(SG_GROUPS):
            _accumulate(dwm_ref.at[g], jnp.where(tri, dwm[g], 0.0), first)

        xin, bg, cg = pm_ref[:, 512:768], pm_ref[:, 768:1024], pm_ref[:, 1024:1280]
        z = cg * xin
        zext = jnp.concatenate([before[:, 1024:1280] * before[:, 512:768], z], axis=0)
        z1, z2 = _shift_down(zext, 1, t), _shift_down(zext, 2, t)
        w0, w1, w2 = cw_ref[0:1, :], cw_ref[1:2, :], cw_ref[2:3, :]
        dcv = dcv_ref[...]
        y = w0 * z2 + w1 * z1 + w2 * z
        dy = dcv * bg
        dyext = jnp.concatenate([dy, jnp.where(i < n - 1, dcva_ref[...], 0.0) * after[:, 768:1024]], axis=0)
        dz = w2 * dy + w1 * _shift_up(dyext, 1, t) + w0 * _shift_up(dyext, 2, t)
        dpm_ref[:, 512:768] = (dz * cg).astype(BF16)
        dpm_ref[:, 768:1024] = (dcv * y).astype(BF16)
        dpm_ref[:, 1024:1280] = (dz * xin).astype(BF16)
        dcw = jnp.concatenate([jnp.sum(dy * z2, axis=0, keepdims=True), jnp.sum(dy * z1, axis=0, keepdims=True),
                               jnp.sum(dy * z, axis=0, keepdims=True)], axis=0)
        _accumulate(dcw_ref, jnp.concatenate([dcw, jnp.zeros((8 - CONV_K, BR_W), F32)], axis=0), first)

        p = pm_ref[:, 1280:1536]
        tok = i * t + lax.broadcasted_iota(jnp.int32, (t, 1), 0)
        sums = _pool_windows(jnp.concatenate([before[:, 1280:1536], p], axis=0))
        pooled = _group_select([sw[HALO:HALO + t] / cnt - p for sw, cnt in zip(sums, _pool_counts(tok))]).astype(BF16)
        mixed = _dot(pooled, wp_ref[...])
        dpl = dpl_ref[...]
        ps = ps_ref[...]
        dmix = (dpl * ps).astype(BF16)
        _accumulate(dps_ref, jnp.sum(dpl * mixed, axis=0, keepdims=True), first)
        _accumulate(dwp_ref, _dot(pooled, dmix, TN), first)
        dmix_after = (jnp.where(i < n - 1, dpla_ref[...], 0.0) * ps).astype(BF16)
        dpo = _dot(dmix, wp_ref[...], NT)
        dpo_ext = jnp.concatenate([dpo, _dot(dmix_after, wp_ref[...], NT)], axis=0)
        tok_ext = i * t + lax.broadcasted_iota(jnp.int32, (t + HALO, 1), 0)
        dp_groups = []
        for g, (win, cnt) in enumerate(zip(POOL_WINDOWS, _pool_counts(tok_ext))):
            e = dpo_ext / cnt
            acc = e
            span = 1
            while span < win:
                acc = acc + pltpu.roll(acc, t + HALO - span, 0)
                span *= 2
            dp_groups.append(acc[0:t] - dpo)
        dpm_ref[:, 1280:1536] = _group_select(dp_groups).astype(BF16)

    row_blk = lambda w: pl.BlockSpec((t, w), lambda i: (i, 0))
    after_blk = lambda w: pl.BlockSpec((HALO, w), lambda i: (jnp.minimum((i + 1) * nb, n * nb - 1), 0))
    return _pcall(
        body, name="mixers_bwd", grid=(n,),
        in_specs=[row_blk(PM_W), _halo_before(t), after_blk(PM_W), row_blk(BR_W), row_blk(BR_W), after_blk(BR_W),
                  row_blk(BR_W), after_blk(BR_W), _whole(ln_g.shape), _whole(ln_b.shape), _whole(wm.shape),
                  _whole(wmt.shape), _whole(sgb.shape), _whole(conv_w.shape), _whole(wp.shape), _whole(pscale.shape)],
        out_specs=[row_blk(PM_W), _acc((1, BR_W)), _acc((1, BR_W)), _acc((SG_GROUPS, SG_CHUNK, SG_CHUNK)),
                   _acc((SG_CHUNK, BR_W)), _acc((8, BR_W)), _acc((BR_W, BR_W)), _acc((1, BR_W))],
        out_shape=[_sds((s, PM_W), BF16), _sds((1, BR_W), F32), _sds((1, BR_W), F32),
                   _sds((SG_GROUPS, SG_CHUNK, SG_CHUNK), F32), _sds((SG_CHUNK, BR_W), F32), _sds((8, BR_W), F32),
                   _sds((BR_W, BR_W), F32), _sds((1, BR_W), F32)],
        vmem=VMEM_LIMIT)(pm, pm, pm, dbsg, dbcv, dbcv, dbpl, dbpl, ln_g, ln_b, wm, wmt, sgb, conv_w, wp, pscale)


def _head_delta(o, do):
    rows = o.shape[0]
    prod = o.astype(F32) * do
    lane = lax.broadcasted_iota(jnp.int32, (rows, LANES), 1)
    out = []
    for h in range(N_HEADS):
        pair = prod[:, (h // 2) * LANES:(h // 2 + 1) * LANES]
        mine = (lane < D_V) if h % 2 == 0 else (lane >= D_V)
        out.append(jnp.broadcast_to(jnp.sum(jnp.where(mine, pair, 0.0), axis=1, keepdims=True), (rows, LANES)))
    return out


def _head_do(do, h):
    lane = lax.broadcasted_iota(jnp.int32, (do.shape[0], LANES), 1)
    mine = (lane < D_V) if h % 2 == 0 else (lane >= D_V)
    pair = do[:, (h // 2) * LANES:(h // 2 + 1) * LANES]
    return jnp.where(mine, pair, 0.0).astype(BF16)


def _flash_bwd(q, k, v, o, do, lse, t, riding):
    s = v.shape[0]
    n = s // t
    qi_tab, kj_tab = _causal_steps(n, key_major=True)
    n_steps = int(qi_tab.shape[0])
    whole = slice(0, t)

    def body(qi_ref, kj_ref, q_ref, k_ref, v_ref, o_ref, do_ref, lse_ref, x_ref, dq_ref, dk_ref, dv_ref, land_ref,
             dv_scr, send_sems, recv_sems, local_sem):
        step_id = pl.program_id(0)
        qi, kj = qi_ref[step_id], kj_ref[step_id]
        at_end = _ride_along(riding, step_id, n_steps, (x_ref, land_ref, send_sems, recv_sems, local_sem))

        @pl.when(step_id == 0)
        def _():
            dq_ref[...] = jnp.zeros(dq_ref.shape, F32)

        @pl.when(qi == kj)
        def _():
            dk_ref[...] = jnp.zeros(dk_ref.shape, F32)
            dv_scr[...] = jnp.zeros(dv_scr.shape, F32)

        def step(masked, qr, kr):
            dov = do_ref[qr, :]
            delta = _head_delta(o_ref[qr, :], dov)
            reps = (kr.stop - kr.start) // LANES
            q_rows = pl.ds(pl.multiple_of(qi * t + qr.start, t // 2), qr.stop - qr.start)
            for h in range(N_HEADS):
                qh, kh = q_ref[h, qr, :], k_ref[h, kr, :]
                p = jnp.exp(_causal_scores(qh, kh, masked) - jnp.tile(lse_ref[h, qr, :], (1, reps)))
                pair = slice((h // 2) * LANES, (h // 2 + 1) * LANES)
                dv_scr[h, kr, :] += _dot(p.astype(BF16), dov[:, pair].astype(BF16), TN)
                dp = _dot(_head_do(dov, h), v_ref[kr, pair], NT)
                ds = (p * (dp - jnp.tile(delta[h], (1, reps)))).astype(BF16)
                dk_ref[h, kr, :] += _dot(ds, qh, TN)
                dq_ref[h, q_rows, :] += _dot(ds, kh)

        @pl.when(qi > kj)
        def _():
            step(False, whole, whole)

        @pl.when(qi == kj)
        def _():
            for masked, qr, kr in _diagonal_parts(t):
                step(masked, qr, kr)

        @pl.when(qi == n - 1)
        def _():
            lane = lax.broadcasted_iota(jnp.int32, (t, LANES), 1)
            for pr in range(N_HEADS // 2):
                dv_ref[:, pr * LANES:(pr + 1) * LANES] = jnp.where(lane < D_V, dv_scr[2 * pr], dv_scr[2 * pr + 1]).astype(BF16)

        at_end()

    q_blk = pl.BlockSpec((N_HEADS, t, HEAD_PAD), lambda i, qi, kj: (0, qi[i], 0))
    k_blk = pl.BlockSpec((N_HEADS, t, HEAD_PAD), lambda i, qi, kj: (0, kj[i], 0))
    v_blk = pl.BlockSpec((t, BR_W), lambda i, qi, kj: (kj[i], 0))
    o_blk = pl.BlockSpec((t, BR_W), lambda i, qi, kj: (qi[i], 0))
    dq_sds = _sds((N_HEADS, s, HEAD_PAD), F32)
    return _pcall(
        body, name="flash_bwd_exchanging", grid=(n_steps,), prefetch=2,
        in_specs=[q_blk, k_blk, v_blk, o_blk, o_blk, q_blk, HBM_REF],
        out_specs=[_whole(dq_sds.shape), k_blk, v_blk, HBM_REF],
        out_shape=[dq_sds, dq_sds, _sds((s, BR_W), BF16), riding[2]],
        scratch=[pltpu.VMEM((N_HEADS, t, LANES), F32)] + list(EXCHANGE_SEMS),
        vmem=VMEM_LIMIT)(qi_tab, kj_tab, q, k, v, o, do, lse, riding[1])


def _attn_prep_bwd(pa, dq, dk, dv, cosf, sins, qn, kvn, wq, wqs, wk, wv, t):
    s = pa.shape[0]

    def body(pa_ref, dq_ref, dk_ref, dv_ref, cos_ref, sin_ref, qn_ref, kvn_ref, wq_ref, wqs_ref, wk_ref, wv_ref,
             dpa_ref, dwq_ref, dwqs_ref, dwk_ref, dwv_ref, dqn_ref, dkvn_ref):
        first = pl.program_id(0) == 0
        cosv, sinv = cos_ref[...], sin_ref[...]
        cq, ckv = pa_ref[:, 0:Q_RANK], pa_ref[:, Q_RANK:Q_RANK + KV_RANK]
        cqn = _rms(cq, qn_ref[...]).astype(BF16)
        ckvn = _rms(ckv, kvn_ref[...]).astype(BF16)
        dvv = dv_ref[...]
        _accumulate(dwv_ref, _dot(dvv, ckvn, TN), first)
        dk_rope = dk_ref[0]
        for h in range(1, N_HEADS):
            dk_rope = dk_rope + dk_ref[h]
        cos_s, sin_s = cosv * SM_SCALE, sinv * SM_SCALE
        dqa = jnp.concatenate([(dq_ref[h] * cos_s).astype(BF16) for h in range(N_HEADS)], axis=1)
        dqs = jnp.concatenate([(dq_ref[h] * sin_s).astype(BF16) for h in range(N_HEADS)], axis=1)
        dkb = jnp.concatenate([dk_ref[h].astype(BF16) for h in range(N_HEADS)], axis=1)
        _accumulate(dwq_ref, _dot(dqa, cqn, TN), first)
        _accumulate(dwqs_ref, _dot(dqs, cqn, TN), first)
        _accumulate(dwk_ref, _dot(dkb, ckvn, TN), first)
        dcqn = _dot(dqa, wq_ref[...]) + _dot(dqs, wqs_ref[...])
        dckvn = _dot(dvv, wv_ref[...]) + _dot(dkb, wk_ref[...])
        dcq, dqn = _rms_bwd(cq, qn_ref[...], dcqn)
        dckv, dkvn = _rms_bwd(ckv, kvn_ref[...], dckvn)
        _accumulate(dqn_ref, dqn, first)
        _accumulate(dkvn_ref, dkvn, first)
        dpa_ref[:, 0:Q_RANK] = dcq.astype(BF16)
        dpa_ref[:, Q_RANK:Q_RANK + KV_RANK] = dckv.astype(BF16)
        dpa_ref[:, 384:512] = (dk_rope * cosv).astype(BF16)
        dpa_ref[:, 512:640] = (dk_rope * sinv).astype(BF16)

    head_blk = pl.BlockSpec((N_HEADS, t, HEAD_PAD), lambda i: (0, i, 0))
    return _pcall(
        body, name="attn_prep_bwd", grid=(s // t,),
        in_specs=[_rows(t, PA_W), head_blk, head_blk, _rows(t, BR_W), _rows(t, LANES), _rows(t, LANES),
                  _whole(qn.shape), _whole(kvn.shape), _whole(wq.shape), _whole(wqs.shape), _whole(wk.shape), _whole(wv.shape)],
        out_specs=[_rows(t, PA_W), _acc(wq.shape), _acc(wqs.shape), _acc(wk.shape), _acc(wv.shape),
                   _acc(qn.shape), _acc(kvn.shape)],
        out_shape=[_sds((s, PA_W), BF16), _sds(wq.shape, F32), _sds(wqs.shape, F32), _sds(wk.shape, F32),
                   _sds(wv.shape, F32), _sds(qn.shape, F32), _sds(kvn.shape, F32)],
        vmem=VMEM_LIMIT)(pa, dq, dk, dv, cosf, sins, qn, kvn, wq, wqs, wk, wv)


def _inproj_bwd(x, g1, dxmid, dpa, dpm, dpres, w_in, wg, t):
    s = x.shape[0]
    n = s // t

    def body(x_ref, g1_ref, dxm_ref, dpa_ref, dpm_ref, d0_ref, d1_ref, d2_ref, d3_ref, w_ref, wg_ref,
             dx_ref, dg1_ref, dwa_ref, dwm_ref, acc_a, acc_m):
        first, last = pl.program_id(0) == 0, pl.program_id(0) == n - 1
        xv, g1v = x_ref[...], g1_ref[...]
        hb = _rms(xv, g1v).astype(BF16)
        dpav, dpmv = dpa_ref[...], dpm_ref[...]
        dh = _dot(dpav, w_ref[:PA_W, :]) + _dot(dpmv, w_ref[PA_W:, :])
        for k, d_ref in enumerate((d0_ref, d1_ref, d2_ref, d3_ref)):
            dh = dh + _dot(d_ref[...], wg_ref[k * D_MODEL:(k + 1) * D_MODEL, :])
        dx, dg1 = _rms_bwd(xv, g1v, dh)
        dx_ref[...] = dxm_ref[...] + dx
        _accumulate(dg1_ref, dg1, first)
        _accumulate_out(acc_a, dwa_ref, _dot(dpav, hb, TN), first, last)
        _accumulate_out(acc_m, dwm_ref, _dot(dpmv, hb, TN), first, last)

    return _pcall(
        body, name="inproj_bwd", grid=(n,),
        in_specs=[_rows(t, D_MODEL), _whole(g1.shape), _rows(t, D_MODEL), _rows(t, PA_W), _rows(t, PM_W)] +
                 [_rows(t, D_MODEL)] * 4 + [_whole(w_in.shape), _whole(wg.shape)],
        out_specs=[_rows(t, D_MODEL), _acc((1, D_MODEL)), _acc((PA_W, D_MODEL)), _acc((PM_W, D_MODEL))],
        out_shape=[_sds((s, D_MODEL), F32), _sds((1, D_MODEL), F32), _sds((PA_W, D_MODEL), BF16), _sds((PM_W, D_MODEL), BF16)],
        scratch=[pltpu.VMEM((PA_W, D_MODEL), F32), pltpu.VMEM((PM_W, D_MODEL), F32)],
        vmem=VMEM_LIMIT)(x, g1, dxmid, dpa, dpm, *dpres, w_in, wg)


def _my_place():
    return lax.axis_index("x"), lax.axis_index("y"), lax.axis_index("c")


def _flip(place, k):
    x, y, c = place
    return (1 - x if k & 4 else x, 1 - y if k & 2 else y, 1 - c if k & 1 else c)


def _rank(place):
    x, y, c = place
    return 4 * x + 2 * y + c


EXCHANGE_SEMS = (pltpu.SemaphoreType.DMA((7,)), pltpu.SemaphoreType.DMA((7,)), pltpu.SemaphoreType.DMA)
HBM_REF = pl.BlockSpec(memory_space=pl.ANY)


def _gather_plan(x_ref, out_ref, send_sems, recv_sems, local_sem):
    me = _my_place()
    sibling = _flip(me, 1)
    chips = (4, 2, 6)

    def copy(k, src_place, to, src=None):
        slot = out_ref.at[_rank(src_place)]
        return pltpu.make_async_remote_copy(
            src_ref=slot if src is None else src, dst_ref=slot, send_sem=send_sems.at[k], recv_sem=recv_sems.at[k],
            device_id=to, device_id_type=MESH_ID)

    mine = pltpu.make_async_copy(x_ref, out_ref.at[_rank(me)], local_sem)
    first = [copy(0, me, sibling, src=x_ref)] + [copy(1 + j, me, _flip(me, kc), src=x_ref) for j, kc in enumerate(chips)]
    passed = [copy(4 + j, _flip(me, kc), sibling) for j, kc in enumerate(chips)]

    def start():
        mine.start()
        for cp in first:
            cp.start()

    def forward():
        for j, kc in enumerate(chips):
            copy(1 + j, _flip(me, kc), me).wait_recv()
            passed[j].start()

    def finish():
        copy(0, sibling, me).wait_recv()
        for j, kc in enumerate(chips):
            copy(4 + j, _flip(sibling, kc), me).wait_recv()
        for cp in first + passed:
            cp.wait_send()
        mine.wait()

    return start, forward, finish


def _exchange_plan(x_ref, out_ref, send_sems, recv_sems, local_sem):
    me = _my_place()
    my_rank = _rank(me)
    mine = pltpu.make_async_copy(x_ref.at[my_rank], out_ref.at[my_rank], local_sem)
    sends = [pltpu.make_async_remote_copy(
        src_ref=x_ref.at[_rank(_flip(me, k))], dst_ref=out_ref.at[my_rank], send_sem=send_sems.at[k - 1],
        recv_sem=recv_sems.at[k - 1], device_id=_flip(me, k), device_id_type=MESH_ID) for k in range(1, N_DEV)]

    def start():
        mine.start()
        for cp in sends:
            cp.start()

    def forward():
        pass

    def finish():
        for k in range(1, N_DEV):
            slot = out_ref.at[_rank(_flip(me, k))]
            pltpu.make_async_remote_copy(
                src_ref=slot, dst_ref=slot, send_sem=send_sems.at[k - 1], recv_sem=recv_sems.at[k - 1],
                device_id=_flip(me, k), device_id_type=MESH_ID).wait_recv()
        for cp in sends:
            cp.wait_send()
        mine.wait()

    return start, forward, finish


def _exchange_alone(plan, src, out_sds, name):
    def body(x_ref, out_ref, send_sems, recv_sems, local_sem):
        start, forward, finish = plan(x_ref, out_ref, send_sems, recv_sems, local_sem)
        start()
        forward()
        finish()

    return pl.pallas_call(body, name=name, out_shape=out_sds, in_specs=[HBM_REF], out_specs=HBM_REF,
                          scratch_shapes=list(EXCHANGE_SEMS))(src)


def _all_gather(shard, name):
    return _exchange_alone(_gather_plan, shard, _sds((N_DEV,) + shard.shape, shard.dtype), name)


def _all_to_all(blocks, name):
    return _exchange_alone(_exchange_plan, blocks, _sds(blocks.shape, blocks.dtype), name)


def _riding(plan, src, out_sds):
    return (plan, src, out_sds)


def _ride_along(riding, step_id, n_steps, refs):
    start, forward, finish = riding[0](*refs)

    @pl.when(step_id == 0)
    def _():
        start()

    @pl.when(step_id == (3 * n_steps) // 4)
    def _():
        forward()

    def at_end():
        @pl.when(step_id == n_steps - 1)
        def _():
            finish()

    return at_end


def _sum_slots(parts, name):
    _, r, c = parts.shape
    t = math.gcd(r, ROW_PAD)

    def body(p_ref, o_ref):
        acc = p_ref[0].astype(F32)
        for d in range(1, N_DEV):
            acc = acc + p_ref[d].astype(F32)
        o_ref[...] = acc

    return _pcall(
        body, name=name, grid=(r // t,),
        in_specs=[pl.BlockSpec((N_DEV, t, c), lambda i: (0, i, 0))], out_specs=_rows(t, c),
        out_shape=_sds((r, c), F32))(parts)


def _adamw(w, g, m, v, name):
    depth, r, c = w.shape
    t = r if r <= ROW_PAD else math.gcd(r, ROW_PAD)
    blk = pl.BlockSpec((1, t, c), lambda l, i: (l, i, 0))

    def body(w_ref, g_ref, m_ref, v_ref, d_ref, nm_ref, nv_ref):
        gv = g_ref[...]
        nm = ADAM_B1 * m_ref[...] + (1.0 - ADAM_B1) * gv
        nv = ADAM_B2 * v_ref[...] + (1.0 - ADAM_B2) * jnp.square(gv)
        m_hat = nm / (1.0 - ADAM_B1 ** ADAM_STEP)
        v_hat = nv / (1.0 - ADAM_B2 ** ADAM_STEP)
        d_ref[...] = -ADAM_LR * (m_hat / (jnp.sqrt(v_hat) + ADAM_EPS) + ADAM_WD * w_ref[...])
        nm_ref[...] = nm
        nv_ref[...] = nv

    return _pcall(
        body, name=name, grid=(depth, r // t),
        in_specs=[blk] * 4, out_specs=[blk] * 3, out_shape=[_sds(w.shape, F32)] * 3)(w, g, m, v)


BEFORE_ATTN = ("w_in", "w_uq", "w_ukv")
AFTER_ATTN = ("w_br_mla", "w_br_sg", "w_br_conv", "w_br_pool", "w_out", "w_ff1", "w_ff2", "conv_w")
SHARDED = BEFORE_ATTN + AFTER_ATTN
ROW_SHARDED = ("w_out", "w_ff2")
REPLICATED = ("norm_mix_pre", "gate_b", "q_norm", "kv_norm", "sg_ln_g", "sg_ln_b", "sg_w", "sg_b", "pool_w",
              "pool_scale", "norm_mix_post", "norm_ffn_pre", "norm_ffn_post")
ROW_PAD = 256
GRAD_ROW_PAD = 64
PART_ROWS = 16


def _pack_rows(arrays, dtype, multiple, lead=()):
    rows, offsets, at = [], [], 0
    zero_pad = ((0, 0),) * len(lead)
    for a in arrays:
        if a.ndim == len(lead) + 2 and a.shape[-1] == D_MODEL:
            n = a.shape[-2]
            n_pad = -(-n // PART_ROWS) * PART_ROWS
            rows.append(jnp.pad(a.astype(dtype), zero_pad + ((0, n_pad - n), (0, 0))))
        else:
            flat = a.reshape(lead + (-1,)).astype(dtype)
            n = -(-flat.shape[-1] // D_MODEL)
            n_pad = -(-n // PART_ROWS) * PART_ROWS
            flat = jnp.pad(flat, zero_pad + ((0, n_pad * D_MODEL - flat.shape[-1]),))
            rows.append(flat.reshape(lead + (n_pad, D_MODEL)))
        offsets.append((at, n))
        at += n_pad
    pad = -at % multiple
    if pad:
        rows.append(jnp.zeros(lead + (pad, D_MODEL), dtype))
    return jnp.concatenate(rows, axis=len(lead)), offsets


def _unpack_rows(buf, offsets, shapes):
    lead = buf.shape[:-2]
    out = []
    for (at, n), shape in zip(offsets, shapes):
        if tuple(shape) == (n, D_MODEL):
            out.append(buf[..., at:at + n, :])
            continue
        size = math.prod(shape)
        flat = buf[..., at:at + n, :].reshape(lead + (n * D_MODEL,))[..., :size]
        out.append(flat.reshape(lead + tuple(shape)))
    return out


def _pack_shards(p, depth, names):
    arrays = []
    for n in names:
        a = p[n][0] if n in ROW_SHARDED else jnp.swapaxes(p[n][0], -1, -2)
        arrays.append(lax.bitcast_convert_type(a, BF16) if n == "conv_w" else a.astype(BF16))
    packed, offsets = _pack_rows(arrays, BF16, GRAD_ROW_PAD, lead=(depth,))
    return packed, offsets, [a.shape[1:] for a in arrays]


def _unpack_layer_weights(names, gathered, offsets, shapes):
    full = {}
    for n, gth in zip(names, _unpack_rows(gathered, offsets, shapes)):
        if n == "conv_w":
            gth = lax.bitcast_convert_type(gth, F32)
        full[n] = gth.reshape((-1,) + gth.shape[2:])
        if n == "conv_w":
            full[n] = full[n].T
    return full


def _owner_major(grad_sharded_dim_first):
    a = grad_sharded_dim_first
    return a.reshape((N_DEV, a.shape[0] // N_DEV) + a.shape[1:])


def _owner_block_shape(name, shard_shape):
    k, n = shard_shape[-2:]
    return (k, n) if name in ROW_SHARDED else (n, k)


def _natural_shard(name, blocks):
    return blocks if name in ROW_SHARDED else jnp.swapaxes(blocks, -1, -2)


def _swap_halves(a, axis=-1):
    lo, hi = jnp.split(a, 2, axis=axis)
    return jnp.concatenate([hi, lo], axis=axis)


def _pad_rows(a, top, total):
    return jnp.pad(a, ((0, 0),) * (a.ndim - 2) + ((top, total - top - a.shape[-2]), (0, 0)))


def _weights_before_attn(full, small, l):
    w_in = full["w_in"]
    k_r = w_in[384:416]
    w_in_k = jnp.concatenate(
        [w_in[0:384], _pad_rows(k_r, D_NOPE, HEAD_PAD), _pad_rows(_swap_halves(k_r, axis=0), D_NOPE, HEAD_PAD), w_in[416:1952]], axis=0)
    w_uq = full["w_uq"].reshape(N_HEADS, D_NOPE + D_ROPE, Q_RANK)
    wq = _pad_rows(w_uq, 0, HEAD_PAD).reshape(N_HEADS * HEAD_PAD, Q_RANK)
    wqs = _pad_rows(_swap_halves(w_uq[:, D_NOPE:], axis=1), D_NOPE, HEAD_PAD).reshape(N_HEADS * HEAD_PAD, Q_RANK)
    w_ukv = full["w_ukv"].reshape(N_HEADS, D_NOPE + D_V, KV_RANK)
    wk = _pad_rows(w_ukv[:, :D_NOPE], 0, HEAD_PAD).reshape(N_HEADS * HEAD_PAD, KV_RANK)
    wv = w_ukv[:, D_NOPE:].reshape(N_HEADS * D_V, KV_RANK)
    tri = jnp.tril(jnp.ones((SG_CHUNK, SG_CHUNK), bool))
    wm = jnp.where(tri, small["sg_w"][l], 0.0)
    pool_w = small["pool_w"][l]
    wp = jnp.zeros((BR_W, BR_W), F32)
    for g in range(len(POOL_WINDOWS)):
        wp = wp.at[g * POOL_GROUP:(g + 1) * POOL_GROUP, g * POOL_GROUP:(g + 1) * POOL_GROUP].set(pool_w[g])
    vec = lambda name: small[name][l][None, :]
    return dict(
        w_in=w_in_k, wg=w_in[1952:], bg=vec("gate_b"), g1=vec("norm_mix_pre"), qn=vec("q_norm"), kvn=vec("kv_norm"),
        wq=wq, wqs=wqs, wk=wk, wv=wv,
        ln_g=vec("sg_ln_g"), ln_b=vec("sg_ln_b"), wm=wm.astype(BF16), wmt=jnp.swapaxes(wm, 1, 2).astype(BF16),
        sgb=jnp.repeat(small["sg_b"][l].T, POOL_GROUP, axis=1), wp=wp.astype(BF16), pscale=vec("pool_scale"),
        g2=vec("norm_mix_post"), g3=vec("norm_ffn_pre"), g4=vec("norm_ffn_post"))


def _weights_after_attn(full):
    return dict(
        conv_w=full["conv_w"], wbr=jnp.stack([full["w_br_mla"], full["w_br_sg"], full["w_br_conv"], full["w_br_pool"]]),
        wout=full["w_out"], w1=full["w_ff1"], w2=full["w_ff2"])


def _grads_after_attn(g):
    own = _owner_major
    return dict(
        w_br_mla=own(g["dwbr"][0]), w_br_sg=own(g["dwbr"][1]), w_br_conv=own(g["dwbr"][2]), w_br_pool=own(g["dwbr"][3]),
        w_out=own(g["dwout"]), w_ff1=own(g["dw1"]), w_ff2=own(g["dw2"]), conv_w=own(g["dcw"][:CONV_K].T))


def _grads_before_attn(g):
    dwa = g["dwa"]
    rope_rows = slice(D_NOPE, D_NOPE + D_ROPE)
    d_kr = dwa[384:512][rope_rows].astype(F32) + _swap_halves(dwa[512:640][rope_rows], axis=0).astype(F32)
    w_in = jnp.concatenate([dwa[0:384], d_kr.astype(BF16), g["dwm_in"]] + list(g["dwg"]), axis=0)
    dwq, dwqs = (g[n].reshape(N_HEADS, HEAD_PAD, Q_RANK) for n in ("dwq", "dwqs"))
    d_rope = dwq[:, rope_rows] + _swap_halves(dwqs[:, rope_rows], axis=1)
    w_uq = jnp.concatenate([dwq[:, :D_NOPE], d_rope], axis=1).reshape(N_HEADS * (D_NOPE + D_ROPE), Q_RANK)
    dwk = g["dwk"].reshape(N_HEADS, HEAD_PAD, KV_RANK)
    w_ukv = jnp.concatenate([dwk[:, :D_NOPE], g["dwv"].reshape(N_HEADS, D_V, KV_RANK)], axis=1)
    w_ukv = w_ukv.reshape(N_HEADS * (D_NOPE + D_V), KV_RANK)
    pool_w = jnp.stack([g["dwp"][i * POOL_GROUP:(i + 1) * POOL_GROUP, i * POOL_GROUP:(i + 1) * POOL_GROUP]
                        for i in range(len(POOL_WINDOWS))])
    sg_b = g["dsgb"].reshape(SG_CHUNK, SG_GROUPS, POOL_GROUP).sum(axis=-1).T
    own = _owner_major
    return dict(
        w_in=own(w_in), w_uq=own(w_uq), w_ukv=own(w_ukv),
        norm_mix_pre=g["dg1"][0], gate_b=jnp.concatenate([b[0] for b in g["dbg"]]), q_norm=g["dqn"][0], kv_norm=g["dkvn"][0],
        sg_ln_g=g["dlng"][0], sg_ln_b=g["dlnb"][0], sg_w=g["dwm"], sg_b=sg_b, pool_w=pool_w, pool_scale=g["dps"][0],
        norm_mix_post=g["dg2"][0], norm_ffn_pre=g["dg3"][0], norm_ffn_post=g["dg4"][0])


def _rope_tables(positions):
    inv_freq = ROPE_BASE ** (-jnp.arange(0, D_ROPE, 2, dtype=F32) / D_ROPE)
    ang = positions.astype(F32)[:, None] * inv_freq
    cos, sin = jnp.cos(ang), jnp.sin(ang)
    s = positions.shape[0]
    cosf = jnp.concatenate([jnp.ones((s, D_NOPE), F32), cos, cos, jnp.zeros((s, HEAD_PAD - D_NOPE - D_ROPE), F32)], axis=1)
    sins = jnp.concatenate([jnp.zeros((s, D_NOPE), F32), -sin, sin, jnp.zeros((s, HEAD_PAD - D_NOPE - D_ROPE), F32)], axis=1)
    return cosf, sins


def _layer_fwd(x, w, weights_after_attn, cosf, sins, tiles, riding):
    t, ta = tiles["tok"], tiles["attn"]
    hb, pa, pm = _inproj_fwd(x, w["g1"], w["w_in"], tiles["wgrad"])
    q, k, v = _attn_prep_fwd(pa, cosf, sins, w["qn"], w["kvn"], w["wq"], w["wqs"], w["wk"], w["wv"], tiles["wgrad"])
    o, lse, landed = _flash_fwd(q, k, v, ta, riding)
    w = {**w, **weights_after_attn(landed)}
    bsg, bcv, bpl = _mixers_fwd(pm, w["ln_g"], w["ln_b"], w["wm"], w["sgb"], w["conv_w"], w["wp"], w["pscale"], tiles["wgrad"])
    x_mid, merged, tt = _merge_fwd(x, hb, (o, bsg, bcv, bpl), w["wg"], w["bg"], w["wbr"], w["wout"], w["g2"], t)
    x_out, f = _ffn_fwd(x_mid, w["g3"], w["w1"], w["w2"], w["g4"], t)
    saved = dict(x=x, hb=hb, pa=pa, pm=pm, q=q, k=k, v=v, o=o, lse=lse, bsg=bsg, bcv=bcv, bpl=bpl, x_mid=x_mid,
                 merged=merged, tt=tt, f=f)
    return x_out, saved, w, landed


def _layer_bwd(dxo, a, w, cosf, sins, tiles, pending):
    t, ta, tb, tw = tiles["tok"], tiles["attn"], tiles["inproj_bwd"], tiles["wgrad"]
    g = {}
    h2, df, g["dg4"] = _ffn_bwd_norms(a["x_mid"], a["f"], dxo, w["g3"], w["g4"], tw)
    da, g["dw1"], g["dw2"] = _ffn_bwd_weights(h2, df, w["w1"], w["w2"], tw)
    dxmid, g["dg3"] = _ffn_bwd_input(da, w["w1"], a["x_mid"], dxo, w["g3"], t)
    dm, g["dwout"], g["dg2"] = _merge_bwd_out(a["tt"], dxmid, w["g2"], w["wout"], a["merged"], tw)
    dbrs, dpres, g["dwg"], g["dbg"], g["dwbr"] = [], [], [], [], []
    for k, br in enumerate((a["o"], a["bsg"], a["bcv"], a["bpl"])):
        dbr, dpre, dwg, dbg, dwbr = _merge_bwd_branch(k, a["hb"], br, dm, w["wg"], w["bg"], w["wbr"], tw)
        dbrs.append(dbr); dpres.append(dpre); g["dwg"].append(dwg); g["dbg"].append(dbg); g["dwbr"].append(dwbr)
    dpm, g["dlng"], g["dlnb"], g["dwm"], g["dsgb"], g["dcw"], g["dwp"], g["dps"] = _mixers_bwd(
        a["pm"], dbrs[1], dbrs[2], dbrs[3], w["ln_g"], w["ln_b"], w["wm"], w["wmt"], w["sgb"], w["conv_w"], w["wp"], w["pscale"], tw)
    after = _grads_after_attn(g)
    send, after_offsets = _pack_rows([after[n] for n in AFTER_ATTN], BF16, GRAD_ROW_PAD, lead=(N_DEV,))
    after_rows = send.shape[1]
    if pending is not None:
        send = jnp.concatenate([send, pending], axis=1)
    riding = _riding(_exchange_plan, send, _sds(send.shape, send.dtype))
    dq, dk, dv, landed = _flash_bwd(a["q"], a["k"], a["v"], a["o"], dbrs[0], a["lse"], ta, riding)
    dpa, g["dwq"], g["dwqs"], g["dwk"], g["dwv"], g["dqn"], g["dkvn"] = _attn_prep_bwd(
        a["pa"], dq, dk, dv, cosf, sins, w["qn"], w["kvn"], w["wq"], w["wqs"], w["wk"], w["wv"], tw)
    dx, g["dg1"], g["dwa"], g["dwm_in"] = _inproj_bwd(a["x"], w["g1"], dxmid, dpa, dpm, dpres, w["w_in"], w["wg"], tb)
    return dx, _grads_before_attn(g), landed, (after_rows, after_offsets)


def _step(p, x, positions, loss_target):
    s = x.shape[0]
    depth = p["w_in"][0].shape[0]
    tiles = dict(tok=min(TOK_TILE, s), attn=min(ATTN_TILE, s), inproj_bwd=min(INPROJ_BWD_TILE, s),
                 wgrad=min(WGRAD_TILE, s))

    small = {n: p[n][0] for n in REPLICATED}
    cosf, sins = _rope_tables(positions)

    pack_b, off_b, shapes_b = _pack_shards(p, depth, BEFORE_ATTN)
    pack_a, off_a, shapes_a = _pack_shards(p, depth, AFTER_ATTN)
    rows_a = pack_a.shape[1]
    before = _all_gather(pack_b[0], "weight_all_gather")
    weights, acts = [], []
    h = x
    for l in range(depth):
        w = _weights_before_attn(_unpack_layer_weights(BEFORE_ATTN, before, off_b, shapes_b), small, l)
        src = pack_a[l] if l + 1 == depth else jnp.concatenate([pack_a[l], pack_b[l + 1]], axis=0)
        riding = _riding(_gather_plan, src, _sds((N_DEV,) + src.shape, src.dtype))

        def after(landed):
            return _weights_after_attn(_unpack_layer_weights(AFTER_ATTN, landed[:, :rows_a], off_a, shapes_a))

        h, saved, w, landed = _layer_fwd(h, w, after, cosf, sins, tiles, riding)
        weights.append(w)
        acts.append(saved)
        before = landed[:, rows_a:] if l + 1 < depth else None
    dh, loss_blk = _loss_head(h, loss_target, tiles["tok"])
    loss = lax.psum(loss_blk[0, 0], ("x", "y", "c"))

    grads_b = [None] * depth
    sum_a, sum_b = [None] * depth, [None] * depth
    pending = None
    for l in reversed(range(depth)):
        dh, grads_b[l], landed, (rows_ga, off_ga) = _layer_bwd(dh, acts[l], weights[l], cosf, sins, tiles, pending)
        summed = _sum_slots(landed, "grad_shard_sum")
        sum_a[l] = summed[:rows_ga]
        if pending is not None:
            sum_b[l + 1] = summed[rows_ga:]
        pending, off_gb = _pack_rows([grads_b[l][n] for n in BEFORE_ATTN], BF16, GRAD_ROW_PAD, lead=(N_DEV,))
    sum_b[0] = _sum_slots(_all_to_all(pending, "grad_all_to_all"), "grad_shard_sum")
    g_shard = {}
    for names, sums, offsets in ((AFTER_ATTN, sum_a, off_ga), (BEFORE_ATTN, sum_b, off_gb)):
        block_shapes = [_owner_block_shape(n, p[n][0].shape) for n in names]
        for n, b in zip(names, _unpack_rows(jnp.stack(sums), offsets, block_shapes)):
            g_shard[n] = _natural_shard(n, b)
    small_grads = [jnp.stack([grads_b[l][n] for l in range(depth)]) for n in REPLICATED]
    small_rows, small_offsets = _pack_rows(small_grads, F32, ROW_PAD)
    g_small_rows = _sum_slots(_all_gather(small_rows, "replicated_grad_all_gather"), "grad_replicated_sum")

    grad, delta, new_m, new_v = {}, {}, {}, {}
    for n in SHARDED:
        w, m, v = p[n]
        grad[n] = g_shard[n]
        delta[n], new_m[n], new_v[n] = _adamw(w, grad[n], m, v, f"adamw_{n}")
    packs = [_pack_rows([p[n][i] for n in REPLICATED], F32, ROW_PAD)[0][None] for i in range(3)]
    d, nm, nv = (a[0] for a in _adamw(packs[0], g_small_rows[None], packs[1], packs[2], "adamw_replicated"))
    shapes = [p[n][0].shape for n in REPLICATED]
    for n, gg, dd, mm, vv in zip(REPLICATED, _unpack_rows(g_small_rows, small_offsets, shapes), _unpack_rows(d, small_offsets, shapes),
                                 _unpack_rows(nm, small_offsets, shapes), _unpack_rows(nv, small_offsets, shapes)):
        grad[n], delta[n], new_m[n], new_v[n] = gg, dd, mm, vv
    return loss, dh, grad, delta, new_m, new_v


WEIGHT_ORDER = ("norm_mix_pre", "w_in", "gate_b", "q_norm", "w_uq", "kv_norm", "w_ukv", "w_br_mla", "sg_ln_g", "sg_ln_b",
                "sg_w", "sg_b", "w_br_sg", "conv_w", "w_br_conv", "pool_w", "pool_scale", "w_br_pool", "w_out",
                "norm_mix_post", "norm_ffn_pre", "w_ff1", "w_ff2", "norm_ffn_post")


def kernel(x, positions, norm_mix_pre, w_in, gate_b, q_norm, w_uq, kv_norm, w_ukv, w_br_mla, sg_ln_g, sg_ln_b, sg_w, sg_b, w_br_sg, conv_w, w_br_conv, pool_w, pool_scale, w_br_pool, w_out, norm_mix_post, norm_ffn_pre, w_ff1, w_ff2, norm_ffn_post, loss_target, m_norm_mix_pre, m_w_in, m_gate_b, m_q_norm, m_w_uq, m_kv_norm, m_w_ukv, m_w_br_mla, m_sg_ln_g, m_sg_ln_b, m_sg_w, m_sg_b, m_w_br_sg, m_conv_w, m_w_br_conv, m_pool_w, m_pool_scale, m_w_br_pool, m_w_out, m_norm_mix_post, m_norm_ffn_pre, m_w_ff1, m_w_ff2, m_norm_ffn_post, v_norm_mix_pre, v_w_in, v_gate_b, v_q_norm, v_w_uq, v_kv_norm, v_w_ukv, v_w_br_mla, v_sg_ln_g, v_sg_ln_b, v_sg_w, v_sg_b, v_w_br_sg, v_conv_w, v_w_br_conv, v_pool_w, v_pool_scale, v_w_br_pool, v_w_out, v_norm_mix_post, v_norm_ffn_pre, v_w_ff1, v_w_ff2, v_norm_ffn_post):
    ws = (norm_mix_pre, w_in, gate_b, q_norm, w_uq, kv_norm, w_ukv, w_br_mla, sg_ln_g, sg_ln_b, sg_w, sg_b, w_br_sg, conv_w,
          w_br_conv, pool_w, pool_scale, w_br_pool, w_out, norm_mix_post, norm_ffn_pre, w_ff1, w_ff2, norm_ffn_post)
    ms = (m_norm_mix_pre, m_w_in, m_gate_b, m_q_norm, m_w_uq, m_kv_norm, m_w_ukv, m_w_br_mla, m_sg_ln_g, m_sg_ln_b, m_sg_w,
          m_sg_b, m_w_br_sg, m_conv_w, m_w_br_conv, m_pool_w, m_pool_scale, m_w_br_pool, m_w_out, m_norm_mix_post,
          m_norm_ffn_pre, m_w_ff1, m_w_ff2, m_norm_ffn_post)
    vs = (v_norm_mix_pre, v_w_in, v_gate_b, v_q_norm, v_w_uq, v_kv_norm, v_w_ukv, v_w_br_mla, v_sg_ln_g, v_sg_ln_b, v_sg_w,
          v_sg_b, v_w_br_sg, v_conv_w, v_w_br_conv, v_pool_w, v_pool_scale, v_w_br_pool, v_w_out, v_norm_mix_post,
          v_norm_ffn_pre, v_w_ff1, v_w_ff2, v_norm_ffn_post)
    p = {n: (w, m, v) for n, w, m, v in zip(WEIGHT_ORDER, ws, ms, vs)}
    loss, grad_x, grad, delta, new_m, new_v = _step(p, x[0], positions[0], loss_target[0])
    return (loss, grad_x[None], *[grad[n] for n in WEIGHT_ORDER], *[delta[n] for n in WEIGHT_ORDER],
            *[new_m[n] for n in WEIGHT_ORDER], *[new_v[n] for n in WEIGHT_ORDER])
```

```python
import functools
import math

import jax
import jax.numpy as jnp
from jax import lax
from jax.experimental import pallas as pl
from jax.experimental.pallas import tpu as pltpu

F32 = jnp.float32
BF16 = jnp.bfloat16

D_MODEL = 1024
N_HEADS = 4
D_NOPE = 64
D_ROPE = 32
D_V = 64
Q_RANK = 256
KV_RANK = 128
BR_W = 256
SG_CHUNK = 128
SG_GROUPS = 4
POOL_WINDOWS = (2, 4, 8, 16)
POOL_GROUP = 64
CONV_K = 3
D_FF = 4096
N_BRANCH = 4
N_IN = 6048
EPS = 1e-6
ROPE_BASE = 10000.0
ADAM_LR = 0.001
ADAM_B1 = 0.9
ADAM_B2 = 0.999
ADAM_EPS = 1e-08
ADAM_WD = 0.01
ADAM_STEP = 10

N_DEV = 8
LANES = 128
HEAD_PAD = 128
HALO = 16
PA_W = 640
PM_W = 6 * BR_W
VMEM_LIMIT = 56 * 1024 * 1024
SM_SCALE = (D_NOPE + D_ROPE) ** -0.5
TOK_TILE = 512
ATTN_TILE = 1024
WGRAD_TILE = 1024
INPROJ_BWD_TILE = 512

NN = (((1,), (0,)), ((), ()))
NT = (((1,), (1,)), ((), ()))
TN = (((0,), (0,)), ((), ()))
MESH_ID = pl.DeviceIdType.MESH


def _dot(a, b, dims=NN):
    return lax.dot_general(a, b, dims, preferred_element_type=F32)


def _pcall(body, *, name, grid, in_specs, out_specs, out_shape, scratch=(), vmem=None, prefetch=0):
    params = pltpu.CompilerParams(vmem_limit_bytes=vmem)
    if prefetch:
        spec = pltpu.PrefetchScalarGridSpec(num_scalar_prefetch=prefetch, grid=grid, in_specs=in_specs,
                                            out_specs=out_specs, scratch_shapes=scratch)
        return pl.pallas_call(body, name=name, grid_spec=spec, out_shape=out_shape, compiler_params=params)
    return pl.pallas_call(
        body, name=name, grid=grid, in_specs=in_specs, out_specs=out_specs, out_shape=out_shape,
        scratch_shapes=scratch, compiler_params=params)


def _rows(t, width):
    return pl.BlockSpec((t, width), lambda i: (i, 0))


def _whole(shape):
    nd = len(shape)
    return pl.BlockSpec(tuple(shape), lambda *_: (0,) * nd, pipeline_mode=pl.Buffered(1))


def _acc(shape):
    nd = len(shape)
    return pl.BlockSpec(tuple(shape), lambda *_: (0,) * nd)


def _sds(shape, dtype):
    return jax.ShapeDtypeStruct(tuple(shape), dtype)


def _rms(x, g):
    return x * lax.rsqrt(jnp.mean(x * x, axis=-1, keepdims=True) + EPS) * g


def _rms_bwd(x, g, dy):
    r = lax.rsqrt(jnp.mean(x * x, axis=-1, keepdims=True) + EPS)
    xh = x * r
    dg = jnp.sum(dy * xh, axis=0, keepdims=True)
    dxh = dy * g
    dx = r * (dxh - xh * jnp.mean(dxh * xh, axis=-1, keepdims=True))
    return dx, dg


def _sigmoid(x):
    return 1.0 / (1.0 + jnp.exp(-x))


def _gelu(x):
    return jax.nn.gelu(x, approximate=True)


def _accumulate(ref, val, first):
    @pl.when(first)
    def _():
        ref[...] = val

    @pl.when(jnp.logical_not(first))
    def _():
        ref[...] += val


def _accumulate_out(acc_ref, out_ref, val, first, last):
    _accumulate(acc_ref, val, first)

    @pl.when(last)
    def _():
        out_ref[...] = acc_ref[...].astype(out_ref.dtype)


def _inproj_fwd(x, g1, w_in, t):
    s = x.shape[0]

    def body(x_ref, g_ref, w_ref, hb_ref, pa_ref, pm_ref):
        hb = _rms(x_ref[...], g_ref[...]).astype(BF16)
        hb_ref[...] = hb
        pa_ref[...] = _dot(hb, w_ref[:PA_W, :], NT)
        pm_ref[...] = _dot(hb, w_ref[PA_W:, :], NT)

    return _pcall(
        body, name="inproj_fwd", grid=(s // t,),
        in_specs=[_rows(t, D_MODEL), _whole((1, D_MODEL)), _whole(w_in.shape)],
        out_specs=[_rows(t, D_MODEL), _rows(t, PA_W), _rows(t, PM_W)],
        out_shape=[_sds((s, D_MODEL), BF16), _sds((s, PA_W), F32), _sds((s, PM_W), F32)],
        vmem=VMEM_LIMIT)(x, g1, w_in)


def _attn_prep_fwd(pa, cosf, sins, qn, kvn, wq, wqs, wk, wv, t):
    s = pa.shape[0]

    def body(pa_ref, cos_ref, sin_ref, qn_ref, kvn_ref, wq_ref, wqs_ref, wk_ref, wv_ref, q_ref, k_ref, v_ref):
        cosv, sinv = cos_ref[...], sin_ref[...]
        cqn = _rms(pa_ref[:, 0:Q_RANK], qn_ref[...]).astype(BF16)
        ckvn = _rms(pa_ref[:, Q_RANK:Q_RANK + KV_RANK], kvn_ref[...]).astype(BF16)
        k_rope = pa_ref[:, 384:512] * cosv + pa_ref[:, 512:640] * sinv
        q_all, qs_all = _dot(cqn, wq_ref[...], NT), _dot(cqn, wqs_ref[...], NT)
        k_all = _dot(ckvn, wk_ref[...], NT)
        for h in range(N_HEADS):
            lanes = slice(h * HEAD_PAD, (h + 1) * HEAD_PAD)
            q_ref[h] = ((q_all[:, lanes] * cosv + qs_all[:, lanes] * sinv) * SM_SCALE).astype(BF16)
            k_ref[h] = (k_all[:, lanes] + k_rope).astype(BF16)
        v_ref[...] = _dot(ckvn, wv_ref[...], NT).astype(BF16)

    head_blk = pl.BlockSpec((N_HEADS, t, HEAD_PAD), lambda i: (0, i, 0))
    return _pcall(
        body, name="attn_prep_fwd", grid=(s // t,),
        in_specs=[_rows(t, PA_W), _rows(t, LANES), _rows(t, LANES), _whole(qn.shape), _whole(kvn.shape),
                  _whole(wq.shape), _whole(wqs.shape), _whole(wk.shape), _whole(wv.shape)],
        out_specs=[head_blk, head_blk, _rows(t, BR_W)],
        out_shape=[_sds((N_HEADS, s, HEAD_PAD), BF16), _sds((N_HEADS, s, HEAD_PAD), BF16), _sds((s, BR_W), BF16)],
        vmem=VMEM_LIMIT)(pa, cosf, sins, qn, kvn, wq, wqs, wk, wv)


def _causal_scores(q, k, masked):
    sc = _dot(q, k, NT)
    if masked:
        row = lax.broadcasted_iota(jnp.int32, sc.shape, 0)
        col = lax.broadcasted_iota(jnp.int32, sc.shape, 1)
        sc = jnp.where(col <= row, sc, -jnp.inf)
    return sc


def _diagonal_parts(t):
    half = t // 2
    if half % LANES:
        return [(True, slice(0, t), slice(0, t))]
    lower, upper = slice(0, half), slice(half, t)
    return [(True, lower, lower), (False, upper, lower), (True, upper, upper)]


def _causal_steps(n, key_major):
    if key_major:
        pairs = [(qi, kj) for kj in range(n) for qi in range(kj, n)]
    else:
        pairs = [(qi, kj) for qi in range(n) for kj in range(qi + 1)]
    return (jnp.asarray([p[0] for p in pairs], jnp.int32), jnp.asarray([p[1] for p in pairs], jnp.int32))


def _flash_fwd(q, k, v, t, riding):
    s = v.shape[0]
    n = s // t
    qi_tab, kj_tab = _causal_steps(n, key_major=False)
    n_steps = int(qi_tab.shape[0])
    whole = slice(0, t)

    def body(qi_ref, kj_ref, q_ref, k_ref, v_ref, x_ref, o_ref, lse_ref, land_ref, m_scr, l_scr, acc_scr,
             send_sems, recv_sems, local_sem):
        step_id = pl.program_id(0)
        qi, kj = qi_ref[step_id], kj_ref[step_id]
        at_end = _ride_along(riding, step_id, n_steps, (x_ref, land_ref, send_sems, recv_sems, local_sem))

        @pl.when(kj == 0)
        def _():
            m_scr[...] = jnp.full(m_scr.shape, -jnp.inf, F32)
            l_scr[...] = jnp.zeros(l_scr.shape, F32)
            acc_scr[...] = jnp.zeros(acc_scr.shape, F32)

        def step(masked, qr, kr):
            for h in range(N_HEADS):
                sc = _causal_scores(q_ref[h, qr, :], k_ref[h, kr, :], masked)
                m_prev = m_scr[h, qr, :]
                m_next = jnp.maximum(m_prev, jnp.max(sc, axis=1, keepdims=True))
                alpha = jnp.exp(m_prev - m_next)
                p = jnp.exp(sc - jnp.tile(m_next, (1, sc.shape[1] // LANES)))
                l_scr[h, qr, :] = alpha * l_scr[h, qr, :] + jnp.sum(p, axis=1, keepdims=True)
                m_scr[h, qr, :] = m_next
                pair = slice((h // 2) * LANES, (h // 2 + 1) * LANES)
                acc_scr[h, qr, :] = acc_scr[h, qr, :] * alpha + _dot(p.astype(BF16), v_ref[kr, pair])

        @pl.when(kj < qi)
        def _():
            step(False, whole, whole)

        @pl.when(kj == qi)
        def _():
            for masked, qr, kr in _diagonal_parts(t):
                step(masked, qr, kr)
            lane = lax.broadcasted_iota(jnp.int32, (t, LANES), 1)
            for pr in range(N_HEADS // 2):
                o0 = acc_scr[2 * pr] / l_scr[2 * pr]
                o1 = acc_scr[2 * pr + 1] / l_scr[2 * pr + 1]
                o_ref[:, pr * LANES:(pr + 1) * LANES] = jnp.where(lane < D_V, o0, o1).astype(BF16)
            for h in range(N_HEADS):
                lse_ref[h] = m_scr[h] + jnp.log(l_scr[h])

        at_end()

    q_blk = pl.BlockSpec((N_HEADS, t, HEAD_PAD), lambda i, qi, kj: (0, qi[i], 0))
    k_blk = pl.BlockSpec((N_HEADS, t, HEAD_PAD), lambda i, qi, kj: (0, kj[i], 0))
    v_blk = pl.BlockSpec((t, BR_W), lambda i, qi, kj: (kj[i], 0))
    return _pcall(
        body, name="flash_fwd_gathering", grid=(n_steps,), prefetch=2,
        in_specs=[q_blk, k_blk, v_blk, HBM_REF],
        out_specs=[pl.BlockSpec((t, BR_W), lambda i, qi, kj: (qi[i], 0)), q_blk, HBM_REF],
        out_shape=[_sds((s, BR_W), BF16), _sds((N_HEADS, s, LANES), F32), riding[2]],
        scratch=[pltpu.VMEM((N_HEADS, t, LANES), F32)] * 3 + list(EXCHANGE_SEMS),
        vmem=VMEM_LIMIT)(qi_tab, kj_tab, q, k, v, riding[1])


def _shift_down(ext, k, t):
    return pltpu.roll(ext, k, 0)[HALO:HALO + t]


def _shift_up(ext, k, t):
    return pltpu.roll(ext, t + HALO - k, 0)[0:t]


def _lane_group(shape):
    return lax.shift_right_logical(lax.broadcasted_iota(jnp.int32, shape, 1), 6)


def _group_select(vals):
    grp = _lane_group(vals[0].shape)
    out = vals[0]
    for g in range(1, len(vals)):
        out = jnp.where(grp == g, vals[g], out)
    return out


def _layernorm(x, g, b):
    mu = jnp.mean(x, axis=-1, keepdims=True)
    xc = x - mu
    return xc * lax.rsqrt(jnp.mean(xc * xc, axis=-1, keepdims=True) + EPS) * g + b


def _sg_mix(wm_ref, vnb, bias):
    return _group_select([_dot(wm_ref[g], vnb) for g in range(SG_GROUPS)]) + bias


def _pool_windows(ext):
    s2 = ext + pltpu.roll(ext, 1, 0)
    s4 = s2 + pltpu.roll(s2, 2, 0)
    s8 = s4 + pltpu.roll(s4, 4, 0)
    s16 = s8 + pltpu.roll(s8, 8, 0)
    return [s2, s4, s8, s16]


def _pool_counts(tok):
    return [jnp.minimum(tok + 1, w).astype(F32) for w in POOL_WINDOWS]


def _halo_before(t):
    return pl.BlockSpec((HALO, PM_W), lambda i: (jnp.maximum(i * (t // HALO) - 1, 0), 0))


def _mixers_fwd(pm, ln_g, ln_b, wm, sgb, conv_w, wp, pscale, t):
    s = pm.shape[0]

    def body(pm_ref, halo_ref, lng_ref, lnb_ref, wm_ref, sgb_ref, cw_ref, wp_ref, ps_ref, bsg_ref, bcv_ref, bpl_ref):
        i = pl.program_id(0)
        halo = jnp.where(i > 0, halo_ref[...], 0.0)
        u = _gelu(pm_ref[:, 0:256])
        vnb = _layernorm(_gelu(pm_ref[:, 256:512]), lng_ref[...], lnb_ref[...]).astype(BF16)
        for c in range(t // SG_CHUNK):
            rows = slice(c * SG_CHUNK, (c + 1) * SG_CHUNK)
            bsg_ref[rows, :] = (u[rows] * _sg_mix(wm_ref, vnb[rows], sgb_ref[...])).astype(BF16)
        z = pm_ref[:, 1024:1280] * pm_ref[:, 512:768]
        zext = jnp.concatenate([halo[:, 1024:1280] * halo[:, 512:768], z], axis=0)
        y = cw_ref[0:1, :] * _shift_down(zext, 2, t) + cw_ref[1:2, :] * _shift_down(zext, 1, t) + cw_ref[2:3, :] * z
        bcv_ref[...] = (pm_ref[:, 768:1024] * y).astype(BF16)
        p = pm_ref[:, 1280:1536]
        sums = _pool_windows(jnp.concatenate([halo[:, 1280:1536], p], axis=0))
        tok = i * t + lax.broadcasted_iota(jnp.int32, (t, 1), 0)
        pooled = _group_select([sw[HALO:HALO + t] / cnt - p for sw, cnt in zip(sums, _pool_counts(tok))])
        bpl_ref[...] = (_dot(pooled.astype(BF16), wp_ref[...]) * ps_ref[...]).astype(BF16)

    return _pcall(
        body, name="mixers_fwd", grid=(s // t,),
        in_specs=[_rows(t, PM_W), _halo_before(t), _whole(ln_g.shape), _whole(ln_b.shape), _whole(wm.shape),
                  _whole(sgb.shape), _whole(conv_w.shape), _whole(wp.shape), _whole(pscale.shape)],
        out_specs=[_rows(t, BR_W)] * 3,
        out_shape=[_sds((s, BR_W), BF16)] * 3,
        vmem=VMEM_LIMIT)(pm, pm, ln_g, ln_b, wm, sgb, conv_w, wp, pscale)


def _merge_fwd(x, hb, branches, wg, bg, wbr, wout, g2, t):
    s = x.shape[0]

    def body(x_ref, hb_ref, b0_ref, b1_ref, b2_ref, b3_ref, wg_ref, bg_ref, wbr_ref, wout_ref, g2_ref,
             xmid_ref, mrg_ref, t_ref):
        hb = hb_ref[...]
        merged = jnp.zeros((t, D_MODEL), F32)
        for k, b_ref in enumerate((b0_ref, b1_ref, b2_ref, b3_ref)):
            cols = slice(k * D_MODEL, (k + 1) * D_MODEL)
            gate = _sigmoid(_dot(hb, wg_ref[cols, :], NT) + bg_ref[:, cols])
            merged = merged + gate * _dot(b_ref[...], wbr_ref[k], NT)
        mb = merged.astype(BF16)
        mrg_ref[...] = mb
        tt = _dot(mb, wout_ref[...])
        t_ref[...] = tt
        xmid_ref[...] = x_ref[...] + _rms(tt, g2_ref[...])

    return _pcall(
        body, name="merge_fwd", grid=(s // t,),
        in_specs=[_rows(t, D_MODEL), _rows(t, D_MODEL)] + [_rows(t, BR_W)] * 4 +
                 [_whole(wg.shape), _whole(bg.shape), _whole(wbr.shape), _whole(wout.shape), _whole(g2.shape)],
        out_specs=[_rows(t, D_MODEL)] * 3,
        out_shape=[_sds((s, D_MODEL), F32), _sds((s, D_MODEL), BF16), _sds((s, D_MODEL), F32)],
        vmem=VMEM_LIMIT)(x, hb, *branches, wg, bg, wbr, wout, g2)


def _ffn_fwd(x, g3, w1, w2, g4, t):
    s = x.shape[0]

    def body(x_ref, g3_ref, w1_ref, w2_ref, g4_ref, xo_ref, f_ref):
        h = _rms(x_ref[...], g3_ref[...]).astype(BF16)
        f = jnp.zeros((t, D_MODEL), F32)
        for j in range(D_FF // D_MODEL):
            cols = slice(j * D_MODEL, (j + 1) * D_MODEL)
            r = jnp.square(jnp.maximum(_dot(h, w1_ref[cols, :], NT), 0.0)).astype(BF16)
            f = f + _dot(r, w2_ref[cols, :])
        f_ref[...] = f
        xo_ref[...] = x_ref[...] + _rms(f, g4_ref[...])

    return _pcall(
        body, name="ffn_fwd", grid=(s // t,),
        in_specs=[_rows(t, D_MODEL), _whole(g3.shape), _whole(w1.shape), _whole(w2.shape), _whole(g4.shape)],
        out_specs=[_rows(t, D_MODEL)] * 2,
        out_shape=[_sds((s, D_MODEL), F32)] * 2,
        vmem=VMEM_LIMIT)(x, g3, w1, w2, g4)


def _loss_head(y, target, t):
    s = y.shape[0]
    n = s // t

    def body(y_ref, tg_ref, dy_ref, loss_ref, acc_scr):
        i = pl.program_id(0)
        d = y_ref[...] - tg_ref[...]
        dy_ref[...] = d * (1.0 / D_MODEL)
        _accumulate(acc_scr, jnp.sum(d * d, axis=0, keepdims=True), i == 0)

        @pl.when(i == n - 1)
        def _():
            loss_ref[...] = jnp.full(loss_ref.shape, 0.5 / D_MODEL, F32) * jnp.sum(acc_scr[...])

    return _pcall(
        body, name="loss_head", grid=(n,),
        in_specs=[_rows(t, D_MODEL)] * 2,
        out_specs=[_rows(t, D_MODEL), _acc((8, LANES))],
        out_shape=[_sds((s, D_MODEL), F32), _sds((8, LANES), F32)],
        scratch=[pltpu.VMEM((1, D_MODEL), F32)])(y, target)


def _ffn_bwd_norms(x_mid, f, dxo, g3, g4, t):
    s = x_mid.shape[0]

    def body(x_ref, f_ref, dxo_ref, g3_ref, g4_ref, h2_ref, df_ref, dg4_ref):
        h2_ref[...] = _rms(x_ref[...], g3_ref[...]).astype(BF16)
        df, dg4 = _rms_bwd(f_ref[...], g4_ref[...], dxo_ref[...])
        df_ref[...] = df.astype(BF16)
        _accumulate(dg4_ref, dg4, pl.program_id(0) == 0)

    return _pcall(
        body, name="ffn_bwd_norms", grid=(s // t,),
        in_specs=[_rows(t, D_MODEL)] * 3 + [_whole(g3.shape), _whole(g4.shape)],
        out_specs=[_rows(t, D_MODEL), _rows(t, D_MODEL), _acc((1, D_MODEL))],
        out_shape=[_sds((s, D_MODEL), BF16), _sds((s, D_MODEL), BF16), _sds((1, D_MODEL), F32)])(x_mid, f, dxo, g3, g4)


def _ffn_bwd_weights(h2, df, w1, w2, t):
    s = h2.shape[0]
    blk = D_MODEL
    n = s // t

    def body(h2_ref, df_ref, w1_ref, w2_ref, da_ref, dw1_ref, dw2_ref, acc1, acc2):
        first, last = pl.program_id(1) == 0, pl.program_id(1) == n - 1
        h2v, dfv = h2_ref[...], df_ref[...]
        rl = jnp.maximum(_dot(h2v, w1_ref[...], NT), 0.0)
        _accumulate_out(acc2, dw2_ref, _dot((rl * rl).astype(BF16), dfv, TN), first, last)
        da = (_dot(dfv, w2_ref[...], NT) * (2.0 * rl)).astype(BF16)
        da_ref[...] = da
        _accumulate_out(acc1, dw1_ref, _dot(da, h2v, TN), first, last)

    tok = pl.BlockSpec((t, D_MODEL), lambda j, i: (i, 0))
    hid = pl.BlockSpec((blk, D_MODEL), lambda j, i: (j, 0))
    return _pcall(
        body, name="ffn_bwd_weights", grid=(D_FF // blk, n),
        in_specs=[tok, tok, hid, hid],
        out_specs=[pl.BlockSpec((t, blk), lambda j, i: (i, j)), hid, hid],
        out_shape=[_sds((s, D_FF), BF16), _sds((D_FF, D_MODEL), BF16), _sds((D_FF, D_MODEL), BF16)],
        scratch=[pltpu.VMEM((blk, D_MODEL), F32)] * 2,
        vmem=VMEM_LIMIT)(h2, df, w1, w2)


def _ffn_bwd_input(da, w1, x_mid, dxo, g3, t):
    s = x_mid.shape[0]

    def body(da_ref, w1_ref, x_ref, dxo_ref, g3_ref, dx_ref, dg3_ref):
        dx, dg3 = _rms_bwd(x_ref[...], g3_ref[...], _dot(da_ref[...], w1_ref[...]))
        dx_ref[...] = dxo_ref[...] + dx
        _accumulate(dg3_ref, dg3, pl.program_id(0) == 0)

    return _pcall(
        body, name="ffn_bwd_input", grid=(s // t,),
        in_specs=[_rows(t, D_FF), _whole(w1.shape), _rows(t, D_MODEL), _rows(t, D_MODEL), _whole(g3.shape)],
        out_specs=[_rows(t, D_MODEL), _acc((1, D_MODEL))],
        out_shape=[_sds((s, D_MODEL), F32), _sds((1, D_MODEL), F32)],
        vmem=VMEM_LIMIT)(da, w1, x_mid, dxo, g3)


def _merge_bwd_out(tt, dxmid, g2, wout, merged, t):
    s = tt.shape[0]
    n = s // t

    def body(t_ref, dx_ref, g2_ref, wout_ref, mrg_ref, dm_ref, dwout_ref, dg2_ref, acc):
        first, last = pl.program_id(0) == 0, pl.program_id(0) == n - 1
        dt, dg2 = _rms_bwd(t_ref[...], g2_ref[...], dx_ref[...])
        dtb = dt.astype(BF16)
        dm_ref[...] = _dot(dtb, wout_ref[...], NT)
        _accumulate_out(acc, dwout_ref, _dot(mrg_ref[...], dtb, TN), first, last)
        _accumulate(dg2_ref, dg2, first)

    return _pcall(
        body, name="merge_bwd_out", grid=(n,),
        in_specs=[_rows(t, D_MODEL), _rows(t, D_MODEL), _whole(g2.shape), _whole(wout.shape), _rows(t, D_MODEL)],
        out_specs=[_rows(t, D_MODEL), _acc((D_MODEL, D_MODEL)), _acc((1, D_MODEL))],
        out_shape=[_sds((s, D_MODEL), F32), _sds((D_MODEL, D_MODEL), BF16), _sds((1, D_MODEL), F32)],
        scratch=[pltpu.VMEM((D_MODEL, D_MODEL), F32)],
        vmem=VMEM_LIMIT)(tt, dxmid, g2, wout, merged)


def _merge_bwd_branch(k, hb, br, dm, wg, bg, wbr, t):
    s = hb.shape[0]
    n = s // t

    def body(hb_ref, br_ref, dm_ref, wg_ref, bg_ref, wbr_ref, dbr_ref, dpre_ref, dwg_ref, dbg_ref, dwbr_ref,
             acc_g, acc_br):
        first, last = pl.program_id(0) == 0, pl.program_id(0) == n - 1
        hbv, brv, dmv, wbrv = hb_ref[...], br_ref[...], dm_ref[...], wbr_ref[0]
        gate = _sigmoid(_dot(hbv, wg_ref[...], NT) + bg_ref[...])
        dy = (dmv * gate).astype(BF16)
        dpre = dmv * _dot(brv, wbrv, NT) * gate * (1.0 - gate)
        dpb = dpre.astype(BF16)
        dpre_ref[...] = dpb
        dbr_ref[...] = _dot(dy, wbrv)
        _accumulate_out(acc_br, dwbr_ref, _dot(dy, brv, TN), first, last)
        _accumulate_out(acc_g, dwg_ref, _dot(dpb, hbv, TN), first, last)
        _accumulate(dbg_ref, jnp.sum(dpre, axis=0, keepdims=True), first)

    return _pcall(
        body, name=f"merge_bwd_branch{k}", grid=(n,),
        in_specs=[_rows(t, D_MODEL), _rows(t, BR_W), _rows(t, D_MODEL),
                  pl.BlockSpec((D_MODEL, D_MODEL), lambda i: (k, 0)), pl.BlockSpec((1, D_MODEL), lambda i: (0, k)),
                  pl.BlockSpec((1, D_MODEL, BR_W), lambda i: (k, 0, 0))],
        out_specs=[_rows(t, BR_W), _rows(t, D_MODEL), _acc((D_MODEL, D_MODEL)), _acc((1, D_MODEL)), _acc((D_MODEL, BR_W))],
        out_shape=[_sds((s, BR_W), F32), _sds((s, D_MODEL), BF16), _sds((D_MODEL, D_MODEL), BF16),
                   _sds((1, D_MODEL), F32), _sds((D_MODEL, BR_W), BF16)],
        scratch=[pltpu.VMEM((D_MODEL, D_MODEL), F32), pltpu.VMEM((D_MODEL, BR_W), F32)],
        vmem=VMEM_LIMIT)(hb, br, dm, wg, bg, wbr)


def _mixers_bwd(pm, dbsg, dbcv, dbpl, ln_g, ln_b, wm, wmt, sgb, conv_w, wp, pscale, t):
    s = pm.shape[0]
    n = s // t
    nb = t // HALO

    def body(pm_ref, before_ref, after_ref, dsg_ref, dcv_ref, dcva_ref, dpl_ref, dpla_ref, lng_ref, lnb_ref,
             wm_ref, wmt_ref, sgb_ref, cw_ref, wp_ref, ps_ref,
             dpm_ref, dlng_ref, dlnb_ref, dwm_ref, dsgb_ref, dcw_ref, dwp_ref, dps_ref):
        i = pl.program_id(0)
        first = i == 0
        before = jnp.where(i > 0, before_ref[...], 0.0)
        after = jnp.where(i < n - 1, after_ref[...], 0.0)

        u_raw, v_raw = pm_ref[:, 0:256], pm_ref[:, 256:512]
        lng, lnb = lng_ref[...], lnb_ref[...]
        u, gelu_u_vjp = jax.vjp(_gelu, u_raw)
        vn, norm_vjp = jax.vjp(lambda v_, g_, b_: _layernorm(_gelu(v_), g_, b_), v_raw, lng, lnb)
        vnb = vn.astype(BF16)
        dsg = dsg_ref[...]
        grp = _lane_group((SG_CHUNK, BR_W))
        tri = (lax.broadcasted_iota(jnp.int32, (SG_CHUNK, SG_CHUNK), 1)
               <= lax.broadcasted_iota(jnp.int32, (SG_CHUNK, SG_CHUNK), 0))
        du_parts, dvn_parts = [], []
        dsgb = jnp.zeros((SG_CHUNK, BR_W), F32)
        dwm = [jnp.zeros((SG_CHUNK, SG_CHUNK), F32) for _ in range(SG_GROUPS)]
        for c in range(t // SG_CHUNK):
            rows = slice(c * SG_CHUNK, (c + 1) * SG_CHUNK)
            mix = _sg_mix(wm_ref, vnb[rows], sgb_ref[...])
            du_parts.append(dsg[rows] * mix)
            ds = dsg[rows] * u[rows]
            dsgb = dsgb + ds
            dsb = [jnp.where(grp == g, ds, 0.0).astype(BF16) for g in range(SG_GROUPS)]
            for g in range(SG_GROUPS):
                dwm[g] = dwm[g] + _dot(dsb[g], vnb[rows], NT)
            dvn_parts.append(_group_select([_dot(wmt_ref[g], dsb[g]) for g in range(SG_GROUPS)]))
        (du_raw,) = gelu_u_vjp(jnp.concatenate(du_parts, axis=0))
        dv_raw, dlng, dlnb = norm_vjp(jnp.concatenate(dvn_parts, axis=0))
        dpm_ref[:, 0:256] = du_raw.astype(BF16)
        dpm_ref[:, 256:512] = dv_raw.astype(BF16)
        _accumulate(dlng_ref, dlng, first)
        _accumulate(dlnb_ref, dlnb, first)
        _accumulate(dsgb_ref, dsgb, first)
        for g in range(SG_GROUPS):
            _accumulate(dwm_ref.at[g], jnp.where(tri, dwm[g], 0.0), first)

        xin, bg, cg = pm_ref[:, 512:768], pm_ref[:, 768:1024], pm_ref[:, 1024:1280]
        z = cg * xin
        zext = jnp.concatenate([before[:, 1024:1280] * before[:, 512:768], z], axis=0)
        z1, z2 = _shift_down(zext, 1, t), _shift_down(zext, 2, t)
        w0, w1, w2 = cw_ref[0:1, :], cw_ref[1:2, :], cw_ref[2:3, :]
        dcv = dcv_ref[...]
        y = w0 * z2 + w1 * z1 + w2 * z
        dy = dcv * bg
        dyext = jnp.concatenate([dy, jnp.where(i < n - 1, dcva_ref[...], 0.0) * after[:, 768:1024]], axis=0)
        dz = w2 * dy + w1 * _shift_up(dyext, 1, t) + w0 * _shift_up(dyext, 2, t)
        dpm_ref[:, 512:768] = (dz * cg).astype(BF16)
        dpm_ref[:, 768:1024] = (dcv * y).astype(BF16)
        dpm_ref[:, 1024:1280] = (dz * xin).astype(BF16)
        dcw = jnp.concatenate([jnp.sum(dy * z2, axis=0, keepdims=True), jnp.sum(dy * z1, axis=0, keepdims=True),
                               jnp.sum(dy * z, axis=0, keepdims=True)], axis=0)
        _accumulate(dcw_ref, jnp.concatenate([dcw, jnp.zeros((8 - CONV_K, BR_W), F32)], axis=0), first)

        p = pm_ref[:, 1280:1536]
        tok = i * t + lax.broadcasted_iota(jnp.int32, (t, 1), 0)
        sums = _pool_windows(jnp.concatenate([before[:, 1280:1536], p], axis=0))
        pooled = _group_select([sw[HALO:HALO + t] / cnt - p for sw, cnt in zip(sums, _pool_counts(tok))]).astype(BF16)
        mixed = _dot(pooled, wp_ref[...])
        dpl = dpl_ref[...]
        ps = ps_ref[...]
        dmix = (dpl * ps).astype(BF16)
        _accumulate(dps_ref, jnp.sum(dpl * mixed, axis=0, keepdims=True), first)
        _accumulate(dwp_ref, _dot(pooled, dmix, TN), first)
        dmix_after = (jnp.where(i < n - 1, dpla_ref[...], 0.0) * ps).astype(BF16)
        dpo = _dot(dmix, wp_ref[...], NT)
        dpo_ext = jnp.concatenate([dpo, _dot(dmix_after, wp_ref[...], NT)], axis=0)
        tok_ext = i * t + lax.broadcasted_iota(jnp.int32, (t + HALO, 1), 0)
        dp_groups = []
        for g, (win, cnt) in enumerate(zip(POOL_WINDOWS, _pool_counts(tok_ext))):
            e = dpo_ext / cnt
            acc = e
            span = 1
            while span < win:
                acc = acc + pltpu.roll(acc, t + HALO - span, 0)
                span *= 2
            dp_groups.append(acc[0:t] - dpo)
        dpm_ref[:, 1280:1536] = _group_select(dp_groups).astype(BF16)

    row_blk = lambda w: pl.BlockSpec((t, w), lambda i: (i, 0))
    after_blk = lambda w: pl.BlockSpec((HALO, w), lambda i: (jnp.minimum((i + 1) * nb, n * nb - 1), 0))
    return _pcall(
        body, name="mixers_bwd", grid=(n,),
        in_specs=[row_blk(PM_W), _halo_before(t), after_blk(PM_W), row_blk(BR_W), row_blk(BR_W), after_blk(BR_W),
                  row_blk(BR_W), after_blk(BR_W), _whole(ln_g.shape), _whole(ln_b.shape), _whole(wm.shape),
                  _whole(wmt.shape), _whole(sgb.shape), _whole(conv_w.shape), _whole(wp.shape), _whole(pscale.shape)],
        out_specs=[row_blk(PM_W), _acc((1, BR_W)), _acc((1, BR_W)), _acc((SG_GROUPS, SG_CHUNK, SG_CHUNK)),
                   _acc((SG_CHUNK, BR_W)), _acc((8, BR_W)), _acc((BR_W, BR_W)), _acc((1, BR_W))],
        out_shape=[_sds((s, PM_W), BF16), _sds((1, BR_W), F32), _sds((1, BR_W), F32),
                   _sds((SG_GROUPS, SG_CHUNK, SG_CHUNK), F32), _sds((SG_CHUNK, BR_W), F32), _sds((8, BR_W), F32),
                   _sds((BR_W, BR_W), F32), _sds((1, BR_W), F32)],
        vmem=VMEM_LIMIT)(pm, pm, pm, dbsg, dbcv, dbcv, dbpl, dbpl, ln_g, ln_b, wm, wmt, sgb, conv_w, wp, pscale)


def _head_delta(o, do):
    rows = o.shape[0]
    prod = o.astype(F32) * do
    lane = lax.broadcasted_iota(jnp.int32, (rows, LANES), 1)
    out = []
    for h in range(N_HEADS):
        pair = prod[:, (h // 2) * LANES:(h // 2 + 1) * LANES]
        mine = (lane < D_V) if h % 2 == 0 else (lane >= D_V)
        out.append(jnp.broadcast_to(jnp.sum(jnp.where(mine, pair, 0.0), axis=1, keepdims=True), (rows, LANES)))
    return out


def _head_do(do, h):
    lane = lax.broadcasted_iota(jnp.int32, (do.shape[0], LANES), 1)
    mine = (lane < D_V) if h % 2 == 0 else (lane >= D_V)
    pair = do[:, (h // 2) * LANES:(h // 2 + 1) * LANES]
    return jnp.where(mine, pair, 0.0).astype(BF16)


def _flash_bwd(q, k, v, o, do, lse, t, riding):
    s = v.shape[0]
    n = s // t
    qi_tab, kj_tab = _causal_steps(n, key_major=True)
    n_steps = int(qi_tab.shape[0])
    whole = slice(0, t)

    def body(qi_ref, kj_ref, q_ref, k_ref, v_ref, o_ref, do_ref, lse_ref, x_ref, dq_ref, dk_ref, dv_ref, land_ref,
             dv_scr, send_sems, recv_sems, local_sem):
        step_id = pl.program_id(0)
        qi, kj = qi_ref[step_id], kj_ref[step_id]
        at_end = _ride_along(riding, step_id, n_steps, (x_ref, land_ref, send_sems, recv_sems, local_sem))

        @pl.when(step_id == 0)
        def _():
            dq_ref[...] = jnp.zeros(dq_ref.shape, F32)

        @pl.when(qi == kj)
        def _():
            dk_ref[...] = jnp.zeros(dk_ref.shape, F32)
            dv_scr[...] = jnp.zeros(dv_scr.shape, F32)

        def step(masked, qr, kr):
            dov = do_ref[qr, :]
            delta = _head_delta(o_ref[qr, :], dov)
            reps = (kr.stop - kr.start) // LANES
            q_rows = pl.ds(pl.multiple_of(qi * t + qr.start, t // 2), qr.stop - qr.start)
            for h in range(N_HEADS):
                qh, kh = q_ref[h, qr, :], k_ref[h, kr, :]
                p = jnp.exp(_causal_scores(qh, kh, masked) - jnp.tile(lse_ref[h, qr, :], (1, reps)))
                pair = slice((h // 2) * LANES, (h // 2 + 1) * LANES)
                dv_scr[h, kr, :] += _dot(p.astype(BF16), dov[:, pair].astype(BF16), TN)
                dp = _dot(_head_do(dov, h), v_ref[kr, pair], NT)
                ds = (p * (dp - jnp.tile(delta[h], (1, reps)))).astype(BF16)
                dk_ref[h, kr, :] += _dot(ds, qh, TN)
                dq_ref[h, q_rows, :] += _dot(ds, kh)

        @pl.when(qi > kj)
        def _():
            step(False, whole, whole)

        @pl.when(qi == kj)
        def _():
            for masked, qr, kr in _diagonal_parts(t):
                step(masked, qr, kr)

        @pl.when(qi == n - 1)
        def _():
            lane = lax.broadcasted_iota(jnp.int32, (t, LANES), 1)
            for pr in range(N_HEADS // 2):
                dv_ref[:, pr * LANES:(pr + 1) * LANES] = jnp.where(lane < D_V, dv_scr[2 * pr], dv_scr[2 * pr + 1]).astype(BF16)

        at_end()

    q_blk = pl.BlockSpec((N_HEADS, t, HEAD_PAD), lambda i, qi, kj: (0, qi[i], 0))
    k_blk = pl.BlockSpec((N_HEADS, t, HEAD_PAD), lambda i, qi, kj: (0, kj[i], 0))
    v_blk = pl.BlockSpec((t, BR_W), lambda i, qi, kj: (kj[i], 0))
    o_blk = pl.BlockSpec((t, BR_W), lambda i, qi, kj: (qi[i], 0))
    dq_sds = _sds((N_HEADS, s, HEAD_PAD), F32)
    return _pcall(
        body, name="flash_bwd_exchanging", grid=(n_steps,), prefetch=2,
        in_specs=[q_blk, k_blk, v_blk, o_blk, o_blk, q_blk, HBM_REF],
        out_specs=[_whole(dq_sds.shape), k_blk, v_blk, HBM_REF],
        out_shape=[dq_sds, dq_sds, _sds((s, BR_W), BF16), riding[2]],
        scratch=[pltpu.VMEM((N_HEADS, t, LANES), F32)] + list(EXCHANGE_SEMS),
        vmem=VMEM_LIMIT)(qi_tab, kj_tab, q, k, v, o, do, lse, riding[1])


def _attn_prep_bwd(pa, dq, dk, dv, cosf, sins, qn, kvn, wq, wqs, wk, wv, t):
    s = pa.shape[0]

    def body(pa_ref, dq_ref, dk_ref, dv_ref, cos_ref, sin_ref, qn_ref, kvn_ref, wq_ref, wqs_ref, wk_ref, wv_ref,
             dpa_ref, dwq_ref, dwqs_ref, dwk_ref, dwv_ref, dqn_ref, dkvn_ref):
        first = pl.program_id(0) == 0
        cosv, sinv = cos_ref[...], sin_ref[...]
        cq, ckv = pa_ref[:, 0:Q_RANK], pa_ref[:, Q_RANK:Q_RANK + KV_RANK]
        cqn = _rms(cq, qn_ref[...]).astype(BF16)
        ckvn = _rms(ckv, kvn_ref[...]).astype(BF16)
        dvv = dv_ref[...]
        _accumulate(dwv_ref, _dot(dvv, ckvn, TN), first)
        dk_rope = dk_ref[0]
        for h in range(1, N_HEADS):
            dk_rope = dk_rope + dk_ref[h]
        cos_s, sin_s = cosv * SM_SCALE, sinv * SM_SCALE
        dqa = jnp.concatenate([(dq_ref[h] * cos_s).astype(BF16) for h in range(N_HEADS)], axis=1)
        dqs = jnp.concatenate([(dq_ref[h] * sin_s).astype(BF16) for h in range(N_HEADS)], axis=1)
        dkb = jnp.concatenate([dk_ref[h].astype(BF16) for h in range(N_HEADS)], axis=1)
        _accumulate(dwq_ref, _dot(dqa, cqn, TN), first)
        _accumulate(dwqs_ref, _dot(dqs, cqn, TN), first)
        _accumulate(dwk_ref, _dot(dkb, ckvn, TN), first)
        dcqn = _dot(dqa, wq_ref[...]) + _dot(dqs, wqs_ref[...])
        dckvn = _dot(dvv, wv_ref[...]) + _dot(dkb, wk_ref[...])
        dcq, dqn = _rms_bwd(cq, qn_ref[...], dcqn)
        dckv, dkvn = _rms_bwd(ckv, kvn_ref[...], dckvn)
        _accumulate(dqn_ref, dqn, first)
        _accumulate(dkvn_ref, dkvn, first)
        dpa_ref[:, 0:Q_RANK] = dcq.astype(BF16)
        dpa_ref[:, Q_RANK:Q_RANK + KV_RANK] = dckv.astype(BF16)
        dpa_ref[:, 384:512] = (dk_rope * cosv).astype(BF16)
        dpa_ref[:, 512:640] = (dk_rope * sinv).astype(BF16)

    head_blk = pl.BlockSpec((N_HEADS, t, HEAD_PAD), lambda i: (0, i, 0))
    return _pcall(
        body, name="attn_prep_bwd", grid=(s // t,),
        in_specs=[_rows(t, PA_W), head_blk, head_blk, _rows(t, BR_W), _rows(t, LANES), _rows(t, LANES),
                  _whole(qn.shape), _whole(kvn.shape), _whole(wq.shape), _whole(wqs.shape), _whole(wk.shape), _whole(wv.shape)],
        out_specs=[_rows(t, PA_W), _acc(wq.shape), _acc(wqs.shape), _acc(wk.shape), _acc(wv.shape),
                   _acc(qn.shape), _acc(kvn.shape)],
        out_shape=[_sds((s, PA_W), BF16), _sds(wq.shape, F32), _sds(wqs.shape, F32), _sds(wk.shape, F32),
                   _sds(wv.shape, F32), _sds(qn.shape, F32), _sds(kvn.shape, F32)],
        vmem=VMEM_LIMIT)(pa, dq, dk, dv, cosf, sins, qn, kvn, wq, wqs, wk, wv)


def _inproj_bwd(x, g1, dxmid, dpa, dpm, dpres, w_in, wg, t):
    s = x.shape[0]
    n = s // t

    def body(x_ref, g1_ref, dxm_ref, dpa_ref, dpm_ref, d0_ref, d1_ref, d2_ref, d3_ref, w_ref, wg_ref,
             dx_ref, dg1_ref, dwa_ref, dwm_ref, acc_a, acc_m):
        first, last = pl.program_id(0) == 0, pl.program_id(0) == n - 1
        xv, g1v = x_ref[...], g1_ref[...]
        hb = _rms(xv, g1v).astype(BF16)
        dpav, dpmv = dpa_ref[...], dpm_ref[...]
        dh = _dot(dpav, w_ref[:PA_W, :]) + _dot(dpmv, w_ref[PA_W:, :])
        for k, d_ref in enumerate((d0_ref, d1_ref, d2_ref, d3_ref)):
            dh = dh + _dot(d_ref[...], wg_ref[k * D_MODEL:(k + 1) * D_MODEL, :])
        dx, dg1 = _rms_bwd(xv, g1v, dh)
        dx_ref[...] = dxm_ref[...] + dx
        _accumulate(dg1_ref, dg1, first)
        _accumulate_out(acc_a, dwa_ref, _dot(dpav, hb, TN), first, last)
        _accumulate_out(acc_m, dwm_ref, _dot(dpmv, hb, TN), first, last)

    return _pcall(
        body, name="inproj_bwd", grid=(n,),
        in_specs=[_rows(t, D_MODEL), _whole(g1.shape), _rows(t, D_MODEL), _rows(t, PA_W), _rows(t, PM_W)] +
                 [_rows(t, D_MODEL)] * 4 + [_whole(w_in.shape), _whole(wg.shape)],
        out_specs=[_rows(t, D_MODEL), _acc((1, D_MODEL)), _acc((PA_W, D_MODEL)), _acc((PM_W, D_MODEL))],
        out_shape=[_sds((s, D_MODEL), F32), _sds((1, D_MODEL), F32), _sds((PA_W, D_MODEL), BF16), _sds((PM_W, D_MODEL), BF16)],
        scratch=[pltpu.VMEM((PA_W, D_MODEL), F32), pltpu.VMEM((PM_W, D_MODEL), F32)],
        vmem=VMEM_LIMIT)(x, g1, dxmid, dpa, dpm, *dpres, w_in, wg)


def _my_place():
    return lax.axis_index("x"), lax.axis_index("y"), lax.axis_index("c")


def _flip(place, k):
    x, y, c = place
    return (1 - x if k & 4 else x, 1 - y if k & 2 else y, 1 - c if k & 1 else c)


def _rank(place):
    x, y, c = place
    return 4 * x + 2 * y + c


EXCHANGE_SEMS = (pltpu.SemaphoreType.DMA((7,)), pltpu.SemaphoreType.DMA((7,)), pltpu.SemaphoreType.DMA)
HBM_REF = pl.BlockSpec(memory_space=pl.ANY)


def _gather_plan(x_ref, out_ref, send_sems, recv_sems, local_sem):
    me = _my_place()
    sibling = _flip(me, 1)
    chips = (4, 2, 6)

    def copy(k, src_place, to, src=None):
        slot = out_ref.at[_rank(src_place)]
        return pltpu.make_async_remote_copy(
            src_ref=slot if src is None else src, dst_ref=slot, send_sem=send_sems.at[k], recv_sem=recv_sems.at[k],
            device_id=to, device_id_type=MESH_ID)

    mine = pltpu.make_async_copy(x_ref, out_ref.at[_rank(me)], local_sem)
    first = [copy(0, me, sibling, src=x_ref)] + [copy(1 + j, me, _flip(me, kc), src=x_ref) for j, kc in enumerate(chips)]
    passed = [copy(4 + j, _flip(me, kc), sibling) for j, kc in enumerate(chips)]

    def start():
        mine.start()
        for cp in first:
            cp.start()

    def forward():
        for j, kc in enumerate(chips):
            copy(1 + j, _flip(me, kc), me).wait_recv()
            passed[j].start()

    def finish():
        copy(0, sibling, me).wait_recv()
        for j, kc in enumerate(chips):
            copy(4 + j, _flip(sibling, kc), me).wait_recv()
        for cp in first + passed:
            cp.wait_send()
        mine.wait()

    return start, forward, finish


def _exchange_plan(x_ref, out_ref, send_sems, recv_sems, local_sem):
    me = _my_place()
    my_rank = _rank(me)
    mine = pltpu.make_async_copy(x_ref.at[my_rank], out_ref.at[my_rank], local_sem)
    sends = [pltpu.make_async_remote_copy(
        src_ref=x_ref.at[_rank(_flip(me, k))], dst_ref=out_ref.at[my_rank], send_sem=send_sems.at[k - 1],
        recv_sem=recv_sems.at[k - 1], device_id=_flip(me, k), device_id_type=MESH_ID) for k in range(1, N_DEV)]

    def start():
        mine.start()
        for cp in sends:
            cp.start()

    def forward():
        pass

    def finish():
        for k in range(1, N_DEV):
            slot = out_ref.at[_rank(_flip(me, k))]
            pltpu.make_async_remote_copy(
                src_ref=slot, dst_ref=slot, send_sem=send_sems.at[k - 1], recv_sem=recv_sems.at[k - 1],
                device_id=_flip(me, k), device_id_type=MESH_ID).wait_recv()
        for cp in sends:
            cp.wait_send()
        mine.wait()

    return start, forward, finish


def _exchange_alone(plan, src, out_sds, name):
    def body(x_ref, out_ref, send_sems, recv_sems, local_sem):
        start, forward, finish = plan(x_ref, out_ref, send_sems, recv_sems, local_sem)
        start()
        forward()
        finish()

    return pl.pallas_call(body, name=name, out_shape=out_sds, in_specs=[HBM_REF], out_specs=HBM_REF,
                          scratch_shapes=list(EXCHANGE_SEMS))(src)


def _all_gather(shard, name):
    return _exchange_alone(_gather_plan, shard, _sds((N_DEV,) + shard.shape, shard.dtype), name)


def _all_to_all(blocks, name):
    return _exchange_alone(_exchange_plan, blocks, _sds(blocks.shape, blocks.dtype), name)


def _riding(plan, src, out_sds):
    return (plan, src, out_sds)


def _ride_along(riding, step_id, n_steps, refs):
    start, forward, finish = riding[0](*refs)

    @pl.when(step_id == 0)
    def _():
        start()

    @pl.when(step_id == (3 * n_steps) // 4)
    def _():
        forward()

    def at_end():
        @pl.when(step_id == n_steps - 1)
        def _():
            finish()

    return at_end


def _sum_slots(parts, name):
    _, r, c = parts.shape
    t = math.gcd(r, ROW_PAD)

    def body(p_ref, o_ref):
        acc = p_ref[0].astype(F32)
        for d in range(1, N_DEV):
            acc = acc + p_ref[d].astype(F32)
        o_ref[...] = acc

    return _pcall(
        body, name=name, grid=(r // t,),
        in_specs=[pl.BlockSpec((N_DEV, t, c), lambda i: (0, i, 0))], out_specs=_rows(t, c),
        out_shape=_sds((r, c), F32))(parts)


def _adamw(w, g, m, v, name):
    depth, r, c = w.shape
    t = math.gcd(r, ROW_PAD)
    if r <= ROW_PAD or t % 8:
        t = r
    blk = pl.BlockSpec((1, t, c), lambda l, i: (l, i, 0))

    def body(w_ref, g_ref, m_ref, v_ref, d_ref, nm_ref, nv_ref):
        gv = g_ref[...]
        nm = ADAM_B1 * m_ref[...] + (1.0 - ADAM_B1) * gv
        nv = ADAM_B2 * v_ref[...] + (1.0 - ADAM_B2) * jnp.square(gv)
        m_hat = nm / (1.0 - ADAM_B1 ** ADAM_STEP)
        v_hat = nv / (1.0 - ADAM_B2 ** ADAM_STEP)
        d_ref[...] = -ADAM_LR * (m_hat / (jnp.sqrt(v_hat) + ADAM_EPS) + ADAM_WD * w_ref[...])
        nm_ref[...] = nm
        nv_ref[...] = nv

    return _pcall(
        body, name=name, grid=(depth, r // t),
        in_specs=[blk] * 4, out_specs=[blk] * 3, out_shape=[_sds(w.shape, F32)] * 3, vmem=VMEM_LIMIT)(w, g, m, v)


BEFORE_ATTN = ("w_in", "w_uq", "w_ukv")
AFTER_ATTN = ("w_br_mla", "w_br_sg", "w_br_conv", "w_br_pool", "w_out", "w_ff1", "w_ff2", "conv_w")
SHARDED = BEFORE_ATTN + AFTER_ATTN
ROW_SHARDED = ("w_out", "w_ff2")
REPLICATED = ("norm_mix_pre", "gate_b", "q_norm", "kv_norm", "sg_ln_g", "sg_ln_b", "sg_w", "sg_b", "pool_w",
              "pool_scale", "norm_mix_post", "norm_ffn_pre", "norm_ffn_post")
ROW_PAD = 256
GRAD_ROW_PAD = 64
PART_ROWS = 16


def _pack_rows(arrays, dtype, multiple, lead=()):
    rows, offsets, at = [], [], 0
    zero_pad = ((0, 0),) * len(lead)
    for a in arrays:
        if a.ndim == len(lead) + 2 and a.shape[-1] == D_MODEL:
            n = a.shape[-2]
            n_pad = -(-n // PART_ROWS) * PART_ROWS
            rows.append(jnp.pad(a.astype(dtype), zero_pad + ((0, n_pad - n), (0, 0))))
        else:
            flat = a.reshape(lead + (-1,)).astype(dtype)
            n = -(-flat.shape[-1] // D_MODEL)
            n_pad = -(-n // PART_ROWS) * PART_ROWS
            flat = jnp.pad(flat, zero_pad + ((0, n_pad * D_MODEL - flat.shape[-1]),))
            rows.append(flat.reshape(lead + (n_pad, D_MODEL)))
        offsets.append((at, n))
        at += n_pad
    pad = -at % multiple
    if pad:
        rows.append(jnp.zeros(lead + (pad, D_MODEL), dtype))
    return jnp.concatenate(rows, axis=len(lead)), offsets


def _unpack_rows(buf, offsets, shapes):
    lead = buf.shape[:-2]
    out = []
    for (at, n), shape in zip(offsets, shapes):
        if tuple(shape) == (n, D_MODEL):
            out.append(buf[..., at:at + n, :])
            continue
        size = math.prod(shape)
        flat = buf[..., at:at + n, :].reshape(lead + (n * D_MODEL,))[..., :size]
        out.append(flat.reshape(lead + tuple(shape)))
    return out


def _pack_shards(p, depth, names):
    arrays = []
    for n in names:
        a = p[n][0] if n in ROW_SHARDED else jnp.swapaxes(p[n][0], -1, -2)
        arrays.append(lax.bitcast_convert_type(a, BF16) if n == "conv_w" else a.astype(BF16))
    packed, offsets = _pack_rows(arrays, BF16, GRAD_ROW_PAD, lead=(depth,))
    return packed, offsets, [a.shape[1:] for a in arrays]


def _unpack_layer_weights(names, gathered, offsets, shapes):
    full = {}
    for n, gth in zip(names, _unpack_rows(gathered, offsets, shapes)):
        if n == "conv_w":
            gth = lax.bitcast_convert_type(gth, F32)
        full[n] = gth.reshape((-1,) + gth.shape[2:])
        if n == "conv_w":
            full[n] = full[n].T
    return full


def _owner_major(grad_sharded_dim_first):
    a = grad_sharded_dim_first
    return a.reshape((N_DEV, a.shape[0] // N_DEV) + a.shape[1:])


def _owner_block_shape(name, shard_shape):
    k, n = shard_shape[-2:]
    return (k, n) if name in ROW_SHARDED else (n, k)


def _natural_shard(name, blocks):
    return blocks if name in ROW_SHARDED else jnp.swapaxes(blocks, -1, -2)


def _swap_halves(a, axis=-1):
    lo, hi = jnp.split(a, 2, axis=axis)
    return jnp.concatenate([hi, lo], axis=axis)


def _pad_rows(a, top, total):
    return jnp.pad(a, ((0, 0),) * (a.ndim - 2) + ((top, total - top - a.shape[-2]), (0, 0)))


def _weights_before_attn(full, small, l):
    w_in = full["w_in"]
    k_r = w_in[384:416]
    w_in_k = jnp.concatenate(
        [w_in[0:384], _pad_rows(k_r, D_NOPE, HEAD_PAD), _pad_rows(_swap_halves(k_r, axis=0), D_NOPE, HEAD_PAD), w_in[416:1952]], axis=0)
    w_uq = full["w_uq"].reshape(N_HEADS, D_NOPE + D_ROPE, Q_RANK)
    wq = _pad_rows(w_uq, 0, HEAD_PAD).reshape(N_HEADS * HEAD_PAD, Q_RANK)
    wqs = _pad_rows(_swap_halves(w_uq[:, D_NOPE:], axis=1), D_NOPE, HEAD_PAD).reshape(N_HEADS * HEAD_PAD, Q_RANK)
    w_ukv = full["w_ukv"].reshape(N_HEADS, D_NOPE + D_V, KV_RANK)
    wk = _pad_rows(w_ukv[:, :D_NOPE], 0, HEAD_PAD).reshape(N_HEADS * HEAD_PAD, KV_RANK)
    wv = w_ukv[:, D_NOPE:].reshape(N_HEADS * D_V, KV_RANK)
    tri = jnp.tril(jnp.ones((SG_CHUNK, SG_CHUNK), bool))
    wm = jnp.where(tri, small["sg_w"][l], 0.0)
    pool_w = small["pool_w"][l]
    wp = jnp.zeros((BR_W, BR_W), F32)
    for g in range(len(POOL_WINDOWS)):
        wp = wp.at[g * POOL_GROUP:(g + 1) * POOL_GROUP, g * POOL_GROUP:(g + 1) * POOL_GROUP].set(pool_w[g])
    vec = lambda name: small[name][l][None, :]
    return dict(
        w_in=w_in_k, wg=w_in[1952:], bg=vec("gate_b"), g1=vec("norm_mix_pre"), qn=vec("q_norm"), kvn=vec("kv_norm"),
        wq=wq, wqs=wqs, wk=wk, wv=wv,
        ln_g=vec("sg_ln_g"), ln_b=vec("sg_ln_b"), wm=wm.astype(BF16), wmt=jnp.swapaxes(wm, 1, 2).astype(BF16),
        sgb=jnp.repeat(small["sg_b"][l].T, POOL_GROUP, axis=1), wp=wp.astype(BF16), pscale=vec("pool_scale"),
        g2=vec("norm_mix_post"), g3=vec("norm_ffn_pre"), g4=vec("norm_ffn_post"))


def _weights_after_attn(full):
    return dict(
        conv_w=full["conv_w"], wbr=jnp.stack([full["w_br_mla"], full["w_br_sg"], full["w_br_conv"], full["w_br_pool"]]),
        wout=full["w_out"], w1=full["w_ff1"], w2=full["w_ff2"])


def _grads_after_attn(g):
    own = _owner_major
    return dict(
        w_br_mla=own(g["dwbr"][0]), w_br_sg=own(g["dwbr"][1]), w_br_conv=own(g["dwbr"][2]), w_br_pool=own(g["dwbr"][3]),
        w_out=own(g["dwout"]), w_ff1=own(g["dw1"]), w_ff2=own(g["dw2"]), conv_w=own(g["dcw"][:CONV_K].T))


def _grads_before_attn(g):
    dwa = g["dwa"]
    rope_rows = slice(D_NOPE, D_NOPE + D_ROPE)
    d_kr = dwa[384:512][rope_rows].astype(F32) + _swap_halves(dwa[512:640][rope_rows], axis=0).astype(F32)
    w_in = jnp.concatenate([dwa[0:384], d_kr.astype(BF16), g["dwm_in"]] + list(g["dwg"]), axis=0)
    dwq, dwqs = (g[n].reshape(N_HEADS, HEAD_PAD, Q_RANK) for n in ("dwq", "dwqs"))
    d_rope = dwq[:, rope_rows] + _swap_halves(dwqs[:, rope_rows], axis=1)
    w_uq = jnp.concatenate([dwq[:, :D_NOPE], d_rope], axis=1).reshape(N_HEADS * (D_NOPE + D_ROPE), Q_RANK)
    dwk = g["dwk"].reshape(N_HEADS, HEAD_PAD, KV_RANK)
    w_ukv = jnp.concatenate([dwk[:, :D_NOPE], g["dwv"].reshape(N_HEADS, D_V, KV_RANK)], axis=1)
    w_ukv = w_ukv.reshape(N_HEADS * (D_NOPE + D_V), KV_RANK)
    pool_w = jnp.stack([g["dwp"][i * POOL_GROUP:(i + 1) * POOL_GROUP, i * POOL_GROUP:(i + 1) * POOL_GROUP]
                        for i in range(len(POOL_WINDOWS))])
    sg_b = g["dsgb"].reshape(SG_CHUNK, SG_GROUPS, POOL_GROUP).sum(axis=-1).T
    own = _owner_major
    return dict(
        w_in=own(w_in), w_uq=own(w_uq), w_ukv=own(w_ukv),
        norm_mix_pre=g["dg1"][0], gate_b=jnp.concatenate([b[0] for b in g["dbg"]]), q_norm=g["dqn"][0], kv_norm=g["dkvn"][0],
        sg_ln_g=g["dlng"][0], sg_ln_b=g["dlnb"][0], sg_w=g["dwm"], sg_b=sg_b, pool_w=pool_w, pool_scale=g["dps"][0],
        norm_mix_post=g["dg2"][0], norm_ffn_pre=g["dg3"][0], norm_ffn_post=g["dg4"][0])


def _rope_tables(positions):
    inv_freq = ROPE_BASE ** (-jnp.arange(0, D_ROPE, 2, dtype=F32) / D_ROPE)
    ang = positions.astype(F32)[:, None] * inv_freq
    cos, sin = jnp.cos(ang), jnp.sin(ang)
    s = positions.shape[0]
    cosf = jnp.concatenate([jnp.ones((s, D_NOPE), F32), cos, cos, jnp.zeros((s, HEAD_PAD - D_NOPE - D_ROPE), F32)], axis=1)
    sins = jnp.concatenate([jnp.zeros((s, D_NOPE), F32), -sin, sin, jnp.zeros((s, HEAD_PAD - D_NOPE - D_ROPE), F32)], axis=1)
    return cosf, sins


def _layer_fwd(x, w, weights_after_attn, cosf, sins, tiles, riding):
    t, ta = tiles["tok"], tiles["attn"]
    hb, pa, pm = _inproj_fwd(x, w["g1"], w["w_in"], tiles["wgrad"])
    q, k, v = _attn_prep_fwd(pa, cosf, sins, w["qn"], w["kvn"], w["wq"], w["wqs"], w["wk"], w["wv"], tiles["wgrad"])
    o, lse, landed = _flash_fwd(q, k, v, ta, riding)
    w = {**w, **weights_after_attn(landed)}
    bsg, bcv, bpl = _mixers_fwd(pm, w["ln_g"], w["ln_b"], w["wm"], w["sgb"], w["conv_w"], w["wp"], w["pscale"], tiles["wgrad"])
    x_mid, merged, tt = _merge_fwd(x, hb, (o, bsg, bcv, bpl), w["wg"], w["bg"], w["wbr"], w["wout"], w["g2"], t)
    x_out, f = _ffn_fwd(x_mid, w["g3"], w["w1"], w["w2"], w["g4"], t)
    saved = dict(x=x, hb=hb, pa=pa, pm=pm, q=q, k=k, v=v, o=o, lse=lse, bsg=bsg, bcv=bcv, bpl=bpl, x_mid=x_mid,
                 merged=merged, tt=tt, f=f)
    return x_out, saved, w, landed


def _layer_bwd(dxo, a, w, cosf, sins, tiles, pending):
    t, ta, tb, tw = tiles["tok"], tiles["attn"], tiles["inproj_bwd"], tiles["wgrad"]
    g = {}
    h2, df, g["dg4"] = _ffn_bwd_norms(a["x_mid"], a["f"], dxo, w["g3"], w["g4"], tw)
    da, g["dw1"], g["dw2"] = _ffn_bwd_weights(h2, df, w["w1"], w["w2"], tw)
    dxmid, g["dg3"] = _ffn_bwd_input(da, w["w1"], a["x_mid"], dxo, w["g3"], t)
    dm, g["dwout"], g["dg2"] = _merge_bwd_out(a["tt"], dxmid, w["g2"], w["wout"], a["merged"], tw)
    dbrs, dpres, g["dwg"], g["dbg"], g["dwbr"] = [], [], [], [], []
    for k, br in enumerate((a["o"], a["bsg"], a["bcv"], a["bpl"])):
        dbr, dpre, dwg, dbg, dwbr = _merge_bwd_branch(k, a["hb"], br, dm, w["wg"], w["bg"], w["wbr"], tw)
        dbrs.append(dbr); dpres.append(dpre); g["dwg"].append(dwg); g["dbg"].append(dbg); g["dwbr"].append(dwbr)
    dpm, g["dlng"], g["dlnb"], g["dwm"], g["dsgb"], g["dcw"], g["dwp"], g["dps"] = _mixers_bwd(
        a["pm"], dbrs[1], dbrs[2], dbrs[3], w["ln_g"], w["ln_b"], w["wm"], w["wmt"], w["sgb"], w["conv_w"], w["wp"], w["pscale"], tw)
    after = _grads_after_attn(g)
    send, after_offsets = _pack_rows([after[n] for n in AFTER_ATTN], BF16, GRAD_ROW_PAD, lead=(N_DEV,))
    after_rows = send.shape[1]
    if pending is not None:
        send = jnp.concatenate([send, pending], axis=1)
    riding = _riding(_exchange_plan, send, _sds(send.shape, send.dtype))
    dq, dk, dv, landed = _flash_bwd(a["q"], a["k"], a["v"], a["o"], dbrs[0], a["lse"], ta, riding)
    dpa, g["dwq"], g["dwqs"], g["dwk"], g["dwv"], g["dqn"], g["dkvn"] = _attn_prep_bwd(
        a["pa"], dq, dk, dv, cosf, sins, w["qn"], w["kvn"], w["wq"], w["wqs"], w["wk"], w["wv"], tw)
    dx, g["dg1"], g["dwa"], g["dwm_in"] = _inproj_bwd(a["x"], w["g1"], dxmid, dpa, dpm, dpres, w["w_in"], w["wg"], tb)
    return dx, _grads_before_attn(g), landed, (after_rows, after_offsets)


def _step(p, x, positions, loss_target):
    s = x.shape[0]
    depth = p["w_in"][0].shape[0]
    tiles = dict(tok=min(TOK_TILE, s), attn=min(ATTN_TILE, s), inproj_bwd=min(INPROJ_BWD_TILE, s),
                 wgrad=min(WGRAD_TILE, s))

    small = {n: p[n][0] for n in REPLICATED}
    cosf, sins = _rope_tables(positions)

    pack_b, off_b, shapes_b = _pack_shards(p, depth, BEFORE_ATTN)
    pack_a, off_a, shapes_a = _pack_shards(p, depth, AFTER_ATTN)
    rows_a = pack_a.shape[1]
    before = _all_gather(pack_b[0], "weight_all_gather")
    weights, acts = [], []
    h = x
    for l in range(depth):
        w = _weights_before_attn(_unpack_layer_weights(BEFORE_ATTN, before, off_b, shapes_b), small, l)
        src = pack_a[l] if l + 1 == depth else jnp.concatenate([pack_a[l], pack_b[l + 1]], axis=0)
        riding = _riding(_gather_plan, src, _sds((N_DEV,) + src.shape, src.dtype))

        def after(landed):
            return _weights_after_attn(_unpack_layer_weights(AFTER_ATTN, landed[:, :rows_a], off_a, shapes_a))

        h, saved, w, landed = _layer_fwd(h, w, after, cosf, sins, tiles, riding)
        weights.append(w)
        acts.append(saved)
        before = landed[:, rows_a:] if l + 1 < depth else None
    dh, loss_blk = _loss_head(h, loss_target, tiles["tok"])
    loss = lax.psum(loss_blk[0, 0], ("x", "y", "c"))

    grads_b = [None] * depth
    sum_a, sum_b = [None] * depth, [None] * depth
    pending = None
    for l in reversed(range(depth)):
        dh, grads_b[l], landed, (rows_ga, off_ga) = _layer_bwd(dh, acts[l], weights[l], cosf, sins, tiles, pending)
        summed = _sum_slots(landed, "grad_shard_sum")
        sum_a[l] = summed[:rows_ga]
        if pending is not None:
            sum_b[l + 1] = summed[rows_ga:]
        pending, off_gb = _pack_rows([grads_b[l][n] for n in BEFORE_ATTN], BF16, GRAD_ROW_PAD, lead=(N_DEV,))
    sum_b[0] = _sum_slots(_all_to_all(pending, "grad_all_to_all"), "grad_shard_sum")
    g_shard = {}
    for names, sums, offsets in ((AFTER_ATTN, sum_a, off_ga), (BEFORE_ATTN, sum_b, off_gb)):
        block_shapes = [_owner_block_shape(n, p[n][0].shape) for n in names]
        for n, b in zip(names, _unpack_rows(jnp.stack(sums), offsets, block_shapes)):
            g_shard[n] = b
    small_grads = [jnp.stack([grads_b[l][n] for l in range(depth)]) for n in REPLICATED]
    small_rows, small_offsets = _pack_rows(small_grads, F32, ROW_PAD)
    g_small_rows = _sum_slots(_all_gather(small_rows, "replicated_grad_all_gather"), "grad_replicated_sum")

    grad, delta, new_m, new_v = {}, {}, {}, {}
    for n in SHARDED:
        w, m, v = (_natural_shard(n, a) for a in p[n])
        results = _adamw(w, g_shard[n], m, v, f"adamw_{n}")
        grad[n], delta[n], new_m[n], new_v[n] = (_natural_shard(n, a) for a in (g_shard[n],) + tuple(results))
    packs = [_pack_rows([p[n][i] for n in REPLICATED], F32, ROW_PAD)[0][None] for i in range(3)]
    d, nm, nv = (a[0] for a in _adamw(packs[0], g_small_rows[None], packs[1], packs[2], "adamw_replicated"))
    shapes = [p[n][0].shape for n in REPLICATED]
    for n, gg, dd, mm, vv in zip(REPLICATED, _unpack_rows(g_small_rows, small_offsets, shapes), _unpack_rows(d, small_offsets, shapes),
                                 _unpack_rows(nm, small_offsets, shapes), _unpack_rows(nv, small_offsets, shapes)):
        grad[n], delta[n], new_m[n], new_v[n] = gg, dd, mm, vv
    return loss, dh, grad, delta, new_m, new_v


WEIGHT_ORDER = ("norm_mix_pre", "w_in", "gate_b", "q_norm", "w_uq", "kv_norm", "w_ukv", "w_br_mla", "sg_ln_g", "sg_ln_b",
                "sg_w", "sg_b", "w_br_sg", "conv_w", "w_br_conv", "pool_w", "pool_scale", "w_br_pool", "w_out",
                "norm_mix_post", "norm_ffn_pre", "w_ff1", "w_ff2", "norm_ffn_post")


def kernel(x, positions, norm_mix_pre, w_in, gate_b, q_norm, w_uq, kv_norm, w_ukv, w_br_mla, sg_ln_g, sg_ln_b, sg_w, sg_b, w_br_sg, conv_w, w_br_conv, pool_w, pool_scale, w_br_pool, w_out, norm_mix_post, norm_ffn_pre, w_ff1, w_ff2, norm_ffn_post, loss_target, m_norm_mix_pre, m_w_in, m_gate_b, m_q_norm, m_w_uq, m_kv_norm, m_w_ukv, m_w_br_mla, m_sg_ln_g, m_sg_ln_b, m_sg_w, m_sg_b, m_w_br_sg, m_conv_w, m_w_br_conv, m_pool_w, m_pool_scale, m_w_br_pool, m_w_out, m_norm_mix_post, m_norm_ffn_pre, m_w_ff1, m_w_ff2, m_norm_ffn_post, v_norm_mix_pre, v_w_in, v_gate_b, v_q_norm, v_w_uq, v_kv_norm, v_w_ukv, v_w_br_mla, v_sg_ln_g, v_sg_ln_b, v_sg_w, v_sg_b, v_w_br_sg, v_conv_w, v_w_br_conv, v_pool_w, v_pool_scale, v_w_br_pool, v_w_out, v_norm_mix_post, v_norm_ffn_pre, v_w_ff1, v_w_ff2, v_norm_ffn_post):
    ws = (norm_mix_pre, w_in, gate_b, q_norm, w_uq, kv_norm, w_ukv, w_br_mla, sg_ln_g, sg_ln_b, sg_w, sg_b, w_br_sg, conv_w,
          w_br_conv, pool_w, pool_scale, w_br_pool, w_out, norm_mix_post, norm_ffn_pre, w_ff1, w_ff2, norm_ffn_post)
    ms = (m_norm_mix_pre, m_w_in, m_gate_b, m_q_norm, m_w_uq, m_kv_norm, m_w_ukv, m_w_br_mla, m_sg_ln_g, m_sg_ln_b, m_sg_w,
          m_sg_b, m_w_br_sg, m_conv_w, m_w_br_conv, m_pool_w, m_pool_scale, m_w_br_pool, m_w_out, m_norm_mix_post,
          m_norm_ffn_pre, m_w_ff1, m_w_ff2, m_norm_ffn_post)
    vs = (v_norm_mix_pre, v_w_in, v_gate_b, v_q_norm, v_w_uq, v_kv_norm, v_w_ukv, v_w_br_mla, v_sg_ln_g, v_sg_ln_b, v_sg_w,
          v_sg_b, v_w_br_sg, v_conv_w, v_w_br_conv, v_pool_w, v_pool_scale, v_w_br_pool, v_w_out, v_norm_mix_post,
          v_norm_ffn_pre, v_w_ff1, v_w_ff2, v_norm_ffn_post)
    p = {n: (w, m, v) for n, w, m, v in zip(WEIGHT_ORDER, ws, ms, vs)}
    loss, grad_x, grad, delta, new_m, new_v = _step(p, x[0], positions[0], loss_target[0])
    return (loss, grad_x[None], *[grad[n] for n in WEIGHT_ORDER], *[delta[n] for n in WEIGHT_ORDER],
            *[new_m[n] for n in WEIGHT_ORDER], *[new_v[n] for n in WEIGHT_ORDER])
```

```python
import functools
import math

import jax
import jax.numpy as jnp
from jax import lax
from jax.experimental import pallas as pl
from jax.experimental.pallas import tpu as pltpu

F32 = jnp.float32
BF16 = jnp.bfloat16

D_MODEL = 1024
N_HEADS = 4
D_NOPE = 64
D_ROPE = 32
D_V = 64
Q_RANK = 256
KV_RANK = 128
BR_W = 256
SG_CHUNK = 128
SG_GROUPS = 4
POOL_WINDOWS = (2, 4, 8, 16)
POOL_GROUP = 64
CONV_K = 3
D_FF = 4096
N_BRANCH = 4
N_IN = 6048
EPS = 1e-6
ROPE_BASE = 10000.0
ADAM_LR = 0.001
ADAM_B1 = 0.9
ADAM_B2 = 0.999
ADAM_EPS = 1e-08
ADAM_WD = 0.01
ADAM_STEP = 10

N_DEV = 8
LANES = 128
HEAD_PAD = 128
HALO = 16
PA_W = 640
PM_W = 6 * BR_W
VMEM_LIMIT = 56 * 1024 * 1024
SM_SCALE = (D_NOPE + D_ROPE) ** -0.5
TOK_TILE = 512
ATTN_TILE = 1024
WGRAD_TILE = 1024
INPROJ_BWD_TILE = 512

NN = (((1,), (0,)), ((), ()))
NT = (((1,), (1,)), ((), ()))
TN = (((0,), (0,)), ((), ()))
MESH_ID = pl.DeviceIdType.MESH


def _dot(a, b, dims=NN):
    return lax.dot_general(a, b, dims, preferred_element_type=F32)


def _pcall(body, *, name, grid, in_specs, out_specs, out_shape, scratch=(), vmem=None, prefetch=0):
    params = pltpu.CompilerParams(vmem_limit_bytes=vmem)
    if prefetch:
        spec = pltpu.PrefetchScalarGridSpec(num_scalar_prefetch=prefetch, grid=grid, in_specs=in_specs,
                                            out_specs=out_specs, scratch_shapes=scratch)
        return pl.pallas_call(body, name=name, grid_spec=spec, out_shape=out_shape, compiler_params=params)
    return pl.pallas_call(
        body, name=name, grid=grid, in_specs=in_specs, out_specs=out_specs, out_shape=out_shape,
        scratch_shapes=scratch, compiler_params=params)


def _rows(t, width):
    return pl.BlockSpec((t, width), lambda i: (i, 0))


def _whole(shape):
    nd = len(shape)
    return pl.BlockSpec(tuple(shape), lambda *_: (0,) * nd, pipeline_mode=pl.Buffered(1))


def _acc(shape):
    nd = len(shape)
    return pl.BlockSpec(tuple(shape), lambda *_: (0,) * nd)


def _sds(shape, dtype):
    return jax.ShapeDtypeStruct(tuple(shape), dtype)


def _rms(x, g):
    return x * lax.rsqrt(jnp.mean(x * x, axis=-1, keepdims=True) + EPS) * g


def _rms_bwd(x, g, dy):
    r = lax.rsqrt(jnp.mean(x * x, axis=-1, keepdims=True) + EPS)
    xh = x * r
    dg = jnp.sum(dy * xh, axis=0, keepdims=True)
    dxh = dy * g
    dx = r * (dxh - xh * jnp.mean(dxh * xh, axis=-1, keepdims=True))
    return dx, dg


def _sigmoid(x):
    return 1.0 / (1.0 + jnp.exp(-x))


def _gelu(x):
    return jax.nn.gelu(x, approximate=True)


def _accumulate(ref, val, first):
    @pl.when(first)
    def _():
        ref[...] = val

    @pl.when(jnp.logical_not(first))
    def _():
        ref[...] += val


def _accumulate_out(acc_ref, out_ref, val, first, last):
    _accumulate(acc_ref, val, first)

    @pl.when(last)
    def _():
        out_ref[...] = acc_ref[...].astype(out_ref.dtype)


def _inproj_fwd(x, g1, w_in, t):
    s = x.shape[0]

    def body(x_ref, g_ref, w_ref, hb_ref, pa_ref, pm_ref):
        hb = _rms(x_ref[...], g_ref[...]).astype(BF16)
        hb_ref[...] = hb
        pa_ref[...] = _dot(hb, w_ref[:PA_W, :], NT)
        pm_ref[...] = _dot(hb, w_ref[PA_W:, :], NT)

    return _pcall(
        body, name="inproj_fwd", grid=(s // t,),
        in_specs=[_rows(t, D_MODEL), _whole((1, D_MODEL)), _whole(w_in.shape)],
        out_specs=[_rows(t, D_MODEL), _rows(t, PA_W), _rows(t, PM_W)],
        out_shape=[_sds((s, D_MODEL), BF16), _sds((s, PA_W), F32), _sds((s, PM_W), F32)],
        vmem=VMEM_LIMIT)(x, g1, w_in)


def _attn_prep_fwd(pa, cosf, sins, qn, kvn, wq, wqs, wk, wv, t):
    s = pa.shape[0]

    def body(pa_ref, cos_ref, sin_ref, qn_ref, kvn_ref, wq_ref, wqs_ref, wk_ref, wv_ref, q_ref, k_ref, v_ref, v1_ref):
        cosv, sinv = cos_ref[...], sin_ref[...]
        cqn = _rms(pa_ref[:, 0:Q_RANK], qn_ref[...]).astype(BF16)
        ckvn = _rms(pa_ref[:, Q_RANK:Q_RANK + KV_RANK], kvn_ref[...]).astype(BF16)
        k_rope = pa_ref[:, 384:512] * cosv + pa_ref[:, 512:640] * sinv
        q_all, qs_all = _dot(cqn, wq_ref[...], NT), _dot(cqn, wqs_ref[...], NT)
        k_all = _dot(ckvn, wk_ref[...], NT)
        for h in range(N_HEADS):
            lanes = slice(h * HEAD_PAD, (h + 1) * HEAD_PAD)
            q_ref[h] = ((q_all[:, lanes] * cosv + qs_all[:, lanes] * sinv) * SM_SCALE).astype(BF16)
            k_ref[h] = (k_all[:, lanes] + k_rope).astype(BF16)
        vb = _dot(ckvn, wv_ref[...], NT).astype(BF16)
        v_ref[...] = vb
        ones = jnp.ones((t, LANES), BF16)
        for pr in range(N_HEADS // 2):
            v1_ref[:, 2 * pr * LANES:(2 * pr + 1) * LANES] = vb[:, pr * LANES:(pr + 1) * LANES]
            v1_ref[:, (2 * pr + 1) * LANES:(2 * pr + 2) * LANES] = ones

    head_blk = pl.BlockSpec((N_HEADS, t, HEAD_PAD), lambda i: (0, i, 0))
    return _pcall(
        body, name="attn_prep_fwd", grid=(s // t,),
        in_specs=[_rows(t, PA_W), _rows(t, LANES), _rows(t, LANES), _whole(qn.shape), _whole(kvn.shape),
                  _whole(wq.shape), _whole(wqs.shape), _whole(wk.shape), _whole(wv.shape)],
        out_specs=[head_blk, head_blk, _rows(t, BR_W), _rows(t, 2 * BR_W)],
        out_shape=[_sds((N_HEADS, s, HEAD_PAD), BF16), _sds((N_HEADS, s, HEAD_PAD), BF16), _sds((s, BR_W), BF16),
                   _sds((s, 2 * BR_W), BF16)],
        vmem=VMEM_LIMIT)(pa, cosf, sins, qn, kvn, wq, wqs, wk, wv)


def _causal_scores(q, k, masked):
    sc = _dot(q, k, NT)
    if masked:
        row = lax.broadcasted_iota(jnp.int32, sc.shape, 0)
        col = lax.broadcasted_iota(jnp.int32, sc.shape, 1)
        sc = jnp.where(col <= row, sc, -jnp.inf)
    return sc


def _diagonal_parts(t):
    half = t // 2
    if half % LANES:
        return [(True, slice(0, t), slice(0, t))]
    lower, upper = slice(0, half), slice(half, t)
    return [(True, lower, lower), (False, upper, lower), (True, upper, upper)]


def _causal_steps(n, key_major):
    if key_major:
        pairs = [(qi, kj) for kj in range(n) for qi in range(kj, n)]
    else:
        pairs = [(qi, kj) for qi in range(n) for kj in range(qi + 1)]
    return (jnp.asarray([p[0] for p in pairs], jnp.int32), jnp.asarray([p[1] for p in pairs], jnp.int32))


def _flash_fwd(q, k, v, t, riding):
    s = v.shape[0]
    n = s // t
    qi_tab, kj_tab = _causal_steps(n, key_major=False)
    n_steps = int(qi_tab.shape[0])
    whole = slice(0, t)

    def body(qi_ref, kj_ref, q_ref, k_ref, v_ref, x_ref, o_ref, lse_ref, land_ref, m_scr, l_scr, acc_scr,
             send_sems, recv_sems, local_sem):
        step_id = pl.program_id(0)
        qi, kj = qi_ref[step_id], kj_ref[step_id]
        at_end = _ride_along(riding, step_id, n_steps, (x_ref, land_ref, send_sems, recv_sems, local_sem))

        @pl.when(kj == 0)
        def _():
            m_scr[...] = jnp.full(m_scr.shape, -jnp.inf, F32)
            l_scr[...] = jnp.zeros(l_scr.shape, F32)
            acc_scr[...] = jnp.zeros(acc_scr.shape, F32)

        def step(masked, qr, kr):
            for h in range(N_HEADS):
                sc = _causal_scores(q_ref[h, qr, :], k_ref[h, kr, :], masked)
                m_prev = m_scr[h, qr, :]
                m_next = jnp.maximum(m_prev, jnp.max(sc, axis=1, keepdims=True))
                alpha = jnp.exp(m_prev - m_next)
                p = jnp.exp(sc - jnp.tile(m_next, (1, sc.shape[1] // LANES)))
                m_scr[h, qr, :] = m_next
                pv = _dot(p.astype(BF16), v_ref[kr, (h // 2) * 2 * LANES:(h // 2 + 1) * 2 * LANES])
                l_scr[h, qr, :] = alpha * l_scr[h, qr, :] + pv[:, LANES:]
                acc_scr[h, qr, :] = acc_scr[h, qr, :] * alpha + pv[:, :LANES]

        @pl.when(kj < qi)
        def _():
            step(False, whole, whole)

        @pl.when(kj == qi)
        def _():
            for masked, qr, kr in _diagonal_parts(t):
                step(masked, qr, kr)
            lane = lax.broadcasted_iota(jnp.int32, (t, LANES), 1)
            for pr in range(N_HEADS // 2):
                o0 = acc_scr[2 * pr] / l_scr[2 * pr]
                o1 = acc_scr[2 * pr + 1] / l_scr[2 * pr + 1]
                o_ref[:, pr * LANES:(pr + 1) * LANES] = jnp.where(lane < D_V, o0, o1).astype(BF16)
            for h in range(N_HEADS):
                lse_ref[h] = m_scr[h] + jnp.log(l_scr[h])

        at_end()

    q_blk = pl.BlockSpec((N_HEADS, t, HEAD_PAD), lambda i, qi, kj: (0, qi[i], 0))
    k_blk = pl.BlockSpec((N_HEADS, t, HEAD_PAD), lambda i, qi, kj: (0, kj[i], 0))
    v_blk = pl.BlockSpec((t, 2 * BR_W), lambda i, qi, kj: (kj[i], 0))
    return _pcall(
        body, name="flash_fwd_gathering", grid=(n_steps,), prefetch=2,
        in_specs=[q_blk, k_blk, v_blk, HBM_REF],
        out_specs=[pl.BlockSpec((t, BR_W), lambda i, qi, kj: (qi[i], 0)), q_blk, HBM_REF],
        out_shape=[_sds((s, BR_W), BF16), _sds((N_HEADS, s, LANES), F32), riding[2]],
        scratch=[pltpu.VMEM((N_HEADS, t, LANES), F32)] * 3 + list(EXCHANGE_SEMS),
        vmem=VMEM_LIMIT)(qi_tab, kj_tab, q, k, v, riding[1])


def _shift_down(ext, k, t):
    return pltpu.roll(ext, k, 0)[HALO:HALO + t]


def _shift_up(ext, k, t):
    return pltpu.roll(ext, t + HALO - k, 0)[0:t]


def _lane_group(shape):
    return lax.shift_right_logical(lax.broadcasted_iota(jnp.int32, shape, 1), 6)


def _group_select(vals):
    grp = _lane_group(vals[0].shape)
    out = vals[0]
    for g in range(1, len(vals)):
        out = jnp.where(grp == g, vals[g], out)
    return out


def _layernorm(x, g, b):
    mu = jnp.mean(x, axis=-1, keepdims=True)
    xc = x - mu
    return xc * lax.rsqrt(jnp.mean(xc * xc, axis=-1, keepdims=True) + EPS) * g + b


def _sg_mix(wm_ref, vnb, bias):
    return _group_select([_dot(wm_ref[g], vnb) for g in range(SG_GROUPS)]) + bias


def _pool_windows(ext):
    s2 = ext + pltpu.roll(ext, 1, 0)
    s4 = s2 + pltpu.roll(s2, 2, 0)
    s8 = s4 + pltpu.roll(s4, 4, 0)
    s16 = s8 + pltpu.roll(s8, 8, 0)
    return [s2, s4, s8, s16]


def _pool_counts(tok):
    return [jnp.minimum(tok + 1, w).astype(F32) for w in POOL_WINDOWS]


def _halo_before(t):
    return pl.BlockSpec((HALO, PM_W), lambda i: (jnp.maximum(i * (t // HALO) - 1, 0), 0))


def _mixers_fwd(pm, ln_g, ln_b, wm, sgb, conv_w, wp, pscale, t):
    s = pm.shape[0]

    def body(pm_ref, halo_ref, lng_ref, lnb_ref, wm_ref, sgb_ref, cw_ref, wp_ref, ps_ref, bsg_ref, bcv_ref, bpl_ref):
        i = pl.program_id(0)
        halo = jnp.where(i > 0, halo_ref[...], 0.0)
        u = _gelu(pm_ref[:, 0:256])
        vnb = _layernorm(_gelu(pm_ref[:, 256:512]), lng_ref[...], lnb_ref[...]).astype(BF16)
        for c in range(t // SG_CHUNK):
            rows = slice(c * SG_CHUNK, (c + 1) * SG_CHUNK)
            bsg_ref[rows, :] = (u[rows] * _sg_mix(wm_ref, vnb[rows], sgb_ref[...])).astype(BF16)
        z = pm_ref[:, 1024:1280] * pm_ref[:, 512:768]
        zext = jnp.concatenate([halo[:, 1024:1280] * halo[:, 512:768], z], axis=0)
        y = cw_ref[0:1, :] * _shift_down(zext, 2, t) + cw_ref[1:2, :] * _shift_down(zext, 1, t) + cw_ref[2:3, :] * z
        bcv_ref[...] = (pm_ref[:, 768:1024] * y).astype(BF16)
        p = pm_ref[:, 1280:1536]
        sums = _pool_windows(jnp.concatenate([halo[:, 1280:1536], p], axis=0))
        tok = i * t + lax.broadcasted_iota(jnp.int32, (t, 1), 0)
        pooled = _group_select([sw[HALO:HALO + t] / cnt - p for sw, cnt in zip(sums, _pool_counts(tok))])
        bpl_ref[...] = (_dot(pooled.astype(BF16), wp_ref[...]) * ps_ref[...]).astype(BF16)

    return _pcall(
        body, name="mixers_fwd", grid=(s // t,),
        in_specs=[_rows(t, PM_W), _halo_before(t), _whole(ln_g.shape), _whole(ln_b.shape), _whole(wm.shape),
                  _whole(sgb.shape), _whole(conv_w.shape), _whole(wp.shape), _whole(pscale.shape)],
        out_specs=[_rows(t, BR_W)] * 3,
        out_shape=[_sds((s, BR_W), BF16)] * 3,
        vmem=VMEM_LIMIT)(pm, pm, ln_g, ln_b, wm, sgb, conv_w, wp, pscale)


def _merge_fwd(x, hb, branches, wg, bg, wbr, wout, g2, t):
    s = x.shape[0]

    def body(x_ref, hb_ref, b0_ref, b1_ref, b2_ref, b3_ref, wg_ref, bg_ref, wbr_ref, wout_ref, g2_ref,
             xmid_ref, mrg_ref, t_ref):
        hb = hb_ref[...]
        merged = jnp.zeros((t, D_MODEL), F32)
        for k, b_ref in enumerate((b0_ref, b1_ref, b2_ref, b3_ref)):
            cols = slice(k * D_MODEL, (k + 1) * D_MODEL)
            gate = _sigmoid(_dot(hb, wg_ref[cols, :], NT) + bg_ref[:, cols])
            merged = merged + gate * _dot(b_ref[...], wbr_ref[k], NT)
        mb = merged.astype(BF16)
        mrg_ref[...] = mb
        tt = _dot(mb, wout_ref[...])
        t_ref[...] = tt
        xmid_ref[...] = x_ref[...] + _rms(tt, g2_ref[...])

    return _pcall(
        body, name="merge_fwd", grid=(s // t,),
        in_specs=[_rows(t, D_MODEL), _rows(t, D_MODEL)] + [_rows(t, BR_W)] * 4 +
                 [_whole(wg.shape), _whole(bg.shape), _whole(wbr.shape), _whole(wout.shape), _whole(g2.shape)],
        out_specs=[_rows(t, D_MODEL)] * 3,
        out_shape=[_sds((s, D_MODEL), F32), _sds((s, D_MODEL), BF16), _sds((s, D_MODEL), F32)],
        vmem=VMEM_LIMIT)(x, hb, *branches, wg, bg, wbr, wout, g2)


def _ffn_fwd(x, g3, w1, w2, g4, t):
    s = x.shape[0]

    def body(x_ref, g3_ref, w1_ref, w2_ref, g4_ref, xo_ref, f_ref):
        h = _rms(x_ref[...], g3_ref[...]).astype(BF16)
        f = jnp.zeros((t, D_MODEL), F32)
        for j in range(D_FF // D_MODEL):
            cols = slice(j * D_MODEL, (j + 1) * D_MODEL)
            r = jnp.square(jnp.maximum(_dot(h, w1_ref[cols, :], NT), 0.0)).astype(BF16)
            f = f + _dot(r, w2_ref[cols, :])
        f_ref[...] = f
        xo_ref[...] = x_ref[...] + _rms(f, g4_ref[...])

    return _pcall(
        body, name="ffn_fwd", grid=(s // t,),
        in_specs=[_rows(t, D_MODEL), _whole(g3.shape), _whole(w1.shape), _whole(w2.shape), _whole(g4.shape)],
        out_specs=[_rows(t, D_MODEL)] * 2,
        out_shape=[_sds((s, D_MODEL), F32)] * 2,
        vmem=VMEM_LIMIT)(x, g3, w1, w2, g4)


def _loss_head(y, target, t):
    s = y.shape[0]
    n = s // t

    def body(y_ref, tg_ref, dy_ref, loss_ref, acc_scr):
        i = pl.program_id(0)
        d = y_ref[...] - tg_ref[...]
        dy_ref[...] = d * (1.0 / D_MODEL)
        _accumulate(acc_scr, jnp.sum(d * d, axis=0, keepdims=True), i == 0)

        @pl.when(i == n - 1)
        def _():
            loss_ref[...] = jnp.full(loss_ref.shape, 0.5 / D_MODEL, F32) * jnp.sum(acc_scr[...])

    return _pcall(
        body, name="loss_head", grid=(n,),
        in_specs=[_rows(t, D_MODEL)] * 2,
        out_specs=[_rows(t, D_MODEL), _acc((8, LANES))],
        out_shape=[_sds((s, D_MODEL), F32), _sds((8, LANES), F32)],
        scratch=[pltpu.VMEM((1, D_MODEL), F32)])(y, target)


def _ffn_bwd_norms(x_mid, f, dxo, g3, g4, t):
    s = x_mid.shape[0]

    def body(x_ref, f_ref, dxo_ref, g3_ref, g4_ref, h2_ref, df_ref, dg4_ref):
        h2_ref[...] = _rms(x_ref[...], g3_ref[...]).astype(BF16)
        df, dg4 = _rms_bwd(f_ref[...], g4_ref[...], dxo_ref[...])
        df_ref[...] = df.astype(BF16)
        _accumulate(dg4_ref, dg4, pl.program_id(0) == 0)

    return _pcall(
        body, name="ffn_bwd_norms", grid=(s // t,),
        in_specs=[_rows(t, D_MODEL)] * 3 + [_whole(g3.shape), _whole(g4.shape)],
        out_specs=[_rows(t, D_MODEL), _rows(t, D_MODEL), _acc((1, D_MODEL))],
        out_shape=[_sds((s, D_MODEL), BF16), _sds((s, D_MODEL), BF16), _sds((1, D_MODEL), F32)])(x_mid, f, dxo, g3, g4)


def _ffn_bwd_weights(h2, df, w1, w2, t):
    s = h2.shape[0]
    blk = D_MODEL
    n = s // t

    def body(h2_ref, df_ref, w1_ref, w2_ref, da_ref, dw1_ref, dw2_ref, acc1, acc2):
        first, last = pl.program_id(1) == 0, pl.program_id(1) == n - 1
        h2v, dfv = h2_ref[...], df_ref[...]
        rl = jnp.maximum(_dot(h2v, w1_ref[...], NT), 0.0)
        _accumulate_out(acc2, dw2_ref, _dot((rl * rl).astype(BF16), dfv, TN), first, last)
        da = (_dot(dfv, w2_ref[...], NT) * (2.0 * rl)).astype(BF16)
        da_ref[...] = da
        _accumulate_out(acc1, dw1_ref, _dot(da, h2v, TN), first, last)

    tok = pl.BlockSpec((t, D_MODEL), lambda j, i: (i, 0))
    hid = pl.BlockSpec((blk, D_MODEL), lambda j, i: (j, 0))
    return _pcall(
        body, name="ffn_bwd_weights", grid=(D_FF // blk, n),
        in_specs=[tok, tok, hid, hid],
        out_specs=[pl.BlockSpec((t, blk), lambda j, i: (i, j)), hid, hid],
        out_shape=[_sds((s, D_FF), BF16), _sds((D_FF, D_MODEL), BF16), _sds((D_FF, D_MODEL), BF16)],
        scratch=[pltpu.VMEM((blk, D_MODEL), F32)] * 2,
        vmem=VMEM_LIMIT)(h2, df, w1, w2)


def _ffn_bwd_input(da, w1, x_mid, dxo, g3, t):
    s = x_mid.shape[0]

    def body(da_ref, w1_ref, x_ref, dxo_ref, g3_ref, dx_ref, dg3_ref):
        dx, dg3 = _rms_bwd(x_ref[...], g3_ref[...], _dot(da_ref[...], w1_ref[...]))
        dx_ref[...] = dxo_ref[...] + dx
        _accumulate(dg3_ref, dg3, pl.program_id(0) == 0)

    return _pcall(
        body, name="ffn_bwd_input", grid=(s // t,),
        in_specs=[_rows(t, D_FF), _whole(w1.shape), _rows(t, D_MODEL), _rows(t, D_MODEL), _whole(g3.shape)],
        out_specs=[_rows(t, D_MODEL), _acc((1, D_MODEL))],
        out_shape=[_sds((s, D_MODEL), F32), _sds((1, D_MODEL), F32)],
        vmem=VMEM_LIMIT)(da, w1, x_mid, dxo, g3)


def _merge_bwd_out(tt, dxmid, g2, wout, merged, t):
    s = tt.shape[0]
    n = s // t

    def body(t_ref, dx_ref, g2_ref, wout_ref, mrg_ref, dm_ref, dwout_ref, dg2_ref, acc):
        first, last = pl.program_id(0) == 0, pl.program_id(0) == n - 1
        dt, dg2 = _rms_bwd(t_ref[...], g2_ref[...], dx_ref[...])
        dtb = dt.astype(BF16)
        dm_ref[...] = _dot(dtb, wout_ref[...], NT)
        _accumulate_out(acc, dwout_ref, _dot(mrg_ref[...], dtb, TN), first, last)
        _accumulate(dg2_ref, dg2, first)

    return _pcall(
        body, name="merge_bwd_out", grid=(n,),
        in_specs=[_rows(t, D_MODEL), _rows(t, D_MODEL), _whole(g2.shape), _whole(wout.shape), _rows(t, D_MODEL)],
        out_specs=[_rows(t, D_MODEL), _acc((D_MODEL, D_MODEL)), _acc((1, D_MODEL))],
        out_shape=[_sds((s, D_MODEL), F32), _sds((D_MODEL, D_MODEL), BF16), _sds((1, D_MODEL), F32)],
        scratch=[pltpu.VMEM((D_MODEL, D_MODEL), F32)],
        vmem=VMEM_LIMIT)(tt, dxmid, g2, wout, merged)


def _merge_bwd_branch(k, hb, br, dm, wg, bg, wbr, t):
    s = hb.shape[0]
    n = s // t

    def body(hb_ref, br_ref, dm_ref, wg_ref, bg_ref, wbr_ref, dbr_ref, dpre_ref, dwg_ref, dbg_ref, dwbr_ref,
             acc_g, acc_br):
        first, last = pl.program_id(0) == 0, pl.program_id(0) == n - 1
        hbv, brv, dmv, wbrv = hb_ref[...], br_ref[...], dm_ref[...], wbr_ref[0]
        gate = _sigmoid(_dot(hbv, wg_ref[...], NT) + bg_ref[...])
        dy = (dmv * gate).astype(BF16)
        dpre = dmv * _dot(brv, wbrv, NT) * gate * (1.0 - gate)
        dpb = dpre.astype(BF16)
        dpre_ref[...] = dpb
        dbr_ref[...] = _dot(dy, wbrv)
        _accumulate_out(acc_br, dwbr_ref, _dot(dy, brv, TN), first, last)
        _accumulate_out(acc_g, dwg_ref, _dot(dpb, hbv, TN), first, last)
        _accumulate(dbg_ref, jnp.sum(dpre, axis=0, keepdims=True), first)

    return _pcall(
        body, name=f"merge_bwd_branch{k}", grid=(n,),
        in_specs=[_rows(t, D_MODEL), _rows(t, BR_W), _rows(t, D_MODEL),
                  pl.BlockSpec((D_MODEL, D_MODEL), lambda i: (k, 0)), pl.BlockSpec((1, D_MODEL), lambda i: (0, k)),
                  pl.BlockSpec((1, D_MODEL, BR_W), lambda i: (k, 0, 0))],
        out_specs=[_rows(t, BR_W), _rows(t, D_MODEL), _acc((D_MODEL, D_MODEL)), _acc((1, D_MODEL)), _acc((D_MODEL, BR_W))],
        out_shape=[_sds((s, BR_W), F32), _sds((s, D_MODEL), BF16), _sds((D_MODEL, D_MODEL), BF16),
                   _sds((1, D_MODEL), F32), _sds((D_MODEL, BR_W), BF16)],
        scratch=[pltpu.VMEM((D_MODEL, D_MODEL), F32), pltpu.VMEM((D_MODEL, BR_W), F32)],
        vmem=VMEM_LIMIT)(hb, br, dm, wg, bg, wbr)


def _mixers_bwd(pm, dbsg, dbcv, dbpl, ln_g, ln_b, wm, wmt, sgb, conv_w, wp, pscale, t):
    s = pm.shape[0]
    n = s // t
    nb = t // HALO

    def body(pm_ref, before_ref, after_ref, dsg_ref, dcv_ref, dcva_ref, dpl_ref, dpla_ref, lng_ref, lnb_ref,
             wm_ref, wmt_ref, sgb_ref, cw_ref, wp_ref, ps_ref,
             dpm_ref, dlng_ref, dlnb_ref, dwm_ref, dsgb_ref, dcw_ref, dwp_ref, dps_ref):
        i = pl.program_id(0)
        first = i == 0
        before = jnp.where(i > 0, before_ref[...], 0.0)
        after = jnp.where(i < n - 1, after_ref[...], 0.0)

        u_raw, v_raw = pm_ref[:, 0:256], pm_ref[:, 256:512]
        lng, lnb = lng_ref[...], lnb_ref[...]
        u, gelu_u_vjp = jax.vjp(_gelu, u_raw)
        vn, norm_vjp = jax.vjp(lambda v_, g_, b_: _layernorm(_gelu(v_), g_, b_), v_raw, lng, lnb)
        vnb = vn.astype(BF16)
        dsg = dsg_ref[...]
        grp = _lane_group((SG_CHUNK, BR_W))
        tri = (lax.broadcasted_iota(jnp.int32, (SG_CHUNK, SG_CHUNK), 1)
               <= lax.broadcasted_iota(jnp.int32, (SG_CHUNK, SG_CHUNK), 0))
        du_parts, dvn_parts = [], []
        dsgb = jnp.zeros((SG_CHUNK, BR_W), F32)
        dwm = [jnp.zeros((SG_CHUNK, SG_CHUNK), F32) for _ in range(SG_GROUPS)]
        for c in range(t // SG_CHUNK):
            rows = slice(c * SG_CHUNK, (c + 1) * SG_CHUNK)
            mix = _sg_mix(wm_ref, vnb[rows], sgb_ref[...])
            du_parts.append(dsg[rows] * mix)
            ds = dsg[rows] * u[rows]
            dsgb = dsgb + ds
            dsb = [jnp.where(grp == g, ds, 0.0).astype(BF16) for g in range(SG_GROUPS)]
            for g in range(SG_GROUPS):
                dwm[g] = dwm[g] + _dot(dsb[g], vnb[rows], NT)
            dvn_parts.append(_group_select([_dot(wmt_ref[g], dsb[g]) for g in range(SG_GROUPS)]))
        (du_raw,) = gelu_u_vjp(jnp.concatenate(du_parts, axis=0))
        dv_raw, dlng, dlnb = norm_vjp(jnp.concatenate(dvn_parts, axis=0))
        dpm_ref[:, 0:256] = du_raw.astype(BF16)
        dpm_ref[:, 256:512] = dv_raw.astype(BF16)
        _accumulate(dlng_ref, dlng, first)
        _accumulate(dlnb_ref, dlnb, first)
        _accumulate(dsgb_ref, dsgb, first)
        for g in range(SG_GROUPS):
            _accumulate(dwm_ref.at[g], jnp.where(tri, dwm[g], 0.0), first)

        xin, bg, cg = pm_ref[:, 512:768], pm_ref[:, 768:1024], pm_ref[:, 1024:1280]
        z = cg * xin
        zext = jnp.concatenate([before[:, 1024:1280] * before[:, 512:768], z], axis=0)
        z1, z2 = _shift_down(zext, 1, t), _shift_down(zext, 2, t)
        w0, w1, w2 = cw_ref[0:1, :], cw_ref[1:2, :], cw_ref[2:3, :]
        dcv = dcv_ref[...]
        y = w0 * z2 + w1 * z1 + w2 * z
        dy = dcv * bg
        dyext = jnp.concatenate([dy, jnp.where(i < n - 1, dcva_ref[...], 0.0) * after[:, 768:1024]], axis=0)
        dz = w2 * dy + w1 * _shift_up(dyext, 1, t) + w0 * _shift_up(dyext, 2, t)
        dpm_ref[:, 512:768] = (dz * cg).astype(BF16)
        dpm_ref[:, 768:1024] = (dcv * y).astype(BF16)
        dpm_ref[:, 1024:1280] = (dz * xin).astype(BF16)
        dcw = jnp.concatenate([jnp.sum(dy * z2, axis=0, keepdims=True), jnp.sum(dy * z1, axis=0, keepdims=True),
                               jnp.sum(dy * z, axis=0, keepdims=True)], axis=0)
        _accumulate(dcw_ref, jnp.concatenate([dcw, jnp.zeros((8 - CONV_K, BR_W), F32)], axis=0), first)

        p = pm_ref[:, 1280:1536]
        tok = i * t + lax.broadcasted_iota(jnp.int32, (t, 1), 0)
        sums = _pool_windows(jnp.concatenate([before[:, 1280:1536], p], axis=0))
        pooled = _group_select([sw[HALO:HALO + t] / cnt - p for sw, cnt in zip(sums, _pool_counts(tok))]).astype(BF16)
        mixed = _dot(pooled, wp_ref[...])
        dpl = dpl_ref[...]
        ps = ps_ref[...]
        dmix = (dpl * ps).astype(BF16)
        _accumulate(dps_ref, jnp.sum(dpl * mixed, axis=0, keepdims=True), first)
        _accumulate(dwp_ref, _dot(pooled, dmix, TN), first)
        dmix_after = (jnp.where(i < n - 1, dpla_ref[...], 0.0) * ps).astype(BF16)
        dpo = _dot(dmix, wp_ref[...], NT)
        dpo_ext = jnp.concatenate([dpo, _dot(dmix_after, wp_ref[...], NT)], axis=0)
        tok_ext = i * t + lax.broadcasted_iota(jnp.int32, (t + HALO, 1), 0)
        dp_groups = []
        for g, (win, cnt) in enumerate(zip(POOL_WINDOWS, _pool_counts(tok_ext))):
            e = dpo_ext / cnt
            acc = e
            span = 1
            while span < win:
                acc = acc + pltpu.roll(acc, t + HALO - span, 0)
                span *= 2
            dp_groups.append(acc[0:t] - dpo)
        dpm_ref[:, 1280:1536] = _group_select(dp_groups).astype(BF16)

    row_blk = lambda w: pl.BlockSpec((t, w), lambda i: (i, 0))
    after_blk = lambda w: pl.BlockSpec((HALO, w), lambda i: (jnp.minimum((i + 1) * nb, n * nb - 1), 0))
    return _pcall(
        body, name="mixers_bwd", grid=(n,),
        in_specs=[row_blk(PM_W), _halo_before(t), after_blk(PM_W), row_blk(BR_W), row_blk(BR_W), after_blk(BR_W),
                  row_blk(BR_W), after_blk(BR_W), _whole(ln_g.shape), _whole(ln_b.shape), _whole(wm.shape),
                  _whole(wmt.shape), _whole(sgb.shape), _whole(conv_w.shape), _whole(wp.shape), _whole(pscale.shape)],
        out_specs=[row_blk(PM_W), _acc((1, BR_W)), _acc((1, BR_W)), _acc((SG_GROUPS, SG_CHUNK, SG_CHUNK)),
                   _acc((SG_CHUNK, BR_W)), _acc((8, BR_W)), _acc((BR_W, BR_W)), _acc((1, BR_W))],
        out_shape=[_sds((s, PM_W), BF16), _sds((1, BR_W), F32), _sds((1, BR_W), F32),
                   _sds((SG_GROUPS, SG_CHUNK, SG_CHUNK), F32), _sds((SG_CHUNK, BR_W), F32), _sds((8, BR_W), F32),
                   _sds((BR_W, BR_W), F32), _sds((1, BR_W), F32)],
        vmem=VMEM_LIMIT)(pm, pm, pm, dbsg, dbcv, dbcv, dbpl, dbpl, ln_g, ln_b, wm, wmt, sgb, conv_w, wp, pscale)


def _head_delta(o, do):
    rows = o.shape[0]
    prod = o.astype(F32) * do
    lane = lax.broadcasted_iota(jnp.int32, (rows, LANES), 1)
    out = []
    for h in range(N_HEADS):
        pair = prod[:, (h // 2) * LANES:(h // 2 + 1) * LANES]
        mine = (lane < D_V) if h % 2 == 0 else (lane >= D_V)
        out.append(jnp.broadcast_to(jnp.sum(jnp.where(mine, pair, 0.0), axis=1, keepdims=True), (rows, LANES)))
    return out


def _head_do(do, h):
    lane = lax.broadcasted_iota(jnp.int32, (do.shape[0], LANES), 1)
    mine = (lane < D_V) if h % 2 == 0 else (lane >= D_V)
    pair = do[:, (h // 2) * LANES:(h // 2 + 1) * LANES]
    return jnp.where(mine, pair, 0.0).astype(BF16)


def _flash_bwd(q, k, v, o, do, lse, t, riding):
    s = v.shape[0]
    n = s // t
    qi_tab, kj_tab = _causal_steps(n, key_major=True)
    n_steps = int(qi_tab.shape[0])
    whole = slice(0, t)

    def body(qi_ref, kj_ref, q_ref, k_ref, v_ref, o_ref, do_ref, lse_ref, x_ref, dq_ref, dk_ref, dv_ref, land_ref,
             dv_scr, send_sems, recv_sems, local_sem):
        step_id = pl.program_id(0)
        qi, kj = qi_ref[step_id], kj_ref[step_id]
        at_end = _ride_along(riding, step_id, n_steps, (x_ref, land_ref, send_sems, recv_sems, local_sem))

        @pl.when(step_id == 0)
        def _():
            dq_ref[...] = jnp.zeros(dq_ref.shape, F32)

        @pl.when(qi == kj)
        def _():
            dk_ref[...] = jnp.zeros(dk_ref.shape, F32)
            dv_scr[...] = jnp.zeros(dv_scr.shape, F32)

        def step(masked, qr, kr):
            dov = do_ref[qr, :]
            delta = _head_delta(o_ref[qr, :], dov)
            reps = (kr.stop - kr.start) // LANES
            q_rows = pl.ds(pl.multiple_of(qi * t + qr.start, t // 2), qr.stop - qr.start)
            for h in range(N_HEADS):
                qh, kh = q_ref[h, qr, :], k_ref[h, kr, :]
                p = jnp.exp(_causal_scores(qh, kh, masked) - jnp.tile(lse_ref[h, qr, :], (1, reps)))
                pair = slice((h // 2) * LANES, (h // 2 + 1) * LANES)
                dv_scr[h, kr, :] += _dot(p.astype(BF16), dov[:, pair].astype(BF16), TN)
                dp = _dot(_head_do(dov, h), v_ref[kr, pair], NT)
                ds = (p * (dp - jnp.tile(delta[h], (1, reps)))).astype(BF16)
                dk_ref[h, kr, :] += _dot(ds, qh, TN)
                dq_ref[h, q_rows, :] += _dot(ds, kh)

        @pl.when(qi > kj)
        def _():
            step(False, whole, whole)

        @pl.when(qi == kj)
        def _():
            for masked, qr, kr in _diagonal_parts(t):
                step(masked, qr, kr)

        @pl.when(qi == n - 1)
        def _():
            lane = lax.broadcasted_iota(jnp.int32, (t, LANES), 1)
            for pr in range(N_HEADS // 2):
                dv_ref[:, pr * LANES:(pr + 1) * LANES] = jnp.where(lane < D_V, dv_scr[2 * pr], dv_scr[2 * pr + 1]).astype(BF16)

        at_end()

    q_blk = pl.BlockSpec((N_HEADS, t, HEAD_PAD), lambda i, qi, kj: (0, qi[i], 0))
    k_blk = pl.BlockSpec((N_HEADS, t, HEAD_PAD), lambda i, qi, kj: (0, kj[i], 0))
    v_blk = pl.BlockSpec((t, BR_W), lambda i, qi, kj: (kj[i], 0))
    o_blk = pl.BlockSpec((t, BR_W), lambda i, qi, kj: (qi[i], 0))
    dq_sds = _sds((N_HEADS, s, HEAD_PAD), F32)
    return _pcall(
        body, name="flash_bwd_exchanging", grid=(n_steps,), prefetch=2,
        in_specs=[q_blk, k_blk, v_blk, o_blk, o_blk, q_blk, HBM_REF],
        out_specs=[_whole(dq_sds.shape), k_blk, v_blk, HBM_REF],
        out_shape=[dq_sds, dq_sds, _sds((s, BR_W), BF16), riding[2]],
        scratch=[pltpu.VMEM((N_HEADS, t, LANES), F32)] + list(EXCHANGE_SEMS),
        vmem=VMEM_LIMIT)(qi_tab, kj_tab, q, k, v, o, do, lse, riding[1])


def _attn_prep_bwd(pa, dq, dk, dv, cosf, sins, qn, kvn, wq, wqs, wk, wv, t):
    s = pa.shape[0]

    def body(pa_ref, dq_ref, dk_ref, dv_ref, cos_ref, sin_ref, qn_ref, kvn_ref, wq_ref, wqs_ref, wk_ref, wv_ref,
             dpa_ref, dwq_ref, dwqs_ref, dwk_ref, dwv_ref, dqn_ref, dkvn_ref):
        first = pl.program_id(0) == 0
        cosv, sinv = cos_ref[...], sin_ref[...]
        cq, ckv = pa_ref[:, 0:Q_RANK], pa_ref[:, Q_RANK:Q_RANK + KV_RANK]
        cqn = _rms(cq, qn_ref[...]).astype(BF16)
        ckvn = _rms(ckv, kvn_ref[...]).astype(BF16)
        dvv = dv_ref[...]
        _accumulate(dwv_ref, _dot(dvv, ckvn, TN), first)
        dk_rope = dk_ref[0]
        for h in range(1, N_HEADS):
            dk_rope = dk_rope + dk_ref[h]
        cos_s, sin_s = cosv * SM_SCALE, sinv * SM_SCALE
        dqa = jnp.concatenate([(dq_ref[h] * cos_s).astype(BF16) for h in range(N_HEADS)], axis=1)
        dqs = jnp.concatenate([(dq_ref[h] * sin_s).astype(BF16) for h in range(N_HEADS)], axis=1)
        dkb = jnp.concatenate([dk_ref[h].astype(BF16) for h in range(N_HEADS)], axis=1)
        _accumulate(dwq_ref, _dot(dqa, cqn, TN), first)
        _accumulate(dwqs_ref, _dot(dqs, cqn, TN), first)
        _accumulate(dwk_ref, _dot(dkb, ckvn, TN), first)
        dcqn = _dot(dqa, wq_ref[...]) + _dot(dqs, wqs_ref[...])
        dckvn = _dot(dvv, wv_ref[...]) + _dot(dkb, wk_ref[...])
        dcq, dqn = _rms_bwd(cq, qn_ref[...], dcqn)
        dckv, dkvn = _rms_bwd(ckv, kvn_ref[...], dckvn)
        _accumulate(dqn_ref, dqn, first)
        _accumulate(dkvn_ref, dkvn, first)
        dpa_ref[:, 0:Q_RANK] = dcq.astype(BF16)
        dpa_ref[:, Q_RANK:Q_RANK + KV_RANK] = dckv.astype(BF16)
        dpa_ref[:, 384:512] = (dk_rope * cosv).astype(BF16)
        dpa_ref[:, 512:640] = (dk_rope * sinv).astype(BF16)

    head_blk = pl.BlockSpec((N_HEADS, t, HEAD_PAD), lambda i: (0, i, 0))
    return _pcall(
        body, name="attn_prep_bwd", grid=(s // t,),
        in_specs=[_rows(t, PA_W), head_blk, head_blk, _rows(t, BR_W), _rows(t, LANES), _rows(t, LANES),
                  _whole(qn.shape), _whole(kvn.shape), _whole(wq.shape), _whole(wqs.shape), _whole(wk.shape), _whole(wv.shape)],
        out_specs=[_rows(t, PA_W), _acc(wq.shape), _acc(wqs.shape), _acc(wk.shape), _acc(wv.shape),
                   _acc(qn.shape), _acc(kvn.shape)],
        out_shape=[_sds((s, PA_W), BF16), _sds(wq.shape, F32), _sds(wqs.shape, F32), _sds(wk.shape, F32),
                   _sds(wv.shape, F32), _sds(qn.shape, F32), _sds(kvn.shape, F32)],
        vmem=VMEM_LIMIT)(pa, dq, dk, dv, cosf, sins, qn, kvn, wq, wqs, wk, wv)


def _inproj_bwd(x, g1, dxmid, dpa, dpm, dpres, w_in, wg, t):
    s = x.shape[0]
    n = s // t

    def body(x_ref, g1_ref, dxm_ref, dpa_ref, dpm_ref, d0_ref, d1_ref, d2_ref, d3_ref, w_ref, wg_ref,
             dx_ref, dg1_ref, dwa_ref, dwm_ref, acc_a, acc_m):
        first, last = pl.program_id(0) == 0, pl.program_id(0) == n - 1
        xv, g1v = x_ref[...], g1_ref[...]
        hb = _rms(xv, g1v).astype(BF16)
        dpav, dpmv = dpa_ref[...], dpm_ref[...]
        dh = _dot(dpav, w_ref[:PA_W, :]) + _dot(dpmv, w_ref[PA_W:, :])
        for k, d_ref in enumerate((d0_ref, d1_ref, d2_ref, d3_ref)):
            dh = dh + _dot(d_ref[...], wg_ref[k * D_MODEL:(k + 1) * D_MODEL, :])
        dx, dg1 = _rms_bwd(xv, g1v, dh)
        dx_ref[...] = dxm_ref[...] + dx
        _accumulate(dg1_ref, dg1, first)
        _accumulate_out(acc_a, dwa_ref, _dot(dpav, hb, TN), first, last)
        _accumulate_out(acc_m, dwm_ref, _dot(dpmv, hb, TN), first, last)

    return _pcall(
        body, name="inproj_bwd", grid=(n,),
        in_specs=[_rows(t, D_MODEL), _whole(g1.shape), _rows(t, D_MODEL), _rows(t, PA_W), _rows(t, PM_W)] +
                 [_rows(t, D_MODEL)] * 4 + [_whole(w_in.shape), _whole(wg.shape)],
        out_specs=[_rows(t, D_MODEL), _acc((1, D_MODEL)), _acc((PA_W, D_MODEL)), _acc((PM_W, D_MODEL))],
        out_shape=[_sds((s, D_MODEL), F32), _sds((1, D_MODEL), F32), _sds((PA_W, D_MODEL), BF16), _sds((PM_W, D_MODEL), BF16)],
        scratch=[pltpu.VMEM((PA_W, D_MODEL), F32), pltpu.VMEM((PM_W, D_MODEL), F32)],
        vmem=VMEM_LIMIT)(x, g1, dxmid, dpa, dpm, *dpres, w_in, wg)


def _my_place():
    return lax.axis_index("x"), lax.axis_index("y"), lax.axis_index("c")


def _flip(place, k):
    x, y, c = place
    return (1 - x if k & 4 else x, 1 - y if k & 2 else y, 1 - c if k & 1 else c)


def _rank(place):
    x, y, c = place
    return 4 * x + 2 * y + c


EXCHANGE_SEMS = (pltpu.SemaphoreType.DMA((7,)), pltpu.SemaphoreType.DMA((7,)), pltpu.SemaphoreType.DMA)
HBM_REF = pl.BlockSpec(memory_space=pl.ANY)


def _gather_plan(x_ref, out_ref, send_sems, recv_sems, local_sem):
    me = _my_place()
    sibling = _flip(me, 1)
    chips = (4, 2, 6)

    def copy(k, src_place, to, src=None):
        slot = out_ref.at[_rank(src_place)]
        return pltpu.make_async_remote_copy(
            src_ref=slot if src is None else src, dst_ref=slot, send_sem=send_sems.at[k], recv_sem=recv_sems.at[k],
            device_id=to, device_id_type=MESH_ID)

    mine = pltpu.make_async_copy(x_ref, out_ref.at[_rank(me)], local_sem)
    first = [copy(0, me, sibling, src=x_ref)] + [copy(1 + j, me, _flip(me, kc), src=x_ref) for j, kc in enumerate(chips)]
    passed = [copy(4 + j, _flip(me, kc), sibling) for j, kc in enumerate(chips)]

    def start():
        mine.start()
        for cp in first:
            cp.start()

    def forward():
        for j, kc in enumerate(chips):
            copy(1 + j, _flip(me, kc), me).wait_recv()
            passed[j].start()

    def finish():
        copy(0, sibling, me).wait_recv()
        for j, kc in enumerate(chips):
            copy(4 + j, _flip(sibling, kc), me).wait_recv()
        for cp in first + passed:
            cp.wait_send()
        mine.wait()

    return start, forward, finish


def _exchange_plan(x_ref, out_ref, send_sems, recv_sems, local_sem):
    me = _my_place()
    my_rank = _rank(me)
    mine = pltpu.make_async_copy(x_ref.at[my_rank], out_ref.at[my_rank], local_sem)
    sends = [pltpu.make_async_remote_copy(
        src_ref=x_ref.at[_rank(_flip(me, k))], dst_ref=out_ref.at[my_rank], send_sem=send_sems.at[k - 1],
        recv_sem=recv_sems.at[k - 1], device_id=_flip(me, k), device_id_type=MESH_ID) for k in range(1, N_DEV)]

    def start():
        mine.start()
        for cp in sends:
            cp.start()

    def forward():
        pass

    def finish():
        for k in range(1, N_DEV):
            slot = out_ref.at[_rank(_flip(me, k))]
            pltpu.make_async_remote_copy(
                src_ref=slot, dst_ref=slot, send_sem=send_sems.at[k - 1], recv_sem=recv_sems.at[k - 1],
                device_id=_flip(me, k), device_id_type=MESH_ID).wait_recv()
        for cp in sends:
            cp.wait_send()
        mine.wait()

    return start, forward, finish


def _exchange_alone(plan, src, out_sds, name):
    def body(x_ref, out_ref, send_sems, recv_sems, local_sem):
        start, forward, finish = plan(x_ref, out_ref, send_sems, recv_sems, local_sem)
        start()
        forward()
        finish()

    return pl.pallas_call(body, name=name, out_shape=out_sds, in_specs=[HBM_REF], out_specs=HBM_REF,
                          scratch_shapes=list(EXCHANGE_SEMS))(src)


def _all_gather(shard, name):
    return _exchange_alone(_gather_plan, shard, _sds((N_DEV,) + shard.shape, shard.dtype), name)


def _all_to_all(blocks, name):
    return _exchange_alone(_exchange_plan, blocks, _sds(blocks.shape, blocks.dtype), name)


def _riding(plan, src, out_sds):
    return (plan, src, out_sds)


def _ride_along(riding, step_id, n_steps, refs):
    start, forward, finish = riding[0](*refs)

    @pl.when(step_id == 0)
    def _():
        start()

    @pl.when(step_id == (3 * n_steps) // 4)
    def _():
        forward()

    def at_end():
        @pl.when(step_id == n_steps - 1)
        def _():
            finish()

    return at_end


def _sum_slots(parts, name):
    _, r, c = parts.shape
    t = math.gcd(r, ROW_PAD)

    def body(p_ref, o_ref):
        acc = p_ref[0].astype(F32)
        for d in range(1, N_DEV):
            acc = acc + p_ref[d].astype(F32)
        o_ref[...] = acc

    return _pcall(
        body, name=name, grid=(r // t,),
        in_specs=[pl.BlockSpec((N_DEV, t, c), lambda i: (0, i, 0))], out_specs=_rows(t, c),
        out_shape=_sds((r, c), F32))(parts)


def _adamw(w, g, m, v, name):
    depth, r, c = w.shape
    t = r if r <= ROW_PAD else math.gcd(r, ROW_PAD)
    blk = pl.BlockSpec((1, t, c), lambda l, i: (l, i, 0))

    def body(w_ref, g_ref, m_ref, v_ref, d_ref, nm_ref, nv_ref):
        gv = g_ref[...]
        nm = ADAM_B1 * m_ref[...] + (1.0 - ADAM_B1) * gv
        nv = ADAM_B2 * v_ref[...] + (1.0 - ADAM_B2) * jnp.square(gv)
        m_hat = nm / (1.0 - ADAM_B1 ** ADAM_STEP)
        v_hat = nv / (1.0 - ADAM_B2 ** ADAM_STEP)
        d_ref[...] = -ADAM_LR * (m_hat / (jnp.sqrt(v_hat) + ADAM_EPS) + ADAM_WD * w_ref[...])
        nm_ref[...] = nm
        nv_ref[...] = nv

    return _pcall(
        body, name=name, grid=(depth, r // t),
        in_specs=[blk] * 4, out_specs=[blk] * 3, out_shape=[_sds(w.shape, F32)] * 3)(w, g, m, v)


BEFORE_ATTN = ("w_in", "w_uq", "w_ukv")
AFTER_ATTN = ("w_br_mla", "w_br_sg", "w_br_conv", "w_br_pool", "w_out", "w_ff1", "w_ff2", "conv_w")
SHARDED = BEFORE_ATTN + AFTER_ATTN
ROW_SHARDED = ("w_out", "w_ff2")
REPLICATED = ("norm_mix_pre", "gate_b", "q_norm", "kv_norm", "sg_ln_g", "sg_ln_b", "sg_w", "sg_b", "pool_w",
              "pool_scale", "norm_mix_post", "norm_ffn_pre", "norm_ffn_post")
ROW_PAD = 256
GRAD_ROW_PAD = 64
PART_ROWS = 16


def _pack_rows(arrays, dtype, multiple, lead=()):
    rows, offsets, at = [], [], 0
    zero_pad = ((0, 0),) * len(lead)
    for a in arrays:
        if a.ndim == len(lead) + 2 and a.shape[-1] == D_MODEL:
            n = a.shape[-2]
            n_pad = -(-n // PART_ROWS) * PART_ROWS
            rows.append(jnp.pad(a.astype(dtype), zero_pad + ((0, n_pad - n), (0, 0))))
        else:
            flat = a.reshape(lead + (-1,)).astype(dtype)
            n = -(-flat.shape[-1] // D_MODEL)
            n_pad = -(-n // PART_ROWS) * PART_ROWS
            flat = jnp.pad(flat, zero_pad + ((0, n_pad * D_MODEL - flat.shape[-1]),))
            rows.append(flat.reshape(lead + (n_pad, D_MODEL)))
        offsets.append((at, n))
        at += n_pad
    pad = -at % multiple
    if pad:
        rows.append(jnp.zeros(lead + (pad, D_MODEL), dtype))
    return jnp.concatenate(rows, axis=len(lead)), offsets


def _unpack_rows(buf, offsets, shapes):
    lead = buf.shape[:-2]
    out = []
    for (at, n), shape in zip(offsets, shapes):
        if tuple(shape) == (n, D_MODEL):
            out.append(buf[..., at:at + n, :])
            continue
        size = math.prod(shape)
        flat = buf[..., at:at + n, :].reshape(lead + (n * D_MODEL,))[..., :size]
        out.append(flat.reshape(lead + tuple(shape)))
    return out


def _pack_shards(p, depth, names):
    arrays = []
    for n in names:
        a = p[n][0] if n in ROW_SHARDED else jnp.swapaxes(p[n][0], -1, -2)
        arrays.append(lax.bitcast_convert_type(a, BF16) if n == "conv_w" else a.astype(BF16))
    packed, offsets = _pack_rows(arrays, BF16, GRAD_ROW_PAD, lead=(depth,))
    return packed, offsets, [a.shape[1:] for a in arrays]


def _unpack_layer_weights(names, gathered, offsets, shapes):
    full = {}
    for n, gth in zip(names, _unpack_rows(gathered, offsets, shapes)):
        if n == "conv_w":
            gth = lax.bitcast_convert_type(gth, F32)
        full[n] = gth.reshape((-1,) + gth.shape[2:])
        if n == "conv_w":
            full[n] = full[n].T
    return full


def _owner_major(grad_sharded_dim_first):
    a = grad_sharded_dim_first
    return a.reshape((N_DEV, a.shape[0] // N_DEV) + a.shape[1:])


def _owner_block_shape(name, shard_shape):
    k, n = shard_shape[-2:]
    return (k, n) if name in ROW_SHARDED else (n, k)


def _natural_shard(name, blocks):
    return blocks if name in ROW_SHARDED else jnp.swapaxes(blocks, -1, -2)


def _swap_halves(a, axis=-1):
    lo, hi = jnp.split(a, 2, axis=axis)
    return jnp.concatenate([hi, lo], axis=axis)


def _pad_rows(a, top, total):
    return jnp.pad(a, ((0, 0),) * (a.ndim - 2) + ((top, total - top - a.shape[-2]), (0, 0)))


def _weights_before_attn(full, small, l):
    w_in = full["w_in"]
    k_r = w_in[384:416]
    w_in_k = jnp.concatenate(
        [w_in[0:384], _pad_rows(k_r, D_NOPE, HEAD_PAD), _pad_rows(_swap_halves(k_r, axis=0), D_NOPE, HEAD_PAD), w_in[416:1952]], axis=0)
    w_uq = full["w_uq"].reshape(N_HEADS, D_NOPE + D_ROPE, Q_RANK)
    wq = _pad_rows(w_uq, 0, HEAD_PAD).reshape(N_HEADS * HEAD_PAD, Q_RANK)
    wqs = _pad_rows(_swap_halves(w_uq[:, D_NOPE:], axis=1), D_NOPE, HEAD_PAD).reshape(N_HEADS * HEAD_PAD, Q_RANK)
    w_ukv = full["w_ukv"].reshape(N_HEADS, D_NOPE + D_V, KV_RANK)
    wk = _pad_rows(w_ukv[:, :D_NOPE], 0, HEAD_PAD).reshape(N_HEADS * HEAD_PAD, KV_RANK)
    wv = w_ukv[:, D_NOPE:].reshape(N_HEADS * D_V, KV_RANK)
    tri = jnp.tril(jnp.ones((SG_CHUNK, SG_CHUNK), bool))
    wm = jnp.where(tri, small["sg_w"][l], 0.0)
    pool_w = small["pool_w"][l]
    wp = jnp.zeros((BR_W, BR_W), F32)
    for g in range(len(POOL_WINDOWS)):
        wp = wp.at[g * POOL_GROUP:(g + 1) * POOL_GROUP, g * POOL_GROUP:(g + 1) * POOL_GROUP].set(pool_w[g])
    vec = lambda name: small[name][l][None, :]
    return dict(
        w_in=w_in_k, wg=w_in[1952:], bg=vec("gate_b"), g1=vec("norm_mix_pre"), qn=vec("q_norm"), kvn=vec("kv_norm"),
        wq=wq, wqs=wqs, wk=wk, wv=wv,
        ln_g=vec("sg_ln_g"), ln_b=vec("sg_ln_b"), wm=wm.astype(BF16), wmt=jnp.swapaxes(wm, 1, 2).astype(BF16),
        sgb=jnp.repeat(small["sg_b"][l].T, POOL_GROUP, axis=1), wp=wp.astype(BF16), pscale=vec("pool_scale"),
        g2=vec("norm_mix_post"), g3=vec("norm_ffn_pre"), g4=vec("norm_ffn_post"))


def _weights_after_attn(full):
    return dict(
        conv_w=full["conv_w"], wbr=jnp.stack([full["w_br_mla"], full["w_br_sg"], full["w_br_conv"], full["w_br_pool"]]),
        wout=full["w_out"], w1=full["w_ff1"], w2=full["w_ff2"])


def _grads_after_attn(g):
    own = _owner_major
    return dict(
        w_br_mla=own(g["dwbr"][0]), w_br_sg=own(g["dwbr"][1]), w_br_conv=own(g["dwbr"][2]), w_br_pool=own(g["dwbr"][3]),
        w_out=own(g["dwout"]), w_ff1=own(g["dw1"]), w_ff2=own(g["dw2"]), conv_w=own(g["dcw"][:CONV_K].T))


def _grads_before_attn(g):
    dwa = g["dwa"]
    rope_rows = slice(D_NOPE, D_NOPE + D_ROPE)
    d_kr = dwa[384:512][rope_rows].astype(F32) + _swap_halves(dwa[512:640][rope_rows], axis=0).astype(F32)
    w_in = jnp.concatenate([dwa[0:384], d_kr.astype(BF16), g["dwm_in"]] + list(g["dwg"]), axis=0)
    dwq, dwqs = (g[n].reshape(N_HEADS, HEAD_PAD, Q_RANK) for n in ("dwq", "dwqs"))
    d_rope = dwq[:, rope_rows] + _swap_halves(dwqs[:, rope_rows], axis=1)
    w_uq = jnp.concatenate([dwq[:, :D_NOPE], d_rope], axis=1).reshape(N_HEADS * (D_NOPE + D_ROPE), Q_RANK)
    dwk = g["dwk"].reshape(N_HEADS, HEAD_PAD, KV_RANK)
    w_ukv = jnp.concatenate([dwk[:, :D_NOPE], g["dwv"].reshape(N_HEADS, D_V, KV_RANK)], axis=1)
    w_ukv = w_ukv.reshape(N_HEADS * (D_NOPE + D_V), KV_RANK)
    pool_w = jnp.stack([g["dwp"][i * POOL_GROUP:(i + 1) * POOL_GROUP, i * POOL_GROUP:(i + 1) * POOL_GROUP]
                        for i in range(len(POOL_WINDOWS))])
    sg_b = g["dsgb"].reshape(SG_CHUNK, SG_GROUPS, POOL_GROUP).sum(axis=-1).T
    own = _owner_major
    return dict(
        w_in=own(w_in), w_uq=own(w_uq), w_ukv=own(w_ukv),
        norm_mix_pre=g["dg1"][0], gate_b=jnp.concatenate([b[0] for b in g["dbg"]]), q_norm=g["dqn"][0], kv_norm=g["dkvn"][0],
        sg_ln_g=g["dlng"][0], sg_ln_b=g["dlnb"][0], sg_w=g["dwm"], sg_b=sg_b, pool_w=pool_w, pool_scale=g["dps"][0],
        norm_mix_post=g["dg2"][0], norm_ffn_pre=g["dg3"][0], norm_ffn_post=g["dg4"][0])


def _rope_tables(positions):
    inv_freq = ROPE_BASE ** (-jnp.arange(0, D_ROPE, 2, dtype=F32) / D_ROPE)
    ang = positions.astype(F32)[:, None] * inv_freq
    cos, sin = jnp.cos(ang), jnp.sin(ang)
    s = positions.shape[0]
    cosf = jnp.concatenate([jnp.ones((s, D_NOPE), F32), cos, cos, jnp.zeros((s, HEAD_PAD - D_NOPE - D_ROPE), F32)], axis=1)
    sins = jnp.concatenate([jnp.zeros((s, D_NOPE), F32), -sin, sin, jnp.zeros((s, HEAD_PAD - D_NOPE - D_ROPE), F32)], axis=1)
    return cosf, sins


def _layer_fwd(x, w, weights_after_attn, cosf, sins, tiles, riding):
    t, ta = tiles["tok"], tiles["attn"]
    hb, pa, pm = _inproj_fwd(x, w["g1"], w["w_in"], tiles["wgrad"])
    q, k, v, v1 = _attn_prep_fwd(pa, cosf, sins, w["qn"], w["kvn"], w["wq"], w["wqs"], w["wk"], w["wv"], tiles["wgrad"])
    o, lse, landed = _flash_fwd(q, k, v1, ta, riding)
    w = {**w, **weights_after_attn(landed)}
    bsg, bcv, bpl = _mixers_fwd(pm, w["ln_g"], w["ln_b"], w["wm"], w["sgb"], w["conv_w"], w["wp"], w["pscale"], tiles["wgrad"])
    x_mid, merged, tt = _merge_fwd(x, hb, (o, bsg, bcv, bpl), w["wg"], w["bg"], w["wbr"], w["wout"], w["g2"], t)
    x_out, f = _ffn_fwd(x_mid, w["g3"], w["w1"], w["w2"], w["g4"], t)
    saved = dict(x=x, hb=hb, pa=pa, pm=pm, q=q, k=k, v=v, o=o, lse=lse, bsg=bsg, bcv=bcv, bpl=bpl, x_mid=x_mid,
                 merged=merged, tt=tt, f=f)
    return x_out, saved, w, landed


def _layer_bwd(dxo, a, w, cosf, sins, tiles, pending):
    t, ta, tb, tw = tiles["tok"], tiles["attn"], tiles["inproj_bwd"], tiles["wgrad"]
    g = {}
    h2, df, g["dg4"] = _ffn_bwd_norms(a["x_mid"], a["f"], dxo, w["g3"], w["g4"], tw)
    da, g["dw1"], g["dw2"] = _ffn_bwd_weights(h2, df, w["w1"], w["w2"], tw)
    dxmid, g["dg3"] = _ffn_bwd_input(da, w["w1"], a["x_mid"], dxo, w["g3"], t)
    dm, g["dwout"], g["dg2"] = _merge_bwd_out(a["tt"], dxmid, w["g2"], w["wout"], a["merged"], tw)
    dbrs, dpres, g["dwg"], g["dbg"], g["dwbr"] = [], [], [], [], []
    for k, br in enumerate((a["o"], a["bsg"], a["bcv"], a["bpl"])):
        dbr, dpre, dwg, dbg, dwbr = _merge_bwd_branch(k, a["hb"], br, dm, w["wg"], w["bg"], w["wbr"], tw)
        dbrs.append(dbr); dpres.append(dpre); g["dwg"].append(dwg); g["dbg"].append(dbg); g["dwbr"].append(dwbr)
    dpm, g["dlng"], g["dlnb"], g["dwm"], g["dsgb"], g["dcw"], g["dwp"], g["dps"] = _mixers_bwd(
        a["pm"], dbrs[1], dbrs[2], dbrs[3], w["ln_g"], w["ln_b"], w["wm"], w["wmt"], w["sgb"], w["conv_w"], w["wp"], w["pscale"], tw)
    after = _grads_after_attn(g)
    send, after_offsets = _pack_rows([after[n] for n in AFTER_ATTN], BF16, GRAD_ROW_PAD, lead=(N_DEV,))
    after_rows = send.shape[1]
    if pending is not None:
        send = jnp.concatenate([send, pending], axis=1)
    riding = _riding(_exchange_plan, send, _sds(send.shape, send.dtype))
    dq, dk, dv, landed = _flash_bwd(a["q"], a["k"], a["v"], a["o"], dbrs[0], a["lse"], ta, riding)
    dpa, g["dwq"], g["dwqs"], g["dwk"], g["dwv"], g["dqn"], g["dkvn"] = _attn_prep_bwd(
        a["pa"], dq, dk, dv, cosf, sins, w["qn"], w["kvn"], w["wq"], w["wqs"], w["wk"], w["wv"], tw)
    dx, g["dg1"], g["dwa"], g["dwm_in"] = _inproj_bwd(a["x"], w["g1"], dxmid, dpa, dpm, dpres, w["w_in"], w["wg"], tb)
    return dx, _grads_before_attn(g), landed, (after_rows, after_offsets)


def _step(p, x, positions, loss_target):
    s = x.shape[0]
    depth = p["w_in"][0].shape[0]
    tiles = dict(tok=min(TOK_TILE, s), attn=min(ATTN_TILE, s), inproj_bwd=min(INPROJ_BWD_TILE, s),
                 wgrad=min(WGRAD_TILE, s))

    small = {n: p[n][0] for n in REPLICATED}
    cosf, sins = _rope_tables(positions)

    pack_b, off_b, shapes_b = _pack_shards(p, depth, BEFORE_ATTN)
    pack_a, off_a, shapes_a = _pack_shards(p, depth, AFTER_ATTN)
    rows_a = pack_a.shape[1]
    before = _all_gather(pack_b[0], "weight_all_gather")
    weights, acts = [], []
    h = x
    for l in range(depth):
        w = _weights_before_attn(_unpack_layer_weights(BEFORE_ATTN, before, off_b, shapes_b), small, l)
        src = pack_a[l] if l + 1 == depth else jnp.concatenate([pack_a[l], pack_b[l + 1]], axis=0)
        riding = _riding(_gather_plan, src, _sds((N_DEV,) + src.shape, src.dtype))

        def after(landed):
            return _weights_after_attn(_unpack_layer_weights(AFTER_ATTN, landed[:, :rows_a], off_a, shapes_a))

        h, saved, w, landed = _layer_fwd(h, w, after, cosf, sins, tiles, riding)
        weights.append(w)
        acts.append(saved)
        before = landed[:, rows_a:] if l + 1 < depth else None
    dh, loss_blk = _loss_head(h, loss_target, tiles["tok"])
    loss = lax.psum(loss_blk[0, 0], ("x", "y", "c"))

    grads_b = [None] * depth
    sum_a, sum_b = [None] * depth, [None] * depth
    pending = None
    for l in reversed(range(depth)):
        dh, grads_b[l], landed, (rows_ga, off_ga) = _layer_bwd(dh, acts[l], weights[l], cosf, sins, tiles, pending)
        summed = _sum_slots(landed, "grad_shard_sum")
        sum_a[l] = summed[:rows_ga]
        if pending is not None:
            sum_b[l + 1] = summed[rows_ga:]
        pending, off_gb = _pack_rows([grads_b[l][n] for n in BEFORE_ATTN], BF16, GRAD_ROW_PAD, lead=(N_DEV,))
    sum_b[0] = _sum_slots(_all_to_all(pending, "grad_all_to_all"), "grad_shard_sum")
    g_shard = {}
    for names, sums, offsets in ((AFTER_ATTN, sum_a, off_ga), (BEFORE_ATTN, sum_b, off_gb)):
        block_shapes = [_owner_block_shape(n, p[n][0].shape) for n in names]
        for n, b in zip(names, _unpack_rows(jnp.stack(sums), offsets, block_shapes)):
            g_shard[n] = _natural_shard(n, b)
    small_grads = [jnp.stack([grads_b[l][n] for l in range(depth)]) for n in REPLICATED]
    small_rows, small_offsets = _pack_rows(small_grads, F32, ROW_PAD)
    g_small_rows = _sum_slots(_all_gather(small_rows, "replicated_grad_all_gather"), "grad_replicated_sum")

    grad, delta, new_m, new_v = {}, {}, {}, {}
    for n in SHARDED:
        w, m, v = p[n]
        grad[n] = g_shard[n]
        delta[n], new_m[n], new_v[n] = _adamw(w, grad[n], m, v, f"adamw_{n}")
    packs = [_pack_rows([p[n][i] for n in REPLICATED], F32, ROW_PAD)[0][None] for i in range(3)]
    d, nm, nv = (a[0] for a in _adamw(packs[0], g_small_rows[None], packs[1], packs[2], "adamw_replicated"))
    shapes = [p[n][0].shape for n in REPLICATED]
    for n, gg, dd, mm, vv in zip(REPLICATED, _unpack_rows(g_small_rows, small_offsets, shapes), _unpack_rows(d, small_offsets, shapes),
                                 _unpack_rows(nm, small_offsets, shapes), _unpack_rows(nv, small_offsets, shapes)):
        grad[n], delta[n], new_m[n], new_v[n] = gg, dd, mm, vv
    return loss, dh, grad, delta, new_m, new_v


WEIGHT_ORDER = ("norm_mix_pre", "w_in", "gate_b", "q_norm", "w_uq", "kv_norm", "w_ukv", "w_br_mla", "sg_ln_g", "sg_ln_b",
                "sg_w", "sg_b", "w_br_sg", "conv_w", "w_br_conv", "pool_w", "pool_scale", "w_br_pool", "w_out",
                "norm_mix_post", "norm_ffn_pre", "w_ff1", "w_ff2", "norm_ffn_post")


def kernel(x, positions, norm_mix_pre, w_in, gate_b, q_norm, w_uq, kv_norm, w_ukv, w_br_mla, sg_ln_g, sg_ln_b, sg_w, sg_b, w_br_sg, conv_w, w_br_conv, pool_w, pool_scale, w_br_pool, w_out, norm_mix_post, norm_ffn_pre, w_ff1, w_ff2, norm_ffn_post, loss_target, m_norm_mix_pre, m_w_in, m_gate_b, m_q_norm, m_w_uq, m_kv_norm, m_w_ukv, m_w_br_mla, m_sg_ln_g, m_sg_ln_b, m_sg_w, m_sg_b, m_w_br_sg, m_conv_w, m_w_br_conv, m_pool_w, m_pool_scale, m_w_br_pool, m_w_out, m_norm_mix_post, m_norm_ffn_pre, m_w_ff1, m_w_ff2, m_norm_ffn_post, v_norm_mix_pre, v_w_in, v_gate_b, v_q_norm, v_w_uq, v_kv_norm, v_w_ukv, v_w_br_mla, v_sg_ln_g, v_sg_ln_b, v_sg_w, v_sg_b, v_w_br_sg, v_conv_w, v_w_br_conv, v_pool_w, v_pool_scale, v_w_br_pool, v_w_out, v_norm_mix_post, v_norm_ffn_pre, v_w_ff1, v_w_ff2, v_norm_ffn_post):
    ws = (norm_mix_pre, w_in, gate_b, q_norm, w_uq, kv_norm, w_ukv, w_br_mla, sg_ln_g, sg_ln_b, sg_w, sg_b, w_br_sg, conv_w,
          w_br_conv, pool_w, pool_scale, w_br_pool, w_out, norm_mix_post, norm_ffn_pre, w_ff1, w_ff2, norm_ffn_post)
    ms = (m_norm_mix_pre, m_w_in, m_gate_b, m_q_norm, m_w_uq, m_kv_norm, m_w_ukv, m_w_br_mla, m_sg_ln_g, m_sg_ln_b, m_sg_w,
          m_sg_b, m_w_br_sg, m_conv_w, m_w_br_conv, m_pool_w, m_pool_scale, m_w_br_pool, m_w_out, m_norm_mix_post,
          m_norm_ffn_pre, m_w_ff1, m_w_ff2, m_norm_ffn_post)
    vs = (v_norm_mix_pre, v_w_in, v_gate_b, v_q_norm, v_w_uq, v_kv_norm, v_w_ukv, v_w_br_mla, v_sg_ln_g, v_sg_ln_b, v_sg_w,
          v_sg_b, v_w_br_sg, v_conv_w, v_w_br_conv, v_pool_w, v_pool_scale, v_w_br_pool, v_w_out, v_norm_mix_post,
          v_norm_ffn_pre, v_w_ff1, v_w_ff2, v_norm_ffn_post)
    p = {n: (w, m, v) for n, w, m, v in zip(WEIGHT_ORDER, ws, ms, vs)}
    loss, grad_x, grad, delta, new_m, new_v = _step(p, x[0], positions[0], loss_target[0])
    return (loss, grad_x[None], *[grad[n] for n in WEIGHT_ORDER], *[delta[n] for n in WEIGHT_ORDER],
            *[new_m[n] for n in WEIGHT_ORDER], *[new_v[n] for n in WEIGHT_ORDER])
```

```python
import functools
import math

import jax
import jax.numpy as jnp
from jax import lax
from jax.experimental import pallas as pl
from jax.experimental.pallas import tpu as pltpu

F32 = jnp.float32
BF16 = jnp.bfloat16

D_MODEL = 1024
N_HEADS = 4
D_NOPE = 64
D_ROPE = 32
D_V = 64
Q_RANK = 256
KV_RANK = 128
BR_W = 256
SG_CHUNK = 128
SG_GROUPS = 4
POOL_WINDOWS = (2, 4, 8, 16)
POOL_GROUP = 64
CONV_K = 3
D_FF = 4096
N_BRANCH = 4
N_IN = 6048
EPS = 1e-6
ROPE_BASE = 10000.0
ADAM_LR = 0.001
ADAM_B1 = 0.9
ADAM_B2 = 0.999
ADAM_EPS = 1e-08
ADAM_WD = 0.01
ADAM_STEP = 10

N_DEV = 8
LANES = 128
HEAD_PAD = 128
HALO = 16
PA_W = 640
PM_W = 6 * BR_W
VMEM_LIMIT = 56 * 1024 * 1024
SM_SCALE = (D_NOPE + D_ROPE) ** -0.5
TOK_TILE = 512
ATTN_TILE = 1024
WGRAD_TILE = 1024
INPROJ_BWD_TILE = 512

NN = (((1,), (0,)), ((), ()))
NT = (((1,), (1,)), ((), ()))
TN = (((0,), (0,)), ((), ()))
MESH_ID = pl.DeviceIdType.MESH


def _dot(a, b, dims=NN):
    return lax.dot_general(a, b, dims, preferred_element_type=F32)


def _pcall(body, *, name, grid, in_specs, out_specs, out_shape, scratch=(), vmem=None, prefetch=0):
    params = pltpu.CompilerParams(vmem_limit_bytes=vmem)
    if prefetch:
        spec = pltpu.PrefetchScalarGridSpec(num_scalar_prefetch=prefetch, grid=grid, in_specs=in_specs,
                                            out_specs=out_specs, scratch_shapes=scratch)
        return pl.pallas_call(body, name=name, grid_spec=spec, out_shape=out_shape, compiler_params=params)
    return pl.pallas_call(
        body, name=name, grid=grid, in_specs=in_specs, out_specs=out_specs, out_shape=out_shape,
        scratch_shapes=scratch, compiler_params=params)


def _rows(t, width):
    return pl.BlockSpec((t, width), lambda i: (i, 0))


def _whole(shape):
    nd = len(shape)
    return pl.BlockSpec(tuple(shape), lambda *_: (0,) * nd, pipeline_mode=pl.Buffered(1))


def _acc(shape):
    nd = len(shape)
    return pl.BlockSpec(tuple(shape), lambda *_: (0,) * nd)


def _sds(shape, dtype):
    return jax.ShapeDtypeStruct(tuple(shape), dtype)


def _rms(x, g):
    return x * lax.rsqrt(jnp.mean(x * x, axis=-1, keepdims=True) + EPS) * g


def _rms_bwd(x, g, dy):
    r = lax.rsqrt(jnp.mean(x * x, axis=-1, keepdims=True) + EPS)
    xh = x * r
    dg = jnp.sum(dy * xh, axis=0, keepdims=True)
    dxh = dy * g
    dx = r * (dxh - xh * jnp.mean(dxh * xh, axis=-1, keepdims=True))
    return dx, dg


def _sigmoid(x):
    return 1.0 / (1.0 + jnp.exp(-x))


def _gelu(x):
    return jax.nn.gelu(x, approximate=True)


def _accumulate(ref, val, first):
    @pl.when(first)
    def _():
        ref[...] = val

    @pl.when(jnp.logical_not(first))
    def _():
        ref[...] += val


def _accumulate_out(acc_ref, out_ref, val, first, last):
    _accumulate(acc_ref, val, first)

    @pl.when(last)
    def _():
        out_ref[...] = acc_ref[...].astype(out_ref.dtype)


def _inproj_fwd(x, g1, w_in, t):
    s = x.shape[0]

    def body(x_ref, g_ref, w_ref, hb_ref, pa_ref, pm_ref):
        hb = _rms(x_ref[...], g_ref[...]).astype(BF16)
        hb_ref[...] = hb
        pa_ref[...] = _dot(hb, w_ref[:PA_W, :], NT)
        pm_ref[...] = _dot(hb, w_ref[PA_W:, :], NT)

    return _pcall(
        body, name="inproj_fwd", grid=(s // t,),
        in_specs=[_rows(t, D_MODEL), _whole((1, D_MODEL)), _whole(w_in.shape)],
        out_specs=[_rows(t, D_MODEL), _rows(t, PA_W), _rows(t, PM_W)],
        out_shape=[_sds((s, D_MODEL), BF16), _sds((s, PA_W), F32), _sds((s, PM_W), F32)],
        vmem=VMEM_LIMIT)(x, g1, w_in)


def _attn_prep_fwd(pa, cosf, sins, qn, kvn, wq, wqs, wk, wv, t):
    s = pa.shape[0]

    def body(pa_ref, cos_ref, sin_ref, qn_ref, kvn_ref, wq_ref, wqs_ref, wk_ref, wv_ref, q_ref, k_ref, v_ref):
        cosv, sinv = cos_ref[...], sin_ref[...]
        cqn = _rms(pa_ref[:, 0:Q_RANK], qn_ref[...]).astype(BF16)
        ckvn = _rms(pa_ref[:, Q_RANK:Q_RANK + KV_RANK], kvn_ref[...]).astype(BF16)
        k_rope = pa_ref[:, 384:512] * cosv + pa_ref[:, 512:640] * sinv
        q_all, qs_all = _dot(cqn, wq_ref[...], NT), _dot(cqn, wqs_ref[...], NT)
        k_all = _dot(ckvn, wk_ref[...], NT)
        for h in range(N_HEADS):
            lanes = slice(h * HEAD_PAD, (h + 1) * HEAD_PAD)
            q_ref[h] = ((q_all[:, lanes] * cosv + qs_all[:, lanes] * sinv) * SM_SCALE).astype(BF16)
            k_ref[h] = (k_all[:, lanes] + k_rope).astype(BF16)
        v_ref[...] = _dot(ckvn, wv_ref[...], NT).astype(BF16)

    head_blk = pl.BlockSpec((N_HEADS, t, HEAD_PAD), lambda i: (0, i, 0))
    return _pcall(
        body, name="attn_prep_fwd", grid=(s // t,),
        in_specs=[_rows(t, PA_W), _rows(t, LANES), _rows(t, LANES), _whole(qn.shape), _whole(kvn.shape),
                  _whole(wq.shape), _whole(wqs.shape), _whole(wk.shape), _whole(wv.shape)],
        out_specs=[head_blk, head_blk, _rows(t, BR_W)],
        out_shape=[_sds((N_HEADS, s, HEAD_PAD), BF16), _sds((N_HEADS, s, HEAD_PAD), BF16), _sds((s, BR_W), BF16)],
        vmem=VMEM_LIMIT)(pa, cosf, sins, qn, kvn, wq, wqs, wk, wv)


def _causal_scores(q, k, masked):
    sc = _dot(q, k, NT)
    if masked:
        row = lax.broadcasted_iota(jnp.int32, sc.shape, 0)
        col = lax.broadcasted_iota(jnp.int32, sc.shape, 1)
        sc = jnp.where(col <= row, sc, -jnp.inf)
    return sc


def _diagonal_parts(t):
    half = t // 2
    if half % LANES:
        return [(True, slice(0, t), slice(0, t))]
    lower, upper = slice(0, half), slice(half, t)
    return [(True, lower, lower), (False, upper, lower), (True, upper, upper)]


def _causal_steps(n, key_major):
    if key_major:
        pairs = [(qi, kj) for kj in range(n) for qi in range(kj, n)]
    else:
        pairs = [(qi, kj) for qi in range(n) for kj in range(qi + 1)]
    return (jnp.asarray([p[0] for p in pairs], jnp.int32), jnp.asarray([p[1] for p in pairs], jnp.int32))


def _flash_fwd(q, k, v, t, riding):
    s = v.shape[0]
    n = s // t
    qi_tab, kj_tab = _causal_steps(n, key_major=False)
    n_steps = int(qi_tab.shape[0])
    whole = slice(0, t)

    def body(qi_ref, kj_ref, q_ref, k_ref, v_ref, x_ref, o_ref, lse_ref, land_ref, m_scr, l_scr, acc_scr,
             send_sems, recv_sems, local_sem):
        step_id = pl.program_id(0)
        qi, kj = qi_ref[step_id], kj_ref[step_id]
        at_end = _ride_along(riding, step_id, n_steps, (x_ref, land_ref, send_sems, recv_sems, local_sem))

        @pl.when(kj == 0)
        def _():
            m_scr[...] = jnp.full(m_scr.shape, -jnp.inf, F32)
            l_scr[...] = jnp.zeros(l_scr.shape, F32)
            acc_scr[...] = jnp.zeros(acc_scr.shape, F32)

        def step(masked, qr, kr):
            for h in range(N_HEADS):
                sc = _causal_scores(q_ref[h, qr, :], k_ref[h, kr, :], masked)
                m_prev = m_scr[h, qr, :]
                m_next = jnp.maximum(m_prev, jnp.max(sc, axis=1, keepdims=True))
                alpha = jnp.exp(m_prev - m_next)
                p = jnp.exp(sc - jnp.tile(m_next, (1, sc.shape[1] // LANES)))
                l_scr[h, qr, :] = alpha * l_scr[h, qr, :] + jnp.sum(p, axis=1, keepdims=True)
                m_scr[h, qr, :] = m_next
                pair = slice((h // 2) * LANES, (h // 2 + 1) * LANES)
                acc_scr[h, qr, :] = acc_scr[h, qr, :] * alpha + _dot(p.astype(BF16), v_ref[kr, pair])

        @pl.when(kj < qi)
        def _():
            step(False, whole, whole)

        @pl.when(kj == qi)
        def _():
            for masked, qr, kr in _diagonal_parts(t):
                step(masked, qr, kr)
            lane = lax.broadcasted_iota(jnp.int32, (t, LANES), 1)
            for pr in range(N_HEADS // 2):
                o0 = acc_scr[2 * pr] / l_scr[2 * pr]
                o1 = acc_scr[2 * pr + 1] / l_scr[2 * pr + 1]
                o_ref[:, pr * LANES:(pr + 1) * LANES] = jnp.where(lane < D_V, o0, o1).astype(BF16)
            for h in range(N_HEADS):
                lse_ref[h] = m_scr[h] + jnp.log(l_scr[h])

        at_end()

    q_blk = pl.BlockSpec((N_HEADS, t, HEAD_PAD), lambda i, qi, kj: (0, qi[i], 0))
    k_blk = pl.BlockSpec((N_HEADS, t, HEAD_PAD), lambda i, qi, kj: (0, kj[i], 0))
    v_blk = pl.BlockSpec((t, BR_W), lambda i, qi, kj: (kj[i], 0))
    return _pcall(
        body, name="flash_fwd_gathering", grid=(n_steps,), prefetch=2,
        in_specs=[q_blk, k_blk, v_blk, HBM_REF],
        out_specs=[pl.BlockSpec((t, BR_W), lambda i, qi, kj: (qi[i], 0)), q_blk, HBM_REF],
        out_shape=[_sds((s, BR_W), BF16), _sds((N_HEADS, s, LANES), F32), riding[2]],
        scratch=[pltpu.VMEM((N_HEADS, t, LANES), F32)] * 3 + list(EXCHANGE_SEMS),
        vmem=VMEM_LIMIT)(qi_tab, kj_tab, q, k, v, riding[1])


def _shift_down(ext, k, t):
    return pltpu.roll(ext, k, 0)[HALO:HALO + t]


def _shift_up(ext, k, t):
    return pltpu.roll(ext, t + HALO - k, 0)[0:t]


def _lane_group(shape):
    return lax.shift_right_logical(lax.broadcasted_iota(jnp.int32, shape, 1), 6)


def _group_select(vals):
    grp = _lane_group(vals[0].shape)
    out = vals[0]
    for g in range(1, len(vals)):
        out = jnp.where(grp == g, vals[g], out)
    return out


def _layernorm(x, g, b):
    mu = jnp.mean(x, axis=-1, keepdims=True)
    xc = x - mu
    return xc * lax.rsqrt(jnp.mean(xc * xc, axis=-1, keepdims=True) + EPS) * g + b


def _sg_mix(wm_ref, vnb, bias):
    return _group_select([_dot(wm_ref[g], vnb) for g in range(SG_GROUPS)]) + bias


def _pool_windows(ext):
    s2 = ext + pltpu.roll(ext, 1, 0)
    s4 = s2 + pltpu.roll(s2, 2, 0)
    s8 = s4 + pltpu.roll(s4, 4, 0)
    s16 = s8 + pltpu.roll(s8, 8, 0)
    return [s2, s4, s8, s16]


def _pool_counts(tok):
    return [jnp.minimum(tok + 1, w).astype(F32) for w in POOL_WINDOWS]


def _halo_before(t):
    return pl.BlockSpec((HALO, PM_W), lambda i: (jnp.maximum(i * (t // HALO) - 1, 0), 0))


def _mixers_fwd(pm, ln_g, ln_b, wm, sgb, conv_w, wp, pscale, t):
    s = pm.shape[0]

    def body(pm_ref, halo_ref, lng_ref, lnb_ref, wm_ref, sgb_ref, cw_ref, wp_ref, ps_ref, bsg_ref, bcv_ref, bpl_ref):
        i = pl.program_id(0)
        halo = jnp.where(i > 0, halo_ref[...], 0.0)
        u = _gelu(pm_ref[:, 0:256])
        vnb = _layernorm(_gelu(pm_ref[:, 256:512]), lng_ref[...], lnb_ref[...]).astype(BF16)
        for c in range(t // SG_CHUNK):
            rows = slice(c * SG_CHUNK, (c + 1) * SG_CHUNK)
            bsg_ref[rows, :] = (u[rows] * _sg_mix(wm_ref, vnb[rows], sgb_ref[...])).astype(BF16)
        z = pm_ref[:, 1024:1280] * pm_ref[:, 512:768]
        zext = jnp.concatenate([halo[:, 1024:1280] * halo[:, 512:768], z], axis=0)
        y = cw_ref[0:1, :] * _shift_down(zext, 2, t) + cw_ref[1:2, :] * _shift_down(zext, 1, t) + cw_ref[2:3, :] * z
        bcv_ref[...] = (pm_ref[:, 768:1024] * y).astype(BF16)
        p = pm_ref[:, 1280:1536]
        sums = _pool_windows(jnp.concatenate([halo[:, 1280:1536], p], axis=0))
        tok = i * t + lax.broadcasted_iota(jnp.int32, (t, 1), 0)
        pooled = _group_select([sw[HALO:HALO + t] / cnt - p for sw, cnt in zip(sums, _pool_counts(tok))])
        bpl_ref[...] = (_dot(pooled.astype(BF16), wp_ref[...]) * ps_ref[...]).astype(BF16)

    return _pcall(
        body, name="mixers_fwd", grid=(s // t,),
        in_specs=[_rows(t, PM_W), _halo_before(t), _whole(ln_g.shape), _whole(ln_b.shape), _whole(wm.shape),
                  _whole(sgb.shape), _whole(conv_w.shape), _whole(wp.shape), _whole(pscale.shape)],
        out_specs=[_rows(t, BR_W)] * 3,
        out_shape=[_sds((s, BR_W), BF16)] * 3,
        vmem=VMEM_LIMIT)(pm, pm, ln_g, ln_b, wm, sgb, conv_w, wp, pscale)


def _merge_fwd(x, hb, branches, wg, bg, wbr, wout, g2, t):
    s = x.shape[0]

    def body(x_ref, hb_ref, b0_ref, b1_ref, b2_ref, b3_ref, wg_ref, bg_ref, wbr_ref, wout_ref, g2_ref,
             xmid_ref, mrg_ref, t_ref):
        hb = hb_ref[...]
        merged = jnp.zeros((t, D_MODEL), F32)
        for k, b_ref in enumerate((b0_ref, b1_ref, b2_ref, b3_ref)):
            cols = slice(k * D_MODEL, (k + 1) * D_MODEL)
            gate = _sigmoid(_dot(hb, wg_ref[cols, :], NT) + bg_ref[:, cols])
            merged = merged + gate * _dot(b_ref[...], wbr_ref[k], NT)
        mb = merged.astype(BF16)
        mrg_ref[...] = mb
        tt = _dot(mb, wout_ref[...])
        t_ref[...] = tt
        xmid_ref[...] = x_ref[...] + _rms(tt, g2_ref[...])

    return _pcall(
        body, name="merge_fwd", grid=(s // t,),
        in_specs=[_rows(t, D_MODEL), _rows(t, D_MODEL)] + [_rows(t, BR_W)] * 4 +
                 [_whole(wg.shape), _whole(bg.shape), _whole(wbr.shape), _whole(wout.shape), _whole(g2.shape)],
        out_specs=[_rows(t, D_MODEL)] * 3,
        out_shape=[_sds((s, D_MODEL), F32), _sds((s, D_MODEL), BF16), _sds((s, D_MODEL), F32)],
        vmem=VMEM_LIMIT)(x, hb, *branches, wg, bg, wbr, wout, g2)


def _ffn_fwd(x, g3, w1, w2, g4, t):
    s = x.shape[0]

    def body(x_ref, g3_ref, w1_ref, w2_ref, g4_ref, xo_ref, f_ref):
        h = _rms(x_ref[...], g3_ref[...]).astype(BF16)
        f = jnp.zeros((t, D_MODEL), F32)
        for j in range(D_FF // D_MODEL):
            cols = slice(j * D_MODEL, (j + 1) * D_MODEL)
            r = jnp.square(jnp.maximum(_dot(h, w1_ref[cols, :], NT), 0.0)).astype(BF16)
            f = f + _dot(r, w2_ref[cols, :])
        f_ref[...] = f
        xo_ref[...] = x_ref[...] + _rms(f, g4_ref[...])

    return _pcall(
        body, name="ffn_fwd", grid=(s // t,),
        in_specs=[_rows(t, D_MODEL), _whole(g3.shape), _whole(w1.shape), _whole(w2.shape), _whole(g4.shape)],
        out_specs=[_rows(t, D_MODEL)] * 2,
        out_shape=[_sds((s, D_MODEL), F32)] * 2,
        vmem=VMEM_LIMIT)(x, g3, w1, w2, g4)


def _loss_head(y, target, t):
    s = y.shape[0]
    n = s // t

    def body(y_ref, tg_ref, dy_ref, loss_ref, acc_scr):
        i = pl.program_id(0)
        d = y_ref[...] - tg_ref[...]
        dy_ref[...] = d * (1.0 / D_MODEL)
        _accumulate(acc_scr, jnp.sum(d * d, axis=0, keepdims=True), i == 0)

        @pl.when(i == n - 1)
        def _():
            loss_ref[...] = jnp.full(loss_ref.shape, 0.5 / D_MODEL, F32) * jnp.sum(acc_scr[...])

    return _pcall(
        body, name="loss_head", grid=(n,),
        in_specs=[_rows(t, D_MODEL)] * 2,
        out_specs=[_rows(t, D_MODEL), _acc((8, LANES))],
        out_shape=[_sds((s, D_MODEL), F32), _sds((8, LANES), F32)],
        scratch=[pltpu.VMEM((1, D_MODEL), F32)])(y, target)


def _ffn_bwd_norms(x_mid, f, dxo, g3, g4, t):
    s = x_mid.shape[0]

    def body(x_ref, f_ref, dxo_ref, g3_ref, g4_ref, h2_ref, df_ref, dg4_ref):
        h2_ref[...] = _rms(x_ref[...], g3_ref[...]).astype(BF16)
        df, dg4 = _rms_bwd(f_ref[...], g4_ref[...], dxo_ref[...])
        df_ref[...] = df.astype(BF16)
        _accumulate(dg4_ref, dg4, pl.program_id(0) == 0)

    return _pcall(
        body, name="ffn_bwd_norms", grid=(s // t,),
        in_specs=[_rows(t, D_MODEL)] * 3 + [_whole(g3.shape), _whole(g4.shape)],
        out_specs=[_rows(t, D_MODEL), _rows(t, D_MODEL), _acc((1, D_MODEL))],
        out_shape=[_sds((s, D_MODEL), BF16), _sds((s, D_MODEL), BF16), _sds((1, D_MODEL), F32)])(x_mid, f, dxo, g3, g4)


def _ffn_bwd_weights(h2, df, w1, w2, t):
    s = h2.shape[0]
    blk = D_MODEL
    n = s // t

    def body(h2_ref, df_ref, w1_ref, w2_ref, da_ref, dw1_ref, dw2_ref, acc1, acc2):
        first, last = pl.program_id(1) == 0, pl.program_id(1) == n - 1
        h2v, dfv = h2_ref[...], df_ref[...]
        rl = jnp.maximum(_dot(h2v, w1_ref[...], NT), 0.0)
        _accumulate_out(acc2, dw2_ref, _dot((rl * rl).astype(BF16), dfv, TN), first, last)
        da = (_dot(dfv, w2_ref[...], NT) * (2.0 * rl)).astype(BF16)
        da_ref[...] = da
        _accumulate_out(acc1, dw1_ref, _dot(da, h2v, TN), first, last)

    tok = pl.BlockSpec((t, D_MODEL), lambda j, i: (i, 0))
    hid = pl.BlockSpec((blk, D_MODEL), lambda j, i: (j, 0))
    return _pcall(
        body, name="ffn_bwd_weights", grid=(D_FF // blk, n),
        in_specs=[tok, tok, hid, hid],
        out_specs=[pl.BlockSpec((t, blk), lambda j, i: (i, j)), hid, hid],
        out_shape=[_sds((s, D_FF), BF16), _sds((D_FF, D_MODEL), BF16), _sds((D_FF, D_MODEL), BF16)],
        scratch=[pltpu.VMEM((blk, D_MODEL), F32)] * 2,
        vmem=VMEM_LIMIT)(h2, df, w1, w2)


def _ffn_bwd_input(da, w1, x_mid, dxo, g3, t):
    s = x_mid.shape[0]

    def body(da_ref, w1_ref, x_ref, dxo_ref, g3_ref, dx_ref, dg3_ref):
        dx, dg3 = _rms_bwd(x_ref[...], g3_ref[...], _dot(da_ref[...], w1_ref[...]))
        dx_ref[...] = dxo_ref[...] + dx
        _accumulate(dg3_ref, dg3, pl.program_id(0) == 0)

    return _pcall(
        body, name="ffn_bwd_input", grid=(s // t,),
        in_specs=[_rows(t, D_FF), _whole(w1.shape), _rows(t, D_MODEL), _rows(t, D_MODEL), _whole(g3.shape)],
        out_specs=[_rows(t, D_MODEL), _acc((1, D_MODEL))],
        out_shape=[_sds((s, D_MODEL), F32), _sds((1, D_MODEL), F32)],
        vmem=VMEM_LIMIT)(da, w1, x_mid, dxo, g3)


def _merge_bwd_out(tt, dxmid, g2, wout, merged, t):
    s = tt.shape[0]
    n = s // t

    def body(t_ref, dx_ref, g2_ref, wout_ref, mrg_ref, dm_ref, dwout_ref, dg2_ref, acc):
        first, last = pl.program_id(0) == 0, pl.program_id(0) == n - 1
        dt, dg2 = _rms_bwd(t_ref[...], g2_ref[...], dx_ref[...])
        dtb = dt.astype(BF16)
        dm_ref[...] = _dot(dtb, wout_ref[...], NT)
        _accumulate_out(acc, dwout_ref, _dot(mrg_ref[...], dtb, TN), first, last)
        _accumulate(dg2_ref, dg2, first)

    return _pcall(
        body, name="merge_bwd_out", grid=(n,),
        in_specs=[_rows(t, D_MODEL), _rows(t, D_MODEL), _whole(g2.shape), _whole(wout.shape), _rows(t, D_MODEL)],
        out_specs=[_rows(t, D_MODEL), _acc((D_MODEL, D_MODEL)), _acc((1, D_MODEL))],
        out_shape=[_sds((s, D_MODEL), F32), _sds((D_MODEL, D_MODEL), BF16), _sds((1, D_MODEL), F32)],
        scratch=[pltpu.VMEM((D_MODEL, D_MODEL), F32)],
        vmem=VMEM_LIMIT)(tt, dxmid, g2, wout, merged)


def _merge_bwd_branch(k, hb, br, dm, wg, bg, wbr, t):
    s = hb.shape[0]
    n = s // t

    def body(hb_ref, br_ref, dm_ref, wg_ref, bg_ref, wbr_ref, dbr_ref, dpre_ref, dwg_ref, dbg_ref, dwbr_ref,
             acc_g, acc_br):
        first, last = pl.program_id(0) == 0, pl.program_id(0) == n - 1
        hbv, brv, dmv, wbrv = hb_ref[...], br_ref[...], dm_ref[...], wbr_ref[0]
        gate = _sigmoid(_dot(hbv, wg_ref[...], NT) + bg_ref[...])
        dy = (dmv * gate).astype(BF16)
        dpre = dmv * _dot(brv, wbrv, NT) * gate * (1.0 - gate)
        dpb = dpre.astype(BF16)
        dpre_ref[...] = dpb
        dbr_ref[...] = _dot(dy, wbrv)
        _accumulate_out(acc_br, dwbr_ref, _dot(dy, brv, TN), first, last)
        _accumulate_out(acc_g, dwg_ref, _dot(dpb, hbv, TN), first, last)
        _accumulate(dbg_ref, jnp.sum(dpre, axis=0, keepdims=True), first)

    return _pcall(
        body, name=f"merge_bwd_branch{k}", grid=(n,),
        in_specs=[_rows(t, D_MODEL), _rows(t, BR_W), _rows(t, D_MODEL),
                  pl.BlockSpec((D_MODEL, D_MODEL), lambda i: (k, 0)), pl.BlockSpec((1, D_MODEL), lambda i: (0, k)),
                  pl.BlockSpec((1, D_MODEL, BR_W), lambda i: (k, 0, 0))],
        out_specs=[_rows(t, BR_W), _rows(t, D_MODEL), _acc((D_MODEL, D_MODEL)), _acc((1, D_MODEL)), _acc((D_MODEL, BR_W))],
        out_shape=[_sds((s, BR_W), F32), _sds((s, D_MODEL), BF16), _sds((D_MODEL, D_MODEL), BF16),
                   _sds((1, D_MODEL), F32), _sds((D_MODEL, BR_W), BF16)],
        scratch=[pltpu.VMEM((D_MODEL, D_MODEL), F32), pltpu.VMEM((D_MODEL, BR_W), F32)],
        vmem=VMEM_LIMIT)(hb, br, dm, wg, bg, wbr)


def _mixers_bwd(pm, dbsg, dbcv, dbpl, ln_g, ln_b, wm, wmt, sgb, conv_w, wp, pscale, t):
    s = pm.shape[0]
    n = s // t
    nb = t // HALO

    def body(pm_ref, before_ref, after_ref, dsg_ref, dcv_ref, dcva_ref, dpl_ref, dpla_ref, lng_ref, lnb_ref,
             wm_ref, wmt_ref, sgb_ref, cw_ref, wp_ref, ps_ref,
             dpm_ref, dlng_ref, dlnb_ref, dwm_ref, dsgb_ref, dcw_ref, dwp_ref, dps_ref):
        i = pl.program_id(0)
        first = i == 0
        before = jnp.where(i > 0, before_ref[...], 0.0)
        after = jnp.where(i < n - 1, after_ref[...], 0.0)

        u_raw, v_raw = pm_ref[:, 0:256], pm_ref[:, 256:512]
        lng, lnb = lng_ref[...], lnb_ref[...]
        u, gelu_u_vjp = jax.vjp(_gelu, u_raw)
        vn, norm_vjp = jax.vjp(lambda v_, g_, b_: _layernorm(_gelu(v_), g_, b_), v_raw, lng, lnb)
        vnb = vn.astype(BF16)
        dsg = dsg_ref[...]
        grp = _lane_group((SG_CHUNK, BR_W))
        tri = (lax.broadcasted_iota(jnp.int32, (SG_CHUNK, SG_CHUNK), 1)
               <= lax.broadcasted_iota(jnp.int32, (SG_CHUNK, SG_CHUNK), 0))
        du_parts, dvn_parts = [], []
        dsgb = jnp.zeros((SG_CHUNK, BR_W), F32)
        dwm = [jnp.zeros((SG_CHUNK, SG_CHUNK), F32) for _ in range(SG_GROUPS)]
        for c in range(t // SG_CHUNK):
            rows = slice(c * SG_CHUNK, (c + 1) * SG_CHUNK)
            mix = _sg_mix(wm_ref, vnb[rows], sgb_ref[...])
            du_parts.append(dsg[rows] * mix)
            ds = dsg[rows] * u[rows]
            dsgb = dsgb + ds
            dsb = [jnp.where(grp == g, ds, 0.0).astype(BF16) for g in range(SG_GROUPS)]
            for g in range(SG_GROUPS):
                dwm[g] = dwm[g] + _dot(dsb[g], vnb[rows], NT)
            dvn_parts.append(_group_select([_dot(wmt_ref[g], dsb[g]) for g in range(SG_GROUPS)]))
        (du_raw,) = gelu_u_vjp(jnp.concatenate(du_parts, axis=0))
        dv_raw, dlng, dlnb = norm_vjp(jnp.concatenate(dvn_parts, axis=0))
        dpm_ref[:, 0:256] = du_raw.astype(BF16)
        dpm_ref[:, 256:512] = dv_raw.astype(BF16)
        _accumulate(dlng_ref, dlng, first)
        _accumulate(dlnb_ref, dlnb, first)
        _accumulate(dsgb_ref, dsgb, first)
        for g in range(SG_GROUPS):
            _accumulate(dwm_ref.at[g], jnp.where(tri, dwm[g], 0.0), first)

        xin, bg, cg = pm_ref[:, 512:768], pm_ref[:, 768:1024], pm_ref[:, 1024:1280]
        z = cg * xin
        zext = jnp.concatenate([before[:, 1024:1280] * before[:, 512:768], z], axis=0)
        z1, z2 = _shift_down(zext, 1, t), _shift_down(zext, 2, t)
        w0, w1, w2 = cw_ref[0:1, :], cw_ref[1:2, :], cw_ref[2:3, :]
        dcv = dcv_ref[...]
        y = w0 * z2 + w1 * z1 + w2 * z
        dy = dcv * bg
        dyext = jnp.concatenate([dy, jnp.where(i < n - 1, dcva_ref[...], 0.0) * after[:, 768:1024]], axis=0)
        dz = w2 * dy + w1 * _shift_up(dyext, 1, t) + w0 * _shift_up(dyext, 2, t)
        dpm_ref[:, 512:768] = (dz * cg).astype(BF16)
        dpm_ref[:, 768:1024] = (dcv * y).astype(BF16)
        dpm_ref[:, 1024:1280] = (dz * xin).astype(BF16)
        dcw = jnp.concatenate([jnp.sum(dy * z2, axis=0, keepdims=True), jnp.sum(dy * z1, axis=0, keepdims=True),
                               jnp.sum(dy * z, axis=0, keepdims=True)], axis=0)
        _accumulate(dcw_ref, jnp.concatenate([dcw, jnp.zeros((8 - CONV_K, BR_W), F32)], axis=0), first)

        p = pm_ref[:, 1280:1536]
        tok = i * t + lax.broadcasted_iota(jnp.int32, (t, 1), 0)
        sums = _pool_windows(jnp.concatenate([before[:, 1280:1536], p], axis=0))
        pooled = _group_select([sw[HALO:HALO + t] / cnt - p for sw, cnt in zip(sums, _pool_counts(tok))]).astype(BF16)
        mixed = _dot(pooled, wp_ref[...])
        dpl = dpl_ref[...]
        ps = ps_ref[...]
        dmix = (dpl * ps).astype(BF16)
        _accumulate(dps_ref, jnp.sum(dpl * mixed, axis=0, keepdims=True), first)
        _accumulate(dwp_ref, _dot(pooled, dmix, TN), first)
        dmix_after = (jnp.where(i < n - 1, dpla_ref[...], 0.0) * ps).astype(BF16)
        dpo = _dot(dmix, wp_ref[...], NT)
        dpo_ext = jnp.concatenate([dpo, _dot(dmix_after, wp_ref[...], NT)], axis=0)
        tok_ext = i * t + lax.broadcasted_iota(jnp.int32, (t + HALO, 1), 0)
        dp_groups = []
        for g, (win, cnt) in enumerate(zip(POOL_WINDOWS, _pool_counts(tok_ext))):
            e = dpo_ext / cnt
            acc = e
            span = 1
            while span < win:
                acc = acc + pltpu.roll(acc, t + HALO - span, 0)
                span *= 2
            dp_groups.append(acc[0:t] - dpo)
        dpm_ref[:, 1280:1536] = _group_select(dp_groups).astype(BF16)

    row_blk = lambda w: pl.BlockSpec((t, w), lambda i: (i, 0))
    after_blk = lambda w: pl.BlockSpec((HALO, w), lambda i: (jnp.minimum((i + 1) * nb, n * nb - 1), 0))
    return _pcall(
        body, name="mixers_bwd", grid=(n,),
        in_specs=[row_blk(PM_W), _halo_before(t), after_blk(PM_W), row_blk(BR_W), row_blk(BR_W), after_blk(BR_W),
                  row_blk(BR_W), after_blk(BR_W), _whole(ln_g.shape), _whole(ln_b.shape), _whole(wm.shape),
                  _whole(wmt.shape), _whole(sgb.shape), _whole(conv_w.shape), _whole(wp.shape), _whole(pscale.shape)],
        out_specs=[row_blk(PM_W), _acc((1, BR_W)), _acc((1, BR_W)), _acc((SG_GROUPS, SG_CHUNK, SG_CHUNK)),
                   _acc((SG_CHUNK, BR_W)), _acc((8, BR_W)), _acc((BR_W, BR_W)), _acc((1, BR_W))],
        out_shape=[_sds((s, PM_W), BF16), _sds((1, BR_W), F32), _sds((1, BR_W), F32),
                   _sds((SG_GROUPS, SG_CHUNK, SG_CHUNK), F32), _sds((SG_CHUNK, BR_W), F32), _sds((8, BR_W), F32),
                   _sds((BR_W, BR_W), F32), _sds((1, BR_W), F32)],
        vmem=VMEM_LIMIT)(pm, pm, pm, dbsg, dbcv, dbcv, dbpl, dbpl, ln_g, ln_b, wm, wmt, sgb, conv_w, wp, pscale)


def _head_delta(o, do):
    rows = o.shape[0]
    prod = o.astype(F32) * do
    lane = lax.broadcasted_iota(jnp.int32, (rows, LANES), 1)
    out = []
    for h in range(N_HEADS):
        pair = prod[:, (h // 2) * LANES:(h // 2 + 1) * LANES]
        mine = (lane < D_V) if h % 2 == 0 else (lane >= D_V)
        out.append(jnp.broadcast_to(jnp.sum(jnp.where(mine, pair, 0.0), axis=1, keepdims=True), (rows, LANES)))
    return out


def _head_do(do, h):
    lane = lax.broadcasted_iota(jnp.int32, (do.shape[0], LANES), 1)
    mine = (lane < D_V) if h % 2 == 0 else (lane >= D_V)
    pair = do[:, (h // 2) * LANES:(h // 2 + 1) * LANES]
    return jnp.where(mine, pair, 0.0).astype(BF16)


def _flash_bwd(q, k, v, o, do, lse, t, riding):
    s = v.shape[0]
    n = s // t
    qi_tab, kj_tab = _causal_steps(n, key_major=True)
    n_steps = int(qi_tab.shape[0])
    whole = slice(0, t)

    def body(qi_ref, kj_ref, q_ref, k_ref, v_ref, o_ref, do_ref, lse_ref, x_ref, dq_ref, dk_ref, dv_ref, land_ref,
             dv_scr, send_sems, recv_sems, local_sem):
        step_id = pl.program_id(0)
        qi, kj = qi_ref[step_id], kj_ref[step_id]
        at_end = _ride_along(riding, step_id, n_steps, (x_ref, land_ref, send_sems, recv_sems, local_sem))

        @pl.when(step_id == 0)
        def _():
            dq_ref[...] = jnp.zeros(dq_ref.shape, F32)

        @pl.when(qi == kj)
        def _():
            dk_ref[...] = jnp.zeros(dk_ref.shape, F32)
            dv_scr[...] = jnp.zeros(dv_scr.shape, F32)

        def step(masked, qr, kr):
            dov = do_ref[qr, :]
            delta = _head_delta(o_ref[qr, :], dov)
            reps = (kr.stop - kr.start) // LANES
            q_rows = pl.ds(pl.multiple_of(qi * t + qr.start, t // 2), qr.stop - qr.start)
            for h in range(N_HEADS):
                qh, kh = q_ref[h, qr, :], k_ref[h, kr, :]
                p = jnp.exp(_causal_scores(qh, kh, masked) - jnp.tile(lse_ref[h, qr, :], (1, reps)))
                pair = slice((h // 2) * LANES, (h // 2 + 1) * LANES)
                dv_scr[h, kr, :] += _dot(p.astype(BF16), dov[:, pair].astype(BF16), TN)
                dp = _dot(_head_do(dov, h), v_ref[kr, pair], NT)
                ds = (p * (dp - jnp.tile(delta[h], (1, reps)))).astype(BF16)
                dk_ref[h, kr, :] += _dot(ds, qh, TN)
                dq_ref[h, q_rows, :] += _dot(ds, kh)

        @pl.when(qi > kj)
        def _():
            step(False, whole, whole)

        @pl.when(qi == kj)
        def _():
            for masked, qr, kr in _diagonal_parts(t):
                step(masked, qr, kr)

        @pl.when(qi == n - 1)
        def _():
            lane = lax.broadcasted_iota(jnp.int32, (t, LANES), 1)
            for pr in range(N_HEADS // 2):
                dv_ref[:, pr * LANES:(pr + 1) * LANES] = jnp.where(lane < D_V, dv_scr[2 * pr], dv_scr[2 * pr + 1]).astype(BF16)

        at_end()

    q_blk = pl.BlockSpec((N_HEADS, t, HEAD_PAD), lambda i, qi, kj: (0, qi[i], 0))
    k_blk = pl.BlockSpec((N_HEADS, t, HEAD_PAD), lambda i, qi, kj: (0, kj[i], 0))
    v_blk = pl.BlockSpec((t, BR_W), lambda i, qi, kj: (kj[i], 0))
    o_blk = pl.BlockSpec((t, BR_W), lambda i, qi, kj: (qi[i], 0))
    dq_sds = _sds((N_HEADS, s, HEAD_PAD), F32)
    return _pcall(
        body, name="flash_bwd_exchanging", grid=(n_steps,), prefetch=2,
        in_specs=[q_blk, k_blk, v_blk, o_blk, o_blk, q_blk, HBM_REF],
        out_specs=[_whole(dq_sds.shape), k_blk, v_blk, HBM_REF],
        out_shape=[dq_sds, dq_sds, _sds((s, BR_W), BF16), riding[2]],
        scratch=[pltpu.VMEM((N_HEADS, t, LANES), F32)] + list(EXCHANGE_SEMS),
        vmem=VMEM_LIMIT)(qi_tab, kj_tab, q, k, v, o, do, lse, riding[1])


def _attn_prep_bwd(pa, dq, dk, dv, cosf, sins, qn, kvn, wq, wqs, wk, wv, t):
    s = pa.shape[0]

    def body(pa_ref, dq_ref, dk_ref, dv_ref, cos_ref, sin_ref, qn_ref, kvn_ref, wq_ref, wqs_ref, wk_ref, wv_ref,
             dpa_ref, dwq_ref, dwqs_ref, dwk_ref, dwv_ref, dqn_ref, dkvn_ref):
        first = pl.program_id(0) == 0
        cosv, sinv = cos_ref[...], sin_ref[...]
        cq, ckv = pa_ref[:, 0:Q_RANK], pa_ref[:, Q_RANK:Q_RANK + KV_RANK]
        cqn = _rms(cq, qn_ref[...]).astype(BF16)
        ckvn = _rms(ckv, kvn_ref[...]).astype(BF16)
        dvv = dv_ref[...]
        _accumulate(dwv_ref, _dot(dvv, ckvn, TN), first)
        dk_rope = dk_ref[0]
        for h in range(1, N_HEADS):
            dk_rope = dk_rope + dk_ref[h]
        cos_s, sin_s = cosv * SM_SCALE, sinv * SM_SCALE
        dqa = jnp.concatenate([(dq_ref[h] * cos_s).astype(BF16) for h in range(N_HEADS)], axis=1)
        dqs = jnp.concatenate([(dq_ref[h] * sin_s).astype(BF16) for h in range(N_HEADS)], axis=1)
        dkb = jnp.concatenate([dk_ref[h].astype(BF16) for h in range(N_HEADS)], axis=1)
        _accumulate(dwq_ref, _dot(dqa, cqn, TN), first)
        _accumulate(dwqs_ref, _dot(dqs, cqn, TN), first)
        _accumulate(dwk_ref, _dot(dkb, ckvn, TN), first)
        dcqn = _dot(dqa, wq_ref[...]) + _dot(dqs, wqs_ref[...])
        dckvn = _dot(dvv, wv_ref[...]) + _dot(dkb, wk_ref[...])
        dcq, dqn = _rms_bwd(cq, qn_ref[...], dcqn)
        dckv, dkvn = _rms_bwd(ckv, kvn_ref[...], dckvn)
        _accumulate(dqn_ref, dqn, first)
        _accumulate(dkvn_ref, dkvn, first)
        dpa_ref[:, 0:Q_RANK] = dcq.astype(BF16)
        dpa_ref[:, Q_RANK:Q_RANK + KV_RANK] = dckv.astype(BF16)
        dpa_ref[:, 384:512] = (dk_rope * cosv).astype(BF16)
        dpa_ref[:, 512:640] = (dk_rope * sinv).astype(BF16)

    head_blk = pl.BlockSpec((N_HEADS, t, HEAD_PAD), lambda i: (0, i, 0))
    return _pcall(
        body, name="attn_prep_bwd", grid=(s // t,),
        in_specs=[_rows(t, PA_W), head_blk, head_blk, _rows(t, BR_W), _rows(t, LANES), _rows(t, LANES),
                  _whole(qn.shape), _whole(kvn.shape), _whole(wq.shape), _whole(wqs.shape), _whole(wk.shape), _whole(wv.shape)],
        out_specs=[_rows(t, PA_W), _acc(wq.shape), _acc(wqs.shape), _acc(wk.shape), _acc(wv.shape),
                   _acc(qn.shape), _acc(kvn.shape)],
        out_shape=[_sds((s, PA_W), BF16), _sds(wq.shape, F32), _sds(wqs.shape, F32), _sds(wk.shape, F32),
                   _sds(wv.shape, F32), _sds(qn.shape, F32), _sds(kvn.shape, F32)],
        vmem=VMEM_LIMIT)(pa, dq, dk, dv, cosf, sins, qn, kvn, wq, wqs, wk, wv)


def _inproj_bwd(x, g1, dxmid, dpa, dpm, dpres, w_in, wg, t):
    s = x.shape[0]
    n = s // t

    def body(x_ref, g1_ref, dxm_ref, dpa_ref, dpm_ref, d0_ref, d1_ref, d2_ref, d3_ref, w_ref, wg_ref,
             dx_ref, dg1_ref, dwa_ref, dwm_ref, acc_a, acc_m):
        first, last = pl.program_id(0) == 0, pl.program_id(0) == n - 1
        xv, g1v = x_ref[...], g1_ref[...]
        hb = _rms(xv, g1v).astype(BF16)
        dpav, dpmv = dpa_ref[...], dpm_ref[...]
        dh = _dot(dpav, w_ref[:PA_W, :]) + _dot(dpmv, w_ref[PA_W:, :])
        for k, d_ref in enumerate((d0_ref, d1_ref, d2_ref, d3_ref)):
            dh = dh + _dot(d_ref[...], wg_ref[k * D_MODEL:(k + 1) * D_MODEL, :])
        dx, dg1 = _rms_bwd(xv, g1v, dh)
        dx_ref[...] = dxm_ref[...] + dx
        _accumulate(dg1_ref, dg1, first)
        _accumulate_out(acc_a, dwa_ref, _dot(dpav, hb, TN), first, last)
        _accumulate_out(acc_m, dwm_ref, _dot(dpmv, hb, TN), first, last)

    return _pcall(
        body, name="inproj_bwd", grid=(n,),
        in_specs=[_rows(t, D_MODEL), _whole(g1.shape), _rows(t, D_MODEL), _rows(t, PA_W), _rows(t, PM_W)] +
                 [_rows(t, D_MODEL)] * 4 + [_whole(w_in.shape), _whole(wg.shape)],
        out_specs=[_rows(t, D_MODEL), _acc((1, D_MODEL)), _acc((PA_W, D_MODEL)), _acc((PM_W, D_MODEL))],
        out_shape=[_sds((s, D_MODEL), F32), _sds((1, D_MODEL), F32), _sds((PA_W, D_MODEL), BF16), _sds((PM_W, D_MODEL), BF16)],
        scratch=[pltpu.VMEM((PA_W, D_MODEL), F32), pltpu.VMEM((PM_W, D_MODEL), F32)],
        vmem=VMEM_LIMIT)(x, g1, dxmid, dpa, dpm, *dpres, w_in, wg)


def _my_place():
    return lax.axis_index("x"), lax.axis_index("y"), lax.axis_index("c")


def _flip(place, k):
    x, y, c = place
    return (1 - x if k & 4 else x, 1 - y if k & 2 else y, 1 - c if k & 1 else c)


def _rank(place):
    x, y, c = place
    return 4 * x + 2 * y + c


EXCHANGE_SEMS = (pltpu.SemaphoreType.DMA((7,)), pltpu.SemaphoreType.DMA((7,)), pltpu.SemaphoreType.DMA)
HBM_REF = pl.BlockSpec(memory_space=pl.ANY)


def _gather_plan(x_ref, out_ref, send_sems, recv_sems, local_sem):
    me = _my_place()
    sibling = _flip(me, 1)
    chips = (4, 2, 6)

    def copy(k, src_place, to, src=None):
        slot = out_ref.at[_rank(src_place)]
        return pltpu.make_async_remote_copy(
            src_ref=slot if src is None else src, dst_ref=slot, send_sem=send_sems.at[k], recv_sem=recv_sems.at[k],
            device_id=to, device_id_type=MESH_ID)

    mine = pltpu.make_async_copy(x_ref, out_ref.at[_rank(me)], local_sem)
    first = [copy(0, me, sibling, src=x_ref)] + [copy(1 + j, me, _flip(me, kc), src=x_ref) for j, kc in enumerate(chips)]
    passed = [copy(4 + j, _flip(me, kc), sibling) for j, kc in enumerate(chips)]

    def start():
        mine.start()
        for cp in first:
            cp.start()

    def forward():
        for j, kc in enumerate(chips):
            copy(1 + j, _flip(me, kc), me).wait_recv()
            passed[j].start()

    def finish():
        copy(0, sibling, me).wait_recv()
        for j, kc in enumerate(chips):
            copy(4 + j, _flip(sibling, kc), me).wait_recv()
        for cp in first + passed:
            cp.wait_send()
        mine.wait()

    return start, forward, finish


def _exchange_plan(x_ref, out_ref, send_sems, recv_sems, local_sem):
    me = _my_place()
    my_rank = _rank(me)
    mine = pltpu.make_async_copy(x_ref.at[my_rank], out_ref.at[my_rank], local_sem)
    sends = [pltpu.make_async_remote_copy(
        src_ref=x_ref.at[_rank(_flip(me, k))], dst_ref=out_ref.at[my_rank], send_sem=send_sems.at[k - 1],
        recv_sem=recv_sems.at[k - 1], device_id=_flip(me, k), device_id_type=MESH_ID) for k in range(1, N_DEV)]

    def start():
        mine.start()
        for cp in sends:
            cp.start()

    def forward():
        pass

    def finish():
        for k in range(1, N_DEV):
            slot = out_ref.at[_rank(_flip(me, k))]
            pltpu.make_async_remote_copy(
                src_ref=slot, dst_ref=slot, send_sem=send_sems.at[k - 1], recv_sem=recv_sems.at[k - 1],
                device_id=_flip(me, k), device_id_type=MESH_ID).wait_recv()
        for cp in sends:
            cp.wait_send()
        mine.wait()

    return start, forward, finish


def _exchange_alone(plan, src, out_sds, name):
    def body(x_ref, out_ref, send_sems, recv_sems, local_sem):
        start, forward, finish = plan(x_ref, out_ref, send_sems, recv_sems, local_sem)
        start()
        forward()
        finish()

    return pl.pallas_call(body, name=name, out_shape=out_sds, in_specs=[HBM_REF], out_specs=HBM_REF,
                          scratch_shapes=list(EXCHANGE_SEMS))(src)


def _exchange_two_alone(first, second, name):
    def body(x1_ref, x2_ref, out1_ref, out2_ref, *sems):
        start1, forward1, finish1 = first[0](x1_ref, out1_ref, *sems[:3])
        start2, forward2, finish2 = second[0](x2_ref, out2_ref, *sems[3:])
        start1()
        start2()
        forward1()
        forward2()
        finish1()
        finish2()

    return pl.pallas_call(body, name=name, out_shape=[first[2], second[2]], in_specs=[HBM_REF, HBM_REF],
                          out_specs=[HBM_REF, HBM_REF], scratch_shapes=list(EXCHANGE_SEMS) * 2)(first[1], second[1])


def _all_gather(shard, name):
    return _exchange_alone(_gather_plan, shard, _sds((N_DEV,) + shard.shape, shard.dtype), name)


def _riding(plan, src, out_sds):
    return (plan, src, out_sds)


def _ride_along(riding, step_id, n_steps, refs):
    start, forward, finish = riding[0](*refs)

    @pl.when(step_id == 0)
    def _():
        start()

    @pl.when(step_id == (3 * n_steps) // 4)
    def _():
        forward()

    def at_end():
        @pl.when(step_id == n_steps - 1)
        def _():
            finish()

    return at_end


def _sum_slots(parts, name):
    _, r, c = parts.shape
    t = math.gcd(r, ROW_PAD)

    def body(p_ref, o_ref):
        acc = p_ref[0].astype(F32)
        for d in range(1, N_DEV):
            acc = acc + p_ref[d].astype(F32)
        o_ref[...] = acc

    return _pcall(
        body, name=name, grid=(r // t,),
        in_specs=[pl.BlockSpec((N_DEV, t, c), lambda i: (0, i, 0))], out_specs=_rows(t, c),
        out_shape=_sds((r, c), F32))(parts)


def _adamw(w, g, m, v, name):
    depth, r, c = w.shape
    t = r if r <= ROW_PAD else math.gcd(r, ROW_PAD)
    blk = pl.BlockSpec((1, t, c), lambda l, i: (l, i, 0))

    def body(w_ref, g_ref, m_ref, v_ref, d_ref, nm_ref, nv_ref):
        gv = g_ref[...]
        nm = ADAM_B1 * m_ref[...] + (1.0 - ADAM_B1) * gv
        nv = ADAM_B2 * v_ref[...] + (1.0 - ADAM_B2) * jnp.square(gv)
        m_hat = nm / (1.0 - ADAM_B1 ** ADAM_STEP)
        v_hat = nv / (1.0 - ADAM_B2 ** ADAM_STEP)
        d_ref[...] = -ADAM_LR * (m_hat / (jnp.sqrt(v_hat) + ADAM_EPS) + ADAM_WD * w_ref[...])
        nm_ref[...] = nm
        nv_ref[...] = nv

    return _pcall(
        body, name=name, grid=(depth, r // t),
        in_specs=[blk] * 4, out_specs=[blk] * 3, out_shape=[_sds(w.shape, F32)] * 3)(w, g, m, v)


BEFORE_ATTN = ("w_in", "w_uq", "w_ukv")
AFTER_ATTN = ("w_br_mla", "w_br_sg", "w_br_conv", "w_br_pool", "w_out", "w_ff1", "w_ff2", "conv_w")
SHARDED = BEFORE_ATTN + AFTER_ATTN
ROW_SHARDED = ("w_out", "w_ff2")
REPLICATED = ("norm_mix_pre", "gate_b", "q_norm", "kv_norm", "sg_ln_g", "sg_ln_b", "sg_w", "sg_b", "pool_w",
              "pool_scale", "norm_mix_post", "norm_ffn_pre", "norm_ffn_post")
ROW_PAD = 256
GRAD_ROW_PAD = 64
PART_ROWS = 16


def _pack_rows(arrays, dtype, multiple, lead=()):
    rows, offsets, at = [], [], 0
    zero_pad = ((0, 0),) * len(lead)
    for a in arrays:
        if a.ndim == len(lead) + 2 and a.shape[-1] == D_MODEL:
            n = a.shape[-2]
            n_pad = -(-n // PART_ROWS) * PART_ROWS
            rows.append(jnp.pad(a.astype(dtype), zero_pad + ((0, n_pad - n), (0, 0))))
        else:
            flat = a.reshape(lead + (-1,)).astype(dtype)
            n = -(-flat.shape[-1] // D_MODEL)
            n_pad = -(-n // PART_ROWS) * PART_ROWS
            flat = jnp.pad(flat, zero_pad + ((0, n_pad * D_MODEL - flat.shape[-1]),))
            rows.append(flat.reshape(lead + (n_pad, D_MODEL)))
        offsets.append((at, n))
        at += n_pad
    pad = -at % multiple
    if pad:
        rows.append(jnp.zeros(lead + (pad, D_MODEL), dtype))
    return jnp.concatenate(rows, axis=len(lead)), offsets


def _unpack_rows(buf, offsets, shapes):
    lead = buf.shape[:-2]
    out = []
    for (at, n), shape in zip(offsets, shapes):
        if tuple(shape) == (n, D_MODEL):
            out.append(buf[..., at:at + n, :])
            continue
        size = math.prod(shape)
        flat = buf[..., at:at + n, :].reshape(lead + (n * D_MODEL,))[..., :size]
        out.append(flat.reshape(lead + tuple(shape)))
    return out


def _pack_shards(p, depth, names):
    arrays = []
    for n in names:
        a = p[n][0] if n in ROW_SHARDED else jnp.swapaxes(p[n][0], -1, -2)
        arrays.append(lax.bitcast_convert_type(a, BF16) if n == "conv_w" else a.astype(BF16))
    packed, offsets = _pack_rows(arrays, BF16, GRAD_ROW_PAD, lead=(depth,))
    return packed, offsets, [a.shape[1:] for a in arrays]


def _unpack_layer_weights(names, gathered, offsets, shapes):
    full = {}
    for n, gth in zip(names, _unpack_rows(gathered, offsets, shapes)):
        if n == "conv_w":
            gth = lax.bitcast_convert_type(gth, F32)
        full[n] = gth.reshape((-1,) + gth.shape[2:])
        if n == "conv_w":
            full[n] = full[n].T
    return full


def _owner_major(grad_sharded_dim_first):
    a = grad_sharded_dim_first
    return a.reshape((N_DEV, a.shape[0] // N_DEV) + a.shape[1:])


def _owner_block_shape(name, shard_shape):
    k, n = shard_shape[-2:]
    return (k, n) if name in ROW_SHARDED else (n, k)


def _natural_shard(name, blocks):
    return blocks if name in ROW_SHARDED else jnp.swapaxes(blocks, -1, -2)


def _swap_halves(a, axis=-1):
    lo, hi = jnp.split(a, 2, axis=axis)
    return jnp.concatenate([hi, lo], axis=axis)


def _pad_rows(a, top, total):
    return jnp.pad(a, ((0, 0),) * (a.ndim - 2) + ((top, total - top - a.shape[-2]), (0, 0)))


def _weights_before_attn(full, small, l):
    w_in = full["w_in"]
    k_r = w_in[384:416]
    w_in_k = jnp.concatenate(
        [w_in[0:384], _pad_rows(k_r, D_NOPE, HEAD_PAD), _pad_rows(_swap_halves(k_r, axis=0), D_NOPE, HEAD_PAD), w_in[416:1952]], axis=0)
    w_uq = full["w_uq"].reshape(N_HEADS, D_NOPE + D_ROPE, Q_RANK)
    wq = _pad_rows(w_uq, 0, HEAD_PAD).reshape(N_HEADS * HEAD_PAD, Q_RANK)
    wqs = _pad_rows(_swap_halves(w_uq[:, D_NOPE:], axis=1), D_NOPE, HEAD_PAD).reshape(N_HEADS * HEAD_PAD, Q_RANK)
    w_ukv = full["w_ukv"].reshape(N_HEADS, D_NOPE + D_V, KV_RANK)
    wk = _pad_rows(w_ukv[:, :D_NOPE], 0, HEAD_PAD).reshape(N_HEADS * HEAD_PAD, KV_RANK)
    wv = w_ukv[:, D_NOPE:].reshape(N_HEADS * D_V, KV_RANK)
    tri = jnp.tril(jnp.ones((SG_CHUNK, SG_CHUNK), bool))
    wm = jnp.where(tri, small["sg_w"][l], 0.0)
    pool_w = small["pool_w"][l]
    wp = jnp.zeros((BR_W, BR_W), F32)
    for g in range(len(POOL_WINDOWS)):
        wp = wp.at[g * POOL_GROUP:(g + 1) * POOL_GROUP, g * POOL_GROUP:(g + 1) * POOL_GROUP].set(pool_w[g])
    vec = lambda name: small[name][l][None, :]
    return dict(
        w_in=w_in_k, wg=w_in[1952:], bg=vec("gate_b"), g1=vec("norm_mix_pre"), qn=vec("q_norm"), kvn=vec("kv_norm"),
        wq=wq, wqs=wqs, wk=wk, wv=wv,
        ln_g=vec("sg_ln_g"), ln_b=vec("sg_ln_b"), wm=wm.astype(BF16), wmt=jnp.swapaxes(wm, 1, 2).astype(BF16),
        sgb=jnp.repeat(small["sg_b"][l].T, POOL_GROUP, axis=1), wp=wp.astype(BF16), pscale=vec("pool_scale"),
        g2=vec("norm_mix_post"), g3=vec("norm_ffn_pre"), g4=vec("norm_ffn_post"))


def _weights_after_attn(full):
    return dict(
        conv_w=full["conv_w"], wbr=jnp.stack([full["w_br_mla"], full["w_br_sg"], full["w_br_conv"], full["w_br_pool"]]),
        wout=full["w_out"], w1=full["w_ff1"], w2=full["w_ff2"])


def _grads_after_attn(g):
    own = _owner_major
    return dict(
        w_br_mla=own(g["dwbr"][0]), w_br_sg=own(g["dwbr"][1]), w_br_conv=own(g["dwbr"][2]), w_br_pool=own(g["dwbr"][3]),
        w_out=own(g["dwout"]), w_ff1=own(g["dw1"]), w_ff2=own(g["dw2"]), conv_w=own(g["dcw"][:CONV_K].T))


def _grads_before_attn(g):
    dwa = g["dwa"]
    rope_rows = slice(D_NOPE, D_NOPE + D_ROPE)
    d_kr = dwa[384:512][rope_rows].astype(F32) + _swap_halves(dwa[512:640][rope_rows], axis=0).astype(F32)
    w_in = jnp.concatenate([dwa[0:384], d_kr.astype(BF16), g["dwm_in"]] + list(g["dwg"]), axis=0)
    dwq, dwqs = (g[n].reshape(N_HEADS, HEAD_PAD, Q_RANK) for n in ("dwq", "dwqs"))
    d_rope = dwq[:, rope_rows] + _swap_halves(dwqs[:, rope_rows], axis=1)
    w_uq = jnp.concatenate([dwq[:, :D_NOPE], d_rope], axis=1).reshape(N_HEADS * (D_NOPE + D_ROPE), Q_RANK)
    dwk = g["dwk"].reshape(N_HEADS, HEAD_PAD, KV_RANK)
    w_ukv = jnp.concatenate([dwk[:, :D_NOPE], g["dwv"].reshape(N_HEADS, D_V, KV_RANK)], axis=1)
    w_ukv = w_ukv.reshape(N_HEADS * (D_NOPE + D_V), KV_RANK)
    pool_w = jnp.stack([g["dwp"][i * POOL_GROUP:(i + 1) * POOL_GROUP, i * POOL_GROUP:(i + 1) * POOL_GROUP]
                        for i in range(len(POOL_WINDOWS))])
    sg_b = g["dsgb"].reshape(SG_CHUNK, SG_GROUPS, POOL_GROUP).sum(axis=-1).T
    own = _owner_major
    return dict(
        w_in=own(w_in), w_uq=own(w_uq), w_ukv=own(w_ukv),
        norm_mix_pre=g["dg1"][0], gate_b=jnp.concatenate([b[0] for b in g["dbg"]]), q_norm=g["dqn"][0], kv_norm=g["dkvn"][0],
        sg_ln_g=g["dlng"][0], sg_ln_b=g["dlnb"][0], sg_w=g["dwm"], sg_b=sg_b, pool_w=pool_w, pool_scale=g["dps"][0],
        norm_mix_post=g["dg2"][0], norm_ffn_pre=g["dg3"][0], norm_ffn_post=g["dg4"][0])


def _rope_tables(positions):
    inv_freq = ROPE_BASE ** (-jnp.arange(0, D_ROPE, 2, dtype=F32) / D_ROPE)
    ang = positions.astype(F32)[:, None] * inv_freq
    cos, sin = jnp.cos(ang), jnp.sin(ang)
    s = positions.shape[0]
    cosf = jnp.concatenate([jnp.ones((s, D_NOPE), F32), cos, cos, jnp.zeros((s, HEAD_PAD - D_NOPE - D_ROPE), F32)], axis=1)
    sins = jnp.concatenate([jnp.zeros((s, D_NOPE), F32), -sin, sin, jnp.zeros((s, HEAD_PAD - D_NOPE - D_ROPE), F32)], axis=1)
    return cosf, sins


def _layer_fwd(x, w, weights_after_attn, cosf, sins, tiles, riding):
    t, ta = tiles["tok"], tiles["attn"]
    hb, pa, pm = _inproj_fwd(x, w["g1"], w["w_in"], tiles["wgrad"])
    q, k, v = _attn_prep_fwd(pa, cosf, sins, w["qn"], w["kvn"], w["wq"], w["wqs"], w["wk"], w["wv"], tiles["wgrad"])
    o, lse, landed = _flash_fwd(q, k, v, ta, riding)
    w = {**w, **weights_after_attn(landed)}
    bsg, bcv, bpl = _mixers_fwd(pm, w["ln_g"], w["ln_b"], w["wm"], w["sgb"], w["conv_w"], w["wp"], w["pscale"], tiles["wgrad"])
    x_mid, merged, tt = _merge_fwd(x, hb, (o, bsg, bcv, bpl), w["wg"], w["bg"], w["wbr"], w["wout"], w["g2"], t)
    x_out, f = _ffn_fwd(x_mid, w["g3"], w["w1"], w["w2"], w["g4"], t)
    saved = dict(x=x, hb=hb, pa=pa, pm=pm, q=q, k=k, v=v, o=o, lse=lse, bsg=bsg, bcv=bcv, bpl=bpl, x_mid=x_mid,
                 merged=merged, tt=tt, f=f)
    return x_out, saved, w, landed


def _layer_bwd(dxo, a, w, cosf, sins, tiles, pending):
    t, ta, tb, tw = tiles["tok"], tiles["attn"], tiles["inproj_bwd"], tiles["wgrad"]
    g = {}
    h2, df, g["dg4"] = _ffn_bwd_norms(a["x_mid"], a["f"], dxo, w["g3"], w["g4"], tw)
    da, g["dw1"], g["dw2"] = _ffn_bwd_weights(h2, df, w["w1"], w["w2"], tw)
    dxmid, g["dg3"] = _ffn_bwd_input(da, w["w1"], a["x_mid"], dxo, w["g3"], t)
    dm, g["dwout"], g["dg2"] = _merge_bwd_out(a["tt"], dxmid, w["g2"], w["wout"], a["merged"], tw)
    dbrs, dpres, g["dwg"], g["dbg"], g["dwbr"] = [], [], [], [], []
    for k, br in enumerate((a["o"], a["bsg"], a["bcv"], a["bpl"])):
        dbr, dpre, dwg, dbg, dwbr = _merge_bwd_branch(k, a["hb"], br, dm, w["wg"], w["bg"], w["wbr"], tw)
        dbrs.append(dbr); dpres.append(dpre); g["dwg"].append(dwg); g["dbg"].append(dbg); g["dwbr"].append(dwbr)
    dpm, g["dlng"], g["dlnb"], g["dwm"], g["dsgb"], g["dcw"], g["dwp"], g["dps"] = _mixers_bwd(
        a["pm"], dbrs[1], dbrs[2], dbrs[3], w["ln_g"], w["ln_b"], w["wm"], w["wmt"], w["sgb"], w["conv_w"], w["wp"], w["pscale"], tw)
    after = _grads_after_attn(g)
    send, after_offsets = _pack_rows([after[n] for n in AFTER_ATTN], BF16, GRAD_ROW_PAD, lead=(N_DEV,))
    after_rows = send.shape[1]
    if pending is not None:
        send = jnp.concatenate([send, pending], axis=1)
    riding = _riding(_exchange_plan, send, _sds(send.shape, send.dtype))
    dq, dk, dv, landed = _flash_bwd(a["q"], a["k"], a["v"], a["o"], dbrs[0], a["lse"], ta, riding)
    dpa, g["dwq"], g["dwqs"], g["dwk"], g["dwv"], g["dqn"], g["dkvn"] = _attn_prep_bwd(
        a["pa"], dq, dk, dv, cosf, sins, w["qn"], w["kvn"], w["wq"], w["wqs"], w["wk"], w["wv"], tw)
    dx, g["dg1"], g["dwa"], g["dwm_in"] = _inproj_bwd(a["x"], w["g1"], dxmid, dpa, dpm, dpres, w["w_in"], w["wg"], tb)
    return dx, _grads_before_attn(g), landed, (after_rows, after_offsets)


def _step(p, x, positions, loss_target):
    s = x.shape[0]
    depth = p["w_in"][0].shape[0]
    tiles = dict(tok=min(TOK_TILE, s), attn=min(ATTN_TILE, s), inproj_bwd=min(INPROJ_BWD_TILE, s),
                 wgrad=min(WGRAD_TILE, s))

    small = {n: p[n][0] for n in REPLICATED}
    cosf, sins = _rope_tables(positions)

    pack_b, off_b, shapes_b = _pack_shards(p, depth, BEFORE_ATTN)
    pack_a, off_a, shapes_a = _pack_shards(p, depth, AFTER_ATTN)
    rows_a = pack_a.shape[1]
    before = _all_gather(pack_b[0], "weight_all_gather")
    weights, acts = [], []
    h = x
    for l in range(depth):
        w = _weights_before_attn(_unpack_layer_weights(BEFORE_ATTN, before, off_b, shapes_b), small, l)
        src = pack_a[l] if l + 1 == depth else jnp.concatenate([pack_a[l], pack_b[l + 1]], axis=0)
        riding = _riding(_gather_plan, src, _sds((N_DEV,) + src.shape, src.dtype))

        def after(landed):
            return _weights_after_attn(_unpack_layer_weights(AFTER_ATTN, landed[:, :rows_a], off_a, shapes_a))

        h, saved, w, landed = _layer_fwd(h, w, after, cosf, sins, tiles, riding)
        weights.append(w)
        acts.append(saved)
        before = landed[:, rows_a:] if l + 1 < depth else None
    dh, loss_blk = _loss_head(h, loss_target, tiles["tok"])
    loss = lax.psum(loss_blk[0, 0], ("x", "y", "c"))

    grads_b = [None] * depth
    sum_a, sum_b = [None] * depth, [None] * depth
    pending = None
    for l in reversed(range(depth)):
        dh, grads_b[l], landed, (rows_ga, off_ga) = _layer_bwd(dh, acts[l], weights[l], cosf, sins, tiles, pending)
        summed = _sum_slots(landed, "grad_shard_sum")
        sum_a[l] = summed[:rows_ga]
        if pending is not None:
            sum_b[l + 1] = summed[rows_ga:]
        pending, off_gb = _pack_rows([grads_b[l][n] for n in BEFORE_ATTN], BF16, GRAD_ROW_PAD, lead=(N_DEV,))
    small_grads = [jnp.stack([grads_b[l][n] for l in range(depth)]) for n in REPLICATED]
    small_rows, small_offsets = _pack_rows(small_grads, F32, ROW_PAD)
    landed, small_landed = _exchange_two_alone(
        _riding(_exchange_plan, pending, _sds(pending.shape, pending.dtype)),
        _riding(_gather_plan, small_rows, _sds((N_DEV,) + small_rows.shape, small_rows.dtype)), "last_grad_exchanges")
    sum_b[0] = _sum_slots(landed, "grad_shard_sum")
    g_small_rows = _sum_slots(small_landed, "grad_replicated_sum")
    g_shard = {}
    for names, sums, offsets in ((AFTER_ATTN, sum_a, off_ga), (BEFORE_ATTN, sum_b, off_gb)):
        block_shapes = [_owner_block_shape(n, p[n][0].shape) for n in names]
        for n, b in zip(names, _unpack_rows(jnp.stack(sums), offsets, block_shapes)):
            g_shard[n] = _natural_shard(n, b)

    grad, delta, new_m, new_v = {}, {}, {}, {}
    for n in SHARDED:
        w, m, v = p[n]
        grad[n] = g_shard[n]
        delta[n], new_m[n], new_v[n] = _adamw(w, grad[n], m, v, f"adamw_{n}")
    packs = [_pack_rows([p[n][i] for n in REPLICATED], F32, ROW_PAD)[0][None] for i in range(3)]
    d, nm, nv = (a[0] for a in _adamw(packs[0], g_small_rows[None], packs[1], packs[2], "adamw_replicated"))
    shapes = [p[n][0].shape for n in REPLICATED]
    for n, gg, dd, mm, vv in zip(REPLICATED, _unpack_rows(g_small_rows, small_offsets, shapes), _unpack_rows(d, small_offsets, shapes),
                                 _unpack_rows(nm, small_offsets, shapes), _unpack_rows(nv, small_offsets, shapes)):
        grad[n], delta[n], new_m[n], new_v[n] = gg, dd, mm, vv
    return loss, dh, grad, delta, new_m, new_v


WEIGHT_ORDER = ("norm_mix_pre", "w_in", "gate_b", "q_norm", "w_uq", "kv_norm", "w_ukv", "w_br_mla", "sg_ln_g", "sg_ln_b",
                "sg_w", "sg_b", "w_br_sg", "conv_w", "w_br_conv", "pool_w", "pool_scale", "w_br_pool", "w_out",
                "norm_mix_post", "norm_ffn_pre", "w_ff1", "w_ff2", "norm_ffn_post")


def kernel(x, positions, norm_mix_pre, w_in, gate_b, q_norm, w_uq, kv_norm, w_ukv, w_br_mla, sg_ln_g, sg_ln_b, sg_w, sg_b, w_br_sg, conv_w, w_br_conv, pool_w, pool_scale, w_br_pool, w_out, norm_mix_post, norm_ffn_pre, w_ff1, w_ff2, norm_ffn_post, loss_target, m_norm_mix_pre, m_w_in, m_gate_b, m_q_norm, m_w_uq, m_kv_norm, m_w_ukv, m_w_br_mla, m_sg_ln_g, m_sg_ln_b, m_sg_w, m_sg_b, m_w_br_sg, m_conv_w, m_w_br_conv, m_pool_w, m_pool_scale, m_w_br_pool, m_w_out, m_norm_mix_post, m_norm_ffn_pre, m_w_ff1, m_w_ff2, m_norm_ffn_post, v_norm_mix_pre, v_w_in, v_gate_b, v_q_norm, v_w_uq, v_kv_norm, v_w_ukv, v_w_br_mla, v_sg_ln_g, v_sg_ln_b, v_sg_w, v_sg_b, v_w_br_sg, v_conv_w, v_w_br_conv, v_pool_w, v_pool_scale, v_w_br_pool, v_w_out, v_norm_mix_post, v_norm_ffn_pre, v_w_ff1, v_w_ff2, v_norm_ffn_post):
    ws = (norm_mix_pre, w_in, gate_b, q_norm, w_uq, kv_norm, w_ukv, w_br_mla, sg_ln_g, sg_ln_b, sg_w, sg_b, w_br_sg, conv_w,
          w_br_conv, pool_w, pool_scale, w_br_pool, w_out, norm_mix_post, norm_ffn_pre, w_ff1, w_ff2, norm_ffn_post)
    ms = (m_norm_mix_pre, m_w_in, m_gate_b, m_q_norm, m_w_uq, m_kv_norm, m_w_ukv, m_w_br_mla, m_sg_ln_g, m_sg_ln_b, m_sg_w,
          m_sg_b, m_w_br_sg, m_conv_w, m_w_br_conv, m_pool_w, m_pool_scale, m_w_br_pool, m_w_out, m_norm_mix_post,
          m_norm_ffn_pre, m_w_ff1, m_w_ff2, m_norm_ffn_post)
    vs = (v_norm_mix_pre, v_w_in, v_gate_b, v_q_norm, v_w_uq, v_kv_norm, v_w_ukv, v_w_br_mla, v_sg_ln_g, v_sg_ln_b, v_sg_w,
          v_sg_b, v_w_br_sg, v_conv_w, v_w_br_conv, v_pool_w, v_pool_scale, v_w_br_pool, v_w_out, v_norm_mix_post,
          v_norm_ffn_pre, v_w_ff1, v_w_ff2, v_norm_ffn_post)
    p = {n: (w, m, v) for n, w, m, v in zip(WEIGHT_ORDER, ws, ms, vs)}
    loss, grad_x, grad, delta, new_m, new_v = _step(p, x[0], positions[0], loss_target[0])
    return (loss, grad_x[None], *[grad[n] for n in WEIGHT_ORDER], *[delta[n] for n in WEIGHT_ORDER],
            *[new_m[n] for n in WEIGHT_ORDER], *[new_v[n] for n in WEIGHT_ORDER])
```

```python
import functools
import math

import jax
import jax.numpy as jnp
from jax import lax
from jax.experimental import pallas as pl
from jax.experimental.pallas import tpu as pltpu

F32 = jnp.float32
BF16 = jnp.bfloat16

D_MODEL = 1024
N_HEADS = 4
D_NOPE = 64
D_ROPE = 32
D_V = 64
Q_RANK = 256
KV_RANK = 128
BR_W = 256
SG_CHUNK = 128
SG_GROUPS = 4
POOL_WINDOWS = (2, 4, 8, 16)
POOL_GROUP = 64
CONV_K = 3
D_FF = 4096
N_BRANCH = 4
N_IN = 6048
EPS = 1e-6
ROPE_BASE = 10000.0
ADAM_LR = 0.001
ADAM_B1 = 0.9
ADAM_B2 = 0.999
ADAM_EPS = 1e-08
ADAM_WD = 0.01
ADAM_STEP = 10

N_DEV = 8
LANES = 128
HEAD_PAD = 128
HALO = 16
PA_W = 640
PM_W = 6 * BR_W
VMEM_LIMIT = 56 * 1024 * 1024
SM_SCALE = (D_NOPE + D_ROPE) ** -0.5
TOK_TILE = 512
ATTN_TILE = 1024
WGRAD_TILE = 1024
INPROJ_BWD_TILE = 512

NN = (((1,), (0,)), ((), ()))
NT = (((1,), (1,)), ((), ()))
TN = (((0,), (0,)), ((), ()))
MESH_ID = pl.DeviceIdType.MESH


def _dot(a, b, dims=NN):
    return lax.dot_general(a, b, dims, preferred_element_type=F32)


def _pcall(body, *, name, grid, in_specs, out_specs, out_shape, scratch=(), vmem=None, prefetch=0):
    params = pltpu.CompilerParams(vmem_limit_bytes=vmem)
    if prefetch:
        spec = pltpu.PrefetchScalarGridSpec(num_scalar_prefetch=prefetch, grid=grid, in_specs=in_specs,
                                            out_specs=out_specs, scratch_shapes=scratch)
        return pl.pallas_call(body, name=name, grid_spec=spec, out_shape=out_shape, compiler_params=params)
    return pl.pallas_call(
        body, name=name, grid=grid, in_specs=in_specs, out_specs=out_specs, out_shape=out_shape,
        scratch_shapes=scratch, compiler_params=params)


def _rows(t, width):
    return pl.BlockSpec((t, width), lambda i: (i, 0))


def _whole(shape):
    nd = len(shape)
    return pl.BlockSpec(tuple(shape), lambda *_: (0,) * nd, pipeline_mode=pl.Buffered(1))


def _acc(shape):
    nd = len(shape)
    return pl.BlockSpec(tuple(shape), lambda *_: (0,) * nd)


def _sds(shape, dtype):
    return jax.ShapeDtypeStruct(tuple(shape), dtype)


def _rms(x, g):
    return x * lax.rsqrt(jnp.mean(x * x, axis=-1, keepdims=True) + EPS) * g


def _rms_bwd(x, g, dy):
    r = lax.rsqrt(jnp.mean(x * x, axis=-1, keepdims=True) + EPS)
    xh = x * r
    dg = jnp.sum(dy * xh, axis=0, keepdims=True)
    dxh = dy * g
    dx = r * (dxh - xh * jnp.mean(dxh * xh, axis=-1, keepdims=True))
    return dx, dg


def _sigmoid(x):
    return 1.0 / (1.0 + jnp.exp(-x))


def _gelu(x):
    return jax.nn.gelu(x, approximate=True)


def _accumulate(ref, val, first):
    @pl.when(first)
    def _():
        ref[...] = val

    @pl.when(jnp.logical_not(first))
    def _():
        ref[...] += val


def _accumulate_out(acc_ref, out_ref, val, first, last):
    _accumulate(acc_ref, val, first)

    @pl.when(last)
    def _():
        out_ref[...] = acc_ref[...].astype(out_ref.dtype)


def _inproj_fwd(x, g1, w_in, t):
    s = x.shape[0]

    def body(x_ref, g_ref, w_ref, hb_ref, pa_ref, pm_ref):
        hb = _rms(x_ref[...], g_ref[...]).astype(BF16)
        hb_ref[...] = hb
        pa_ref[...] = _dot(hb, w_ref[:PA_W, :], NT)
        pm_ref[...] = _dot(hb, w_ref[PA_W:, :], NT)

    return _pcall(
        body, name="inproj_fwd", grid=(s // t,),
        in_specs=[_rows(t, D_MODEL), _whole((1, D_MODEL)), _whole(w_in.shape)],
        out_specs=[_rows(t, D_MODEL), _rows(t, PA_W), _rows(t, PM_W)],
        out_shape=[_sds((s, D_MODEL), BF16), _sds((s, PA_W), F32), _sds((s, PM_W), F32)],
        vmem=VMEM_LIMIT)(x, g1, w_in)


def _attn_prep_fwd(pa, cosf, sins, qn, kvn, wq, wqs, wk, wv, t):
    s = pa.shape[0]

    def body(pa_ref, cos_ref, sin_ref, qn_ref, kvn_ref, wq_ref, wqs_ref, wk_ref, wv_ref, q_ref, k_ref, v_ref):
        cosv, sinv = cos_ref[...], sin_ref[...]
        cqn = _rms(pa_ref[:, 0:Q_RANK], qn_ref[...]).astype(BF16)
        ckvn = _rms(pa_ref[:, Q_RANK:Q_RANK + KV_RANK], kvn_ref[...]).astype(BF16)
        k_rope = pa_ref[:, 384:512] * cosv + pa_ref[:, 512:640] * sinv
        q_all, qs_all = _dot(cqn, wq_ref[...], NT), _dot(cqn, wqs_ref[...], NT)
        k_all = _dot(ckvn, wk_ref[...], NT)
        for h in range(N_HEADS):
            lanes = slice(h * HEAD_PAD, (h + 1) * HEAD_PAD)
            q_ref[h] = ((q_all[:, lanes] * cosv + qs_all[:, lanes] * sinv) * SM_SCALE).astype(BF16)
            k_ref[h] = (k_all[:, lanes] + k_rope).astype(BF16)
        v_ref[...] = _dot(ckvn, wv_ref[...], NT).astype(BF16)

    head_blk = pl.BlockSpec((N_HEADS, t, HEAD_PAD), lambda i: (0, i, 0))
    return _pcall(
        body, name="attn_prep_fwd", grid=(s // t,),
        in_specs=[_rows(t, PA_W), _rows(t, LANES), _rows(t, LANES), _whole(qn.shape), _whole(kvn.shape),
                  _whole(wq.shape), _whole(wqs.shape), _whole(wk.shape), _whole(wv.shape)],
        out_specs=[head_blk, head_blk, _rows(t, BR_W)],
        out_shape=[_sds((N_HEADS, s, HEAD_PAD), BF16), _sds((N_HEADS, s, HEAD_PAD), BF16), _sds((s, BR_W), BF16)],
        vmem=VMEM_LIMIT)(pa, cosf, sins, qn, kvn, wq, wqs, wk, wv)


def _causal_scores(q, k, masked):
    sc = _dot(q, k, NT)
    if masked:
        row = lax.broadcasted_iota(jnp.int32, sc.shape, 0)
        col = lax.broadcasted_iota(jnp.int32, sc.shape, 1)
        sc = jnp.where(col <= row, sc, -jnp.inf)
    return sc


def _diagonal_parts(t):
    half = t // 2
    if half % LANES:
        return [(True, slice(0, t), slice(0, t))]
    lower, upper = slice(0, half), slice(half, t)
    return [(True, lower, lower), (False, upper, lower), (True, upper, upper)]


def _causal_steps(n, key_major):
    if key_major:
        pairs = [(qi, kj) for kj in range(n) for qi in range(kj, n)]
    else:
        pairs = [(qi, kj) for qi in range(n) for kj in range(qi + 1)]
    return (jnp.asarray([p[0] for p in pairs], jnp.int32), jnp.asarray([p[1] for p in pairs], jnp.int32))


def _flash_fwd(q, k, v, t, riding):
    s = v.shape[0]
    n = s // t
    qi_tab, kj_tab = _causal_steps(n, key_major=False)
    n_steps = int(qi_tab.shape[0])
    whole = slice(0, t)

    def body(qi_ref, kj_ref, q_ref, k_ref, v_ref, x_ref, o_ref, lse_ref, land_ref, m_scr, l_scr, acc_scr,
             send_sems, recv_sems, local_sem):
        step_id = pl.program_id(0)
        qi, kj = qi_ref[step_id], kj_ref[step_id]
        at_end = _ride_along(riding, step_id, n_steps, (x_ref, land_ref, send_sems, recv_sems, local_sem))

        @pl.when(kj == 0)
        def _():
            m_scr[...] = jnp.full(m_scr.shape, -jnp.inf, F32)
            l_scr[...] = jnp.zeros(l_scr.shape, F32)
            acc_scr[...] = jnp.zeros(acc_scr.shape, F32)

        def step(masked, qr, kr):
            for h in range(N_HEADS):
                sc = _causal_scores(q_ref[h, qr, :], k_ref[h, kr, :], masked)
                m_prev = m_scr[h, qr, :]
                m_next = jnp.maximum(m_prev, jnp.max(sc, axis=1, keepdims=True))
                alpha = jnp.exp(m_prev - m_next)
                p = jnp.exp(sc - jnp.tile(m_next, (1, sc.shape[1] // LANES)))
                l_scr[h, qr, :] = alpha * l_scr[h, qr, :] + jnp.sum(p, axis=1, keepdims=True)
                m_scr[h, qr, :] = m_next
                pair = slice((h // 2) * LANES, (h // 2 + 1) * LANES)
                acc_scr[h, qr, :] = acc_scr[h, qr, :] * alpha + _dot(p.astype(BF16), v_ref[kr, pair])

        @pl.when(kj < qi)
        def _():
            step(False, whole, whole)

        @pl.when(kj == qi)
        def _():
            for masked, qr, kr in _diagonal_parts(t):
                step(masked, qr, kr)
            lane = lax.broadcasted_iota(jnp.int32, (t, LANES), 1)
            for pr in range(N_HEADS // 2):
                o0 = acc_scr[2 * pr] / l_scr[2 * pr]
                o1 = acc_scr[2 * pr + 1] / l_scr[2 * pr + 1]
                o_ref[:, pr * LANES:(pr + 1) * LANES] = jnp.where(lane < D_V, o0, o1).astype(BF16)
            for h in range(N_HEADS):
                lse_ref[h] = m_scr[h] + jnp.log(l_scr[h])

        at_end()

    q_blk = pl.BlockSpec((N_HEADS, t, HEAD_PAD), lambda i, qi, kj: (0, qi[i], 0))
    k_blk = pl.BlockSpec((N_HEADS, t, HEAD_PAD), lambda i, qi, kj: (0, kj[i], 0))
    v_blk = pl.BlockSpec((t, BR_W), lambda i, qi, kj: (kj[i], 0))
    return _pcall(
        body, name="flash_fwd_gathering", grid=(n_steps,), prefetch=2,
        in_specs=[q_blk, k_blk, v_blk, HBM_REF],
        out_specs=[pl.BlockSpec((t, BR_W), lambda i, qi, kj: (qi[i], 0)), q_blk, HBM_REF],
        out_shape=[_sds((s, BR_W), BF16), _sds((N_HEADS, s, LANES), F32), riding[2]],
        scratch=[pltpu.VMEM((N_HEADS, t, LANES), F32)] * 3 + list(EXCHANGE_SEMS),
        vmem=VMEM_LIMIT)(qi_tab, kj_tab, q, k, v, riding[1])


def _shift_down(ext, k, t):
    return pltpu.roll(ext, k, 0)[HALO:HALO + t]


def _shift_up(ext, k, t):
    return pltpu.roll(ext, t + HALO - k, 0)[0:t]


def _lane_group(shape):
    return lax.shift_right_logical(lax.broadcasted_iota(jnp.int32, shape, 1), 6)


def _group_select(vals):
    grp = _lane_group(vals[0].shape)
    out = vals[0]
    for g in range(1, len(vals)):
        out = jnp.where(grp == g, vals[g], out)
    return out


def _layernorm(x, g, b):
    mu = jnp.mean(x, axis=-1, keepdims=True)
    xc = x - mu
    return xc * lax.rsqrt(jnp.mean(xc * xc, axis=-1, keepdims=True) + EPS) * g + b


def _sg_mix(wm_ref, vnb, bias):
    return _group_select([_dot(wm_ref[g], vnb) for g in range(SG_GROUPS)]) + bias


def _pool_windows(ext):
    s2 = ext + pltpu.roll(ext, 1, 0)
    s4 = s2 + pltpu.roll(s2, 2, 0)
    s8 = s4 + pltpu.roll(s4, 4, 0)
    s16 = s8 + pltpu.roll(s8, 8, 0)
    return [s2, s4, s8, s16]


def _pool_counts(tok):
    return [jnp.minimum(tok + 1, w).astype(F32) for w in POOL_WINDOWS]


def _halo_before(t):
    return pl.BlockSpec((HALO, PM_W), lambda i: (jnp.maximum(i * (t // HALO) - 1, 0), 0))


def _mixers_fwd(pm, ln_g, ln_b, wm, sgb, conv_w, wp, pscale, t):
    s = pm.shape[0]

    def body(pm_ref, halo_ref, lng_ref, lnb_ref, wm_ref, sgb_ref, cw_ref, wp_ref, ps_ref, bsg_ref, bcv_ref, bpl_ref):
        i = pl.program_id(0)
        halo = jnp.where(i > 0, halo_ref[...], 0.0)
        u = _gelu(pm_ref[:, 0:256])
        vnb = _layernorm(_gelu(pm_ref[:, 256:512]), lng_ref[...], lnb_ref[...]).astype(BF16)
        for c in range(t // SG_CHUNK):
            rows = slice(c * SG_CHUNK, (c + 1) * SG_CHUNK)
            bsg_ref[rows, :] = (u[rows] * _sg_mix(wm_ref, vnb[rows], sgb_ref[...])).astype(BF16)
        z = pm_ref[:, 1024:1280] * pm_ref[:, 512:768]
        zext = jnp.concatenate([halo[:, 1024:1280] * halo[:, 512:768], z], axis=0)
        y = cw_ref[0:1, :] * _shift_down(zext, 2, t) + cw_ref[1:2, :] * _shift_down(zext, 1, t) + cw_ref[2:3, :] * z
        bcv_ref[...] = (pm_ref[:, 768:1024] * y).astype(BF16)
        p = pm_ref[:, 1280:1536]
        sums = _pool_windows(jnp.concatenate([halo[:, 1280:1536], p], axis=0))
        tok = i * t + lax.broadcasted_iota(jnp.int32, (t, 1), 0)
        pooled = _group_select([sw[HALO:HALO + t] / cnt - p for sw, cnt in zip(sums, _pool_counts(tok))])
        bpl_ref[...] = (_dot(pooled.astype(BF16), wp_ref[...]) * ps_ref[...]).astype(BF16)

    return _pcall(
        body, name="mixers_fwd", grid=(s // t,),
        in_specs=[_rows(t, PM_W), _halo_before(t), _whole(ln_g.shape), _whole(ln_b.shape), _whole(wm.shape),
                  _whole(sgb.shape), _whole(conv_w.shape), _whole(wp.shape), _whole(pscale.shape)],
        out_specs=[_rows(t, BR_W)] * 3,
        out_shape=[_sds((s, BR_W), BF16)] * 3,
        vmem=VMEM_LIMIT)(pm, pm, ln_g, ln_b, wm, sgb, conv_w, wp, pscale)


def _merge_fwd(x, hb, branches, wg, bg, wbr, wout, g2, t):
    s = x.shape[0]

    def body(x_ref, hb_ref, b0_ref, b1_ref, b2_ref, b3_ref, wg_ref, bg_ref, wbr_ref, wout_ref, g2_ref,
             xmid_ref, mrg_ref, t_ref):
        hb = hb_ref[...]
        merged = jnp.zeros((t, D_MODEL), F32)
        for k, b_ref in enumerate((b0_ref, b1_ref, b2_ref, b3_ref)):
            cols = slice(k * D_MODEL, (k + 1) * D_MODEL)
            gate = _sigmoid(_dot(hb, wg_ref[cols, :], NT) + bg_ref[:, cols])
            merged = merged + gate * _dot(b_ref[...], wbr_ref[k], NT)
        mb = merged.astype(BF16)
        mrg_ref[...] = mb
        tt = _dot(mb, wout_ref[...])
        t_ref[...] = tt
        xmid_ref[...] = x_ref[...] + _rms(tt, g2_ref[...])

    return _pcall(
        body, name="merge_fwd", grid=(s // t,),
        in_specs=[_rows(t, D_MODEL), _rows(t, D_MODEL)] + [_rows(t, BR_W)] * 4 +
                 [_whole(wg.shape), _whole(bg.shape), _whole(wbr.shape), _whole(wout.shape), _whole(g2.shape)],
        out_specs=[_rows(t, D_MODEL)] * 3,
        out_shape=[_sds((s, D_MODEL), F32), _sds((s, D_MODEL), BF16), _sds((s, D_MODEL), F32)],
        vmem=VMEM_LIMIT)(x, hb, *branches, wg, bg, wbr, wout, g2)


def _ffn_fwd(x, g3, w1, w2, g4, t):
    s = x.shape[0]

    def body(x_ref, g3_ref, w1_ref, w2_ref, g4_ref, xo_ref, f_ref):
        h = _rms(x_ref[...], g3_ref[...]).astype(BF16)
        f = jnp.zeros((t, D_MODEL), F32)
        for j in range(D_FF // D_MODEL):
            cols = slice(j * D_MODEL, (j + 1) * D_MODEL)
            r = jnp.square(jnp.maximum(_dot(h, w1_ref[cols, :], NT), 0.0)).astype(BF16)
            f = f + _dot(r, w2_ref[cols, :])
        f_ref[...] = f
        xo_ref[...] = x_ref[...] + _rms(f, g4_ref[...])

    return _pcall(
        body, name="ffn_fwd", grid=(s // t,),
        in_specs=[_rows(t, D_MODEL), _whole(g3.shape), _whole(w1.shape), _whole(w2.shape), _whole(g4.shape)],
        out_specs=[_rows(t, D_MODEL)] * 2,
        out_shape=[_sds((s, D_MODEL), F32)] * 2,
        vmem=VMEM_LIMIT)(x, g3, w1, w2, g4)


def _loss_head(y, target, t):
    s = y.shape[0]
    n = s // t

    def body(y_ref, tg_ref, dy_ref, loss_ref, acc_scr):
        i = pl.program_id(0)
        d = y_ref[...] - tg_ref[...]
        dy_ref[...] = d * (1.0 / D_MODEL)
        _accumulate(acc_scr, jnp.sum(d * d, axis=0, keepdims=True), i == 0)

        @pl.when(i == n - 1)
        def _():
            loss_ref[...] = jnp.full(loss_ref.shape, 0.5 / D_MODEL, F32) * jnp.sum(acc_scr[...])

    return _pcall(
        body, name="loss_head", grid=(n,),
        in_specs=[_rows(t, D_MODEL)] * 2,
        out_specs=[_rows(t, D_MODEL), _acc((8, LANES))],
        out_shape=[_sds((s, D_MODEL), F32), _sds((8, LANES), F32)],
        scratch=[pltpu.VMEM((1, D_MODEL), F32)])(y, target)


def _ffn_bwd_norms(x_mid, f, dxo, g3, g4, t):
    s = x_mid.shape[0]

    def body(x_ref, f_ref, dxo_ref, g3_ref, g4_ref, h2_ref, df_ref, dg4_ref):
        h2_ref[...] = _rms(x_ref[...], g3_ref[...]).astype(BF16)
        df, dg4 = _rms_bwd(f_ref[...], g4_ref[...], dxo_ref[...])
        df_ref[...] = df.astype(BF16)
        _accumulate(dg4_ref, dg4, pl.program_id(0) == 0)

    return _pcall(
        body, name="ffn_bwd_norms", grid=(s // t,),
        in_specs=[_rows(t, D_MODEL)] * 3 + [_whole(g3.shape), _whole(g4.shape)],
        out_specs=[_rows(t, D_MODEL), _rows(t, D_MODEL), _acc((1, D_MODEL))],
        out_shape=[_sds((s, D_MODEL), BF16), _sds((s, D_MODEL), BF16), _sds((1, D_MODEL), F32)])(x_mid, f, dxo, g3, g4)


def _ffn_bwd_weights(h2, df, w1, w2, t):
    s = h2.shape[0]
    blk = D_MODEL
    n = s // t

    def body(h2_ref, df_ref, w1_ref, w2_ref, da_ref, dw1_ref, dw2_ref, acc1, acc2):
        first, last = pl.program_id(1) == 0, pl.program_id(1) == n - 1
        h2v, dfv = h2_ref[...], df_ref[...]
        rl = jnp.maximum(_dot(h2v, w1_ref[...], NT), 0.0)
        _accumulate_out(acc2, dw2_ref, _dot((rl * rl).astype(BF16), dfv, TN), first, last)
        da = (_dot(dfv, w2_ref[...], NT) * (2.0 * rl)).astype(BF16)
        da_ref[...] = da
        _accumulate_out(acc1, dw1_ref, _dot(da, h2v, TN), first, last)

    tok = pl.BlockSpec((t, D_MODEL), lambda j, i: (i, 0))
    hid = pl.BlockSpec((blk, D_MODEL), lambda j, i: (j, 0))
    return _pcall(
        body, name="ffn_bwd_weights", grid=(D_FF // blk, n),
        in_specs=[tok, tok, hid, hid],
        out_specs=[pl.BlockSpec((t, blk), lambda j, i: (i, j)), hid, hid],
        out_shape=[_sds((s, D_FF), BF16), _sds((D_FF, D_MODEL), BF16), _sds((D_FF, D_MODEL), BF16)],
        scratch=[pltpu.VMEM((blk, D_MODEL), F32)] * 2,
        vmem=VMEM_LIMIT)(h2, df, w1, w2)


def _ffn_bwd_input(da, w1, x_mid, dxo, g3, t):
    s = x_mid.shape[0]

    def body(da_ref, w1_ref, x_ref, dxo_ref, g3_ref, dx_ref, dg3_ref):
        dx, dg3 = _rms_bwd(x_ref[...], g3_ref[...], _dot(da_ref[...], w1_ref[...]))
        dx_ref[...] = dxo_ref[...] + dx
        _accumulate(dg3_ref, dg3, pl.program_id(0) == 0)

    return _pcall(
        body, name="ffn_bwd_input", grid=(s // t,),
        in_specs=[_rows(t, D_FF), _whole(w1.shape), _rows(t, D_MODEL), _rows(t, D_MODEL), _whole(g3.shape)],
        out_specs=[_rows(t, D_MODEL), _acc((1, D_MODEL))],
        out_shape=[_sds((s, D_MODEL), F32), _sds((1, D_MODEL), F32)],
        vmem=VMEM_LIMIT)(da, w1, x_mid, dxo, g3)


def _merge_bwd_out(tt, dxmid, g2, wout, merged, t):
    s = tt.shape[0]
    n = s // t

    def body(t_ref, dx_ref, g2_ref, wout_ref, mrg_ref, dm_ref, dwout_ref, dg2_ref, acc):
        first, last = pl.program_id(0) == 0, pl.program_id(0) == n - 1
        dt, dg2 = _rms_bwd(t_ref[...], g2_ref[...], dx_ref[...])
        dtb = dt.astype(BF16)
        dm_ref[...] = _dot(dtb, wout_ref[...], NT)
        _accumulate_out(acc, dwout_ref, _dot(mrg_ref[...], dtb, TN), first, last)
        _accumulate(dg2_ref, dg2, first)

    return _pcall(
        body, name="merge_bwd_out", grid=(n,),
        in_specs=[_rows(t, D_MODEL), _rows(t, D_MODEL), _whole(g2.shape), _whole(wout.shape), _rows(t, D_MODEL)],
        out_specs=[_rows(t, D_MODEL), _acc((D_MODEL, D_MODEL)), _acc((1, D_MODEL))],
        out_shape=[_sds((s, D_MODEL), F32), _sds((D_MODEL, D_MODEL), BF16), _sds((1, D_MODEL), F32)],
        scratch=[pltpu.VMEM((D_MODEL, D_MODEL), F32)],
        vmem=VMEM_LIMIT)(tt, dxmid, g2, wout, merged)


def _merge_bwd_branch(k, hb, br, dm, wg, bg, wbr, t):
    s = hb.shape[0]
    n = s // t

    def body(hb_ref, br_ref, dm_ref, wg_ref, bg_ref, wbr_ref, dbr_ref, dpre_ref, dwg_ref, dbg_ref, dwbr_ref,
             acc_g, acc_br):
        first, last = pl.program_id(0) == 0, pl.program_id(0) == n - 1
        hbv, brv, dmv, wbrv = hb_ref[...], br_ref[...], dm_ref[...], wbr_ref[0]
        gate = _sigmoid(_dot(hbv, wg_ref[...], NT) + bg_ref[...])
        dy = (dmv * gate).astype(BF16)
        dpre = dmv * _dot(brv, wbrv, NT) * gate * (1.0 - gate)
        dpb = dpre.astype(BF16)
        dpre_ref[...] = dpb
        dbr_ref[...] = _dot(dy, wbrv)
        _accumulate_out(acc_br, dwbr_ref, _dot(dy, brv, TN), first, last)
        _accumulate_out(acc_g, dwg_ref, _dot(dpb, hbv, TN), first, last)
        _accumulate(dbg_ref, jnp.sum(dpre, axis=0, keepdims=True), first)

    return _pcall(
        body, name=f"merge_bwd_branch{k}", grid=(n,),
        in_specs=[_rows(t, D_MODEL), _rows(t, BR_W), _rows(t, D_MODEL),
                  pl.BlockSpec((D_MODEL, D_MODEL), lambda i: (k, 0)), pl.BlockSpec((1, D_MODEL), lambda i: (0, k)),
                  pl.BlockSpec((1, D_MODEL, BR_W), lambda i: (k, 0, 0))],
        out_specs=[_rows(t, BR_W), _rows(t, D_MODEL), _acc((D_MODEL, D_MODEL)), _acc((1, D_MODEL)), _acc((D_MODEL, BR_W))],
        out_shape=[_sds((s, BR_W), F32), _sds((s, D_MODEL), BF16), _sds((D_MODEL, D_MODEL), BF16),
                   _sds((1, D_MODEL), F32), _sds((D_MODEL, BR_W), BF16)],
        scratch=[pltpu.VMEM((D_MODEL, D_MODEL), F32), pltpu.VMEM((D_MODEL, BR_W), F32)],
        vmem=VMEM_LIMIT)(hb, br, dm, wg, bg, wbr)


def _mixers_bwd(pm, dbsg, dbcv, dbpl, ln_g, ln_b, wm, wmt, sgb, conv_w, wp, pscale, t):
    s = pm.shape[0]
    n = s // t
    nb = t // HALO

    def body(pm_ref, before_ref, after_ref, dsg_ref, dcv_ref, dcva_ref, dpl_ref, dpla_ref, lng_ref, lnb_ref,
             wm_ref, wmt_ref, sgb_ref, cw_ref, wp_ref, ps_ref,
             dpm_ref, dlng_ref, dlnb_ref, dwm_ref, dsgb_ref, dcw_ref, dwp_ref, dps_ref):
        i = pl.program_id(0)
        first = i == 0
        before = jnp.where(i > 0, before_ref[...], 0.0)
        after = jnp.where(i < n - 1, after_ref[...], 0.0)

        u_raw, v_raw = pm_ref[:, 0:256], pm_ref[:, 256:512]
        lng, lnb = lng_ref[...], lnb_ref[...]
        u, gelu_u_vjp = jax.vjp(_gelu, u_raw)
        vn, norm_vjp = jax.vjp(lambda v_, g_, b_: _layernorm(_gelu(v_), g_, b_), v_raw, lng, lnb)
        vnb = vn.astype(BF16)
        dsg = dsg_ref[...]
        grp = _lane_group((SG_CHUNK, BR_W))
        tri = (lax.broadcasted_iota(jnp.int32, (SG_CHUNK, SG_CHUNK), 1)
               <= lax.broadcasted_iota(jnp.int32, (SG_CHUNK, SG_CHUNK), 0))
        du_parts, dvn_parts = [], []
        dsgb = jnp.zeros((SG_CHUNK, BR_W), F32)
        dwm = [jnp.zeros((SG_CHUNK, SG_CHUNK), F32) for _ in range(SG_GROUPS)]
        for c in range(t // SG_CHUNK):
            rows = slice(c * SG_CHUNK, (c + 1) * SG_CHUNK)
            mix = _sg_mix(wm_ref, vnb[rows], sgb_ref[...])
            du_parts.append(dsg[rows] * mix)
            ds = dsg[rows] * u[rows]
            dsgb = dsgb + ds
            dsb = [jnp.where(grp == g, ds, 0.0).astype(BF16) for g in range(SG_GROUPS)]
            for g in range(SG_GROUPS):
                dwm[g] = dwm[g] + _dot(dsb[g], vnb[rows], NT)
            dvn_parts.append(_group_select([_dot(wmt_ref[g], dsb[g]) for g in range(SG_GROUPS)]))
        (du_raw,) = gelu_u_vjp(jnp.concatenate(du_parts, axis=0))
        dv_raw, dlng, dlnb = norm_vjp(jnp.concatenate(dvn_parts, axis=0))
        dpm_ref[:, 0:256] = du_raw.astype(BF16)
        dpm_ref[:, 256:512] = dv_raw.astype(BF16)
        _accumulate(dlng_ref, dlng, first)
        _accumulate(dlnb_ref, dlnb, first)
        _accumulate(dsgb_ref, dsgb, first)
        for g in range(SG_GROUPS):
            _accumulate(dwm_ref.at[g], jnp.where(tri, dwm[g], 0.0), first)

        xin, bg, cg = pm_ref[:, 512:768], pm_ref[:, 768:1024], pm_ref[:, 1024:1280]
        z = cg * xin
        zext = jnp.concatenate([before[:, 1024:1280] * before[:, 512:768], z], axis=0)
        z1, z2 = _shift_down(zext, 1, t), _shift_down(zext, 2, t)
        w0, w1, w2 = cw_ref[0:1, :], cw_ref[1:2, :], cw_ref[2:3, :]
        dcv = dcv_ref[...]
        y = w0 * z2 + w1 * z1 + w2 * z
        dy = dcv * bg
        dyext = jnp.concatenate([dy, jnp.where(i < n - 1, dcva_ref[...], 0.0) * after[:, 768:1024]], axis=0)
        dz = w2 * dy + w1 * _shift_up(dyext, 1, t) + w0 * _shift_up(dyext, 2, t)
        dpm_ref[:, 512:768] = (dz * cg).astype(BF16)
        dpm_ref[:, 768:1024] = (dcv * y).astype(BF16)
        dpm_ref[:, 1024:1280] = (dz * xin).astype(BF16)
        dcw = jnp.concatenate([jnp.sum(dy * z2, axis=0, keepdims=True), jnp.sum(dy * z1, axis=0, keepdims=True),
                               jnp.sum(dy * z, axis=0, keepdims=True)], axis=0)
        _accumulate(dcw_ref, jnp.concatenate([dcw, jnp.zeros((8 - CONV_K, BR_W), F32)], axis=0), first)

        p = pm_ref[:, 1280:1536]
        tok = i * t + lax.broadcasted_iota(jnp.int32, (t, 1), 0)
        sums = _pool_windows(jnp.concatenate([before[:, 1280:1536], p], axis=0))
        pooled = _group_select([sw[HALO:HALO + t] / cnt - p for sw, cnt in zip(sums, _pool_counts(tok))]).astype(BF16)
        mixed = _dot(pooled, wp_ref[...])
        dpl = dpl_ref[...]
        ps = ps_ref[...]
        dmix = (dpl * ps).astype(BF16)
        _accumulate(dps_ref, jnp.sum(dpl * mixed, axis=0, keepdims=True), first)
        _accumulate(dwp_ref, _dot(pooled, dmix, TN), first)
        dmix_after = (jnp.where(i < n - 1, dpla_ref[...], 0.0) * ps).astype(BF16)
        dpo = _dot(dmix, wp_ref[...], NT)
        dpo_ext = jnp.concatenate([dpo, _dot(dmix_after, wp_ref[...], NT)], axis=0)
        tok_ext = i * t + lax.broadcasted_iota(jnp.int32, (t + HALO, 1), 0)
        dp_groups = []
        for g, (win, cnt) in enumerate(zip(POOL_WINDOWS, _pool_counts(tok_ext))):
            e = dpo_ext / cnt
            acc = e
            span = 1
            while span < win:
                acc = acc + pltpu.roll(acc, t + HALO - span, 0)
                span *= 2
            dp_groups.append(acc[0:t] - dpo)
        dpm_ref[:, 1280:1536] = _group_select(dp_groups).astype(BF16)

    row_blk = lambda w: pl.BlockSpec((t, w), lambda i: (i, 0))
    after_blk = lambda w: pl.BlockSpec((HALO, w), lambda i: (jnp.minimum((i + 1) * nb, n * nb - 1), 0))
    return _pcall(
        body, name="mixers_bwd", grid=(n,),
        in_specs=[row_blk(PM_W), _halo_before(t), after_blk(PM_W), row_blk(BR_W), row_blk(BR_W), after_blk(BR_W),
                  row_blk(BR_W), after_blk(BR_W), _whole(ln_g.shape), _whole(ln_b.shape), _whole(wm.shape),
                  _whole(wmt.shape), _whole(sgb.shape), _whole(conv_w.shape), _whole(wp.shape), _whole(pscale.shape)],
        out_specs=[row_blk(PM_W), _acc((1, BR_W)), _acc((1, BR_W)), _acc((SG_GROUPS, SG_CHUNK, SG_CHUNK)),
                   _acc((SG_CHUNK, BR_W)), _acc((8, BR_W)), _acc((BR_W, BR_W)), _acc((1, BR_W))],
        out_shape=[_sds((s, PM_W), BF16), _sds((1, BR_W), F32), _sds((1, BR_W), F32),
                   _sds((SG_GROUPS, SG_CHUNK, SG_CHUNK), F32), _sds((SG_CHUNK, BR_W), F32), _sds((8, BR_W), F32),
                   _sds((BR_W, BR_W), F32), _sds((1, BR_W), F32)],
        vmem=VMEM_LIMIT)(pm, pm, pm, dbsg, dbcv, dbcv, dbpl, dbpl, ln_g, ln_b, wm, wmt, sgb, conv_w, wp, pscale)


def _head_delta(o, do):
    rows = o.shape[0]
    prod = o.astype(F32) * do
    lane = lax.broadcasted_iota(jnp.int32, (rows, LANES), 1)
    out = []
    for h in range(N_HEADS):
        pair = prod[:, (h // 2) * LANES:(h // 2 + 1) * LANES]
        mine = (lane < D_V) if h % 2 == 0 else (lane >= D_V)
        out.append(jnp.broadcast_to(jnp.sum(jnp.where(mine, pair, 0.0), axis=1, keepdims=True), (rows, LANES)))
    return out


def _head_do(do, h):
    lane = lax.broadcasted_iota(jnp.int32, (do.shape[0], LANES), 1)
    mine = (lane < D_V) if h % 2 == 0 else (lane >= D_V)
    pair = do[:, (h // 2) * LANES:(h // 2 + 1) * LANES]
    return jnp.where(mine, pair, 0.0).astype(BF16)


def _flash_bwd(q, k, v, o, do, lse, t, riding):
    s = v.shape[0]
    n = s // t
    qi_tab, kj_tab = _causal_steps(n, key_major=True)
    n_steps = int(qi_tab.shape[0])
    whole = slice(0, t)

    def body(qi_ref, kj_ref, q_ref, k_ref, v_ref, o_ref, do_ref, lse_ref, x_ref, dq_ref, dk_ref, dv_ref, land_ref,
             dv_scr, send_sems, recv_sems, local_sem):
        step_id = pl.program_id(0)
        qi, kj = qi_ref[step_id], kj_ref[step_id]
        at_end = _ride_along(riding, step_id, n_steps, (x_ref, land_ref, send_sems, recv_sems, local_sem))

        @pl.when(step_id == 0)
        def _():
            dq_ref[...] = jnp.zeros(dq_ref.shape, F32)

        @pl.when(qi == kj)
        def _():
            dk_ref[...] = jnp.zeros(dk_ref.shape, F32)
            dv_scr[...] = jnp.zeros(dv_scr.shape, F32)

        def step(masked, qr, kr):
            dov = do_ref[qr, :]
            delta = _head_delta(o_ref[qr, :], dov)
            reps = (kr.stop - kr.start) // LANES
            q_rows = pl.ds(pl.multiple_of(qi * t + qr.start, t // 2), qr.stop - qr.start)
            for h in range(N_HEADS):
                qh, kh = q_ref[h, qr, :], k_ref[h, kr, :]
                p = jnp.exp(_causal_scores(qh, kh, masked) - jnp.tile(lse_ref[h, qr, :], (1, reps)))
                pair = slice((h // 2) * LANES, (h // 2 + 1) * LANES)
                dv_scr[h, kr, :] += _dot(p.astype(BF16), dov[:, pair].astype(BF16), TN)
                dp = _dot(_head_do(dov, h), v_ref[kr, pair], NT)
                ds = (p * (dp - jnp.tile(delta[h], (1, reps)))).astype(BF16)
                dk_ref[h, kr, :] += _dot(ds, qh, TN)
                dq_ref[h, q_rows, :] += _dot(ds, kh)

        @pl.when(qi > kj)
        def _():
            step(False, whole, whole)

        @pl.when(qi == kj)
        def _():
            for masked, qr, kr in _diagonal_parts(t):
                step(masked, qr, kr)

        @pl.when(qi == n - 1)
        def _():
            lane = lax.broadcasted_iota(jnp.int32, (t, LANES), 1)
            for pr in range(N_HEADS // 2):
                dv_ref[:, pr * LANES:(pr + 1) * LANES] = jnp.where(lane < D_V, dv_scr[2 * pr], dv_scr[2 * pr + 1]).astype(BF16)

        at_end()

    q_blk = pl.BlockSpec((N_HEADS, t, HEAD_PAD), lambda i, qi, kj: (0, qi[i], 0))
    k_blk = pl.BlockSpec((N_HEADS, t, HEAD_PAD), lambda i, qi, kj: (0, kj[i], 0))
    v_blk = pl.BlockSpec((t, BR_W), lambda i, qi, kj: (kj[i], 0))
    o_blk = pl.BlockSpec((t, BR_W), lambda i, qi, kj: (qi[i], 0))
    dq_sds = _sds((N_HEADS, s, HEAD_PAD), F32)
    return _pcall(
        body, name="flash_bwd_exchanging", grid=(n_steps,), prefetch=2,
        in_specs=[q_blk, k_blk, v_blk, o_blk, o_blk, q_blk, HBM_REF],
        out_specs=[_whole(dq_sds.shape), k_blk, v_blk, HBM_REF],
        out_shape=[dq_sds, dq_sds, _sds((s, BR_W), BF16), riding[2]],
        scratch=[pltpu.VMEM((N_HEADS, t, LANES), F32)] + list(EXCHANGE_SEMS),
        vmem=VMEM_LIMIT)(qi_tab, kj_tab, q, k, v, o, do, lse, riding[1])


def _attn_prep_bwd(pa, dq, dk, dv, cosf, sins, qn, kvn, wq, wqs, wk, wv, t):
    s = pa.shape[0]

    def body(pa_ref, dq_ref, dk_ref, dv_ref, cos_ref, sin_ref, qn_ref, kvn_ref, wq_ref, wqs_ref, wk_ref, wv_ref,
             dpa_ref, dwq_ref, dwqs_ref, dwk_ref, dwv_ref, dqn_ref, dkvn_ref):
        first = pl.program_id(0) == 0
        cosv, sinv = cos_ref[...], sin_ref[...]
        cq, ckv = pa_ref[:, 0:Q_RANK], pa_ref[:, Q_RANK:Q_RANK + KV_RANK]
        cqn = _rms(cq, qn_ref[...]).astype(BF16)
        ckvn = _rms(ckv, kvn_ref[...]).astype(BF16)
        dvv = dv_ref[...]
        _accumulate(dwv_ref, _dot(dvv, ckvn, TN), first)
        dk_rope = dk_ref[0]
        for h in range(1, N_HEADS):
            dk_rope = dk_rope + dk_ref[h]
        cos_s, sin_s = cosv * SM_SCALE, sinv * SM_SCALE
        dqa = jnp.concatenate([(dq_ref[h] * cos_s).astype(BF16) for h in range(N_HEADS)], axis=1)
        dqs = jnp.concatenate([(dq_ref[h] * sin_s).astype(BF16) for h in range(N_HEADS)], axis=1)
        dkb = jnp.concatenate([dk_ref[h].astype(BF16) for h in range(N_HEADS)], axis=1)
        _accumulate(dwq_ref, _dot(dqa, cqn, TN), first)
        _accumulate(dwqs_ref, _dot(dqs, cqn, TN), first)
        _accumulate(dwk_ref, _dot(dkb, ckvn, TN), first)
        dcqn = _dot(dqa, wq_ref[...]) + _dot(dqs, wqs_ref[...])
        dckvn = _dot(dvv, wv_ref[...]) + _dot(dkb, wk_ref[...])
        dcq, dqn = _rms_bwd(cq, qn_ref[...], dcqn)
        dckv, dkvn = _rms_bwd(ckv, kvn_ref[...], dckvn)
        _accumulate(dqn_ref, dqn, first)
        _accumulate(dkvn_ref, dkvn, first)
        dpa_ref[:, 0:Q_RANK] = dcq.astype(BF16)
        dpa_ref[:, Q_RANK:Q_RANK + KV_RANK] = dckv.astype(BF16)
        dpa_ref[:, 384:512] = (dk_rope * cosv).astype(BF16)
        dpa_ref[:, 512:640] = (dk_rope * sinv).astype(BF16)

    head_blk = pl.BlockSpec((N_HEADS, t, HEAD_PAD), lambda i: (0, i, 0))
    return _pcall(
        body, name="attn_prep_bwd", grid=(s // t,),
        in_specs=[_rows(t, PA_W), head_blk, head_blk, _rows(t, BR_W), _rows(t, LANES), _rows(t, LANES),
                  _whole(qn.shape), _whole(kvn.shape), _whole(wq.shape), _whole(wqs.shape), _whole(wk.shape), _whole(wv.shape)],
        out_specs=[_rows(t, PA_W), _acc(wq.shape), _acc(wqs.shape), _acc(wk.shape), _acc(wv.shape),
                   _acc(qn.shape), _acc(kvn.shape)],
        out_shape=[_sds((s, PA_W), BF16), _sds(wq.shape, F32), _sds(wqs.shape, F32), _sds(wk.shape, F32),
                   _sds(wv.shape, F32), _sds(qn.shape, F32), _sds(kvn.shape, F32)],
        vmem=VMEM_LIMIT)(pa, dq, dk, dv, cosf, sins, qn, kvn, wq, wqs, wk, wv)


def _inproj_bwd(x, g1, dxmid, dpa, dpm, dpres, w_in, wg, t):
    s = x.shape[0]
    n = s // t

    def body(x_ref, g1_ref, dxm_ref, dpa_ref, dpm_ref, d0_ref, d1_ref, d2_ref, d3_ref, w_ref, wg_ref,
             dx_ref, dg1_ref, dwa_ref, dwm_ref, acc_a, acc_m):
        first, last = pl.program_id(0) == 0, pl.program_id(0) == n - 1
        xv, g1v = x_ref[...], g1_ref[...]
        hb = _rms(xv, g1v).astype(BF16)
        dpav, dpmv = dpa_ref[...], dpm_ref[...]
        dh = _dot(dpav, w_ref[:PA_W, :]) + _dot(dpmv, w_ref[PA_W:, :])
        for k, d_ref in enumerate((d0_ref, d1_ref, d2_ref, d3_ref)):
            dh = dh + _dot(d_ref[...], wg_ref[k * D_MODEL:(k + 1) * D_MODEL, :])
        dx, dg1 = _rms_bwd(xv, g1v, dh)
        dx_ref[...] = dxm_ref[...] + dx
        _accumulate(dg1_ref, dg1, first)
        _accumulate_out(acc_a, dwa_ref, _dot(dpav, hb, TN), first, last)
        _accumulate_out(acc_m, dwm_ref, _dot(dpmv, hb, TN), first, last)

    return _pcall(
        body, name="inproj_bwd", grid=(n,),
        in_specs=[_rows(t, D_MODEL), _whole(g1.shape), _rows(t, D_MODEL), _rows(t, PA_W), _rows(t, PM_W)] +
                 [_rows(t, D_MODEL)] * 4 + [_whole(w_in.shape), _whole(wg.shape)],
        out_specs=[_rows(t, D_MODEL), _acc((1, D_MODEL)), _acc((PA_W, D_MODEL)), _acc((PM_W, D_MODEL))],
        out_shape=[_sds((s, D_MODEL), F32), _sds((1, D_MODEL), F32), _sds((PA_W, D_MODEL), BF16), _sds((PM_W, D_MODEL), BF16)],
        scratch=[pltpu.VMEM((PA_W, D_MODEL), F32), pltpu.VMEM((PM_W, D_MODEL), F32)],
        vmem=VMEM_LIMIT)(x, g1, dxmid, dpa, dpm, *dpres, w_in, wg)


def _my_place():
    return lax.axis_index("x"), lax.axis_index("y"), lax.axis_index("c")


def _flip(place, k):
    x, y, c = place
    return (1 - x if k & 4 else x, 1 - y if k & 2 else y, 1 - c if k & 1 else c)


def _rank(place):
    x, y, c = place
    return 4 * x + 2 * y + c


EXCHANGE_SEMS = (pltpu.SemaphoreType.DMA((7,)), pltpu.SemaphoreType.DMA((7,)), pltpu.SemaphoreType.DMA)
HBM_REF = pl.BlockSpec(memory_space=pl.ANY)


def _gather_plan(x_ref, out_ref, send_sems, recv_sems, local_sem):
    me = _my_place()
    sibling = _flip(me, 1)
    chips = (4, 2, 6)

    def copy(k, src_place, to, src=None):
        slot = out_ref.at[_rank(src_place)]
        return pltpu.make_async_remote_copy(
            src_ref=slot if src is None else src, dst_ref=slot, send_sem=send_sems.at[k], recv_sem=recv_sems.at[k],
            device_id=to, device_id_type=MESH_ID)

    mine = pltpu.make_async_copy(x_ref, out_ref.at[_rank(me)], local_sem)
    first = [copy(0, me, sibling, src=x_ref)] + [copy(1 + j, me, _flip(me, kc), src=x_ref) for j, kc in enumerate(chips)]
    passed = [copy(4 + j, _flip(me, kc), sibling) for j, kc in enumerate(chips)]

    def start():
        mine.start()
        for cp in first:
            cp.start()

    def forward():
        for j, kc in enumerate(chips):
            copy(1 + j, _flip(me, kc), me).wait_recv()
            passed[j].start()

    def finish():
        copy(0, sibling, me).wait_recv()
        for j, kc in enumerate(chips):
            copy(4 + j, _flip(sibling, kc), me).wait_recv()
        for cp in first + passed:
            cp.wait_send()
        mine.wait()

    return start, forward, finish


def _exchange_plan(x_ref, out_ref, send_sems, recv_sems, local_sem):
    me = _my_place()
    my_rank = _rank(me)
    mine = pltpu.make_async_copy(x_ref.at[my_rank], out_ref.at[my_rank], local_sem)
    sends = [pltpu.make_async_remote_copy(
        src_ref=x_ref.at[_rank(_flip(me, k))], dst_ref=out_ref.at[my_rank], send_sem=send_sems.at[k - 1],
        recv_sem=recv_sems.at[k - 1], device_id=_flip(me, k), device_id_type=MESH_ID) for k in range(1, N_DEV)]

    def start():
        mine.start()
        for cp in sends:
            cp.start()

    def forward():
        pass

    def finish():
        for k in range(1, N_DEV):
            slot = out_ref.at[_rank(_flip(me, k))]
            pltpu.make_async_remote_copy(
                src_ref=slot, dst_ref=slot, send_sem=send_sems.at[k - 1], recv_sem=recv_sems.at[k - 1],
                device_id=_flip(me, k), device_id_type=MESH_ID).wait_recv()
        for cp in sends:
            cp.wait_send()
        mine.wait()

    return start, forward, finish


def _exchange_alone(plan, src, out_sds, name):
    def body(x_ref, out_ref, send_sems, recv_sems, local_sem):
        start, forward, finish = plan(x_ref, out_ref, send_sems, recv_sems, local_sem)
        start()
        forward()
        finish()

    return pl.pallas_call(body, name=name, out_shape=out_sds, in_specs=[HBM_REF], out_specs=HBM_REF,
                          scratch_shapes=list(EXCHANGE_SEMS))(src)


def _exchange_two_alone(first, second, name):
    def body(x1_ref, x2_ref, out1_ref, out2_ref, *sems):
        start1, forward1, finish1 = first[0](x1_ref, out1_ref, *sems[:3])
        start2, forward2, finish2 = second[0](x2_ref, out2_ref, *sems[3:])
        start1()
        start2()
        forward1()
        forward2()
        finish1()
        finish2()

    return pl.pallas_call(body, name=name, out_shape=[first[2], second[2]], in_specs=[HBM_REF, HBM_REF],
                          out_specs=[HBM_REF, HBM_REF], scratch_shapes=list(EXCHANGE_SEMS) * 2)(first[1], second[1])


def _all_gather(shard, name):
    return _exchange_alone(_gather_plan, shard, _sds((N_DEV,) + shard.shape, shard.dtype), name)


def _riding(plan, src, out_sds):
    return (plan, src, out_sds)


def _ride_along(riding, step_id, n_steps, refs):
    start, forward, finish = riding[0](*refs)

    @pl.when(step_id == 0)
    def _():
        start()

    @pl.when(step_id == (7 * n_steps) // 8)
    def _():
        forward()

    def at_end():
        @pl.when(step_id == n_steps - 1)
        def _():
            finish()

    return at_end


def _sum_slots(parts, name):
    _, r, c = parts.shape
    t = math.gcd(r, ROW_PAD)

    def body(p_ref, o_ref):
        acc = p_ref[0].astype(F32)
        for d in range(1, N_DEV):
            acc = acc + p_ref[d].astype(F32)
        o_ref[...] = acc

    return _pcall(
        body, name=name, grid=(r // t,),
        in_specs=[pl.BlockSpec((N_DEV, t, c), lambda i: (0, i, 0))], out_specs=_rows(t, c),
        out_shape=_sds((r, c), F32))(parts)


def _adamw(w, g, m, v, name):
    depth, r, c = w.shape
    t = r if r <= ROW_PAD else math.gcd(r, ROW_PAD)
    blk = pl.BlockSpec((1, t, c), lambda l, i: (l, i, 0))

    def body(w_ref, g_ref, m_ref, v_ref, d_ref, nm_ref, nv_ref):
        gv = g_ref[...]
        nm = ADAM_B1 * m_ref[...] + (1.0 - ADAM_B1) * gv
        nv = ADAM_B2 * v_ref[...] + (1.0 - ADAM_B2) * jnp.square(gv)
        m_hat = nm / (1.0 - ADAM_B1 ** ADAM_STEP)
        v_hat = nv / (1.0 - ADAM_B2 ** ADAM_STEP)
        d_ref[...] = -ADAM_LR * (m_hat / (jnp.sqrt(v_hat) + ADAM_EPS) + ADAM_WD * w_ref[...])
        nm_ref[...] = nm
        nv_ref[...] = nv

    return _pcall(
        body, name=name, grid=(depth, r // t),
        in_specs=[blk] * 4, out_specs=[blk] * 3, out_shape=[_sds(w.shape, F32)] * 3)(w, g, m, v)


BEFORE_ATTN = ("w_in", "w_uq", "w_ukv")
AFTER_ATTN = ("w_br_mla", "w_br_sg", "w_br_conv", "w_br_pool", "w_out", "w_ff1", "w_ff2", "conv_w")
SHARDED = BEFORE_ATTN + AFTER_ATTN
ROW_SHARDED = ("w_out", "w_ff2")
REPLICATED = ("norm_mix_pre", "gate_b", "q_norm", "kv_norm", "sg_ln_g", "sg_ln_b", "sg_w", "sg_b", "pool_w",
              "pool_scale", "norm_mix_post", "norm_ffn_pre", "norm_ffn_post")
ROW_PAD = 256
GRAD_ROW_PAD = 64
PART_ROWS = 16


def _pack_rows(arrays, dtype, multiple, lead=()):
    rows, offsets, at = [], [], 0
    zero_pad = ((0, 0),) * len(lead)
    for a in arrays:
        if a.ndim == len(lead) + 2 and a.shape[-1] == D_MODEL:
            n = a.shape[-2]
            n_pad = -(-n // PART_ROWS) * PART_ROWS
            rows.append(jnp.pad(a.astype(dtype), zero_pad + ((0, n_pad - n), (0, 0))))
        else:
            flat = a.reshape(lead + (-1,)).astype(dtype)
            n = -(-flat.shape[-1] // D_MODEL)
            n_pad = -(-n // PART_ROWS) * PART_ROWS
            flat = jnp.pad(flat, zero_pad + ((0, n_pad * D_MODEL - flat.shape[-1]),))
            rows.append(flat.reshape(lead + (n_pad, D_MODEL)))
        offsets.append((at, n))
        at += n_pad
    pad = -at % multiple
    if pad:
        rows.append(jnp.zeros(lead + (pad, D_MODEL), dtype))
    return jnp.concatenate(rows, axis=len(lead)), offsets


def _unpack_rows(buf, offsets, shapes):
    lead = buf.shape[:-2]
    out = []
    for (at, n), shape in zip(offsets, shapes):
        if tuple(shape) == (n, D_MODEL):
            out.append(buf[..., at:at + n, :])
            continue
        size = math.prod(shape)
        flat = buf[..., at:at + n, :].reshape(lead + (n * D_MODEL,))[..., :size]
        out.append(flat.reshape(lead + tuple(shape)))
    return out


def _pack_shards(p, depth, names):
    arrays = []
    for n in names:
        a = p[n][0] if n in ROW_SHARDED else jnp.swapaxes(p[n][0], -1, -2)
        arrays.append(lax.bitcast_convert_type(a, BF16) if n == "conv_w" else a.astype(BF16))
    packed, offsets = _pack_rows(arrays, BF16, GRAD_ROW_PAD, lead=(depth,))
    return packed, offsets, [a.shape[1:] for a in arrays]


def _unpack_layer_weights(names, gathered, offsets, shapes):
    full = {}
    for n, gth in zip(names, _unpack_rows(gathered, offsets, shapes)):
        if n == "conv_w":
            gth = lax.bitcast_convert_type(gth, F32)
        full[n] = gth.reshape((-1,) + gth.shape[2:])
        if n == "conv_w":
            full[n] = full[n].T
    return full


def _owner_major(grad_sharded_dim_first):
    a = grad_sharded_dim_first
    return a.reshape((N_DEV, a.shape[0] // N_DEV) + a.shape[1:])


def _owner_block_shape(name, shard_shape):
    k, n = shard_shape[-2:]
    return (k, n) if name in ROW_SHARDED else (n, k)


def _natural_shard(name, blocks):
    return blocks if name in ROW_SHARDED else jnp.swapaxes(blocks, -1, -2)


def _swap_halves(a, axis=-1):
    lo, hi = jnp.split(a, 2, axis=axis)
    return jnp.concatenate([hi, lo], axis=axis)


def _pad_rows(a, top, total):
    return jnp.pad(a, ((0, 0),) * (a.ndim - 2) + ((top, total - top - a.shape[-2]), (0, 0)))


def _weights_before_attn(full, small, l):
    w_in = full["w_in"]
    k_r = w_in[384:416]
    w_in_k = jnp.concatenate(
        [w_in[0:384], _pad_rows(k_r, D_NOPE, HEAD_PAD), _pad_rows(_swap_halves(k_r, axis=0), D_NOPE, HEAD_PAD), w_in[416:1952]], axis=0)
    w_uq = full["w_uq"].reshape(N_HEADS, D_NOPE + D_ROPE, Q_RANK)
    wq = _pad_rows(w_uq, 0, HEAD_PAD).reshape(N_HEADS * HEAD_PAD, Q_RANK)
    wqs = _pad_rows(_swap_halves(w_uq[:, D_NOPE:], axis=1), D_NOPE, HEAD_PAD).reshape(N_HEADS * HEAD_PAD, Q_RANK)
    w_ukv = full["w_ukv"].reshape(N_HEADS, D_NOPE + D_V, KV_RANK)
    wk = _pad_rows(w_ukv[:, :D_NOPE], 0, HEAD_PAD).reshape(N_HEADS * HEAD_PAD, KV_RANK)
    wv = w_ukv[:, D_NOPE:].reshape(N_HEADS * D_V, KV_RANK)
    tri = jnp.tril(jnp.ones((SG_CHUNK, SG_CHUNK), bool))
    wm = jnp.where(tri, small["sg_w"][l], 0.0)
    pool_w = small["pool_w"][l]
    wp = jnp.zeros((BR_W, BR_W), F32)
    for g in range(len(POOL_WINDOWS)):
        wp = wp.at[g * POOL_GROUP:(g + 1) * POOL_GROUP, g * POOL_GROUP:(g + 1) * POOL_GROUP].set(pool_w[g])
    vec = lambda name: small[name][l][None, :]
    return dict(
        w_in=w_in_k, wg=w_in[1952:], bg=vec("gate_b"), g1=vec("norm_mix_pre"), qn=vec("q_norm"), kvn=vec("kv_norm"),
        wq=wq, wqs=wqs, wk=wk, wv=wv,
        ln_g=vec("sg_ln_g"), ln_b=vec("sg_ln_b"), wm=wm.astype(BF16), wmt=jnp.swapaxes(wm, 1, 2).astype(BF16),
        sgb=jnp.repeat(small["sg_b"][l].T, POOL_GROUP, axis=1), wp=wp.astype(BF16), pscale=vec("pool_scale"),
        g2=vec("norm_mix_post"), g3=vec("norm_ffn_pre"), g4=vec("norm_ffn_post"))


def _weights_after_attn(full):
    return dict(
        conv_w=full["conv_w"], wbr=jnp.stack([full["w_br_mla"], full["w_br_sg"], full["w_br_conv"], full["w_br_pool"]]),
        wout=full["w_out"], w1=full["w_ff1"], w2=full["w_ff2"])


def _grads_after_attn(g):
    own = _owner_major
    return dict(
        w_br_mla=own(g["dwbr"][0]), w_br_sg=own(g["dwbr"][1]), w_br_conv=own(g["dwbr"][2]), w_br_pool=own(g["dwbr"][3]),
        w_out=own(g["dwout"]), w_ff1=own(g["dw1"]), w_ff2=own(g["dw2"]), conv_w=own(g["dcw"][:CONV_K].T))


def _grads_before_attn(g):
    dwa = g["dwa"]
    rope_rows = slice(D_NOPE, D_NOPE + D_ROPE)
    d_kr = dwa[384:512][rope_rows].astype(F32) + _swap_halves(dwa[512:640][rope_rows], axis=0).astype(F32)
    w_in = jnp.concatenate([dwa[0:384], d_kr.astype(BF16), g["dwm_in"]] + list(g["dwg"]), axis=0)
    dwq, dwqs = (g[n].reshape(N_HEADS, HEAD_PAD, Q_RANK) for n in ("dwq", "dwqs"))
    d_rope = dwq[:, rope_rows] + _swap_halves(dwqs[:, rope_rows], axis=1)
    w_uq = jnp.concatenate([dwq[:, :D_NOPE], d_rope], axis=1).reshape(N_HEADS * (D_NOPE + D_ROPE), Q_RANK)
    dwk = g["dwk"].reshape(N_HEADS, HEAD_PAD, KV_RANK)
    w_ukv = jnp.concatenate([dwk[:, :D_NOPE], g["dwv"].reshape(N_HEADS, D_V, KV_RANK)], axis=1)
    w_ukv = w_ukv.reshape(N_HEADS * (D_NOPE + D_V), KV_RANK)
    pool_w = jnp.stack([g["dwp"][i * POOL_GROUP:(i + 1) * POOL_GROUP, i * POOL_GROUP:(i + 1) * POOL_GROUP]
                        for i in range(len(POOL_WINDOWS))])
    sg_b = g["dsgb"].reshape(SG_CHUNK, SG_GROUPS, POOL_GROUP).sum(axis=-1).T
    own = _owner_major
    return dict(
        w_in=own(w_in), w_uq=own(w_uq), w_ukv=own(w_ukv),
        norm_mix_pre=g["dg1"][0], gate_b=jnp.concatenate([b[0] for b in g["dbg"]]), q_norm=g["dqn"][0], kv_norm=g["dkvn"][0],
        sg_ln_g=g["dlng"][0], sg_ln_b=g["dlnb"][0], sg_w=g["dwm"], sg_b=sg_b, pool_w=pool_w, pool_scale=g["dps"][0],
        norm_mix_post=g["dg2"][0], norm_ffn_pre=g["dg3"][0], norm_ffn_post=g["dg4"][0])


def _rope_tables(positions):
    inv_freq = ROPE_BASE ** (-jnp.arange(0, D_ROPE, 2, dtype=F32) / D_ROPE)
    ang = positions.astype(F32)[:, None] * inv_freq
    cos, sin = jnp.cos(ang), jnp.sin(ang)
    s = positions.shape[0]
    cosf = jnp.concatenate([jnp.ones((s, D_NOPE), F32), cos, cos, jnp.zeros((s, HEAD_PAD - D_NOPE - D_ROPE), F32)], axis=1)
    sins = jnp.concatenate([jnp.zeros((s, D_NOPE), F32), -sin, sin, jnp.zeros((s, HEAD_PAD - D_NOPE - D_ROPE), F32)], axis=1)
    return cosf, sins


def _layer_fwd(x, w, weights_after_attn, cosf, sins, tiles, riding):
    t, ta = tiles["tok"], tiles["attn"]
    hb, pa, pm = _inproj_fwd(x, w["g1"], w["w_in"], tiles["wgrad"])
    q, k, v = _attn_prep_fwd(pa, cosf, sins, w["qn"], w["kvn"], w["wq"], w["wqs"], w["wk"], w["wv"], tiles["wgrad"])
    o, lse, landed = _flash_fwd(q, k, v, ta, riding)
    w = {**w, **weights_after_attn(landed)}
    bsg, bcv, bpl = _mixers_fwd(pm, w["ln_g"], w["ln_b"], w["wm"], w["sgb"], w["conv_w"], w["wp"], w["pscale"], tiles["wgrad"])
    x_mid, merged, tt = _merge_fwd(x, hb, (o, bsg, bcv, bpl), w["wg"], w["bg"], w["wbr"], w["wout"], w["g2"], t)
    x_out, f = _ffn_fwd(x_mid, w["g3"], w["w1"], w["w2"], w["g4"], t)
    saved = dict(x=x, hb=hb, pa=pa, pm=pm, q=q, k=k, v=v, o=o, lse=lse, bsg=bsg, bcv=bcv, bpl=bpl, x_mid=x_mid,
                 merged=merged, tt=tt, f=f)
    return x_out, saved, w, landed


def _layer_bwd(dxo, a, w, cosf, sins, tiles, pending):
    t, ta, tb, tw = tiles["tok"], tiles["attn"], tiles["inproj_bwd"], tiles["wgrad"]
    g = {}
    h2, df, g["dg4"] = _ffn_bwd_norms(a["x_mid"], a["f"], dxo, w["g3"], w["g4"], tw)
    da, g["dw1"], g["dw2"] = _ffn_bwd_weights(h2, df, w["w1"], w["w2"], tw)
    dxmid, g["dg3"] = _ffn_bwd_input(da, w["w1"], a["x_mid"], dxo, w["g3"], t)
    dm, g["dwout"], g["dg2"] = _merge_bwd_out(a["tt"], dxmid, w["g2"], w["wout"], a["merged"], tw)
    dbrs, dpres, g["dwg"], g["dbg"], g["dwbr"] = [], [], [], [], []
    for k, br in enumerate((a["o"], a["bsg"], a["bcv"], a["bpl"])):
        dbr, dpre, dwg, dbg, dwbr = _merge_bwd_branch(k, a["hb"], br, dm, w["wg"], w["bg"], w["wbr"], tw)
        dbrs.append(dbr); dpres.append(dpre); g["dwg"].append(dwg); g["dbg"].append(dbg); g["dwbr"].append(dwbr)
    dpm, g["dlng"], g["dlnb"], g["dwm"], g["dsgb"], g["dcw"], g["dwp"], g["dps"] = _mixers_bwd(
        a["pm"], dbrs[1], dbrs[2], dbrs[3], w["ln_g"], w["ln_b"], w["wm"], w["wmt"], w["sgb"], w["conv_w"], w["wp"], w["pscale"], tw)
    after = _grads_after_attn(g)
    send, after_offsets = _pack_rows([after[n] for n in AFTER_ATTN], BF16, GRAD_ROW_PAD, lead=(N_DEV,))
    after_rows = send.shape[1]
    if pending is not None:
        send = jnp.concatenate([send, pending], axis=1)
    riding = _riding(_exchange_plan, send, _sds(send.shape, send.dtype))
    dq, dk, dv, landed = _flash_bwd(a["q"], a["k"], a["v"], a["o"], dbrs[0], a["lse"], ta, riding)
    dpa, g["dwq"], g["dwqs"], g["dwk"], g["dwv"], g["dqn"], g["dkvn"] = _attn_prep_bwd(
        a["pa"], dq, dk, dv, cosf, sins, w["qn"], w["kvn"], w["wq"], w["wqs"], w["wk"], w["wv"], tw)
    dx, g["dg1"], g["dwa"], g["dwm_in"] = _inproj_bwd(a["x"], w["g1"], dxmid, dpa, dpm, dpres, w["w_in"], w["wg"], tb)
    return dx, _grads_before_attn(g), landed, (after_rows, after_offsets)


def _step(p, x, positions, loss_target):
    s = x.shape[0]
    depth = p["w_in"][0].shape[0]
    tiles = dict(tok=min(TOK_TILE, s), attn=min(ATTN_TILE, s), inproj_bwd=min(INPROJ_BWD_TILE, s),
                 wgrad=min(WGRAD_TILE, s))

    small = {n: p[n][0] for n in REPLICATED}
    cosf, sins = _rope_tables(positions)

    pack_b, off_b, shapes_b = _pack_shards(p, depth, BEFORE_ATTN)
    pack_a, off_a, shapes_a = _pack_shards(p, depth, AFTER_ATTN)
    rows_a = pack_a.shape[1]
    before = _all_gather(pack_b[0], "weight_all_gather")
    weights, acts = [], []
    h = x
    for l in range(depth):
        w = _weights_before_attn(_unpack_layer_weights(BEFORE_ATTN, before, off_b, shapes_b), small, l)
        src = pack_a[l] if l + 1 == depth else jnp.concatenate([pack_a[l], pack_b[l + 1]], axis=0)
        riding = _riding(_gather_plan, src, _sds((N_DEV,) + src.shape, src.dtype))

        def after(landed):
            return _weights_after_attn(_unpack_layer_weights(AFTER_ATTN, landed[:, :rows_a], off_a, shapes_a))

        h, saved, w, landed = _layer_fwd(h, w, after, cosf, sins, tiles, riding)
        weights.append(w)
        acts.append(saved)
        before = landed[:, rows_a:] if l + 1 < depth else None
    dh, loss_blk = _loss_head(h, loss_target, tiles["tok"])
    loss = lax.psum(loss_blk[0, 0], ("x", "y", "c"))

    grads_b = [None] * depth
    sum_a, sum_b = [None] * depth, [None] * depth
    pending = None
    for l in reversed(range(depth)):
        dh, grads_b[l], landed, (rows_ga, off_ga) = _layer_bwd(dh, acts[l], weights[l], cosf, sins, tiles, pending)
        summed = _sum_slots(landed, "grad_shard_sum")
        sum_a[l] = summed[:rows_ga]
        if pending is not None:
            sum_b[l + 1] = summed[rows_ga:]
        pending, off_gb = _pack_rows([grads_b[l][n] for n in BEFORE_ATTN], BF16, GRAD_ROW_PAD, lead=(N_DEV,))
    small_grads = [jnp.stack([grads_b[l][n] for l in range(depth)]) for n in REPLICATED]
    small_rows, small_offsets = _pack_rows(small_grads, F32, ROW_PAD)
    landed, small_landed = _exchange_two_alone(
        _riding(_exchange_plan, pending, _sds(pending.shape, pending.dtype)),
        _riding(_gather_plan, small_rows, _sds((N_DEV,) + small_rows.shape, small_rows.dtype)), "last_grad_exchanges")
    sum_b[0] = _sum_slots(landed, "grad_shard_sum")
    g_small_rows = _sum_slots(small_landed, "grad_replicated_sum")
    g_shard = {}
    for names, sums, offsets in ((AFTER_ATTN, sum_a, off_ga), (BEFORE_ATTN, sum_b, off_gb)):
        block_shapes = [_owner_block_shape(n, p[n][0].shape) for n in names]
        for n, b in zip(names, _unpack_rows(jnp.stack(sums), offsets, block_shapes)):
            g_shard[n] = _natural_shard(n, b)

    grad, delta, new_m, new_v = {}, {}, {}, {}
    for n in SHARDED:
        w, m, v = p[n]
        grad[n] = g_shard[n]
        delta[n], new_m[n], new_v[n] = _adamw(w, grad[n], m, v, f"adamw_{n}")
    packs = [_pack_rows([p[n][i] for n in REPLICATED], F32, ROW_PAD)[0][None] for i in range(3)]
    d, nm, nv = (a[0] for a in _adamw(packs[0], g_small_rows[None], packs[1], packs[2], "adamw_replicated"))
    shapes = [p[n][0].shape for n in REPLICATED]
    for n, gg, dd, mm, vv in zip(REPLICATED, _unpack_rows(g_small_rows, small_offsets, shapes), _unpack_rows(d, small_offsets, shapes),
                                 _unpack_rows(nm, small_offsets, shapes), _unpack_rows(nv, small_offsets, shapes)):
        grad[n], delta[n], new_m[n], new_v[n] = gg, dd, mm, vv
    return loss, dh, grad, delta, new_m, new_v


WEIGHT_ORDER = ("norm_mix_pre", "w_in", "gate_b", "q_norm", "w_uq", "kv_norm", "w_ukv", "w_br_mla", "sg_ln_g", "sg_ln_b",
                "sg_w", "sg_b", "w_br_sg", "conv_w", "w_br_conv", "pool_w", "pool_scale", "w_br_pool", "w_out",
                "norm_mix_post", "norm_ffn_pre", "w_ff1", "w_ff2", "norm_ffn_post")


def kernel(x, positions, norm_mix_pre, w_in, gate_b, q_norm, w_uq, kv_norm, w_ukv, w_br_mla, sg_ln_g, sg_ln_b, sg_w, sg_b, w_br_sg, conv_w, w_br_conv, pool_w, pool_scale, w_br_pool, w_out, norm_mix_post, norm_ffn_pre, w_ff1, w_ff2, norm_ffn_post, loss_target, m_norm_mix_pre, m_w_in, m_gate_b, m_q_norm, m_w_uq, m_kv_norm, m_w_ukv, m_w_br_mla, m_sg_ln_g, m_sg_ln_b, m_sg_w, m_sg_b, m_w_br_sg, m_conv_w, m_w_br_conv, m_pool_w, m_pool_scale, m_w_br_pool, m_w_out, m_norm_mix_post, m_norm_ffn_pre, m_w_ff1, m_w_ff2, m_norm_ffn_post, v_norm_mix_pre, v_w_in, v_gate_b, v_q_norm, v_w_uq, v_kv_norm, v_w_ukv, v_w_br_mla, v_sg_ln_g, v_sg_ln_b, v_sg_w, v_sg_b, v_w_br_sg, v_conv_w, v_w_br_conv, v_pool_w, v_pool_scale, v_w_br_pool, v_w_out, v_norm_mix_post, v_norm_ffn_pre, v_w_ff1, v_w_ff2, v_norm_ffn_post):
    ws = (norm_mix_pre, w_in, gate_b, q_norm, w_uq, kv_norm, w_ukv, w_br_mla, sg_ln_g, sg_ln_b, sg_w, sg_b, w_br_sg, conv_w,
          w_br_conv, pool_w, pool_scale, w_br_pool, w_out, norm_mix_post, norm_ffn_pre, w_ff1, w_ff2, norm_ffn_post)
    ms = (m_norm_mix_pre, m_w_in, m_gate_b, m_q_norm, m_w_uq, m_kv_norm, m_w_ukv, m_w_br_mla, m_sg_ln_g, m_sg_ln_b, m_sg_w,
          m_sg_b, m_w_br_sg, m_conv_w, m_w_br_conv, m_pool_w, m_pool_scale, m_w_br_pool, m_w_out, m_norm_mix_post,
          m_norm_ffn_pre, m_w_ff1, m_w_ff2, m_norm_ffn_post)
    vs = (v_norm_mix_pre, v_w_in, v_gate_b, v_q_norm, v_w_uq, v_kv_norm, v_w_ukv, v_w_br_mla, v_sg_ln_g, v_sg_ln_b, v_sg_w,
          v_sg_b, v_w_br_sg, v_conv_w, v_w_br_conv, v_pool_w, v_pool_scale, v_w_br_pool, v_w_out, v_norm_mix_post,
          v_norm_ffn_pre, v_w_ff1, v_w_ff2, v_norm_ffn_post)
    p = {n: (w, m, v) for n, w, m, v in zip(WEIGHT_ORDER, ws, ms, vs)}
    loss, grad_x, grad, delta, new_m, new_v = _step(p, x[0], positions[0], loss_target[0])
    return (loss, grad_x[None], *[grad[n] for n in WEIGHT_ORDER], *[delta[n] for n in WEIGHT_ORDER],
            *[new_m[n] for n in WEIGHT_ORDER], *[new_v[n] for n in WEIGHT_ORDER])
```
